```python
import math
import jax, jax.numpy as jnp
from jax import lax
import numpy as np

D_MODEL = 1024
BATCH = 8
SEQ = 4096
DEPTH = 1

D_FF = 2816
EPS = 1e-6
N_HEADS_MLA = 16
QK_NOPE = 64
QK_ROPE = 32
QK_HEAD = QK_NOPE + QK_ROPE
V_HEAD = 64
Q_LORA = 384
KV_LORA = 256
ROPE_BASE = 10000.0
Q_BLOCK = 128
D_INNER = 2 * D_MODEL
SSM_HEAD_DIM = 64
SSM_HEADS = D_INNER // SSM_HEAD_DIM
SSM_GROUPS = 4
D_STATE = 128
CONV_WIDTH = 5
CHUNK = 128
XBC_DIM = D_INNER + 2 * SSM_GROUPS * D_STATE
IN_SPLITS = (Q_LORA, KV_LORA, QK_ROPE, D_INNER, XBC_DIM, SSM_HEADS, SSM_HEADS, D_MODEL, D_MODEL)
IN_DIM = sum(IN_SPLITS)

kernel_name = "hybrid_mla_bissd_macaron_block"


def rmsnorm(x, g):
    xf = x.astype(jnp.float32)
    y = xf * lax.rsqrt(jnp.mean(xf * xf, axis=-1, keepdims=True) + EPS)
    return (y * g).astype(x.dtype)


def swiglu(h, w_gate, w_up, w_down):
    return (jax.nn.silu(h @ w_gate) * (h @ w_up)) @ w_down


def rope_tables(positions):
    inv_freq = 1.0 / (ROPE_BASE ** (jnp.arange(0, QK_ROPE, 2, dtype=jnp.float32) / QK_ROPE))
    ang = positions.astype(jnp.float32)[..., None] * inv_freq
    return jnp.cos(ang)[:, :, None, :], jnp.sin(ang)[:, :, None, :]


def apply_rope(t, cos, sin):
    tf = t.astype(jnp.float32)
    t1, t2 = tf[..., : QK_ROPE // 2], tf[..., QK_ROPE // 2:]
    return jnp.concatenate([t1 * cos - t2 * sin, t2 * cos + t1 * sin], axis=-1).astype(t.dtype)


def mla(c_q, c_kv, k_pe, positions, q_a_norm, w_q_b, kv_a_norm, w_kv_b, q_head_norm, k_head_norm):
    b, s, _ = c_q.shape
    q = (rmsnorm(c_q, q_a_norm) @ w_q_b).reshape(b, s, N_HEADS_MLA, QK_HEAD)
    kv = (rmsnorm(c_kv, kv_a_norm) @ w_kv_b).reshape(b, s, N_HEADS_MLA, QK_NOPE + V_HEAD)
    k_nope, v = kv[..., :QK_NOPE], kv[..., QK_NOPE:]
    k_pe_h = jnp.broadcast_to(k_pe[:, :, None, :], (b, s, N_HEADS_MLA, QK_ROPE))
    k = jnp.concatenate([k_nope, k_pe_h], axis=-1)
    q = rmsnorm(q, q_head_norm)
    k = rmsnorm(k, k_head_norm)
    cos, sin = rope_tables(positions)
    q = jnp.concatenate([q[..., :QK_NOPE], apply_rope(q[..., QK_NOPE:], cos, sin)], axis=-1)
    k = jnp.concatenate([k[..., :QK_NOPE], apply_rope(k[..., QK_NOPE:], cos, sin)], axis=-1)
    scale = 1.0 / math.sqrt(QK_HEAD)
    n_blk = s // Q_BLOCK
    qb = q.reshape(b, n_blk, Q_BLOCK, N_HEADS_MLA, QK_HEAD).transpose(1, 0, 2, 3, 4)

    def attend(q_blk):
        sc = jnp.einsum('bqhd,bkhd->bhqk', q_blk, k).astype(jnp.float32) * scale
        p = jax.nn.softmax(sc, axis=-1).astype(v.dtype)
        return jnp.einsum('bhqk,bkhd->bqhd', p, v)

    o = lax.map(attend, qb)
    return o.transpose(1, 0, 2, 3, 4).reshape(b, s, N_HEADS_MLA * V_HEAD)


def ssd(x, dt, a, bm, cm):
    b, l, h, p = x.shape
    g, n = bm.shape[2], bm.shape[3]
    hg = h // g
    nc = l // CHUNK
    f32 = jnp.float32
    xdt = (x.astype(f32) * dt[..., None]).reshape(b, nc, CHUNK, g, hg, p)
    da = jnp.moveaxis((dt * a).reshape(b, nc, CHUNK, g, hg), 2, -1)
    a_cs = jnp.cumsum(da, axis=-1)
    bc = bm.astype(f32).reshape(b, nc, CHUNK, g, n)
    cc = cm.astype(f32).reshape(b, nc, CHUNK, g, n)
    tril = jnp.tril(jnp.ones((CHUNK, CHUNK), dtype=bool))
    seg = a_cs[..., :, None] - a_cs[..., None, :]
    decay = jnp.exp(jnp.where(tril, seg, -jnp.inf))
    cb = jnp.einsum('bclgn,bcsgn->bcgls', cc, bc)
    y_diag = jnp.einsum('bcgls,bcghls,bcsghp->bclghp', cb, decay, xdt)
    decay_states = jnp.exp(a_cs[..., -1:] - a_cs)
    states = jnp.einsum('bclgn,bcghl,bclghp->bcghpn', bc, decay_states, xdt)
    chunk_decay = jnp.exp(a_cs[..., -1])

    def step(carry, inp):
        st, dec = inp
        return carry * dec[..., None, None] + st, carry

    init = jnp.zeros((b, g, hg, p, n), f32)
    _, prev = lax.scan(step, init, (jnp.moveaxis(states, 1, 0), jnp.moveaxis(chunk_decay, 1, 0)))
    prev = jnp.moveaxis(prev, 0, 1)
    y_off = jnp.einsum('bclgn,bcghpn,bcghl->bclghp', cc, prev, jnp.exp(a_cs))
    return (y_diag + y_off).reshape(b, l, h, p)


def bi_mamba2(xbc, z, dt_f_raw, dt_b_raw, conv_w, conv_b, a_log_fwd, a_log_bwd,
              dt_bias_fwd, dt_bias_bwd, d_skip, ssm_norm):
    b, s, _ = xbc.shape
    pad = CONV_WIDTH // 2
    xbc = lax.conv_general_dilated(xbc, conv_w, window_strides=(1,), padding=[(pad, pad)],
                                   dimension_numbers=('NWC', 'WIO', 'NWC'),
                                   feature_group_count=XBC_DIM)
    xbc = jax.nn.silu(xbc + conv_b)
    xs, bm, cm = jnp.split(xbc, [D_INNER, D_INNER + SSM_GROUPS * D_STATE], axis=-1)
    xs = xs.reshape(b, s, SSM_HEADS, SSM_HEAD_DIM)
    bm = bm.reshape(b, s, SSM_GROUPS, D_STATE)
    cm = cm.reshape(b, s, SSM_GROUPS, D_STATE)
    dt_f = jax.nn.softplus(dt_f_raw.astype(jnp.float32) + dt_bias_fwd)
    dt_b = jax.nn.softplus(dt_b_raw.astype(jnp.float32) + dt_bias_bwd)
    a_f = -jnp.exp(a_log_fwd.astype(jnp.float32))
    a_b = -jnp.exp(a_log_bwd.astype(jnp.float32))
    rev = lambda t: jnp.flip(t, axis=1)
    y_f = ssd(xs, dt_f, a_f, bm, cm)
    y_b = rev(ssd(rev(xs), rev(dt_b), a_b, rev(bm), rev(cm)))
    y = y_f + y_b + d_skip.astype(jnp.float32)[:, None] * xs.astype(jnp.float32)
    y = y.reshape(b, s, D_INNER) * jax.nn.silu(z.astype(jnp.float32))
    yg = y.reshape(b, s, SSM_GROUPS, D_INNER // SSM_GROUPS)
    yg = yg * lax.rsqrt(jnp.mean(yg * yg, axis=-1, keepdims=True) + EPS)
    return (yg.reshape(b, s, D_INNER) * ssm_norm).astype(xs.dtype)


def _fwd_setup_inputs(seed: int = 0) -> dict:
    key = jax.random.key(seed)
    ks = iter(jax.random.split(key, 40))
    f32 = jnp.float32

    def nrm(shape, scale):
        return jax.random.normal(next(ks), (DEPTH,) + shape, f32) * scale

    def gain(n):
        return 1.0 + 0.01 * jax.random.normal(next(ks), (DEPTH, n), f32)

    x = jax.random.normal(next(ks), (BATCH, SEQ, D_MODEL), f32)
    positions = jnp.broadcast_to(jnp.arange(SEQ, dtype=jnp.int32)[None, :], (BATCH, SEQ))
    d = {}
    d['x'] = x
    d['positions'] = positions
    d['ffn1_norm'] = gain(D_MODEL)
    d['ffn1_w_gate'] = nrm((D_MODEL, D_FF), D_MODEL ** -0.5)
    d['ffn1_w_up'] = nrm((D_MODEL, D_FF), D_MODEL ** -0.5)
    d['ffn1_w_down'] = nrm((D_FF, D_MODEL), D_FF ** -0.5)
    d['mix_norm'] = gain(D_MODEL)
    d['w_in'] = nrm((D_MODEL, IN_DIM), D_MODEL ** -0.5)
    d['q_a_norm'] = gain(Q_LORA)
    d['w_q_b'] = nrm((Q_LORA, N_HEADS_MLA * QK_HEAD), Q_LORA ** -0.5)
    d['kv_a_norm'] = gain(KV_LORA)
    d['w_kv_b'] = nrm((KV_LORA, N_HEADS_MLA * (QK_NOPE + V_HEAD)), KV_LORA ** -0.5)
    d['q_head_norm'] = gain(QK_HEAD)
    d['k_head_norm'] = gain(QK_HEAD)
    d['conv_w'] = nrm((CONV_WIDTH, 1, XBC_DIM), CONV_WIDTH ** -0.5)
    d['conv_b'] = nrm((XBC_DIM,), 0.01)
    d['a_log_fwd'] = jnp.log(jax.random.uniform(next(ks), (DEPTH, SSM_HEADS), f32, 1.0, 16.0))
    d['a_log_bwd'] = jnp.log(jax.random.uniform(next(ks), (DEPTH, SSM_HEADS), f32, 1.0, 16.0))
    dt0_f = jnp.exp(jax.random.uniform(next(ks), (DEPTH, SSM_HEADS), f32, math.log(1e-3), math.log(1e-1)))
    dt0_b = jnp.exp(jax.random.uniform(next(ks), (DEPTH, SSM_HEADS), f32, math.log(1e-3), math.log(1e-1)))
    d['dt_bias_fwd'] = dt0_f + jnp.log(-jnp.expm1(-dt0_f))
    d['dt_bias_bwd'] = dt0_b + jnp.log(-jnp.expm1(-dt0_b))
    d['d_skip'] = gain(SSM_HEADS)
    d['ssm_norm'] = gain(D_INNER)
    d['w_attn_branch'] = nrm((N_HEADS_MLA * V_HEAD, D_MODEL), (N_HEADS_MLA * V_HEAD) ** -0.5)
    d['w_ssm_branch'] = nrm((D_INNER, D_MODEL), D_INNER ** -0.5)
    d['w_out'] = nrm((D_MODEL, D_MODEL), D_MODEL ** -0.5)
    d['ffn2_norm'] = gain(D_MODEL)
    d['ffn2_w_gate'] = nrm((D_MODEL, D_FF), D_MODEL ** -0.5)
    d['ffn2_w_up'] = nrm((D_MODEL, D_FF), D_MODEL ** -0.5)
    d['ffn2_w_down'] = nrm((D_FF, D_MODEL), D_FF ** -0.5)
    return d


def _fwd_reference(x, positions, ffn1_norm, ffn1_w_gate, ffn1_w_up, ffn1_w_down, mix_norm, w_in,
              q_a_norm, w_q_b, kv_a_norm, w_kv_b, q_head_norm, k_head_norm,
              conv_w, conv_b, a_log_fwd, a_log_bwd, dt_bias_fwd, dt_bias_bwd, d_skip, ssm_norm,
              w_attn_branch, w_ssm_branch, w_out,
              ffn2_norm, ffn2_w_gate, ffn2_w_up, ffn2_w_down):
    split_idx = list(np.cumsum(IN_SPLITS)[:-1])
    for l in range(DEPTH):
        x = x + 0.5 * swiglu(rmsnorm(x, ffn1_norm[l]), ffn1_w_gate[l], ffn1_w_up[l], ffn1_w_down[l])
        h = rmsnorm(x, mix_norm[l])
        u = h @ w_in[l]
        c_q, c_kv, k_pe, z, xbc, dt_f, dt_b, g_a, g_b = jnp.split(u, split_idx, axis=-1)
        a = mla(c_q, c_kv, k_pe, positions, q_a_norm[l], w_q_b[l], kv_a_norm[l], w_kv_b[l],
                q_head_norm[l], k_head_norm[l])
        m = bi_mamba2(xbc, z, dt_f, dt_b, conv_w[l], conv_b[l], a_log_fwd[l], a_log_bwd[l],
                      dt_bias_fwd[l], dt_bias_bwd[l], d_skip[l], ssm_norm[l])
        merged = jax.nn.sigmoid(g_a) * (a @ w_attn_branch[l]) + jax.nn.sigmoid(g_b) * (m @ w_ssm_branch[l])
        x = x + merged @ w_out[l]
        x = x + 0.5 * swiglu(rmsnorm(x, ffn2_norm[l]), ffn2_w_gate[l], ffn2_w_up[l], ffn2_w_down[l])
    return x


import jax as _jax
import jax.numpy as _jnp

TWIN_FORMAT = 'train_step'
FWD_PARAMS = ['x', 'positions', 'ffn1_norm', 'ffn1_w_gate', 'ffn1_w_up', 'ffn1_w_down', 'mix_norm', 'w_in', 'q_a_norm', 'w_q_b', 'kv_a_norm', 'w_kv_b', 'q_head_norm', 'k_head_norm', 'conv_w', 'conv_b', 'a_log_fwd', 'a_log_bwd', 'dt_bias_fwd', 'dt_bias_bwd', 'd_skip', 'ssm_norm', 'w_attn_branch', 'w_ssm_branch', 'w_out', 'ffn2_norm', 'ffn2_w_gate', 'ffn2_w_up', 'ffn2_w_down']
TWIN_WEIGHTS = ['ffn1_norm', 'ffn1_w_gate', 'ffn1_w_up', 'ffn1_w_down', 'mix_norm', 'w_in', 'q_a_norm', 'w_q_b', 'kv_a_norm', 'w_kv_b', 'q_head_norm', 'k_head_norm', 'conv_w', 'conv_b', 'a_log_fwd', 'a_log_bwd', 'dt_bias_fwd', 'dt_bias_bwd', 'd_skip', 'ssm_norm', 'w_attn_branch', 'w_ssm_branch', 'w_out', 'ffn2_norm', 'ffn2_w_gate', 'ffn2_w_up', 'ffn2_w_down']
TWIN_DIFF_INPUT = 'x'
TWIN_INPUTS = ['x', 'positions', 'ffn1_norm', 'ffn1_w_gate', 'ffn1_w_up', 'ffn1_w_down', 'mix_norm', 'w_in', 'q_a_norm', 'w_q_b', 'kv_a_norm', 'w_kv_b', 'q_head_norm', 'k_head_norm', 'conv_w', 'conv_b', 'a_log_fwd', 'a_log_bwd', 'dt_bias_fwd', 'dt_bias_bwd', 'd_skip', 'ssm_norm', 'w_attn_branch', 'w_ssm_branch', 'w_out', 'ffn2_norm', 'ffn2_w_gate', 'ffn2_w_up', 'ffn2_w_down', 'loss_target', 'm_ffn1_norm', 'm_ffn1_w_gate', 'm_ffn1_w_up', 'm_ffn1_w_down', 'm_mix_norm', 'm_w_in', 'm_q_a_norm', 'm_w_q_b', 'm_kv_a_norm', 'm_w_kv_b', 'm_q_head_norm', 'm_k_head_norm', 'm_conv_w', 'm_conv_b', 'm_a_log_fwd', 'm_a_log_bwd', 'm_dt_bias_fwd', 'm_dt_bias_bwd', 'm_d_skip', 'm_ssm_norm', 'm_w_attn_branch', 'm_w_ssm_branch', 'm_w_out', 'm_ffn2_norm', 'm_ffn2_w_gate', 'm_ffn2_w_up', 'm_ffn2_w_down', 'v_ffn1_norm', 'v_ffn1_w_gate', 'v_ffn1_w_up', 'v_ffn1_w_down', 'v_mix_norm', 'v_w_in', 'v_q_a_norm', 'v_w_q_b', 'v_kv_a_norm', 'v_w_kv_b', 'v_q_head_norm', 'v_k_head_norm', 'v_conv_w', 'v_conv_b', 'v_a_log_fwd', 'v_a_log_bwd', 'v_dt_bias_fwd', 'v_dt_bias_bwd', 'v_d_skip', 'v_ssm_norm', 'v_w_attn_branch', 'v_w_ssm_branch', 'v_w_out', 'v_ffn2_norm', 'v_ffn2_w_gate', 'v_ffn2_w_up', 'v_ffn2_w_down']
TWIN_OUTPUTS = ['loss', 'grad_x', 'grad_ffn1_norm', 'grad_ffn1_w_gate', 'grad_ffn1_w_up', 'grad_ffn1_w_down', 'grad_mix_norm', 'grad_w_in', 'grad_q_a_norm', 'grad_w_q_b', 'grad_kv_a_norm', 'grad_w_kv_b', 'grad_q_head_norm', 'grad_k_head_norm', 'grad_conv_w', 'grad_conv_b', 'grad_a_log_fwd', 'grad_a_log_bwd', 'grad_dt_bias_fwd', 'grad_dt_bias_bwd', 'grad_d_skip', 'grad_ssm_norm', 'grad_w_attn_branch', 'grad_w_ssm_branch', 'grad_w_out', 'grad_ffn2_norm', 'grad_ffn2_w_gate', 'grad_ffn2_w_up', 'grad_ffn2_w_down', 'delta_ffn1_norm', 'delta_ffn1_w_gate', 'delta_ffn1_w_up', 'delta_ffn1_w_down', 'delta_mix_norm', 'delta_w_in', 'delta_q_a_norm', 'delta_w_q_b', 'delta_kv_a_norm', 'delta_w_kv_b', 'delta_q_head_norm', 'delta_k_head_norm', 'delta_conv_w', 'delta_conv_b', 'delta_a_log_fwd', 'delta_a_log_bwd', 'delta_dt_bias_fwd', 'delta_dt_bias_bwd', 'delta_d_skip', 'delta_ssm_norm', 'delta_w_attn_branch', 'delta_w_ssm_branch', 'delta_w_out', 'delta_ffn2_norm', 'delta_ffn2_w_gate', 'delta_ffn2_w_up', 'delta_ffn2_w_down', 'new_m_ffn1_norm', 'new_m_ffn1_w_gate', 'new_m_ffn1_w_up', 'new_m_ffn1_w_down', 'new_m_mix_norm', 'new_m_w_in', 'new_m_q_a_norm', 'new_m_w_q_b', 'new_m_kv_a_norm', 'new_m_w_kv_b', 'new_m_q_head_norm', 'new_m_k_head_norm', 'new_m_conv_w', 'new_m_conv_b', 'new_m_a_log_fwd', 'new_m_a_log_bwd', 'new_m_dt_bias_fwd', 'new_m_dt_bias_bwd', 'new_m_d_skip', 'new_m_ssm_norm', 'new_m_w_attn_branch', 'new_m_w_ssm_branch', 'new_m_w_out', 'new_m_ffn2_norm', 'new_m_ffn2_w_gate', 'new_m_ffn2_w_up', 'new_m_ffn2_w_down', 'new_v_ffn1_norm', 'new_v_ffn1_w_gate', 'new_v_ffn1_w_up', 'new_v_ffn1_w_down', 'new_v_mix_norm', 'new_v_w_in', 'new_v_q_a_norm', 'new_v_w_q_b', 'new_v_kv_a_norm', 'new_v_w_kv_b', 'new_v_q_head_norm', 'new_v_k_head_norm', 'new_v_conv_w', 'new_v_conv_b', 'new_v_a_log_fwd', 'new_v_a_log_bwd', 'new_v_dt_bias_fwd', 'new_v_dt_bias_bwd', 'new_v_d_skip', 'new_v_ssm_norm', 'new_v_w_attn_branch', 'new_v_w_ssm_branch', 'new_v_w_out', 'new_v_ffn2_norm', 'new_v_ffn2_w_gate', 'new_v_ffn2_w_up', 'new_v_ffn2_w_down']
TWIN_LEAF_KINDS = {'loss': 'loss', 'grad_x': 'grad_x', 'grad_ffn1_norm': 'grad_w', 'grad_ffn1_w_gate': 'grad_w', 'grad_ffn1_w_up': 'grad_w', 'grad_ffn1_w_down': 'grad_w', 'grad_mix_norm': 'grad_w', 'grad_w_in': 'grad_w', 'grad_q_a_norm': 'grad_w', 'grad_w_q_b': 'grad_w', 'grad_kv_a_norm': 'grad_w', 'grad_w_kv_b': 'grad_w', 'grad_q_head_norm': 'grad_w', 'grad_k_head_norm': 'grad_w', 'grad_conv_w': 'grad_w', 'grad_conv_b': 'grad_w', 'grad_a_log_fwd': 'grad_w', 'grad_a_log_bwd': 'grad_w', 'grad_dt_bias_fwd': 'grad_w', 'grad_dt_bias_bwd': 'grad_w', 'grad_d_skip': 'grad_w', 'grad_ssm_norm': 'grad_w', 'grad_w_attn_branch': 'grad_w', 'grad_w_ssm_branch': 'grad_w', 'grad_w_out': 'grad_w', 'grad_ffn2_norm': 'grad_w', 'grad_ffn2_w_gate': 'grad_w', 'grad_ffn2_w_up': 'grad_w', 'grad_ffn2_w_down': 'grad_w', 'delta_ffn1_norm': 'delta_w', 'delta_ffn1_w_gate': 'delta_w', 'delta_ffn1_w_up': 'delta_w', 'delta_ffn1_w_down': 'delta_w', 'delta_mix_norm': 'delta_w', 'delta_w_in': 'delta_w', 'delta_q_a_norm': 'delta_w', 'delta_w_q_b': 'delta_w', 'delta_kv_a_norm': 'delta_w', 'delta_w_kv_b': 'delta_w', 'delta_q_head_norm': 'delta_w', 'delta_k_head_norm': 'delta_w', 'delta_conv_w': 'delta_w', 'delta_conv_b': 'delta_w', 'delta_a_log_fwd': 'delta_w', 'delta_a_log_bwd': 'delta_w', 'delta_dt_bias_fwd': 'delta_w', 'delta_dt_bias_bwd': 'delta_w', 'delta_d_skip': 'delta_w', 'delta_ssm_norm': 'delta_w', 'delta_w_attn_branch': 'delta_w', 'delta_w_ssm_branch': 'delta_w', 'delta_w_out': 'delta_w', 'delta_ffn2_norm': 'delta_w', 'delta_ffn2_w_gate': 'delta_w', 'delta_ffn2_w_up': 'delta_w', 'delta_ffn2_w_down': 'delta_w', 'new_m_ffn1_norm': 'new_m', 'new_m_ffn1_w_gate': 'new_m', 'new_m_ffn1_w_up': 'new_m', 'new_m_ffn1_w_down': 'new_m', 'new_m_mix_norm': 'new_m', 'new_m_w_in': 'new_m', 'new_m_q_a_norm': 'new_m', 'new_m_w_q_b': 'new_m', 'new_m_kv_a_norm': 'new_m', 'new_m_w_kv_b': 'new_m', 'new_m_q_head_norm': 'new_m', 'new_m_k_head_norm': 'new_m', 'new_m_conv_w': 'new_m', 'new_m_conv_b': 'new_m', 'new_m_a_log_fwd': 'new_m', 'new_m_a_log_bwd': 'new_m', 'new_m_dt_bias_fwd': 'new_m', 'new_m_dt_bias_bwd': 'new_m', 'new_m_d_skip': 'new_m', 'new_m_ssm_norm': 'new_m', 'new_m_w_attn_branch': 'new_m', 'new_m_w_ssm_branch': 'new_m', 'new_m_w_out': 'new_m', 'new_m_ffn2_norm': 'new_m', 'new_m_ffn2_w_gate': 'new_m', 'new_m_ffn2_w_up': 'new_m', 'new_m_ffn2_w_down': 'new_m', 'new_v_ffn1_norm': 'new_v', 'new_v_ffn1_w_gate': 'new_v', 'new_v_ffn1_w_up': 'new_v', 'new_v_ffn1_w_down': 'new_v', 'new_v_mix_norm': 'new_v', 'new_v_w_in': 'new_v', 'new_v_q_a_norm': 'new_v', 'new_v_w_q_b': 'new_v', 'new_v_kv_a_norm': 'new_v', 'new_v_w_kv_b': 'new_v', 'new_v_q_head_norm': 'new_v', 'new_v_k_head_norm': 'new_v', 'new_v_conv_w': 'new_v', 'new_v_conv_b': 'new_v', 'new_v_a_log_fwd': 'new_v', 'new_v_a_log_bwd': 'new_v', 'new_v_dt_bias_fwd': 'new_v', 'new_v_dt_bias_bwd': 'new_v', 'new_v_d_skip': 'new_v', 'new_v_ssm_norm': 'new_v', 'new_v_w_attn_branch': 'new_v', 'new_v_w_ssm_branch': 'new_v', 'new_v_w_out': 'new_v', 'new_v_ffn2_norm': 'new_v', 'new_v_ffn2_w_gate': 'new_v', 'new_v_ffn2_w_up': 'new_v', 'new_v_ffn2_w_down': 'new_v'}


def _forward(args):
    return _fwd_reference(*[args[k] for k in FWD_PARAMS])


def _output_shape():
    out = _jax.eval_shape(lambda: _forward(_fwd_setup_inputs(0)))
    return out.shape, out.dtype

N_MICROBATCH = 1
ADAM_LR = 0.001
ADAM_B1 = 0.9
ADAM_B2 = 0.999
ADAM_EPS = 1e-08
ADAM_WD = 0.01
ADAM_STEP = 10
PER_EXAMPLE_BATCH_AXIS = {'x': 0, 'positions': 0, 'loss_target': 0}
SHARED_INPUTS = []
_WEIGHT_DTYPES = {'ffn1_norm': _jnp.float32, 'ffn1_w_gate': _jnp.float32, 'ffn1_w_up': _jnp.float32, 'ffn1_w_down': _jnp.float32, 'mix_norm': _jnp.float32, 'w_in': _jnp.float32, 'q_a_norm': _jnp.float32, 'w_q_b': _jnp.float32, 'kv_a_norm': _jnp.float32, 'w_kv_b': _jnp.float32, 'q_head_norm': _jnp.float32, 'k_head_norm': _jnp.float32, 'conv_w': _jnp.float32, 'conv_b': _jnp.float32, 'a_log_fwd': _jnp.float32, 'a_log_bwd': _jnp.float32, 'dt_bias_fwd': _jnp.float32, 'dt_bias_bwd': _jnp.float32, 'd_skip': _jnp.float32, 'ssm_norm': _jnp.float32, 'w_attn_branch': _jnp.float32, 'w_ssm_branch': _jnp.float32, 'w_out': _jnp.float32, 'ffn2_norm': _jnp.float32, 'ffn2_w_gate': _jnp.float32, 'ffn2_w_up': _jnp.float32, 'ffn2_w_down': _jnp.float32}
MOMENT_SCALE = {'ffn1_norm': 6.119545e+00, 'ffn1_w_gate': 7.787244e-02, 'ffn1_w_up': 8.326708e-02, 'ffn1_w_down': 1.356889e-01, 'mix_norm': 6.878040e-01, 'w_in': 1.119427e-01, 'q_a_norm': 3.385912e-02, 'w_q_b': 1.636161e-02, 'kv_a_norm': 1.178420e-01, 'w_kv_b': 2.200305e-02, 'q_head_norm': 2.246336e-01, 'k_head_norm': 2.242429e-01, 'conv_w': 1.979911e-01, 'conv_b': 7.416872e-01, 'a_log_fwd': 4.832849e-01, 'a_log_bwd': 1.357209e+00, 'dt_bias_fwd': 2.387272e-01, 'dt_bias_bwd': 3.173999e-01, 'd_skip': 8.700189e-01, 'ssm_norm': 6.382465e+00, 'w_attn_branch': 2.527324e-02, 'w_ssm_branch': 5.755673e-01, 'w_out': 4.493686e-01, 'ffn2_norm': 6.174242e+00, 'ffn2_w_gate': 7.438750e-02, 'ffn2_w_up': 7.562590e-02, 'ffn2_w_down': 1.212996e-01}


def _to_microbatches(a, axis):
    t = _jnp.moveaxis(a, axis, 0)
    t = t.reshape((N_MICROBATCH, t.shape[0] // N_MICROBATCH) + t.shape[1:])
    return _jnp.moveaxis(t, 1, axis + 1)


def setup_inputs(seed: int = 0) -> dict:
    inp = _fwd_setup_inputs(seed)
    key = _jax.random.fold_in(_jax.random.key(seed), 7919)
    shape, _ = _output_shape()
    out = dict(inp)
    out["loss_target"] = _jax.random.normal(_jax.random.fold_in(key, 0), shape, _jnp.float32)
    for i, name in enumerate(TWIN_WEIGHTS):
        w = inp[name].astype(_jnp.float32)
        if MOMENT_SCALE is None:
            s = _jnp.sqrt(_jnp.mean(_jnp.square(w)) + 1e-30)
        else:
            s = MOMENT_SCALE[name]
        km, kv = _jax.random.split(_jax.random.fold_in(key, i + 1))
        out[name] = w
        out["m_" + name] = s * _jax.random.normal(km, w.shape, _jnp.float32)
        out["v_" + name] = (s * s) * _jax.random.uniform(kv, w.shape, _jnp.float32, 0.5, 1.5)
    if N_MICROBATCH > 1:
        for name, axis in PER_EXAMPLE_BATCH_AXIS.items():
            out[name] = _to_microbatches(out[name], axis)
    return {'x': out['x'], 'positions': out['positions'], 'ffn1_norm': out['ffn1_norm'], 'ffn1_w_gate': out['ffn1_w_gate'], 'ffn1_w_up': out['ffn1_w_up'], 'ffn1_w_down': out['ffn1_w_down'], 'mix_norm': out['mix_norm'], 'w_in': out['w_in'], 'q_a_norm': out['q_a_norm'], 'w_q_b': out['w_q_b'], 'kv_a_norm': out['kv_a_norm'], 'w_kv_b': out['w_kv_b'], 'q_head_norm': out['q_head_norm'], 'k_head_norm': out['k_head_norm'], 'conv_w': out['conv_w'], 'conv_b': out['conv_b'], 'a_log_fwd': out['a_log_fwd'], 'a_log_bwd': out['a_log_bwd'], 'dt_bias_fwd': out['dt_bias_fwd'], 'dt_bias_bwd': out['dt_bias_bwd'], 'd_skip': out['d_skip'], 'ssm_norm': out['ssm_norm'], 'w_attn_branch': out['w_attn_branch'], 'w_ssm_branch': out['w_ssm_branch'], 'w_out': out['w_out'], 'ffn2_norm': out['ffn2_norm'], 'ffn2_w_gate': out['ffn2_w_gate'], 'ffn2_w_up': out['ffn2_w_up'], 'ffn2_w_down': out['ffn2_w_down'], 'loss_target': out['loss_target'], 'm_ffn1_norm': out['m_ffn1_norm'], 'm_ffn1_w_gate': out['m_ffn1_w_gate'], 'm_ffn1_w_up': out['m_ffn1_w_up'], 'm_ffn1_w_down': out['m_ffn1_w_down'], 'm_mix_norm': out['m_mix_norm'], 'm_w_in': out['m_w_in'], 'm_q_a_norm': out['m_q_a_norm'], 'm_w_q_b': out['m_w_q_b'], 'm_kv_a_norm': out['m_kv_a_norm'], 'm_w_kv_b': out['m_w_kv_b'], 'm_q_head_norm': out['m_q_head_norm'], 'm_k_head_norm': out['m_k_head_norm'], 'm_conv_w': out['m_conv_w'], 'm_conv_b': out['m_conv_b'], 'm_a_log_fwd': out['m_a_log_fwd'], 'm_a_log_bwd': out['m_a_log_bwd'], 'm_dt_bias_fwd': out['m_dt_bias_fwd'], 'm_dt_bias_bwd': out['m_dt_bias_bwd'], 'm_d_skip': out['m_d_skip'], 'm_ssm_norm': out['m_ssm_norm'], 'm_w_attn_branch': out['m_w_attn_branch'], 'm_w_ssm_branch': out['m_w_ssm_branch'], 'm_w_out': out['m_w_out'], 'm_ffn2_norm': out['m_ffn2_norm'], 'm_ffn2_w_gate': out['m_ffn2_w_gate'], 'm_ffn2_w_up': out['m_ffn2_w_up'], 'm_ffn2_w_down': out['m_ffn2_w_down'], 'v_ffn1_norm': out['v_ffn1_norm'], 'v_ffn1_w_gate': out['v_ffn1_w_gate'], 'v_ffn1_w_up': out['v_ffn1_w_up'], 'v_ffn1_w_down': out['v_ffn1_w_down'], 'v_mix_norm': out['v_mix_norm'], 'v_w_in': out['v_w_in'], 'v_q_a_norm': out['v_q_a_norm'], 'v_w_q_b': out['v_w_q_b'], 'v_kv_a_norm': out['v_kv_a_norm'], 'v_w_kv_b': out['v_w_kv_b'], 'v_q_head_norm': out['v_q_head_norm'], 'v_k_head_norm': out['v_k_head_norm'], 'v_conv_w': out['v_conv_w'], 'v_conv_b': out['v_conv_b'], 'v_a_log_fwd': out['v_a_log_fwd'], 'v_a_log_bwd': out['v_a_log_bwd'], 'v_dt_bias_fwd': out['v_dt_bias_fwd'], 'v_dt_bias_bwd': out['v_dt_bias_bwd'], 'v_d_skip': out['v_d_skip'], 'v_ssm_norm': out['v_ssm_norm'], 'v_w_attn_branch': out['v_w_attn_branch'], 'v_w_ssm_branch': out['v_w_ssm_branch'], 'v_w_out': out['v_w_out'], 'v_ffn2_norm': out['v_ffn2_norm'], 'v_ffn2_w_gate': out['v_ffn2_w_gate'], 'v_ffn2_w_up': out['v_ffn2_w_up'], 'v_ffn2_w_down': out['v_ffn2_w_down']}


def _loss(weights, diff, rest, loss_target):
    with _jax.named_scope("forward"):
        args = {**rest, TWIN_DIFF_INPUT: diff, **{k: w.astype(_WEIGHT_DTYPES[k]) for k, w in weights.items()}}
        y = _forward(args)
    with _jax.named_scope("loss_head"):
        err = _jnp.square(y.astype(_jnp.float32) - loss_target)
        return 0.5 * _jnp.sum(_jnp.mean(err, axis=-1)) if err.ndim else 0.5 * err


def _adamw(w, g, m, v):
    m = ADAM_B1 * m + (1.0 - ADAM_B1) * g
    v = ADAM_B2 * v + (1.0 - ADAM_B2) * _jnp.square(g)
    m_hat = m / (1.0 - ADAM_B1 ** ADAM_STEP)
    v_hat = v / (1.0 - ADAM_B2 ** ADAM_STEP)
    delta = -ADAM_LR * (m_hat / (_jnp.sqrt(v_hat) + ADAM_EPS) + ADAM_WD * w)
    return delta, m, v


def reference(x, positions, ffn1_norm, ffn1_w_gate, ffn1_w_up, ffn1_w_down, mix_norm, w_in, q_a_norm, w_q_b, kv_a_norm, w_kv_b, q_head_norm, k_head_norm, conv_w, conv_b, a_log_fwd, a_log_bwd, dt_bias_fwd, dt_bias_bwd, d_skip, ssm_norm, w_attn_branch, w_ssm_branch, w_out, ffn2_norm, ffn2_w_gate, ffn2_w_up, ffn2_w_down, loss_target, m_ffn1_norm, m_ffn1_w_gate, m_ffn1_w_up, m_ffn1_w_down, m_mix_norm, m_w_in, m_q_a_norm, m_w_q_b, m_kv_a_norm, m_w_kv_b, m_q_head_norm, m_k_head_norm, m_conv_w, m_conv_b, m_a_log_fwd, m_a_log_bwd, m_dt_bias_fwd, m_dt_bias_bwd, m_d_skip, m_ssm_norm, m_w_attn_branch, m_w_ssm_branch, m_w_out, m_ffn2_norm, m_ffn2_w_gate, m_ffn2_w_up, m_ffn2_w_down, v_ffn1_norm, v_ffn1_w_gate, v_ffn1_w_up, v_ffn1_w_down, v_mix_norm, v_w_in, v_q_a_norm, v_w_q_b, v_kv_a_norm, v_w_kv_b, v_q_head_norm, v_k_head_norm, v_conv_w, v_conv_b, v_a_log_fwd, v_a_log_bwd, v_dt_bias_fwd, v_dt_bias_bwd, v_d_skip, v_ssm_norm, v_w_attn_branch, v_w_ssm_branch, v_w_out, v_ffn2_norm, v_ffn2_w_gate, v_ffn2_w_up, v_ffn2_w_down):
    given = dict(x=x, positions=positions, ffn1_norm=ffn1_norm, ffn1_w_gate=ffn1_w_gate, ffn1_w_up=ffn1_w_up, ffn1_w_down=ffn1_w_down, mix_norm=mix_norm, w_in=w_in, q_a_norm=q_a_norm, w_q_b=w_q_b, kv_a_norm=kv_a_norm, w_kv_b=w_kv_b, q_head_norm=q_head_norm, k_head_norm=k_head_norm, conv_w=conv_w, conv_b=conv_b, a_log_fwd=a_log_fwd, a_log_bwd=a_log_bwd, dt_bias_fwd=dt_bias_fwd, dt_bias_bwd=dt_bias_bwd, d_skip=d_skip, ssm_norm=ssm_norm, w_attn_branch=w_attn_branch, w_ssm_branch=w_ssm_branch, w_out=w_out, ffn2_norm=ffn2_norm, ffn2_w_gate=ffn2_w_gate, ffn2_w_up=ffn2_w_up, ffn2_w_down=ffn2_w_down, loss_target=loss_target, m_ffn1_norm=m_ffn1_norm, m_ffn1_w_gate=m_ffn1_w_gate, m_ffn1_w_up=m_ffn1_w_up, m_ffn1_w_down=m_ffn1_w_down, m_mix_norm=m_mix_norm, m_w_in=m_w_in, m_q_a_norm=m_q_a_norm, m_w_q_b=m_w_q_b, m_kv_a_norm=m_kv_a_norm, m_w_kv_b=m_w_kv_b, m_q_head_norm=m_q_head_norm, m_k_head_norm=m_k_head_norm, m_conv_w=m_conv_w, m_conv_b=m_conv_b, m_a_log_fwd=m_a_log_fwd, m_a_log_bwd=m_a_log_bwd, m_dt_bias_fwd=m_dt_bias_fwd, m_dt_bias_bwd=m_dt_bias_bwd, m_d_skip=m_d_skip, m_ssm_norm=m_ssm_norm, m_w_attn_branch=m_w_attn_branch, m_w_ssm_branch=m_w_ssm_branch, m_w_out=m_w_out, m_ffn2_norm=m_ffn2_norm, m_ffn2_w_gate=m_ffn2_w_gate, m_ffn2_w_up=m_ffn2_w_up, m_ffn2_w_down=m_ffn2_w_down, v_ffn1_norm=v_ffn1_norm, v_ffn1_w_gate=v_ffn1_w_gate, v_ffn1_w_up=v_ffn1_w_up, v_ffn1_w_down=v_ffn1_w_down, v_mix_norm=v_mix_norm, v_w_in=v_w_in, v_q_a_norm=v_q_a_norm, v_w_q_b=v_w_q_b, v_kv_a_norm=v_kv_a_norm, v_w_kv_b=v_w_kv_b, v_q_head_norm=v_q_head_norm, v_k_head_norm=v_k_head_norm, v_conv_w=v_conv_w, v_conv_b=v_conv_b, v_a_log_fwd=v_a_log_fwd, v_a_log_bwd=v_a_log_bwd, v_dt_bias_fwd=v_dt_bias_fwd, v_dt_bias_bwd=v_dt_bias_bwd, v_d_skip=v_d_skip, v_ssm_norm=v_ssm_norm, v_w_attn_branch=v_w_attn_branch, v_w_ssm_branch=v_w_ssm_branch, v_w_out=v_w_out, v_ffn2_norm=v_ffn2_norm, v_ffn2_w_gate=v_ffn2_w_gate, v_ffn2_w_up=v_ffn2_w_up, v_ffn2_w_down=v_ffn2_w_down)
    weights = {n: given[n] for n in TWIN_WEIGHTS}
    shared = {n: given[n] for n in SHARED_INPUTS}
    per_example = {n: given[n] for n in ['x', 'positions']}
    grad_fn = _jax.value_and_grad(_loss, argnums=(0, 1))

    def one_microbatch(ex, loss_target):
        ex = dict(ex)
        diff = ex.pop(TWIN_DIFF_INPUT)
        return grad_fn(weights, diff, {**shared, **ex}, loss_target)

    if N_MICROBATCH == 1:
        loss, (grad_w, grad_x) = one_microbatch(per_example, given["loss_target"])
    else:
        def body(carry, xs):
            loss_sum, grad_sum = carry
            l_k, (gw_k, gx_k) = one_microbatch(xs[0], xs[1])
            with _jax.named_scope("update"):
                return (loss_sum + l_k, _jax.tree.map(_jnp.add, grad_sum, gw_k)), gx_k

        init = (_jnp.zeros((), _jnp.float32), _jax.tree.map(_jnp.zeros_like, weights))
        (loss, grad_w), grad_x = _jax.lax.scan(body, init, (per_example, given["loss_target"]))
    with _jax.named_scope("update"):
        delta_w, new_m, new_v = {}, {}, {}
        for n in TWIN_WEIGHTS:
            delta_w[n], new_m[n], new_v[n] = _adamw(weights[n], grad_w[n], given["m_" + n], given["v_" + n])
    return (loss, grad_x, *[grad_w[n] for n in TWIN_WEIGHTS], *[delta_w[n] for n in TWIN_WEIGHTS],
            *[new_m[n] for n in TWIN_WEIGHTS], *[new_v[n] for n in TWIN_WEIGHTS])
```

```python
import math

import jax
import jax.numpy as jnp
from jax import lax
from jax.experimental import pallas as pl
from jax.experimental.pallas import tpu as pltpu

F32, BF16 = jnp.float32, jnp.bfloat16
HIGHEST = lax.Precision.HIGHEST

D_MODEL, D_FF = 1024, 2816
EPS = 1e-6
N_HEADS, QK_NOPE, QK_ROPE, QK_HEAD, V_HEAD = 16, 64, 32, 96, 64
Q_LORA, KV_LORA = 384, 256
ROPE_BASE = 10000.0
D_INNER, SSM_HEADS, SSM_GROUPS, D_STATE, CONV_WIDTH, CHUNK = 2048, 32, 4, 128, 5, 128
XBC_DIM = D_INNER + 2 * SSM_GROUPS * D_STATE
ADAM_LR, ADAM_B1, ADAM_B2, ADAM_EPS, ADAM_WD, ADAM_STEP = 0.001, 0.9, 0.999, 1e-08, 0.01, 10
N_DEV = 8

V7X_VMEM_BYTES = 64 * 1024 * 1024
VMEM_LIMIT = V7X_VMEM_BYTES - 8 * 1024 * 1024
LANES = 128
SLAB_COLS = 1024
SLAB_ROW_TILE = 256
SMALL_ROWS = 16

SHARDED = (
    ("ffn1_w_gate", (1024, 2816), 1), ("ffn1_w_up", (1024, 2816), 1), ("ffn1_w_down", (2816, 1024), 0),
    ("w_in", (1024, 7904), 1), ("w_q_b", (384, 1536), 1), ("w_kv_b", (256, 2048), 1), ("conv_w", (5, 3072), 1),
    ("w_attn_branch", (1024, 1024), 0), ("w_ssm_branch", (2048, 1024), 0), ("w_out", (1024, 1024), 0),
    ("ffn2_w_gate", (1024, 2816), 1), ("ffn2_w_up", (1024, 2816), 1), ("ffn2_w_down", (2816, 1024), 0),
)
SMALL = (
    ("ffn1_norm", 1024), ("mix_norm", 1024), ("q_a_norm", 384), ("kv_a_norm", 256), ("q_head_norm", 96),
    ("k_head_norm", 96), ("conv_b", 3072), ("a_log_fwd", 32), ("a_log_bwd", 32), ("dt_bias_fwd", 32),
    ("dt_bias_bwd", 32), ("d_skip", 32), ("ssm_norm", 2048), ("ffn2_norm", 1024),
)
WEIGHT_ORDER = (
    "ffn1_norm", "ffn1_w_gate", "ffn1_w_up", "ffn1_w_down", "mix_norm", "w_in", "q_a_norm", "w_q_b", "kv_a_norm",
    "w_kv_b", "q_head_norm", "k_head_norm", "conv_w", "conv_b", "a_log_fwd", "a_log_bwd", "dt_bias_fwd", "dt_bias_bwd",
    "d_skip", "ssm_norm", "w_attn_branch", "w_ssm_branch", "w_out", "ffn2_norm", "ffn2_w_gate", "ffn2_w_up",
    "ffn2_w_down",
)
SHARD_ELEMS = sum(r * c // N_DEV for _, (r, c), _ in SHARDED)
SLAB_ROWS = -(-SHARD_ELEMS // (SLAB_COLS * SLAB_ROW_TILE)) * SLAB_ROW_TILE


def _pallas(body, **kw):
    return pl.pallas_call(body, **kw)


def _params(sem):
    return pltpu.CompilerParams(dimension_semantics=sem, vmem_limit_bytes=VMEM_LIMIT)


def _pick(dim, pref):
    if dim <= pref:
        return dim
    c = (pref // LANES) * LANES
    while c >= LANES:
        if dim % c == 0:
            return c
        c -= LANES
    raise ValueError((dim, pref))


def _sigmoid(x):
    return 1.0 / (1.0 + jnp.exp(-x))


def _softplus(x):
    return jnp.maximum(x, 0.0) + jnp.log(1.0 + jnp.exp(-jnp.abs(x)))


def _dot(a, b):
    return jnp.dot(a, b, preferred_element_type=F32)


def _dot_nt(a, b):
    return lax.dot_general(a, b, (((1,), (1,)), ((), ())), preferred_element_type=F32)


def _dot_tn(a, b):
    return lax.dot_general(a, b, (((0,), (0,)), ((), ())), preferred_element_type=F32)


def _dot_h(a, b):
    return jnp.dot(a, b, preferred_element_type=F32, precision=HIGHEST)


def _dot_h_nt(a, b):
    return lax.dot_general(a, b, (((1,), (1,)), ((), ())), preferred_element_type=F32, precision=HIGHEST)


def _dot_h_tn(a, b):
    return lax.dot_general(a, b, (((0,), (0,)), ((), ())), preferred_element_type=F32, precision=HIGHEST)


def _mm(a, b, *, name, ta=False, tb=False, out_dtype=F32, alpha=1.0, res=None, tm=1024, tn=1408, tk=1024):
    (K, M) = a.shape if ta else a.shape[::-1]
    (N, Kb) = b.shape if tb else b.shape[::-1]
    assert K == Kb, (a.shape, b.shape, ta, tb)
    tm, tn, tk = _pick(M, tm), _pick(N, tn), _pick(K, tk)
    nk = K // tk
    a_spec = pl.BlockSpec((tk, tm), lambda i, j, k: (k, i)) if ta else pl.BlockSpec((tm, tk), lambda i, j, k: (i, k))
    b_spec = pl.BlockSpec((tn, tk), lambda i, j, k: (j, k)) if tb else pl.BlockSpec((tk, tn), lambda i, j, k: (k, j))
    o_spec = pl.BlockSpec((tm, tn), lambda i, j, k: (i, j))
    dn = (((0 if ta else 1,), (1 if tb else 0,)), ((), ()))
    has_res = res is not None

    def body(*refs):
        a_ref, b_ref = refs[0], refs[1]
        r_ref = refs[2] if has_res else None
        o_ref = refs[3] if has_res else refs[2]
        part = lax.dot_general(a_ref[...].astype(BF16), b_ref[...].astype(BF16), dn, preferred_element_type=F32)

        def finish(acc):
            if alpha != 1.0:
                acc = acc * alpha
            if has_res:
                acc = acc + r_ref[...]
            o_ref[...] = acc.astype(o_ref.dtype)

        if nk == 1:
            finish(part)
        else:
            acc_ref = refs[-1]
            k = pl.program_id(2)

            @pl.when(k == 0)
            def _():
                acc_ref[...] = part

            @pl.when(k > 0)
            def _():
                acc_ref[...] += part

            @pl.when(k == nk - 1)
            def _():
                finish(acc_ref[...])

    ins = [a, b] + ([res] if has_res else [])
    in_specs = [a_spec, b_spec] + ([o_spec] if has_res else [])
    return _pallas(
        body, name=name, grid=(M // tm, N // tn, nk), in_specs=in_specs, out_specs=o_spec,
        out_shape=jax.ShapeDtypeStruct((M, N), out_dtype),
        scratch_shapes=[pltpu.VMEM((tm, tn), F32)] if nk > 1 else [],
        compiler_params=_params(("parallel", "parallel", "arbitrary")),
    )(*ins)


def _col0(j):
    return 0


def _colj(j):
    return j


def _rowmap(fn, *, name, rows, tile, ins, consts=(), outs=(), accs=(), ncol=1):
    tile = min(tile, rows)
    nrow = rows // tile
    in_specs = [pl.BlockSpec((tile, w), lambda j, i, f=f: (i, f(j))) for _, w, f in ins]
    for arr, w, f in consts:
        in_specs.append(pl.BlockSpec((arr.shape[0], w), lambda j, i, f=f: (0, f(j))))
    out_specs = [pl.BlockSpec((tile, w), lambda j, i, f=f: (i, f(j))) for _, _, w, f in outs]
    out_specs += [pl.BlockSpec((1, w), lambda j, i, f=f: (0, f(j))) for _, w, f in accs]
    out_shape = [jax.ShapeDtypeStruct((rows, c), dt) for c, dt, _, _ in outs]
    out_shape += [jax.ShapeDtypeStruct((1, c), F32) for c, _, _ in accs]
    n_in, n_out = len(ins) + len(consts), len(outs)
    acc_fixed = [f is _col0 for _, _, f in accs]

    def body(*refs):
        res = fn(*[r[...] for r in refs[:n_in]])
        if not isinstance(res, (tuple, list)):
            res = (res,)
        for r, v in zip(refs[n_in:n_in + n_out], res[:n_out]):
            r[...] = v.astype(r.dtype)
        j, i = pl.program_id(0), pl.program_id(1)
        for r, v, fixed in zip(refs[n_in + n_out:], res[n_out:], acc_fixed):
            first = ((i == 0) & (j == 0)) if fixed else (i == 0)

            @pl.when(first)
            def _(r=r, v=v):
                r[...] = v

            @pl.when(jnp.logical_not(first))
            def _(r=r, v=v):
                r[...] += v

    arrays = [a for a, _, _ in ins] + [a for a, _, _ in consts]
    got = _pallas(
        body, name=name, grid=(ncol, nrow), in_specs=in_specs, out_specs=out_specs, out_shape=out_shape,
        compiler_params=_params(("arbitrary", "arbitrary")),
    )(*arrays)
    return got


def _rms_fwd(x, g, *, name, tile=512):
    rows, d = x.shape

    def fn(xv, gv):
        r = lax.rsqrt(jnp.mean(xv * xv, axis=-1, keepdims=True) + EPS)
        return xv * r * gv

    return _rowmap(fn, name=name, rows=rows, tile=tile, ins=[(x, d, _col0)], consts=[(g, d, _col0)],
                   outs=[(d, BF16, d, _col0)])[0]


def _rms_bwd(dh, x, g, res, *, name, tile=512):
    rows, d = x.shape

    def fn(dhv, xv, rv, gv):
        r = lax.rsqrt(jnp.mean(xv * xv, axis=-1, keepdims=True) + EPS)
        xh = xv * r
        dxh = dhv * gv
        dx = r * (dxh - xh * jnp.mean(dxh * xh, axis=-1, keepdims=True))
        return rv + dx, jnp.sum(dhv * xh, axis=0, keepdims=True)

    return _rowmap(fn, name=name, rows=rows, tile=tile, ins=[(dh, d, _col0), (x, d, _col0), (res, d, _col0)],
                   consts=[(g, d, _col0)], outs=[(d, F32, d, _col0)], accs=[(d, d, _col0)])


def _swiglu_fwd(gu, *, name, tile=512):
    rows = gu.shape[0]
    w = _pick(D_FF, 1408)
    nb = D_FF // w

    def fn(gv, uv):
        return gv * _sigmoid(gv) * uv

    return _rowmap(fn, name=name, rows=rows, tile=tile, ncol=nb,
                   ins=[(gu, w, _colj), (gu, w, lambda j: j + nb)], outs=[(D_FF, BF16, w, _colj)])[0]


def _swiglu_bwd(da, gu, *, name, tile=512):
    rows = gu.shape[0]
    w = _pick(D_FF, 1408)
    nb = D_FF // w

    def fn(dav, gv, uv):
        sg = _sigmoid(gv)
        dg = dav * uv * (sg * (1.0 + gv * (1.0 - sg)))
        du = dav * (gv * sg)
        return dg, du

    return _rowmap(fn, name=name, rows=rows, tile=tile, ncol=nb,
                   ins=[(da, w, _colj), (gu, w, _colj), (gu, w, lambda j: j + nb)],
                   outs=[(D_FF, BF16, w, _colj), (D_FF, BF16, w, _colj)])


def _lora_norm_fwd(u_small, qg, kvg, *, name, tile=512):
    rows = u_small.shape[0]

    def fn(cq, ckv0, ckv1, qgv, kg0, kg1):
        rq = lax.rsqrt(jnp.mean(cq * cq, axis=-1, keepdims=True) + EPS)
        ss = jnp.sum(ckv0 * ckv0, axis=-1, keepdims=True) + jnp.sum(ckv1 * ckv1, axis=-1, keepdims=True)
        rk = lax.rsqrt(ss / KV_LORA + EPS)
        return cq * rq * qgv, ckv0 * rk * kg0, ckv1 * rk * kg1

    got = _rowmap(fn, name=name, rows=rows, tile=tile,
                  ins=[(u_small, 384, _col0), (u_small, 128, lambda j: 3), (u_small, 128, lambda j: 4)],
                  consts=[(qg, 384, _col0), (kvg, 128, _col0), (kvg, 128, lambda j: 1)],
                  outs=[(384, BF16, 384, _col0), (128, BF16, 128, _col0), (128, BF16, 128, _col0)])
    return got[0], jnp.concatenate([got[1], got[2]], axis=1)


def _lora_norm_bwd(dcqn, dckvn, u_small, qg, kvg, dkpe, draw_f, draw_b, *, name, tile=512):
    rows = u_small.shape[0]

    def fn(dq, dk0, dk1, cq, c0, c1, dkp, df, db, qgv, kg0, kg1):
        rq = lax.rsqrt(jnp.mean(cq * cq, axis=-1, keepdims=True) + EPS)
        xh = cq * rq
        dxh = dq * qgv
        dcq = rq * (dxh - xh * jnp.mean(dxh * xh, axis=-1, keepdims=True))
        ss = jnp.sum(c0 * c0, axis=-1, keepdims=True) + jnp.sum(c1 * c1, axis=-1, keepdims=True)
        rk = lax.rsqrt(ss / KV_LORA + EPS)
        h0, h1 = c0 * rk, c1 * rk
        e0, e1 = dk0 * kg0, dk1 * kg1
        mean = (jnp.sum(e0 * h0, axis=-1, keepdims=True) + jnp.sum(e1 * h1, axis=-1, keepdims=True)) / KV_LORA
        d0 = rk * (e0 - h0 * mean)
        d1 = rk * (e1 - h1 * mean)
        return (dcq, d0, d1, dkp, df + db,
                jnp.sum(dq * xh, axis=0, keepdims=True), jnp.sum(dk0 * h0, axis=0, keepdims=True),
                jnp.sum(dk1 * h1, axis=0, keepdims=True))

    got = _rowmap(fn, name=name, rows=rows, tile=tile,
                  ins=[(dcqn, 384, _col0), (dckvn, 128, _col0), (dckvn, 128, lambda j: 1),
                       (u_small, 384, _col0), (u_small, 128, lambda j: 3), (u_small, 128, lambda j: 4),
                       (dkpe, 128, _col0), (draw_f, 128, _col0), (draw_b, 128, _col0)],
                  consts=[(qg, 384, _col0), (kvg, 128, _col0), (kvg, 128, lambda j: 1)],
                  outs=[(384, BF16, 384, _col0), (128, BF16, 128, _col0), (128, BF16, 128, _col0),
                        (128, BF16, 128, _col0), (128, BF16, 128, _col0)],
                  accs=[(384, 384, _col0), (128, 128, _col0), (128, 128, _col0)])
    du_small = jnp.concatenate(got[:5], axis=1)
    return du_small, got[5], jnp.concatenate([got[6], got[7]], axis=1)


def _rope(x, c, s1, s2):
    return x * c + pltpu.roll(x, 112, 1) * s1 + pltpu.roll(x, 16, 1) * s2


def _rope_t(d, c, s1, s2):
    return d * c + pltpu.roll(d * s1, 16, 1) + pltpu.roll(d * s2, 112, 1)


def _qk_prep_fwd(q_raw, kv_raw, u_small, tabs, qg, kg, *, name, tile=512):
    rows = q_raw.shape[0]
    c_t, s1_t, s2_t = tabs
    scale = 1.0 / math.sqrt(QK_HEAD)

    def fn(qr, kr, vr, kpe, c, s1, s2, qgv, kgv):
        rq = lax.rsqrt(jnp.sum(qr * qr, axis=-1, keepdims=True) / QK_HEAD + EPS)
        q = _rope(qr * rq * qgv, c, s1, s2) * scale
        xk = kr + pltpu.roll(kpe, 64, 1)
        rk = lax.rsqrt(jnp.sum(xk * xk, axis=-1, keepdims=True) / QK_HEAD + EPS)
        k = _rope(xk * rk * kgv, c, s1, s2)
        return q, k, vr

    return _rowmap(fn, name=name, rows=rows, tile=tile, ncol=N_HEADS,
                   ins=[(q_raw, 128, _colj), (kv_raw, 128, _colj), (kv_raw, 128, lambda j: j + N_HEADS),
                        (u_small, 128, lambda j: 5), (c_t, 128, _col0), (s1_t, 128, _col0), (s2_t, 128, _col0)],
                   consts=[(qg, 128, _col0), (kg, 128, _col0)],
                   outs=[(2048, BF16, 128, _colj), (2048, BF16, 128, _colj), (2048, BF16, 128, _colj)])


def _qk_prep_bwd(dq, dk, dv, q_raw, kv_raw, u_small, tabs, qg, kg, *, name, tile=512):
    rows = q_raw.shape[0]
    c_t, s1_t, s2_t = tabs
    scale = 1.0 / math.sqrt(QK_HEAD)

    def fn(dqv, dkv, dvv, qr, kr, kpe, c, s1, s2, qgv, kgv):
        lane = lax.broadcasted_iota(jnp.int32, qr.shape, 1)
        rq = lax.rsqrt(jnp.sum(qr * qr, axis=-1, keepdims=True) / QK_HEAD + EPS)
        xh = qr * rq
        dy = _rope_t(dqv * scale, c, s1, s2)
        dxh = dy * qgv
        dqr = rq * (dxh - xh * (jnp.sum(dxh * xh, axis=-1, keepdims=True) / QK_HEAD))
        dqg = jnp.sum(dy * xh, axis=0, keepdims=True)
        xk = kr + pltpu.roll(kpe, 64, 1)
        rk = lax.rsqrt(jnp.sum(xk * xk, axis=-1, keepdims=True) / QK_HEAD + EPS)
        kh = xk * rk
        dyk = _rope_t(dkv, c, s1, s2)
        dkh = dyk * kgv
        dxk = rk * (dkh - kh * (jnp.sum(dkh * kh, axis=-1, keepdims=True) / QK_HEAD))
        dkg = jnp.sum(dyk * kh, axis=0, keepdims=True)
        dkr = jnp.where(lane < QK_NOPE, dxk, 0.0)
        dkpe = jnp.where(lane < QK_ROPE, pltpu.roll(dxk, 64, 1), 0.0)
        return dqr, dkr, dvv, dkpe, dqg, dkg

    got = _rowmap(fn, name=name, rows=rows, tile=tile, ncol=N_HEADS,
                  ins=[(dq, 128, _colj), (dk, 128, _colj), (dv, 128, _colj), (q_raw, 128, _colj),
                       (kv_raw, 128, _colj), (u_small, 128, lambda j: 5),
                       (c_t, 128, _col0), (s1_t, 128, _col0), (s2_t, 128, _col0)],
                  consts=[(qg, 128, _col0), (kg, 128, _col0)],
                  outs=[(2048, BF16, 128, _colj), (2048, BF16, 128, _colj), (2048, BF16, 128, _colj),
                        (2048, F32, 128, _colj)],
                  accs=[(128, 128, _col0), (128, 128, _col0)])
    return got


def _sum_heads(x, *, name, tile=512):
    rows = x.shape[0]
    tile = min(tile, rows)

    def body(x_ref, o_ref):
        acc = x_ref[:, 0:LANES]
        for h in range(1, N_HEADS):
            acc = acc + x_ref[:, h * LANES:(h + 1) * LANES]
        o_ref[...] = acc

    return _pallas(body, name=name, grid=(rows // tile,),
                   in_specs=[pl.BlockSpec((tile, N_HEADS * LANES), lambda i: (i, 0))],
                   out_specs=pl.BlockSpec((tile, LANES), lambda i: (i, 0)),
                   out_shape=jax.ShapeDtypeStruct((rows, LANES), F32),
                   compiler_params=_params(("parallel",)))(x)


def _attn_fwd(q, k, v, *, name, tq=512):
    T = q.shape[0]
    tq = min(tq, T)

    def body(q_ref, k_ref, v_ref, o_ref, lse_ref):
        out = None
        for hh in range(2):
            sl = slice(hh * LANES, (hh + 1) * LANES)
            s = _dot_nt(q_ref[:, sl], k_ref[:, sl])
            m = jnp.max(s, axis=-1, keepdims=True)
            p = jnp.exp(s - m)
            l = jnp.sum(p, axis=-1, keepdims=True)
            o = _dot(p.astype(BF16), v_ref[:, sl]) / l
            out = o if out is None else out + o
            lse_ref[hh] = m + jnp.log(l)
        o_ref[...] = out

    return _pallas(
        body, name=name, grid=(N_HEADS // 2, T // tq),
        in_specs=[pl.BlockSpec((tq, 2 * LANES), lambda j, i: (i, j)), pl.BlockSpec((T, 2 * LANES), lambda j, i: (0, j)),
                  pl.BlockSpec((T, 2 * LANES), lambda j, i: (0, j))],
        out_specs=[pl.BlockSpec((tq, LANES), lambda j, i: (i, j)), pl.BlockSpec((2, tq, 1), lambda j, i: (j, i, 0))],
        out_shape=[jax.ShapeDtypeStruct((T, N_HEADS * V_HEAD), F32), jax.ShapeDtypeStruct((N_HEADS, T, 1), F32)],
        compiler_params=_params(("parallel", "parallel")),
    )(q, k, v)


def _attn_bwd(q, k, v, o, lse, do, *, name, tk=256):
    T = q.shape[0]
    tk = min(tk, T)

    def body(q_ref, k_ref, v_ref, o_ref, lse_ref, do_ref, dq_ref, dk_ref, dv_ref, d_s):
        kb = pl.program_id(1)
        lane = lax.broadcasted_iota(jnp.int32, (1, LANES), 1)
        dov = do_ref[...]

        @pl.when(kb == 0)
        def _():
            prod = dov * o_ref[...]
            for hh in range(2):
                keep = (lane < V_HEAD) if hh == 0 else (lane >= V_HEAD)
                d_s[hh] = jnp.sum(jnp.where(keep, prod, 0.0), axis=-1, keepdims=True)

        do_b = dov.astype(BF16)
        for hh in range(2):
            sl = slice(hh * LANES, (hh + 1) * LANES)
            keep = (lane < V_HEAD) if hh == 0 else (lane >= V_HEAD)
            qv, kv, vv = q_ref[:, sl], k_ref[:, sl], v_ref[:, sl]
            s = _dot_nt(qv, kv)
            p = jnp.exp(s - lse_ref[hh])
            dp = _dot_nt(do_b, vv)
            ds = (p * (dp - d_s[hh])).astype(BF16)
            dv_ref[:, sl] = jnp.where(keep, _dot_tn(p.astype(BF16), do_b), 0.0)
            dk_ref[:, sl] = _dot_tn(ds, qv)
            dqp = _dot(ds, kv)

            @pl.when(kb == 0)
            def _(dqp=dqp, sl=sl):
                dq_ref[:, sl] = dqp

            @pl.when(kb > 0)
            def _(dqp=dqp, sl=sl):
                dq_ref[:, sl] += dqp

    pair = pl.BlockSpec((T, 2 * LANES), lambda j, kb: (0, j))
    kblk = pl.BlockSpec((tk, 2 * LANES), lambda j, kb: (kb, j))
    return _pallas(
        body, name=name, grid=(N_HEADS // 2, T // tk),
        in_specs=[pair, kblk, kblk, pl.BlockSpec((T, LANES), lambda j, kb: (0, j)),
                  pl.BlockSpec((2, T, 1), lambda j, kb: (j, 0, 0)), pl.BlockSpec((T, LANES), lambda j, kb: (0, j))],
        out_specs=[pair, kblk, kblk],
        out_shape=[jax.ShapeDtypeStruct((T, 2048), F32)] * 3,
        scratch_shapes=[pltpu.VMEM((2, T, 1), F32)],
        compiler_params=_params(("parallel", "arbitrary")),
    )(q, k, v, o, lse, do)


def _conv_shift(x, sh, t_idx):
    if sh == 0:
        return x
    T = x.shape[0]
    y = pltpu.roll(x, (-sh) % T, 0)
    ok = (t_idx + sh >= 0) & (t_idx + sh < T)
    return jnp.where(ok, y, 0.0)


def _conv_fwd(u_big, conv_w, conv_b, *, name, w=256):
    T = u_big.shape[0]
    first = D_INNER // w

    def body(x_ref, w_ref, b_ref, o_ref):
        x = x_ref[...]
        t_idx = lax.broadcasted_iota(jnp.int32, x.shape, 0)
        acc = b_ref[...] + w_ref[2:3, :] * x
        for j in (0, 1, 3, 4):
            acc = acc + w_ref[j:j + 1, :] * _conv_shift(x, j - 2, t_idx)
        o_ref[...] = acc * _sigmoid(acc)

    return _pallas(
        body, name=name, grid=(XBC_DIM // w,),
        in_specs=[pl.BlockSpec((T, w), lambda j: (0, j + first)), pl.BlockSpec((CONV_WIDTH, w), lambda j: (0, j)),
                  pl.BlockSpec((1, w), lambda j: (0, j))],
        out_specs=pl.BlockSpec((T, w), lambda j: (0, j)),
        out_shape=jax.ShapeDtypeStruct((T, XBC_DIM), F32),
        compiler_params=_params(("parallel",)),
    )(u_big, conv_w, conv_b)


def _conv_bwd(dact_f, dact_b, u_big, conv_w, conv_b, *, name, w=128):
    T = u_big.shape[0]
    first = D_INNER // w

    def body(df_ref, db_ref, x_ref, w_ref, b_ref, dx_ref, dw_ref, dbias_ref):
        x = x_ref[...]
        t_idx = lax.broadcasted_iota(jnp.int32, x.shape, 0)
        pre = b_ref[...] + w_ref[2:3, :] * x
        for j in (0, 1, 3, 4):
            pre = pre + w_ref[j:j + 1, :] * _conv_shift(x, j - 2, t_idx)
        sg = _sigmoid(pre)
        dpre = (df_ref[...] + db_ref[...]) * (sg * (1.0 + pre * (1.0 - sg)))
        dbias_ref[...] = jnp.sum(dpre, axis=0, keepdims=True)
        dx = w_ref[2:3, :] * dpre
        dw_ref[2:3, :] = jnp.sum(dpre * x, axis=0, keepdims=True)
        for j in (0, 1, 3, 4):
            dx = dx + w_ref[j:j + 1, :] * _conv_shift(dpre, 2 - j, t_idx)
            dw_ref[j:j + 1, :] = jnp.sum(dpre * _conv_shift(x, j - 2, t_idx), axis=0, keepdims=True)
        dx_ref[...] = dx.astype(dx_ref.dtype)

    blk = pl.BlockSpec((T, w), lambda j: (0, j))
    return _pallas(
        body, name=name, grid=(XBC_DIM // w,),
        in_specs=[blk, blk, pl.BlockSpec((T, w), lambda j: (0, j + first)),
                  pl.BlockSpec((CONV_WIDTH, w), lambda j: (0, j)), pl.BlockSpec((1, w), lambda j: (0, j))],
        out_specs=[blk, pl.BlockSpec((CONV_WIDTH, w), lambda j: (0, j)), pl.BlockSpec((1, w), lambda j: (0, j))],
        out_shape=[jax.ShapeDtypeStruct((T, XBC_DIM), BF16), jax.ShapeDtypeStruct((CONV_WIDTH, XBC_DIM), F32),
                   jax.ShapeDtypeStruct((1, XBC_DIM), F32)],
        compiler_params=_params(("parallel",)),
    )(dact_f, dact_b, u_big, conv_w, conv_b)


def _ssd_consts(rev):
    off = SSM_HEADS if rev else 0
    h = jnp.arange(LANES, dtype=jnp.int32)[:, None]
    e = (jnp.arange(D_INNER, dtype=jnp.int32)[None, :] // 64 + off == h).astype(F32)
    sel = (jnp.arange(SSM_HEADS * LANES, dtype=jnp.int32)[None, :] // LANES + off == h).astype(F32)
    return e, sel


def _ssd_head_terms(dt_ref, bias_ref, alog_ref, acst_s, rev):
    L = CHUNK
    row = lax.broadcasted_iota(jnp.int32, (L, L), 0)
    col = lax.broadcasted_iota(jnp.int32, (L, L), 1)
    mask = (row <= col) if rev else (row >= col)
    cm = mask.astype(F32)
    cmt = ((row >= col) if rev else (row <= col)).astype(F32)
    pre = dt_ref[...] + bias_ref[...]
    dt = _softplus(pre)
    a = -jnp.exp(alog_ref[...])
    da = dt * a
    acs = _dot_h(cm, da)
    acst_s[...] = _dot_h_tn(da, cmt)
    tot = jnp.sum(da, axis=0, keepdims=True)
    return dict(mask=mask, cm=cm, cmt=cmt, pre=pre, dt=dt, a=a, da=da, acs=acs, tot=tot,
                e=jnp.exp(acs), w=jnp.exp(tot - acs), dec=jnp.exp(tot))


def _ssd_fwd(xbc_act, u_small, bias128, alog128, *, rev, name):
    T = xbc_act.shape[0]
    L = CHUNK
    nc = T // L
    e_mat, sel = _ssd_consts(rev)
    off = SSM_HEADS if rev else 0

    def cidx(c):
        return (nc - 1 - c) if rev else c

    def body(xs_ref, bm_ref, cm_ref, dt_ref, bias_ref, alog_ref, e_ref, sel_ref, y_ref, hin_ref, ht_s, acst_s):
        c = pl.program_id(0)

        @pl.when(c == 0)
        def _():
            ht_s[...] = jnp.zeros_like(ht_s)

        t = _ssd_head_terms(dt_ref, bias_ref, alog_ref, acst_s, rev)
        lo = lax.broadcasted_iota(jnp.int32, (L, LANES), 1) < 64
        dec_rows = jnp.broadcast_to(t["dec"], (L, LANES))
        for g in range(SSM_GROUPS):
            eg = e_ref[:, g * 512:(g + 1) * 512]
            dt_x, e_x, w_x = _dot_h(t["dt"], eg), _dot_h(t["e"], eg), _dot_h(t["w"], eg)
            dec_x = _dot_h(dec_rows, eg)
            bmat = bm_ref[:, g * LANES:(g + 1) * LANES].astype(BF16)
            cmat = cm_ref[:, g * LANES:(g + 1) * LANES].astype(BF16)
            xdt = xs_ref[:, g * 512:(g + 1) * 512] * dt_x
            gmat = _dot_nt(cmat, bmat)
            acat = _dot_h(t["acs"], sel_ref[:, g * 1024:(g + 1) * 1024])
            ht = ht_s[g]
            yoff = e_x * _dot(cmat, ht.astype(BF16))
            for pr in range(4):
                xp = xdt[:, pr * LANES:(pr + 1) * LANES]
                acc = yoff[:, pr * LANES:(pr + 1) * LANES]
                for s_ in range(2):
                    i = 2 * pr + s_
                    h = off + 8 * g + i
                    seg = acat[:, i * LANES:(i + 1) * LANES] - acst_s[h:h + 1, :]
                    lam = jnp.exp(jnp.where(t["mask"], seg, -1e30))
                    m = (gmat * lam).astype(BF16)
                    xm = jnp.where(lo if s_ == 0 else jnp.logical_not(lo), xp, 0.0).astype(BF16)
                    acc = acc + _dot(m, xm)
                y_ref[:, g * 512 + pr * LANES:g * 512 + (pr + 1) * LANES] = acc
            hin_ref[0, g] = ht
            ht_s[g] = ht * dec_x + _dot_tn(bmat, (w_x * xdt).astype(BF16))

    return _pallas(
        body, name=name, grid=(nc,),
        in_specs=[pl.BlockSpec((L, D_INNER), lambda c: (cidx(c), 0)), pl.BlockSpec((L, 512), lambda c: (cidx(c), 4)),
                  pl.BlockSpec((L, 512), lambda c: (cidx(c), 5)), pl.BlockSpec((L, LANES), lambda c: (cidx(c), 6)),
                  pl.BlockSpec((1, LANES), lambda c: (0, 0)), pl.BlockSpec((1, LANES), lambda c: (0, 0)),
                  pl.BlockSpec((LANES, D_INNER), lambda c: (0, 0)), pl.BlockSpec((LANES, 4096), lambda c: (0, 0))],
        out_specs=[pl.BlockSpec((L, D_INNER), lambda c: (cidx(c), 0)),
                   pl.BlockSpec((1, SSM_GROUPS, D_STATE, 512), lambda c: (cidx(c), 0, 0, 0))],
        out_shape=[jax.ShapeDtypeStruct((T, D_INNER), F32), jax.ShapeDtypeStruct((nc, SSM_GROUPS, D_STATE, 512), F32)],
        scratch_shapes=[pltpu.VMEM((SSM_GROUPS, D_STATE, 512), F32), pltpu.VMEM((LANES, L), F32)],
        compiler_params=_params(("arbitrary",)),
    )(xbc_act, xbc_act, xbc_act, u_small, bias128, alog128, e_mat, sel)


def _ssd_bwd(dy, xbc_act, u_small, bias128, alog128, hin, skip_x, *, rev, name):
    T = xbc_act.shape[0]
    L = CHUNK
    nc = T // L
    e_mat, sel = _ssd_consts(rev)
    off = SSM_HEADS if rev else 0
    has_skip = skip_x is not None

    def cidx(c):
        return c if rev else (nc - 1 - c)

    def body(*refs):
        (dy_ref, xs_ref, bm_ref, cm_ref, dt_ref, bias_ref, alog_ref, hin_ref, e_ref, sel_ref) = refs[:10]
        k = 10
        skip_ref = refs[k] if has_skip else None
        k += 1 if has_skip else 0
        dx_ref, draw_ref, dalog_ref, dbias_ref, dht_s, acst_s, rowt_s, dxdt_s, te_s, tw_s = refs[k:]
        c = pl.program_id(0)

        @pl.when(c == 0)
        def _():
            dht_s[...] = jnp.zeros_like(dht_s)
            rowt_s[...] = jnp.zeros_like(rowt_s)

        t = _ssd_head_terms(dt_ref, bias_ref, alog_ref, acst_s, rev)
        lane1 = lax.broadcasted_iota(jnp.int32, (1, LANES), 1)
        lo = lax.broadcasted_iota(jnp.int32, (L, LANES), 1) < 64
        dec_rows = jnp.broadcast_to(t["dec"], (L, LANES))
        colpart = jnp.zeros((L, LANES), F32)
        dtot_h = jnp.zeros((L, LANES), F32)
        for g in range(SSM_GROUPS):
            gs = slice(g * 512, (g + 1) * 512)
            eg = e_ref[:, gs]
            dt_x, e_x, w_x = _dot_h(t["dt"], eg), _dot_h(t["e"], eg), _dot_h(t["w"], eg)
            dec_x = _dot_h(dec_rows, eg)
            bmat = bm_ref[:, g * LANES:(g + 1) * LANES].astype(BF16)
            cmat = cm_ref[:, g * LANES:(g + 1) * LANES].astype(BF16)
            xdt = xs_ref[:, gs] * dt_x
            dyg = dy_ref[:, gs]
            gmat = _dot_nt(cmat, bmat)
            acat = _dot_h(t["acs"], sel_ref[:, g * 1024:(g + 1) * 1024])
            ht_in = hin_ref[0, g]
            dht = dht_s[g]
            ht_in_b, dht_b = ht_in.astype(BF16), dht.astype(BF16)
            ch = _dot(cmat, ht_in_b)
            bdh = _dot(bmat, dht_b)
            edy = e_x * dyg
            wx = w_x * xdt
            te_s[:, gs] = edy * ch
            tw_s[:, gs] = wx * bdh
            edy_b = edy.astype(BF16)
            dcm = _dot_nt(edy_b, ht_in_b)
            dbm = _dot_nt(wx.astype(BF16), dht_b)
            dht_s[g] = dec_x * dht + _dot_tn(cmat, edy_b)
            th = jnp.broadcast_to(jnp.sum(dht * ht_in, axis=0, keepdims=True), (L, 512))
            dtot_h = dtot_h + _dot_h_nt(th, eg)
            dgm = jnp.zeros((L, L), F32)
            for pr in range(4):
                ps = slice(pr * LANES, (pr + 1) * LANES)
                xp = xdt[:, ps]
                dyp_b = dyg[:, ps].astype(BF16)
                dxp = w_x[:, ps] * bdh[:, ps]
                for s_ in range(2):
                    i = 2 * pr + s_
                    h = off + 8 * g + i
                    keep = lo if s_ == 0 else jnp.logical_not(lo)
                    seg = acat[:, i * LANES:(i + 1) * LANES] - acst_s[h:h + 1, :]
                    lam = jnp.exp(jnp.where(t["mask"], seg, -1e30))
                    mf = gmat * lam
                    m = mf.astype(BF16)
                    xm = jnp.where(keep, xp, 0.0).astype(BF16)
                    dm = _dot_nt(dyp_b, xm)
                    dgm = dgm + dm * lam
                    q = dm * mf
                    colpart = colpart + jnp.sum(q, axis=1, keepdims=True) * (lane1 == h).astype(F32)
                    rowt_s[h:h + 1, :] = jnp.sum(q, axis=0, keepdims=True)
                    dxp = dxp + jnp.where(keep, _dot_tn(m, dyp_b), 0.0)
                dxdt_s[:, g * 512 + pr * LANES:g * 512 + (pr + 1) * LANES] = dxp
            dgm_b = dgm.astype(BF16)
            dx_ref[:, D_INNER + g * LANES:D_INNER + (g + 1) * LANES] = dbm + _dot_tn(dgm_b, cmat)
            dx_ref[:, D_INNER + 512 + g * LANES:D_INNER + 512 + (g + 1) * LANES] = dcm + _dot(dgm_b, bmat)

        e_all = e_ref[...]
        t_e = _dot_h_nt(te_s[...], e_all)
        t_w = _dot_h_nt(tw_s[...], e_all)
        row = lax.broadcasted_iota(jnp.int32, (L, L), 0)
        col = lax.broadcasted_iota(jnp.int32, (L, L), 1)
        ident = (row == col).astype(F32)
        colsum_part = _dot_h_tn(rowt_s[...], ident)
        dtot = jnp.sum(t_w, axis=0, keepdims=True) + t["dec"] * dtot_h
        row1 = lax.broadcasted_iota(jnp.int32, (L, LANES), 0)
        last = row1 == (0 if rev else L - 1)
        dacs = colpart - colsum_part + t_e - t_w + jnp.where(last, dtot, 0.0)
        dda = _dot_h(t["cmt"], dacs)
        dxdt = dxdt_s[...]
        ddt = dda * t["a"] + _dot_h_nt(dxdt * xs_ref[...], e_all)
        dalog = jnp.sum(dda * t["dt"], axis=0, keepdims=True) * t["a"]
        draw = ddt * _sigmoid(t["pre"])
        draw_ref[...] = draw
        dbias = jnp.sum(draw, axis=0, keepdims=True)

        @pl.when(c == 0)
        def _():
            dalog_ref[...] = dalog
            dbias_ref[...] = dbias

        @pl.when(c > 0)
        def _():
            dalog_ref[...] += dalog
            dbias_ref[...] += dbias

        dxs = dxdt * _dot_h(t["dt"], e_all)
        if has_skip:
            dxs = dxs + dy_ref[...] * skip_ref[...]
        dx_ref[:, 0:D_INNER] = dxs

    one = pl.BlockSpec((1, LANES), lambda c: (0, 0))
    in_specs = [pl.BlockSpec((L, D_INNER), lambda c: (cidx(c), 0)), pl.BlockSpec((L, D_INNER), lambda c: (cidx(c), 0)),
                pl.BlockSpec((L, 512), lambda c: (cidx(c), 4)), pl.BlockSpec((L, 512), lambda c: (cidx(c), 5)),
                pl.BlockSpec((L, LANES), lambda c: (cidx(c), 6)), one, one,
                pl.BlockSpec((1, SSM_GROUPS, D_STATE, 512), lambda c: (cidx(c), 0, 0, 0)),
                pl.BlockSpec((LANES, D_INNER), lambda c: (0, 0)), pl.BlockSpec((LANES, 4096), lambda c: (0, 0))]
    ins = [dy, xbc_act, xbc_act, xbc_act, u_small, bias128, alog128, hin, e_mat, sel]
    if has_skip:
        in_specs.append(pl.BlockSpec((1, D_INNER), lambda c: (0, 0)))
        ins.append(skip_x)
    return _pallas(
        body, name=name, grid=(nc,), in_specs=in_specs,
        out_specs=[pl.BlockSpec((L, XBC_DIM), lambda c: (cidx(c), 0)), pl.BlockSpec((L, LANES), lambda c: (cidx(c), 0)),
                   one, one],
        out_shape=[jax.ShapeDtypeStruct((T, XBC_DIM), F32), jax.ShapeDtypeStruct((T, LANES), F32),
                   jax.ShapeDtypeStruct((1, LANES), F32), jax.ShapeDtypeStruct((1, LANES), F32)],
        scratch_shapes=[pltpu.VMEM((SSM_GROUPS, D_STATE, 512), F32), pltpu.VMEM((LANES, L), F32),
                        pltpu.VMEM((LANES, L), F32), pltpu.VMEM((L, D_INNER), F32), pltpu.VMEM((L, D_INNER), F32),
                        pltpu.VMEM((L, D_INNER), F32)],
        compiler_params=_params(("arbitrary",)),
    )(*ins)


def _ssm_out_fwd(y_f, y_b, xbc_act, u_big, skip_x, ssm_norm, *, name, tile=512):
    rows = y_f.shape[0]

    def fn(yf, yb, xs, z, sk, nw):
        yz = (yf + yb + sk * xs) * (z * _sigmoid(z))
        r = lax.rsqrt(jnp.mean(yz * yz, axis=-1, keepdims=True) + EPS)
        return yz * r * nw

    return _rowmap(fn, name=name, rows=rows, tile=tile, ncol=SSM_GROUPS,
                   ins=[(y_f, 512, _colj), (y_b, 512, _colj), (xbc_act, 512, _colj), (u_big, 512, _colj)],
                   consts=[(skip_x, 512, _colj), (ssm_norm, 512, _colj)], outs=[(D_INNER, BF16, 512, _colj)])[0]


def _ssm_out_bwd(dm, y_f, y_b, xbc_act, u_big, skip_x, ssm_norm, *, name, tile=512):
    rows = y_f.shape[0]

    def fn(dmv, yf, yb, xs, z, sk, nw):
        sg = _sigmoid(z)
        y = yf + yb + sk * xs
        yz = y * (z * sg)
        r = lax.rsqrt(jnp.mean(yz * yz, axis=-1, keepdims=True) + EPS)
        xh = yz * r
        dxh = dmv * nw
        dyz = r * (dxh - xh * jnp.mean(dxh * xh, axis=-1, keepdims=True))
        dy = dyz * (z * sg)
        dz = dyz * y * (sg * (1.0 + z * (1.0 - sg)))
        return dy, dz, jnp.sum(dmv * xh, axis=0, keepdims=True), jnp.sum(dy * xs, axis=0, keepdims=True)

    return _rowmap(fn, name=name, rows=rows, tile=tile, ncol=SSM_GROUPS,
                   ins=[(dm, 512, _colj), (y_f, 512, _colj), (y_b, 512, _colj), (xbc_act, 512, _colj),
                        (u_big, 512, _colj)],
                   consts=[(skip_x, 512, _colj), (ssm_norm, 512, _colj)],
                   outs=[(D_INNER, F32, 512, _colj), (D_INNER, BF16, 512, _colj)],
                   accs=[(D_INNER, 512, _colj), (D_INNER, 512, _colj)])


def _head_sum(x, *, name):
    e_mat, _ = _ssd_consts(False)

    def body(x_ref, e_ref, o_ref):
        o_ref[...] = _dot_h_nt(jnp.broadcast_to(x_ref[...], (8, D_INNER)), e_ref[...])

    return _pallas(body, name=name, out_shape=jax.ShapeDtypeStruct((8, LANES), F32))(x, e_mat)


def _merge_fwd(pa, pb, u_big, *, name, tile=512):
    rows = pa.shape[0]

    def fn(a, b, ga, gb):
        return _sigmoid(ga) * a + _sigmoid(gb) * b

    return _rowmap(fn, name=name, rows=rows, tile=tile,
                   ins=[(pa, 1024, _col0), (pb, 1024, _col0), (u_big, 1024, lambda j: 5), (u_big, 1024, lambda j: 6)],
                   outs=[(1024, BF16, 1024, _col0)])[0]


def _merge_bwd(dmg, pa, pb, u_big, *, name, tile=512):
    rows = pa.shape[0]

    def fn(d, a, b, ga, gb):
        sa, sb = _sigmoid(ga), _sigmoid(gb)
        return d * sa, d * sb, d * a * sa * (1.0 - sa), d * b * sb * (1.0 - sb)

    return _rowmap(fn, name=name, rows=rows, tile=tile,
                   ins=[(dmg, 1024, _col0), (pa, 1024, _col0), (pb, 1024, _col0), (u_big, 1024, lambda j: 5),
                        (u_big, 1024, lambda j: 6)],
                   outs=[(1024, BF16, 1024, _col0)] * 4)


def _loss_bwd(y, target, *, name, tile=512):
    rows, d = y.shape

    def fn(yv, tv):
        err = yv - tv
        part = jnp.sum(jnp.sum(err * err, axis=-1, keepdims=True), axis=0, keepdims=True)
        return err * (1.0 / d), jnp.broadcast_to(part * (0.5 / d), (1, LANES))

    dy, part = _rowmap(fn, name=name, rows=rows, tile=tile, ins=[(y, d, _col0), (target, d, _col0)],
                       outs=[(d, F32, d, _col0)], accs=[(LANES, LANES, _col0)])
    return dy, part[0, 0]


def _mesh_pos():
    return lax.axis_index("x"), lax.axis_index("y"), lax.axis_index("c")


def _all_gather(shard, *, name):
    R, C = shard.shape

    def body(x_ref, out_ref, send_sems, recv_sems, local_sem):
        x, y, c = _mesh_pos()
        me, sibling = (x, y, c), (x, y, 1 - c)
        chips = [(1 - x, y), (x, 1 - y), (1 - x, 1 - y)]

        def rows(px, py, pc):
            return out_ref.at[4 * px + 2 * py + pc]

        def copy(k, block, to, src=None):
            return pltpu.make_async_remote_copy(
                src_ref=rows(*block) if src is None else src, dst_ref=rows(*block),
                send_sem=send_sems.at[k], recv_sem=recv_sems.at[k], device_id=to, device_id_type=pl.DeviceIdType.MESH)

        mine = pltpu.make_async_copy(x_ref, rows(*me), local_sem)
        mine.start()
        first = [copy(0, me, sibling, src=x_ref)]
        first += [copy(1 + j, me, (*chip, c), src=x_ref) for j, chip in enumerate(chips)]
        for cp in first:
            cp.start()
        passed = [copy(4 + j, (*chip, c), sibling) for j, chip in enumerate(chips)]
        for j, chip in enumerate(chips):
            copy(1 + j, (*chip, c), me).wait_recv()
            passed[j].start()
        copy(0, sibling, me).wait_recv()
        for j, chip in enumerate(chips):
            copy(4 + j, (*chip, 1 - c), me).wait_recv()
        for cp in first + passed:
            cp.wait_send()
        mine.wait()

    return _pallas(
        body, name=name, out_shape=jax.ShapeDtypeStruct((N_DEV, R, C), shard.dtype),
        in_specs=[pl.BlockSpec(memory_space=pl.ANY)], out_specs=pl.BlockSpec(memory_space=pl.ANY),
        scratch_shapes=[pltpu.SemaphoreType.DMA((7,)), pltpu.SemaphoreType.DMA((7,)), pltpu.SemaphoreType.DMA],
    )(shard)


def _grad_exchange(gslab, small, *, name):
    _, R, C = gslab.shape
    S = small.shape[0]

    def body(g_ref, s_ref, recv_ref, srecv_ref, send_sems, recv_sems, local_sems):
        x, y, c = _mesh_pos()
        me = 4 * x + 2 * y + c
        loc_g = pltpu.make_async_copy(g_ref.at[me], recv_ref.at[me], local_sems.at[0])
        loc_s = pltpu.make_async_copy(s_ref, srecv_ref.at[me], local_sems.at[1])
        loc_g.start()
        loc_s.start()
        copies = []
        for k in range(1, N_DEV):
            px = (1 - x) if (k & 4) else x
            py = (1 - y) if (k & 2) else y
            pc = (1 - c) if (k & 1) else c
            peer = 4 * px + 2 * py + pc
            copies.append(pltpu.make_async_remote_copy(
                src_ref=g_ref.at[peer], dst_ref=recv_ref.at[me], send_sem=send_sems.at[k - 1],
                recv_sem=recv_sems.at[k - 1], device_id=(px, py, pc), device_id_type=pl.DeviceIdType.MESH))
            copies.append(pltpu.make_async_remote_copy(
                src_ref=s_ref, dst_ref=srecv_ref.at[me], send_sem=send_sems.at[6 + k],
                recv_sem=recv_sems.at[6 + k], device_id=(px, py, pc), device_id_type=pl.DeviceIdType.MESH))
        for cp in copies:
            cp.start()
        for k in range(1, N_DEV):
            px = (1 - x) if (k & 4) else x
            py = (1 - y) if (k & 2) else y
            pc = (1 - c) if (k & 1) else c
            peer = 4 * px + 2 * py + pc
            pltpu.make_async_remote_copy(
                src_ref=g_ref.at[peer], dst_ref=recv_ref.at[peer], send_sem=send_sems.at[k - 1],
                recv_sem=recv_sems.at[k - 1], device_id=(px, py, pc), device_id_type=pl.DeviceIdType.MESH).wait_recv()
            pltpu.make_async_remote_copy(
                src_ref=s_ref, dst_ref=srecv_ref.at[peer], send_sem=send_sems.at[6 + k],
                recv_sem=recv_sems.at[6 + k], device_id=(px, py, pc), device_id_type=pl.DeviceIdType.MESH).wait_recv()
        for cp in copies:
            cp.wait_send()
        loc_g.wait()
        loc_s.wait()

    any_spec = pl.BlockSpec(memory_space=pl.ANY)
    return _pallas(
        body, name=name,
        out_shape=[jax.ShapeDtypeStruct((N_DEV, R, C), gslab.dtype), jax.ShapeDtypeStruct((N_DEV, S, C), small.dtype)],
        in_specs=[any_spec, any_spec], out_specs=[any_spec, any_spec],
        scratch_shapes=[pltpu.SemaphoreType.DMA((14,)), pltpu.SemaphoreType.DMA((14,)), pltpu.SemaphoreType.DMA((2,))],
    )(gslab, small)


def _adamw_math(g, w, m, v):
    m2 = ADAM_B1 * m + (1.0 - ADAM_B1) * g
    v2 = ADAM_B2 * v + (1.0 - ADAM_B2) * (g * g)
    m_hat = m2 / (1.0 - ADAM_B1 ** ADAM_STEP)
    v_hat = v2 / (1.0 - ADAM_B2 ** ADAM_STEP)
    delta = -ADAM_LR * (m_hat / (jnp.sqrt(v_hat) + ADAM_EPS) + ADAM_WD * w)
    return delta, m2, v2


def _reduce_adamw(recv, w, m, v, *, name, tile):
    _, R, C = recv.shape
    tile = min(tile, R)

    def body(r_ref, w_ref, m_ref, v_ref, g_ref, d_ref, m2_ref, v2_ref):
        g = r_ref[0].astype(F32)
        for s in range(1, N_DEV):
            g = g + r_ref[s].astype(F32)
        delta, m2, v2 = _adamw_math(g, w_ref[...], m_ref[...], v_ref[...])
        g_ref[...] = g
        d_ref[...] = delta
        m2_ref[...] = m2
        v2_ref[...] = v2

    blk = pl.BlockSpec((tile, C), lambda i: (i, 0))
    return _pallas(
        body, name=name, grid=(R // tile,),
        in_specs=[pl.BlockSpec((N_DEV, tile, C), lambda i: (0, i, 0)), blk, blk, blk], out_specs=[blk] * 4,
        out_shape=[jax.ShapeDtypeStruct((R, C), F32)] * 4, compiler_params=_params(("parallel",)),
    )(recv, w, m, v)


def _pack_rows(parts, rows, dtype):
    flat = jnp.concatenate([p.reshape(-1).astype(dtype) for p in parts])
    return jnp.pad(flat, (0, rows * SLAB_COLS - flat.shape[0])).reshape(rows, SLAB_COLS)


def _unpack_full(gathered):
    full, off = {}, 0
    for name, (r, c), axis in SHARDED:
        n = r * c // N_DEV
        seg = gathered[:, off:off + n]
        if axis == 1:
            full[name] = seg.reshape(N_DEV, r, c // N_DEV).transpose(1, 0, 2).reshape(r, c)
        else:
            full[name] = seg.reshape(r, c)
        off += n
    return full


def _pack_grads(grads):
    parts = []
    for name, (r, c), axis in SHARDED:
        g = grads[name]
        if axis == 1:
            parts.append(g.reshape(r, N_DEV, c // N_DEV).transpose(1, 0, 2).reshape(N_DEV, -1))
        else:
            parts.append(g.reshape(N_DEV, -1))
    flat = jnp.concatenate(parts, axis=1).astype(BF16)
    flat = jnp.pad(flat, ((0, 0), (0, SLAB_ROWS * SLAB_COLS - flat.shape[1])))
    return flat.reshape(N_DEV, SLAB_ROWS, SLAB_COLS)


def _unpack_shards(slab, shard_shapes):
    flat, out, off = slab.reshape(-1), {}, 0
    for name, (r, c), _ in SHARDED:
        n = r * c // N_DEV
        out[name] = flat[off:off + n].reshape(shard_shapes[name])
        off += n
    return out


def _unpack_small(slab, shapes):
    flat, out, off = slab.reshape(-1), {}, 0
    for name, n in SMALL:
        out[name] = flat[off:off + n].reshape(shapes[name])
        off += n
    return out


def _split_bf16(w):
    hi = w.astype(BF16)
    return hi, (w - hi.astype(F32)).astype(BF16)


def _ffn_fwd(x, norm, w_gu, w_d, tag):
    h = _rms_fwd(x, norm, name=f"{tag}_rms")
    gu = _mm(h, w_gu, name=f"{tag}_gu")
    act = _swiglu_fwd(gu, name=f"{tag}_act")
    out = _mm(act, w_d, name=f"{tag}_down", alpha=0.5, res=x)
    return out, (h, gu, act)


def _ffn_bwd(dout, x, norm, w_gu, w_d, saved, tag):
    h, gu, act = saved
    d_act = _mm(dout, w_d, name=f"{tag}_dact", tb=True, alpha=0.5)
    d_wd = _mm(act, dout, name=f"{tag}_dwd", ta=True, alpha=0.5, tn=1024)
    dg, du = _swiglu_bwd(d_act, gu, name=f"{tag}_dswiglu")
    d_wg = _mm(h, dg, name=f"{tag}_dwg", ta=True)
    d_wu = _mm(h, du, name=f"{tag}_dwu", ta=True)
    dh = _mm(dg, w_gu[:, :D_FF], name=f"{tag}_dh_g", tb=True)
    dh = _mm(du, w_gu[:, D_FF:], name=f"{tag}_dh_u", tb=True, res=dh)
    dx, dnorm = _rms_bwd(dh, x, norm, dout, name=f"{tag}_drms")
    return dx, dnorm, d_wg, d_wu, d_wd


def _rope_tables(positions, T):
    pos = positions.reshape(T).astype(F32)
    inv_freq = 1.0 / (ROPE_BASE ** (jnp.arange(0, QK_ROPE, 2, dtype=F32) / QK_ROPE))
    ang = pos[:, None] * inv_freq
    cos, sin = jnp.cos(ang), jnp.sin(ang)
    one64, z64 = jnp.ones((T, 64), F32), jnp.zeros((T, 64), F32)
    z16, z32, one32 = jnp.zeros((T, 16), F32), jnp.zeros((T, 32), F32), jnp.ones((T, 32), F32)
    c = jnp.concatenate([one64, cos, cos, one32], axis=1)
    s1 = jnp.concatenate([z64, -sin, z16, z32], axis=1)
    s2 = jnp.concatenate([z64, z16, sin, z32], axis=1)
    return c, s1, s2


def _derived_weights(full):
    w_in = full["w_in"]
    z1024 = lambda n: jnp.zeros((D_MODEL, n), BF16)
    w_small = jnp.concatenate([w_in[:, 0:384], w_in[:, 384:640], w_in[:, 640:672], z1024(96),
                               w_in[:, 5792:5824], w_in[:, 5824:5856], z1024(64)], axis=1)
    w_big = jnp.concatenate([w_in[:, 672:2720], w_in[:, 2720:5792], w_in[:, 5856:6880], w_in[:, 6880:7904]], axis=1)
    wq = full["w_q_b"].reshape(Q_LORA, N_HEADS, QK_HEAD)
    wq = jnp.pad(wq, ((0, 0), (0, 0), (0, LANES - QK_HEAD))).reshape(Q_LORA, N_HEADS * LANES)
    wkv = full["w_kv_b"].reshape(KV_LORA, N_HEADS, QK_NOPE + V_HEAD)
    wk = jnp.pad(wkv[..., :QK_NOPE], ((0, 0), (0, 0), (0, LANES - QK_NOPE))).reshape(KV_LORA, N_HEADS * LANES)
    v = wkv[..., QK_NOPE:]
    zv = jnp.zeros_like(v)
    even = (jnp.arange(N_HEADS) % 2 == 0)[None, :, None]
    wv = jnp.where(even, jnp.concatenate([v, zv], -1), jnp.concatenate([zv, v], -1)).reshape(KV_LORA, N_HEADS * LANES)
    return dict(
        gu1=jnp.concatenate([full["ffn1_w_gate"], full["ffn1_w_up"]], axis=1), d1=full["ffn1_w_down"],
        gu2=jnp.concatenate([full["ffn2_w_gate"], full["ffn2_w_up"]], axis=1), d2=full["ffn2_w_down"],
        small=w_small, big=w_big, wq=wq, wkv=jnp.concatenate([wk, wv], axis=1),
        pa=full["w_attn_branch"], pb=full["w_ssm_branch"], out=full["w_out"],
    )


def _pad_lanes(v, n=LANES):
    return jnp.pad(v, ((0, 0), (0, n - v.shape[1])))


def _block_step(x, positions, target, p, wd, conv_w):
    T = x.shape[0]
    tabs = _rope_tables(positions, T)
    qg, kg = _pad_lanes(p["q_head_norm"]), _pad_lanes(p["k_head_norm"])
    bias128 = _pad_lanes(jnp.concatenate([p["dt_bias_fwd"], p["dt_bias_bwd"]], axis=1))
    alog128 = _pad_lanes(jnp.concatenate([p["a_log_fwd"], p["a_log_bwd"]], axis=1))
    skip_x = jnp.repeat(p["d_skip"], 64, axis=1)

    x1, ffn1_saved = _ffn_fwd(x, p["ffn1_norm"], wd["gu1"], wd["d1"], "ffn1")
    h2 = _rms_fwd(x1, p["mix_norm"], name="mix_rms")
    u_big = _mm(h2, wd["big"], name="in_big")
    u_small = _mm(h2, wd["small"], name="in_small")
    cqn, ckvn = _lora_norm_fwd(u_small, p["q_a_norm"], p["kv_a_norm"], name="lora_norm")
    q_raw = _mm(cqn, wd["wq"], name="q_up")
    kv_raw = _mm(ckvn, wd["wkv"], name="kv_up")
    q, k, v = _qk_prep_fwd(q_raw, kv_raw, u_small, tabs, qg, kg, name="qk_prep")
    a_out, lse = _attn_fwd(q, k, v, name="attn_fwd")
    xbc_act = _conv_fwd(u_big, conv_w, p["conv_b"], name="conv_fwd")
    y_f, hin_f = _ssd_fwd(xbc_act, u_small, bias128, alog128, rev=False, name="ssd_fwd_f")
    y_b, hin_b = _ssd_fwd(xbc_act, u_small, bias128, alog128, rev=True, name="ssd_fwd_b")
    m_out = _ssm_out_fwd(y_f, y_b, xbc_act, u_big, skip_x, p["ssm_norm"], name="ssm_out")
    pa = _mm(a_out, wd["pa"], name="branch_a")
    pb = _mm(m_out, wd["pb"], name="branch_b")
    merged = _merge_fwd(pa, pb, u_big, name="merge")
    x2 = _mm(merged, wd["out"], name="mix_out", res=x1)
    y, ffn2_saved = _ffn_fwd(x2, p["ffn2_norm"], wd["gu2"], wd["d2"], "ffn2")
    dy, loss_part = _loss_bwd(y, target, name="loss")

    gw, gs = {}, {}
    dx2, gs["ffn2_norm"], gw["ffn2_w_gate"], gw["ffn2_w_up"], gw["ffn2_w_down"] = _ffn_bwd(
        dy, x2, p["ffn2_norm"], wd["gu2"], wd["d2"], ffn2_saved, "ffn2b")
    dmerged = _mm(dx2, wd["out"], name="d_merged", tb=True)
    gw["w_out"] = _mm(merged, dx2, name="d_w_out", ta=True)
    dpa, dpb, dga, dgb = _merge_bwd(dmerged, pa, pb, u_big, name="d_merge")
    gw["w_attn_branch"] = _mm(a_out, dpa, name="d_w_pa", ta=True)
    gw["w_ssm_branch"] = _mm(m_out, dpb, name="d_w_pb", ta=True)
    da_out = _mm(dpa, wd["pa"], name="d_a", tb=True)
    dm_out = _mm(dpb, wd["pb"], name="d_m", tb=True)

    dyss, dz, gs["ssm_norm"], dskip_ch = _ssm_out_bwd(dm_out, y_f, y_b, xbc_act, u_big, skip_x, p["ssm_norm"],
                                                      name="d_ssm_out")
    gs["d_skip"] = _head_sum(dskip_ch, name="d_skip_sum")[0:1, :SSM_HEADS]
    dact_f, draw_f, dalog_f, dbias_f = _ssd_bwd(dyss, xbc_act, u_small, bias128, alog128, hin_f, skip_x,
                                                rev=False, name="ssd_bwd_f")
    dact_b, draw_b, dalog_b, dbias_b = _ssd_bwd(dyss, xbc_act, u_small, bias128, alog128, hin_b, None,
                                                rev=True, name="ssd_bwd_b")
    gs["a_log_fwd"], gs["a_log_bwd"] = dalog_f[:, :32], dalog_b[:, 32:64]
    gs["dt_bias_fwd"], gs["dt_bias_bwd"] = dbias_f[:, :32], dbias_b[:, 32:64]
    dxbc, gw["conv_w"], gs["conv_b"] = _conv_bwd(dact_f, dact_b, u_big, conv_w, p["conv_b"], name="conv_bwd")

    dq, dk, dv = _attn_bwd(q, k, v, a_out, lse, da_out, name="attn_bwd")
    dq_raw, dk_raw, dv_raw, dkpe_h, gqh, gkh = _qk_prep_bwd(dq, dk, dv, q_raw, kv_raw, u_small, tabs, qg, kg,
                                                           name="d_qk_prep")
    gs["q_head_norm"], gs["k_head_norm"] = gqh[:, :QK_HEAD], gkh[:, :QK_HEAD]
    dkpe = _sum_heads(dkpe_h, name="d_kpe_sum")
    dkv_raw = jnp.concatenate([dk_raw, dv_raw], axis=1)
    g_wq = _mm(cqn, dq_raw, name="d_w_q", ta=True)
    g_wkv = _mm(ckvn, dkv_raw, name="d_w_kv", ta=True)
    dcqn = _mm(dq_raw, wd["wq"], name="d_cqn", tb=True)
    dckvn = _mm(dkv_raw, wd["wkv"], name="d_ckvn", tb=True)
    du_small, gs["q_a_norm"], gs["kv_a_norm"] = _lora_norm_bwd(
        dcqn, dckvn, u_small, p["q_a_norm"], p["kv_a_norm"], dkpe, draw_f, draw_b, name="d_lora_norm")

    wb = wd["big"]
    dh2 = _mm(du_small, wd["small"], name="d_h2_small", tb=True)
    dh2 = _mm(dz, wb[:, 0:2048], name="d_h2_z", tb=True, res=dh2)
    dh2 = _mm(dxbc, wb[:, 2048:5120], name="d_h2_xbc", tb=True, res=dh2)
    dh2 = _mm(dga, wb[:, 5120:6144], name="d_h2_ga", tb=True, res=dh2)
    dh2 = _mm(dgb, wb[:, 6144:7168], name="d_h2_gb", tb=True, res=dh2)
    g_small = _mm(h2, du_small, name="d_w_small", ta=True)
    g_z = _mm(h2, dz, name="d_w_z", ta=True)
    g_xbc = _mm(h2, dxbc, name="d_w_xbc", ta=True)
    g_ga = _mm(h2, dga, name="d_w_ga", ta=True)
    g_gb = _mm(h2, dgb, name="d_w_gb", ta=True)
    gw["w_in"] = jnp.concatenate([g_small[:, 0:672], g_z, g_xbc, g_small[:, 768:832], g_ga, g_gb], axis=1)
    gq = g_wq.reshape(Q_LORA, N_HEADS, LANES)[..., :QK_HEAD].reshape(Q_LORA, N_HEADS * QK_HEAD)
    gw["w_q_b"] = gq
    gk = g_wkv[:, :2048].reshape(KV_LORA, N_HEADS, LANES)[..., :QK_NOPE]
    gv = g_wkv[:, 2048:].reshape(KV_LORA, N_HEADS, LANES)
    even = (jnp.arange(N_HEADS) % 2 == 0)[None, :, None]
    gv = jnp.where(even, gv[..., :V_HEAD], gv[..., V_HEAD:])
    gw["w_kv_b"] = jnp.concatenate([gk, gv], axis=-1).reshape(KV_LORA, N_HEADS * (QK_NOPE + V_HEAD))
    dx1, gs["mix_norm"] = _rms_bwd(dh2, x1, p["mix_norm"], dx2, name="d_mix_rms")
    grad_x, gs["ffn1_norm"], gw["ffn1_w_gate"], gw["ffn1_w_up"], gw["ffn1_w_down"] = _ffn_bwd(
        dx1, x, p["ffn1_norm"], wd["gu1"], wd["d1"], ffn1_saved, "ffn1b")
    return loss_part, grad_x, gw, gs


def kernel(x, positions, ffn1_norm, ffn1_w_gate, ffn1_w_up, ffn1_w_down, mix_norm, w_in, q_a_norm, w_q_b, kv_a_norm, w_kv_b, q_head_norm, k_head_norm, conv_w, conv_b, a_log_fwd, a_log_bwd, dt_bias_fwd, dt_bias_bwd, d_skip, ssm_norm, w_attn_branch, w_ssm_branch, w_out, ffn2_norm, ffn2_w_gate, ffn2_w_up, ffn2_w_down, loss_target, m_ffn1_norm, m_ffn1_w_gate, m_ffn1_w_up, m_ffn1_w_down, m_mix_norm, m_w_in, m_q_a_norm, m_w_q_b, m_kv_a_norm, m_w_kv_b, m_q_head_norm, m_k_head_norm, m_conv_w, m_conv_b, m_a_log_fwd, m_a_log_bwd, m_dt_bias_fwd, m_dt_bias_bwd, m_d_skip, m_ssm_norm, m_w_attn_branch, m_w_ssm_branch, m_w_out, m_ffn2_norm, m_ffn2_w_gate, m_ffn2_w_up, m_ffn2_w_down, v_ffn1_norm, v_ffn1_w_gate, v_ffn1_w_up, v_ffn1_w_down, v_mix_norm, v_w_in, v_q_a_norm, v_w_q_b, v_kv_a_norm, v_w_kv_b, v_q_head_norm, v_k_head_norm, v_conv_w, v_conv_b, v_a_log_fwd, v_a_log_bwd, v_dt_bias_fwd, v_dt_bias_bwd, v_d_skip, v_ssm_norm, v_w_attn_branch, v_w_ssm_branch, v_w_out, v_ffn2_norm, v_ffn2_w_gate, v_ffn2_w_up, v_ffn2_w_down):
    w_all = dict(ffn1_norm=ffn1_norm, ffn1_w_gate=ffn1_w_gate, ffn1_w_up=ffn1_w_up, ffn1_w_down=ffn1_w_down, mix_norm=mix_norm, w_in=w_in, q_a_norm=q_a_norm, w_q_b=w_q_b, kv_a_norm=kv_a_norm, w_kv_b=w_kv_b, q_head_norm=q_head_norm, k_head_norm=k_head_norm, conv_w=conv_w, conv_b=conv_b, a_log_fwd=a_log_fwd, a_log_bwd=a_log_bwd, dt_bias_fwd=dt_bias_fwd, dt_bias_bwd=dt_bias_bwd, d_skip=d_skip, ssm_norm=ssm_norm, w_attn_branch=w_attn_branch, w_ssm_branch=w_ssm_branch, w_out=w_out, ffn2_norm=ffn2_norm, ffn2_w_gate=ffn2_w_gate, ffn2_w_up=ffn2_w_up, ffn2_w_down=ffn2_w_down)
    m_all = dict(ffn1_norm=m_ffn1_norm, ffn1_w_gate=m_ffn1_w_gate, ffn1_w_up=m_ffn1_w_up, ffn1_w_down=m_ffn1_w_down, mix_norm=m_mix_norm, w_in=m_w_in, q_a_norm=m_q_a_norm, w_q_b=m_w_q_b, kv_a_norm=m_kv_a_norm, w_kv_b=m_w_kv_b, q_head_norm=m_q_head_norm, k_head_norm=m_k_head_norm, conv_w=m_conv_w, conv_b=m_conv_b, a_log_fwd=m_a_log_fwd, a_log_bwd=m_a_log_bwd, dt_bias_fwd=m_dt_bias_fwd, dt_bias_bwd=m_dt_bias_bwd, d_skip=m_d_skip, ssm_norm=m_ssm_norm, w_attn_branch=m_w_attn_branch, w_ssm_branch=m_w_ssm_branch, w_out=m_w_out, ffn2_norm=m_ffn2_norm, ffn2_w_gate=m_ffn2_w_gate, ffn2_w_up=m_ffn2_w_up, ffn2_w_down=m_ffn2_w_down)
    v_all = dict(ffn1_norm=v_ffn1_norm, ffn1_w_gate=v_ffn1_w_gate, ffn1_w_up=v_ffn1_w_up, ffn1_w_down=v_ffn1_w_down, mix_norm=v_mix_norm, w_in=v_w_in, q_a_norm=v_q_a_norm, w_q_b=v_w_q_b, kv_a_norm=v_kv_a_norm, w_kv_b=v_w_kv_b, q_head_norm=v_q_head_norm, k_head_norm=v_k_head_norm, conv_w=v_conv_w, conv_b=v_conv_b, a_log_fwd=v_a_log_fwd, a_log_bwd=v_a_log_bwd, dt_bias_fwd=v_dt_bias_fwd, dt_bias_bwd=v_dt_bias_bwd, d_skip=v_d_skip, ssm_norm=v_ssm_norm, w_attn_branch=v_w_attn_branch, w_ssm_branch=v_w_ssm_branch, w_out=v_w_out, ffn2_norm=v_ffn2_norm, ffn2_w_gate=v_ffn2_w_gate, ffn2_w_up=v_ffn2_w_up, ffn2_w_down=v_ffn2_w_down)
    T = x.shape[1]
    shard_names = [n for n, _, _ in SHARDED]
    small_names = [n for n, _ in SMALL]
    shard_shapes = {n: w_all[n].shape for n in shard_names}
    small_shapes = {n: w_all[n].shape for n in small_names}

    conv_hi, conv_lo = _split_bf16(w_all["conv_w"])
    send = [conv_hi if n == "conv_w" else w_all[n] for n in shard_names]
    slab = _pack_rows(send + [conv_lo], SLAB_ROWS, BF16)
    gathered = _all_gather(slab, name="gather_weights").reshape(N_DEV, SLAB_ROWS * SLAB_COLS)
    full = _unpack_full(gathered)
    lo = gathered[:, SHARD_ELEMS:SHARD_ELEMS + CONV_WIDTH * XBC_DIM // N_DEV]
    lo = lo.reshape(N_DEV, CONV_WIDTH, XBC_DIM // N_DEV).transpose(1, 0, 2).reshape(CONV_WIDTH, XBC_DIM)
    conv_full = full["conv_w"].astype(F32) + lo.astype(F32)
    wd = _derived_weights(full)
    p = {n: w_all[n].reshape(1, -1) for n in small_names}

    loss_part, grad_x, gw, gs = _block_step(x[0], positions, loss_target[0], p, wd, conv_full)
    loss = lax.psum(loss_part, ("x", "y", "c"))

    gslab = _pack_grads(gw)
    gsmall = _pack_rows([gs[n] for n in small_names], SMALL_ROWS, F32)
    recv, srecv = _grad_exchange(gslab, gsmall, name="grad_exchange")

    pack = lambda d, names, rows: _pack_rows([d[n] for n in names], rows, F32)
    g_sh, d_sh, m_sh, v_sh = _reduce_adamw(recv, pack(w_all, shard_names, SLAB_ROWS), pack(m_all, shard_names, SLAB_ROWS),
                                           pack(v_all, shard_names, SLAB_ROWS), name="adamw_shards", tile=SLAB_ROW_TILE)
    g_sm, d_sm, m_sm, v_sm = _reduce_adamw(srecv, pack(w_all, small_names, SMALL_ROWS), pack(m_all, small_names, SMALL_ROWS),
                                           pack(v_all, small_names, SMALL_ROWS), name="adamw_small", tile=SMALL_ROWS)
    outs = []
    for sh, sm in ((g_sh, g_sm), (d_sh, d_sm), (m_sh, m_sm), (v_sh, v_sm)):
        d = {**_unpack_shards(sh, shard_shapes), **_unpack_small(sm, small_shapes)}
        outs.append([d[n] for n in WEIGHT_ORDER])
    return (loss, grad_x[None], *outs[0], *outs[1], *outs[2], *outs[3])
```

```python
import math

import jax
import jax.numpy as jnp
from jax import lax
from jax.experimental import pallas as pl
from jax.experimental.pallas import tpu as pltpu

F32, BF16 = jnp.float32, jnp.bfloat16
HIGHEST = lax.Precision.HIGHEST

D_MODEL, D_FF = 1024, 2816
EPS = 1e-6
N_HEADS, QK_NOPE, QK_ROPE, QK_HEAD, V_HEAD = 16, 64, 32, 96, 64
Q_LORA, KV_LORA = 384, 256
ROPE_BASE = 10000.0
D_INNER, SSM_HEADS, SSM_GROUPS, D_STATE, CONV_WIDTH, CHUNK = 2048, 32, 4, 128, 5, 128
XBC_DIM = D_INNER + 2 * SSM_GROUPS * D_STATE
IN_DIM = 7904
ADAM_LR, ADAM_B1, ADAM_B2, ADAM_EPS, ADAM_WD, ADAM_STEP = 0.001, 0.9, 0.999, 1e-08, 0.01, 10
N_DEV = 8

V7X_VMEM_BYTES = 64 * 1024 * 1024
VMEM_LIMIT = V7X_VMEM_BYTES - 8 * 1024 * 1024
LANES = 128
W_IN_SHARD = IN_DIM // N_DEV
W_IN_SHARD_PAD = 992

SMALL = (
    ("ffn1_norm", 1024), ("mix_norm", 1024), ("q_a_norm", 384), ("kv_a_norm", 256), ("q_head_norm", 96),
    ("k_head_norm", 96), ("conv_b", 3072), ("a_log_fwd", 32), ("a_log_bwd", 32), ("dt_bias_fwd", 32),
    ("dt_bias_bwd", 32), ("d_skip", 32), ("ssm_norm", 2048), ("ffn2_norm", 1024),
)
SMALL_ROW = {n: i for i, (n, _) in enumerate(SMALL)}
CONV_ROW = len(SMALL)
SMALL_ROWS, SMALL_COLS = 24, XBC_DIM
WEIGHT_ORDER = (
    "ffn1_norm", "ffn1_w_gate", "ffn1_w_up", "ffn1_w_down", "mix_norm", "w_in", "q_a_norm", "w_q_b", "kv_a_norm",
    "w_kv_b", "q_head_norm", "k_head_norm", "conv_w", "conv_b", "a_log_fwd", "a_log_bwd", "dt_bias_fwd", "dt_bias_bwd",
    "d_skip", "ssm_norm", "w_attn_branch", "w_ssm_branch", "w_out", "ffn2_norm", "ffn2_w_gate", "ffn2_w_up",
    "ffn2_w_down",
)


def _pallas(body, **kw):
    return pl.pallas_call(body, **kw)


def _params(sem):
    return pltpu.CompilerParams(dimension_semantics=sem, vmem_limit_bytes=VMEM_LIMIT)


def _pick(dim, pref):
    if dim <= pref:
        return dim
    c = (pref // LANES) * LANES
    while c >= LANES:
        if dim % c == 0:
            return c
        c -= LANES
    raise ValueError((dim, pref))


def _sigmoid(x):
    return 1.0 / (1.0 + jnp.exp(-x))


def _softplus(x):
    return jnp.maximum(x, 0.0) + jnp.log(1.0 + jnp.exp(-jnp.abs(x)))


def _dot(a, b):
    return jnp.dot(a, b, preferred_element_type=F32)


def _dot_nt(a, b):
    return lax.dot_general(a, b, (((1,), (1,)), ((), ())), preferred_element_type=F32)


def _dot_tn(a, b):
    return lax.dot_general(a, b, (((0,), (0,)), ((), ())), preferred_element_type=F32)


def _dot_h(a, b):
    return jnp.dot(a, b, preferred_element_type=F32, precision=HIGHEST)


def _dot_h_nt(a, b):
    return lax.dot_general(a, b, (((1,), (1,)), ((), ())), preferred_element_type=F32, precision=HIGHEST)


def _dot_h_tn(a, b):
    return lax.dot_general(a, b, (((0,), (0,)), ((), ())), preferred_element_type=F32, precision=HIGHEST)


def _mesh_pos():
    return lax.axis_index("x"), lax.axis_index("y"), lax.axis_index("c")


def _comm_scratch(n):
    return [pltpu.SemaphoreType.DMA((7 * n,)), pltpu.SemaphoreType.DMA((7 * n,)), pltpu.SemaphoreType.DMA((n,))]


def _comm_copies(modes, srcs, dsts, send_sems, recv_sems, local_sems):
    x, y, c = _mesh_pos()
    me = 4 * x + 2 * y + c
    local, sends, recvs = [], [], []
    for w, (mode, s, d) in enumerate(zip(modes, srcs, dsts)):
        gather = mode == "gather"
        local.append(pltpu.make_async_copy(s if gather else s.at[me], d.at[me], local_sems.at[w]))
        for k in range(1, N_DEV):
            px = (1 - x) if (k & 4) else x
            py = (1 - y) if (k & 2) else y
            pc = (1 - c) if (k & 1) else c
            peer = 4 * px + 2 * py + pc
            idx = 7 * w + k - 1
            src = s if gather else s.at[peer]
            for dst, out in ((d.at[me], sends), (d.at[peer], recvs)):
                out.append(pltpu.make_async_remote_copy(
                    src_ref=src, dst_ref=dst, send_sem=send_sems.at[idx], recv_sem=recv_sems.at[idx],
                    device_id=(px, py, pc), device_id_type=pl.DeviceIdType.MESH))
    return local, sends, recvs


def _comm_start(modes, srcs, dsts, sems):
    local, sends, _ = _comm_copies(modes, srcs, dsts, *sems)
    for cp in local + sends:
        cp.start()


def _comm_wait(modes, srcs, dsts, sems):
    local, sends, recvs = _comm_copies(modes, srcs, dsts, *sems)
    for cp in recvs:
        cp.wait_recv()
    for cp in sends:
        cp.wait_send()
    for cp in local:
        cp.wait()


def _comm_out_shapes(modes, arrays):
    return [jax.ShapeDtypeStruct((N_DEV,) + (a.shape if m == "gather" else a.shape[1:]), a.dtype)
            for m, a in zip(modes, arrays)]


def _exchange(modes, arrays, *, name):
    n = len(arrays)

    def body(*refs):
        srcs, dsts, sems = refs[:n], refs[n:2 * n], refs[2 * n:]
        _comm_start(modes, srcs, dsts, sems)
        _comm_wait(modes, srcs, dsts, sems)

    any_spec = pl.BlockSpec(memory_space=pl.ANY)
    return _pallas(body, name=name, out_shape=_comm_out_shapes(modes, arrays), in_specs=[any_spec] * n,
                   out_specs=[any_spec] * n, scratch_shapes=_comm_scratch(n))(*arrays)


def _all_gather_two_level(shards, *, name):
    n = len(shards)

    def body(*refs):
        srcs, outs = refs[:n], refs[n:2 * n]
        send_sems, recv_sems, local_sems = refs[2 * n:]
        x, y, c = _mesh_pos()
        me, sibling = (x, y, c), (x, y, 1 - c)
        chips = [(1 - x, y), (x, 1 - y), (1 - x, 1 - y)]

        def blk(w, px, py, pc):
            return outs[w].at[4 * px + 2 * py + pc]

        def copy(w, k, block, to, src=None):
            return pltpu.make_async_remote_copy(
                src_ref=blk(w, *block) if src is None else src, dst_ref=blk(w, *block),
                send_sem=send_sems.at[7 * w + k], recv_sem=recv_sems.at[7 * w + k], device_id=to,
                device_id_type=pl.DeviceIdType.MESH)

        mine = [pltpu.make_async_copy(srcs[w], blk(w, *me), local_sems.at[w]) for w in range(n)]
        for cp in mine:
            cp.start()
        first = []
        for w in range(n):
            first.append(copy(w, 0, me, sibling, src=srcs[w]))
            first += [copy(w, 1 + j, me, (*chip, c), src=srcs[w]) for j, chip in enumerate(chips)]
        for cp in first:
            cp.start()
        passed = []
        for w in range(n):
            for j, chip in enumerate(chips):
                copy(w, 1 + j, (*chip, c), me).wait_recv()
                fwd = copy(w, 4 + j, (*chip, c), sibling)
                fwd.start()
                passed.append(fwd)
        for w in range(n):
            copy(w, 0, sibling, me).wait_recv()
            for j, chip in enumerate(chips):
                copy(w, 4 + j, (*chip, 1 - c), me).wait_recv()
        for cp in first + passed:
            cp.wait_send()
        for cp in mine:
            cp.wait()

    any_spec = pl.BlockSpec(memory_space=pl.ANY)
    return _pallas(body, name=name, out_shape=_comm_out_shapes(["gather"] * n, shards), in_specs=[any_spec] * n,
                   out_specs=[any_spec] * n, scratch_shapes=_comm_scratch(n))(*shards)


def _mm(a, b, *, name, ta=False, tb=False, out_dtype=F32, alpha=1.0, res=None, tm=1024, tn=1408, tk=1024):
    (K, M) = a.shape if ta else a.shape[::-1]
    (N, Kb) = b.shape if tb else b.shape[::-1]
    assert K == Kb, (a.shape, b.shape, ta, tb)
    tm, tn, tk = _pick(M, tm), _pick(N, tn), _pick(K, tk)
    nk = K // tk
    a_spec = pl.BlockSpec((tk, tm), lambda i, j, k: (k, i)) if ta else pl.BlockSpec((tm, tk), lambda i, j, k: (i, k))
    b_spec = pl.BlockSpec((tn, tk), lambda i, j, k: (j, k)) if tb else pl.BlockSpec((tk, tn), lambda i, j, k: (k, j))
    o_spec = pl.BlockSpec((tm, tn), lambda i, j, k: (i, j))
    dn = (((0 if ta else 1,), (1 if tb else 0,)), ((), ()))
    has_res = res is not None

    def body(*refs):
        a_ref, b_ref = refs[0], refs[1]
        r_ref = refs[2] if has_res else None
        o_ref = refs[3] if has_res else refs[2]
        part = lax.dot_general(a_ref[...].astype(BF16), b_ref[...].astype(BF16), dn, preferred_element_type=F32)

        def finish(acc):
            if alpha != 1.0:
                acc = acc * alpha
            if has_res:
                acc = acc + r_ref[...]
            o_ref[...] = acc.astype(o_ref.dtype)

        if nk == 1:
            finish(part)
        else:
            acc_ref = refs[-1]
            k = pl.program_id(2)

            @pl.when(k == 0)
            def _():
                acc_ref[...] = part

            @pl.when(k > 0)
            def _():
                acc_ref[...] += part

            @pl.when(k == nk - 1)
            def _():
                finish(acc_ref[...])

    ins = [a, b] + ([res] if has_res else [])
    in_specs = [a_spec, b_spec] + ([o_spec] if has_res else [])
    return _pallas(
        body, name=name, grid=(M // tm, N // tn, nk), in_specs=in_specs, out_specs=o_spec,
        out_shape=jax.ShapeDtypeStruct((M, N), out_dtype),
        scratch_shapes=[pltpu.VMEM((tm, tn), F32)] if nk > 1 else [],
        compiler_params=_params(("parallel", "parallel", "arbitrary")),
    )(*ins)


def _col0(j):
    return 0


def _colj(j):
    return j


def _rowmap(fn, *, name, rows, tile, ins, consts=(), outs=(), accs=(), ncol=1):
    tile = min(tile, rows)
    nrow = rows // tile
    in_specs = [pl.BlockSpec((tile, w), lambda j, i, f=f: (i, f(j))) for _, w, f in ins]
    for arr, w, f in consts:
        in_specs.append(pl.BlockSpec((arr.shape[0], w), lambda j, i, f=f: (0, f(j))))
    out_specs = [pl.BlockSpec((tile, w), lambda j, i, f=f: (i, f(j))) for _, _, w, f in outs]
    out_specs += [pl.BlockSpec((1, w), lambda j, i, f=f: (0, f(j))) for _, w, f in accs]
    out_shape = [jax.ShapeDtypeStruct((rows, c), dt) for c, dt, _, _ in outs]
    out_shape += [jax.ShapeDtypeStruct((1, c), F32) for c, _, _ in accs]
    n_in, n_out = len(ins) + len(consts), len(outs)
    acc_fixed = [f is _col0 for _, _, f in accs]

    def body(*refs):
        res = fn(*[r[...] for r in refs[:n_in]])
        if not isinstance(res, (tuple, list)):
            res = (res,)
        for r, v in zip(refs[n_in:n_in + n_out], res[:n_out]):
            r[...] = v.astype(r.dtype)
        j, i = pl.program_id(0), pl.program_id(1)
        for r, v, fixed in zip(refs[n_in + n_out:], res[n_out:], acc_fixed):
            first = ((i == 0) & (j == 0)) if fixed else (i == 0)

            @pl.when(first)
            def _(r=r, v=v):
                r[...] = v

            @pl.when(jnp.logical_not(first))
            def _(r=r, v=v):
                r[...] += v

    arrays = [a for a, _, _ in ins] + [a for a, _, _ in consts]
    return _pallas(
        body, name=name, grid=(ncol, nrow), in_specs=in_specs, out_specs=out_specs, out_shape=out_shape,
        compiler_params=_params(("arbitrary", "arbitrary")),
    )(*arrays)


def _rms_fwd(x, g, *, name, tile=512):
    rows, d = x.shape

    def fn(xv, gv):
        r = lax.rsqrt(jnp.mean(xv * xv, axis=-1, keepdims=True) + EPS)
        return xv * r * gv

    return _rowmap(fn, name=name, rows=rows, tile=tile, ins=[(x, d, _col0)], consts=[(g, d, _col0)],
                   outs=[(d, BF16, d, _col0)])[0]


def _rms_bwd(dh, x, g, res, *, name, tile=512):
    rows, d = x.shape

    def fn(dhv, xv, rv, gv):
        r = lax.rsqrt(jnp.mean(xv * xv, axis=-1, keepdims=True) + EPS)
        xh = xv * r
        dxh = dhv * gv
        dx = r * (dxh - xh * jnp.mean(dxh * xh, axis=-1, keepdims=True))
        return rv + dx, jnp.sum(dhv * xh, axis=0, keepdims=True)

    return _rowmap(fn, name=name, rows=rows, tile=tile, ins=[(dh, d, _col0), (x, d, _col0), (res, d, _col0)],
                   consts=[(g, d, _col0)], outs=[(d, F32, d, _col0)], accs=[(d, d, _col0)])


def _swiglu_fwd(gu, *, name, tile=512):
    rows = gu.shape[0]
    w = _pick(D_FF, 1408)
    nb = D_FF // w

    def fn(gv, uv):
        return gv * _sigmoid(gv) * uv

    return _rowmap(fn, name=name, rows=rows, tile=tile, ncol=nb,
                   ins=[(gu, w, _colj), (gu, w, lambda j: j + nb)], outs=[(D_FF, BF16, w, _colj)])[0]


def _swiglu_bwd(da, gu, *, name, tile=512):
    rows = gu.shape[0]
    w = _pick(D_FF, 1408)
    nb = D_FF // w

    def fn(dav, gv, uv):
        sg = _sigmoid(gv)
        dg = dav * uv * (sg * (1.0 + gv * (1.0 - sg)))
        du = dav * (gv * sg)
        return dg, du

    return _rowmap(fn, name=name, rows=rows, tile=tile, ncol=nb,
                   ins=[(da, w, _colj), (gu, w, _colj), (gu, w, lambda j: j + nb)],
                   outs=[(D_FF, BF16, w, _colj), (D_FF, BF16, w, _colj)])


U_CKV, U_KPE, U_DT = 512, 768, 896


def _lora_norm_fwd(u_small, qg, kvg, *, name, tile=512):
    rows = u_small.shape[0]

    def fn(cq, ckv, qgv, kgv):
        rq = lax.rsqrt(jnp.mean(cq * cq, axis=-1, keepdims=True) + EPS)
        rk = lax.rsqrt(jnp.mean(ckv * ckv, axis=-1, keepdims=True) + EPS)
        return cq * rq * qgv, ckv * rk * kgv

    return _rowmap(fn, name=name, rows=rows, tile=tile,
                   ins=[(u_small, Q_LORA, _col0), (u_small, KV_LORA, lambda j: U_CKV // KV_LORA)],
                   consts=[(qg, Q_LORA, _col0), (kvg, KV_LORA, _col0)],
                   outs=[(Q_LORA, BF16, Q_LORA, _col0), (KV_LORA, BF16, KV_LORA, _col0)])


def _lora_norm_bwd(dcqn, dckvn, u_small, qg, kvg, dkpe, draw_f, draw_b, *, name, tile=512):
    rows = u_small.shape[0]
    tile = min(tile, rows)

    def body(dq_ref, dk_ref, u_ref, dkp_ref, df_ref, db_ref, qg_ref, kg_ref, du_ref, gq_ref, gk_ref):
        cq, ckv = u_ref[:, 0:Q_LORA], u_ref[:, U_CKV:U_CKV + KV_LORA]
        dq, dk = dq_ref[...], dk_ref[...]
        rq = lax.rsqrt(jnp.mean(cq * cq, axis=-1, keepdims=True) + EPS)
        xh = cq * rq
        dxh = dq * qg_ref[...]
        du_ref[:, 0:Q_LORA] = (rq * (dxh - xh * jnp.mean(dxh * xh, axis=-1, keepdims=True))).astype(BF16)
        du_ref[:, Q_LORA:U_CKV] = jnp.zeros((tile, U_CKV - Q_LORA), BF16)
        rk = lax.rsqrt(jnp.mean(ckv * ckv, axis=-1, keepdims=True) + EPS)
        kh = ckv * rk
        dkh = dk * kg_ref[...]
        du_ref[:, U_CKV:U_KPE] = (rk * (dkh - kh * jnp.mean(dkh * kh, axis=-1, keepdims=True))).astype(BF16)
        du_ref[:, U_KPE:U_DT] = dkp_ref[...].astype(BF16)
        du_ref[:, U_DT:U_DT + LANES] = (df_ref[...] + db_ref[...]).astype(BF16)
        gq = jnp.sum(dq * xh, axis=0, keepdims=True)
        gk = jnp.sum(dk * kh, axis=0, keepdims=True)
        i = pl.program_id(0)

        @pl.when(i == 0)
        def _():
            gq_ref[...] = gq
            gk_ref[...] = gk

        @pl.when(i > 0)
        def _():
            gq_ref[...] += gq
            gk_ref[...] += gk

    def rowblk(w):
        return pl.BlockSpec((tile, w), lambda i: (i, 0))

    def whole(w):
        return pl.BlockSpec((1, w), lambda i: (0, 0))

    return _pallas(
        body, name=name, grid=(rows // tile,),
        in_specs=[rowblk(Q_LORA), rowblk(KV_LORA), rowblk(1024), rowblk(LANES), rowblk(LANES), rowblk(LANES),
                  whole(Q_LORA), whole(KV_LORA)],
        out_specs=[rowblk(1024), whole(Q_LORA), whole(KV_LORA)],
        out_shape=[jax.ShapeDtypeStruct((rows, 1024), BF16), jax.ShapeDtypeStruct((1, Q_LORA), F32),
                   jax.ShapeDtypeStruct((1, KV_LORA), F32)],
        compiler_params=_params(("arbitrary",)),
    )(dcqn, dckvn, u_small, dkpe, draw_f, draw_b, qg, kvg)


def _rope(x, c, s1, s2):
    return x * c + pltpu.roll(x, 112, 1) * s1 + pltpu.roll(x, 16, 1) * s2


def _rope_t(d, c, s1, s2):
    return d * c + pltpu.roll(d * s1, 16, 1) + pltpu.roll(d * s2, 112, 1)


def _qk_prep_fwd(q_raw, k_raw, v_raw, u_small, tabs, qg, kg, *, name, tile=256):
    rows = q_raw.shape[0]
    tile = min(tile, rows)
    scale = 1.0 / math.sqrt(QK_HEAD)

    def body(q_ref, k_ref, v_ref, u_ref, c_ref, s1_ref, s2_ref, qg_ref, kg_ref, qo_ref, ko_ref, vo_ref):
        c, s1, s2 = c_ref[...], s1_ref[...], s2_ref[...]
        qgv, kgv = qg_ref[...], kg_ref[...]
        kpe = pltpu.roll(u_ref[:, U_KPE:U_KPE + LANES], 64, 1)
        vo_ref[...] = v_ref[...].astype(BF16)
        for h in range(N_HEADS):
            hs = slice(h * LANES, (h + 1) * LANES)
            qr = q_ref[:, hs]
            rq = lax.rsqrt(jnp.sum(qr * qr, axis=-1, keepdims=True) / QK_HEAD + EPS)
            qo_ref[:, hs] = (_rope(qr * rq * qgv, c, s1, s2) * scale).astype(BF16)
            xk = k_ref[:, hs] + kpe
            rk = lax.rsqrt(jnp.sum(xk * xk, axis=-1, keepdims=True) / QK_HEAD + EPS)
            ko_ref[:, hs] = _rope(xk * rk * kgv, c, s1, s2).astype(BF16)

    wide = pl.BlockSpec((tile, 2048), lambda i: (i, 0))
    narrow = pl.BlockSpec((tile, LANES), lambda i: (i, 0))
    gain = pl.BlockSpec((1, LANES), lambda i: (0, 0))
    return _pallas(
        body, name=name, grid=(rows // tile,),
        in_specs=[wide, wide, wide, pl.BlockSpec((tile, 1024), lambda i: (i, 0)), narrow, narrow, narrow, gain, gain],
        out_specs=[wide, wide, wide], out_shape=[jax.ShapeDtypeStruct((rows, 2048), BF16)] * 3,
        compiler_params=_params(("parallel",)),
    )(q_raw, k_raw, v_raw, u_small, *tabs, qg, kg)


def _qk_prep_bwd(dq, dk, dv, q_raw, k_raw, u_small, tabs, qg, kg, *, name, tile=256):
    rows = q_raw.shape[0]
    tile = min(tile, rows)
    scale = 1.0 / math.sqrt(QK_HEAD)

    def body(dq_ref, dk_ref, dv_ref, q_ref, k_ref, u_ref, c_ref, s1_ref, s2_ref, qg_ref, kg_ref,
             dqo_ref, dko_ref, dvo_ref, dkpe_ref, gq_ref, gk_ref):
        c, s1, s2 = c_ref[...], s1_ref[...], s2_ref[...]
        qgv, kgv = qg_ref[...], kg_ref[...]
        kpe = pltpu.roll(u_ref[:, U_KPE:U_KPE + LANES], 64, 1)
        lane = lax.broadcasted_iota(jnp.int32, (tile, LANES), 1)
        dvo_ref[...] = dv_ref[...].astype(BF16)
        gq = jnp.zeros((1, LANES), F32)
        gk = jnp.zeros((1, LANES), F32)
        dkpe = jnp.zeros((tile, LANES), F32)
        for h in range(N_HEADS):
            hs = slice(h * LANES, (h + 1) * LANES)
            qr = q_ref[:, hs]
            rq = lax.rsqrt(jnp.sum(qr * qr, axis=-1, keepdims=True) / QK_HEAD + EPS)
            xh = qr * rq
            dy = _rope_t(dq_ref[:, hs] * scale, c, s1, s2)
            dxh = dy * qgv
            dqo_ref[:, hs] = (rq * (dxh - xh * (jnp.sum(dxh * xh, axis=-1, keepdims=True) / QK_HEAD))).astype(BF16)
            gq = gq + jnp.sum(dy * xh, axis=0, keepdims=True)
            xk = k_ref[:, hs] + kpe
            rk = lax.rsqrt(jnp.sum(xk * xk, axis=-1, keepdims=True) / QK_HEAD + EPS)
            kh = xk * rk
            dyk = _rope_t(dk_ref[:, hs], c, s1, s2)
            dkh = dyk * kgv
            dxk = rk * (dkh - kh * (jnp.sum(dkh * kh, axis=-1, keepdims=True) / QK_HEAD))
            gk = gk + jnp.sum(dyk * kh, axis=0, keepdims=True)
            dko_ref[:, hs] = jnp.where(lane < QK_NOPE, dxk, 0.0).astype(BF16)
            dkpe = dkpe + dxk
        dkpe_ref[...] = jnp.where(lane < QK_ROPE, pltpu.roll(dkpe, 64, 1), 0.0)
        i = pl.program_id(0)

        @pl.when(i == 0)
        def _():
            gq_ref[...] = gq
            gk_ref[...] = gk

        @pl.when(i > 0)
        def _():
            gq_ref[...] += gq
            gk_ref[...] += gk

    wide = pl.BlockSpec((tile, 2048), lambda i: (i, 0))
    narrow = pl.BlockSpec((tile, LANES), lambda i: (i, 0))
    gain = pl.BlockSpec((1, LANES), lambda i: (0, 0))
    return _pallas(
        body, name=name, grid=(rows // tile,),
        in_specs=[wide, wide, wide, wide, wide, pl.BlockSpec((tile, 1024), lambda i: (i, 0)), narrow, narrow, narrow,
                  gain, gain],
        out_specs=[wide, wide, wide, narrow, gain, gain],
        out_shape=[jax.ShapeDtypeStruct((rows, 2048), BF16)] * 3
        + [jax.ShapeDtypeStruct((rows, LANES), F32), jax.ShapeDtypeStruct((1, LANES), F32),
           jax.ShapeDtypeStruct((1, LANES), F32)],
        compiler_params=_params(("arbitrary",)),
    )(dq, dk, dv, q_raw, k_raw, u_small, *tabs, qg, kg)


def _attn_fwd(q, k, v, comm_modes, comm_arrays, *, name, tq=512):
    T = q.shape[0]
    tq = min(tq, T)
    n = len(comm_arrays)
    nj, ni = N_HEADS // 2, T // tq

    def body(*refs):
        q_ref, k_ref, v_ref = refs[:3]
        srcs = refs[3:3 + n]
        o_ref, lse_ref = refs[3 + n:5 + n]
        dsts = refs[5 + n:5 + 2 * n]
        sems = refs[5 + 2 * n:]
        j, i = pl.program_id(0), pl.program_id(1)

        @pl.when((j == 0) & (i == 0))
        def _():
            _comm_start(comm_modes, srcs, dsts, sems)

        out = None
        for hh in range(2):
            sl = slice(hh * LANES, (hh + 1) * LANES)
            s = _dot_nt(q_ref[:, sl], k_ref[:, sl])
            m = jnp.max(s, axis=-1, keepdims=True)
            p = jnp.exp(s - m)
            l = jnp.sum(p, axis=-1, keepdims=True)
            o = _dot(p.astype(BF16), v_ref[:, sl]) / l
            out = o if out is None else out + o
            lse_ref[hh] = m + jnp.log(l)
        o_ref[...] = out

        @pl.when((j == nj - 1) & (i == ni - 1))
        def _():
            _comm_wait(comm_modes, srcs, dsts, sems)

    any_spec = pl.BlockSpec(memory_space=pl.ANY)
    got = _pallas(
        body, name=name, grid=(nj, ni),
        in_specs=[pl.BlockSpec((tq, 2 * LANES), lambda j, i: (i, j)), pl.BlockSpec((T, 2 * LANES), lambda j, i: (0, j)),
                  pl.BlockSpec((T, 2 * LANES), lambda j, i: (0, j))] + [any_spec] * n,
        out_specs=[pl.BlockSpec((tq, LANES), lambda j, i: (i, j)), pl.BlockSpec((2, tq, 1), lambda j, i: (j, i, 0))]
        + [any_spec] * n,
        out_shape=[jax.ShapeDtypeStruct((T, N_HEADS * V_HEAD), F32), jax.ShapeDtypeStruct((N_HEADS, T, 1), F32)]
        + _comm_out_shapes(comm_modes, comm_arrays),
        scratch_shapes=_comm_scratch(n),
        compiler_params=_params(("arbitrary", "arbitrary")),
    )(q, k, v, *comm_arrays)
    return got[0], got[1], got[2:]


def _attn_bwd(q, k, v, o, lse, do, comm_modes, comm_arrays, *, name, tk=256):
    T = q.shape[0]
    tk = min(tk, T)
    n = len(comm_arrays)
    nj, nkb = N_HEADS // 2, T // tk

    def body(*refs):
        q_ref, k_ref, v_ref, o_ref, lse_ref, do_ref = refs[:6]
        srcs = refs[6:6 + n]
        dq_ref, dk_ref, dv_ref = refs[6 + n:9 + n]
        dsts = refs[9 + n:9 + 2 * n]
        d_s = refs[9 + 2 * n]
        sems = refs[10 + 2 * n:]
        j, kb = pl.program_id(0), pl.program_id(1)

        @pl.when((j == 0) & (kb == 0))
        def _():
            _comm_start(comm_modes, srcs, dsts, sems)

        lane = lax.broadcasted_iota(jnp.int32, (1, LANES), 1)
        dov = do_ref[...]

        @pl.when(kb == 0)
        def _():
            prod = dov * o_ref[...]
            for hh in range(2):
                keep = (lane < V_HEAD) if hh == 0 else (lane >= V_HEAD)
                d_s[hh] = jnp.sum(jnp.where(keep, prod, 0.0), axis=-1, keepdims=True)

        do_b = dov.astype(BF16)
        for hh in range(2):
            sl = slice(hh * LANES, (hh + 1) * LANES)
            keep = (lane < V_HEAD) if hh == 0 else (lane >= V_HEAD)
            qv, kv, vv = q_ref[:, sl], k_ref[:, sl], v_ref[:, sl]
            s = _dot_nt(qv, kv)
            p = jnp.exp(s - lse_ref[hh])
            dp = _dot_nt(do_b, vv)
            ds = (p * (dp - d_s[hh])).astype(BF16)
            dv_ref[:, sl] = jnp.where(keep, _dot_tn(p.astype(BF16), do_b), 0.0)
            dk_ref[:, sl] = _dot_tn(ds, qv)
            dqp = _dot(ds, kv)

            @pl.when(kb == 0)
            def _(dqp=dqp, sl=sl):
                dq_ref[:, sl] = dqp

            @pl.when(kb > 0)
            def _(dqp=dqp, sl=sl):
                dq_ref[:, sl] += dqp

        @pl.when((j == nj - 1) & (kb == nkb - 1))
        def _():
            _comm_wait(comm_modes, srcs, dsts, sems)

    any_spec = pl.BlockSpec(memory_space=pl.ANY)
    pair = pl.BlockSpec((T, 2 * LANES), lambda j, kb: (0, j))
    kblk = pl.BlockSpec((tk, 2 * LANES), lambda j, kb: (kb, j))
    got = _pallas(
        body, name=name, grid=(nj, nkb),
        in_specs=[pair, kblk, kblk, pl.BlockSpec((T, LANES), lambda j, kb: (0, j)),
                  pl.BlockSpec((2, T, 1), lambda j, kb: (j, 0, 0)), pl.BlockSpec((T, LANES), lambda j, kb: (0, j))]
        + [any_spec] * n,
        out_specs=[pair, kblk, kblk] + [any_spec] * n,
        out_shape=[jax.ShapeDtypeStruct((T, 2048), F32)] * 3 + _comm_out_shapes(comm_modes, comm_arrays),
        scratch_shapes=[pltpu.VMEM((2, T, 1), F32)] + _comm_scratch(n),
        compiler_params=_params(("arbitrary", "arbitrary")),
    )(q, k, v, o, lse, do, *comm_arrays)
    return got[0], got[1], got[2], got[3:]


def _conv_shift(x, sh, t_idx):
    if sh == 0:
        return x
    T = x.shape[0]
    y = pltpu.roll(x, (-sh) % T, 0)
    ok = (t_idx + sh >= 0) & (t_idx + sh < T)
    return jnp.where(ok, y, 0.0)


def _conv_fwd(u_big, conv_w, conv_b, *, name, w=256):
    T = u_big.shape[0]
    first = D_INNER // w

    def body(x_ref, w_ref, b_ref, o_ref):
        x = x_ref[...]
        t_idx = lax.broadcasted_iota(jnp.int32, x.shape, 0)
        acc = b_ref[...] + w_ref[2:3, :] * x
        for j in (0, 1, 3, 4):
            acc = acc + w_ref[j:j + 1, :] * _conv_shift(x, j - 2, t_idx)
        o_ref[...] = acc * _sigmoid(acc)

    return _pallas(
        body, name=name, grid=(XBC_DIM // w,),
        in_specs=[pl.BlockSpec((T, w), lambda j: (0, j + first)), pl.BlockSpec((CONV_WIDTH, w), lambda j: (0, j)),
                  pl.BlockSpec((1, w), lambda j: (0, j))],
        out_specs=pl.BlockSpec((T, w), lambda j: (0, j)),
        out_shape=jax.ShapeDtypeStruct((T, XBC_DIM), F32),
        compiler_params=_params(("parallel",)),
    )(u_big, conv_w, conv_b)


def _conv_bwd(dact_f, dact_b, u_big, conv_w, conv_b, *, name, w=128):
    T = u_big.shape[0]
    first = D_INNER // w

    def body(df_ref, db_ref, x_ref, w_ref, b_ref, dx_ref, dw_ref, dbias_ref):
        x = x_ref[...]
        t_idx = lax.broadcasted_iota(jnp.int32, x.shape, 0)
        pre = b_ref[...] + w_ref[2:3, :] * x
        for j in (0, 1, 3, 4):
            pre = pre + w_ref[j:j + 1, :] * _conv_shift(x, j - 2, t_idx)
        sg = _sigmoid(pre)
        dpre = (df_ref[...] + db_ref[...]) * (sg * (1.0 + pre * (1.0 - sg)))
        dbias_ref[...] = jnp.sum(dpre, axis=0, keepdims=True)
        dx = w_ref[2:3, :] * dpre
        dw_ref[2:3, :] = jnp.sum(dpre * x, axis=0, keepdims=True)
        for j in (0, 1, 3, 4):
            dx = dx + w_ref[j:j + 1, :] * _conv_shift(dpre, 2 - j, t_idx)
            dw_ref[j:j + 1, :] = jnp.sum(dpre * _conv_shift(x, j - 2, t_idx), axis=0, keepdims=True)
        dx_ref[...] = dx.astype(dx_ref.dtype)

    blk = pl.BlockSpec((T, w), lambda j: (0, j))
    return _pallas(
        body, name=name, grid=(XBC_DIM // w,),
        in_specs=[blk, blk, pl.BlockSpec((T, w), lambda j: (0, j + first)),
                  pl.BlockSpec((CONV_WIDTH, w), lambda j: (0, j)), pl.BlockSpec((1, w), lambda j: (0, j))],
        out_specs=[blk, pl.BlockSpec((CONV_WIDTH, w), lambda j: (0, j)), pl.BlockSpec((1, w), lambda j: (0, j))],
        out_shape=[jax.ShapeDtypeStruct((T, XBC_DIM), BF16), jax.ShapeDtypeStruct((CONV_WIDTH, XBC_DIM), F32),
                   jax.ShapeDtypeStruct((1, XBC_DIM), F32)],
        compiler_params=_params(("parallel",)),
    )(dact_f, dact_b, u_big, conv_w, conv_b)


def _ssd_expand(rev):
    off = SSM_HEADS if rev else 0
    h = jnp.arange(LANES, dtype=jnp.int32)[:, None]
    return (jnp.arange(D_INNER, dtype=jnp.int32)[None, :] // 64 + off == h).astype(F32)


def _ssd_head_terms(dt_ref, bias_ref, alog_ref, acst_s, rev):
    L = CHUNK
    row = lax.broadcasted_iota(jnp.int32, (L, L), 0)
    col = lax.broadcasted_iota(jnp.int32, (L, L), 1)
    mask = (row <= col) if rev else (row >= col)
    cm = mask.astype(F32)
    cmt = ((row >= col) if rev else (row <= col)).astype(F32)
    pre = dt_ref[...] + bias_ref[...]
    dt = _softplus(pre)
    a = -jnp.exp(alog_ref[...])
    da = dt * a
    acs = _dot_h(cm, da)
    acst_s[...] = _dot_h_tn(da, cmt)
    tot = jnp.sum(da, axis=0, keepdims=True)
    return dict(mask=mask, cm=cm, cmt=cmt, pre=pre, dt=dt, a=a, da=da, acs=acs, tot=tot,
                e=jnp.exp(acs), w=jnp.exp(tot - acs), dec=jnp.exp(tot))


def _ssd_fwd(xbc_act, u_small, bias128, alog128, *, rev, name):
    T = xbc_act.shape[0]
    L = CHUNK
    nc = T // L
    e_mat = _ssd_expand(rev)
    off = SSM_HEADS if rev else 0

    def cidx(c):
        return (nc - 1 - c) if rev else c

    def body(xs_ref, bm_ref, cm_ref, dt_ref, bias_ref, alog_ref, e_ref, y_ref, hin_ref, ht_s, acst_s):
        c = pl.program_id(0)

        @pl.when(c == 0)
        def _():
            ht_s[...] = jnp.zeros_like(ht_s)

        t = _ssd_head_terms(dt_ref, bias_ref, alog_ref, acst_s, rev)
        lo = lax.broadcasted_iota(jnp.int32, (L, LANES), 1) < 64
        dec8 = jnp.broadcast_to(t["dec"], (8, LANES))
        for g in range(SSM_GROUPS):
            eg = e_ref[:, g * 512:(g + 1) * 512]
            dt_x, e_x, w_x = _dot_h(t["dt"], eg), _dot_h(t["e"], eg), _dot_h(t["w"], eg)
            dec_x = _dot_h(dec8, eg)[0:1, :]
            bmat = bm_ref[:, g * LANES:(g + 1) * LANES].astype(BF16)
            cmat = cm_ref[:, g * LANES:(g + 1) * LANES].astype(BF16)
            xdt = xs_ref[:, g * 512:(g + 1) * 512] * dt_x
            gmat = _dot_nt(cmat, bmat)
            ht = ht_s[g]
            yoff = e_x * _dot(cmat, ht.astype(BF16))
            for pr in range(4):
                xp = xdt[:, pr * LANES:(pr + 1) * LANES]
                acc = yoff[:, pr * LANES:(pr + 1) * LANES]
                for s_ in range(2):
                    h = off + 8 * g + 2 * pr + s_
                    seg = t["acs"][:, h:h + 1] - acst_s[h:h + 1, :]
                    lam = jnp.exp(jnp.where(t["mask"], seg, -1e30))
                    m = (gmat * lam).astype(BF16)
                    xm = jnp.where(lo if s_ == 0 else jnp.logical_not(lo), xp, 0.0).astype(BF16)
                    acc = acc + _dot(m, xm)
                y_ref[:, g * 512 + pr * LANES:g * 512 + (pr + 1) * LANES] = acc
            hin_ref[0, g] = ht
            ht_s[g] = ht * dec_x + _dot_tn(bmat, (w_x * xdt).astype(BF16))

    return _pallas(
        body, name=name, grid=(nc,),
        in_specs=[pl.BlockSpec((L, D_INNER), lambda c: (cidx(c), 0)), pl.BlockSpec((L, 512), lambda c: (cidx(c), 4)),
                  pl.BlockSpec((L, 512), lambda c: (cidx(c), 5)),
                  pl.BlockSpec((L, LANES), lambda c: (cidx(c), U_DT // LANES)),
                  pl.BlockSpec((1, LANES), lambda c: (0, 0)), pl.BlockSpec((1, LANES), lambda c: (0, 0)),
                  pl.BlockSpec((LANES, D_INNER), lambda c: (0, 0))],
        out_specs=[pl.BlockSpec((L, D_INNER), lambda c: (cidx(c), 0)),
                   pl.BlockSpec((1, SSM_GROUPS, D_STATE, 512), lambda c: (cidx(c), 0, 0, 0))],
        out_shape=[jax.ShapeDtypeStruct((T, D_INNER), F32), jax.ShapeDtypeStruct((nc, SSM_GROUPS, D_STATE, 512), F32)],
        scratch_shapes=[pltpu.VMEM((SSM_GROUPS, D_STATE, 512), F32), pltpu.VMEM((LANES, L), F32)],
        compiler_params=_params(("arbitrary",)),
    )(xbc_act, xbc_act, xbc_act, u_small, bias128, alog128, e_mat)


def _ssd_bwd(dy, xbc_act, u_small, bias128, alog128, hin, skip_x, *, rev, name):
    T = xbc_act.shape[0]
    L = CHUNK
    nc = T // L
    e_mat = _ssd_expand(rev)
    off = SSM_HEADS if rev else 0
    has_skip = skip_x is not None

    def cidx(c):
        return c if rev else (nc - 1 - c)

    def body(*refs):
        (dy_ref, xs_ref, bm_ref, cm_ref, dt_ref, bias_ref, alog_ref, hin_ref, e_ref) = refs[:9]
        k = 9
        skip_ref = refs[k] if has_skip else None
        k += 1 if has_skip else 0
        dx_ref, draw_ref, dalog_ref, dbias_ref, dht_s, acst_s, rowt_s, dxdt_s, te_s, tw_s = refs[k:]
        c = pl.program_id(0)

        @pl.when(c == 0)
        def _():
            dht_s[...] = jnp.zeros_like(dht_s)
            rowt_s[...] = jnp.zeros_like(rowt_s)

        t = _ssd_head_terms(dt_ref, bias_ref, alog_ref, acst_s, rev)
        lane1 = lax.broadcasted_iota(jnp.int32, (1, LANES), 1)
        lo = lax.broadcasted_iota(jnp.int32, (L, LANES), 1) < 64
        dec8 = jnp.broadcast_to(t["dec"], (8, LANES))
        colpart = jnp.zeros((L, LANES), F32)
        dtot_h = jnp.zeros((8, LANES), F32)
        for g in range(SSM_GROUPS):
            gs = slice(g * 512, (g + 1) * 512)
            eg = e_ref[:, gs]
            dt_x, e_x, w_x = _dot_h(t["dt"], eg), _dot_h(t["e"], eg), _dot_h(t["w"], eg)
            dec_x = _dot_h(dec8, eg)[0:1, :]
            bmat = bm_ref[:, g * LANES:(g + 1) * LANES].astype(BF16)
            cmat = cm_ref[:, g * LANES:(g + 1) * LANES].astype(BF16)
            xdt = xs_ref[:, gs] * dt_x
            dyg = dy_ref[:, gs]
            gmat = _dot_nt(cmat, bmat)
            ht_in = hin_ref[0, g]
            dht = dht_s[g]
            ht_in_b, dht_b = ht_in.astype(BF16), dht.astype(BF16)
            ch = _dot(cmat, ht_in_b)
            bdh = _dot(bmat, dht_b)
            edy = e_x * dyg
            wx = w_x * xdt
            te_s[:, gs] = edy * ch
            tw_s[:, gs] = wx * bdh
            edy_b = edy.astype(BF16)
            dcm = _dot_nt(edy_b, ht_in_b)
            dbm = _dot_nt(wx.astype(BF16), dht_b)
            dht_s[g] = dec_x * dht + _dot_tn(cmat, edy_b)
            th = jnp.broadcast_to(jnp.sum(dht * ht_in, axis=0, keepdims=True), (8, 512))
            dtot_h = dtot_h + _dot_h_nt(th, eg)
            dgm = jnp.zeros((L, L), F32)
            for pr in range(4):
                ps = slice(pr * LANES, (pr + 1) * LANES)
                xp = xdt[:, ps]
                dyp_b = dyg[:, ps].astype(BF16)
                dxp = w_x[:, ps] * bdh[:, ps]
                for s_ in range(2):
                    h = off + 8 * g + 2 * pr + s_
                    keep = lo if s_ == 0 else jnp.logical_not(lo)
                    seg = t["acs"][:, h:h + 1] - acst_s[h:h + 1, :]
                    lam = jnp.exp(jnp.where(t["mask"], seg, -1e30))
                    mf = gmat * lam
                    m = mf.astype(BF16)
                    xm = jnp.where(keep, xp, 0.0).astype(BF16)
                    dm = _dot_nt(dyp_b, xm)
                    dgm = dgm + dm * lam
                    q = dm * mf
                    colpart = colpart + jnp.sum(q, axis=1, keepdims=True) * (lane1 == h).astype(F32)
                    rowt_s[h:h + 1, :] = jnp.sum(q, axis=0, keepdims=True)
                    dxp = dxp + jnp.where(keep, _dot_tn(m, dyp_b), 0.0)
                dxdt_s[:, g * 512 + pr * LANES:g * 512 + (pr + 1) * LANES] = dxp
            dgm_b = dgm.astype(BF16)
            dx_ref[:, D_INNER + g * LANES:D_INNER + (g + 1) * LANES] = dbm + _dot_tn(dgm_b, cmat)
            dx_ref[:, D_INNER + 512 + g * LANES:D_INNER + 512 + (g + 1) * LANES] = dcm + _dot(dgm_b, bmat)

        e_all = e_ref[...]
        t_e = _dot_h_nt(te_s[...], e_all)
        t_w = _dot_h_nt(tw_s[...], e_all)
        row = lax.broadcasted_iota(jnp.int32, (L, L), 0)
        col = lax.broadcasted_iota(jnp.int32, (L, L), 1)
        ident = (row == col).astype(F32)
        colsum_part = _dot_h_tn(rowt_s[...], ident)
        dtot = jnp.sum(t_w, axis=0, keepdims=True) + t["dec"] * dtot_h[0:1, :]
        row1 = lax.broadcasted_iota(jnp.int32, (L, LANES), 0)
        last = row1 == (0 if rev else L - 1)
        dacs = colpart - colsum_part + t_e - t_w + jnp.where(last, dtot, 0.0)
        dda = _dot_h(t["cmt"], dacs)
        dxdt = dxdt_s[...]
        ddt = dda * t["a"] + _dot_h_nt(dxdt * xs_ref[...], e_all)
        dalog = jnp.sum(dda * t["dt"], axis=0, keepdims=True) * t["a"]
        draw = ddt * _sigmoid(t["pre"])
        draw_ref[...] = draw
        dbias = jnp.sum(draw, axis=0, keepdims=True)

        @pl.when(c == 0)
        def _():
            dalog_ref[...] = dalog
            dbias_ref[...] = dbias

        @pl.when(c > 0)
        def _():
            dalog_ref[...] += dalog
            dbias_ref[...] += dbias

        dxs = dxdt * _dot_h(t["dt"], e_all)
        if has_skip:
            dxs = dxs + dy_ref[...] * skip_ref[...]
        dx_ref[:, 0:D_INNER] = dxs

    one = pl.BlockSpec((1, LANES), lambda c: (0, 0))
    in_specs = [pl.BlockSpec((L, D_INNER), lambda c: (cidx(c), 0)), pl.BlockSpec((L, D_INNER), lambda c: (cidx(c), 0)),
                pl.BlockSpec((L, 512), lambda c: (cidx(c), 4)), pl.BlockSpec((L, 512), lambda c: (cidx(c), 5)),
                pl.BlockSpec((L, LANES), lambda c: (cidx(c), U_DT // LANES)), one, one,
                pl.BlockSpec((1, SSM_GROUPS, D_STATE, 512), lambda c: (cidx(c), 0, 0, 0)),
                pl.BlockSpec((LANES, D_INNER), lambda c: (0, 0))]
    ins = [dy, xbc_act, xbc_act, xbc_act, u_small, bias128, alog128, hin, e_mat]
    if has_skip:
        in_specs.append(pl.BlockSpec((1, D_INNER), lambda c: (0, 0)))
        ins.append(skip_x)
    return _pallas(
        body, name=name, grid=(nc,), in_specs=in_specs,
        out_specs=[pl.BlockSpec((L, XBC_DIM), lambda c: (cidx(c), 0)), pl.BlockSpec((L, LANES), lambda c: (cidx(c), 0)),
                   one, one],
        out_shape=[jax.ShapeDtypeStruct((T, XBC_DIM), F32), jax.ShapeDtypeStruct((T, LANES), F32),
                   jax.ShapeDtypeStruct((1, LANES), F32), jax.ShapeDtypeStruct((1, LANES), F32)],
        scratch_shapes=[pltpu.VMEM((SSM_GROUPS, D_STATE, 512), F32), pltpu.VMEM((LANES, L), F32),
                        pltpu.VMEM((LANES, L), F32), pltpu.VMEM((L, D_INNER), F32), pltpu.VMEM((L, D_INNER), F32),
                        pltpu.VMEM((L, D_INNER), F32)],
        compiler_params=_params(("arbitrary",)),
    )(*ins)


def _ssm_out_fwd(y_f, y_b, xbc_act, u_big, skip_x, ssm_norm, *, name, tile=512):
    rows = y_f.shape[0]

    def fn(yf, yb, xs, z, sk, nw):
        yz = (yf + yb + sk * xs) * (z * _sigmoid(z))
        r = lax.rsqrt(jnp.mean(yz * yz, axis=-1, keepdims=True) + EPS)
        return yz * r * nw

    return _rowmap(fn, name=name, rows=rows, tile=tile, ncol=SSM_GROUPS,
                   ins=[(y_f, 512, _colj), (y_b, 512, _colj), (xbc_act, 512, _colj), (u_big, 512, _colj)],
                   consts=[(skip_x, 512, _colj), (ssm_norm, 512, _colj)], outs=[(D_INNER, BF16, 512, _colj)])[0]


def _ssm_out_bwd(dm, y_f, y_b, xbc_act, u_big, skip_x, ssm_norm, *, name, tile=512):
    rows = y_f.shape[0]

    def fn(dmv, yf, yb, xs, z, sk, nw):
        sg = _sigmoid(z)
        y = yf + yb + sk * xs
        yz = y * (z * sg)
        r = lax.rsqrt(jnp.mean(yz * yz, axis=-1, keepdims=True) + EPS)
        xh = yz * r
        dxh = dmv * nw
        dyz = r * (dxh - xh * jnp.mean(dxh * xh, axis=-1, keepdims=True))
        dy = dyz * (z * sg)
        dz = dyz * y * (sg * (1.0 + z * (1.0 - sg)))
        return dy, dz, jnp.sum(dmv * xh, axis=0, keepdims=True), jnp.sum(dy * xs, axis=0, keepdims=True)

    return _rowmap(fn, name=name, rows=rows, tile=tile, ncol=SSM_GROUPS,
                   ins=[(dm, 512, _colj), (y_f, 512, _colj), (y_b, 512, _colj), (xbc_act, 512, _colj),
                        (u_big, 512, _colj)],
                   consts=[(skip_x, 512, _colj), (ssm_norm, 512, _colj)],
                   outs=[(D_INNER, F32, 512, _colj), (D_INNER, BF16, 512, _colj)],
                   accs=[(D_INNER, 512, _colj), (D_INNER, 512, _colj)])


def _merge_fwd(pa, pb, u_big, *, name, tile=512):
    rows = pa.shape[0]

    def fn(a, b, ga, gb):
        return _sigmoid(ga) * a + _sigmoid(gb) * b

    return _rowmap(fn, name=name, rows=rows, tile=tile,
                   ins=[(pa, 1024, _col0), (pb, 1024, _col0), (u_big, 1024, lambda j: 5), (u_big, 1024, lambda j: 6)],
                   outs=[(1024, BF16, 1024, _col0)])[0]


def _merge_bwd(dmg, pa, pb, u_big, *, name, tile=512):
    rows = pa.shape[0]

    def fn(d, a, b, ga, gb):
        sa, sb = _sigmoid(ga), _sigmoid(gb)
        return d * sa, d * sb, d * a * sa * (1.0 - sa), d * b * sb * (1.0 - sb)

    return _rowmap(fn, name=name, rows=rows, tile=tile,
                   ins=[(dmg, 1024, _col0), (pa, 1024, _col0), (pb, 1024, _col0), (u_big, 1024, lambda j: 5),
                        (u_big, 1024, lambda j: 6)],
                   outs=[(1024, BF16, 1024, _col0)] * 4)


def _loss_bwd(y, target, *, name, tile=512):
    rows, d = y.shape

    def fn(yv, tv):
        err = yv - tv
        part = jnp.sum(jnp.sum(err * err, axis=-1, keepdims=True), axis=0, keepdims=True)
        return err * (1.0 / d), jnp.broadcast_to(part * (0.5 / d), (1, LANES))

    dy, part = _rowmap(fn, name=name, rows=rows, tile=tile, ins=[(y, d, _col0), (target, d, _col0)],
                       outs=[(d, F32, d, _col0)], accs=[(LANES, LANES, _col0)])
    return dy, part[0, 0]


def _small_slab(gs, dskip_ch, dalog_f, dalog_b, dbias_f, dbias_b, gkv, gqh, gkh, dconv_w, *, name):
    e_mat = _ssd_expand(False)
    full_names = ("ffn1_norm", "mix_norm", "q_a_norm", "conv_b", "ssm_norm", "ffn2_norm")
    full = [gs[n] for n in full_names]
    nf = len(full)

    def body(*refs):
        fulls = refs[:nf]
        (dsk_ref, e_ref, af_ref, ab_ref, bf_ref, bb_ref, gkv_ref, gqh_ref, gkh_ref, cw_ref, o_ref) = refs[nf:]
        o_ref[...] = jnp.zeros_like(o_ref)
        for n, r in zip(full_names, fulls):
            o_ref[SMALL_ROW[n]:SMALL_ROW[n] + 1, 0:r.shape[1]] = r[...]
        o_ref[SMALL_ROW["kv_a_norm"]:SMALL_ROW["kv_a_norm"] + 1, 0:KV_LORA] = gkv_ref[...]
        o_ref[SMALL_ROW["q_head_norm"]:SMALL_ROW["q_head_norm"] + 1, 0:LANES] = gqh_ref[...]
        o_ref[SMALL_ROW["k_head_norm"]:SMALL_ROW["k_head_norm"] + 1, 0:LANES] = gkh_ref[...]
        o_ref[SMALL_ROW["a_log_fwd"]:SMALL_ROW["a_log_fwd"] + 1, 0:LANES] = af_ref[...]
        o_ref[SMALL_ROW["a_log_bwd"]:SMALL_ROW["a_log_bwd"] + 1, 0:LANES] = pltpu.roll(ab_ref[...], 96, 1)
        o_ref[SMALL_ROW["dt_bias_fwd"]:SMALL_ROW["dt_bias_fwd"] + 1, 0:LANES] = bf_ref[...]
        o_ref[SMALL_ROW["dt_bias_bwd"]:SMALL_ROW["dt_bias_bwd"] + 1, 0:LANES] = pltpu.roll(bb_ref[...], 96, 1)
        dsk = _dot_h_nt(jnp.broadcast_to(dsk_ref[...], (8, D_INNER)), e_ref[...])
        o_ref[SMALL_ROW["d_skip"]:SMALL_ROW["d_skip"] + 1, 0:LANES] = dsk[0:1, :]
        o_ref[CONV_ROW:CONV_ROW + CONV_WIDTH, :] = cw_ref[...]

    return _pallas(body, name=name, out_shape=jax.ShapeDtypeStruct((SMALL_ROWS, SMALL_COLS), F32))(
        *full, dskip_ch, e_mat, dalog_f, dalog_b, dbias_f, dbias_b, gkv, gqh, gkh, dconv_w)


def _adamw_math(g, w, m, v):
    m2 = ADAM_B1 * m + (1.0 - ADAM_B1) * g
    v2 = ADAM_B2 * v + (1.0 - ADAM_B2) * (g * g)
    m_hat = m2 / (1.0 - ADAM_B1 ** ADAM_STEP)
    v_hat = v2 / (1.0 - ADAM_B2 ** ADAM_STEP)
    delta = -ADAM_LR * (m_hat / (jnp.sqrt(v_hat) + ADAM_EPS) + ADAM_WD * w)
    return delta, m2, v2


def _sum8(r_ref):
    g = r_ref[0].astype(F32)
    for s in range(1, N_DEV):
        g = g + r_ref[s].astype(F32)
    return g


def _reduce_adamw(recv, w, m, v, *, name, tile=256):
    _, R, C = recv.shape
    tile = _pick(R, tile) if R % LANES == 0 else R
    assert R % tile == 0

    def body(r_ref, w_ref, m_ref, v_ref, g_ref, d_ref, m2_ref, v2_ref):
        g = _sum8(r_ref)
        delta, m2, v2 = _adamw_math(g, w_ref[...], m_ref[...], v_ref[...])
        g_ref[...] = g
        d_ref[...] = delta
        m2_ref[...] = m2
        v2_ref[...] = v2

    blk = pl.BlockSpec((tile, C), lambda i: (i, 0))
    return _pallas(
        body, name=name, grid=(R // tile,),
        in_specs=[pl.BlockSpec((N_DEV, tile, C), lambda i: (0, i, 0)), blk, blk, blk], out_specs=[blk] * 4,
        out_shape=[jax.ShapeDtypeStruct((R, C), F32)] * 4, compiler_params=_params(("parallel",)),
    )(recv, w, m, v)


def _reduce_t_adamw(recv, w, m, v, *, name):
    R, cs = w.shape

    def body(r_ref, w_ref, m_ref, v_ref, g_ref, d_ref, m2_ref, v2_ref):
        g = _sum8(r_ref).T
        delta, m2, v2 = _adamw_math(g, w_ref[...], m_ref[...], v_ref[...])
        g_ref[...] = g
        d_ref[...] = delta
        m2_ref[...] = m2
        v2_ref[...] = v2

    return _pallas(body, name=name, out_shape=[jax.ShapeDtypeStruct((R, cs), F32)] * 4,
                   compiler_params=pltpu.CompilerParams(vmem_limit_bytes=VMEM_LIMIT))(recv, w, m, v)


def _reduce8(recv, *, name, tile):
    _, R, C = recv.shape

    def body(r_ref, g_ref):
        g_ref[...] = _sum8(r_ref)

    return _pallas(body, name=name, grid=(R // tile,),
                   in_specs=[pl.BlockSpec((N_DEV, tile, C), lambda i: (0, i, 0))],
                   out_specs=pl.BlockSpec((tile, C), lambda i: (i, 0)),
                   out_shape=jax.ShapeDtypeStruct((R, C), F32), compiler_params=_params(("parallel",)))(recv)


def _adamw(g, w, m, v, *, name, tile=256):
    R, C = w.shape

    def body(g_ref, w_ref, m_ref, v_ref, d_ref, m2_ref, v2_ref):
        delta, m2, v2 = _adamw_math(g_ref[...], w_ref[...], m_ref[...], v_ref[...])
        d_ref[...] = delta
        m2_ref[...] = m2
        v2_ref[...] = v2

    blk = pl.BlockSpec((tile, C), lambda i: (i, 0))
    return _pallas(body, name=name, grid=(R // tile,), in_specs=[blk] * 4, out_specs=[blk] * 3,
                   out_shape=[jax.ShapeDtypeStruct((R, C), F32)] * 3, compiler_params=_params(("parallel",)))(g, w, m, v)


def _adamw_small(srecv, conv_g, ws, ms, vs, *, name):
    n = len(ws)

    def body(*refs):
        s_ref, c_ref = refs[0], refs[1]
        w_refs, m_refs, v_refs = refs[2:2 + n], refs[2 + n:2 + 2 * n], refs[2 + 2 * n:2 + 3 * n]
        outs = refs[2 + 3 * n:]
        gsum = _sum8(s_ref)
        for i in range(n):
            if i < len(SMALL):
                g = gsum[i:i + 1, 0:SMALL[i][1]]
            else:
                g = _sum8(c_ref)
            delta, m2, v2 = _adamw_math(g, w_refs[i][...], m_refs[i][...], v_refs[i][...])
            outs[i][...] = g
            outs[n + i][...] = delta
            outs[2 * n + i][...] = m2
            outs[3 * n + i][...] = v2

    shapes = [jax.ShapeDtypeStruct(w.shape, F32) for w in ws]
    got = _pallas(body, name=name, out_shape=shapes * 4,
                  compiler_params=pltpu.CompilerParams(vmem_limit_bytes=VMEM_LIMIT))(srecv, conv_g, *ws, *ms, *vs)
    return got[:n], got[n:2 * n], got[2 * n:3 * n], got[3 * n:]


def _ffn_fwd(x, norm, w_gu, w_d, tag):
    h = _rms_fwd(x, norm, name=f"{tag}_rms")
    gu = _mm(h, w_gu, name=f"{tag}_gu")
    act = _swiglu_fwd(gu, name=f"{tag}_act")
    out = _mm(act, w_d, name=f"{tag}_down", alpha=0.5, res=x)
    return out, (h, gu, act)


def _ffn_bwd(dout, x, norm, w_gu, w_d, saved, tag):
    h, gu, act = saved
    d_act = _mm(dout, w_d, name=f"{tag}_dact", tb=True, alpha=0.5)
    d_wd = _mm(act, dout, name=f"{tag}_dwd", ta=True, alpha=0.5, tm=1408, tn=1024, out_dtype=BF16)
    dg, du = _swiglu_bwd(d_act, gu, name=f"{tag}_dswiglu")
    d_wg_t = _mm(dg, h, name=f"{tag}_dwg", ta=True, tm=1408, tn=1024, out_dtype=BF16)
    d_wu_t = _mm(du, h, name=f"{tag}_dwu", ta=True, tm=1408, tn=1024, out_dtype=BF16)
    dh = _mm(dg, w_gu[:, :D_FF], name=f"{tag}_dh_g", tb=True)
    dh = _mm(du, w_gu[:, D_FF:], name=f"{tag}_dh_u", tb=True, res=dh)
    dx, dnorm = _rms_bwd(dh, x, norm, dout, name=f"{tag}_drms")
    return dx, dnorm, d_wg_t, d_wu_t, d_wd


def _rope_tables(positions, T):
    pos = positions.reshape(T).astype(F32)
    inv_freq = 1.0 / (ROPE_BASE ** (jnp.arange(0, QK_ROPE, 2, dtype=F32) / QK_ROPE))
    ang = pos[:, None] * inv_freq
    cos, sin = jnp.cos(ang), jnp.sin(ang)
    one64, z64 = jnp.ones((T, 64), F32), jnp.zeros((T, 64), F32)
    z16, z32, one32 = jnp.zeros((T, 16), F32), jnp.zeros((T, 32), F32), jnp.ones((T, 32), F32)
    c = jnp.concatenate([one64, cos, cos, one32], axis=1)
    s1 = jnp.concatenate([z64, -sin, z16, z32], axis=1)
    s2 = jnp.concatenate([z64, z16, sin, z32], axis=1)
    return c, s1, s2


def _cols(g):
    n, r, cs = g.shape
    return g.transpose(1, 0, 2).reshape(r, n * cs)


def _rows(g):
    n, rs, c = g.shape
    return g.reshape(n * rs, c)


def _pad_lanes(v, n=LANES):
    return jnp.pad(v, ((0, 0), (0, n - v.shape[1])))


def _in_proj_weights(w_in):
    z = lambda n: jnp.zeros((D_MODEL, n), w_in.dtype)
    w_small = jnp.concatenate([w_in[:, 0:384], z(128), w_in[:, 384:640], w_in[:, 640:672], z(96),
                               w_in[:, 5792:5856], z(64)], axis=1)
    w_big = jnp.concatenate([w_in[:, 672:2720], w_in[:, 2720:5792], w_in[:, 5856:7904]], axis=1)
    return w_small, w_big


def _mla_up_weights(w_q_b, w_kv_b):
    wq = w_q_b.reshape(Q_LORA, N_HEADS, QK_HEAD)
    wq = jnp.pad(wq, ((0, 0), (0, 0), (0, LANES - QK_HEAD))).reshape(Q_LORA, N_HEADS * LANES)
    wkv = w_kv_b.reshape(KV_LORA, N_HEADS, QK_NOPE + V_HEAD)
    wk = jnp.pad(wkv[..., :QK_NOPE], ((0, 0), (0, 0), (0, LANES - QK_NOPE))).reshape(KV_LORA, N_HEADS * LANES)
    v = wkv[..., QK_NOPE:]
    zv = jnp.zeros_like(v)
    even = (jnp.arange(N_HEADS) % 2 == 0)[None, :, None]
    wv = jnp.where(even, jnp.concatenate([v, zv], -1), jnp.concatenate([zv, v], -1)).reshape(KV_LORA, N_HEADS * LANES)
    return wq, wk, wv


def _shard_rows(g):
    return g.reshape(N_DEV, g.shape[0] // N_DEV, g.shape[1])


def kernel(x, positions, ffn1_norm, ffn1_w_gate, ffn1_w_up, ffn1_w_down, mix_norm, w_in, q_a_norm, w_q_b, kv_a_norm, w_kv_b, q_head_norm, k_head_norm, conv_w, conv_b, a_log_fwd, a_log_bwd, dt_bias_fwd, dt_bias_bwd, d_skip, ssm_norm, w_attn_branch, w_ssm_branch, w_out, ffn2_norm, ffn2_w_gate, ffn2_w_up, ffn2_w_down, loss_target, m_ffn1_norm, m_ffn1_w_gate, m_ffn1_w_up, m_ffn1_w_down, m_mix_norm, m_w_in, m_q_a_norm, m_w_q_b, m_kv_a_norm, m_w_kv_b, m_q_head_norm, m_k_head_norm, m_conv_w, m_conv_b, m_a_log_fwd, m_a_log_bwd, m_dt_bias_fwd, m_dt_bias_bwd, m_d_skip, m_ssm_norm, m_w_attn_branch, m_w_ssm_branch, m_w_out, m_ffn2_norm, m_ffn2_w_gate, m_ffn2_w_up, m_ffn2_w_down, v_ffn1_norm, v_ffn1_w_gate, v_ffn1_w_up, v_ffn1_w_down, v_mix_norm, v_w_in, v_q_a_norm, v_w_q_b, v_kv_a_norm, v_w_kv_b, v_q_head_norm, v_k_head_norm, v_conv_w, v_conv_b, v_a_log_fwd, v_a_log_bwd, v_dt_bias_fwd, v_dt_bias_bwd, v_d_skip, v_ssm_norm, v_w_attn_branch, v_w_ssm_branch, v_w_out, v_ffn2_norm, v_ffn2_w_gate, v_ffn2_w_up, v_ffn2_w_down):
    w_all = dict(ffn1_norm=ffn1_norm, ffn1_w_gate=ffn1_w_gate, ffn1_w_up=ffn1_w_up, ffn1_w_down=ffn1_w_down, mix_norm=mix_norm, w_in=w_in, q_a_norm=q_a_norm, w_q_b=w_q_b, kv_a_norm=kv_a_norm, w_kv_b=w_kv_b, q_head_norm=q_head_norm, k_head_norm=k_head_norm, conv_w=conv_w, conv_b=conv_b, a_log_fwd=a_log_fwd, a_log_bwd=a_log_bwd, dt_bias_fwd=dt_bias_fwd, dt_bias_bwd=dt_bias_bwd, d_skip=d_skip, ssm_norm=ssm_norm, w_attn_branch=w_attn_branch, w_ssm_branch=w_ssm_branch, w_out=w_out, ffn2_norm=ffn2_norm, ffn2_w_gate=ffn2_w_gate, ffn2_w_up=ffn2_w_up, ffn2_w_down=ffn2_w_down)
    m_all = dict(ffn1_norm=m_ffn1_norm, ffn1_w_gate=m_ffn1_w_gate, ffn1_w_up=m_ffn1_w_up, ffn1_w_down=m_ffn1_w_down, mix_norm=m_mix_norm, w_in=m_w_in, q_a_norm=m_q_a_norm, w_q_b=m_w_q_b, kv_a_norm=m_kv_a_norm, w_kv_b=m_w_kv_b, q_head_norm=m_q_head_norm, k_head_norm=m_k_head_norm, conv_w=m_conv_w, conv_b=m_conv_b, a_log_fwd=m_a_log_fwd, a_log_bwd=m_a_log_bwd, dt_bias_fwd=m_dt_bias_fwd, dt_bias_bwd=m_dt_bias_bwd, d_skip=m_d_skip, ssm_norm=m_ssm_norm, w_attn_branch=m_w_attn_branch, w_ssm_branch=m_w_ssm_branch, w_out=m_w_out, ffn2_norm=m_ffn2_norm, ffn2_w_gate=m_ffn2_w_gate, ffn2_w_up=m_ffn2_w_up, ffn2_w_down=m_ffn2_w_down)
    v_all = dict(ffn1_norm=v_ffn1_norm, ffn1_w_gate=v_ffn1_w_gate, ffn1_w_up=v_ffn1_w_up, ffn1_w_down=v_ffn1_w_down, mix_norm=v_mix_norm, w_in=v_w_in, q_a_norm=v_q_a_norm, w_q_b=v_w_q_b, kv_a_norm=v_kv_a_norm, w_kv_b=v_w_kv_b, q_head_norm=v_q_head_norm, k_head_norm=v_k_head_norm, conv_w=v_conv_w, conv_b=v_conv_b, a_log_fwd=v_a_log_fwd, a_log_bwd=v_a_log_bwd, dt_bias_fwd=v_dt_bias_fwd, dt_bias_bwd=v_dt_bias_bwd, d_skip=v_d_skip, ssm_norm=v_ssm_norm, w_attn_branch=v_w_attn_branch, w_ssm_branch=v_w_ssm_branch, w_out=v_w_out, ffn2_norm=v_ffn2_norm, ffn2_w_gate=v_ffn2_w_gate, ffn2_w_up=v_ffn2_w_up, ffn2_w_down=v_ffn2_w_down)
    T = x.shape[1]
    xs_in, target = x[0], loss_target[0]
    w2 = {n: a.reshape(-1, a.shape[-1]) for n, a in w_all.items()}
    m2 = {n: m_all[n].reshape(w2[n].shape) for n in w2}
    v2 = {n: v_all[n].reshape(w2[n].shape) for n in w2}
    p = {n: w2[n] for n, _ in SMALL}
    bf = lambda n: w2[n].astype(BF16)

    early = ["ffn1_w_gate", "ffn1_w_up", "ffn1_w_down", "w_in", "w_q_b", "w_kv_b"]
    g_early = _all_gather_two_level([bf(n) for n in early] + [w2["conv_w"]], name="gather_early")
    ge = dict(zip(early + ["conv_w"], g_early))
    w_gu1 = jnp.concatenate([_cols(ge["ffn1_w_gate"]), _cols(ge["ffn1_w_up"])], axis=1)
    w_d1 = _rows(ge["ffn1_w_down"])
    w_small, w_big = _in_proj_weights(_cols(ge["w_in"]))
    wq, wk, wv = _mla_up_weights(_cols(ge["w_q_b"]), _cols(ge["w_kv_b"]))
    conv_full = _cols(ge["conv_w"])
    late = ["w_attn_branch", "w_ssm_branch", "w_out", "ffn2_w_gate", "ffn2_w_up", "ffn2_w_down"]
    late_shards = [bf(n) for n in late]

    tabs = _rope_tables(positions, T)
    qg, kg = _pad_lanes(p["q_head_norm"]), _pad_lanes(p["k_head_norm"])
    bias128 = _pad_lanes(jnp.concatenate([p["dt_bias_fwd"], p["dt_bias_bwd"]], axis=1))
    alog128 = _pad_lanes(jnp.concatenate([p["a_log_fwd"], p["a_log_bwd"]], axis=1))
    skip_x = jnp.repeat(p["d_skip"], 64, axis=1)

    x1, ffn1_saved = _ffn_fwd(xs_in, p["ffn1_norm"], w_gu1, w_d1, "ffn1")
    h2 = _rms_fwd(x1, p["mix_norm"], name="mix_rms")
    u_big = _mm(h2, w_big, name="in_big")
    u_small = _mm(h2, w_small, name="in_small")
    cqn, ckvn = _lora_norm_fwd(u_small, p["q_a_norm"], p["kv_a_norm"], name="lora_norm")
    q_raw = _mm(cqn, wq, name="q_up")
    k_raw = _mm(ckvn, wk, name="k_up")
    v_raw = _mm(ckvn, wv, name="v_up")
    q, k, v = _qk_prep_fwd(q_raw, k_raw, v_raw, u_small, tabs, qg, kg, name="qk_prep")
    a_out, lse, g_late = _attn_fwd(q, k, v, ["gather"] * len(late), late_shards, name="attn_fwd")
    gl = dict(zip(late, g_late))
    w_pa, w_pb, w_o = _rows(gl["w_attn_branch"]), _rows(gl["w_ssm_branch"]), _rows(gl["w_out"])
    w_gu2 = jnp.concatenate([_cols(gl["ffn2_w_gate"]), _cols(gl["ffn2_w_up"])], axis=1)
    w_d2 = _rows(gl["ffn2_w_down"])
    xbc_act = _conv_fwd(u_big, conv_full, p["conv_b"], name="conv_fwd")
    y_f, hin_f = _ssd_fwd(xbc_act, u_small, bias128, alog128, rev=False, name="ssd_fwd_f")
    y_b, hin_b = _ssd_fwd(xbc_act, u_small, bias128, alog128, rev=True, name="ssd_fwd_b")
    m_out = _ssm_out_fwd(y_f, y_b, xbc_act, u_big, skip_x, p["ssm_norm"], name="ssm_out")
    pa = _mm(a_out, w_pa, name="branch_a")
    pb = _mm(m_out, w_pb, name="branch_b")
    merged = _merge_fwd(pa, pb, u_big, name="merge")
    x2 = _mm(merged, w_o, name="mix_out", res=x1)
    y, ffn2_saved = _ffn_fwd(x2, p["ffn2_norm"], w_gu2, w_d2, "ffn2")
    dy, loss_part = _loss_bwd(y, target, name="loss")
    loss = lax.psum(loss_part, ("x", "y", "c"))

    gs = {}
    dx2, gs["ffn2_norm"], g_gate2, g_up2, g_down2 = _ffn_bwd(dy, x2, p["ffn2_norm"], w_gu2, w_d2, ffn2_saved, "ffn2b")
    dmerged = _mm(dx2, w_o, name="d_merged", tb=True)
    g_out = _mm(merged, dx2, name="d_w_out", ta=True, out_dtype=BF16)
    dpa, dpb, dga, dgb = _merge_bwd(dmerged, pa, pb, u_big, name="d_merge")
    g_pa = _mm(a_out, dpa, name="d_w_pa", ta=True, out_dtype=BF16)
    g_pb = _mm(m_out, dpb, name="d_w_pb", ta=True, out_dtype=BF16)
    da_out = _mm(dpa, w_pa, name="d_a", tb=True)
    dm_out = _mm(dpb, w_pb, name="d_m", tb=True)
    late_grads = [_shard_rows(g) for g in (g_pa, g_pb, g_out, g_gate2, g_up2, g_down2)]
    dq, dk, dv, r_late = _attn_bwd(q, k, v, a_out, lse, da_out, ["scatter"] * len(late_grads), late_grads,
                                   name="attn_bwd")
    recv = dict(zip(late, r_late))

    dyss, dz, gs["ssm_norm"], dskip_ch = _ssm_out_bwd(dm_out, y_f, y_b, xbc_act, u_big, skip_x, p["ssm_norm"],
                                                      name="d_ssm_out")
    dact_f, draw_f, dalog_f, dbias_f = _ssd_bwd(dyss, xbc_act, u_small, bias128, alog128, hin_f, skip_x,
                                                rev=False, name="ssd_bwd_f")
    dact_b, draw_b, dalog_b, dbias_b = _ssd_bwd(dyss, xbc_act, u_small, bias128, alog128, hin_b, None,
                                                rev=True, name="ssd_bwd_b")
    dxbc, g_conv, gs["conv_b"] = _conv_bwd(dact_f, dact_b, u_big, conv_full, p["conv_b"], name="conv_bwd")

    dq_raw, dk_raw, dv_raw, dkpe, gqh, gkh = _qk_prep_bwd(dq, dk, dv, q_raw, k_raw, u_small, tabs, qg, kg,
                                                          name="d_qk_prep")
    g_wq_t = _mm(dq_raw, cqn, name="d_w_q", ta=True, out_dtype=BF16)
    g_wk_t = _mm(dk_raw, ckvn, name="d_w_k", ta=True, out_dtype=BF16)
    g_wv_t = _mm(dv_raw, ckvn, name="d_w_v", ta=True, out_dtype=BF16)
    dcqn = _mm(dq_raw, wq, name="d_cqn", tb=True)
    dckvn = _mm(dk_raw, wk, name="d_ckvn_k", tb=True)
    dckvn = _mm(dv_raw, wv, name="d_ckvn_v", tb=True, res=dckvn)
    du_small, gs["q_a_norm"], gkv = _lora_norm_bwd(dcqn, dckvn, u_small, p["q_a_norm"], p["kv_a_norm"], dkpe,
                                                   draw_f, draw_b, name="d_lora_norm")

    dh2 = _mm(du_small, w_small, name="d_h2_small", tb=True)
    dh2 = _mm(dz, w_big[:, 0:2048], name="d_h2_z", tb=True, res=dh2)
    dh2 = _mm(dxbc, w_big[:, 2048:5120], name="d_h2_xbc", tb=True, res=dh2)
    dh2 = _mm(dga, w_big[:, 5120:6144], name="d_h2_ga", tb=True, res=dh2)
    dh2 = _mm(dgb, w_big[:, 6144:7168], name="d_h2_gb", tb=True, res=dh2)
    gt_small = _mm(du_small, h2, name="d_w_small", ta=True, out_dtype=BF16)
    gt_z = _mm(dz, h2, name="d_w_z", ta=True, out_dtype=BF16)
    gt_xbc = _mm(dxbc, h2, name="d_w_xbc", ta=True, out_dtype=BF16)
    gt_ga = _mm(dga, h2, name="d_w_ga", ta=True, out_dtype=BF16)
    gt_gb = _mm(dgb, h2, name="d_w_gb", ta=True, out_dtype=BF16)
    dx1, gs["mix_norm"] = _rms_bwd(dh2, x1, p["mix_norm"], dx2, name="d_mix_rms")
    grad_x, gs["ffn1_norm"], g_gate1, g_up1, g_down1 = _ffn_bwd(dx1, xs_in, p["ffn1_norm"], w_gu1, w_d1, ffn1_saved,
                                                                "ffn1b")

    gt_in = jnp.concatenate([gt_small[0:384], gt_small[U_CKV:U_KPE + QK_ROPE], gt_z, gt_xbc,
                             gt_small[U_DT:U_DT + 64], gt_ga, gt_gb], axis=0)
    gt_in = jnp.pad(gt_in.reshape(N_DEV, W_IN_SHARD, D_MODEL), ((0, 0), (0, W_IN_SHARD_PAD - W_IN_SHARD), (0, 0)))
    gt_q = g_wq_t.reshape(N_HEADS, LANES, Q_LORA)[:, :QK_HEAD].reshape(N_DEV, -1, Q_LORA)
    gk3 = g_wk_t.reshape(N_HEADS, LANES, KV_LORA)[:, :QK_NOPE]
    gv3 = g_wv_t.reshape(N_HEADS, LANES, KV_LORA)
    even = (jnp.arange(N_HEADS) % 2 == 0)[:, None, None]
    gv3 = jnp.where(even, gv3[:, :V_HEAD], gv3[:, V_HEAD:])
    gt_kv = jnp.concatenate([gk3, gv3], axis=1).reshape(N_DEV, -1, KV_LORA)
    gsmall = _small_slab(gs, dskip_ch, dalog_f, dalog_b, dbias_f, dbias_b, gkv, gqh, gkh, g_conv, name="small_slab")
    last = ["ffn1_w_gate", "ffn1_w_up", "ffn1_w_down", "w_in", "w_q_b", "w_kv_b"]
    last_grads = [_shard_rows(g_gate1), _shard_rows(g_up1), _shard_rows(g_down1), gt_in, gt_q, gt_kv]
    r_last = _exchange(["scatter"] * len(last) + ["gather"], last_grads + [gsmall], name="grad_exchange")
    recv.update(zip(last, r_last[:-1]))
    srecv = r_last[-1]

    out = {}
    for n in ("ffn1_w_down", "ffn2_w_down", "w_attn_branch", "w_ssm_branch", "w_out"):
        out[n] = _reduce_adamw(recv[n], w2[n], m2[n], v2[n], name=f"adamw_{n}")
    for n in ("ffn1_w_gate", "ffn1_w_up", "ffn2_w_gate", "ffn2_w_up", "w_q_b", "w_kv_b"):
        out[n] = _reduce_t_adamw(recv[n], w2[n], m2[n], v2[n], name=f"adamw_{n}")
    g_in = _reduce8(recv["w_in"], name="sum_w_in", tile=W_IN_SHARD_PAD // 2)[:W_IN_SHARD].T
    out["w_in"] = [g_in] + list(_adamw(g_in, w2["w_in"], m2["w_in"], v2["w_in"], name="adamw_w_in"))
    me = 4 * lax.axis_index("x") + 2 * lax.axis_index("y") + lax.axis_index("c")
    conv_g = lax.dynamic_slice(srecv, (0, CONV_ROW, me * (XBC_DIM // N_DEV)), (N_DEV, CONV_WIDTH, XBC_DIM // N_DEV))
    sn = [n for n, _ in SMALL] + ["conv_w"]
    sg, sd, sm, sv = _adamw_small(srecv, conv_g, [w2[n] for n in sn], [m2[n] for n in sn], [v2[n] for n in sn],
                                  name="adamw_small")
    for i, n in enumerate(sn):
        out[n] = (sg[i], sd[i], sm[i], sv[i])
    outs = [[out[n][kind].reshape(w_all[n].shape) for n in WEIGHT_ORDER] for kind in range(4)]
    return (loss, grad_x[None], *outs[0], *outs[1], *outs[2], *outs[3])
```

```python
import math

import jax
import jax.numpy as jnp
from jax import lax
from jax.experimental import pallas as pl
from jax.experimental.pallas import tpu as pltpu

F32, BF16 = jnp.float32, jnp.bfloat16
HIGHEST = lax.Precision.HIGHEST

D_MODEL, D_FF = 1024, 2816
EPS = 1e-6
N_HEADS, QK_NOPE, QK_ROPE, QK_HEAD, V_HEAD = 16, 64, 32, 96, 64
Q_LORA, KV_LORA = 384, 256
ROPE_BASE = 10000.0
D_INNER, SSM_HEADS, SSM_GROUPS, D_STATE, CONV_WIDTH, CHUNK = 2048, 32, 4, 128, 5, 128
XBC_DIM = D_INNER + 2 * SSM_GROUPS * D_STATE
IN_DIM = 7904
ADAM_LR, ADAM_B1, ADAM_B2, ADAM_EPS, ADAM_WD, ADAM_STEP = 0.001, 0.9, 0.999, 1e-08, 0.01, 10
N_DEV = 8

V7X_VMEM_BYTES = 64 * 1024 * 1024
VMEM_LIMIT = V7X_VMEM_BYTES - 8 * 1024 * 1024
LANES = 128
W_IN_SHARD = IN_DIM // N_DEV
W_IN_SHARD_PAD = 992

SMALL = (
    ("ffn1_norm", 1024), ("mix_norm", 1024), ("q_a_norm", 384), ("kv_a_norm", 256), ("q_head_norm", 96),
    ("k_head_norm", 96), ("conv_b", 3072), ("a_log_fwd", 32), ("a_log_bwd", 32), ("dt_bias_fwd", 32),
    ("dt_bias_bwd", 32), ("d_skip", 32), ("ssm_norm", 2048), ("ffn2_norm", 1024),
)
SMALL_ROW = {n: i for i, (n, _) in enumerate(SMALL)}
CONV_ROW = len(SMALL)
SMALL_ROWS, SMALL_COLS = 24, XBC_DIM
WEIGHT_ORDER = (
    "ffn1_norm", "ffn1_w_gate", "ffn1_w_up", "ffn1_w_down", "mix_norm", "w_in", "q_a_norm", "w_q_b", "kv_a_norm",
    "w_kv_b", "q_head_norm", "k_head_norm", "conv_w", "conv_b", "a_log_fwd", "a_log_bwd", "dt_bias_fwd", "dt_bias_bwd",
    "d_skip", "ssm_norm", "w_attn_branch", "w_ssm_branch", "w_out", "ffn2_norm", "ffn2_w_gate", "ffn2_w_up",
    "ffn2_w_down",
)


def _pallas(body, **kw):
    return pl.pallas_call(body, **kw)


def _params(sem):
    return pltpu.CompilerParams(dimension_semantics=sem, vmem_limit_bytes=VMEM_LIMIT)


def _pick(dim, pref):
    if dim <= pref:
        return dim
    c = (pref // LANES) * LANES
    while c >= LANES:
        if dim % c == 0:
            return c
        c -= LANES
    raise ValueError((dim, pref))


def _sigmoid(x):
    return 1.0 / (1.0 + jnp.exp(-x))


def _softplus(x):
    return jnp.maximum(x, 0.0) + jnp.log(1.0 + jnp.exp(-jnp.abs(x)))


def _dot(a, b):
    return jnp.dot(a, b, preferred_element_type=F32)


def _dot_nt(a, b):
    return lax.dot_general(a, b, (((1,), (1,)), ((), ())), preferred_element_type=F32)


def _dot_tn(a, b):
    return lax.dot_general(a, b, (((0,), (0,)), ((), ())), preferred_element_type=F32)


def _dot_h(a, b):
    return jnp.dot(a, b, preferred_element_type=F32, precision=HIGHEST)


def _dot_h_nt(a, b):
    return lax.dot_general(a, b, (((1,), (1,)), ((), ())), preferred_element_type=F32, precision=HIGHEST)


def _dot_h_tn(a, b):
    return lax.dot_general(a, b, (((0,), (0,)), ((), ())), preferred_element_type=F32, precision=HIGHEST)


def _mesh_pos():
    return lax.axis_index("x"), lax.axis_index("y"), lax.axis_index("c")


def _comm_scratch(n):
    return [pltpu.SemaphoreType.DMA((7 * n,)), pltpu.SemaphoreType.DMA((7 * n,)), pltpu.SemaphoreType.DMA((n,))]


def _comm_copies(modes, srcs, dsts, send_sems, recv_sems, local_sems, arrivals):
    x, y, c = _mesh_pos()
    me = 4 * x + 2 * y + c
    local, remote = [], []
    for w, (mode, s, d) in enumerate(zip(modes, srcs, dsts)):
        gather = mode == "gather"
        if not arrivals:
            local.append(pltpu.make_async_copy(s if gather else s.at[me], d.at[me], local_sems.at[w]))
        for k in range(1, N_DEV):
            px = (1 - x) if (k & 4) else x
            py = (1 - y) if (k & 2) else y
            pc = (1 - c) if (k & 1) else c
            peer = 4 * px + 2 * py + pc
            idx = 7 * w + k - 1
            remote.append(pltpu.make_async_remote_copy(
                src_ref=s if gather else s.at[peer], dst_ref=d.at[peer] if arrivals else d.at[me],
                send_sem=send_sems.at[idx], recv_sem=recv_sems.at[idx],
                device_id=(px, py, pc), device_id_type=pl.DeviceIdType.MESH))
    return local, remote


def _comm_start(modes, srcs, dsts, sems):
    local, sends = _comm_copies(modes, srcs, dsts, *sems, arrivals=False)
    for cp in local + sends:
        cp.start()


def _comm_wait(modes, srcs, dsts, sems):
    _, recvs = _comm_copies(modes, srcs, dsts, *sems, arrivals=True)
    for cp in recvs:
        cp.wait_recv()
    local, sends = _comm_copies(modes, srcs, dsts, *sems, arrivals=False)
    for cp in sends:
        cp.wait_send()
    for cp in local:
        cp.wait()


def _comm_out_shapes(modes, arrays):
    return [jax.ShapeDtypeStruct((N_DEV,) + (a.shape if m == "gather" else a.shape[1:]), a.dtype)
            for m, a in zip(modes, arrays)]


def _exchange(modes, arrays, *, name):
    n = len(arrays)

    def body(*refs):
        srcs, dsts, sems = refs[:n], refs[n:2 * n], refs[2 * n:]
        _comm_start(modes, srcs, dsts, sems)
        _comm_wait(modes, srcs, dsts, sems)

    any_spec = pl.BlockSpec(memory_space=pl.ANY)
    return _pallas(body, name=name, out_shape=_comm_out_shapes(modes, arrays), in_specs=[any_spec] * n,
                   out_specs=[any_spec] * n, scratch_shapes=_comm_scratch(n))(*arrays)


def _all_gather_two_level(shards, *, name):
    n = len(shards)

    def body(*refs):
        srcs, outs = refs[:n], refs[n:2 * n]
        send_sems, recv_sems, local_sems = refs[2 * n:]
        x, y, c = _mesh_pos()
        me, sibling = (x, y, c), (x, y, 1 - c)
        chips = [(1 - x, y), (x, 1 - y), (1 - x, 1 - y)]

        def blk(w, px, py, pc):
            return outs[w].at[4 * px + 2 * py + pc]

        def copy(w, k, block, to, src=None):
            return pltpu.make_async_remote_copy(
                src_ref=blk(w, *block) if src is None else src, dst_ref=blk(w, *block),
                send_sem=send_sems.at[7 * w + k], recv_sem=recv_sems.at[7 * w + k], device_id=to,
                device_id_type=pl.DeviceIdType.MESH)

        mine = [pltpu.make_async_copy(srcs[w], blk(w, *me), local_sems.at[w]) for w in range(n)]
        for cp in mine:
            cp.start()
        first = []
        for w in range(n):
            first.append(copy(w, 0, me, sibling, src=srcs[w]))
            first += [copy(w, 1 + j, me, (*chip, c), src=srcs[w]) for j, chip in enumerate(chips)]
        for cp in first:
            cp.start()
        passed = []
        for w in range(n):
            for j, chip in enumerate(chips):
                copy(w, 1 + j, (*chip, c), me).wait_recv()
                fwd = copy(w, 4 + j, (*chip, c), sibling)
                fwd.start()
                passed.append(fwd)
        for w in range(n):
            copy(w, 0, sibling, me).wait_recv()
            for j, chip in enumerate(chips):
                copy(w, 4 + j, (*chip, 1 - c), me).wait_recv()
        for cp in first + passed:
            cp.wait_send()
        for cp in mine:
            cp.wait()

    any_spec = pl.BlockSpec(memory_space=pl.ANY)
    return _pallas(body, name=name, out_shape=_comm_out_shapes(["gather"] * n, shards), in_specs=[any_spec] * n,
                   out_specs=[any_spec] * n, scratch_shapes=_comm_scratch(n))(*shards)


def _mm(a, b, *, name, ta=False, tb=False, out_dtype=F32, alpha=1.0, res=None, tm=1024, tn=1408, tk=1408):
    (K, M) = a.shape if ta else a.shape[::-1]
    (N, Kb) = b.shape if tb else b.shape[::-1]
    assert K == Kb, (a.shape, b.shape, ta, tb)
    tm, tn, tk = _pick(M, tm), _pick(N, tn), _pick(K, tk)
    nk = K // tk
    a_spec = pl.BlockSpec((tk, tm), lambda i, j, k: (k, i)) if ta else pl.BlockSpec((tm, tk), lambda i, j, k: (i, k))
    b_spec = pl.BlockSpec((tn, tk), lambda i, j, k: (j, k)) if tb else pl.BlockSpec((tk, tn), lambda i, j, k: (k, j))
    o_spec = pl.BlockSpec((tm, tn), lambda i, j, k: (i, j))
    dn = (((0 if ta else 1,), (1 if tb else 0,)), ((), ()))
    has_res = res is not None

    def body(*refs):
        a_ref, b_ref = refs[0], refs[1]
        r_ref = refs[2] if has_res else None
        o_ref = refs[3] if has_res else refs[2]
        part = lax.dot_general(a_ref[...].astype(BF16), b_ref[...].astype(BF16), dn, preferred_element_type=F32)

        def finish(acc):
            if alpha != 1.0:
                acc = acc * alpha
            if has_res:
                acc = acc + r_ref[...]
            o_ref[...] = acc.astype(o_ref.dtype)

        if nk == 1:
            finish(part)
        else:
            acc_ref = refs[-1]
            k = pl.program_id(2)

            @pl.when(k == 0)
            def _():
                acc_ref[...] = part

            @pl.when(k > 0)
            def _():
                acc_ref[...] += part

            @pl.when(k == nk - 1)
            def _():
                finish(acc_ref[...])

    ins = [a, b] + ([res] if has_res else [])
    in_specs = [a_spec, b_spec] + ([o_spec] if has_res else [])
    return _pallas(
        body, name=name, grid=(M // tm, N // tn, nk), in_specs=in_specs, out_specs=o_spec,
        out_shape=jax.ShapeDtypeStruct((M, N), out_dtype),
        scratch_shapes=[pltpu.VMEM((tm, tn), F32)] if nk > 1 else [],
        compiler_params=_params(("parallel", "parallel", "arbitrary")),
    )(*ins)


def _col0(j):
    return 0


def _colj(j):
    return j


def _rowmap(fn, *, name, rows, tile, ins, consts=(), outs=(), accs=(), ncol=1):
    tile = min(tile, rows)
    nrow = rows // tile
    in_specs = [pl.BlockSpec((tile, w), lambda j, i, f=f: (i, f(j))) for _, w, f in ins]
    for arr, w, f in consts:
        in_specs.append(pl.BlockSpec((arr.shape[0], w), lambda j, i, f=f: (0, f(j))))
    out_specs = [pl.BlockSpec((tile, w), lambda j, i, f=f: (i, f(j))) for _, _, w, f in outs]
    out_specs += [pl.BlockSpec((1, w), lambda j, i, f=f: (0, f(j))) for _, w, f in accs]
    out_shape = [jax.ShapeDtypeStruct((rows, c), dt) for c, dt, _, _ in outs]
    out_shape += [jax.ShapeDtypeStruct((1, c), F32) for c, _, _ in accs]
    n_in, n_out = len(ins) + len(consts), len(outs)
    acc_fixed = [f is _col0 for _, _, f in accs]

    def body(*refs):
        res = fn(*[r[...] for r in refs[:n_in]])
        if not isinstance(res, (tuple, list)):
            res = (res,)
        for r, v in zip(refs[n_in:n_in + n_out], res[:n_out]):
            r[...] = v.astype(r.dtype)
        j, i = pl.program_id(0), pl.program_id(1)
        for r, v, fixed in zip(refs[n_in + n_out:], res[n_out:], acc_fixed):
            first = ((i == 0) & (j == 0)) if fixed else (i == 0)

            @pl.when(first)
            def _(r=r, v=v):
                r[...] = v

            @pl.when(jnp.logical_not(first))
            def _(r=r, v=v):
                r[...] += v

    arrays = [a for a, _, _ in ins] + [a for a, _, _ in consts]
    return _pallas(
        body, name=name, grid=(ncol, nrow), in_specs=in_specs, out_specs=out_specs, out_shape=out_shape,
        compiler_params=_params(("arbitrary", "arbitrary")),
    )(*arrays)


def _rms_fwd(x, g, *, name, tile=512):
    rows, d = x.shape

    def fn(xv, gv):
        r = lax.rsqrt(jnp.mean(xv * xv, axis=-1, keepdims=True) + EPS)
        return xv * r * gv

    return _rowmap(fn, name=name, rows=rows, tile=tile, ins=[(x, d, _col0)], consts=[(g, d, _col0)],
                   outs=[(d, BF16, d, _col0)])[0]


def _rms_bwd(dh, x, g, res, *, name, tile=512):
    rows, d = x.shape

    def fn(dhv, xv, rv, gv):
        r = lax.rsqrt(jnp.mean(xv * xv, axis=-1, keepdims=True) + EPS)
        xh = xv * r
        dxh = dhv * gv
        dx = r * (dxh - xh * jnp.mean(dxh * xh, axis=-1, keepdims=True))
        return rv + dx, jnp.sum(dhv * xh, axis=0, keepdims=True)

    return _rowmap(fn, name=name, rows=rows, tile=tile, ins=[(dh, d, _col0), (x, d, _col0), (res, d, _col0)],
                   consts=[(g, d, _col0)], outs=[(d, F32, d, _col0)], accs=[(d, d, _col0)])


def _swiglu_fwd(gu, *, name, tile=512):
    rows = gu.shape[0]
    w = _pick(D_FF, 1408)
    nb = D_FF // w

    def fn(gv, uv):
        return gv * _sigmoid(gv) * uv

    return _rowmap(fn, name=name, rows=rows, tile=tile, ncol=nb,
                   ins=[(gu, w, _colj), (gu, w, lambda j: j + nb)], outs=[(D_FF, BF16, w, _colj)])[0]


def _swiglu_bwd(da, gu, *, name, tile=512):
    rows = gu.shape[0]
    w = _pick(D_FF, 1408)
    nb = D_FF // w

    def fn(dav, gv, uv):
        sg = _sigmoid(gv)
        dg = dav * uv * (sg * (1.0 + gv * (1.0 - sg)))
        du = dav * (gv * sg)
        return dg, du

    return _rowmap(fn, name=name, rows=rows, tile=tile, ncol=nb,
                   ins=[(da, w, _colj), (gu, w, _colj), (gu, w, lambda j: j + nb)],
                   outs=[(D_FF, BF16, w, _colj), (D_FF, BF16, w, _colj)])


U_CKV, U_KPE, U_DT = 512, 768, 896


def _lora_norm_fwd(u_small, qg, kvg, *, name, tile=512):
    rows = u_small.shape[0]

    def fn(cq, ckv, qgv, kgv):
        rq = lax.rsqrt(jnp.mean(cq * cq, axis=-1, keepdims=True) + EPS)
        rk = lax.rsqrt(jnp.mean(ckv * ckv, axis=-1, keepdims=True) + EPS)
        return cq * rq * qgv, ckv * rk * kgv

    return _rowmap(fn, name=name, rows=rows, tile=tile,
                   ins=[(u_small, Q_LORA, _col0), (u_small, KV_LORA, lambda j: U_CKV // KV_LORA)],
                   consts=[(qg, Q_LORA, _col0), (kvg, KV_LORA, _col0)],
                   outs=[(Q_LORA, BF16, Q_LORA, _col0), (KV_LORA, BF16, KV_LORA, _col0)])


def _lora_norm_bwd(dcqn, dckvn, u_small, qg, kvg, dkpe, draw_f, draw_b, *, name, tile=512):
    rows = u_small.shape[0]
    tile = min(tile, rows)

    def body(dq_ref, dk_ref, u_ref, dkp_ref, df_ref, db_ref, qg_ref, kg_ref, du_ref, gq_ref, gk_ref):
        cq, ckv = u_ref[:, 0:Q_LORA], u_ref[:, U_CKV:U_CKV + KV_LORA]
        dq, dk = dq_ref[...], dk_ref[...]
        rq = lax.rsqrt(jnp.mean(cq * cq, axis=-1, keepdims=True) + EPS)
        xh = cq * rq
        dxh = dq * qg_ref[...]
        du_ref[:, 0:Q_LORA] = (rq * (dxh - xh * jnp.mean(dxh * xh, axis=-1, keepdims=True))).astype(BF16)
        du_ref[:, Q_LORA:U_CKV] = jnp.zeros((tile, U_CKV - Q_LORA), BF16)
        rk = lax.rsqrt(jnp.mean(ckv * ckv, axis=-1, keepdims=True) + EPS)
        kh = ckv * rk
        dkh = dk * kg_ref[...]
        du_ref[:, U_CKV:U_KPE] = (rk * (dkh - kh * jnp.mean(dkh * kh, axis=-1, keepdims=True))).astype(BF16)
        du_ref[:, U_KPE:U_DT] = dkp_ref[...].astype(BF16)
        du_ref[:, U_DT:U_DT + LANES] = (df_ref[...] + db_ref[...]).astype(BF16)
        gq = jnp.sum(dq * xh, axis=0, keepdims=True)
        gk = jnp.sum(dk * kh, axis=0, keepdims=True)
        i = pl.program_id(0)

        @pl.when(i == 0)
        def _():
            gq_ref[...] = gq
            gk_ref[...] = gk

        @pl.when(i > 0)
        def _():
            gq_ref[...] += gq
            gk_ref[...] += gk

    def rowblk(w):
        return pl.BlockSpec((tile, w), lambda i: (i, 0))

    def whole(w):
        return pl.BlockSpec((1, w), lambda i: (0, 0))

    return _pallas(
        body, name=name, grid=(rows // tile,),
        in_specs=[rowblk(Q_LORA), rowblk(KV_LORA), rowblk(1024), rowblk(LANES), rowblk(LANES), rowblk(LANES),
                  whole(Q_LORA), whole(KV_LORA)],
        out_specs=[rowblk(1024), whole(Q_LORA), whole(KV_LORA)],
        out_shape=[jax.ShapeDtypeStruct((rows, 1024), BF16), jax.ShapeDtypeStruct((1, Q_LORA), F32),
                   jax.ShapeDtypeStruct((1, KV_LORA), F32)],
        compiler_params=_params(("arbitrary",)),
    )(dcqn, dckvn, u_small, dkpe, draw_f, draw_b, qg, kvg)


def _rope(x, c, s1, s2):
    return x * c + pltpu.roll(x, 112, 1) * s1 + pltpu.roll(x, 16, 1) * s2


def _rope_t(d, c, s1, s2):
    return d * c + pltpu.roll(d * s1, 16, 1) + pltpu.roll(d * s2, 112, 1)


def _qk_prep_fwd(q_raw, k_raw, v_raw, u_small, tabs, qg, kg, *, name, tile=256):
    rows = q_raw.shape[0]
    tile = min(tile, rows)
    scale = 1.0 / math.sqrt(QK_HEAD)

    def body(q_ref, k_ref, v_ref, u_ref, c_ref, s1_ref, s2_ref, qg_ref, kg_ref, qo_ref, ko_ref, vo_ref):
        c, s1, s2 = c_ref[...], s1_ref[...], s2_ref[...]
        qgv, kgv = qg_ref[...], kg_ref[...]
        kpe = pltpu.roll(u_ref[:, U_KPE:U_KPE + LANES], 64, 1)
        vo_ref[...] = v_ref[...].astype(BF16)
        for h in range(N_HEADS):
            hs = slice(h * LANES, (h + 1) * LANES)
            qr = q_ref[:, hs]
            rq = lax.rsqrt(jnp.sum(qr * qr, axis=-1, keepdims=True) / QK_HEAD + EPS)
            qo_ref[:, hs] = (_rope(qr * rq * qgv, c, s1, s2) * scale).astype(BF16)
            xk = k_ref[:, hs] + kpe
            rk = lax.rsqrt(jnp.sum(xk * xk, axis=-1, keepdims=True) / QK_HEAD + EPS)
            ko_ref[:, hs] = _rope(xk * rk * kgv, c, s1, s2).astype(BF16)

    wide = pl.BlockSpec((tile, 2048), lambda i: (i, 0))
    narrow = pl.BlockSpec((tile, LANES), lambda i: (i, 0))
    gain = pl.BlockSpec((1, LANES), lambda i: (0, 0))
    return _pallas(
        body, name=name, grid=(rows // tile,),
        in_specs=[wide, wide, wide, pl.BlockSpec((tile, 1024), lambda i: (i, 0)), narrow, narrow, narrow, gain, gain],
        out_specs=[wide, wide, wide], out_shape=[jax.ShapeDtypeStruct((rows, 2048), BF16)] * 3,
        compiler_params=_params(("parallel",)),
    )(q_raw, k_raw, v_raw, u_small, *tabs, qg, kg)


def _qk_prep_bwd(dq, dk, dv, q_raw, k_raw, u_small, tabs, qg, kg, *, name, tile=256):
    rows = q_raw.shape[0]
    tile = min(tile, rows)
    scale = 1.0 / math.sqrt(QK_HEAD)

    def body(dq_ref, dk_ref, dv_ref, q_ref, k_ref, u_ref, c_ref, s1_ref, s2_ref, qg_ref, kg_ref,
             dqo_ref, dko_ref, dvo_ref, dkpe_ref, gq_ref, gk_ref):
        c, s1, s2 = c_ref[...], s1_ref[...], s2_ref[...]
        qgv, kgv = qg_ref[...], kg_ref[...]
        kpe = pltpu.roll(u_ref[:, U_KPE:U_KPE + LANES], 64, 1)
        lane = lax.broadcasted_iota(jnp.int32, (tile, LANES), 1)
        dvo_ref[...] = dv_ref[...].astype(BF16)
        gq = jnp.zeros((1, LANES), F32)
        gk = jnp.zeros((1, LANES), F32)
        dkpe = jnp.zeros((tile, LANES), F32)
        for h in range(N_HEADS):
            hs = slice(h * LANES, (h + 1) * LANES)
            qr = q_ref[:, hs]
            rq = lax.rsqrt(jnp.sum(qr * qr, axis=-1, keepdims=True) / QK_HEAD + EPS)
            xh = qr * rq
            dy = _rope_t(dq_ref[:, hs] * scale, c, s1, s2)
            dxh = dy * qgv
            dqo_ref[:, hs] = (rq * (dxh - xh * (jnp.sum(dxh * xh, axis=-1, keepdims=True) / QK_HEAD))).astype(BF16)
            gq = gq + jnp.sum(dy * xh, axis=0, keepdims=True)
            xk = k_ref[:, hs] + kpe
            rk = lax.rsqrt(jnp.sum(xk * xk, axis=-1, keepdims=True) / QK_HEAD + EPS)
            kh = xk * rk
            dyk = _rope_t(dk_ref[:, hs], c, s1, s2)
            dkh = dyk * kgv
            dxk = rk * (dkh - kh * (jnp.sum(dkh * kh, axis=-1, keepdims=True) / QK_HEAD))
            gk = gk + jnp.sum(dyk * kh, axis=0, keepdims=True)
            dko_ref[:, hs] = jnp.where(lane < QK_NOPE, dxk, 0.0).astype(BF16)
            dkpe = dkpe + dxk
        dkpe_ref[...] = jnp.where(lane < QK_ROPE, pltpu.roll(dkpe, 64, 1), 0.0)
        i = pl.program_id(0)

        @pl.when(i == 0)
        def _():
            gq_ref[...] = gq
            gk_ref[...] = gk

        @pl.when(i > 0)
        def _():
            gq_ref[...] += gq
            gk_ref[...] += gk

    wide = pl.BlockSpec((tile, 2048), lambda i: (i, 0))
    narrow = pl.BlockSpec((tile, LANES), lambda i: (i, 0))
    gain = pl.BlockSpec((1, LANES), lambda i: (0, 0))
    return _pallas(
        body, name=name, grid=(rows // tile,),
        in_specs=[wide, wide, wide, wide, wide, pl.BlockSpec((tile, 1024), lambda i: (i, 0)), narrow, narrow, narrow,
                  gain, gain],
        out_specs=[wide, wide, wide, narrow, gain, gain],
        out_shape=[jax.ShapeDtypeStruct((rows, 2048), BF16)] * 3
        + [jax.ShapeDtypeStruct((rows, LANES), F32), jax.ShapeDtypeStruct((1, LANES), F32),
           jax.ShapeDtypeStruct((1, LANES), F32)],
        compiler_params=_params(("arbitrary",)),
    )(dq, dk, dv, q_raw, k_raw, u_small, *tabs, qg, kg)


def _attn_fwd(q, k, v, comm_modes, comm_arrays, *, name, tq=512, tkc=512):
    T = q.shape[0]
    tq = min(tq, T)
    tkc = min(tkc, T)
    n = len(comm_arrays)
    nj, ni = N_HEADS // 2, T // tq

    def body(*refs):
        q_ref, k_ref, v_ref = refs[:3]
        srcs = refs[3:3 + n]
        o_ref, lse_ref = refs[3 + n:5 + n]
        dsts = refs[5 + n:5 + 2 * n]
        sems = refs[5 + 2 * n:]
        j, i = pl.program_id(0), pl.program_id(1)

        @pl.when((j == 0) & (i == 0))
        def _():
            _comm_start(comm_modes, srcs, dsts, sems)

        out = None
        for hh in range(2):
            sl = slice(hh * LANES, (hh + 1) * LANES)
            qv = q_ref[:, sl]
            m = l = acc = None
            for kc in range(T // tkc):
                ks = slice(kc * tkc, (kc + 1) * tkc)
                s = _dot_nt(qv, k_ref[ks, sl])
                mc = jnp.max(s, axis=-1, keepdims=True)
                if m is None:
                    m = mc
                    p = jnp.exp(s - m)
                    l = jnp.sum(p, axis=-1, keepdims=True)
                    acc = _dot(p.astype(BF16), v_ref[ks, sl])
                else:
                    m_new = jnp.maximum(m, mc)
                    alpha = jnp.exp(m - m_new)
                    p = jnp.exp(s - m_new)
                    l = alpha * l + jnp.sum(p, axis=-1, keepdims=True)
                    acc = alpha * acc + _dot(p.astype(BF16), v_ref[ks, sl])
                    m = m_new
            o = acc / l
            out = o if out is None else out + o
            lse_ref[hh] = m + jnp.log(l)
        o_ref[...] = out

        @pl.when((j == nj - 1) & (i == ni - 1))
        def _():
            _comm_wait(comm_modes, srcs, dsts, sems)

    any_spec = pl.BlockSpec(memory_space=pl.ANY)
    got = _pallas(
        body, name=name, grid=(nj, ni),
        in_specs=[pl.BlockSpec((tq, 2 * LANES), lambda j, i: (i, j)), pl.BlockSpec((T, 2 * LANES), lambda j, i: (0, j)),
                  pl.BlockSpec((T, 2 * LANES), lambda j, i: (0, j))] + [any_spec] * n,
        out_specs=[pl.BlockSpec((tq, LANES), lambda j, i: (i, j)), pl.BlockSpec((2, tq, 1), lambda j, i: (j, i, 0))]
        + [any_spec] * n,
        out_shape=[jax.ShapeDtypeStruct((T, N_HEADS * V_HEAD), F32), jax.ShapeDtypeStruct((N_HEADS, T, 1), F32)]
        + _comm_out_shapes(comm_modes, comm_arrays),
        scratch_shapes=_comm_scratch(n),
        compiler_params=_params(("arbitrary", "arbitrary")),
    )(q, k, v, *comm_arrays)
    return got[0], got[1], got[2:]


def _attn_bwd(q, k, v, o, lse, do, comm_modes, comm_arrays, *, name, tk=256, tqc=4096):
    T = q.shape[0]
    tk = min(tk, T)
    tqc = min(tqc, T)
    n = len(comm_arrays)
    nj, nkb = N_HEADS // 2, T // tk

    def body(*refs):
        q_ref, k_ref, v_ref, o_ref, lse_ref, do_ref = refs[:6]
        srcs = refs[6:6 + n]
        dq_ref, dk_ref, dv_ref = refs[6 + n:9 + n]
        dsts = refs[9 + n:9 + 2 * n]
        d_s = refs[9 + 2 * n]
        sems = refs[10 + 2 * n:]
        j, kb = pl.program_id(0), pl.program_id(1)

        @pl.when((j == 0) & (kb == 0))
        def _():
            _comm_start(comm_modes, srcs, dsts, sems)

        lane = lax.broadcasted_iota(jnp.int32, (1, LANES), 1)
        @pl.when(kb == 0)
        def _():
            prod = do_ref[...] * o_ref[...]
            for hh in range(2):
                keep = (lane < V_HEAD) if hh == 0 else (lane >= V_HEAD)
                d_s[hh] = jnp.sum(jnp.where(keep, prod, 0.0), axis=-1, keepdims=True)

        for hh in range(2):
            sl = slice(hh * LANES, (hh + 1) * LANES)
            keep = (lane < V_HEAD) if hh == 0 else (lane >= V_HEAD)
            kv, vv = k_ref[:, sl], v_ref[:, sl]
            dv_acc = dk_acc = None
            for qc in range(T // tqc):
                qs = slice(qc * tqc, (qc + 1) * tqc)
                qv = q_ref[qs, sl]
                do_b = do_ref[qs, :].astype(BF16)
                s = _dot_nt(qv, kv)
                p = jnp.exp(s - lse_ref[hh, qs])
                dp = _dot_nt(do_b, vv)
                ds = (p * (dp - d_s[hh, qs])).astype(BF16)
                dvc = _dot_tn(p.astype(BF16), do_b)
                dkc = _dot_tn(ds, qv)
                dv_acc = dvc if dv_acc is None else dv_acc + dvc
                dk_acc = dkc if dk_acc is None else dk_acc + dkc
                dqp = _dot(ds, kv)

                @pl.when(kb == 0)
                def _(dqp=dqp, sl=sl, qs=qs):
                    dq_ref[qs, sl] = dqp

                @pl.when(kb > 0)
                def _(dqp=dqp, sl=sl, qs=qs):
                    dq_ref[qs, sl] += dqp

            dv_ref[:, sl] = jnp.where(keep, dv_acc, 0.0)
            dk_ref[:, sl] = dk_acc

        @pl.when((j == nj - 1) & (kb == nkb - 1))
        def _():
            _comm_wait(comm_modes, srcs, dsts, sems)

    any_spec = pl.BlockSpec(memory_space=pl.ANY)
    pair = pl.BlockSpec((T, 2 * LANES), lambda j, kb: (0, j))
    kblk = pl.BlockSpec((tk, 2 * LANES), lambda j, kb: (kb, j))
    got = _pallas(
        body, name=name, grid=(nj, nkb),
        in_specs=[pair, kblk, kblk, pl.BlockSpec((T, LANES), lambda j, kb: (0, j)),
                  pl.BlockSpec((2, T, 1), lambda j, kb: (j, 0, 0)), pl.BlockSpec((T, LANES), lambda j, kb: (0, j))]
        + [any_spec] * n,
        out_specs=[pair, kblk, kblk] + [any_spec] * n,
        out_shape=[jax.ShapeDtypeStruct((T, 2048), F32)] * 3 + _comm_out_shapes(comm_modes, comm_arrays),
        scratch_shapes=[pltpu.VMEM((2, T, 1), F32)] + _comm_scratch(n),
        compiler_params=_params(("arbitrary", "arbitrary")),
    )(q, k, v, o, lse, do, *comm_arrays)
    return got[0], got[1], got[2], got[3:]


def _conv_shift(x, sh, t_idx):
    if sh == 0:
        return x
    T = x.shape[0]
    y = pltpu.roll(x, (-sh) % T, 0)
    ok = (t_idx + sh >= 0) & (t_idx + sh < T)
    return jnp.where(ok, y, 0.0)


def _conv_fwd(u_big, conv_w, conv_b, *, name, w=256):
    T = u_big.shape[0]
    first = D_INNER // w

    def body(x_ref, w_ref, b_ref, o_ref):
        x = x_ref[...]
        t_idx = lax.broadcasted_iota(jnp.int32, x.shape, 0)
        acc = b_ref[...] + w_ref[2:3, :] * x
        for j in (0, 1, 3, 4):
            acc = acc + w_ref[j:j + 1, :] * _conv_shift(x, j - 2, t_idx)
        o_ref[...] = acc * _sigmoid(acc)

    return _pallas(
        body, name=name, grid=(XBC_DIM // w,),
        in_specs=[pl.BlockSpec((T, w), lambda j: (0, j + first)), pl.BlockSpec((CONV_WIDTH, w), lambda j: (0, j)),
                  pl.BlockSpec((1, w), lambda j: (0, j))],
        out_specs=pl.BlockSpec((T, w), lambda j: (0, j)),
        out_shape=jax.ShapeDtypeStruct((T, XBC_DIM), F32),
        compiler_params=_params(("parallel",)),
    )(u_big, conv_w, conv_b)


def _conv_bwd(dact_f, dact_b, u_big, conv_w, conv_b, *, name, w=128):
    T = u_big.shape[0]
    first = D_INNER // w

    def body(df_ref, db_ref, x_ref, w_ref, b_ref, dx_ref, dw_ref, dbias_ref):
        x = x_ref[...]
        t_idx = lax.broadcasted_iota(jnp.int32, x.shape, 0)
        pre = b_ref[...] + w_ref[2:3, :] * x
        for j in (0, 1, 3, 4):
            pre = pre + w_ref[j:j + 1, :] * _conv_shift(x, j - 2, t_idx)
        sg = _sigmoid(pre)
        dpre = (df_ref[...] + db_ref[...]) * (sg * (1.0 + pre * (1.0 - sg)))
        dbias_ref[...] = jnp.sum(dpre, axis=0, keepdims=True)
        dx = w_ref[2:3, :] * dpre
        dw_ref[2:3, :] = jnp.sum(dpre * x, axis=0, keepdims=True)
        for j in (0, 1, 3, 4):
            dx = dx + w_ref[j:j + 1, :] * _conv_shift(dpre, 2 - j, t_idx)
            dw_ref[j:j + 1, :] = jnp.sum(dpre * _conv_shift(x, j - 2, t_idx), axis=0, keepdims=True)
        dx_ref[...] = dx.astype(dx_ref.dtype)

    blk = pl.BlockSpec((T, w), lambda j: (0, j))
    return _pallas(
        body, name=name, grid=(XBC_DIM // w,),
        in_specs=[blk, blk, pl.BlockSpec((T, w), lambda j: (0, j + first)),
                  pl.BlockSpec((CONV_WIDTH, w), lambda j: (0, j)), pl.BlockSpec((1, w), lambda j: (0, j))],
        out_specs=[blk, pl.BlockSpec((CONV_WIDTH, w), lambda j: (0, j)), pl.BlockSpec((1, w), lambda j: (0, j))],
        out_shape=[jax.ShapeDtypeStruct((T, XBC_DIM), BF16), jax.ShapeDtypeStruct((CONV_WIDTH, XBC_DIM), F32),
                   jax.ShapeDtypeStruct((1, XBC_DIM), F32)],
        compiler_params=_params(("parallel",)),
    )(dact_f, dact_b, u_big, conv_w, conv_b)


def _ssd_expand(rev):
    off = SSM_HEADS if rev else 0
    h = jnp.arange(LANES, dtype=jnp.int32)[:, None]
    return (jnp.arange(D_INNER, dtype=jnp.int32)[None, :] // 64 + off == h).astype(F32)


def _ssd_head_terms(dt_ref, bias_ref, alog_ref, acst_s, dtt_s, rev):
    L = CHUNK
    row = lax.broadcasted_iota(jnp.int32, (L, L), 0)
    col = lax.broadcasted_iota(jnp.int32, (L, L), 1)
    mask = (row <= col) if rev else (row >= col)
    cm = mask.astype(F32)
    cmt = ((row >= col) if rev else (row <= col)).astype(F32)
    pre = dt_ref[...] + bias_ref[...]
    dt = _softplus(pre)
    a = -jnp.exp(alog_ref[...])
    da = dt * a
    acs = _dot_h(cm, da)
    acst_s[...] = _dot_h_tn(da, cmt)
    dtt_s[...] = _dot_h_tn(dt, (row == col).astype(F32))
    tot = jnp.sum(da, axis=0, keepdims=True)
    w = jnp.exp(tot - acs)
    return dict(mask=mask, cm=cm, cmt=cmt, ident=(row == col).astype(F32), pre=pre, dt=dt, a=a, da=da, acs=acs,
                tot=tot, e=jnp.exp(acs), w=w, wdt=w * dt, dec=jnp.exp(tot))


def _pair(lo, v, h0):
    return jnp.where(lo, v[:, h0:h0 + 1], v[:, h0 + 1:h0 + 2])


def _ssd_fwd(xbc_act, u_small, bias128, alog128, *, rev, name):
    T = xbc_act.shape[0]
    L = CHUNK
    nc = T // L
    off = SSM_HEADS if rev else 0

    def cidx(c):
        return (nc - 1 - c) if rev else c

    def body(xs_ref, bm_ref, cm_ref, dt_ref, bias_ref, alog_ref, y_ref, hin_ref, ht_s, acst_s, dtt_s, wx_s, dec_s):
        c = pl.program_id(0)

        @pl.when(c == 0)
        def _():
            ht_s[...] = jnp.zeros_like(ht_s)

        t = _ssd_head_terms(dt_ref, bias_ref, alog_ref, acst_s, dtt_s, rev)
        lo = lax.broadcasted_iota(jnp.int32, (L, LANES), 1) < 64
        lo1 = lax.broadcasted_iota(jnp.int32, (1, LANES), 1) < 64
        for g in range(SSM_GROUPS):
            bmat = bm_ref[:, g * LANES:(g + 1) * LANES].astype(BF16)
            cmat = cm_ref[:, g * LANES:(g + 1) * LANES].astype(BF16)
            gmat = _dot_nt(cmat, bmat)
            ht = ht_s[g]
            ch = _dot(cmat, ht.astype(BF16))
            for pr in range(4):
                ps = slice(pr * LANES, (pr + 1) * LANES)
                cs = slice(g * 512 + pr * LANES, g * 512 + (pr + 1) * LANES)
                h0 = off + 8 * g + 2 * pr
                xp = xs_ref[:, cs]
                acc = _pair(lo, t["e"], h0) * ch[:, ps]
                for s_ in range(2):
                    h = h0 + s_
                    seg = t["acs"][:, h:h + 1] - acst_s[h:h + 1, :]
                    lam = jnp.exp(jnp.where(t["mask"], seg, -1e30))
                    m = (gmat * lam * dtt_s[h:h + 1, :]).astype(BF16)
                    xm = jnp.where(lo if s_ == 0 else jnp.logical_not(lo), xp, 0.0).astype(BF16)
                    acc = acc + _dot(m, xm)
                y_ref[:, cs] = acc
                wx_s[:, ps] = (_pair(lo, t["wdt"], h0) * xp).astype(BF16)
                dec_s[0:1, ps] = _pair(lo1, t["dec"], h0)
            hin_ref[0, g] = ht
            ht_s[g] = ht * dec_s[0:1, :] + _dot_tn(bmat, wx_s[...])

    return _pallas(
        body, name=name, grid=(nc,),
        in_specs=[pl.BlockSpec((L, D_INNER), lambda c: (cidx(c), 0)), pl.BlockSpec((L, 512), lambda c: (cidx(c), 4)),
                  pl.BlockSpec((L, 512), lambda c: (cidx(c), 5)),
                  pl.BlockSpec((L, LANES), lambda c: (cidx(c), U_DT // LANES)),
                  pl.BlockSpec((1, LANES), lambda c: (0, 0)), pl.BlockSpec((1, LANES), lambda c: (0, 0))],
        out_specs=[pl.BlockSpec((L, D_INNER), lambda c: (cidx(c), 0)),
                   pl.BlockSpec((1, SSM_GROUPS, D_STATE, 512), lambda c: (cidx(c), 0, 0, 0))],
        out_shape=[jax.ShapeDtypeStruct((T, D_INNER), F32), jax.ShapeDtypeStruct((nc, SSM_GROUPS, D_STATE, 512), F32)],
        scratch_shapes=[pltpu.VMEM((SSM_GROUPS, D_STATE, 512), F32), pltpu.VMEM((LANES, L), F32),
                        pltpu.VMEM((LANES, L), F32), pltpu.VMEM((L, 512), BF16), pltpu.VMEM((8, 512), F32)],
        compiler_params=_params(("arbitrary",)),
    )(xbc_act, xbc_act, xbc_act, u_small, bias128, alog128)


def _ssd_bwd(dy, xbc_act, u_small, bias128, alog128, hin, skip_x, *, rev, name):
    T = xbc_act.shape[0]
    L = CHUNK
    nc = T // L
    off = SSM_HEADS if rev else 0
    has_skip = skip_x is not None

    def cidx(c):
        return c if rev else (nc - 1 - c)

    def body(*refs):
        (dy_ref, xs_ref, bm_ref, cm_ref, dt_ref, bias_ref, alog_ref, hin_ref) = refs[:8]
        k = 8
        skip_ref = refs[k] if has_skip else None
        k += 1 if has_skip else 0
        (dx_ref, draw_ref, dalog_ref, dbias_ref, dht_s, acst_s, dtt_s, rowt_s, ddtt_s, wx_s, edy_s, dec_s) = refs[k:]
        c = pl.program_id(0)

        @pl.when(c == 0)
        def _():
            dht_s[...] = jnp.zeros_like(dht_s)
            rowt_s[...] = jnp.zeros_like(rowt_s)
            ddtt_s[...] = jnp.zeros_like(ddtt_s)

        t = _ssd_head_terms(dt_ref, bias_ref, alog_ref, acst_s, dtt_s, rev)
        lane1 = lax.broadcasted_iota(jnp.int32, (1, LANES), 1)
        lo = lax.broadcasted_iota(jnp.int32, (L, LANES), 1) < 64
        lo1 = lane1 < 64
        colpart = jnp.zeros((L, LANES), F32)
        u_cols = jnp.zeros((L, LANES), F32)
        v_cols = jnp.zeros((L, LANES), F32)
        dtot_h = jnp.zeros((1, LANES), F32)
        for g in range(SSM_GROUPS):
            bmat = bm_ref[:, g * LANES:(g + 1) * LANES].astype(BF16)
            cmat = cm_ref[:, g * LANES:(g + 1) * LANES].astype(BF16)
            gmat = _dot_nt(cmat, bmat)
            ht_in = hin_ref[0, g]
            dht = dht_s[g]
            ht_in_b, dht_b = ht_in.astype(BF16), dht.astype(BF16)
            ch = _dot(cmat, ht_in_b)
            bdh = _dot(bmat, dht_b)
            th = jnp.sum(dht * ht_in, axis=0, keepdims=True)
            dgm = jnp.zeros((L, L), F32)
            for pr in range(4):
                ps = slice(pr * LANES, (pr + 1) * LANES)
                cs = slice(g * 512 + pr * LANES, g * 512 + (pr + 1) * LANES)
                h0 = off + 8 * g + 2 * pr
                xp = xs_ref[:, cs]
                dyp = dy_ref[:, cs]
                dyp_b = dyp.astype(BF16)
                wdt_p = _pair(lo, t["wdt"], h0)
                e_p = _pair(lo, t["e"], h0)
                xb = xp * bdh[:, ps]
                dc = dyp * ch[:, ps]
                dxp = wdt_p * bdh[:, ps]
                for s_ in range(2):
                    h = h0 + s_
                    keep = lo if s_ == 0 else jnp.logical_not(lo)
                    keep1 = lo1 if s_ == 0 else jnp.logical_not(lo1)
                    onehot = (lane1 == h).astype(F32)
                    dtrow = dtt_s[h:h + 1, :]
                    seg = t["acs"][:, h:h + 1] - acst_s[h:h + 1, :]
                    lam = jnp.exp(jnp.where(t["mask"], seg, -1e30))
                    mf0 = gmat * lam
                    m = (mf0 * dtrow).astype(BF16)
                    xm = jnp.where(keep, xp, 0.0).astype(BF16)
                    dm = _dot_nt(dyp_b, xm)
                    r = dm * mf0
                    q = r * dtrow
                    dgm = dgm + dm * lam * dtrow
                    colpart = colpart + jnp.sum(q, axis=1, keepdims=True) * onehot
                    rowt_s[h:h + 1, :] = jnp.sum(q, axis=0, keepdims=True)
                    ddtt_s[h:h + 1, :] = jnp.sum(r, axis=0, keepdims=True)
                    u_cols = u_cols + jnp.sum(jnp.where(keep, xb, 0.0), axis=1, keepdims=True) * onehot
                    v_cols = v_cols + jnp.sum(jnp.where(keep, dc, 0.0), axis=1, keepdims=True) * onehot
                    dtot_h = dtot_h + jnp.sum(jnp.where(keep1, th[:, ps], 0.0), axis=1, keepdims=True) * onehot
                    dxp = dxp + jnp.where(keep, _dot_tn(m, dyp_b), 0.0)
                if has_skip:
                    dxp = dxp + dyp * skip_ref[:, cs]
                dx_ref[:, cs] = dxp
                wx_s[:, ps] = (wdt_p * xp).astype(BF16)
                edy_s[:, ps] = (e_p * dyp).astype(BF16)
                dec_s[0:1, ps] = _pair(lo1, t["dec"], h0)
            edy_b = edy_s[...]
            dgm_b = dgm.astype(BF16)
            dx_ref[:, D_INNER + g * LANES:D_INNER + (g + 1) * LANES] = (
                _dot_nt(wx_s[...], dht_b) + _dot_tn(dgm_b, cmat))
            dx_ref[:, D_INNER + 512 + g * LANES:D_INNER + 512 + (g + 1) * LANES] = (
                _dot_nt(edy_b, ht_in_b) + _dot(dgm_b, bmat))
            dht_s[g] = dec_s[0:1, :] * dht + _dot_tn(cmat, edy_b)

        t_e = v_cols * t["e"]
        t_w = u_cols * t["wdt"]
        colsum_part = _dot_h_tn(rowt_s[...], t["ident"])
        dtot = jnp.sum(t_w, axis=0, keepdims=True) + t["dec"] * dtot_h
        row1 = lax.broadcasted_iota(jnp.int32, (L, LANES), 0)
        last = row1 == (0 if rev else L - 1)
        dacs = colpart - colsum_part + t_e - t_w + jnp.where(last, dtot, 0.0)
        dda = _dot_h(t["cmt"], dacs)
        ddt = dda * t["a"] + u_cols * t["w"] + _dot_h_tn(ddtt_s[...], t["ident"])
        dalog = jnp.sum(dda * t["dt"], axis=0, keepdims=True) * t["a"]
        draw = ddt * _sigmoid(t["pre"])
        draw_ref[...] = draw
        dbias = jnp.sum(draw, axis=0, keepdims=True)

        @pl.when(c == 0)
        def _():
            dalog_ref[...] = dalog
            dbias_ref[...] = dbias

        @pl.when(c > 0)
        def _():
            dalog_ref[...] += dalog
            dbias_ref[...] += dbias

    one = pl.BlockSpec((1, LANES), lambda c: (0, 0))
    in_specs = [pl.BlockSpec((L, D_INNER), lambda c: (cidx(c), 0)), pl.BlockSpec((L, D_INNER), lambda c: (cidx(c), 0)),
                pl.BlockSpec((L, 512), lambda c: (cidx(c), 4)), pl.BlockSpec((L, 512), lambda c: (cidx(c), 5)),
                pl.BlockSpec((L, LANES), lambda c: (cidx(c), U_DT // LANES)), one, one,
                pl.BlockSpec((1, SSM_GROUPS, D_STATE, 512), lambda c: (cidx(c), 0, 0, 0))]
    ins = [dy, xbc_act, xbc_act, xbc_act, u_small, bias128, alog128, hin]
    if has_skip:
        in_specs.append(pl.BlockSpec((1, D_INNER), lambda c: (0, 0)))
        ins.append(skip_x)
    return _pallas(
        body, name=name, grid=(nc,), in_specs=in_specs,
        out_specs=[pl.BlockSpec((L, XBC_DIM), lambda c: (cidx(c), 0)), pl.BlockSpec((L, LANES), lambda c: (cidx(c), 0)),
                   one, one],
        out_shape=[jax.ShapeDtypeStruct((T, XBC_DIM), F32), jax.ShapeDtypeStruct((T, LANES), F32),
                   jax.ShapeDtypeStruct((1, LANES), F32), jax.ShapeDtypeStruct((1, LANES), F32)],
        scratch_shapes=[pltpu.VMEM((SSM_GROUPS, D_STATE, 512), F32), pltpu.VMEM((LANES, L), F32),
                        pltpu.VMEM((LANES, L), F32), pltpu.VMEM((LANES, L), F32), pltpu.VMEM((LANES, L), F32),
                        pltpu.VMEM((L, 512), BF16), pltpu.VMEM((L, 512), BF16), pltpu.VMEM((8, 512), F32)],
        compiler_params=_params(("arbitrary",)),
    )(*ins)


def _ssm_out_fwd(y_f, y_b, xbc_act, u_big, skip_x, ssm_norm, *, name, tile=512):
    rows = y_f.shape[0]

    def fn(yf, yb, xs, z, sk, nw):
        yz = (yf + yb + sk * xs) * (z * _sigmoid(z))
        r = lax.rsqrt(jnp.mean(yz * yz, axis=-1, keepdims=True) + EPS)
        return yz * r * nw

    return _rowmap(fn, name=name, rows=rows, tile=tile, ncol=SSM_GROUPS,
                   ins=[(y_f, 512, _colj), (y_b, 512, _colj), (xbc_act, 512, _colj), (u_big, 512, _colj)],
                   consts=[(skip_x, 512, _colj), (ssm_norm, 512, _colj)], outs=[(D_INNER, BF16, 512, _colj)])[0]


def _ssm_out_bwd(dm, y_f, y_b, xbc_act, u_big, skip_x, ssm_norm, *, name, tile=512):
    rows = y_f.shape[0]

    def fn(dmv, yf, yb, xs, z, sk, nw):
        sg = _sigmoid(z)
        y = yf + yb + sk * xs
        yz = y * (z * sg)
        r = lax.rsqrt(jnp.mean(yz * yz, axis=-1, keepdims=True) + EPS)
        xh = yz * r
        dxh = dmv * nw
        dyz = r * (dxh - xh * jnp.mean(dxh * xh, axis=-1, keepdims=True))
        dy = dyz * (z * sg)
        dz = dyz * y * (sg * (1.0 + z * (1.0 - sg)))
        return dy, dz, jnp.sum(dmv * xh, axis=0, keepdims=True), jnp.sum(dy * xs, axis=0, keepdims=True)

    return _rowmap(fn, name=name, rows=rows, tile=tile, ncol=SSM_GROUPS,
                   ins=[(dm, 512, _colj), (y_f, 512, _colj), (y_b, 512, _colj), (xbc_act, 512, _colj),
                        (u_big, 512, _colj)],
                   consts=[(skip_x, 512, _colj), (ssm_norm, 512, _colj)],
                   outs=[(D_INNER, F32, 512, _colj), (D_INNER, BF16, 512, _colj)],
                   accs=[(D_INNER, 512, _colj), (D_INNER, 512, _colj)])


def _merge_fwd(pa, pb, u_big, *, name, tile=512):
    rows = pa.shape[0]

    def fn(a, b, ga, gb):
        return _sigmoid(ga) * a + _sigmoid(gb) * b

    return _rowmap(fn, name=name, rows=rows, tile=tile,
                   ins=[(pa, 1024, _col0), (pb, 1024, _col0), (u_big, 1024, lambda j: 5), (u_big, 1024, lambda j: 6)],
                   outs=[(1024, BF16, 1024, _col0)])[0]


def _merge_bwd(dmg, pa, pb, u_big, *, name, tile=512):
    rows = pa.shape[0]

    def fn(d, a, b, ga, gb):
        sa, sb = _sigmoid(ga), _sigmoid(gb)
        return d * sa, d * sb, d * a * sa * (1.0 - sa), d * b * sb * (1.0 - sb)

    return _rowmap(fn, name=name, rows=rows, tile=tile,
                   ins=[(dmg, 1024, _col0), (pa, 1024, _col0), (pb, 1024, _col0), (u_big, 1024, lambda j: 5),
                        (u_big, 1024, lambda j: 6)],
                   outs=[(1024, BF16, 1024, _col0)] * 4)


def _loss_bwd(y, target, *, name, tile=512):
    rows, d = y.shape

    def fn(yv, tv):
        err = yv - tv
        part = jnp.sum(jnp.sum(err * err, axis=-1, keepdims=True), axis=0, keepdims=True)
        return err * (1.0 / d), jnp.broadcast_to(part * (0.5 / d), (1, LANES))

    dy, part = _rowmap(fn, name=name, rows=rows, tile=tile, ins=[(y, d, _col0), (target, d, _col0)],
                       outs=[(d, F32, d, _col0)], accs=[(LANES, LANES, _col0)])
    return dy, part[0, 0]


def _small_slab(gs, dskip_ch, dalog_f, dalog_b, dbias_f, dbias_b, gkv, gqh, gkh, dconv_w, *, name):
    e_mat = _ssd_expand(False)
    full_names = ("ffn1_norm", "mix_norm", "q_a_norm", "conv_b", "ssm_norm", "ffn2_norm")
    full = [gs[n] for n in full_names]
    nf = len(full)

    def body(*refs):
        fulls = refs[:nf]
        (dsk_ref, e_ref, af_ref, ab_ref, bf_ref, bb_ref, gkv_ref, gqh_ref, gkh_ref, cw_ref, o_ref) = refs[nf:]
        o_ref[...] = jnp.zeros_like(o_ref)
        for n, r in zip(full_names, fulls):
            o_ref[SMALL_ROW[n]:SMALL_ROW[n] + 1, 0:r.shape[1]] = r[...]
        o_ref[SMALL_ROW["kv_a_norm"]:SMALL_ROW["kv_a_norm"] + 1, 0:KV_LORA] = gkv_ref[...]
        o_ref[SMALL_ROW["q_head_norm"]:SMALL_ROW["q_head_norm"] + 1, 0:LANES] = gqh_ref[...]
        o_ref[SMALL_ROW["k_head_norm"]:SMALL_ROW["k_head_norm"] + 1, 0:LANES] = gkh_ref[...]
        o_ref[SMALL_ROW["a_log_fwd"]:SMALL_ROW["a_log_fwd"] + 1, 0:LANES] = af_ref[...]
        o_ref[SMALL_ROW["a_log_bwd"]:SMALL_ROW["a_log_bwd"] + 1, 0:LANES] = pltpu.roll(ab_ref[...], 96, 1)
        o_ref[SMALL_ROW["dt_bias_fwd"]:SMALL_ROW["dt_bias_fwd"] + 1, 0:LANES] = bf_ref[...]
        o_ref[SMALL_ROW["dt_bias_bwd"]:SMALL_ROW["dt_bias_bwd"] + 1, 0:LANES] = pltpu.roll(bb_ref[...], 96, 1)
        dsk = _dot_h_nt(jnp.broadcast_to(dsk_ref[...], (8, D_INNER)), e_ref[...])
        o_ref[SMALL_ROW["d_skip"]:SMALL_ROW["d_skip"] + 1, 0:LANES] = dsk[0:1, :]
        o_ref[CONV_ROW:CONV_ROW + CONV_WIDTH, :] = cw_ref[...]

    return _pallas(body, name=name, out_shape=jax.ShapeDtypeStruct((SMALL_ROWS, SMALL_COLS), F32))(
        *full, dskip_ch, e_mat, dalog_f, dalog_b, dbias_f, dbias_b, gkv, gqh, gkh, dconv_w)


def _adamw_math(g, w, m, v):
    m2 = ADAM_B1 * m + (1.0 - ADAM_B1) * g
    v2 = ADAM_B2 * v + (1.0 - ADAM_B2) * (g * g)
    m_hat = m2 / (1.0 - ADAM_B1 ** ADAM_STEP)
    v_hat = v2 / (1.0 - ADAM_B2 ** ADAM_STEP)
    delta = -ADAM_LR * (m_hat / (jnp.sqrt(v_hat) + ADAM_EPS) + ADAM_WD * w)
    return delta, m2, v2


def _sum8(r_ref):
    g = r_ref[0].astype(F32)
    for s in range(1, N_DEV):
        g = g + r_ref[s].astype(F32)
    return g


def _reduce_adamw(recv, w, m, v, *, name, tile=256):
    _, R, C = recv.shape
    tile = _pick(R, tile) if R % LANES == 0 else R
    assert R % tile == 0

    def body(r_ref, w_ref, m_ref, v_ref, g_ref, d_ref, m2_ref, v2_ref):
        g = _sum8(r_ref)
        delta, m2, v2 = _adamw_math(g, w_ref[...], m_ref[...], v_ref[...])
        g_ref[...] = g
        d_ref[...] = delta
        m2_ref[...] = m2
        v2_ref[...] = v2

    blk = pl.BlockSpec((tile, C), lambda i: (i, 0))
    return _pallas(
        body, name=name, grid=(R // tile,),
        in_specs=[pl.BlockSpec((N_DEV, tile, C), lambda i: (0, i, 0)), blk, blk, blk], out_specs=[blk] * 4,
        out_shape=[jax.ShapeDtypeStruct((R, C), F32)] * 4, compiler_params=_params(("parallel",)),
    )(recv, w, m, v)


def _reduce_t_adamw(recv, w, m, v, *, name):
    R, cs = w.shape

    def body(r_ref, w_ref, m_ref, v_ref, g_ref, d_ref, m2_ref, v2_ref):
        g = _sum8(r_ref).T
        delta, m2, v2 = _adamw_math(g, w_ref[...], m_ref[...], v_ref[...])
        g_ref[...] = g
        d_ref[...] = delta
        m2_ref[...] = m2
        v2_ref[...] = v2

    return _pallas(body, name=name, out_shape=[jax.ShapeDtypeStruct((R, cs), F32)] * 4,
                   compiler_params=pltpu.CompilerParams(vmem_limit_bytes=VMEM_LIMIT))(recv, w, m, v)


def _reduce8(recv, *, name, tile):
    _, R, C = recv.shape

    def body(r_ref, g_ref):
        g_ref[...] = _sum8(r_ref)

    return _pallas(body, name=name, grid=(R // tile,),
                   in_specs=[pl.BlockSpec((N_DEV, tile, C), lambda i: (0, i, 0))],
                   out_specs=pl.BlockSpec((tile, C), lambda i: (i, 0)),
                   out_shape=jax.ShapeDtypeStruct((R, C), F32), compiler_params=_params(("parallel",)))(recv)


def _adamw(g, w, m, v, *, name, tile=256):
    R, C = w.shape

    def body(g_ref, w_ref, m_ref, v_ref, d_ref, m2_ref, v2_ref):
        delta, m2, v2 = _adamw_math(g_ref[...], w_ref[...], m_ref[...], v_ref[...])
        d_ref[...] = delta
        m2_ref[...] = m2
        v2_ref[...] = v2

    blk = pl.BlockSpec((tile, C), lambda i: (i, 0))
    return _pallas(body, name=name, grid=(R // tile,), in_specs=[blk] * 4, out_specs=[blk] * 3,
                   out_shape=[jax.ShapeDtypeStruct((R, C), F32)] * 3, compiler_params=_params(("parallel",)))(g, w, m, v)


def _adamw_small(srecv, conv_g, ws, ms, vs, *, name):
    n = len(ws)

    def body(*refs):
        s_ref, c_ref = refs[0], refs[1]
        w_refs, m_refs, v_refs = refs[2:2 + n], refs[2 + n:2 + 2 * n], refs[2 + 2 * n:2 + 3 * n]
        outs = refs[2 + 3 * n:]
        gsum = _sum8(s_ref)
        for i in range(n):
            if i < len(SMALL):
                g = gsum[i:i + 1, 0:SMALL[i][1]]
            else:
                g = _sum8(c_ref)
            delta, m2, v2 = _adamw_math(g, w_refs[i][...], m_refs[i][...], v_refs[i][...])
            outs[i][...] = g
            outs[n + i][...] = delta
            outs[2 * n + i][...] = m2
            outs[3 * n + i][...] = v2

    shapes = [jax.ShapeDtypeStruct(w.shape, F32) for w in ws]
    got = _pallas(body, name=name, out_shape=shapes * 4,
                  compiler_params=pltpu.CompilerParams(vmem_limit_bytes=VMEM_LIMIT))(srecv, conv_g, *ws, *ms, *vs)
    return got[:n], got[n:2 * n], got[2 * n:3 * n], got[3 * n:]


def _ffn_fwd(x, norm, w_gu, w_d, tag):
    h = _rms_fwd(x, norm, name=f"{tag}_rms")
    gu = _mm(h, w_gu, name=f"{tag}_gu")
    act = _swiglu_fwd(gu, name=f"{tag}_act")
    out = _mm(act, w_d, name=f"{tag}_down", alpha=0.5, res=x)
    return out, (h, gu, act)


def _ffn_bwd(dout, x, norm, w_gu, w_d, saved, tag):
    h, gu, act = saved
    d_act = _mm(dout, w_d, name=f"{tag}_dact", tb=True, alpha=0.5)
    d_wd = _mm(act, dout, name=f"{tag}_dwd", ta=True, alpha=0.5, tm=1408, tn=1024, out_dtype=BF16)
    dg, du = _swiglu_bwd(d_act, gu, name=f"{tag}_dswiglu")
    d_wg_t = _mm(dg, h, name=f"{tag}_dwg", ta=True, tm=1408, tn=1024, out_dtype=BF16)
    d_wu_t = _mm(du, h, name=f"{tag}_dwu", ta=True, tm=1408, tn=1024, out_dtype=BF16)
    dh = _mm(dg, w_gu[:, :D_FF], name=f"{tag}_dh_g", tb=True)
    dh = _mm(du, w_gu[:, D_FF:], name=f"{tag}_dh_u", tb=True, res=dh)
    dx, dnorm = _rms_bwd(dh, x, norm, dout, name=f"{tag}_drms")
    return dx, dnorm, d_wg_t, d_wu_t, d_wd


def _rope_tables(positions, T):
    pos = positions.reshape(T).astype(F32)
    inv_freq = 1.0 / (ROPE_BASE ** (jnp.arange(0, QK_ROPE, 2, dtype=F32) / QK_ROPE))
    ang = pos[:, None] * inv_freq
    cos, sin = jnp.cos(ang), jnp.sin(ang)
    one64, z64 = jnp.ones((T, 64), F32), jnp.zeros((T, 64), F32)
    z16, z32, one32 = jnp.zeros((T, 16), F32), jnp.zeros((T, 32), F32), jnp.ones((T, 32), F32)
    c = jnp.concatenate([one64, cos, cos, one32], axis=1)
    s1 = jnp.concatenate([z64, -sin, z16, z32], axis=1)
    s2 = jnp.concatenate([z64, z16, sin, z32], axis=1)
    return c, s1, s2


def _cols(g):
    n, r, cs = g.shape
    return g.transpose(1, 0, 2).reshape(r, n * cs)


def _rows(g):
    n, rs, c = g.shape
    return g.reshape(n * rs, c)


def _pad_lanes(v, n=LANES):
    return jnp.pad(v, ((0, 0), (0, n - v.shape[1])))


def _in_proj_weights(w_in):
    z = lambda n: jnp.zeros((D_MODEL, n), w_in.dtype)
    w_small = jnp.concatenate([w_in[:, 0:384], z(128), w_in[:, 384:640], w_in[:, 640:672], z(96),
                               w_in[:, 5792:5856], z(64)], axis=1)
    w_big = jnp.concatenate([w_in[:, 672:2720], w_in[:, 2720:5792], w_in[:, 5856:7904]], axis=1)
    return w_small, w_big


def _mla_up_weights(w_q_b, w_kv_b):
    wq = w_q_b.reshape(Q_LORA, N_HEADS, QK_HEAD)
    wq = jnp.pad(wq, ((0, 0), (0, 0), (0, LANES - QK_HEAD))).reshape(Q_LORA, N_HEADS * LANES)
    wkv = w_kv_b.reshape(KV_LORA, N_HEADS, QK_NOPE + V_HEAD)
    wk = jnp.pad(wkv[..., :QK_NOPE], ((0, 0), (0, 0), (0, LANES - QK_NOPE))).reshape(KV_LORA, N_HEADS * LANES)
    v = wkv[..., QK_NOPE:]
    zv = jnp.zeros_like(v)
    even = (jnp.arange(N_HEADS) % 2 == 0)[None, :, None]
    wv = jnp.where(even, jnp.concatenate([v, zv], -1), jnp.concatenate([zv, v], -1)).reshape(KV_LORA, N_HEADS * LANES)
    return wq, wk, wv


def _shard_rows(g):
    return g.reshape(N_DEV, g.shape[0] // N_DEV, g.shape[1])


def kernel(x, positions, ffn1_norm, ffn1_w_gate, ffn1_w_up, ffn1_w_down, mix_norm, w_in, q_a_norm, w_q_b, kv_a_norm, w_kv_b, q_head_norm, k_head_norm, conv_w, conv_b, a_log_fwd, a_log_bwd, dt_bias_fwd, dt_bias_bwd, d_skip, ssm_norm, w_attn_branch, w_ssm_branch, w_out, ffn2_norm, ffn2_w_gate, ffn2_w_up, ffn2_w_down, loss_target, m_ffn1_norm, m_ffn1_w_gate, m_ffn1_w_up, m_ffn1_w_down, m_mix_norm, m_w_in, m_q_a_norm, m_w_q_b, m_kv_a_norm, m_w_kv_b, m_q_head_norm, m_k_head_norm, m_conv_w, m_conv_b, m_a_log_fwd, m_a_log_bwd, m_dt_bias_fwd, m_dt_bias_bwd, m_d_skip, m_ssm_norm, m_w_attn_branch, m_w_ssm_branch, m_w_out, m_ffn2_norm, m_ffn2_w_gate, m_ffn2_w_up, m_ffn2_w_down, v_ffn1_norm, v_ffn1_w_gate, v_ffn1_w_up, v_ffn1_w_down, v_mix_norm, v_w_in, v_q_a_norm, v_w_q_b, v_kv_a_norm, v_w_kv_b, v_q_head_norm, v_k_head_norm, v_conv_w, v_conv_b, v_a_log_fwd, v_a_log_bwd, v_dt_bias_fwd, v_dt_bias_bwd, v_d_skip, v_ssm_norm, v_w_attn_branch, v_w_ssm_branch, v_w_out, v_ffn2_norm, v_ffn2_w_gate, v_ffn2_w_up, v_ffn2_w_down):
    w_all = dict(ffn1_norm=ffn1_norm, ffn1_w_gate=ffn1_w_gate, ffn1_w_up=ffn1_w_up, ffn1_w_down=ffn1_w_down, mix_norm=mix_norm, w_in=w_in, q_a_norm=q_a_norm, w_q_b=w_q_b, kv_a_norm=kv_a_norm, w_kv_b=w_kv_b, q_head_norm=q_head_norm, k_head_norm=k_head_norm, conv_w=conv_w, conv_b=conv_b, a_log_fwd=a_log_fwd, a_log_bwd=a_log_bwd, dt_bias_fwd=dt_bias_fwd, dt_bias_bwd=dt_bias_bwd, d_skip=d_skip, ssm_norm=ssm_norm, w_attn_branch=w_attn_branch, w_ssm_branch=w_ssm_branch, w_out=w_out, ffn2_norm=ffn2_norm, ffn2_w_gate=ffn2_w_gate, ffn2_w_up=ffn2_w_up, ffn2_w_down=ffn2_w_down)
    m_all = dict(ffn1_norm=m_ffn1_norm, ffn1_w_gate=m_ffn1_w_gate, ffn1_w_up=m_ffn1_w_up, ffn1_w_down=m_ffn1_w_down, mix_norm=m_mix_norm, w_in=m_w_in, q_a_norm=m_q_a_norm, w_q_b=m_w_q_b, kv_a_norm=m_kv_a_norm, w_kv_b=m_w_kv_b, q_head_norm=m_q_head_norm, k_head_norm=m_k_head_norm, conv_w=m_conv_w, conv_b=m_conv_b, a_log_fwd=m_a_log_fwd, a_log_bwd=m_a_log_bwd, dt_bias_fwd=m_dt_bias_fwd, dt_bias_bwd=m_dt_bias_bwd, d_skip=m_d_skip, ssm_norm=m_ssm_norm, w_attn_branch=m_w_attn_branch, w_ssm_branch=m_w_ssm_branch, w_out=m_w_out, ffn2_norm=m_ffn2_norm, ffn2_w_gate=m_ffn2_w_gate, ffn2_w_up=m_ffn2_w_up, ffn2_w_down=m_ffn2_w_down)
    v_all = dict(ffn1_norm=v_ffn1_norm, ffn1_w_gate=v_ffn1_w_gate, ffn1_w_up=v_ffn1_w_up, ffn1_w_down=v_ffn1_w_down, mix_norm=v_mix_norm, w_in=v_w_in, q_a_norm=v_q_a_norm, w_q_b=v_w_q_b, kv_a_norm=v_kv_a_norm, w_kv_b=v_w_kv_b, q_head_norm=v_q_head_norm, k_head_norm=v_k_head_norm, conv_w=v_conv_w, conv_b=v_conv_b, a_log_fwd=v_a_log_fwd, a_log_bwd=v_a_log_bwd, dt_bias_fwd=v_dt_bias_fwd, dt_bias_bwd=v_dt_bias_bwd, d_skip=v_d_skip, ssm_norm=v_ssm_norm, w_attn_branch=v_w_attn_branch, w_ssm_branch=v_w_ssm_branch, w_out=v_w_out, ffn2_norm=v_ffn2_norm, ffn2_w_gate=v_ffn2_w_gate, ffn2_w_up=v_ffn2_w_up, ffn2_w_down=v_ffn2_w_down)
    T = x.shape[1]
    xs_in, target = x[0], loss_target[0]
    w2 = {n: a.reshape(-1, a.shape[-1]) for n, a in w_all.items()}
    m2 = {n: m_all[n].reshape(w2[n].shape) for n in w2}
    v2 = {n: v_all[n].reshape(w2[n].shape) for n in w2}
    p = {n: w2[n] for n, _ in SMALL}
    bf = lambda n: w2[n].astype(BF16)

    early = ["ffn1_w_gate", "ffn1_w_up", "ffn1_w_down", "w_in", "w_q_b", "w_kv_b"]
    g_early = _all_gather_two_level([bf(n) for n in early] + [w2["conv_w"]], name="gather_early")
    ge = dict(zip(early + ["conv_w"], g_early))
    w_gu1 = jnp.concatenate([_cols(ge["ffn1_w_gate"]), _cols(ge["ffn1_w_up"])], axis=1)
    w_d1 = _rows(ge["ffn1_w_down"])
    w_small, w_big = _in_proj_weights(_cols(ge["w_in"]))
    wq, wk, wv = _mla_up_weights(_cols(ge["w_q_b"]), _cols(ge["w_kv_b"]))
    conv_full = _cols(ge["conv_w"])
    late = ["w_attn_branch", "w_ssm_branch", "w_out", "ffn2_w_gate", "ffn2_w_up", "ffn2_w_down"]
    late_shards = [bf(n) for n in late]

    tabs = _rope_tables(positions, T)
    qg, kg = _pad_lanes(p["q_head_norm"]), _pad_lanes(p["k_head_norm"])
    bias128 = _pad_lanes(jnp.concatenate([p["dt_bias_fwd"], p["dt_bias_bwd"]], axis=1))
    alog128 = _pad_lanes(jnp.concatenate([p["a_log_fwd"], p["a_log_bwd"]], axis=1))
    skip_x = jnp.repeat(p["d_skip"], 64, axis=1)

    x1, ffn1_saved = _ffn_fwd(xs_in, p["ffn1_norm"], w_gu1, w_d1, "ffn1")
    h2 = _rms_fwd(x1, p["mix_norm"], name="mix_rms")
    u_big = _mm(h2, w_big, name="in_big")
    u_small = _mm(h2, w_small, name="in_small")
    cqn, ckvn = _lora_norm_fwd(u_small, p["q_a_norm"], p["kv_a_norm"], name="lora_norm")
    q_raw = _mm(cqn, wq, name="q_up")
    k_raw = _mm(ckvn, wk, name="k_up")
    v_raw = _mm(ckvn, wv, name="v_up")
    q, k, v = _qk_prep_fwd(q_raw, k_raw, v_raw, u_small, tabs, qg, kg, name="qk_prep")
    a_out, lse, g_late = _attn_fwd(q, k, v, ["gather"] * len(late), late_shards, name="attn_fwd")
    gl = dict(zip(late, g_late))
    w_pa, w_pb, w_o = _rows(gl["w_attn_branch"]), _rows(gl["w_ssm_branch"]), _rows(gl["w_out"])
    w_gu2 = jnp.concatenate([_cols(gl["ffn2_w_gate"]), _cols(gl["ffn2_w_up"])], axis=1)
    w_d2 = _rows(gl["ffn2_w_down"])
    xbc_act = _conv_fwd(u_big, conv_full, p["conv_b"], name="conv_fwd")
    y_f, hin_f = _ssd_fwd(xbc_act, u_small, bias128, alog128, rev=False, name="ssd_fwd_f")
    y_b, hin_b = _ssd_fwd(xbc_act, u_small, bias128, alog128, rev=True, name="ssd_fwd_b")
    m_out = _ssm_out_fwd(y_f, y_b, xbc_act, u_big, skip_x, p["ssm_norm"], name="ssm_out")
    pa = _mm(a_out, w_pa, name="branch_a")
    pb = _mm(m_out, w_pb, name="branch_b")
    merged = _merge_fwd(pa, pb, u_big, name="merge")
    x2 = _mm(merged, w_o, name="mix_out", res=x1)
    y, ffn2_saved = _ffn_fwd(x2, p["ffn2_norm"], w_gu2, w_d2, "ffn2")
    dy, loss_part = _loss_bwd(y, target, name="loss")
    loss = lax.psum(loss_part, ("x", "y", "c"))

    gs = {}
    dx2, gs["ffn2_norm"], g_gate2, g_up2, g_down2 = _ffn_bwd(dy, x2, p["ffn2_norm"], w_gu2, w_d2, ffn2_saved, "ffn2b")
    dmerged = _mm(dx2, w_o, name="d_merged", tb=True)
    g_out = _mm(merged, dx2, name="d_w_out", ta=True, out_dtype=BF16)
    dpa, dpb, dga, dgb = _merge_bwd(dmerged, pa, pb, u_big, name="d_merge")
    g_pa = _mm(a_out, dpa, name="d_w_pa", ta=True, out_dtype=BF16)
    g_pb = _mm(m_out, dpb, name="d_w_pb", ta=True, out_dtype=BF16)
    da_out = _mm(dpa, w_pa, name="d_a", tb=True)
    dm_out = _mm(dpb, w_pb, name="d_m", tb=True)
    late_grads = [_shard_rows(g) for g in (g_pa, g_pb, g_out, g_gate2, g_up2, g_down2)]
    dq, dk, dv, r_late = _attn_bwd(q, k, v, a_out, lse, da_out, ["scatter"] * len(late_grads), late_grads,
                                   name="attn_bwd")
    recv = dict(zip(late, r_late))

    dyss, dz, gs["ssm_norm"], dskip_ch = _ssm_out_bwd(dm_out, y_f, y_b, xbc_act, u_big, skip_x, p["ssm_norm"],
                                                      name="d_ssm_out")
    dact_f, draw_f, dalog_f, dbias_f = _ssd_bwd(dyss, xbc_act, u_small, bias128, alog128, hin_f, skip_x,
                                                rev=False, name="ssd_bwd_f")
    dact_b, draw_b, dalog_b, dbias_b = _ssd_bwd(dyss, xbc_act, u_small, bias128, alog128, hin_b, None,
                                                rev=True, name="ssd_bwd_b")
    dxbc, g_conv, gs["conv_b"] = _conv_bwd(dact_f, dact_b, u_big, conv_full, p["conv_b"], name="conv_bwd")

    dq_raw, dk_raw, dv_raw, dkpe, gqh, gkh = _qk_prep_bwd(dq, dk, dv, q_raw, k_raw, u_small, tabs, qg, kg,
                                                          name="d_qk_prep")
    g_wq_t = _mm(dq_raw, cqn, name="d_w_q", ta=True, out_dtype=BF16)
    g_wk_t = _mm(dk_raw, ckvn, name="d_w_k", ta=True, out_dtype=BF16)
    g_wv_t = _mm(dv_raw, ckvn, name="d_w_v", ta=True, out_dtype=BF16)
    dcqn = _mm(dq_raw, wq, name="d_cqn", tb=True)
    dckvn = _mm(dk_raw, wk, name="d_ckvn_k", tb=True)
    dckvn = _mm(dv_raw, wv, name="d_ckvn_v", tb=True, res=dckvn)
    du_small, gs["q_a_norm"], gkv = _lora_norm_bwd(dcqn, dckvn, u_small, p["q_a_norm"], p["kv_a_norm"], dkpe,
                                                   draw_f, draw_b, name="d_lora_norm")

    dh2 = _mm(du_small, w_small, name="d_h2_small", tb=True)
    dh2 = _mm(dz, w_big[:, 0:2048], name="d_h2_z", tb=True, res=dh2)
    dh2 = _mm(dxbc, w_big[:, 2048:5120], name="d_h2_xbc", tb=True, res=dh2)
    dh2 = _mm(dga, w_big[:, 5120:6144], name="d_h2_ga", tb=True, res=dh2)
    dh2 = _mm(dgb, w_big[:, 6144:7168], name="d_h2_gb", tb=True, res=dh2)
    gt_small = _mm(du_small, h2, name="d_w_small", ta=True, out_dtype=BF16)
    gt_z = _mm(dz, h2, name="d_w_z", ta=True, out_dtype=BF16)
    gt_xbc = _mm(dxbc, h2, name="d_w_xbc", ta=True, out_dtype=BF16)
    gt_ga = _mm(dga, h2, name="d_w_ga", ta=True, out_dtype=BF16)
    gt_gb = _mm(dgb, h2, name="d_w_gb", ta=True, out_dtype=BF16)
    dx1, gs["mix_norm"] = _rms_bwd(dh2, x1, p["mix_norm"], dx2, name="d_mix_rms")
    grad_x, gs["ffn1_norm"], g_gate1, g_up1, g_down1 = _ffn_bwd(dx1, xs_in, p["ffn1_norm"], w_gu1, w_d1, ffn1_saved,
                                                                "ffn1b")

    gt_in = jnp.concatenate([gt_small[0:384], gt_small[U_CKV:U_KPE + QK_ROPE], gt_z, gt_xbc,
                             gt_small[U_DT:U_DT + 64], gt_ga, gt_gb], axis=0)
    gt_in = jnp.pad(gt_in.reshape(N_DEV, W_IN_SHARD, D_MODEL), ((0, 0), (0, W_IN_SHARD_PAD - W_IN_SHARD), (0, 0)))
    gt_q = g_wq_t.reshape(N_HEADS, LANES, Q_LORA)[:, :QK_HEAD].reshape(N_DEV, -1, Q_LORA)
    gk3 = g_wk_t.reshape(N_HEADS, LANES, KV_LORA)[:, :QK_NOPE]
    gv3 = g_wv_t.reshape(N_HEADS, LANES, KV_LORA)
    even = (jnp.arange(N_HEADS) % 2 == 0)[:, None, None]
    gv3 = jnp.where(even, gv3[:, :V_HEAD], gv3[:, V_HEAD:])
    gt_kv = jnp.concatenate([gk3, gv3], axis=1).reshape(N_DEV, -1, KV_LORA)
    gsmall = _small_slab(gs, dskip_ch, dalog_f, dalog_b, dbias_f, dbias_b, gkv, gqh, gkh, g_conv, name="small_slab")
    last = ["ffn1_w_gate", "ffn1_w_up", "ffn1_w_down", "w_in", "w_q_b", "w_kv_b"]
    last_grads = [_shard_rows(g_gate1), _shard_rows(g_up1), _shard_rows(g_down1), gt_in, gt_q, gt_kv]
    r_last = _exchange(["scatter"] * len(last) + ["gather"], last_grads + [gsmall], name="grad_exchange")
    recv.update(zip(last, r_last[:-1]))
    srecv = r_last[-1]

    out = {}
    for n in ("ffn1_w_down", "ffn2_w_down", "w_attn_branch", "w_ssm_branch", "w_out"):
        out[n] = _reduce_adamw(recv[n], w2[n], m2[n], v2[n], name=f"adamw_{n}")
    for n in ("ffn1_w_gate", "ffn1_w_up", "ffn2_w_gate", "ffn2_w_up", "w_q_b", "w_kv_b"):
        out[n] = _reduce_t_adamw(recv[n], w2[n], m2[n], v2[n], name=f"adamw_{n}")
    g_in = _reduce8(recv["w_in"], name="sum_w_in", tile=W_IN_SHARD_PAD // 2)[:W_IN_SHARD].T
    out["w_in"] = [g_in] + list(_adamw(g_in, w2["w_in"], m2["w_in"], v2["w_in"], name="adamw_w_in"))
    me = 4 * lax.axis_index("x") + 2 * lax.axis_index("y") + lax.axis_index("c")
    conv_g = lax.dynamic_slice(srecv, (0, CONV_ROW, me * (XBC_DIM // N_DEV)), (N_DEV, CONV_WIDTH, XBC_DIM // N_DEV))
    sn = [n for n, _ in SMALL] + ["conv_w"]
    sg, sd, sm, sv = _adamw_small(srecv, conv_g, [w2[n] for n in sn], [m2[n] for n in sn], [v2[n] for n in sn],
                                  name="adamw_small")
    for i, n in enumerate(sn):
        out[n] = (sg[i], sd[i], sm[i], sv[i])
    outs = [[out[n][kind].reshape(w_all[n].shape) for n in WEIGHT_ORDER] for kind in range(4)]
    return (loss, grad_x[None], *outs[0], *outs[1], *outs[2], *outs[3])
```

```python
import math

import jax
import jax.numpy as jnp
from jax import lax
from jax.experimental import pallas as pl
from jax.experimental.pallas import tpu as pltpu

F32, BF16 = jnp.float32, jnp.bfloat16
HIGHEST = lax.Precision.HIGHEST

D_MODEL, D_FF = 1024, 2816
EPS = 1e-6
N_HEADS, QK_NOPE, QK_ROPE, QK_HEAD, V_HEAD = 16, 64, 32, 96, 64
Q_LORA, KV_LORA = 384, 256
ROPE_BASE = 10000.0
D_INNER, SSM_HEADS, SSM_GROUPS, D_STATE, CONV_WIDTH, CHUNK = 2048, 32, 4, 128, 5, 128
XBC_DIM = D_INNER + 2 * SSM_GROUPS * D_STATE
IN_DIM = 7904
ADAM_LR, ADAM_B1, ADAM_B2, ADAM_EPS, ADAM_WD, ADAM_STEP = 0.001, 0.9, 0.999, 1e-08, 0.01, 10
N_DEV = 8

V7X_VMEM_BYTES = 64 * 1024 * 1024
VMEM_LIMIT = V7X_VMEM_BYTES - 8 * 1024 * 1024
LANES = 128
W_IN_SHARD = IN_DIM // N_DEV
W_IN_SHARD_PAD = 992

SMALL = (
    ("ffn1_norm", 1024), ("mix_norm", 1024), ("q_a_norm", 384), ("kv_a_norm", 256), ("q_head_norm", 96),
    ("k_head_norm", 96), ("conv_b", 3072), ("a_log_fwd", 32), ("a_log_bwd", 32), ("dt_bias_fwd", 32),
    ("dt_bias_bwd", 32), ("d_skip", 32), ("ssm_norm", 2048), ("ffn2_norm", 1024),
)
SMALL_ROW = {n: i for i, (n, _) in enumerate(SMALL)}
CONV_ROW = len(SMALL)
SMALL_ROWS, SMALL_COLS = 24, XBC_DIM
WEIGHT_ORDER = (
    "ffn1_norm", "ffn1_w_gate", "ffn1_w_up", "ffn1_w_down", "mix_norm", "w_in", "q_a_norm", "w_q_b", "kv_a_norm",
    "w_kv_b", "q_head_norm", "k_head_norm", "conv_w", "conv_b", "a_log_fwd", "a_log_bwd", "dt_bias_fwd", "dt_bias_bwd",
    "d_skip", "ssm_norm", "w_attn_branch", "w_ssm_branch", "w_out", "ffn2_norm", "ffn2_w_gate", "ffn2_w_up",
    "ffn2_w_down",
)


def _pallas(body, **kw):
    return pl.pallas_call(body, **kw)


def _params(sem):
    return pltpu.CompilerParams(dimension_semantics=sem, vmem_limit_bytes=VMEM_LIMIT)


def _pick(dim, pref):
    if dim <= pref:
        return dim
    c = (pref // LANES) * LANES
    while c >= LANES:
        if dim % c == 0:
            return c
        c -= LANES
    raise ValueError((dim, pref))


def _sigmoid(x):
    return 1.0 / (1.0 + jnp.exp(-x))


def _softplus(x):
    return jnp.maximum(x, 0.0) + jnp.log(1.0 + jnp.exp(-jnp.abs(x)))


def _dot(a, b):
    return jnp.dot(a, b, preferred_element_type=F32)


def _dot_nt(a, b):
    return lax.dot_general(a, b, (((1,), (1,)), ((), ())), preferred_element_type=F32)


def _dot_tn(a, b):
    return lax.dot_general(a, b, (((0,), (0,)), ((), ())), preferred_element_type=F32)


def _dot_h(a, b):
    return jnp.dot(a, b, preferred_element_type=F32, precision=HIGHEST)


def _dot_h_nt(a, b):
    return lax.dot_general(a, b, (((1,), (1,)), ((), ())), preferred_element_type=F32, precision=HIGHEST)


def _dot_h_tn(a, b):
    return lax.dot_general(a, b, (((0,), (0,)), ((), ())), preferred_element_type=F32, precision=HIGHEST)


def _mesh_pos():
    return lax.axis_index("x"), lax.axis_index("y"), lax.axis_index("c")


def _comm_scratch(n):
    return [pltpu.SemaphoreType.DMA((7 * n,)), pltpu.SemaphoreType.DMA((7 * n,)), pltpu.SemaphoreType.DMA((n,))]


def _comm_copies(modes, srcs, dsts, send_sems, recv_sems, local_sems, arrivals):
    x, y, c = _mesh_pos()
    me = 4 * x + 2 * y + c
    local, remote = [], []
    for w, (mode, s, d) in enumerate(zip(modes, srcs, dsts)):
        gather = mode == "gather"
        if not arrivals:
            local.append(pltpu.make_async_copy(s if gather else s.at[me], d.at[me], local_sems.at[w]))
        for k in range(1, N_DEV):
            px = (1 - x) if (k & 4) else x
            py = (1 - y) if (k & 2) else y
            pc = (1 - c) if (k & 1) else c
            peer = 4 * px + 2 * py + pc
            idx = 7 * w + k - 1
            remote.append(pltpu.make_async_remote_copy(
                src_ref=s if gather else s.at[peer], dst_ref=d.at[peer] if arrivals else d.at[me],
                send_sem=send_sems.at[idx], recv_sem=recv_sems.at[idx],
                device_id=(px, py, pc), device_id_type=pl.DeviceIdType.MESH))
    return local, remote


def _comm_start(modes, srcs, dsts, sems):
    local, sends = _comm_copies(modes, srcs, dsts, *sems, arrivals=False)
    for cp in local + sends:
        cp.start()


def _comm_wait(modes, srcs, dsts, sems):
    _, recvs = _comm_copies(modes, srcs, dsts, *sems, arrivals=True)
    for cp in recvs:
        cp.wait_recv()
    local, sends = _comm_copies(modes, srcs, dsts, *sems, arrivals=False)
    for cp in sends:
        cp.wait_send()
    for cp in local:
        cp.wait()


def _comm_out_shapes(modes, arrays):
    return [jax.ShapeDtypeStruct((N_DEV,) + (a.shape if m == "gather" else a.shape[1:]), a.dtype)
            for m, a in zip(modes, arrays)]


def _exchange(modes, arrays, *, name):
    n = len(arrays)

    def body(*refs):
        srcs, dsts, sems = refs[:n], refs[n:2 * n], refs[2 * n:]
        _comm_start(modes, srcs, dsts, sems)
        _comm_wait(modes, srcs, dsts, sems)

    any_spec = pl.BlockSpec(memory_space=pl.ANY)
    return _pallas(body, name=name, out_shape=_comm_out_shapes(modes, arrays), in_specs=[any_spec] * n,
                   out_specs=[any_spec] * n, scratch_shapes=_comm_scratch(n))(*arrays)


def _all_gather_two_level(shards, *, name):
    n = len(shards)

    def body(*refs):
        srcs, outs = refs[:n], refs[n:2 * n]
        send_sems, recv_sems, local_sems = refs[2 * n:]
        x, y, c = _mesh_pos()
        me, sibling = (x, y, c), (x, y, 1 - c)
        chips = [(1 - x, y), (x, 1 - y), (1 - x, 1 - y)]

        def blk(w, px, py, pc):
            return outs[w].at[4 * px + 2 * py + pc]

        def copy(w, k, block, to, src=None):
            return pltpu.make_async_remote_copy(
                src_ref=blk(w, *block) if src is None else src, dst_ref=blk(w, *block),
                send_sem=send_sems.at[7 * w + k], recv_sem=recv_sems.at[7 * w + k], device_id=to,
                device_id_type=pl.DeviceIdType.MESH)

        mine = [pltpu.make_async_copy(srcs[w], blk(w, *me), local_sems.at[w]) for w in range(n)]
        for cp in mine:
            cp.start()
        first = []
        for w in range(n):
            first.append(copy(w, 0, me, sibling, src=srcs[w]))
            first += [copy(w, 1 + j, me, (*chip, c), src=srcs[w]) for j, chip in enumerate(chips)]
        for cp in first:
            cp.start()
        passed = []
        for w in range(n):
            for j, chip in enumerate(chips):
                copy(w, 1 + j, (*chip, c), me).wait_recv()
                fwd = copy(w, 4 + j, (*chip, c), sibling)
                fwd.start()
                passed.append(fwd)
        for w in range(n):
            copy(w, 0, sibling, me).wait_recv()
            for j, chip in enumerate(chips):
                copy(w, 4 + j, (*chip, 1 - c), me).wait_recv()
        for cp in first + passed:
            cp.wait_send()
        for cp in mine:
            cp.wait()

    any_spec = pl.BlockSpec(memory_space=pl.ANY)
    return _pallas(body, name=name, out_shape=_comm_out_shapes(["gather"] * n, shards), in_specs=[any_spec] * n,
                   out_specs=[any_spec] * n, scratch_shapes=_comm_scratch(n))(*shards)


def _mm(a, b, *, name, ta=False, tb=False, out_dtype=F32, alpha=1.0, res=None, tm=1024, tn=1408, tk=1408):
    (K, M) = a.shape if ta else a.shape[::-1]
    (N, Kb) = b.shape if tb else b.shape[::-1]
    assert K == Kb, (a.shape, b.shape, ta, tb)
    tm, tn, tk = _pick(M, tm), _pick(N, tn), _pick(K, tk)
    nk = K // tk
    a_spec = pl.BlockSpec((tk, tm), lambda i, j, k: (k, i)) if ta else pl.BlockSpec((tm, tk), lambda i, j, k: (i, k))
    b_spec = pl.BlockSpec((tn, tk), lambda i, j, k: (j, k)) if tb else pl.BlockSpec((tk, tn), lambda i, j, k: (k, j))
    o_spec = pl.BlockSpec((tm, tn), lambda i, j, k: (i, j))
    dn = (((0 if ta else 1,), (1 if tb else 0,)), ((), ()))
    has_res = res is not None

    def body(*refs):
        a_ref, b_ref = refs[0], refs[1]
        r_ref = refs[2] if has_res else None
        o_ref = refs[3] if has_res else refs[2]
        part = lax.dot_general(a_ref[...].astype(BF16), b_ref[...].astype(BF16), dn, preferred_element_type=F32)

        def finish(acc):
            if alpha != 1.0:
                acc = acc * alpha
            if has_res:
                acc = acc + r_ref[...]
            o_ref[...] = acc.astype(o_ref.dtype)

        if nk == 1:
            finish(part)
        else:
            acc_ref = refs[-1]
            k = pl.program_id(2)

            @pl.when(k == 0)
            def _():
                acc_ref[...] = part

            @pl.when(k > 0)
            def _():
                acc_ref[...] += part

            @pl.when(k == nk - 1)
            def _():
                finish(acc_ref[...])

    ins = [a, b] + ([res] if has_res else [])
    in_specs = [a_spec, b_spec] + ([o_spec] if has_res else [])
    return _pallas(
        body, name=name, grid=(M // tm, N // tn, nk), in_specs=in_specs, out_specs=o_spec,
        out_shape=jax.ShapeDtypeStruct((M, N), out_dtype),
        scratch_shapes=[pltpu.VMEM((tm, tn), F32)] if nk > 1 else [],
        compiler_params=_params(("parallel", "parallel", "arbitrary")),
    )(*ins)


def _col0(j):
    return 0


def _colj(j):
    return j


def _rowmap(fn, *, name, rows, tile, ins, consts=(), outs=(), accs=(), ncol=1):
    tile = min(tile, rows)
    nrow = rows // tile
    in_specs = [pl.BlockSpec((tile, w), lambda j, i, f=f: (i, f(j))) for _, w, f in ins]
    for arr, w, f in consts:
        in_specs.append(pl.BlockSpec((arr.shape[0], w), lambda j, i, f=f: (0, f(j))))
    out_specs = [pl.BlockSpec((tile, w), lambda j, i, f=f: (i, f(j))) for _, _, w, f in outs]
    out_specs += [pl.BlockSpec((1, w), lambda j, i, f=f: (0, f(j))) for _, w, f in accs]
    out_shape = [jax.ShapeDtypeStruct((rows, c), dt) for c, dt, _, _ in outs]
    out_shape += [jax.ShapeDtypeStruct((1, c), F32) for c, _, _ in accs]
    n_in, n_out = len(ins) + len(consts), len(outs)
    acc_fixed = [f is _col0 for _, _, f in accs]

    def body(*refs):
        res = fn(*[r[...].astype(F32) for r in refs[:n_in]])
        if not isinstance(res, (tuple, list)):
            res = (res,)
        for r, v in zip(refs[n_in:n_in + n_out], res[:n_out]):
            r[...] = v.astype(r.dtype)
        j, i = pl.program_id(0), pl.program_id(1)
        for r, v, fixed in zip(refs[n_in + n_out:], res[n_out:], acc_fixed):
            first = ((i == 0) & (j == 0)) if fixed else (i == 0)

            @pl.when(first)
            def _(r=r, v=v):
                r[...] = v

            @pl.when(jnp.logical_not(first))
            def _(r=r, v=v):
                r[...] += v

    arrays = [a for a, _, _ in ins] + [a for a, _, _ in consts]
    return _pallas(
        body, name=name, grid=(ncol, nrow), in_specs=in_specs, out_specs=out_specs, out_shape=out_shape,
        compiler_params=_params(("arbitrary", "arbitrary")),
    )(*arrays)


def _rms_fwd(x, g, *, name, tile=512):
    rows, d = x.shape

    def fn(xv, gv):
        r = lax.rsqrt(jnp.mean(xv * xv, axis=-1, keepdims=True) + EPS)
        return xv * r * gv

    return _rowmap(fn, name=name, rows=rows, tile=tile, ins=[(x, d, _col0)], consts=[(g, d, _col0)],
                   outs=[(d, BF16, d, _col0)])[0]


def _rms_bwd(dh, x, g, res, *, name, tile=512):
    rows, d = x.shape

    def fn(dhv, xv, rv, gv):
        r = lax.rsqrt(jnp.mean(xv * xv, axis=-1, keepdims=True) + EPS)
        xh = xv * r
        dxh = dhv * gv
        dx = r * (dxh - xh * jnp.mean(dxh * xh, axis=-1, keepdims=True))
        return rv + dx, jnp.sum(dhv * xh, axis=0, keepdims=True)

    return _rowmap(fn, name=name, rows=rows, tile=tile, ins=[(dh, d, _col0), (x, d, _col0), (res, d, _col0)],
                   consts=[(g, d, _col0)], outs=[(d, F32, d, _col0)], accs=[(d, d, _col0)])


def _swiglu_fwd(gu, *, name, tile=512):
    rows = gu.shape[0]
    w = _pick(D_FF, 1408)
    nb = D_FF // w

    def fn(gv, uv):
        return gv * _sigmoid(gv) * uv

    return _rowmap(fn, name=name, rows=rows, tile=tile, ncol=nb,
                   ins=[(gu, w, _colj), (gu, w, lambda j: j + nb)], outs=[(D_FF, BF16, w, _colj)])[0]


def _swiglu_bwd(da, gu, *, name, tile=512):
    rows = gu.shape[0]
    w = _pick(D_FF, 1408)
    nb = D_FF // w

    def fn(dav, gv, uv):
        sg = _sigmoid(gv)
        dg = dav * uv * (sg * (1.0 + gv * (1.0 - sg)))
        du = dav * (gv * sg)
        return dg, du

    return _rowmap(fn, name=name, rows=rows, tile=tile, ncol=nb,
                   ins=[(da, w, _colj), (gu, w, _colj), (gu, w, lambda j: j + nb)],
                   outs=[(D_FF, BF16, w, _colj), (D_FF, BF16, w, _colj)])


U_CKV, U_KPE, U_DT = 512, 768, 896


def _lora_norm_fwd(u_small, qg, kvg, *, name, tile=512):
    rows = u_small.shape[0]

    def fn(cq, ckv, qgv, kgv):
        rq = lax.rsqrt(jnp.mean(cq * cq, axis=-1, keepdims=True) + EPS)
        rk = lax.rsqrt(jnp.mean(ckv * ckv, axis=-1, keepdims=True) + EPS)
        return cq * rq * qgv, ckv * rk * kgv

    return _rowmap(fn, name=name, rows=rows, tile=tile,
                   ins=[(u_small, Q_LORA, _col0), (u_small, KV_LORA, lambda j: U_CKV // KV_LORA)],
                   consts=[(qg, Q_LORA, _col0), (kvg, KV_LORA, _col0)],
                   outs=[(Q_LORA, BF16, Q_LORA, _col0), (KV_LORA, BF16, KV_LORA, _col0)])


def _lora_norm_bwd(dcqn, dckvn, u_small, qg, kvg, dkpe, draw_f, draw_b, *, name, tile=512):
    rows = u_small.shape[0]
    tile = min(tile, rows)

    def body(dq_ref, dk_ref, u_ref, dkp_ref, df_ref, db_ref, qg_ref, kg_ref, du_ref, gq_ref, gk_ref):
        cq, ckv = u_ref[:, 0:Q_LORA], u_ref[:, U_CKV:U_CKV + KV_LORA]
        dq, dk = dq_ref[...], dk_ref[...]
        rq = lax.rsqrt(jnp.mean(cq * cq, axis=-1, keepdims=True) + EPS)
        xh = cq * rq
        dxh = dq * qg_ref[...]
        du_ref[:, 0:Q_LORA] = (rq * (dxh - xh * jnp.mean(dxh * xh, axis=-1, keepdims=True))).astype(BF16)
        du_ref[:, Q_LORA:U_CKV] = jnp.zeros((tile, U_CKV - Q_LORA), BF16)
        rk = lax.rsqrt(jnp.mean(ckv * ckv, axis=-1, keepdims=True) + EPS)
        kh = ckv * rk
        dkh = dk * kg_ref[...]
        du_ref[:, U_CKV:U_KPE] = (rk * (dkh - kh * jnp.mean(dkh * kh, axis=-1, keepdims=True))).astype(BF16)
        du_ref[:, U_KPE:U_DT] = dkp_ref[...].astype(BF16)
        du_ref[:, U_DT:U_DT + LANES] = (df_ref[...] + db_ref[...]).astype(BF16)
        gq = jnp.sum(dq * xh, axis=0, keepdims=True)
        gk = jnp.sum(dk * kh, axis=0, keepdims=True)
        i = pl.program_id(0)

        @pl.when(i == 0)
        def _():
            gq_ref[...] = gq
            gk_ref[...] = gk

        @pl.when(i > 0)
        def _():
            gq_ref[...] += gq
            gk_ref[...] += gk

    def rowblk(w):
        return pl.BlockSpec((tile, w), lambda i: (i, 0))

    def whole(w):
        return pl.BlockSpec((1, w), lambda i: (0, 0))

    return _pallas(
        body, name=name, grid=(rows // tile,),
        in_specs=[rowblk(Q_LORA), rowblk(KV_LORA), rowblk(1024), rowblk(LANES), rowblk(LANES), rowblk(LANES),
                  whole(Q_LORA), whole(KV_LORA)],
        out_specs=[rowblk(1024), whole(Q_LORA), whole(KV_LORA)],
        out_shape=[jax.ShapeDtypeStruct((rows, 1024), BF16), jax.ShapeDtypeStruct((1, Q_LORA), F32),
                   jax.ShapeDtypeStruct((1, KV_LORA), F32)],
        compiler_params=_params(("arbitrary",)),
    )(dcqn, dckvn, u_small, dkpe, draw_f, draw_b, qg, kvg)


def _rope(x, c, s1, s2):
    return x * c + pltpu.roll(x, 112, 1) * s1 + pltpu.roll(x, 16, 1) * s2


def _rope_t(d, c, s1, s2):
    return d * c + pltpu.roll(d * s1, 16, 1) + pltpu.roll(d * s2, 112, 1)


def _qk_prep_fwd(q_raw, k_raw, u_small, tabs, qg, kg, *, name, tile=256):
    rows = q_raw.shape[0]
    tile = min(tile, rows)
    scale = 1.0 / math.sqrt(QK_HEAD)

    def body(q_ref, k_ref, u_ref, c_ref, s1_ref, s2_ref, qg_ref, kg_ref, qo_ref, ko_ref):
        c, s1, s2 = c_ref[...], s1_ref[...], s2_ref[...]
        qgv, kgv = qg_ref[...], kg_ref[...]
        kpe = pltpu.roll(u_ref[:, U_KPE:U_KPE + LANES], 64, 1)
        for h in range(N_HEADS):
            hs = slice(h * LANES, (h + 1) * LANES)
            qr = q_ref[:, hs]
            rq = lax.rsqrt(jnp.sum(qr * qr, axis=-1, keepdims=True) / QK_HEAD + EPS)
            qo_ref[:, hs] = (_rope(qr * rq * qgv, c, s1, s2) * scale).astype(BF16)
            xk = k_ref[:, hs] + kpe
            rk = lax.rsqrt(jnp.sum(xk * xk, axis=-1, keepdims=True) / QK_HEAD + EPS)
            ko_ref[:, hs] = _rope(xk * rk * kgv, c, s1, s2).astype(BF16)

    wide = pl.BlockSpec((tile, 2048), lambda i: (i, 0))
    narrow = pl.BlockSpec((tile, LANES), lambda i: (i, 0))
    gain = pl.BlockSpec((1, LANES), lambda i: (0, 0))
    return _pallas(
        body, name=name, grid=(rows // tile,),
        in_specs=[wide, wide, pl.BlockSpec((tile, 1024), lambda i: (i, 0)), narrow, narrow, narrow, gain, gain],
        out_specs=[wide, wide], out_shape=[jax.ShapeDtypeStruct((rows, 2048), BF16)] * 2,
        compiler_params=_params(("parallel",)),
    )(q_raw, k_raw, u_small, *tabs, qg, kg)


def _qk_prep_bwd(dq, dk, q_raw, k_raw, u_small, tabs, qg, kg, *, name, tile=256):
    rows = q_raw.shape[0]
    tile = min(tile, rows)
    scale = 1.0 / math.sqrt(QK_HEAD)

    def body(dq_ref, dk_ref, q_ref, k_ref, u_ref, c_ref, s1_ref, s2_ref, qg_ref, kg_ref,
             dqo_ref, dko_ref, dkpe_ref, gq_ref, gk_ref):
        c, s1, s2 = c_ref[...], s1_ref[...], s2_ref[...]
        qgv, kgv = qg_ref[...], kg_ref[...]
        kpe = pltpu.roll(u_ref[:, U_KPE:U_KPE + LANES], 64, 1)
        lane = lax.broadcasted_iota(jnp.int32, (tile, LANES), 1)
        gq = jnp.zeros((1, LANES), F32)
        gk = jnp.zeros((1, LANES), F32)
        dkpe = jnp.zeros((tile, LANES), F32)
        for h in range(N_HEADS):
            hs = slice(h * LANES, (h + 1) * LANES)
            qr = q_ref[:, hs]
            rq = lax.rsqrt(jnp.sum(qr * qr, axis=-1, keepdims=True) / QK_HEAD + EPS)
            xh = qr * rq
            dy = _rope_t(dq_ref[:, hs] * scale, c, s1, s2)
            dxh = dy * qgv
            dqo_ref[:, hs] = (rq * (dxh - xh * (jnp.sum(dxh * xh, axis=-1, keepdims=True) / QK_HEAD))).astype(BF16)
            gq = gq + jnp.sum(dy * xh, axis=0, keepdims=True)
            xk = k_ref[:, hs] + kpe
            rk = lax.rsqrt(jnp.sum(xk * xk, axis=-1, keepdims=True) / QK_HEAD + EPS)
            kh = xk * rk
            dyk = _rope_t(dk_ref[:, hs], c, s1, s2)
            dkh = dyk * kgv
            dxk = rk * (dkh - kh * (jnp.sum(dkh * kh, axis=-1, keepdims=True) / QK_HEAD))
            gk = gk + jnp.sum(dyk * kh, axis=0, keepdims=True)
            dko_ref[:, hs] = jnp.where(lane < QK_NOPE, dxk, 0.0).astype(BF16)
            dkpe = dkpe + dxk
        dkpe_ref[...] = jnp.where(lane < QK_ROPE, pltpu.roll(dkpe, 64, 1), 0.0)
        i = pl.program_id(0)

        @pl.when(i == 0)
        def _():
            gq_ref[...] = gq
            gk_ref[...] = gk

        @pl.when(i > 0)
        def _():
            gq_ref[...] += gq
            gk_ref[...] += gk

    wide = pl.BlockSpec((tile, 2048), lambda i: (i, 0))
    narrow = pl.BlockSpec((tile, LANES), lambda i: (i, 0))
    gain = pl.BlockSpec((1, LANES), lambda i: (0, 0))
    return _pallas(
        body, name=name, grid=(rows // tile,),
        in_specs=[wide, wide, wide, wide, pl.BlockSpec((tile, 1024), lambda i: (i, 0)), narrow, narrow, narrow,
                  gain, gain],
        out_specs=[wide, wide, narrow, gain, gain],
        out_shape=[jax.ShapeDtypeStruct((rows, 2048), BF16)] * 2
        + [jax.ShapeDtypeStruct((rows, LANES), F32), jax.ShapeDtypeStruct((1, LANES), F32),
           jax.ShapeDtypeStruct((1, LANES), F32)],
        compiler_params=_params(("arbitrary",)),
    )(dq, dk, q_raw, k_raw, u_small, *tabs, qg, kg)


def _attn_fwd(q, k, v, comm_modes, comm_arrays, *, name, tq=512, tkc=512):
    T = q.shape[0]
    tq = min(tq, T)
    tkc = min(tkc, T)
    n = len(comm_arrays)
    nj, ni = N_HEADS // 2, T // tq

    def body(*refs):
        q_ref, k_ref, v_ref = refs[:3]
        srcs = refs[3:3 + n]
        o_ref, lse_ref = refs[3 + n:5 + n]
        dsts = refs[5 + n:5 + 2 * n]
        sems = refs[5 + 2 * n:]
        j, i = pl.program_id(0), pl.program_id(1)

        @pl.when((j == 0) & (i == 0))
        def _():
            _comm_start(comm_modes, srcs, dsts, sems)

        out = None
        for hh in range(2):
            sl = slice(hh * LANES, (hh + 1) * LANES)
            qv = q_ref[:, sl]
            m = l = acc = None
            for kc in range(T // tkc):
                ks = slice(kc * tkc, (kc + 1) * tkc)
                s = _dot_nt(qv, k_ref[ks, sl])
                mc = jnp.max(s, axis=-1, keepdims=True)
                if m is None:
                    m = mc
                    p = jnp.exp(s - m)
                    l = jnp.sum(p, axis=-1, keepdims=True)
                    acc = _dot(p.astype(BF16), v_ref[ks, sl])
                else:
                    m_new = jnp.maximum(m, mc)
                    alpha = jnp.exp(m - m_new)
                    p = jnp.exp(s - m_new)
                    l = alpha * l + jnp.sum(p, axis=-1, keepdims=True)
                    acc = alpha * acc + _dot(p.astype(BF16), v_ref[ks, sl])
                    m = m_new
            o = acc / l
            out = o if out is None else out + o
            lse_ref[hh] = m + jnp.log(l)
        o_ref[...] = out

        @pl.when((j == nj - 1) & (i == ni - 1))
        def _():
            _comm_wait(comm_modes, srcs, dsts, sems)

    any_spec = pl.BlockSpec(memory_space=pl.ANY)
    got = _pallas(
        body, name=name, grid=(nj, ni),
        in_specs=[pl.BlockSpec((tq, 2 * LANES), lambda j, i: (i, j)), pl.BlockSpec((T, 2 * LANES), lambda j, i: (0, j)),
                  pl.BlockSpec((T, 2 * LANES), lambda j, i: (0, j))] + [any_spec] * n,
        out_specs=[pl.BlockSpec((tq, LANES), lambda j, i: (i, j)), pl.BlockSpec((2, tq, 1), lambda j, i: (j, i, 0))]
        + [any_spec] * n,
        out_shape=[jax.ShapeDtypeStruct((T, N_HEADS * V_HEAD), F32), jax.ShapeDtypeStruct((N_HEADS, T, 1), F32)]
        + _comm_out_shapes(comm_modes, comm_arrays),
        scratch_shapes=_comm_scratch(n),
        compiler_params=_params(("arbitrary", "arbitrary")),
    )(q, k, v, *comm_arrays)
    return got[0], got[1], got[2:]


def _attn_bwd(q, k, v, o, lse, do, comm_modes, comm_arrays, *, name, tk=256, tqc=4096):
    T = q.shape[0]
    tk = min(tk, T)
    tqc = min(tqc, T)
    n = len(comm_arrays)
    nj, nkb = N_HEADS // 2, T // tk

    def body(*refs):
        q_ref, k_ref, v_ref, o_ref, lse_ref, do_ref = refs[:6]
        srcs = refs[6:6 + n]
        dq_ref, dk_ref, dv_ref = refs[6 + n:9 + n]
        dsts = refs[9 + n:9 + 2 * n]
        d_s = refs[9 + 2 * n]
        sems = refs[10 + 2 * n:]
        j, kb = pl.program_id(0), pl.program_id(1)

        @pl.when((j == 0) & (kb == 0))
        def _():
            _comm_start(comm_modes, srcs, dsts, sems)

        lane = lax.broadcasted_iota(jnp.int32, (1, LANES), 1)
        @pl.when(kb == 0)
        def _():
            prod = do_ref[...] * o_ref[...]
            for hh in range(2):
                keep = (lane < V_HEAD) if hh == 0 else (lane >= V_HEAD)
                d_s[hh] = jnp.sum(jnp.where(keep, prod, 0.0), axis=-1, keepdims=True)

        for hh in range(2):
            sl = slice(hh * LANES, (hh + 1) * LANES)
            keep = (lane < V_HEAD) if hh == 0 else (lane >= V_HEAD)
            kv, vv = k_ref[:, sl], v_ref[:, sl]
            dv_acc = dk_acc = None
            for qc in range(T // tqc):
                qs = slice(qc * tqc, (qc + 1) * tqc)
                qv = q_ref[qs, sl]
                do_b = do_ref[qs, :].astype(BF16)
                s = _dot_nt(qv, kv)
                p = jnp.exp(s - lse_ref[hh, qs])
                dp = _dot_nt(do_b, vv)
                ds = (p * (dp - d_s[hh, qs])).astype(BF16)
                dvc = _dot_tn(p.astype(BF16), do_b)
                dkc = _dot_tn(ds, qv)
                dv_acc = dvc if dv_acc is None else dv_acc + dvc
                dk_acc = dkc if dk_acc is None else dk_acc + dkc
                dqp = _dot(ds, kv)

                @pl.when(kb == 0)
                def _(dqp=dqp, sl=sl, qs=qs):
                    dq_ref[qs, sl] = dqp

                @pl.when(kb > 0)
                def _(dqp=dqp, sl=sl, qs=qs):
                    dq_ref[qs, sl] += dqp

            dv_ref[:, sl] = jnp.where(keep, dv_acc, 0.0).astype(BF16)
            dk_ref[:, sl] = dk_acc

        @pl.when((j == nj - 1) & (kb == nkb - 1))
        def _():
            _comm_wait(comm_modes, srcs, dsts, sems)

    any_spec = pl.BlockSpec(memory_space=pl.ANY)
    pair = pl.BlockSpec((T, 2 * LANES), lambda j, kb: (0, j))
    kblk = pl.BlockSpec((tk, 2 * LANES), lambda j, kb: (kb, j))
    got = _pallas(
        body, name=name, grid=(nj, nkb),
        in_specs=[pair, kblk, kblk, pl.BlockSpec((T, LANES), lambda j, kb: (0, j)),
                  pl.BlockSpec((2, T, 1), lambda j, kb: (j, 0, 0)), pl.BlockSpec((T, LANES), lambda j, kb: (0, j))]
        + [any_spec] * n,
        out_specs=[pair, kblk, kblk] + [any_spec] * n,
        out_shape=[jax.ShapeDtypeStruct((T, 2048), F32)] * 2 + [jax.ShapeDtypeStruct((T, 2048), BF16)]
        + _comm_out_shapes(comm_modes, comm_arrays),
        scratch_shapes=[pltpu.VMEM((2, T, 1), F32)] + _comm_scratch(n),
        compiler_params=_params(("arbitrary", "arbitrary")),
    )(q, k, v, o, lse, do, *comm_arrays)
    return got[0], got[1], got[2], got[3:]


CONV_ROWS, CONV_HALO = 64, 8
CONV_WIN = CONV_ROWS + 2 * CONV_HALO


def _conv_shift(x, sh, t_idx, total):
    if sh == 0:
        return x
    y = pltpu.roll(x, (-sh) % x.shape[0], 0)
    if t_idx is None:
        return y
    ok = (t_idx + sh >= 0) & (t_idx + sh < total)
    return jnp.where(ok, y, 0.0)


def _conv_positions(ws, shape):
    return ws + lax.broadcasted_iota(jnp.int32, shape, 0) if isinstance(ws, int) else None


def _aligned(v, m):
    return v if isinstance(v, int) else pl.multiple_of(v, m)


def _conv_chunks(T, chunk, carry):
    n = T // CONV_ROWS
    carry = chunk(0, 0, carry)

    def mid(ci, c):
        return chunk(pl.multiple_of(ci * CONV_ROWS - CONV_HALO, CONV_HALO), CONV_HALO, c)

    carry = lax.fori_loop(1, n - 1, mid, carry)
    return chunk(T - CONV_WIN, 2 * CONV_HALO, carry)


def _conv_pre(x, w_ref, b_ref, t_idx, total):
    pre = b_ref[...] + w_ref[2:3, :] * x
    for j in (0, 1, 3, 4):
        pre = pre + w_ref[j:j + 1, :] * _conv_shift(x, j - 2, t_idx, total)
    return pre


def _conv_fwd(u_big, conv_w, conv_b, *, name, w=256):
    T = u_big.shape[0]
    first = D_INNER // w

    def body(x_ref, w_ref, b_ref, o_ref):
        def chunk(ws, off, carry):
            x = x_ref[pl.ds(ws, CONV_WIN), :]
            pre = _conv_pre(x, w_ref, b_ref, _conv_positions(ws, x.shape), T)
            act = pre * _sigmoid(pre)
            o_ref[pl.ds(_aligned(ws + off, CONV_ROWS), CONV_ROWS), :] = act[off:off + CONV_ROWS]
            return carry

        _conv_chunks(T, chunk, 0)

    return _pallas(
        body, name=name, grid=(XBC_DIM // w,),
        in_specs=[pl.BlockSpec((T, w), lambda j: (0, j + first)), pl.BlockSpec((CONV_WIDTH, w), lambda j: (0, j)),
                  pl.BlockSpec((1, w), lambda j: (0, j))],
        out_specs=pl.BlockSpec((T, w), lambda j: (0, j)),
        out_shape=jax.ShapeDtypeStruct((T, XBC_DIM), F32),
        compiler_params=_params(("parallel",)),
    )(u_big, conv_w, conv_b)


def _conv_bwd(dact_f, dact_b, u_big, conv_w, conv_b, *, name, w=128):
    T = u_big.shape[0]
    first = D_INNER // w

    def body(df_ref, db_ref, x_ref, w_ref, b_ref, dx_ref, dw_ref, dbias_ref):
        def chunk(ws, off, sums):
            rows = pl.ds(ws, CONV_WIN)
            x = x_ref[rows, :]
            row = lax.broadcasted_iota(jnp.int32, x.shape, 0)
            t_idx = _conv_positions(ws, x.shape)
            pre = _conv_pre(x, w_ref, b_ref, t_idx, T)
            sg = _sigmoid(pre)
            dpre = (df_ref[rows, :] + db_ref[rows, :]) * (sg * (1.0 + pre * (1.0 - sg)))
            dx = w_ref[2:3, :] * dpre
            for j in (0, 1, 3, 4):
                dx = dx + w_ref[j:j + 1, :] * _conv_shift(dpre, 2 - j, t_idx, T)
            dx_ref[pl.ds(_aligned(ws + off, CONV_ROWS), CONV_ROWS), :] = dx[off:off + CONV_ROWS].astype(dx_ref.dtype)
            own = jnp.where((row >= off) & (row < off + CONV_ROWS), dpre, 0.0)
            new = [sums[5] + jnp.sum(own, axis=0, keepdims=True)]
            for j in range(CONV_WIDTH):
                new.insert(j, sums[j] + jnp.sum(own * _conv_shift(x, j - 2, t_idx, T), axis=0, keepdims=True))
            return tuple(new)

        zero = jnp.zeros((1, w), F32)
        sums = _conv_chunks(T, chunk, (zero,) * (CONV_WIDTH + 1))
        for j in range(CONV_WIDTH):
            dw_ref[j:j + 1, :] = sums[j]
        dbias_ref[...] = sums[CONV_WIDTH]

    blk = pl.BlockSpec((T, w), lambda j: (0, j))
    return _pallas(
        body, name=name, grid=(XBC_DIM // w,),
        in_specs=[blk, blk, pl.BlockSpec((T, w), lambda j: (0, j + first)),
                  pl.BlockSpec((CONV_WIDTH, w), lambda j: (0, j)), pl.BlockSpec((1, w), lambda j: (0, j))],
        out_specs=[blk, pl.BlockSpec((CONV_WIDTH, w), lambda j: (0, j)), pl.BlockSpec((1, w), lambda j: (0, j))],
        out_shape=[jax.ShapeDtypeStruct((T, XBC_DIM), BF16), jax.ShapeDtypeStruct((CONV_WIDTH, XBC_DIM), F32),
                   jax.ShapeDtypeStruct((1, XBC_DIM), F32)],
        compiler_params=_params(("parallel",)),
    )(dact_f, dact_b, u_big, conv_w, conv_b)


def _ssd_expand(rev):
    off = SSM_HEADS if rev else 0
    h = jnp.arange(LANES, dtype=jnp.int32)[:, None]
    return (jnp.arange(D_INNER, dtype=jnp.int32)[None, :] // 64 + off == h).astype(F32)


def _ssd_head_terms(dt_ref, bias_ref, alog_ref, acst_s, dtt_s, rev):
    L = CHUNK
    row = lax.broadcasted_iota(jnp.int32, (L, L), 0)
    col = lax.broadcasted_iota(jnp.int32, (L, L), 1)
    mask = (row <= col) if rev else (row >= col)
    cm = mask.astype(F32)
    cmt = ((row >= col) if rev else (row <= col)).astype(F32)
    pre = dt_ref[...] + bias_ref[...]
    dt = _softplus(pre)
    a = -jnp.exp(alog_ref[...])
    da = dt * a
    acs = _dot_h(cm, da)
    acst_s[...] = _dot_h_tn(da, cmt)
    dtt_s[...] = _dot_h_tn(dt, (row == col).astype(F32))
    tot = jnp.sum(da, axis=0, keepdims=True)
    w = jnp.exp(tot - acs)
    return dict(mask=mask, cm=cm, cmt=cmt, ident=(row == col).astype(F32), pre=pre, dt=dt, a=a, da=da, acs=acs,
                tot=tot, e=jnp.exp(acs), w=w, wdt=w * dt, dec=jnp.exp(tot))


def _pair(lo, v, h0):
    return jnp.where(lo, v[:, h0:h0 + 1], v[:, h0 + 1:h0 + 2])


def _ssd_fwd(xbc_act, u_small, bias128, alog128, *, rev, name):
    T = xbc_act.shape[0]
    L = CHUNK
    nc = T // L
    off = SSM_HEADS if rev else 0

    def cidx(c):
        return (nc - 1 - c) if rev else c

    def body(xs_ref, bm_ref, cm_ref, dt_ref, bias_ref, alog_ref, y_ref, hin_ref, ht_s, acst_s, dtt_s, wx_s, dec_s):
        c = pl.program_id(0)

        @pl.when(c == 0)
        def _():
            ht_s[...] = jnp.zeros_like(ht_s)

        t = _ssd_head_terms(dt_ref, bias_ref, alog_ref, acst_s, dtt_s, rev)
        lo = lax.broadcasted_iota(jnp.int32, (L, LANES), 1) < 64
        lo1 = lax.broadcasted_iota(jnp.int32, (1, LANES), 1) < 64
        for g in range(SSM_GROUPS):
            bmat = bm_ref[:, g * LANES:(g + 1) * LANES].astype(BF16)
            cmat = cm_ref[:, g * LANES:(g + 1) * LANES].astype(BF16)
            gmat = _dot_nt(cmat, bmat)
            ht = ht_s[g]
            ch = _dot(cmat, ht.astype(BF16))
            for pr in range(4):
                ps = slice(pr * LANES, (pr + 1) * LANES)
                cs = slice(g * 512 + pr * LANES, g * 512 + (pr + 1) * LANES)
                h0 = off + 8 * g + 2 * pr
                xp = xs_ref[:, cs]
                acc = _pair(lo, t["e"], h0) * ch[:, ps]
                for s_ in range(2):
                    h = h0 + s_
                    seg = t["acs"][:, h:h + 1] - acst_s[h:h + 1, :]
                    lam = jnp.exp(jnp.where(t["mask"], seg, -1e30))
                    m = (gmat * lam * dtt_s[h:h + 1, :]).astype(BF16)
                    xm = jnp.where(lo if s_ == 0 else jnp.logical_not(lo), xp, 0.0).astype(BF16)
                    acc = acc + _dot(m, xm)
                y_ref[:, cs] = acc
                wx_s[:, ps] = (_pair(lo, t["wdt"], h0) * xp).astype(BF16)
                dec_s[0:1, ps] = _pair(lo1, t["dec"], h0)
            hin_ref[0, g] = ht.astype(BF16)
            ht_s[g] = ht * dec_s[0:1, :] + _dot_tn(bmat, wx_s[...])

    return _pallas(
        body, name=name, grid=(nc,),
        in_specs=[pl.BlockSpec((L, D_INNER), lambda c: (cidx(c), 0)), pl.BlockSpec((L, 512), lambda c: (cidx(c), 4)),
                  pl.BlockSpec((L, 512), lambda c: (cidx(c), 5)),
                  pl.BlockSpec((L, LANES), lambda c: (cidx(c), U_DT // LANES)),
                  pl.BlockSpec((1, LANES), lambda c: (0, 0)), pl.BlockSpec((1, LANES), lambda c: (0, 0))],
        out_specs=[pl.BlockSpec((L, D_INNER), lambda c: (cidx(c), 0)),
                   pl.BlockSpec((1, SSM_GROUPS, D_STATE, 512), lambda c: (cidx(c), 0, 0, 0))],
        out_shape=[jax.ShapeDtypeStruct((T, D_INNER), F32), jax.ShapeDtypeStruct((nc, SSM_GROUPS, D_STATE, 512), BF16)],
        scratch_shapes=[pltpu.VMEM((SSM_GROUPS, D_STATE, 512), F32), pltpu.VMEM((LANES, L), F32),
                        pltpu.VMEM((LANES, L), F32), pltpu.VMEM((L, 512), BF16), pltpu.VMEM((8, 512), F32)],
        compiler_params=_params(("arbitrary",)),
    )(xbc_act, xbc_act, xbc_act, u_small, bias128, alog128)


def _ssd_bwd(dy, xbc_act, u_small, bias128, alog128, hin, skip_x, *, rev, name):
    T = xbc_act.shape[0]
    L = CHUNK
    nc = T // L
    off = SSM_HEADS if rev else 0
    has_skip = skip_x is not None

    def cidx(c):
        return c if rev else (nc - 1 - c)

    def body(*refs):
        (dy_ref, xs_ref, bm_ref, cm_ref, dt_ref, bias_ref, alog_ref, hin_ref) = refs[:8]
        k = 8
        skip_ref = refs[k] if has_skip else None
        k += 1 if has_skip else 0
        (dx_ref, draw_ref, dalog_ref, dbias_ref, dht_s, acst_s, dtt_s, rowt_s, ddtt_s, wx_s, edy_s, dec_s) = refs[k:]
        c = pl.program_id(0)

        @pl.when(c == 0)
        def _():
            dht_s[...] = jnp.zeros_like(dht_s)
            rowt_s[...] = jnp.zeros_like(rowt_s)
            ddtt_s[...] = jnp.zeros_like(ddtt_s)

        t = _ssd_head_terms(dt_ref, bias_ref, alog_ref, acst_s, dtt_s, rev)
        lane1 = lax.broadcasted_iota(jnp.int32, (1, LANES), 1)
        lo = lax.broadcasted_iota(jnp.int32, (L, LANES), 1) < 64
        lo1 = lane1 < 64
        colpart = jnp.zeros((L, LANES), F32)
        u_cols = jnp.zeros((L, LANES), F32)
        v_cols = jnp.zeros((L, LANES), F32)
        dtot_h = jnp.zeros((1, LANES), F32)
        for g in range(SSM_GROUPS):
            bmat = bm_ref[:, g * LANES:(g + 1) * LANES].astype(BF16)
            cmat = cm_ref[:, g * LANES:(g + 1) * LANES].astype(BF16)
            gmat = _dot_nt(cmat, bmat)
            ht_in = hin_ref[0, g]
            dht = dht_s[g]
            ht_in_b, dht_b = ht_in.astype(BF16), dht.astype(BF16)
            ch = _dot(cmat, ht_in_b)
            bdh = _dot(bmat, dht_b)
            th = jnp.sum(dht * ht_in, axis=0, keepdims=True)
            dgm = jnp.zeros((L, L), F32)
            for pr in range(4):
                ps = slice(pr * LANES, (pr + 1) * LANES)
                cs = slice(g * 512 + pr * LANES, g * 512 + (pr + 1) * LANES)
                h0 = off + 8 * g + 2 * pr
                xp = xs_ref[:, cs]
                dyp = dy_ref[:, cs]
                dyp_b = dyp.astype(BF16)
                wdt_p = _pair(lo, t["wdt"], h0)
                e_p = _pair(lo, t["e"], h0)
                xb = xp * bdh[:, ps]
                dc = dyp * ch[:, ps]
                dxp = wdt_p * bdh[:, ps]
                for s_ in range(2):
                    h = h0 + s_
                    keep = lo if s_ == 0 else jnp.logical_not(lo)
                    keep1 = lo1 if s_ == 0 else jnp.logical_not(lo1)
                    onehot = (lane1 == h).astype(F32)
                    dtrow = dtt_s[h:h + 1, :]
                    seg = t["acs"][:, h:h + 1] - acst_s[h:h + 1, :]
                    lam = jnp.exp(jnp.where(t["mask"], seg, -1e30))
                    mf0 = gmat * lam
                    m = (mf0 * dtrow).astype(BF16)
                    xm = jnp.where(keep, xp, 0.0).astype(BF16)
                    dm = _dot_nt(dyp_b, xm)
                    r = dm * mf0
                    q = r * dtrow
                    dgm = dgm + dm * lam * dtrow
                    colpart = colpart + jnp.sum(q, axis=1, keepdims=True) * onehot
                    rowt_s[h:h + 1, :] = jnp.sum(q, axis=0, keepdims=True)
                    ddtt_s[h:h + 1, :] = jnp.sum(r, axis=0, keepdims=True)
                    u_cols = u_cols + jnp.sum(jnp.where(keep, xb, 0.0), axis=1, keepdims=True) * onehot
                    v_cols = v_cols + jnp.sum(jnp.where(keep, dc, 0.0), axis=1, keepdims=True) * onehot
                    dtot_h = dtot_h + jnp.sum(jnp.where(keep1, th[:, ps], 0.0), axis=1, keepdims=True) * onehot
                    dxp = dxp + jnp.where(keep, _dot_tn(m, dyp_b), 0.0)
                if has_skip:
                    dxp = dxp + dyp * skip_ref[:, cs]
                dx_ref[:, cs] = dxp
                wx_s[:, ps] = (wdt_p * xp).astype(BF16)
                edy_s[:, ps] = (e_p * dyp).astype(BF16)
                dec_s[0:1, ps] = _pair(lo1, t["dec"], h0)
            edy_b = edy_s[...]
            dgm_b = dgm.astype(BF16)
            dx_ref[:, D_INNER + g * LANES:D_INNER + (g + 1) * LANES] = (
                _dot_nt(wx_s[...], dht_b) + _dot_tn(dgm_b, cmat))
            dx_ref[:, D_INNER + 512 + g * LANES:D_INNER + 512 + (g + 1) * LANES] = (
                _dot_nt(edy_b, ht_in_b) + _dot(dgm_b, bmat))
            dht_s[g] = dec_s[0:1, :] * dht + _dot_tn(cmat, edy_b)

        t_e = v_cols * t["e"]
        t_w = u_cols * t["wdt"]
        colsum_part = _dot_h_tn(rowt_s[...], t["ident"])
        dtot = jnp.sum(t_w, axis=0, keepdims=True) + t["dec"] * dtot_h
        row1 = lax.broadcasted_iota(jnp.int32, (L, LANES), 0)
        last = row1 == (0 if rev else L - 1)
        dacs = colpart - colsum_part + t_e - t_w + jnp.where(last, dtot, 0.0)
        dda = _dot_h(t["cmt"], dacs)
        ddt = dda * t["a"] + u_cols * t["w"] + _dot_h_tn(ddtt_s[...], t["ident"])
        dalog = jnp.sum(dda * t["dt"], axis=0, keepdims=True) * t["a"]
        draw = ddt * _sigmoid(t["pre"])
        draw_ref[...] = draw
        dbias = jnp.sum(draw, axis=0, keepdims=True)

        @pl.when(c == 0)
        def _():
            dalog_ref[...] = dalog
            dbias_ref[...] = dbias

        @pl.when(c > 0)
        def _():
            dalog_ref[...] += dalog
            dbias_ref[...] += dbias

    one = pl.BlockSpec((1, LANES), lambda c: (0, 0))
    in_specs = [pl.BlockSpec((L, D_INNER), lambda c: (cidx(c), 0)), pl.BlockSpec((L, D_INNER), lambda c: (cidx(c), 0)),
                pl.BlockSpec((L, 512), lambda c: (cidx(c), 4)), pl.BlockSpec((L, 512), lambda c: (cidx(c), 5)),
                pl.BlockSpec((L, LANES), lambda c: (cidx(c), U_DT // LANES)), one, one,
                pl.BlockSpec((1, SSM_GROUPS, D_STATE, 512), lambda c: (cidx(c), 0, 0, 0))]
    ins = [dy, xbc_act, xbc_act, xbc_act, u_small, bias128, alog128, hin]
    if has_skip:
        in_specs.append(pl.BlockSpec((1, D_INNER), lambda c: (0, 0)))
        ins.append(skip_x)
    return _pallas(
        body, name=name, grid=(nc,), in_specs=in_specs,
        out_specs=[pl.BlockSpec((L, XBC_DIM), lambda c: (cidx(c), 0)), pl.BlockSpec((L, LANES), lambda c: (cidx(c), 0)),
                   one, one],
        out_shape=[jax.ShapeDtypeStruct((T, XBC_DIM), F32), jax.ShapeDtypeStruct((T, LANES), F32),
                   jax.ShapeDtypeStruct((1, LANES), F32), jax.ShapeDtypeStruct((1, LANES), F32)],
        scratch_shapes=[pltpu.VMEM((SSM_GROUPS, D_STATE, 512), F32), pltpu.VMEM((LANES, L), F32),
                        pltpu.VMEM((LANES, L), F32), pltpu.VMEM((LANES, L), F32), pltpu.VMEM((LANES, L), F32),
                        pltpu.VMEM((L, 512), BF16), pltpu.VMEM((L, 512), BF16), pltpu.VMEM((8, 512), F32)],
        compiler_params=_params(("arbitrary",)),
    )(*ins)


def _ssm_out_fwd(y_f, y_b, xbc_act, u_big, skip_x, ssm_norm, *, name, tile=512):
    rows = y_f.shape[0]

    def fn(yf, yb, xs, z, sk, nw):
        yz = (yf + yb + sk * xs) * (z * _sigmoid(z))
        r = lax.rsqrt(jnp.mean(yz * yz, axis=-1, keepdims=True) + EPS)
        return yz * r * nw

    return _rowmap(fn, name=name, rows=rows, tile=tile, ncol=SSM_GROUPS,
                   ins=[(y_f, 512, _colj), (y_b, 512, _colj), (xbc_act, 512, _colj), (u_big, 512, _colj)],
                   consts=[(skip_x, 512, _colj), (ssm_norm, 512, _colj)], outs=[(D_INNER, BF16, 512, _colj)])[0]


def _ssm_out_bwd(dm, y_f, y_b, xbc_act, u_big, skip_x, ssm_norm, *, name, tile=512):
    rows = y_f.shape[0]

    def fn(dmv, yf, yb, xs, z, sk, nw):
        sg = _sigmoid(z)
        y = yf + yb + sk * xs
        yz = y * (z * sg)
        r = lax.rsqrt(jnp.mean(yz * yz, axis=-1, keepdims=True) + EPS)
        xh = yz * r
        dxh = dmv * nw
        dyz = r * (dxh - xh * jnp.mean(dxh * xh, axis=-1, keepdims=True))
        dy = dyz * (z * sg)
        dz = dyz * y * (sg * (1.0 + z * (1.0 - sg)))
        return dy, dz, jnp.sum(dmv * xh, axis=0, keepdims=True), jnp.sum(dy * xs, axis=0, keepdims=True)

    return _rowmap(fn, name=name, rows=rows, tile=tile, ncol=SSM_GROUPS,
                   ins=[(dm, 512, _colj), (y_f, 512, _colj), (y_b, 512, _colj), (xbc_act, 512, _colj),
                        (u_big, 512, _colj)],
                   consts=[(skip_x, 512, _colj), (ssm_norm, 512, _colj)],
                   outs=[(D_INNER, F32, 512, _colj), (D_INNER, BF16, 512, _colj)],
                   accs=[(D_INNER, 512, _colj), (D_INNER, 512, _colj)])


def _merge_fwd(pa, pb, u_big, *, name, tile=512):
    rows = pa.shape[0]

    def fn(a, b, ga, gb):
        return _sigmoid(ga) * a + _sigmoid(gb) * b

    return _rowmap(fn, name=name, rows=rows, tile=tile,
                   ins=[(pa, 1024, _col0), (pb, 1024, _col0), (u_big, 1024, lambda j: 5), (u_big, 1024, lambda j: 6)],
                   outs=[(1024, BF16, 1024, _col0)])[0]


def _merge_bwd(dmg, pa, pb, u_big, *, name, tile=512):
    rows = pa.shape[0]

    def fn(d, a, b, ga, gb):
        sa, sb = _sigmoid(ga), _sigmoid(gb)
        return d * sa, d * sb, d * a * sa * (1.0 - sa), d * b * sb * (1.0 - sb)

    return _rowmap(fn, name=name, rows=rows, tile=tile,
                   ins=[(dmg, 1024, _col0), (pa, 1024, _col0), (pb, 1024, _col0), (u_big, 1024, lambda j: 5),
                        (u_big, 1024, lambda j: 6)],
                   outs=[(1024, BF16, 1024, _col0)] * 4)


def _loss_bwd(y, target, *, name, tile=512):
    rows, d = y.shape

    def fn(yv, tv):
        err = yv - tv
        part = jnp.sum(jnp.sum(err * err, axis=-1, keepdims=True), axis=0, keepdims=True)
        return err * (1.0 / d), jnp.broadcast_to(part * (0.5 / d), (1, LANES))

    dy, part = _rowmap(fn, name=name, rows=rows, tile=tile, ins=[(y, d, _col0), (target, d, _col0)],
                       outs=[(d, F32, d, _col0)], accs=[(LANES, LANES, _col0)])
    return dy, part[0, 0]


def _small_slab(gs, dskip_ch, dalog_f, dalog_b, dbias_f, dbias_b, gkv, gqh, gkh, dconv_w, *, name):
    e_mat = _ssd_expand(False)
    full_names = ("ffn1_norm", "mix_norm", "q_a_norm", "conv_b", "ssm_norm", "ffn2_norm")
    full = [gs[n] for n in full_names]
    nf = len(full)

    def body(*refs):
        fulls = refs[:nf]
        (dsk_ref, e_ref, af_ref, ab_ref, bf_ref, bb_ref, gkv_ref, gqh_ref, gkh_ref, cw_ref, o_ref) = refs[nf:]
        o_ref[...] = jnp.zeros_like(o_ref)
        for n, r in zip(full_names, fulls):
            o_ref[SMALL_ROW[n]:SMALL_ROW[n] + 1, 0:r.shape[1]] = r[...]
        o_ref[SMALL_ROW["kv_a_norm"]:SMALL_ROW["kv_a_norm"] + 1, 0:KV_LORA] = gkv_ref[...]
        o_ref[SMALL_ROW["q_head_norm"]:SMALL_ROW["q_head_norm"] + 1, 0:LANES] = gqh_ref[...]
        o_ref[SMALL_ROW["k_head_norm"]:SMALL_ROW["k_head_norm"] + 1, 0:LANES] = gkh_ref[...]
        o_ref[SMALL_ROW["a_log_fwd"]:SMALL_ROW["a_log_fwd"] + 1, 0:LANES] = af_ref[...]
        o_ref[SMALL_ROW["a_log_bwd"]:SMALL_ROW["a_log_bwd"] + 1, 0:LANES] = pltpu.roll(ab_ref[...], 96, 1)
        o_ref[SMALL_ROW["dt_bias_fwd"]:SMALL_ROW["dt_bias_fwd"] + 1, 0:LANES] = bf_ref[...]
        o_ref[SMALL_ROW["dt_bias_bwd"]:SMALL_ROW["dt_bias_bwd"] + 1, 0:LANES] = pltpu.roll(bb_ref[...], 96, 1)
        dsk = _dot_h_nt(jnp.broadcast_to(dsk_ref[...], (8, D_INNER)), e_ref[...])
        o_ref[SMALL_ROW["d_skip"]:SMALL_ROW["d_skip"] + 1, 0:LANES] = dsk[0:1, :]
        o_ref[CONV_ROW:CONV_ROW + CONV_WIDTH, :] = cw_ref[...]

    return _pallas(body, name=name, out_shape=jax.ShapeDtypeStruct((SMALL_ROWS, SMALL_COLS), F32))(
        *full, dskip_ch, e_mat, dalog_f, dalog_b, dbias_f, dbias_b, gkv, gqh, gkh, dconv_w)


def _adamw_math(g, w, m, v):
    m2 = ADAM_B1 * m + (1.0 - ADAM_B1) * g
    v2 = ADAM_B2 * v + (1.0 - ADAM_B2) * (g * g)
    m_hat = m2 / (1.0 - ADAM_B1 ** ADAM_STEP)
    v_hat = v2 / (1.0 - ADAM_B2 ** ADAM_STEP)
    delta = -ADAM_LR * (m_hat / (jnp.sqrt(v_hat) + ADAM_EPS) + ADAM_WD * w)
    return delta, m2, v2


def _sum8(r_ref):
    g = r_ref[0].astype(F32)
    for s in range(1, N_DEV):
        g = g + r_ref[s].astype(F32)
    return g


def _reduce_adamw(recv, w, m, v, *, name, tile=256):
    _, R, C = recv.shape
    tile = _pick(R, tile) if R % LANES == 0 else R
    assert R % tile == 0

    def body(r_ref, w_ref, m_ref, v_ref, g_ref, d_ref, m2_ref, v2_ref):
        g = _sum8(r_ref)
        delta, m2, v2 = _adamw_math(g, w_ref[...], m_ref[...], v_ref[...])
        g_ref[...] = g
        d_ref[...] = delta
        m2_ref[...] = m2
        v2_ref[...] = v2

    blk = pl.BlockSpec((tile, C), lambda i: (i, 0))
    return _pallas(
        body, name=name, grid=(R // tile,),
        in_specs=[pl.BlockSpec((N_DEV, tile, C), lambda i: (0, i, 0)), blk, blk, blk], out_specs=[blk] * 4,
        out_shape=[jax.ShapeDtypeStruct((R, C), F32)] * 4, compiler_params=_params(("parallel",)),
    )(recv, w, m, v)


def _reduce_t_adamw(recv, w, m, v, *, name):
    R, cs = w.shape

    def body(r_ref, w_ref, m_ref, v_ref, g_ref, d_ref, m2_ref, v2_ref):
        g = _sum8(r_ref).T
        delta, m2, v2 = _adamw_math(g, w_ref[...], m_ref[...], v_ref[...])
        g_ref[...] = g
        d_ref[...] = delta
        m2_ref[...] = m2
        v2_ref[...] = v2

    return _pallas(body, name=name, out_shape=[jax.ShapeDtypeStruct((R, cs), F32)] * 4,
                   compiler_params=pltpu.CompilerParams(vmem_limit_bytes=VMEM_LIMIT))(recv, w, m, v)


def _reduce8(recv, *, name, tile):
    _, R, C = recv.shape

    def body(r_ref, g_ref):
        g_ref[...] = _sum8(r_ref)

    return _pallas(body, name=name, grid=(R // tile,),
                   in_specs=[pl.BlockSpec((N_DEV, tile, C), lambda i: (0, i, 0))],
                   out_specs=pl.BlockSpec((tile, C), lambda i: (i, 0)),
                   out_shape=jax.ShapeDtypeStruct((R, C), F32), compiler_params=_params(("parallel",)))(recv)


def _adamw(g, w, m, v, *, name, tile=256):
    R, C = w.shape

    def body(g_ref, w_ref, m_ref, v_ref, d_ref, m2_ref, v2_ref):
        delta, m2, v2 = _adamw_math(g_ref[...], w_ref[...], m_ref[...], v_ref[...])
        d_ref[...] = delta
        m2_ref[...] = m2
        v2_ref[...] = v2

    blk = pl.BlockSpec((tile, C), lambda i: (i, 0))
    return _pallas(body, name=name, grid=(R // tile,), in_specs=[blk] * 4, out_specs=[blk] * 3,
                   out_shape=[jax.ShapeDtypeStruct((R, C), F32)] * 3, compiler_params=_params(("parallel",)))(g, w, m, v)


def _adamw_small(srecv, conv_g, ws, ms, vs, *, name):
    n = len(ws)

    def body(*refs):
        s_ref, c_ref = refs[0], refs[1]
        w_refs, m_refs, v_refs = refs[2:2 + n], refs[2 + n:2 + 2 * n], refs[2 + 2 * n:2 + 3 * n]
        outs = refs[2 + 3 * n:]
        gsum = _sum8(s_ref)
        for i in range(n):
            if i < len(SMALL):
                g = gsum[i:i + 1, 0:SMALL[i][1]]
            else:
                g = _sum8(c_ref)
            delta, m2, v2 = _adamw_math(g, w_refs[i][...], m_refs[i][...], v_refs[i][...])
            outs[i][...] = g
            outs[n + i][...] = delta
            outs[2 * n + i][...] = m2
            outs[3 * n + i][...] = v2

    shapes = [jax.ShapeDtypeStruct(w.shape, F32) for w in ws]
    got = _pallas(body, name=name, out_shape=shapes * 4,
                  compiler_params=pltpu.CompilerParams(vmem_limit_bytes=VMEM_LIMIT))(srecv, conv_g, *ws, *ms, *vs)
    return got[:n], got[n:2 * n], got[2 * n:3 * n], got[3 * n:]


def _ffn_fwd(x, norm, w_gu, w_d, tag):
    h = _rms_fwd(x, norm, name=f"{tag}_rms")
    gu = _mm(h, w_gu, name=f"{tag}_gu", out_dtype=BF16)
    act = _swiglu_fwd(gu, name=f"{tag}_act")
    out = _mm(act, w_d, name=f"{tag}_down", alpha=0.5, res=x)
    return out, (h, gu, act)


def _ffn_bwd(dout, x, norm, w_gu, w_d, saved, tag):
    h, gu, act = saved
    d_act = _mm(dout, w_d, name=f"{tag}_dact", tb=True, alpha=0.5, out_dtype=BF16)
    d_wd = _mm(act, dout, name=f"{tag}_dwd", ta=True, alpha=0.5, tm=1408, tn=1024, out_dtype=BF16)
    dg, du = _swiglu_bwd(d_act, gu, name=f"{tag}_dswiglu")
    d_wg_t = _mm(dg, h, name=f"{tag}_dwg", ta=True, tm=1408, tn=1024, out_dtype=BF16)
    d_wu_t = _mm(du, h, name=f"{tag}_dwu", ta=True, tm=1408, tn=1024, out_dtype=BF16)
    dh = _mm(dg, w_gu[:, :D_FF], name=f"{tag}_dh_g", tb=True)
    dh = _mm(du, w_gu[:, D_FF:], name=f"{tag}_dh_u", tb=True, res=dh)
    dx, dnorm = _rms_bwd(dh, x, norm, dout, name=f"{tag}_drms")
    return dx, dnorm, d_wg_t, d_wu_t, d_wd


def _rope_tables(positions, T):
    pos = positions.reshape(T).astype(F32)
    inv_freq = 1.0 / (ROPE_BASE ** (jnp.arange(0, QK_ROPE, 2, dtype=F32) / QK_ROPE))
    ang = pos[:, None] * inv_freq
    cos, sin = jnp.cos(ang), jnp.sin(ang)
    one64, z64 = jnp.ones((T, 64), F32), jnp.zeros((T, 64), F32)
    z16, z32, one32 = jnp.zeros((T, 16), F32), jnp.zeros((T, 32), F32), jnp.ones((T, 32), F32)
    c = jnp.concatenate([one64, cos, cos, one32], axis=1)
    s1 = jnp.concatenate([z64, -sin, z16, z32], axis=1)
    s2 = jnp.concatenate([z64, z16, sin, z32], axis=1)
    return c, s1, s2


def _cols(g):
    n, r, cs = g.shape
    return g.transpose(1, 0, 2).reshape(r, n * cs)


def _rows(g):
    n, rs, c = g.shape
    return g.reshape(n * rs, c)


def _pad_lanes(v, n=LANES):
    return jnp.pad(v, ((0, 0), (0, n - v.shape[1])))


def _in_proj_weights(w_in):
    z = lambda n: jnp.zeros((D_MODEL, n), w_in.dtype)
    w_small = jnp.concatenate([w_in[:, 0:384], z(128), w_in[:, 384:640], w_in[:, 640:672], z(96),
                               w_in[:, 5792:5856], z(64)], axis=1)
    w_big = jnp.concatenate([w_in[:, 672:2720], w_in[:, 2720:5792], w_in[:, 5856:7904]], axis=1)
    return w_small, w_big


def _mla_up_weights(w_q_b, w_kv_b):
    wq = w_q_b.reshape(Q_LORA, N_HEADS, QK_HEAD)
    wq = jnp.pad(wq, ((0, 0), (0, 0), (0, LANES - QK_HEAD))).reshape(Q_LORA, N_HEADS * LANES)
    wkv = w_kv_b.reshape(KV_LORA, N_HEADS, QK_NOPE + V_HEAD)
    wk = jnp.pad(wkv[..., :QK_NOPE], ((0, 0), (0, 0), (0, LANES - QK_NOPE))).reshape(KV_LORA, N_HEADS * LANES)
    v = wkv[..., QK_NOPE:]
    zv = jnp.zeros_like(v)
    even = (jnp.arange(N_HEADS) % 2 == 0)[None, :, None]
    wv = jnp.where(even, jnp.concatenate([v, zv], -1), jnp.concatenate([zv, v], -1)).reshape(KV_LORA, N_HEADS * LANES)
    return wq, wk, wv


def _shard_rows(g):
    return g.reshape(N_DEV, g.shape[0] // N_DEV, g.shape[1])


def kernel(x, positions, ffn1_norm, ffn1_w_gate, ffn1_w_up, ffn1_w_down, mix_norm, w_in, q_a_norm, w_q_b, kv_a_norm, w_kv_b, q_head_norm, k_head_norm, conv_w, conv_b, a_log_fwd, a_log_bwd, dt_bias_fwd, dt_bias_bwd, d_skip, ssm_norm, w_attn_branch, w_ssm_branch, w_out, ffn2_norm, ffn2_w_gate, ffn2_w_up, ffn2_w_down, loss_target, m_ffn1_norm, m_ffn1_w_gate, m_ffn1_w_up, m_ffn1_w_down, m_mix_norm, m_w_in, m_q_a_norm, m_w_q_b, m_kv_a_norm, m_w_kv_b, m_q_head_norm, m_k_head_norm, m_conv_w, m_conv_b, m_a_log_fwd, m_a_log_bwd, m_dt_bias_fwd, m_dt_bias_bwd, m_d_skip, m_ssm_norm, m_w_attn_branch, m_w_ssm_branch, m_w_out, m_ffn2_norm, m_ffn2_w_gate, m_ffn2_w_up, m_ffn2_w_down, v_ffn1_norm, v_ffn1_w_gate, v_ffn1_w_up, v_ffn1_w_down, v_mix_norm, v_w_in, v_q_a_norm, v_w_q_b, v_kv_a_norm, v_w_kv_b, v_q_head_norm, v_k_head_norm, v_conv_w, v_conv_b, v_a_log_fwd, v_a_log_bwd, v_dt_bias_fwd, v_dt_bias_bwd, v_d_skip, v_ssm_norm, v_w_attn_branch, v_w_ssm_branch, v_w_out, v_ffn2_norm, v_ffn2_w_gate, v_ffn2_w_up, v_ffn2_w_down):
    w_all = dict(ffn1_norm=ffn1_norm, ffn1_w_gate=ffn1_w_gate, ffn1_w_up=ffn1_w_up, ffn1_w_down=ffn1_w_down, mix_norm=mix_norm, w_in=w_in, q_a_norm=q_a_norm, w_q_b=w_q_b, kv_a_norm=kv_a_norm, w_kv_b=w_kv_b, q_head_norm=q_head_norm, k_head_norm=k_head_norm, conv_w=conv_w, conv_b=conv_b, a_log_fwd=a_log_fwd, a_log_bwd=a_log_bwd, dt_bias_fwd=dt_bias_fwd, dt_bias_bwd=dt_bias_bwd, d_skip=d_skip, ssm_norm=ssm_norm, w_attn_branch=w_attn_branch, w_ssm_branch=w_ssm_branch, w_out=w_out, ffn2_norm=ffn2_norm, ffn2_w_gate=ffn2_w_gate, ffn2_w_up=ffn2_w_up, ffn2_w_down=ffn2_w_down)
    m_all = dict(ffn1_norm=m_ffn1_norm, ffn1_w_gate=m_ffn1_w_gate, ffn1_w_up=m_ffn1_w_up, ffn1_w_down=m_ffn1_w_down, mix_norm=m_mix_norm, w_in=m_w_in, q_a_norm=m_q_a_norm, w_q_b=m_w_q_b, kv_a_norm=m_kv_a_norm, w_kv_b=m_w_kv_b, q_head_norm=m_q_head_norm, k_head_norm=m_k_head_norm, conv_w=m_conv_w, conv_b=m_conv_b, a_log_fwd=m_a_log_fwd, a_log_bwd=m_a_log_bwd, dt_bias_fwd=m_dt_bias_fwd, dt_bias_bwd=m_dt_bias_bwd, d_skip=m_d_skip, ssm_norm=m_ssm_norm, w_attn_branch=m_w_attn_branch, w_ssm_branch=m_w_ssm_branch, w_out=m_w_out, ffn2_norm=m_ffn2_norm, ffn2_w_gate=m_ffn2_w_gate, ffn2_w_up=m_ffn2_w_up, ffn2_w_down=m_ffn2_w_down)
    v_all = dict(ffn1_norm=v_ffn1_norm, ffn1_w_gate=v_ffn1_w_gate, ffn1_w_up=v_ffn1_w_up, ffn1_w_down=v_ffn1_w_down, mix_norm=v_mix_norm, w_in=v_w_in, q_a_norm=v_q_a_norm, w_q_b=v_w_q_b, kv_a_norm=v_kv_a_norm, w_kv_b=v_w_kv_b, q_head_norm=v_q_head_norm, k_head_norm=v_k_head_norm, conv_w=v_conv_w, conv_b=v_conv_b, a_log_fwd=v_a_log_fwd, a_log_bwd=v_a_log_bwd, dt_bias_fwd=v_dt_bias_fwd, dt_bias_bwd=v_dt_bias_bwd, d_skip=v_d_skip, ssm_norm=v_ssm_norm, w_attn_branch=v_w_attn_branch, w_ssm_branch=v_w_ssm_branch, w_out=v_w_out, ffn2_norm=v_ffn2_norm, ffn2_w_gate=v_ffn2_w_gate, ffn2_w_up=v_ffn2_w_up, ffn2_w_down=v_ffn2_w_down)
    T = x.shape[1]
    xs_in, target = x[0], loss_target[0]
    w2 = {n: a.reshape(-1, a.shape[-1]) for n, a in w_all.items()}
    m2 = {n: m_all[n].reshape(w2[n].shape) for n in w2}
    v2 = {n: v_all[n].reshape(w2[n].shape) for n in w2}
    p = {n: w2[n] for n, _ in SMALL}
    bf = lambda n: w2[n].astype(BF16)

    early = ["ffn1_w_gate", "ffn1_w_up", "ffn1_w_down", "w_in", "w_q_b", "w_kv_b"]
    g_early = _all_gather_two_level([bf(n) for n in early] + [w2["conv_w"]], name="gather_early")
    ge = dict(zip(early + ["conv_w"], g_early))
    w_gu1 = jnp.concatenate([_cols(ge["ffn1_w_gate"]), _cols(ge["ffn1_w_up"])], axis=1)
    w_d1 = _rows(ge["ffn1_w_down"])
    w_small, w_big = _in_proj_weights(_cols(ge["w_in"]))
    wq, wk, wv = _mla_up_weights(_cols(ge["w_q_b"]), _cols(ge["w_kv_b"]))
    conv_full = _cols(ge["conv_w"])
    late = ["w_attn_branch", "w_ssm_branch", "w_out", "ffn2_w_gate", "ffn2_w_up", "ffn2_w_down"]
    late_shards = [bf(n) for n in late]

    tabs = _rope_tables(positions, T)
    qg, kg = _pad_lanes(p["q_head_norm"]), _pad_lanes(p["k_head_norm"])
    bias128 = _pad_lanes(jnp.concatenate([p["dt_bias_fwd"], p["dt_bias_bwd"]], axis=1))
    alog128 = _pad_lanes(jnp.concatenate([p["a_log_fwd"], p["a_log_bwd"]], axis=1))
    skip_x = jnp.repeat(p["d_skip"], 64, axis=1)

    x1, ffn1_saved = _ffn_fwd(xs_in, p["ffn1_norm"], w_gu1, w_d1, "ffn1")
    h2 = _rms_fwd(x1, p["mix_norm"], name="mix_rms")
    u_big = _mm(h2, w_big, name="in_big")
    u_small = _mm(h2, w_small, name="in_small")
    cqn, ckvn = _lora_norm_fwd(u_small, p["q_a_norm"], p["kv_a_norm"], name="lora_norm")
    q_raw = _mm(cqn, wq, name="q_up")
    k_raw = _mm(ckvn, wk, name="k_up")
    v = _mm(ckvn, wv, name="v_up", out_dtype=BF16)
    q, k = _qk_prep_fwd(q_raw, k_raw, u_small, tabs, qg, kg, name="qk_prep")
    a_out, lse, g_late = _attn_fwd(q, k, v, ["gather"] * len(late), late_shards, name="attn_fwd")
    gl = dict(zip(late, g_late))
    w_pa, w_pb, w_o = _rows(gl["w_attn_branch"]), _rows(gl["w_ssm_branch"]), _rows(gl["w_out"])
    w_gu2 = jnp.concatenate([_cols(gl["ffn2_w_gate"]), _cols(gl["ffn2_w_up"])], axis=1)
    w_d2 = _rows(gl["ffn2_w_down"])
    xbc_act = _conv_fwd(u_big, conv_full, p["conv_b"], name="conv_fwd")
    y_f, hin_f = _ssd_fwd(xbc_act, u_small, bias128, alog128, rev=False, name="ssd_fwd_f")
    y_b, hin_b = _ssd_fwd(xbc_act, u_small, bias128, alog128, rev=True, name="ssd_fwd_b")
    m_out = _ssm_out_fwd(y_f, y_b, xbc_act, u_big, skip_x, p["ssm_norm"], name="ssm_out")
    pa = _mm(a_out, w_pa, name="branch_a")
    pb = _mm(m_out, w_pb, name="branch_b")
    merged = _merge_fwd(pa, pb, u_big, name="merge")
    x2 = _mm(merged, w_o, name="mix_out", res=x1)
    y, ffn2_saved = _ffn_fwd(x2, p["ffn2_norm"], w_gu2, w_d2, "ffn2")
    dy, loss_part = _loss_bwd(y, target, name="loss")
    loss = lax.psum(loss_part, ("x", "y", "c"))

    gs = {}
    dx2, gs["ffn2_norm"], g_gate2, g_up2, g_down2 = _ffn_bwd(dy, x2, p["ffn2_norm"], w_gu2, w_d2, ffn2_saved, "ffn2b")
    dmerged = _mm(dx2, w_o, name="d_merged", tb=True)
    g_out = _mm(merged, dx2, name="d_w_out", ta=True, out_dtype=BF16)
    dpa, dpb, dga, dgb = _merge_bwd(dmerged, pa, pb, u_big, name="d_merge")
    g_pa = _mm(a_out, dpa, name="d_w_pa", ta=True, out_dtype=BF16)
    g_pb = _mm(m_out, dpb, name="d_w_pb", ta=True, out_dtype=BF16)
    da_out = _mm(dpa, w_pa, name="d_a", tb=True)
    dm_out = _mm(dpb, w_pb, name="d_m", tb=True)
    late_grads = [_shard_rows(g) for g in (g_pa, g_pb, g_out, g_gate2, g_up2, g_down2)]
    dq, dk, dv, r_late = _attn_bwd(q, k, v, a_out, lse, da_out, ["scatter"] * len(late_grads), late_grads,
                                   name="attn_bwd")
    recv = dict(zip(late, r_late))

    dyss, dz, gs["ssm_norm"], dskip_ch = _ssm_out_bwd(dm_out, y_f, y_b, xbc_act, u_big, skip_x, p["ssm_norm"],
                                                      name="d_ssm_out")
    dact_f, draw_f, dalog_f, dbias_f = _ssd_bwd(dyss, xbc_act, u_small, bias128, alog128, hin_f, skip_x,
                                                rev=False, name="ssd_bwd_f")
    dact_b, draw_b, dalog_b, dbias_b = _ssd_bwd(dyss, xbc_act, u_small, bias128, alog128, hin_b, None,
                                                rev=True, name="ssd_bwd_b")
    dxbc, g_conv, gs["conv_b"] = _conv_bwd(dact_f, dact_b, u_big, conv_full, p["conv_b"], name="conv_bwd")

    dq_raw, dk_raw, dkpe, gqh, gkh = _qk_prep_bwd(dq, dk, q_raw, k_raw, u_small, tabs, qg, kg, name="d_qk_prep")
    g_wq_t = _mm(dq_raw, cqn, name="d_w_q", ta=True, out_dtype=BF16)
    g_wk_t = _mm(dk_raw, ckvn, name="d_w_k", ta=True, out_dtype=BF16)
    g_wv_t = _mm(dv, ckvn, name="d_w_v", ta=True, out_dtype=BF16)
    dcqn = _mm(dq_raw, wq, name="d_cqn", tb=True)
    dckvn = _mm(dk_raw, wk, name="d_ckvn_k", tb=True)
    dckvn = _mm(dv, wv, name="d_ckvn_v", tb=True, res=dckvn)
    du_small, gs["q_a_norm"], gkv = _lora_norm_bwd(dcqn, dckvn, u_small, p["q_a_norm"], p["kv_a_norm"], dkpe,
                                                   draw_f, draw_b, name="d_lora_norm")

    dh2 = _mm(du_small, w_small, name="d_h2_small", tb=True)
    dh2 = _mm(dz, w_big[:, 0:2048], name="d_h2_z", tb=True, res=dh2)
    dh2 = _mm(dxbc, w_big[:, 2048:5120], name="d_h2_xbc", tb=True, res=dh2)
    dh2 = _mm(dga, w_big[:, 5120:6144], name="d_h2_ga", tb=True, res=dh2)
    dh2 = _mm(dgb, w_big[:, 6144:7168], name="d_h2_gb", tb=True, res=dh2)
    gt_small = _mm(du_small, h2, name="d_w_small", ta=True, out_dtype=BF16)
    gt_z = _mm(dz, h2, name="d_w_z", ta=True, out_dtype=BF16)
    gt_xbc = _mm(dxbc, h2, name="d_w_xbc", ta=True, out_dtype=BF16)
    gt_ga = _mm(dga, h2, name="d_w_ga", ta=True, out_dtype=BF16)
    gt_gb = _mm(dgb, h2, name="d_w_gb", ta=True, out_dtype=BF16)
    dx1, gs["mix_norm"] = _rms_bwd(dh2, x1, p["mix_norm"], dx2, name="d_mix_rms")
    grad_x, gs["ffn1_norm"], g_gate1, g_up1, g_down1 = _ffn_bwd(dx1, xs_in, p["ffn1_norm"], w_gu1, w_d1, ffn1_saved,
                                                                "ffn1b")

    gt_in = jnp.concatenate([gt_small[0:384], gt_small[U_CKV:U_KPE + QK_ROPE], gt_z, gt_xbc,
                             gt_small[U_DT:U_DT + 64], gt_ga, gt_gb], axis=0)
    gt_in = jnp.pad(gt_in.reshape(N_DEV, W_IN_SHARD, D_MODEL), ((0, 0), (0, W_IN_SHARD_PAD - W_IN_SHARD), (0, 0)))
    gt_q = g_wq_t.reshape(N_HEADS, LANES, Q_LORA)[:, :QK_HEAD].reshape(N_DEV, -1, Q_LORA)
    gk3 = g_wk_t.reshape(N_HEADS, LANES, KV_LORA)[:, :QK_NOPE]
    gv3 = g_wv_t.reshape(N_HEADS, LANES, KV_LORA)
    even = (jnp.arange(N_HEADS) % 2 == 0)[:, None, None]
    gv3 = jnp.where(even, gv3[:, :V_HEAD], gv3[:, V_HEAD:])
    gt_kv = jnp.concatenate([gk3, gv3], axis=1).reshape(N_DEV, -1, KV_LORA)
    gsmall = _small_slab(gs, dskip_ch, dalog_f, dalog_b, dbias_f, dbias_b, gkv, gqh, gkh, g_conv, name="small_slab")
    last = ["ffn1_w_gate", "ffn1_w_up", "ffn1_w_down", "w_in", "w_q_b", "w_kv_b"]
    last_grads = [_shard_rows(g_gate1), _shard_rows(g_up1), _shard_rows(g_down1), gt_in, gt_q, gt_kv]
    r_last = _exchange(["scatter"] * len(last) + ["gather"], last_grads + [gsmall], name="grad_exchange")
    recv.update(zip(last, r_last[:-1]))
    srecv = r_last[-1]

    out = {}
    for n in ("ffn1_w_down", "ffn2_w_down", "w_attn_branch", "w_ssm_branch", "w_out"):
        out[n] = _reduce_adamw(recv[n], w2[n], m2[n], v2[n], name=f"adamw_{n}")
    for n in ("ffn1_w_gate", "ffn1_w_up", "ffn2_w_gate", "ffn2_w_up", "w_q_b", "w_kv_b"):
        out[n] = _reduce_t_adamw(recv[n], w2[n], m2[n], v2[n], name=f"adamw_{n}")
    g_in = _reduce8(recv["w_in"], name="sum_w_in", tile=W_IN_SHARD_PAD // 2)[:W_IN_SHARD].T
    out["w_in"] = [g_in] + list(_adamw(g_in, w2["w_in"], m2["w_in"], v2["w_in"], name="adamw_w_in"))
    me = 4 * lax.axis_index("x") + 2 * lax.axis_index("y") + lax.axis_index("c")
    conv_g = lax.dynamic_slice(srecv, (0, CONV_ROW, me * (XBC_DIM // N_DEV)), (N_DEV, CONV_WIDTH, XBC_DIM // N_DEV))
    sn = [n for n, _ in SMALL] + ["conv_w"]
    sg, sd, sm, sv = _adamw_small(srecv, conv_g, [w2[n] for n in sn], [m2[n] for n in sn], [v2[n] for n in sn],
                                  name="adamw_small")
    for i, n in enumerate(sn):
        out[n] = (sg[i], sd[i], sm[i], sv[i])
    outs = [[out[n][kind].reshape(w_all[n].shape) for n in WEIGHT_ORDER] for kind in range(4)]
    return (loss, grad_x[None], *outs[0], *outs[1], *outs[2], *outs[3])
```

```python
import math

import jax
import jax.numpy as jnp
from jax import lax
from jax.experimental import pallas as pl
from jax.experimental.pallas import tpu as pltpu

F32, BF16 = jnp.float32, jnp.bfloat16
HIGHEST = lax.Precision.HIGHEST

D_MODEL, D_FF = 1024, 2816
EPS = 1e-6
N_HEADS, QK_NOPE, QK_ROPE, QK_HEAD, V_HEAD = 16, 64, 32, 96, 64
Q_LORA, KV_LORA = 384, 256
ROPE_BASE = 10000.0
D_INNER, SSM_HEADS, SSM_GROUPS, D_STATE, CONV_WIDTH, CHUNK = 2048, 32, 4, 128, 5, 128
XBC_DIM = D_INNER + 2 * SSM_GROUPS * D_STATE
IN_DIM = 7904
ADAM_LR, ADAM_B1, ADAM_B2, ADAM_EPS, ADAM_WD, ADAM_STEP = 0.001, 0.9, 0.999, 1e-08, 0.01, 10
N_DEV = 8

V7X_VMEM_BYTES = 64 * 1024 * 1024
VMEM_LIMIT = V7X_VMEM_BYTES - 8 * 1024 * 1024
LANES = 128
W_IN_SHARD = IN_DIM // N_DEV
W_IN_SHARD_PAD = 992

SMALL = (
    ("ffn1_norm", 1024), ("mix_norm", 1024), ("q_a_norm", 384), ("kv_a_norm", 256), ("q_head_norm", 96),
    ("k_head_norm", 96), ("conv_b", 3072), ("a_log_fwd", 32), ("a_log_bwd", 32), ("dt_bias_fwd", 32),
    ("dt_bias_bwd", 32), ("d_skip", 32), ("ssm_norm", 2048), ("ffn2_norm", 1024),
)
TRANSPOSED = ("ffn1_w_gate", "ffn1_w_up", "ffn2_w_gate", "ffn2_w_up", "w_in", "w_q_b")
SMALL_ROW = {n: i for i, (n, _) in enumerate(SMALL)}
CONV_ROW = len(SMALL)
SMALL_ROWS, SMALL_COLS = 24, XBC_DIM
WEIGHT_ORDER = (
    "ffn1_norm", "ffn1_w_gate", "ffn1_w_up", "ffn1_w_down", "mix_norm", "w_in", "q_a_norm", "w_q_b", "kv_a_norm",
    "w_kv_b", "q_head_norm", "k_head_norm", "conv_w", "conv_b", "a_log_fwd", "a_log_bwd", "dt_bias_fwd", "dt_bias_bwd",
    "d_skip", "ssm_norm", "w_attn_branch", "w_ssm_branch", "w_out", "ffn2_norm", "ffn2_w_gate", "ffn2_w_up",
    "ffn2_w_down",
)


def _pallas(body, **kw):
    return pl.pallas_call(body, **kw)


def _params(sem):
    return pltpu.CompilerParams(dimension_semantics=sem, vmem_limit_bytes=VMEM_LIMIT)


def _pick(dim, pref):
    if dim <= pref:
        return dim
    c = (pref // LANES) * LANES
    while c >= LANES:
        if dim % c == 0:
            return c
        c -= LANES
    raise ValueError((dim, pref))


def _sigmoid(x):
    return 1.0 / (1.0 + jnp.exp(-x))


def _softplus(x):
    return jnp.maximum(x, 0.0) + jnp.log(1.0 + jnp.exp(-jnp.abs(x)))


def _dot(a, b):
    return jnp.dot(a, b, preferred_element_type=F32)


def _dot_nt(a, b):
    return lax.dot_general(a, b, (((1,), (1,)), ((), ())), preferred_element_type=F32)


def _dot_tn(a, b):
    return lax.dot_general(a, b, (((0,), (0,)), ((), ())), preferred_element_type=F32)


def _dot_h(a, b):
    return jnp.dot(a, b, preferred_element_type=F32, precision=HIGHEST)


def _dot_h_nt(a, b):
    return lax.dot_general(a, b, (((1,), (1,)), ((), ())), preferred_element_type=F32, precision=HIGHEST)


def _dot_h_tn(a, b):
    return lax.dot_general(a, b, (((0,), (0,)), ((), ())), preferred_element_type=F32, precision=HIGHEST)


def _mesh_pos():
    return lax.axis_index("x"), lax.axis_index("y"), lax.axis_index("c")


def _comm_scratch(n):
    return [pltpu.SemaphoreType.DMA((7 * n,)), pltpu.SemaphoreType.DMA((7 * n,)), pltpu.SemaphoreType.DMA((n,))]


def _comm_copies(modes, srcs, dsts, send_sems, recv_sems, local_sems, arrivals):
    x, y, c = _mesh_pos()
    me = 4 * x + 2 * y + c
    local, remote = [], []
    for w, (mode, s, d) in enumerate(zip(modes, srcs, dsts)):
        gather = mode == "gather"
        if not arrivals:
            local.append(pltpu.make_async_copy(s if gather else s.at[me], d.at[me], local_sems.at[w]))
        for k in range(1, N_DEV):
            px = (1 - x) if (k & 4) else x
            py = (1 - y) if (k & 2) else y
            pc = (1 - c) if (k & 1) else c
            peer = 4 * px + 2 * py + pc
            idx = 7 * w + k - 1
            remote.append(pltpu.make_async_remote_copy(
                src_ref=s if gather else s.at[peer], dst_ref=d.at[peer] if arrivals else d.at[me],
                send_sem=send_sems.at[idx], recv_sem=recv_sems.at[idx],
                device_id=(px, py, pc), device_id_type=pl.DeviceIdType.MESH))
    return local, remote


def _comm_start(modes, srcs, dsts, sems):
    local, sends = _comm_copies(modes, srcs, dsts, *sems, arrivals=False)
    for cp in local + sends:
        cp.start()


def _comm_wait(modes, srcs, dsts, sems):
    _, recvs = _comm_copies(modes, srcs, dsts, *sems, arrivals=True)
    for cp in recvs:
        cp.wait_recv()
    local, sends = _comm_copies(modes, srcs, dsts, *sems, arrivals=False)
    for cp in sends:
        cp.wait_send()
    for cp in local:
        cp.wait()


def _comm_out_shapes(modes, arrays):
    return [jax.ShapeDtypeStruct((N_DEV,) + (a.shape if m == "gather" else a.shape[1:]), a.dtype)
            for m, a in zip(modes, arrays)]


def _exchange(modes, arrays, *, name):
    n = len(arrays)

    def body(*refs):
        srcs, dsts, sems = refs[:n], refs[n:2 * n], refs[2 * n:]
        _comm_start(modes, srcs, dsts, sems)
        _comm_wait(modes, srcs, dsts, sems)

    any_spec = pl.BlockSpec(memory_space=pl.ANY)
    return _pallas(body, name=name, out_shape=_comm_out_shapes(modes, arrays), in_specs=[any_spec] * n,
                   out_specs=[any_spec] * n, scratch_shapes=_comm_scratch(n))(*arrays)


def _all_gather_two_level(shards, *, name):
    n = len(shards)

    def body(*refs):
        srcs, outs = refs[:n], refs[n:2 * n]
        send_sems, recv_sems, local_sems = refs[2 * n:]
        x, y, c = _mesh_pos()
        me, sibling = (x, y, c), (x, y, 1 - c)
        chips = [(1 - x, y), (x, 1 - y), (1 - x, 1 - y)]

        def blk(w, px, py, pc):
            return outs[w].at[4 * px + 2 * py + pc]

        def copy(w, k, block, to, src=None):
            return pltpu.make_async_remote_copy(
                src_ref=blk(w, *block) if src is None else src, dst_ref=blk(w, *block),
                send_sem=send_sems.at[7 * w + k], recv_sem=recv_sems.at[7 * w + k], device_id=to,
                device_id_type=pl.DeviceIdType.MESH)

        mine = [pltpu.make_async_copy(srcs[w], blk(w, *me), local_sems.at[w]) for w in range(n)]
        for cp in mine:
            cp.start()
        first = []
        for w in range(n):
            first.append(copy(w, 0, me, sibling, src=srcs[w]))
            first += [copy(w, 1 + j, me, (*chip, c), src=srcs[w]) for j, chip in enumerate(chips)]
        for cp in first:
            cp.start()
        passed = []
        for w in range(n):
            for j, chip in enumerate(chips):
                copy(w, 1 + j, (*chip, c), me).wait_recv()
                fwd = copy(w, 4 + j, (*chip, c), sibling)
                fwd.start()
                passed.append(fwd)
        for w in range(n):
            copy(w, 0, sibling, me).wait_recv()
            for j, chip in enumerate(chips):
                copy(w, 4 + j, (*chip, 1 - c), me).wait_recv()
        for cp in first + passed:
            cp.wait_send()
        for cp in mine:
            cp.wait()

    any_spec = pl.BlockSpec(memory_space=pl.ANY)
    return _pallas(body, name=name, out_shape=_comm_out_shapes(["gather"] * n, shards), in_specs=[any_spec] * n,
                   out_specs=[any_spec] * n, scratch_shapes=_comm_scratch(n))(*shards)


def _mm(a, b, *, name, ta=False, tb=False, out_dtype=F32, alpha=1.0, res=None, tm=1024, tn=1408, tk=1408,
        b_row0=None):
    (K, M) = a.shape if ta else a.shape[::-1]
    (N, Kb) = b.shape if tb else b.shape[::-1]
    tm, tn, tk = _pick(M, tm), _pick(N, tn), _pick(K, tk)
    nk = K // tk
    if b_row0 is None:
        assert K == Kb, (a.shape, b.shape, ta, tb)
        kb0 = 0
    else:
        assert not tb and b_row0 % tk == 0 and b_row0 + K <= Kb, (a.shape, b.shape, b_row0)
        kb0 = b_row0 // tk
    a_spec = pl.BlockSpec((tk, tm), lambda i, j, k: (k, i)) if ta else pl.BlockSpec((tm, tk), lambda i, j, k: (i, k))
    b_spec = (pl.BlockSpec((tn, tk), lambda i, j, k: (j, k)) if tb
              else pl.BlockSpec((tk, tn), lambda i, j, k: (k + kb0, j)))
    o_spec = pl.BlockSpec((tm, tn), lambda i, j, k: (i, j))
    dn = (((0 if ta else 1,), (1 if tb else 0,)), ((), ()))
    has_res = res is not None

    def body(*refs):
        a_ref, b_ref = refs[0], refs[1]
        r_ref = refs[2] if has_res else None
        o_ref = refs[3] if has_res else refs[2]
        part = lax.dot_general(a_ref[...].astype(BF16), b_ref[...].astype(BF16), dn, preferred_element_type=F32)

        def finish(acc):
            if alpha != 1.0:
                acc = acc * alpha
            if has_res:
                acc = acc + r_ref[...]
            o_ref[...] = acc.astype(o_ref.dtype)

        if nk == 1:
            finish(part)
        else:
            acc_ref = refs[-1]
            k = pl.program_id(2)

            @pl.when(k == 0)
            def _():
                acc_ref[...] = part

            @pl.when(k > 0)
            def _():
                acc_ref[...] += part

            @pl.when(k == nk - 1)
            def _():
                finish(acc_ref[...])

    ins = [a, b] + ([res] if has_res else [])
    in_specs = [a_spec, b_spec] + ([o_spec] if has_res else [])
    return _pallas(
        body, name=name, grid=(M // tm, N // tn, nk), in_specs=in_specs, out_specs=o_spec,
        out_shape=jax.ShapeDtypeStruct((M, N), out_dtype),
        scratch_shapes=[pltpu.VMEM((tm, tn), F32)] if nk > 1 else [],
        compiler_params=_params(("parallel", "parallel", "arbitrary")),
    )(*ins)


def _col0(j):
    return 0


def _colj(j):
    return j


def _rowmap(fn, *, name, rows, tile, ins, consts=(), outs=(), accs=(), ncol=1):
    tile = min(tile, rows)
    nrow = rows // tile
    in_specs = [pl.BlockSpec((tile, w), lambda j, i, f=f: (i, f(j))) for _, w, f in ins]
    for arr, w, f in consts:
        in_specs.append(pl.BlockSpec((arr.shape[0], w), lambda j, i, f=f: (0, f(j))))
    out_specs = [pl.BlockSpec((tile, w), lambda j, i, f=f: (i, f(j))) for _, _, w, f in outs]
    out_specs += [pl.BlockSpec((1, w), lambda j, i, f=f: (0, f(j))) for _, w, f in accs]
    out_shape = [jax.ShapeDtypeStruct((rows, c), dt) for c, dt, _, _ in outs]
    out_shape += [jax.ShapeDtypeStruct((1, c), F32) for c, _, _ in accs]
    n_in, n_out = len(ins) + len(consts), len(outs)
    acc_fixed = [f is _col0 for _, _, f in accs]

    def body(*refs):
        res = fn(*[r[...].astype(F32) for r in refs[:n_in]])
        if not isinstance(res, (tuple, list)):
            res = (res,)
        for r, v in zip(refs[n_in:n_in + n_out], res[:n_out]):
            r[...] = v.astype(r.dtype)
        j, i = pl.program_id(0), pl.program_id(1)
        for r, v, fixed in zip(refs[n_in + n_out:], res[n_out:], acc_fixed):
            first = ((i == 0) & (j == 0)) if fixed else (i == 0)

            @pl.when(first)
            def _(r=r, v=v):
                r[...] = v

            @pl.when(jnp.logical_not(first))
            def _(r=r, v=v):
                r[...] += v

    arrays = [a for a, _, _ in ins] + [a for a, _, _ in consts]
    return _pallas(
        body, name=name, grid=(ncol, nrow), in_specs=in_specs, out_specs=out_specs, out_shape=out_shape,
        compiler_params=_params(("arbitrary", "arbitrary")),
    )(*arrays)


def _rms_fwd(x, g, *, name, tile=512):
    rows, d = x.shape

    def fn(xv, gv):
        r = lax.rsqrt(jnp.mean(xv * xv, axis=-1, keepdims=True) + EPS)
        return xv * r * gv

    return _rowmap(fn, name=name, rows=rows, tile=tile, ins=[(x, d, _col0)], consts=[(g, d, _col0)],
                   outs=[(d, BF16, d, _col0)])[0]


def _rms_bwd(dh, x, g, res, *, name, tile=512):
    rows, d = x.shape

    def fn(dhv, xv, rv, gv):
        r = lax.rsqrt(jnp.mean(xv * xv, axis=-1, keepdims=True) + EPS)
        xh = xv * r
        dxh = dhv * gv
        dx = r * (dxh - xh * jnp.mean(dxh * xh, axis=-1, keepdims=True))
        return rv + dx, jnp.sum(dhv * xh, axis=0, keepdims=True)

    return _rowmap(fn, name=name, rows=rows, tile=tile, ins=[(dh, d, _col0), (x, d, _col0), (res, d, _col0)],
                   consts=[(g, d, _col0)], outs=[(d, F32, d, _col0)], accs=[(d, d, _col0)])


def _swiglu_fwd(gu, *, name, tile=512):
    rows = gu.shape[0]
    w = _pick(D_FF, 1408)
    nb = D_FF // w

    def fn(gv, uv):
        return gv * _sigmoid(gv) * uv

    return _rowmap(fn, name=name, rows=rows, tile=tile, ncol=nb,
                   ins=[(gu, w, _colj), (gu, w, lambda j: j + nb)], outs=[(D_FF, BF16, w, _colj)])[0]


def _swiglu_bwd(da, gu, *, name, tile=512):
    rows = gu.shape[0]
    w = _pick(D_FF, 1408)
    nb = D_FF // w

    def fn(dav, gv, uv):
        sg = _sigmoid(gv)
        dg = dav * uv * (sg * (1.0 + gv * (1.0 - sg)))
        du = dav * (gv * sg)
        return dg, du

    return _rowmap(fn, name=name, rows=rows, tile=tile, ncol=nb,
                   ins=[(da, w, _colj), (gu, w, _colj), (gu, w, lambda j: j + nb)],
                   outs=[(D_FF, BF16, w, _colj), (D_FF, BF16, w, _colj)])


U_CKV, U_KPE, U_DT = 512, 768, 896


def _lora_norm_fwd(u_small, qg, kvg, *, name, tile=512):
    rows = u_small.shape[0]

    def fn(cq, ckv, qgv, kgv):
        rq = lax.rsqrt(jnp.mean(cq * cq, axis=-1, keepdims=True) + EPS)
        rk = lax.rsqrt(jnp.mean(ckv * ckv, axis=-1, keepdims=True) + EPS)
        return cq * rq * qgv, ckv * rk * kgv

    return _rowmap(fn, name=name, rows=rows, tile=tile,
                   ins=[(u_small, Q_LORA, _col0), (u_small, KV_LORA, lambda j: U_CKV // KV_LORA)],
                   consts=[(qg, Q_LORA, _col0), (kvg, KV_LORA, _col0)],
                   outs=[(Q_LORA, BF16, Q_LORA, _col0), (KV_LORA, BF16, KV_LORA, _col0)])


def _lora_norm_bwd(dcqn, dckvn, u_small, qg, kvg, dkpe, draw_f, draw_b, *, name, tile=512):
    rows = u_small.shape[0]
    tile = min(tile, rows)

    def body(dq_ref, dk_ref, u_ref, dkp_ref, df_ref, db_ref, qg_ref, kg_ref, du_ref, gq_ref, gk_ref):
        cq, ckv = u_ref[:, 0:Q_LORA], u_ref[:, U_CKV:U_CKV + KV_LORA]
        dq, dk = dq_ref[...], dk_ref[...]
        rq = lax.rsqrt(jnp.mean(cq * cq, axis=-1, keepdims=True) + EPS)
        xh = cq * rq
        dxh = dq * qg_ref[...]
        du_ref[:, 0:Q_LORA] = (rq * (dxh - xh * jnp.mean(dxh * xh, axis=-1, keepdims=True))).astype(BF16)
        du_ref[:, Q_LORA:U_CKV] = jnp.zeros((tile, U_CKV - Q_LORA), BF16)
        rk = lax.rsqrt(jnp.mean(ckv * ckv, axis=-1, keepdims=True) + EPS)
        kh = ckv * rk
        dkh = dk * kg_ref[...]
        du_ref[:, U_CKV:U_KPE] = (rk * (dkh - kh * jnp.mean(dkh * kh, axis=-1, keepdims=True))).astype(BF16)
        du_ref[:, U_KPE:U_DT] = dkp_ref[...].astype(BF16)
        du_ref[:, U_DT:U_DT + LANES] = (df_ref[...] + db_ref[...]).astype(BF16)
        gq = jnp.sum(dq * xh, axis=0, keepdims=True)
        gk = jnp.sum(dk * kh, axis=0, keepdims=True)
        i = pl.program_id(0)

        @pl.when(i == 0)
        def _():
            gq_ref[...] = gq
            gk_ref[...] = gk

        @pl.when(i > 0)
        def _():
            gq_ref[...] += gq
            gk_ref[...] += gk

    def rowblk(w):
        return pl.BlockSpec((tile, w), lambda i: (i, 0))

    def whole(w):
        return pl.BlockSpec((1, w), lambda i: (0, 0))

    return _pallas(
        body, name=name, grid=(rows // tile,),
        in_specs=[rowblk(Q_LORA), rowblk(KV_LORA), rowblk(1024), rowblk(LANES), rowblk(LANES), rowblk(LANES),
                  whole(Q_LORA), whole(KV_LORA)],
        out_specs=[rowblk(1024), whole(Q_LORA), whole(KV_LORA)],
        out_shape=[jax.ShapeDtypeStruct((rows, 1024), BF16), jax.ShapeDtypeStruct((1, Q_LORA), F32),
                   jax.ShapeDtypeStruct((1, KV_LORA), F32)],
        compiler_params=_params(("arbitrary",)),
    )(dcqn, dckvn, u_small, dkpe, draw_f, draw_b, qg, kvg)


def _rope(x, c, s1, s2):
    return x * c + pltpu.roll(x, 112, 1) * s1 + pltpu.roll(x, 16, 1) * s2


def _rope_t(d, c, s1, s2):
    return d * c + pltpu.roll(d * s1, 16, 1) + pltpu.roll(d * s2, 112, 1)


def _qk_prep_fwd(q_raw, k_raw, u_small, tabs, qg, kg, *, name, tile=256):
    rows = q_raw.shape[0]
    tile = min(tile, rows)
    scale = 1.0 / math.sqrt(QK_HEAD)

    def body(q_ref, k_ref, u_ref, c_ref, s1_ref, s2_ref, qg_ref, kg_ref, qo_ref, ko_ref):
        c, s1, s2 = c_ref[...], s1_ref[...], s2_ref[...]
        qgv, kgv = qg_ref[...], kg_ref[...]
        kpe = pltpu.roll(u_ref[:, U_KPE:U_KPE + LANES], 64, 1)
        for h in range(N_HEADS):
            hs = slice(h * LANES, (h + 1) * LANES)
            qr = q_ref[:, hs]
            rq = lax.rsqrt(jnp.sum(qr * qr, axis=-1, keepdims=True) / QK_HEAD + EPS)
            qo_ref[:, hs] = (_rope(qr * rq * qgv, c, s1, s2) * scale).astype(BF16)
            xk = k_ref[:, hs] + kpe
            rk = lax.rsqrt(jnp.sum(xk * xk, axis=-1, keepdims=True) / QK_HEAD + EPS)
            ko_ref[:, hs] = _rope(xk * rk * kgv, c, s1, s2).astype(BF16)

    wide = pl.BlockSpec((tile, 2048), lambda i: (i, 0))
    narrow = pl.BlockSpec((tile, LANES), lambda i: (i, 0))
    gain = pl.BlockSpec((1, LANES), lambda i: (0, 0))
    return _pallas(
        body, name=name, grid=(rows // tile,),
        in_specs=[wide, wide, pl.BlockSpec((tile, 1024), lambda i: (i, 0)), narrow, narrow, narrow, gain, gain],
        out_specs=[wide, wide], out_shape=[jax.ShapeDtypeStruct((rows, 2048), BF16)] * 2,
        compiler_params=_params(("parallel",)),
    )(q_raw, k_raw, u_small, *tabs, qg, kg)


def _qk_prep_bwd(dq, dk, q_raw, k_raw, u_small, tabs, qg, kg, *, name, tile=256):
    rows = q_raw.shape[0]
    tile = min(tile, rows)
    scale = 1.0 / math.sqrt(QK_HEAD)

    def body(dq_ref, dk_ref, q_ref, k_ref, u_ref, c_ref, s1_ref, s2_ref, qg_ref, kg_ref,
             dqo_ref, dko_ref, dkpe_ref, gq_ref, gk_ref):
        c, s1, s2 = c_ref[...], s1_ref[...], s2_ref[...]
        qgv, kgv = qg_ref[...], kg_ref[...]
        kpe = pltpu.roll(u_ref[:, U_KPE:U_KPE + LANES], 64, 1)
        lane = lax.broadcasted_iota(jnp.int32, (tile, LANES), 1)
        gq = jnp.zeros((1, LANES), F32)
        gk = jnp.zeros((1, LANES), F32)
        dkpe = jnp.zeros((tile, LANES), F32)
        for h in range(N_HEADS):
            hs = slice(h * LANES, (h + 1) * LANES)
            qr = q_ref[:, hs]
            rq = lax.rsqrt(jnp.sum(qr * qr, axis=-1, keepdims=True) / QK_HEAD + EPS)
            xh = qr * rq
            dy = _rope_t(dq_ref[:, hs] * scale, c, s1, s2)
            dxh = dy * qgv
            dqo_ref[:, hs] = (rq * (dxh - xh * (jnp.sum(dxh * xh, axis=-1, keepdims=True) / QK_HEAD))).astype(BF16)
            gq = gq + jnp.sum(dy * xh, axis=0, keepdims=True)
            xk = k_ref[:, hs] + kpe
            rk = lax.rsqrt(jnp.sum(xk * xk, axis=-1, keepdims=True) / QK_HEAD + EPS)
            kh = xk * rk
            dyk = _rope_t(dk_ref[:, hs], c, s1, s2)
            dkh = dyk * kgv
            dxk = rk * (dkh - kh * (jnp.sum(dkh * kh, axis=-1, keepdims=True) / QK_HEAD))
            gk = gk + jnp.sum(dyk * kh, axis=0, keepdims=True)
            dko_ref[:, hs] = jnp.where(lane < QK_NOPE, dxk, 0.0).astype(BF16)
            dkpe = dkpe + dxk
        dkpe_ref[...] = jnp.where(lane < QK_ROPE, pltpu.roll(dkpe, 64, 1), 0.0)
        i = pl.program_id(0)

        @pl.when(i == 0)
        def _():
            gq_ref[...] = gq
            gk_ref[...] = gk

        @pl.when(i > 0)
        def _():
            gq_ref[...] += gq
            gk_ref[...] += gk

    wide = pl.BlockSpec((tile, 2048), lambda i: (i, 0))
    narrow = pl.BlockSpec((tile, LANES), lambda i: (i, 0))
    gain = pl.BlockSpec((1, LANES), lambda i: (0, 0))
    return _pallas(
        body, name=name, grid=(rows // tile,),
        in_specs=[wide, wide, wide, wide, pl.BlockSpec((tile, 1024), lambda i: (i, 0)), narrow, narrow, narrow,
                  gain, gain],
        out_specs=[wide, wide, narrow, gain, gain],
        out_shape=[jax.ShapeDtypeStruct((rows, 2048), BF16)] * 2
        + [jax.ShapeDtypeStruct((rows, LANES), F32), jax.ShapeDtypeStruct((1, LANES), F32),
           jax.ShapeDtypeStruct((1, LANES), F32)],
        compiler_params=_params(("arbitrary",)),
    )(dq, dk, q_raw, k_raw, u_small, *tabs, qg, kg)


def _attn_fwd(q, k, v, comm_modes, comm_arrays, *, name, tq=512, tkc=512):
    T = q.shape[0]
    tq = min(tq, T)
    tkc = min(tkc, T)
    n = len(comm_arrays)
    nj, ni = N_HEADS // 2, T // tq

    def body(*refs):
        q_ref, k_ref, v_ref = refs[:3]
        srcs = refs[3:3 + n]
        o_ref, lse_ref = refs[3 + n:5 + n]
        dsts = refs[5 + n:5 + 2 * n]
        sems = refs[5 + 2 * n:]
        j, i = pl.program_id(0), pl.program_id(1)

        @pl.when((j == 0) & (i == 0))
        def _():
            _comm_start(comm_modes, srcs, dsts, sems)

        out = None
        for hh in range(2):
            sl = slice(hh * LANES, (hh + 1) * LANES)
            qv = q_ref[:, sl]
            m = l = acc = None
            for kc in range(T // tkc):
                ks = slice(kc * tkc, (kc + 1) * tkc)
                s = _dot_nt(qv, k_ref[ks, sl])
                mc = jnp.max(s, axis=-1, keepdims=True)
                if m is None:
                    m = mc
                    p = jnp.exp(s - m)
                    l = jnp.sum(p, axis=-1, keepdims=True)
                    acc = _dot(p.astype(BF16), v_ref[ks, sl])
                else:
                    m_new = jnp.maximum(m, mc)
                    alpha = jnp.exp(m - m_new)
                    p = jnp.exp(s - m_new)
                    l = alpha * l + jnp.sum(p, axis=-1, keepdims=True)
                    acc = alpha * acc + _dot(p.astype(BF16), v_ref[ks, sl])
                    m = m_new
            o = acc / l
            out = o if out is None else out + o
            lse_ref[hh] = m + jnp.log(l)
        o_ref[...] = out

        @pl.when((j == nj - 1) & (i == ni - 1))
        def _():
            _comm_wait(comm_modes, srcs, dsts, sems)

    any_spec = pl.BlockSpec(memory_space=pl.ANY)
    got = _pallas(
        body, name=name, grid=(nj, ni),
        in_specs=[pl.BlockSpec((tq, 2 * LANES), lambda j, i: (i, j)), pl.BlockSpec((T, 2 * LANES), lambda j, i: (0, j)),
                  pl.BlockSpec((T, 2 * LANES), lambda j, i: (0, j))] + [any_spec] * n,
        out_specs=[pl.BlockSpec((tq, LANES), lambda j, i: (i, j)), pl.BlockSpec((2, tq, 1), lambda j, i: (j, i, 0))]
        + [any_spec] * n,
        out_shape=[jax.ShapeDtypeStruct((T, N_HEADS * V_HEAD), F32), jax.ShapeDtypeStruct((N_HEADS, T, 1), F32)]
        + _comm_out_shapes(comm_modes, comm_arrays),
        scratch_shapes=_comm_scratch(n),
        compiler_params=_params(("arbitrary", "arbitrary")),
    )(q, k, v, *comm_arrays)
    return got[0], got[1], got[2:]


def _attn_bwd(q, k, v, o, lse, do, comm_modes, comm_arrays, *, name, tk=256, tqc=4096):
    T = q.shape[0]
    tk = min(tk, T)
    tqc = min(tqc, T)
    n = len(comm_arrays)
    nj, nkb = N_HEADS // 2, T // tk

    def body(*refs):
        q_ref, k_ref, v_ref, o_ref, lse_ref, do_ref = refs[:6]
        srcs = refs[6:6 + n]
        dq_ref, dk_ref, dv_ref = refs[6 + n:9 + n]
        dsts = refs[9 + n:9 + 2 * n]
        d_s = refs[9 + 2 * n]
        sems = refs[10 + 2 * n:]
        j, kb = pl.program_id(0), pl.program_id(1)

        @pl.when((j == 0) & (kb == 0))
        def _():
            _comm_start(comm_modes, srcs, dsts, sems)

        lane = lax.broadcasted_iota(jnp.int32, (1, LANES), 1)
        @pl.when(kb == 0)
        def _():
            prod = do_ref[...] * o_ref[...]
            for hh in range(2):
                keep = (lane < V_HEAD) if hh == 0 else (lane >= V_HEAD)
                d_s[hh] = jnp.sum(jnp.where(keep, prod, 0.0), axis=-1, keepdims=True)

        for hh in range(2):
            sl = slice(hh * LANES, (hh + 1) * LANES)
            keep = (lane < V_HEAD) if hh == 0 else (lane >= V_HEAD)
            kv, vv = k_ref[:, sl], v_ref[:, sl]
            dv_acc = dk_acc = None
            for qc in range(T // tqc):
                qs = slice(qc * tqc, (qc + 1) * tqc)
                qv = q_ref[qs, sl]
                do_b = do_ref[qs, :].astype(BF16)
                s = _dot_nt(qv, kv)
                p = jnp.exp(s - lse_ref[hh, qs])
                dp = _dot_nt(do_b, vv)
                ds = (p * (dp - d_s[hh, qs])).astype(BF16)
                dvc = _dot_tn(p.astype(BF16), do_b)
                dkc = _dot_tn(ds, qv)
                dv_acc = dvc if dv_acc is None else dv_acc + dvc
                dk_acc = dkc if dk_acc is None else dk_acc + dkc
                dqp = _dot(ds, kv)

                @pl.when(kb == 0)
                def _(dqp=dqp, sl=sl, qs=qs):
                    dq_ref[qs, sl] = dqp

                @pl.when(kb > 0)
                def _(dqp=dqp, sl=sl, qs=qs):
                    dq_ref[qs, sl] += dqp

            dv_ref[:, sl] = jnp.where(keep, dv_acc, 0.0).astype(BF16)
            dk_ref[:, sl] = dk_acc

        @pl.when((j == nj - 1) & (kb == nkb - 1))
        def _():
            _comm_wait(comm_modes, srcs, dsts, sems)

    any_spec = pl.BlockSpec(memory_space=pl.ANY)
    pair = pl.BlockSpec((T, 2 * LANES), lambda j, kb: (0, j))
    kblk = pl.BlockSpec((tk, 2 * LANES), lambda j, kb: (kb, j))
    got = _pallas(
        body, name=name, grid=(nj, nkb),
        in_specs=[pair, kblk, kblk, pl.BlockSpec((T, LANES), lambda j, kb: (0, j)),
                  pl.BlockSpec((2, T, 1), lambda j, kb: (j, 0, 0)), pl.BlockSpec((T, LANES), lambda j, kb: (0, j))]
        + [any_spec] * n,
        out_specs=[pair, kblk, kblk] + [any_spec] * n,
        out_shape=[jax.ShapeDtypeStruct((T, 2048), F32)] * 2 + [jax.ShapeDtypeStruct((T, 2048), BF16)]
        + _comm_out_shapes(comm_modes, comm_arrays),
        scratch_shapes=[pltpu.VMEM((2, T, 1), F32)] + _comm_scratch(n),
        compiler_params=_params(("arbitrary", "arbitrary")),
    )(q, k, v, o, lse, do, *comm_arrays)
    return got[0], got[1], got[2], got[3:]


CONV_ROWS, CONV_HALO = 64, 8
CONV_WIN = CONV_ROWS + 2 * CONV_HALO


def _conv_shift(x, sh, t_idx, total):
    if sh == 0:
        return x
    y = pltpu.roll(x, (-sh) % x.shape[0], 0)
    if t_idx is None:
        return y
    ok = (t_idx + sh >= 0) & (t_idx + sh < total)
    return jnp.where(ok, y, 0.0)


def _conv_positions(ws, shape):
    return ws + lax.broadcasted_iota(jnp.int32, shape, 0) if isinstance(ws, int) else None


def _aligned(v, m):
    return v if isinstance(v, int) else pl.multiple_of(v, m)


def _conv_chunks(T, chunk, carry):
    n = T // CONV_ROWS
    carry = chunk(0, 0, carry)

    def mid(ci, c):
        return chunk(pl.multiple_of(ci * CONV_ROWS - CONV_HALO, CONV_HALO), CONV_HALO, c)

    carry = lax.fori_loop(1, n - 1, mid, carry)
    return chunk(T - CONV_WIN, 2 * CONV_HALO, carry)


def _conv_pre(x, w_ref, b_ref, t_idx, total):
    pre = b_ref[...] + w_ref[2:3, :] * x
    for j in (0, 1, 3, 4):
        pre = pre + w_ref[j:j + 1, :] * _conv_shift(x, j - 2, t_idx, total)
    return pre


def _conv_fwd(u_big, conv_w, conv_b, *, name, w=256):
    T = u_big.shape[0]
    first = D_INNER // w

    def body(x_ref, w_ref, b_ref, o_ref):
        def chunk(ws, off, carry):
            x = x_ref[pl.ds(ws, CONV_WIN), :]
            pre = _conv_pre(x, w_ref, b_ref, _conv_positions(ws, x.shape), T)
            act = pre * _sigmoid(pre)
            o_ref[pl.ds(_aligned(ws + off, CONV_ROWS), CONV_ROWS), :] = act[off:off + CONV_ROWS]
            return carry

        _conv_chunks(T, chunk, 0)

    return _pallas(
        body, name=name, grid=(XBC_DIM // w,),
        in_specs=[pl.BlockSpec((T, w), lambda j: (0, j + first)), pl.BlockSpec((CONV_WIDTH, w), lambda j: (0, j)),
                  pl.BlockSpec((1, w), lambda j: (0, j))],
        out_specs=pl.BlockSpec((T, w), lambda j: (0, j)),
        out_shape=jax.ShapeDtypeStruct((T, XBC_DIM), F32),
        compiler_params=_params(("parallel",)),
    )(u_big, conv_w, conv_b)


def _conv_bwd(dact_f, dact_b, u_big, conv_w, conv_b, *, name, w=128):
    T = u_big.shape[0]
    first = D_INNER // w

    def body(df_ref, db_ref, x_ref, w_ref, b_ref, dx_ref, dw_ref, dbias_ref):
        def chunk(ws, off, sums):
            rows = pl.ds(ws, CONV_WIN)
            x = x_ref[rows, :]
            row = lax.broadcasted_iota(jnp.int32, x.shape, 0)
            t_idx = _conv_positions(ws, x.shape)
            pre = _conv_pre(x, w_ref, b_ref, t_idx, T)
            sg = _sigmoid(pre)
            dpre = (df_ref[rows, :] + db_ref[rows, :]) * (sg * (1.0 + pre * (1.0 - sg)))
            dx = w_ref[2:3, :] * dpre
            for j in (0, 1, 3, 4):
                dx = dx + w_ref[j:j + 1, :] * _conv_shift(dpre, 2 - j, t_idx, T)
            dx_ref[pl.ds(_aligned(ws + off, CONV_ROWS), CONV_ROWS), :] = dx[off:off + CONV_ROWS].astype(dx_ref.dtype)
            own = jnp.where((row >= off) & (row < off + CONV_ROWS), dpre, 0.0)
            new = [sums[5] + jnp.sum(own, axis=0, keepdims=True)]
            for j in range(CONV_WIDTH):
                new.insert(j, sums[j] + jnp.sum(own * _conv_shift(x, j - 2, t_idx, T), axis=0, keepdims=True))
            return tuple(new)

        zero = jnp.zeros((1, w), F32)
        sums = _conv_chunks(T, chunk, (zero,) * (CONV_WIDTH + 1))
        for j in range(CONV_WIDTH):
            dw_ref[j:j + 1, :] = sums[j]
        dbias_ref[...] = sums[CONV_WIDTH]

    blk = pl.BlockSpec((T, w), lambda j: (0, j))
    return _pallas(
        body, name=name, grid=(XBC_DIM // w,),
        in_specs=[blk, blk, pl.BlockSpec((T, w), lambda j: (0, j + first)),
                  pl.BlockSpec((CONV_WIDTH, w), lambda j: (0, j)), pl.BlockSpec((1, w), lambda j: (0, j))],
        out_specs=[blk, pl.BlockSpec((CONV_WIDTH, w), lambda j: (0, j)), pl.BlockSpec((1, w), lambda j: (0, j))],
        out_shape=[jax.ShapeDtypeStruct((T, XBC_DIM), BF16), jax.ShapeDtypeStruct((CONV_WIDTH, XBC_DIM), F32),
                   jax.ShapeDtypeStruct((1, XBC_DIM), F32)],
        compiler_params=_params(("parallel",)),
    )(dact_f, dact_b, u_big, conv_w, conv_b)


def _ssd_expand(rev):
    off = SSM_HEADS if rev else 0
    h = jnp.arange(LANES, dtype=jnp.int32)[:, None]
    return (jnp.arange(D_INNER, dtype=jnp.int32)[None, :] // 64 + off == h).astype(F32)


def _ssd_head_terms(dt_ref, bias_ref, alog_ref, acst_s, dtt_s, rev):
    L = CHUNK
    row = lax.broadcasted_iota(jnp.int32, (L, L), 0)
    col = lax.broadcasted_iota(jnp.int32, (L, L), 1)
    mask = (row <= col) if rev else (row >= col)
    cm = mask.astype(F32)
    cmt = ((row >= col) if rev else (row <= col)).astype(F32)
    pre = dt_ref[...] + bias_ref[...]
    dt = _softplus(pre)
    a = -jnp.exp(alog_ref[...])
    da = dt * a
    acs = _dot_h(cm, da)
    acst_s[...] = _dot_h_tn(da, cmt)
    dtt_s[...] = _dot_h_tn(dt, (row == col).astype(F32))
    tot = jnp.sum(da, axis=0, keepdims=True)
    w = jnp.exp(tot - acs)
    return dict(mask=mask, cm=cm, cmt=cmt, ident=(row == col).astype(F32), pre=pre, dt=dt, a=a, da=da, acs=acs,
                tot=tot, e=jnp.exp(acs), w=w, wdt=w * dt, dec=jnp.exp(tot))


def _pair(lo, v, h0):
    return jnp.where(lo, v[:, h0:h0 + 1], v[:, h0 + 1:h0 + 2])


def _ssd_fwd(xbc_act, u_small, bias128, alog128, *, rev, name):
    T = xbc_act.shape[0]
    L = CHUNK
    nc = T // L
    off = SSM_HEADS if rev else 0

    def cidx(c):
        return (nc - 1 - c) if rev else c

    def body(xs_ref, bm_ref, cm_ref, dt_ref, bias_ref, alog_ref, y_ref, hin_ref, ht_s, acst_s, dtt_s, wx_s, dec_s):
        c = pl.program_id(0)

        @pl.when(c == 0)
        def _():
            ht_s[...] = jnp.zeros_like(ht_s)

        t = _ssd_head_terms(dt_ref, bias_ref, alog_ref, acst_s, dtt_s, rev)
        lo = lax.broadcasted_iota(jnp.int32, (L, LANES), 1) < 64
        lo1 = lax.broadcasted_iota(jnp.int32, (1, LANES), 1) < 64
        for g in range(SSM_GROUPS):
            bmat = bm_ref[:, g * LANES:(g + 1) * LANES].astype(BF16)
            cmat = cm_ref[:, g * LANES:(g + 1) * LANES].astype(BF16)
            gmat = _dot_nt(cmat, bmat)
            ht = ht_s[g]
            ch = _dot(cmat, ht.astype(BF16))
            for pr in range(4):
                ps = slice(pr * LANES, (pr + 1) * LANES)
                cs = slice(g * 512 + pr * LANES, g * 512 + (pr + 1) * LANES)
                h0 = off + 8 * g + 2 * pr
                xp = xs_ref[:, cs]
                acc = _pair(lo, t["e"], h0) * ch[:, ps]
                for s_ in range(2):
                    h = h0 + s_
                    seg = t["acs"][:, h:h + 1] - acst_s[h:h + 1, :]
                    lam = jnp.exp(jnp.where(t["mask"], seg, -1e30))
                    m = (gmat * lam * dtt_s[h:h + 1, :]).astype(BF16)
                    xm = jnp.where(lo if s_ == 0 else jnp.logical_not(lo), xp, 0.0).astype(BF16)
                    acc = acc + _dot(m, xm)
                y_ref[:, cs] = acc
                wx_s[:, ps] = (_pair(lo, t["wdt"], h0) * xp).astype(BF16)
                dec_s[0:1, ps] = _pair(lo1, t["dec"], h0)
            hin_ref[0, g] = ht.astype(BF16)
            ht_s[g] = ht * dec_s[0:1, :] + _dot_tn(bmat, wx_s[...])

    return _pallas(
        body, name=name, grid=(nc,),
        in_specs=[pl.BlockSpec((L, D_INNER), lambda c: (cidx(c), 0)), pl.BlockSpec((L, 512), lambda c: (cidx(c), 4)),
                  pl.BlockSpec((L, 512), lambda c: (cidx(c), 5)),
                  pl.BlockSpec((L, LANES), lambda c: (cidx(c), U_DT // LANES)),
                  pl.BlockSpec((1, LANES), lambda c: (0, 0)), pl.BlockSpec((1, LANES), lambda c: (0, 0))],
        out_specs=[pl.BlockSpec((L, D_INNER), lambda c: (cidx(c), 0)),
                   pl.BlockSpec((1, SSM_GROUPS, D_STATE, 512), lambda c: (cidx(c), 0, 0, 0))],
        out_shape=[jax.ShapeDtypeStruct((T, D_INNER), F32), jax.ShapeDtypeStruct((nc, SSM_GROUPS, D_STATE, 512), BF16)],
        scratch_shapes=[pltpu.VMEM((SSM_GROUPS, D_STATE, 512), F32), pltpu.VMEM((LANES, L), F32),
                        pltpu.VMEM((LANES, L), F32), pltpu.VMEM((L, 512), BF16), pltpu.VMEM((8, 512), F32)],
        compiler_params=_params(("arbitrary",)),
    )(xbc_act, xbc_act, xbc_act, u_small, bias128, alog128)


def _ssd_bwd(dy, xbc_act, u_small, bias128, alog128, hin, skip_x, *, rev, name):
    T = xbc_act.shape[0]
    L = CHUNK
    nc = T // L
    off = SSM_HEADS if rev else 0
    has_skip = skip_x is not None

    def cidx(c):
        return c if rev else (nc - 1 - c)

    def body(*refs):
        (dy_ref, xs_ref, bm_ref, cm_ref, dt_ref, bias_ref, alog_ref, hin_ref) = refs[:8]
        k = 8
        skip_ref = refs[k] if has_skip else None
        k += 1 if has_skip else 0
        (dx_ref, draw_ref, dalog_ref, dbias_ref, dht_s, acst_s, dtt_s, rowt_s, ddtt_s, wx_s, edy_s, dec_s) = refs[k:]
        c = pl.program_id(0)

        @pl.when(c == 0)
        def _():
            dht_s[...] = jnp.zeros_like(dht_s)
            rowt_s[...] = jnp.zeros_like(rowt_s)
            ddtt_s[...] = jnp.zeros_like(ddtt_s)

        t = _ssd_head_terms(dt_ref, bias_ref, alog_ref, acst_s, dtt_s, rev)
        lane1 = lax.broadcasted_iota(jnp.int32, (1, LANES), 1)
        lo = lax.broadcasted_iota(jnp.int32, (L, LANES), 1) < 64
        lo1 = lane1 < 64
        colpart = jnp.zeros((L, LANES), F32)
        u_cols = jnp.zeros((L, LANES), F32)
        v_cols = jnp.zeros((L, LANES), F32)
        dtot_h = jnp.zeros((1, LANES), F32)
        for g in range(SSM_GROUPS):
            bmat = bm_ref[:, g * LANES:(g + 1) * LANES].astype(BF16)
            cmat = cm_ref[:, g * LANES:(g + 1) * LANES].astype(BF16)
            gmat = _dot_nt(cmat, bmat)
            ht_in = hin_ref[0, g]
            dht = dht_s[g]
            ht_in_b, dht_b = ht_in.astype(BF16), dht.astype(BF16)
            ch = _dot(cmat, ht_in_b)
            bdh = _dot(bmat, dht_b)
            th = jnp.sum(dht * ht_in, axis=0, keepdims=True)
            dgm = jnp.zeros((L, L), F32)
            for pr in range(4):
                ps = slice(pr * LANES, (pr + 1) * LANES)
                cs = slice(g * 512 + pr * LANES, g * 512 + (pr + 1) * LANES)
                h0 = off + 8 * g + 2 * pr
                xp = xs_ref[:, cs]
                dyp = dy_ref[:, cs]
                dyp_b = dyp.astype(BF16)
                wdt_p = _pair(lo, t["wdt"], h0)
                e_p = _pair(lo, t["e"], h0)
                xb = xp * bdh[:, ps]
                dc = dyp * ch[:, ps]
                dxp = wdt_p * bdh[:, ps]
                for s_ in range(2):
                    h = h0 + s_
                    keep = lo if s_ == 0 else jnp.logical_not(lo)
                    keep1 = lo1 if s_ == 0 else jnp.logical_not(lo1)
                    onehot = (lane1 == h).astype(F32)
                    dtrow = dtt_s[h:h + 1, :]
                    seg = t["acs"][:, h:h + 1] - acst_s[h:h + 1, :]
                    lam = jnp.exp(jnp.where(t["mask"], seg, -1e30))
                    mf0 = gmat * lam
                    m = (mf0 * dtrow).astype(BF16)
                    xm = jnp.where(keep, xp, 0.0).astype(BF16)
                    dm = _dot_nt(dyp_b, xm)
                    r = dm * mf0
                    q = r * dtrow
                    dgm = dgm + dm * lam * dtrow
                    colpart = colpart + jnp.sum(q, axis=1, keepdims=True) * onehot
                    rowt_s[h:h + 1, :] = jnp.sum(q, axis=0, keepdims=True)
                    ddtt_s[h:h + 1, :] = jnp.sum(r, axis=0, keepdims=True)
                    u_cols = u_cols + jnp.sum(jnp.where(keep, xb, 0.0), axis=1, keepdims=True) * onehot
                    v_cols = v_cols + jnp.sum(jnp.where(keep, dc, 0.0), axis=1, keepdims=True) * onehot
                    dtot_h = dtot_h + jnp.sum(jnp.where(keep1, th[:, ps], 0.0), axis=1, keepdims=True) * onehot
                    dxp = dxp + jnp.where(keep, _dot_tn(m, dyp_b), 0.0)
                if has_skip:
                    dxp = dxp + dyp * skip_ref[:, cs]
                dx_ref[:, cs] = dxp
                wx_s[:, ps] = (wdt_p * xp).astype(BF16)
                edy_s[:, ps] = (e_p * dyp).astype(BF16)
                dec_s[0:1, ps] = _pair(lo1, t["dec"], h0)
            edy_b = edy_s[...]
            dgm_b = dgm.astype(BF16)
            dx_ref[:, D_INNER + g * LANES:D_INNER + (g + 1) * LANES] = (
                _dot_nt(wx_s[...], dht_b) + _dot_tn(dgm_b, cmat))
            dx_ref[:, D_INNER + 512 + g * LANES:D_INNER + 512 + (g + 1) * LANES] = (
                _dot_nt(edy_b, ht_in_b) + _dot(dgm_b, bmat))
            dht_s[g] = dec_s[0:1, :] * dht + _dot_tn(cmat, edy_b)

        t_e = v_cols * t["e"]
        t_w = u_cols * t["wdt"]
        colsum_part = _dot_h_tn(rowt_s[...], t["ident"])
        dtot = jnp.sum(t_w, axis=0, keepdims=True) + t["dec"] * dtot_h
        row1 = lax.broadcasted_iota(jnp.int32, (L, LANES), 0)
        last = row1 == (0 if rev else L - 1)
        dacs = colpart - colsum_part + t_e - t_w + jnp.where(last, dtot, 0.0)
        dda = _dot_h(t["cmt"], dacs)
        ddt = dda * t["a"] + u_cols * t["w"] + _dot_h_tn(ddtt_s[...], t["ident"])
        dalog = jnp.sum(dda * t["dt"], axis=0, keepdims=True) * t["a"]
        draw = ddt * _sigmoid(t["pre"])
        draw_ref[...] = draw
        dbias = jnp.sum(draw, axis=0, keepdims=True)

        @pl.when(c == 0)
        def _():
            dalog_ref[...] = dalog
            dbias_ref[...] = dbias

        @pl.when(c > 0)
        def _():
            dalog_ref[...] += dalog
            dbias_ref[...] += dbias

    one = pl.BlockSpec((1, LANES), lambda c: (0, 0))
    in_specs = [pl.BlockSpec((L, D_INNER), lambda c: (cidx(c), 0)), pl.BlockSpec((L, D_INNER), lambda c: (cidx(c), 0)),
                pl.BlockSpec((L, 512), lambda c: (cidx(c), 4)), pl.BlockSpec((L, 512), lambda c: (cidx(c), 5)),
                pl.BlockSpec((L, LANES), lambda c: (cidx(c), U_DT // LANES)), one, one,
                pl.BlockSpec((1, SSM_GROUPS, D_STATE, 512), lambda c: (cidx(c), 0, 0, 0))]
    ins = [dy, xbc_act, xbc_act, xbc_act, u_small, bias128, alog128, hin]
    if has_skip:
        in_specs.append(pl.BlockSpec((1, D_INNER), lambda c: (0, 0)))
        ins.append(skip_x)
    return _pallas(
        body, name=name, grid=(nc,), in_specs=in_specs,
        out_specs=[pl.BlockSpec((L, XBC_DIM), lambda c: (cidx(c), 0)), pl.BlockSpec((L, LANES), lambda c: (cidx(c), 0)),
                   one, one],
        out_shape=[jax.ShapeDtypeStruct((T, XBC_DIM), F32), jax.ShapeDtypeStruct((T, LANES), F32),
                   jax.ShapeDtypeStruct((1, LANES), F32), jax.ShapeDtypeStruct((1, LANES), F32)],
        scratch_shapes=[pltpu.VMEM((SSM_GROUPS, D_STATE, 512), F32), pltpu.VMEM((LANES, L), F32),
                        pltpu.VMEM((LANES, L), F32), pltpu.VMEM((LANES, L), F32), pltpu.VMEM((LANES, L), F32),
                        pltpu.VMEM((L, 512), BF16), pltpu.VMEM((L, 512), BF16), pltpu.VMEM((8, 512), F32)],
        compiler_params=_params(("arbitrary",)),
    )(*ins)


def _ssm_out_fwd(y_f, y_b, xbc_act, u_big, skip_x, ssm_norm, *, name, tile=512):
    rows = y_f.shape[0]

    def fn(yf, yb, xs, z, sk, nw):
        yz = (yf + yb + sk * xs) * (z * _sigmoid(z))
        r = lax.rsqrt(jnp.mean(yz * yz, axis=-1, keepdims=True) + EPS)
        return yz * r * nw

    return _rowmap(fn, name=name, rows=rows, tile=tile, ncol=SSM_GROUPS,
                   ins=[(y_f, 512, _colj), (y_b, 512, _colj), (xbc_act, 512, _colj), (u_big, 512, _colj)],
                   consts=[(skip_x, 512, _colj), (ssm_norm, 512, _colj)], outs=[(D_INNER, BF16, 512, _colj)])[0]


def _ssm_out_bwd(dm, y_f, y_b, xbc_act, u_big, skip_x, ssm_norm, *, name, tile=512):
    rows = y_f.shape[0]

    def fn(dmv, yf, yb, xs, z, sk, nw):
        sg = _sigmoid(z)
        y = yf + yb + sk * xs
        yz = y * (z * sg)
        r = lax.rsqrt(jnp.mean(yz * yz, axis=-1, keepdims=True) + EPS)
        xh = yz * r
        dxh = dmv * nw
        dyz = r * (dxh - xh * jnp.mean(dxh * xh, axis=-1, keepdims=True))
        dy = dyz * (z * sg)
        dz = dyz * y * (sg * (1.0 + z * (1.0 - sg)))
        return dy, dz, jnp.sum(dmv * xh, axis=0, keepdims=True), jnp.sum(dy * xs, axis=0, keepdims=True)

    return _rowmap(fn, name=name, rows=rows, tile=tile, ncol=SSM_GROUPS,
                   ins=[(dm, 512, _colj), (y_f, 512, _colj), (y_b, 512, _colj), (xbc_act, 512, _colj),
                        (u_big, 512, _colj)],
                   consts=[(skip_x, 512, _colj), (ssm_norm, 512, _colj)],
                   outs=[(D_INNER, F32, 512, _colj), (D_INNER, BF16, 512, _colj)],
                   accs=[(D_INNER, 512, _colj), (D_INNER, 512, _colj)])


def _merge_fwd(pa, pb, u_big, *, name, tile=512):
    rows = pa.shape[0]

    def fn(a, b, ga, gb):
        return _sigmoid(ga) * a + _sigmoid(gb) * b

    return _rowmap(fn, name=name, rows=rows, tile=tile,
                   ins=[(pa, 1024, _col0), (pb, 1024, _col0), (u_big, 1024, lambda j: 5), (u_big, 1024, lambda j: 6)],
                   outs=[(1024, BF16, 1024, _col0)])[0]


def _merge_bwd(dmg, pa, pb, u_big, *, name, tile=512):
    rows = pa.shape[0]

    def fn(d, a, b, ga, gb):
        sa, sb = _sigmoid(ga), _sigmoid(gb)
        return d * sa, d * sb, d * a * sa * (1.0 - sa), d * b * sb * (1.0 - sb)

    return _rowmap(fn, name=name, rows=rows, tile=tile,
                   ins=[(dmg, 1024, _col0), (pa, 1024, _col0), (pb, 1024, _col0), (u_big, 1024, lambda j: 5),
                        (u_big, 1024, lambda j: 6)],
                   outs=[(1024, BF16, 1024, _col0)] * 4)


def _loss_bwd(y, target, *, name, tile=512):
    rows, d = y.shape

    def fn(yv, tv):
        err = yv - tv
        part = jnp.sum(jnp.sum(err * err, axis=-1, keepdims=True), axis=0, keepdims=True)
        return err * (1.0 / d), jnp.broadcast_to(part * (0.5 / d), (1, LANES))

    dy, part = _rowmap(fn, name=name, rows=rows, tile=tile, ins=[(y, d, _col0), (target, d, _col0)],
                       outs=[(d, F32, d, _col0)], accs=[(LANES, LANES, _col0)])
    return dy, part[0, 0]


def _small_slab(gs, dskip_ch, dalog_f, dalog_b, dbias_f, dbias_b, gkv, gqh, gkh, dconv_w, *, name):
    e_mat = _ssd_expand(False)
    full_names = ("ffn1_norm", "mix_norm", "q_a_norm", "conv_b", "ssm_norm", "ffn2_norm")
    full = [gs[n] for n in full_names]
    nf = len(full)

    def body(*refs):
        fulls = refs[:nf]
        (dsk_ref, e_ref, af_ref, ab_ref, bf_ref, bb_ref, gkv_ref, gqh_ref, gkh_ref, cw_ref, o_ref) = refs[nf:]
        o_ref[...] = jnp.zeros_like(o_ref)
        for n, r in zip(full_names, fulls):
            o_ref[SMALL_ROW[n]:SMALL_ROW[n] + 1, 0:r.shape[1]] = r[...]
        o_ref[SMALL_ROW["kv_a_norm"]:SMALL_ROW["kv_a_norm"] + 1, 0:KV_LORA] = gkv_ref[...]
        o_ref[SMALL_ROW["q_head_norm"]:SMALL_ROW["q_head_norm"] + 1, 0:LANES] = gqh_ref[...]
        o_ref[SMALL_ROW["k_head_norm"]:SMALL_ROW["k_head_norm"] + 1, 0:LANES] = gkh_ref[...]
        o_ref[SMALL_ROW["a_log_fwd"]:SMALL_ROW["a_log_fwd"] + 1, 0:LANES] = af_ref[...]
        o_ref[SMALL_ROW["a_log_bwd"]:SMALL_ROW["a_log_bwd"] + 1, 0:LANES] = pltpu.roll(ab_ref[...], 96, 1)
        o_ref[SMALL_ROW["dt_bias_fwd"]:SMALL_ROW["dt_bias_fwd"] + 1, 0:LANES] = bf_ref[...]
        o_ref[SMALL_ROW["dt_bias_bwd"]:SMALL_ROW["dt_bias_bwd"] + 1, 0:LANES] = pltpu.roll(bb_ref[...], 96, 1)
        dsk = _dot_h_nt(jnp.broadcast_to(dsk_ref[...], (8, D_INNER)), e_ref[...])
        o_ref[SMALL_ROW["d_skip"]:SMALL_ROW["d_skip"] + 1, 0:LANES] = dsk[0:1, :]
        o_ref[CONV_ROW:CONV_ROW + CONV_WIDTH, :] = cw_ref[...]

    return _pallas(body, name=name, out_shape=jax.ShapeDtypeStruct((SMALL_ROWS, SMALL_COLS), F32))(
        *full, dskip_ch, e_mat, dalog_f, dalog_b, dbias_f, dbias_b, gkv, gqh, gkh, dconv_w)


def _adamw_math(g, w, m, v):
    m2 = ADAM_B1 * m + (1.0 - ADAM_B1) * g
    v2 = ADAM_B2 * v + (1.0 - ADAM_B2) * (g * g)
    m_hat = m2 / (1.0 - ADAM_B1 ** ADAM_STEP)
    v_hat = v2 / (1.0 - ADAM_B2 ** ADAM_STEP)
    delta = -ADAM_LR * (m_hat / (jnp.sqrt(v_hat) + ADAM_EPS) + ADAM_WD * w)
    return delta, m2, v2


def _sum8(r_ref):
    g = r_ref[0].astype(F32)
    for s in range(1, N_DEV):
        g = g + r_ref[s].astype(F32)
    return g


def _reduce_adamw(recv, w, m, v, *, name, tile=256):
    _, R, C = recv.shape
    tile = _pick(R, tile) if R % LANES == 0 else R
    assert R % tile == 0

    def body(r_ref, w_ref, m_ref, v_ref, g_ref, d_ref, m2_ref, v2_ref):
        g = _sum8(r_ref)
        delta, m2, v2 = _adamw_math(g, w_ref[...], m_ref[...], v_ref[...])
        g_ref[...] = g
        d_ref[...] = delta
        m2_ref[...] = m2
        v2_ref[...] = v2

    blk = pl.BlockSpec((tile, C), lambda i: (i, 0))
    return _pallas(
        body, name=name, grid=(R // tile,),
        in_specs=[pl.BlockSpec((N_DEV, tile, C), lambda i: (0, i, 0)), blk, blk, blk], out_specs=[blk] * 4,
        out_shape=[jax.ShapeDtypeStruct((R, C), F32)] * 4, compiler_params=_params(("parallel",)),
    )(recv, w, m, v)


def _reduce_t_adamw(recv, w, m, v, *, name):
    R, cs = w.shape

    def body(r_ref, w_ref, m_ref, v_ref, g_ref, d_ref, m2_ref, v2_ref):
        g = _sum8(r_ref).T
        delta, m2, v2 = _adamw_math(g, w_ref[...], m_ref[...], v_ref[...])
        g_ref[...] = g
        d_ref[...] = delta
        m2_ref[...] = m2
        v2_ref[...] = v2

    return _pallas(body, name=name, out_shape=[jax.ShapeDtypeStruct((R, cs), F32)] * 4,
                   compiler_params=pltpu.CompilerParams(vmem_limit_bytes=VMEM_LIMIT))(recv, w, m, v)


def _reduce8(recv, *, name, tile):
    _, R, C = recv.shape

    def body(r_ref, g_ref):
        g_ref[...] = _sum8(r_ref)

    return _pallas(body, name=name, grid=(R // tile,),
                   in_specs=[pl.BlockSpec((N_DEV, tile, C), lambda i: (0, i, 0))],
                   out_specs=pl.BlockSpec((tile, C), lambda i: (i, 0)),
                   out_shape=jax.ShapeDtypeStruct((R, C), F32), compiler_params=_params(("parallel",)))(recv)


def _adamw(g, w, m, v, *, name, tile=256):
    R, C = w.shape

    def body(g_ref, w_ref, m_ref, v_ref, d_ref, m2_ref, v2_ref):
        delta, m2, v2 = _adamw_math(g_ref[...], w_ref[...], m_ref[...], v_ref[...])
        d_ref[...] = delta
        m2_ref[...] = m2
        v2_ref[...] = v2

    blk = pl.BlockSpec((R, tile), lambda i: (0, i))
    return _pallas(body, name=name, grid=(C // tile,), in_specs=[blk] * 4, out_specs=[blk] * 3,
                   out_shape=[jax.ShapeDtypeStruct((R, C), F32)] * 3, compiler_params=_params(("parallel",)))(g, w, m, v)


def _adamw_small(srecv, conv_g, ws, ms, vs, *, name):
    n = len(ws)

    def body(*refs):
        s_ref, c_ref = refs[0], refs[1]
        w_refs, m_refs, v_refs = refs[2:2 + n], refs[2 + n:2 + 2 * n], refs[2 + 2 * n:2 + 3 * n]
        outs = refs[2 + 3 * n:]
        gsum = _sum8(s_ref)
        for i in range(n):
            if i < len(SMALL):
                g = gsum[i:i + 1, 0:SMALL[i][1]]
            else:
                g = _sum8(c_ref)
            delta, m2, v2 = _adamw_math(g, w_refs[i][...], m_refs[i][...], v_refs[i][...])
            outs[i][...] = g
            outs[n + i][...] = delta
            outs[2 * n + i][...] = m2
            outs[3 * n + i][...] = v2

    shapes = [jax.ShapeDtypeStruct(w.shape, F32) for w in ws]
    got = _pallas(body, name=name, out_shape=shapes * 4,
                  compiler_params=pltpu.CompilerParams(vmem_limit_bytes=VMEM_LIMIT))(srecv, conv_g, *ws, *ms, *vs)
    return got[:n], got[n:2 * n], got[2 * n:3 * n], got[3 * n:]


def _ffn_fwd(x, norm, w_g_t, w_u_t, w_d, tag):
    h = _rms_fwd(x, norm, name=f"{tag}_rms")
    gu = _mm(h, jnp.concatenate([w_g_t, w_u_t], axis=0), name=f"{tag}_gu", tb=True, out_dtype=BF16)
    act = _swiglu_fwd(gu, name=f"{tag}_act")
    out = _mm(act, w_d, name=f"{tag}_down", alpha=0.5, res=x)
    return out, (h, gu, act)


def _ffn_bwd(dout, x, norm, w_g_t, w_u_t, w_d, saved, tag):
    h, gu, act = saved
    d_act = _mm(dout, w_d, name=f"{tag}_dact", tb=True, alpha=0.5, out_dtype=BF16)
    d_wd = _mm(act, dout, name=f"{tag}_dwd", ta=True, alpha=0.5, tm=1408, tn=1024, out_dtype=BF16)
    dg, du = _swiglu_bwd(d_act, gu, name=f"{tag}_dswiglu")
    d_wg_t = _mm(dg, h, name=f"{tag}_dwg", ta=True, tm=1408, tn=1024, out_dtype=BF16)
    d_wu_t = _mm(du, h, name=f"{tag}_dwu", ta=True, tm=1408, tn=1024, out_dtype=BF16)
    dh = _mm(dg, w_g_t, name=f"{tag}_dh_g")
    dh = _mm(du, w_u_t, name=f"{tag}_dh_u", res=dh)
    dx, dnorm = _rms_bwd(dh, x, norm, dout, name=f"{tag}_drms")
    return dx, dnorm, d_wg_t, d_wu_t, d_wd


def _rope_tables(positions, T):
    pos = positions.reshape(T).astype(F32)
    inv_freq = 1.0 / (ROPE_BASE ** (jnp.arange(0, QK_ROPE, 2, dtype=F32) / QK_ROPE))
    ang = pos[:, None] * inv_freq
    cos, sin = jnp.cos(ang), jnp.sin(ang)
    one64, z64 = jnp.ones((T, 64), F32), jnp.zeros((T, 64), F32)
    z16, z32, one32 = jnp.zeros((T, 16), F32), jnp.zeros((T, 32), F32), jnp.ones((T, 32), F32)
    c = jnp.concatenate([one64, cos, cos, one32], axis=1)
    s1 = jnp.concatenate([z64, -sin, z16, z32], axis=1)
    s2 = jnp.concatenate([z64, z16, sin, z32], axis=1)
    return c, s1, s2


def _cols(g):
    n, r, cs = g.shape
    return g.transpose(1, 0, 2).reshape(r, n * cs)


def _rows(g):
    n, rs, c = g.shape
    return g.reshape(n * rs, c)


def _pad_lanes(v, n=LANES):
    return jnp.pad(v, ((0, 0), (0, n - v.shape[1])))


def _in_proj_weights(w_in_t):
    z = lambda n: jnp.zeros((n, D_MODEL), w_in_t.dtype)
    w_small_t = jnp.concatenate([w_in_t[0:384], z(128), w_in_t[384:672], z(96), w_in_t[5792:5856], z(64)], axis=0)
    w_big_t = jnp.concatenate([w_in_t[672:5792], w_in_t[5856:7904]], axis=0)
    return w_small_t, w_big_t


def _mla_up_weights(w_q_b_t, w_kv_b):
    wq = w_q_b_t.reshape(N_HEADS, QK_HEAD, Q_LORA)
    wq = jnp.pad(wq, ((0, 0), (0, LANES - QK_HEAD), (0, 0))).reshape(N_HEADS * LANES, Q_LORA)
    wkv = w_kv_b.reshape(KV_LORA, N_HEADS, QK_NOPE + V_HEAD)
    wk = jnp.pad(wkv[..., :QK_NOPE], ((0, 0), (0, 0), (0, LANES - QK_NOPE))).reshape(KV_LORA, N_HEADS * LANES)
    v = wkv[..., QK_NOPE:]
    zv = jnp.zeros_like(v)
    even = (jnp.arange(N_HEADS) % 2 == 0)[None, :, None]
    wv = jnp.where(even, jnp.concatenate([v, zv], -1), jnp.concatenate([zv, v], -1)).reshape(KV_LORA, N_HEADS * LANES)
    return wq, wk, wv


def _shard_rows(g):
    return g.reshape(N_DEV, g.shape[0] // N_DEV, g.shape[1])


def kernel(x, positions, ffn1_norm, ffn1_w_gate, ffn1_w_up, ffn1_w_down, mix_norm, w_in, q_a_norm, w_q_b, kv_a_norm, w_kv_b, q_head_norm, k_head_norm, conv_w, conv_b, a_log_fwd, a_log_bwd, dt_bias_fwd, dt_bias_bwd, d_skip, ssm_norm, w_attn_branch, w_ssm_branch, w_out, ffn2_norm, ffn2_w_gate, ffn2_w_up, ffn2_w_down, loss_target, m_ffn1_norm, m_ffn1_w_gate, m_ffn1_w_up, m_ffn1_w_down, m_mix_norm, m_w_in, m_q_a_norm, m_w_q_b, m_kv_a_norm, m_w_kv_b, m_q_head_norm, m_k_head_norm, m_conv_w, m_conv_b, m_a_log_fwd, m_a_log_bwd, m_dt_bias_fwd, m_dt_bias_bwd, m_d_skip, m_ssm_norm, m_w_attn_branch, m_w_ssm_branch, m_w_out, m_ffn2_norm, m_ffn2_w_gate, m_ffn2_w_up, m_ffn2_w_down, v_ffn1_norm, v_ffn1_w_gate, v_ffn1_w_up, v_ffn1_w_down, v_mix_norm, v_w_in, v_q_a_norm, v_w_q_b, v_kv_a_norm, v_w_kv_b, v_q_head_norm, v_k_head_norm, v_conv_w, v_conv_b, v_a_log_fwd, v_a_log_bwd, v_dt_bias_fwd, v_dt_bias_bwd, v_d_skip, v_ssm_norm, v_w_attn_branch, v_w_ssm_branch, v_w_out, v_ffn2_norm, v_ffn2_w_gate, v_ffn2_w_up, v_ffn2_w_down):
    w_all = dict(ffn1_norm=ffn1_norm, ffn1_w_gate=ffn1_w_gate, ffn1_w_up=ffn1_w_up, ffn1_w_down=ffn1_w_down, mix_norm=mix_norm, w_in=w_in, q_a_norm=q_a_norm, w_q_b=w_q_b, kv_a_norm=kv_a_norm, w_kv_b=w_kv_b, q_head_norm=q_head_norm, k_head_norm=k_head_norm, conv_w=conv_w, conv_b=conv_b, a_log_fwd=a_log_fwd, a_log_bwd=a_log_bwd, dt_bias_fwd=dt_bias_fwd, dt_bias_bwd=dt_bias_bwd, d_skip=d_skip, ssm_norm=ssm_norm, w_attn_branch=w_attn_branch, w_ssm_branch=w_ssm_branch, w_out=w_out, ffn2_norm=ffn2_norm, ffn2_w_gate=ffn2_w_gate, ffn2_w_up=ffn2_w_up, ffn2_w_down=ffn2_w_down)
    m_all = dict(ffn1_norm=m_ffn1_norm, ffn1_w_gate=m_ffn1_w_gate, ffn1_w_up=m_ffn1_w_up, ffn1_w_down=m_ffn1_w_down, mix_norm=m_mix_norm, w_in=m_w_in, q_a_norm=m_q_a_norm, w_q_b=m_w_q_b, kv_a_norm=m_kv_a_norm, w_kv_b=m_w_kv_b, q_head_norm=m_q_head_norm, k_head_norm=m_k_head_norm, conv_w=m_conv_w, conv_b=m_conv_b, a_log_fwd=m_a_log_fwd, a_log_bwd=m_a_log_bwd, dt_bias_fwd=m_dt_bias_fwd, dt_bias_bwd=m_dt_bias_bwd, d_skip=m_d_skip, ssm_norm=m_ssm_norm, w_attn_branch=m_w_attn_branch, w_ssm_branch=m_w_ssm_branch, w_out=m_w_out, ffn2_norm=m_ffn2_norm, ffn2_w_gate=m_ffn2_w_gate, ffn2_w_up=m_ffn2_w_up, ffn2_w_down=m_ffn2_w_down)
    v_all = dict(ffn1_norm=v_ffn1_norm, ffn1_w_gate=v_ffn1_w_gate, ffn1_w_up=v_ffn1_w_up, ffn1_w_down=v_ffn1_w_down, mix_norm=v_mix_norm, w_in=v_w_in, q_a_norm=v_q_a_norm, w_q_b=v_w_q_b, kv_a_norm=v_kv_a_norm, w_kv_b=v_w_kv_b, q_head_norm=v_q_head_norm, k_head_norm=v_k_head_norm, conv_w=v_conv_w, conv_b=v_conv_b, a_log_fwd=v_a_log_fwd, a_log_bwd=v_a_log_bwd, dt_bias_fwd=v_dt_bias_fwd, dt_bias_bwd=v_dt_bias_bwd, d_skip=v_d_skip, ssm_norm=v_ssm_norm, w_attn_branch=v_w_attn_branch, w_ssm_branch=v_w_ssm_branch, w_out=v_w_out, ffn2_norm=v_ffn2_norm, ffn2_w_gate=v_ffn2_w_gate, ffn2_w_up=v_ffn2_w_up, ffn2_w_down=v_ffn2_w_down)
    T = x.shape[1]
    xs_in, target = x[0], loss_target[0]
    def two_d(n, a):
        if n in TRANSPOSED:
            return jnp.swapaxes(a, 1, 2).reshape(a.shape[2], a.shape[1])
        return a.reshape(-1, a.shape[-1])

    w2 = {n: two_d(n, a) for n, a in w_all.items()}
    m2 = {n: two_d(n, a) for n, a in m_all.items()}
    v2 = {n: two_d(n, a) for n, a in v_all.items()}
    p = {n: w2[n] for n, _ in SMALL}
    bf = lambda n: w2[n].astype(BF16)

    early = ["ffn1_w_gate", "ffn1_w_up", "ffn1_w_down", "w_in", "w_q_b", "w_kv_b"]
    g_early = _all_gather_two_level([bf(n) for n in early] + [w2["conv_w"]], name="gather_early")
    ge = dict(zip(early + ["conv_w"], g_early))
    w_g1t, w_u1t = _rows(ge["ffn1_w_gate"]), _rows(ge["ffn1_w_up"])
    w_d1 = _rows(ge["ffn1_w_down"])
    w_small_t, w_big_t = _in_proj_weights(_rows(ge["w_in"]))
    wq_t, wk, wv = _mla_up_weights(_rows(ge["w_q_b"]), _cols(ge["w_kv_b"]))
    conv_full = _cols(ge["conv_w"])
    late = ["w_attn_branch", "w_ssm_branch", "w_out", "ffn2_w_gate", "ffn2_w_up", "ffn2_w_down"]
    late_shards = [bf(n) for n in late]

    tabs = _rope_tables(positions, T)
    qg, kg = _pad_lanes(p["q_head_norm"]), _pad_lanes(p["k_head_norm"])
    bias128 = _pad_lanes(jnp.concatenate([p["dt_bias_fwd"], p["dt_bias_bwd"]], axis=1))
    alog128 = _pad_lanes(jnp.concatenate([p["a_log_fwd"], p["a_log_bwd"]], axis=1))
    skip_x = jnp.repeat(p["d_skip"], 64, axis=1)

    x1, ffn1_saved = _ffn_fwd(xs_in, p["ffn1_norm"], w_g1t, w_u1t, w_d1, "ffn1")
    h2 = _rms_fwd(x1, p["mix_norm"], name="mix_rms")
    u_big = _mm(h2, w_big_t, name="in_big", tb=True)
    u_small = _mm(h2, w_small_t, name="in_small", tb=True)
    cqn, ckvn = _lora_norm_fwd(u_small, p["q_a_norm"], p["kv_a_norm"], name="lora_norm")
    q_raw = _mm(cqn, wq_t, name="q_up", tb=True)
    k_raw = _mm(ckvn, wk, name="k_up")
    v = _mm(ckvn, wv, name="v_up", out_dtype=BF16)
    q, k = _qk_prep_fwd(q_raw, k_raw, u_small, tabs, qg, kg, name="qk_prep")
    a_out, lse, g_late = _attn_fwd(q, k, v, ["gather"] * len(late), late_shards, name="attn_fwd")
    gl = dict(zip(late, g_late))
    w_pa, w_pb, w_o = _rows(gl["w_attn_branch"]), _rows(gl["w_ssm_branch"]), _rows(gl["w_out"])
    w_g2t, w_u2t = _rows(gl["ffn2_w_gate"]), _rows(gl["ffn2_w_up"])
    w_d2 = _rows(gl["ffn2_w_down"])
    xbc_act = _conv_fwd(u_big, conv_full, p["conv_b"], name="conv_fwd")
    y_f, hin_f = _ssd_fwd(xbc_act, u_small, bias128, alog128, rev=False, name="ssd_fwd_f")
    y_b, hin_b = _ssd_fwd(xbc_act, u_small, bias128, alog128, rev=True, name="ssd_fwd_b")
    m_out = _ssm_out_fwd(y_f, y_b, xbc_act, u_big, skip_x, p["ssm_norm"], name="ssm_out")
    pa = _mm(a_out, w_pa, name="branch_a")
    pb = _mm(m_out, w_pb, name="branch_b")
    merged = _merge_fwd(pa, pb, u_big, name="merge")
    x2 = _mm(merged, w_o, name="mix_out", res=x1)
    y, ffn2_saved = _ffn_fwd(x2, p["ffn2_norm"], w_g2t, w_u2t, w_d2, "ffn2")
    dy, loss_part = _loss_bwd(y, target, name="loss")
    loss = lax.psum(loss_part, ("x", "y", "c"))

    gs = {}
    dx2, gs["ffn2_norm"], g_gate2, g_up2, g_down2 = _ffn_bwd(dy, x2, p["ffn2_norm"], w_g2t, w_u2t, w_d2, ffn2_saved,
                                                             "ffn2b")
    dmerged = _mm(dx2, w_o, name="d_merged", tb=True)
    g_out = _mm(merged, dx2, name="d_w_out", ta=True, out_dtype=BF16)
    dpa, dpb, dga, dgb = _merge_bwd(dmerged, pa, pb, u_big, name="d_merge")
    g_pa = _mm(a_out, dpa, name="d_w_pa", ta=True, out_dtype=BF16)
    g_pb = _mm(m_out, dpb, name="d_w_pb", ta=True, out_dtype=BF16)
    da_out = _mm(dpa, w_pa, name="d_a", tb=True)
    dm_out = _mm(dpb, w_pb, name="d_m", tb=True)
    late_grads = [_shard_rows(g) for g in (g_pa, g_pb, g_out, g_gate2, g_up2, g_down2)]
    dq, dk, dv, r_late = _attn_bwd(q, k, v, a_out, lse, da_out, ["scatter"] * len(late_grads), late_grads,
                                   name="attn_bwd")
    recv = dict(zip(late, r_late))

    dyss, dz, gs["ssm_norm"], dskip_ch = _ssm_out_bwd(dm_out, y_f, y_b, xbc_act, u_big, skip_x, p["ssm_norm"],
                                                      name="d_ssm_out")
    dact_f, draw_f, dalog_f, dbias_f = _ssd_bwd(dyss, xbc_act, u_small, bias128, alog128, hin_f, skip_x,
                                                rev=False, name="ssd_bwd_f")
    dact_b, draw_b, dalog_b, dbias_b = _ssd_bwd(dyss, xbc_act, u_small, bias128, alog128, hin_b, None,
                                                rev=True, name="ssd_bwd_b")
    dxbc, g_conv, gs["conv_b"] = _conv_bwd(dact_f, dact_b, u_big, conv_full, p["conv_b"], name="conv_bwd")

    dq_raw, dk_raw, dkpe, gqh, gkh = _qk_prep_bwd(dq, dk, q_raw, k_raw, u_small, tabs, qg, kg, name="d_qk_prep")
    g_wq_t = _mm(dq_raw, cqn, name="d_w_q", ta=True, out_dtype=BF16)
    g_wk_t = _mm(dk_raw, ckvn, name="d_w_k", ta=True, out_dtype=BF16)
    g_wv_t = _mm(dv, ckvn, name="d_w_v", ta=True, out_dtype=BF16)
    dcqn = _mm(dq_raw, wq_t, name="d_cqn")
    dckvn = _mm(dk_raw, wk, name="d_ckvn_k", tb=True)
    dckvn = _mm(dv, wv, name="d_ckvn_v", tb=True, res=dckvn)
    du_small, gs["q_a_norm"], gkv = _lora_norm_bwd(dcqn, dckvn, u_small, p["q_a_norm"], p["kv_a_norm"], dkpe,
                                                   draw_f, draw_b, name="d_lora_norm")

    dh2 = _mm(du_small, w_small_t, name="d_h2_small")
    dh2 = _mm(dz, w_big_t, name="d_h2_z", b_row0=0, res=dh2)
    dh2 = _mm(dxbc, w_big_t, name="d_h2_xbc", b_row0=2048, res=dh2)
    dh2 = _mm(dga, w_big_t, name="d_h2_ga", b_row0=5120, res=dh2)
    dh2 = _mm(dgb, w_big_t, name="d_h2_gb", b_row0=6144, res=dh2)
    gt_small = _mm(du_small, h2, name="d_w_small", ta=True, out_dtype=BF16)
    gt_z = _mm(dz, h2, name="d_w_z", ta=True, out_dtype=BF16)
    gt_xbc = _mm(dxbc, h2, name="d_w_xbc", ta=True, out_dtype=BF16)
    gt_ga = _mm(dga, h2, name="d_w_ga", ta=True, out_dtype=BF16)
    gt_gb = _mm(dgb, h2, name="d_w_gb", ta=True, out_dtype=BF16)
    dx1, gs["mix_norm"] = _rms_bwd(dh2, x1, p["mix_norm"], dx2, name="d_mix_rms")
    grad_x, gs["ffn1_norm"], g_gate1, g_up1, g_down1 = _ffn_bwd(dx1, xs_in, p["ffn1_norm"], w_g1t, w_u1t, w_d1,
                                                                ffn1_saved, "ffn1b")

    gt_in = jnp.concatenate([gt_small[0:384], gt_small[U_CKV:U_KPE + QK_ROPE], gt_z, gt_xbc,
                             gt_small[U_DT:U_DT + 64], gt_ga, gt_gb], axis=0)
    gt_in = jnp.pad(gt_in.reshape(N_DEV, W_IN_SHARD, D_MODEL), ((0, 0), (0, W_IN_SHARD_PAD - W_IN_SHARD), (0, 0)))
    gt_q = g_wq_t.reshape(N_HEADS, LANES, Q_LORA)[:, :QK_HEAD].reshape(N_DEV, -1, Q_LORA)
    gk3 = g_wk_t.reshape(N_HEADS, LANES, KV_LORA)[:, :QK_NOPE]
    gv3 = g_wv_t.reshape(N_HEADS, LANES, KV_LORA)
    even = (jnp.arange(N_HEADS) % 2 == 0)[:, None, None]
    gv3 = jnp.where(even, gv3[:, :V_HEAD], gv3[:, V_HEAD:])
    gt_kv = jnp.concatenate([gk3, gv3], axis=1).reshape(N_DEV, -1, KV_LORA)
    gsmall = _small_slab(gs, dskip_ch, dalog_f, dalog_b, dbias_f, dbias_b, gkv, gqh, gkh, g_conv, name="small_slab")
    last = ["ffn1_w_gate", "ffn1_w_up", "ffn1_w_down", "w_in", "w_q_b", "w_kv_b"]
    last_grads = [_shard_rows(g_gate1), _shard_rows(g_up1), _shard_rows(g_down1), gt_in, gt_q, gt_kv]
    r_last = _exchange(["scatter"] * len(last) + ["gather"], last_grads + [gsmall], name="grad_exchange")
    recv.update(zip(last, r_last[:-1]))
    srecv = r_last[-1]

    out = {}
    for n in ("ffn1_w_down", "ffn2_w_down", "w_attn_branch", "w_ssm_branch", "w_out", "ffn1_w_gate", "ffn1_w_up",
              "ffn2_w_gate", "ffn2_w_up", "w_q_b"):
        out[n] = _reduce_adamw(recv[n], w2[n], m2[n], v2[n], name=f"adamw_{n}")
    out["w_kv_b"] = _reduce_t_adamw(recv["w_kv_b"], w2["w_kv_b"], m2["w_kv_b"], v2["w_kv_b"], name="adamw_w_kv_b")
    g_in = _reduce8(recv["w_in"], name="sum_w_in", tile=W_IN_SHARD_PAD // 2)[:W_IN_SHARD]
    out["w_in"] = [g_in] + list(_adamw(g_in, w2["w_in"], m2["w_in"], v2["w_in"], name="adamw_w_in"))
    me = 4 * lax.axis_index("x") + 2 * lax.axis_index("y") + lax.axis_index("c")
    conv_g = lax.dynamic_slice(srecv, (0, CONV_ROW, me * (XBC_DIM // N_DEV)), (N_DEV, CONV_WIDTH, XBC_DIM // N_DEV))
    sn = [n for n, _ in SMALL] + ["conv_w"]
    sg, sd, sm, sv = _adamw_small(srecv, conv_g, [w2[n] for n in sn], [m2[n] for n in sn], [v2[n] for n in sn],
                                  name="adamw_small")
    for i, n in enumerate(sn):
        out[n] = (sg[i], sd[i], sm[i], sv[i])
    def back(n, a):
        if n in TRANSPOSED:
            return jnp.swapaxes(a.reshape(1, a.shape[0], a.shape[1]), 1, 2)
        return a.reshape(w_all[n].shape)

    outs = [[back(n, out[n][kind]) for n in WEIGHT_ORDER] for kind in range(4)]
    return (loss, grad_x[None], *outs[0], *outs[1], *outs[2], *outs[3])
```

```python
import math

import jax
import jax.numpy as jnp
from jax import lax
from jax.experimental import pallas as pl
from jax.experimental.pallas import tpu as pltpu

F32, BF16 = jnp.float32, jnp.bfloat16
HIGHEST = lax.Precision.HIGHEST

D_MODEL, D_FF = 1024, 2816
EPS = 1e-6
N_HEADS, QK_NOPE, QK_ROPE, QK_HEAD, V_HEAD = 16, 64, 32, 96, 64
Q_LORA, KV_LORA = 384, 256
ROPE_BASE = 10000.0
D_INNER, SSM_HEADS, SSM_GROUPS, D_STATE, CONV_WIDTH, CHUNK = 2048, 32, 4, 128, 5, 128
XBC_DIM = D_INNER + 2 * SSM_GROUPS * D_STATE
IN_DIM = 7904
ADAM_LR, ADAM_B1, ADAM_B2, ADAM_EPS, ADAM_WD, ADAM_STEP = 0.001, 0.9, 0.999, 1e-08, 0.01, 10
N_DEV = 8

V7X_VMEM_BYTES = 64 * 1024 * 1024
VMEM_LIMIT = V7X_VMEM_BYTES - 8 * 1024 * 1024
LANES = 128
W_IN_SHARD = IN_DIM // N_DEV
W_IN_SHARD_PAD = 992

SMALL = (
    ("ffn1_norm", 1024), ("mix_norm", 1024), ("q_a_norm", 384), ("kv_a_norm", 256), ("q_head_norm", 96),
    ("k_head_norm", 96), ("conv_b", 3072), ("a_log_fwd", 32), ("a_log_bwd", 32), ("dt_bias_fwd", 32),
    ("dt_bias_bwd", 32), ("d_skip", 32), ("ssm_norm", 2048), ("ffn2_norm", 1024),
)
TRANSPOSED = ("ffn1_w_gate", "ffn1_w_up", "ffn2_w_gate", "ffn2_w_up", "w_in", "w_q_b")
SMALL_ROW = {n: i for i, (n, _) in enumerate(SMALL)}
CONV_ROW = len(SMALL)
SMALL_ROWS, SMALL_COLS = 24, XBC_DIM
WEIGHT_ORDER = (
    "ffn1_norm", "ffn1_w_gate", "ffn1_w_up", "ffn1_w_down", "mix_norm", "w_in", "q_a_norm", "w_q_b", "kv_a_norm",
    "w_kv_b", "q_head_norm", "k_head_norm", "conv_w", "conv_b", "a_log_fwd", "a_log_bwd", "dt_bias_fwd", "dt_bias_bwd",
    "d_skip", "ssm_norm", "w_attn_branch", "w_ssm_branch", "w_out", "ffn2_norm", "ffn2_w_gate", "ffn2_w_up",
    "ffn2_w_down",
)


def _pallas(body, **kw):
    return pl.pallas_call(body, **kw)


def _params(sem):
    return pltpu.CompilerParams(dimension_semantics=sem, vmem_limit_bytes=VMEM_LIMIT)


def _pick(dim, pref):
    if dim <= pref:
        return dim
    c = (pref // LANES) * LANES
    while c >= LANES:
        if dim % c == 0:
            return c
        c -= LANES
    raise ValueError((dim, pref))


def _sigmoid(x):
    return 1.0 / (1.0 + jnp.exp(-x))


def _softplus(x):
    return jnp.maximum(x, 0.0) + jnp.log(1.0 + jnp.exp(-jnp.abs(x)))


def _dot(a, b):
    return jnp.dot(a, b, preferred_element_type=F32)


def _dot_nt(a, b):
    return lax.dot_general(a, b, (((1,), (1,)), ((), ())), preferred_element_type=F32)


def _dot_tn(a, b):
    return lax.dot_general(a, b, (((0,), (0,)), ((), ())), preferred_element_type=F32)


def _dot_h(a, b):
    return jnp.dot(a, b, preferred_element_type=F32, precision=HIGHEST)


def _dot_h_nt(a, b):
    return lax.dot_general(a, b, (((1,), (1,)), ((), ())), preferred_element_type=F32, precision=HIGHEST)


def _dot_h_tn(a, b):
    return lax.dot_general(a, b, (((0,), (0,)), ((), ())), preferred_element_type=F32, precision=HIGHEST)


def _mesh_pos():
    return lax.axis_index("x"), lax.axis_index("y"), lax.axis_index("c")


def _comm_scratch(n):
    return [pltpu.SemaphoreType.DMA((7 * n,)), pltpu.SemaphoreType.DMA((7 * n,)), pltpu.SemaphoreType.DMA((n,))]


def _comm_copies(modes, srcs, dsts, send_sems, recv_sems, local_sems, arrivals):
    x, y, c = _mesh_pos()
    me = 4 * x + 2 * y + c
    local, remote = [], []
    for w, (mode, s, d) in enumerate(zip(modes, srcs, dsts)):
        gather = mode == "gather"
        if not arrivals and local_sems is not None:
            local.append(pltpu.make_async_copy(s if gather else s.at[me], d.at[me], local_sems.at[w]))
        for k in range(1, N_DEV):
            px = (1 - x) if (k & 4) else x
            py = (1 - y) if (k & 2) else y
            pc = (1 - c) if (k & 1) else c
            peer = 4 * px + 2 * py + pc
            idx = 7 * w + k - 1
            remote.append(pltpu.make_async_remote_copy(
                src_ref=s if gather else s.at[peer], dst_ref=d.at[peer] if arrivals else d.at[me],
                send_sem=send_sems.at[idx], recv_sem=recv_sems.at[idx],
                device_id=(px, py, pc), device_id_type=pl.DeviceIdType.MESH))
    return local, remote


def _comm_start(modes, srcs, dsts, sems):
    local, sends = _comm_copies(modes, srcs, dsts, *sems, arrivals=False)
    for cp in local + sends:
        cp.start()


def _comm_wait(modes, srcs, dsts, sems):
    _, recvs = _comm_copies(modes, srcs, dsts, *sems, arrivals=True)
    for cp in recvs:
        cp.wait_recv()
    local, sends = _comm_copies(modes, srcs, dsts, *sems, arrivals=False)
    for cp in sends:
        cp.wait_send()
    for cp in local:
        cp.wait()


def _comm_out_shapes(modes, arrays):
    return [jax.ShapeDtypeStruct((N_DEV,) + (a.shape if m == "gather" else a.shape[1:]), a.dtype)
            for m, a in zip(modes, arrays)]


def _exchange(modes, arrays, *, name):
    n = len(arrays)

    def body(*refs):
        srcs, dsts, sems = refs[:n], refs[n:2 * n], refs[2 * n:]
        _comm_start(modes, srcs, dsts, sems)
        _comm_wait(modes, srcs, dsts, sems)

    any_spec = pl.BlockSpec(memory_space=pl.ANY)
    return _pallas(body, name=name, out_shape=_comm_out_shapes(modes, arrays), in_specs=[any_spec] * n,
                   out_specs=[any_spec] * n, scratch_shapes=_comm_scratch(n))(*arrays)


def _exchange_start(modes, arrays, *, name):
    n = len(arrays)
    me = 4 * lax.axis_index("x") + 2 * lax.axis_index("y") + lax.axis_index("c")
    lands = []
    for m, a in zip(modes, arrays):
        own = a if m == "gather" else lax.dynamic_index_in_dim(a, me, 0, keepdims=False)
        zone = lax.empty((N_DEV,) + own.shape, a.dtype)
        lands.append(lax.dynamic_update_index_in_dim(zone, own, me, 0))

    def body(*refs):
        srcs, dsts = refs[:n], refs[n:2 * n]
        send_sems, recv_sems = refs[2 * n], refs[2 * n + 1]
        token = refs[-1]
        _, sends = _comm_copies(modes, srcs, dsts, send_sems, recv_sems, None, arrivals=False)
        for cp in sends:
            cp.start()
        token[...] = jnp.zeros_like(token)

    hbm = pl.BlockSpec(memory_space=pltpu.HBM)
    sem = pl.BlockSpec(memory_space=pltpu.SEMAPHORE)
    ins = [pltpu.with_memory_space_constraint(a, pltpu.HBM) for a in list(arrays) + lands]
    got = _pallas(
        body, name=name,
        out_shape=(pltpu.SemaphoreType.DMA((7 * n,)), pltpu.SemaphoreType.DMA((7 * n,)),
                   *[pltpu.HBM(a.shape, a.dtype) for a in ins], jax.ShapeDtypeStruct((8, LANES), F32)),
        in_specs=[hbm] * (2 * n), out_specs=(sem, sem, *[hbm] * (2 * n), pl.BlockSpec(memory_space=pltpu.VMEM)),
        input_output_aliases={i: 2 + i for i in range(2 * n)},
        compiler_params=pltpu.CompilerParams(has_side_effects=pltpu.SideEffectType.DATAFLOW_SIDE_EFFECTING),
    )(*ins)
    return (got[0], got[1], got[2:2 + n], got[2 + n:2 + 2 * n]), got[-1]


def _exchange_wait(modes, started, after, *, name):
    send_sems, recv_sems, srcs, lands = started
    n = len(srcs)

    def body(*refs):
        src_refs, dst_refs = refs[:n], refs[n:2 * n]
        ssem, rsem = refs[2 * n], refs[2 * n + 1]
        _, recvs = _comm_copies(modes, src_refs, dst_refs, ssem, rsem, None, arrivals=True)
        for cp in recvs:
            cp.wait_recv()
        _, sends = _comm_copies(modes, src_refs, dst_refs, ssem, rsem, None, arrivals=False)
        for cp in sends:
            cp.wait_send()

    hbm = pl.BlockSpec(memory_space=pltpu.HBM)
    sem = pl.BlockSpec(memory_space=pltpu.SEMAPHORE)
    both = list(srcs) + list(lands)
    got = _pallas(
        body, name=name, out_shape=tuple(pltpu.HBM(a.shape, a.dtype) for a in both),
        in_specs=[hbm] * (2 * n) + [sem, sem, pl.BlockSpec(memory_space=pl.ANY)], out_specs=tuple([hbm] * (2 * n)),
        input_output_aliases={i: i for i in range(2 * n)},
        compiler_params=pltpu.CompilerParams(has_side_effects=pltpu.SideEffectType.DATAFLOW_SIDE_EFFECTING),
    )(*both, send_sems, recv_sems, after)
    return got[n:]


def _all_gather_two_level(shards, *, name):
    n = len(shards)

    def body(*refs):
        srcs, outs = refs[:n], refs[n:2 * n]
        send_sems, recv_sems, local_sems = refs[2 * n:]
        x, y, c = _mesh_pos()
        me, sibling = (x, y, c), (x, y, 1 - c)
        chips = [(1 - x, y), (x, 1 - y), (1 - x, 1 - y)]

        def blk(w, px, py, pc):
            return outs[w].at[4 * px + 2 * py + pc]

        def copy(w, k, block, to, src=None):
            return pltpu.make_async_remote_copy(
                src_ref=blk(w, *block) if src is None else src, dst_ref=blk(w, *block),
                send_sem=send_sems.at[7 * w + k], recv_sem=recv_sems.at[7 * w + k], device_id=to,
                device_id_type=pl.DeviceIdType.MESH)

        mine = [pltpu.make_async_copy(srcs[w], blk(w, *me), local_sems.at[w]) for w in range(n)]
        for cp in mine:
            cp.start()
        first = []
        for w in range(n):
            first.append(copy(w, 0, me, sibling, src=srcs[w]))
            first += [copy(w, 1 + j, me, (*chip, c), src=srcs[w]) for j, chip in enumerate(chips)]
        for cp in first:
            cp.start()
        passed = []
        for w in range(n):
            for j, chip in enumerate(chips):
                copy(w, 1 + j, (*chip, c), me).wait_recv()
                fwd = copy(w, 4 + j, (*chip, c), sibling)
                fwd.start()
                passed.append(fwd)
        for w in range(n):
            copy(w, 0, sibling, me).wait_recv()
            for j, chip in enumerate(chips):
                copy(w, 4 + j, (*chip, 1 - c), me).wait_recv()
        for cp in first + passed:
            cp.wait_send()
        for cp in mine:
            cp.wait()

    any_spec = pl.BlockSpec(memory_space=pl.ANY)
    return _pallas(body, name=name, out_shape=_comm_out_shapes(["gather"] * n, shards), in_specs=[any_spec] * n,
                   out_specs=[any_spec] * n, scratch_shapes=_comm_scratch(n))(*shards)


def _mm(a, b, *, name, ta=False, tb=False, out_dtype=F32, alpha=1.0, res=None, tm=1024, tn=1408, tk=1408,
        b_row0=None):
    (K, M) = a.shape if ta else a.shape[::-1]
    (N, Kb) = b.shape if tb else b.shape[::-1]
    tm, tn, tk = _pick(M, tm), _pick(N, tn), _pick(K, tk)
    nk = K // tk
    if b_row0 is None:
        assert K == Kb, (a.shape, b.shape, ta, tb)
        kb0 = 0
    else:
        assert not tb and b_row0 % tk == 0 and b_row0 + K <= Kb, (a.shape, b.shape, b_row0)
        kb0 = b_row0 // tk
    a_spec = pl.BlockSpec((tk, tm), lambda i, j, k: (k, i)) if ta else pl.BlockSpec((tm, tk), lambda i, j, k: (i, k))
    b_spec = (pl.BlockSpec((tn, tk), lambda i, j, k: (j, k)) if tb
              else pl.BlockSpec((tk, tn), lambda i, j, k: (k + kb0, j)))
    o_spec = pl.BlockSpec((tm, tn), lambda i, j, k: (i, j))
    dn = (((0 if ta else 1,), (1 if tb else 0,)), ((), ()))
    has_res = res is not None

    def body(*refs):
        a_ref, b_ref = refs[0], refs[1]
        r_ref = refs[2] if has_res else None
        o_ref = refs[3] if has_res else refs[2]
        part = lax.dot_general(a_ref[...].astype(BF16), b_ref[...].astype(BF16), dn, preferred_element_type=F32)

        def finish(acc):
            if alpha != 1.0:
                acc = acc * alpha
            if has_res:
                acc = acc + r_ref[...]
            o_ref[...] = acc.astype(o_ref.dtype)

        if nk == 1:
            finish(part)
        else:
            acc_ref = refs[-1]
            k = pl.program_id(2)

            @pl.when(k == 0)
            def _():
                acc_ref[...] = part

            @pl.when(k > 0)
            def _():
                acc_ref[...] += part

            @pl.when(k == nk - 1)
            def _():
                finish(acc_ref[...])

    ins = [a, b] + ([res] if has_res else [])
    in_specs = [a_spec, b_spec] + ([o_spec] if has_res else [])
    return _pallas(
        body, name=name, grid=(M // tm, N // tn, nk), in_specs=in_specs, out_specs=o_spec,
        out_shape=jax.ShapeDtypeStruct((M, N), out_dtype),
        scratch_shapes=[pltpu.VMEM((tm, tn), F32)] if nk > 1 else [],
        compiler_params=_params(("parallel", "parallel", "arbitrary")),
    )(*ins)


def _col0(j):
    return 0


def _colj(j):
    return j


def _rowmap(fn, *, name, rows, tile, ins, consts=(), outs=(), accs=(), ncol=1):
    tile = min(tile, rows)
    nrow = rows // tile
    in_specs = [pl.BlockSpec((tile, w), lambda j, i, f=f: (i, f(j))) for _, w, f in ins]
    for arr, w, f in consts:
        in_specs.append(pl.BlockSpec((arr.shape[0], w), lambda j, i, f=f: (0, f(j))))
    out_specs = [pl.BlockSpec((tile, w), lambda j, i, f=f: (i, f(j))) for _, _, w, f in outs]
    out_specs += [pl.BlockSpec((1, w), lambda j, i, f=f: (0, f(j))) for _, w, f in accs]
    out_shape = [jax.ShapeDtypeStruct((rows, c), dt) for c, dt, _, _ in outs]
    out_shape += [jax.ShapeDtypeStruct((1, c), F32) for c, _, _ in accs]
    n_in, n_out = len(ins) + len(consts), len(outs)
    acc_fixed = [f is _col0 for _, _, f in accs]

    def body(*refs):
        res = fn(*[r[...].astype(F32) for r in refs[:n_in]])
        if not isinstance(res, (tuple, list)):
            res = (res,)
        for r, v in zip(refs[n_in:n_in + n_out], res[:n_out]):
            r[...] = v.astype(r.dtype)
        j, i = pl.program_id(0), pl.program_id(1)
        for r, v, fixed in zip(refs[n_in + n_out:], res[n_out:], acc_fixed):
            first = ((i == 0) & (j == 0)) if fixed else (i == 0)

            @pl.when(first)
            def _(r=r, v=v):
                r[...] = v

            @pl.when(jnp.logical_not(first))
            def _(r=r, v=v):
                r[...] += v

    arrays = [a for a, _, _ in ins] + [a for a, _, _ in consts]
    return _pallas(
        body, name=name, grid=(ncol, nrow), in_specs=in_specs, out_specs=out_specs, out_shape=out_shape,
        compiler_params=_params(("arbitrary", "arbitrary")),
    )(*arrays)


def _rms_fwd(x, g, *, name, tile=512):
    rows, d = x.shape

    def fn(xv, gv):
        r = lax.rsqrt(jnp.mean(xv * xv, axis=-1, keepdims=True) + EPS)
        return xv * r * gv

    return _rowmap(fn, name=name, rows=rows, tile=tile, ins=[(x, d, _col0)], consts=[(g, d, _col0)],
                   outs=[(d, BF16, d, _col0)])[0]


def _rms_bwd(dh, x, g, res, *, name, tile=512):
    rows, d = x.shape

    def fn(dhv, xv, rv, gv):
        r = lax.rsqrt(jnp.mean(xv * xv, axis=-1, keepdims=True) + EPS)
        xh = xv * r
        dxh = dhv * gv
        dx = r * (dxh - xh * jnp.mean(dxh * xh, axis=-1, keepdims=True))
        return rv + dx, jnp.sum(dhv * xh, axis=0, keepdims=True)

    return _rowmap(fn, name=name, rows=rows, tile=tile, ins=[(dh, d, _col0), (x, d, _col0), (res, d, _col0)],
                   consts=[(g, d, _col0)], outs=[(d, F32, d, _col0)], accs=[(d, d, _col0)])


def _swiglu_fwd(gu, *, name, tile=512):
    rows = gu.shape[0]
    w = _pick(D_FF, 1408)
    nb = D_FF // w

    def fn(gv, uv):
        return gv * _sigmoid(gv) * uv

    return _rowmap(fn, name=name, rows=rows, tile=tile, ncol=nb,
                   ins=[(gu, w, _colj), (gu, w, lambda j: j + nb)], outs=[(D_FF, BF16, w, _colj)])[0]


def _swiglu_bwd(da, gu, *, name, tile=512):
    rows = gu.shape[0]
    w = _pick(D_FF, 1408)
    nb = D_FF // w

    def fn(dav, gv, uv):
        sg = _sigmoid(gv)
        dg = dav * uv * (sg * (1.0 + gv * (1.0 - sg)))
        du = dav * (gv * sg)
        return dg, du

    return _rowmap(fn, name=name, rows=rows, tile=tile, ncol=nb,
                   ins=[(da, w, _colj), (gu, w, _colj), (gu, w, lambda j: j + nb)],
                   outs=[(D_FF, BF16, w, _colj), (D_FF, BF16, w, _colj)])


U_CKV, U_KPE, U_DT = 512, 768, 896


def _lora_norm_fwd(u_small, qg, kvg, *, name, tile=512):
    rows = u_small.shape[0]

    def fn(cq, ckv, qgv, kgv):
        rq = lax.rsqrt(jnp.mean(cq * cq, axis=-1, keepdims=True) + EPS)
        rk = lax.rsqrt(jnp.mean(ckv * ckv, axis=-1, keepdims=True) + EPS)
        return cq * rq * qgv, ckv * rk * kgv

    return _rowmap(fn, name=name, rows=rows, tile=tile,
                   ins=[(u_small, Q_LORA, _col0), (u_small, KV_LORA, lambda j: U_CKV // KV_LORA)],
                   consts=[(qg, Q_LORA, _col0), (kvg, KV_LORA, _col0)],
                   outs=[(Q_LORA, BF16, Q_LORA, _col0), (KV_LORA, BF16, KV_LORA, _col0)])


def _lora_norm_bwd(dcqn, dckvn, u_small, qg, kvg, dkpe, draw_f, draw_b, *, name, tile=512):
    rows = u_small.shape[0]
    tile = min(tile, rows)

    def body(dq_ref, dk_ref, u_ref, dkp_ref, df_ref, db_ref, qg_ref, kg_ref, du_ref, gq_ref, gk_ref):
        cq, ckv = u_ref[:, 0:Q_LORA], u_ref[:, U_CKV:U_CKV + KV_LORA]
        dq, dk = dq_ref[...], dk_ref[...]
        rq = lax.rsqrt(jnp.mean(cq * cq, axis=-1, keepdims=True) + EPS)
        xh = cq * rq
        dxh = dq * qg_ref[...]
        du_ref[:, 0:Q_LORA] = (rq * (dxh - xh * jnp.mean(dxh * xh, axis=-1, keepdims=True))).astype(BF16)
        du_ref[:, Q_LORA:U_CKV] = jnp.zeros((tile, U_CKV - Q_LORA), BF16)
        rk = lax.rsqrt(jnp.mean(ckv * ckv, axis=-1, keepdims=True) + EPS)
        kh = ckv * rk
        dkh = dk * kg_ref[...]
        du_ref[:, U_CKV:U_KPE] = (rk * (dkh - kh * jnp.mean(dkh * kh, axis=-1, keepdims=True))).astype(BF16)
        du_ref[:, U_KPE:U_DT] = dkp_ref[...].astype(BF16)
        du_ref[:, U_DT:U_DT + LANES] = (df_ref[...] + db_ref[...]).astype(BF16)
        gq = jnp.sum(dq * xh, axis=0, keepdims=True)
        gk = jnp.sum(dk * kh, axis=0, keepdims=True)
        i = pl.program_id(0)

        @pl.when(i == 0)
        def _():
            gq_ref[...] = gq
            gk_ref[...] = gk

        @pl.when(i > 0)
        def _():
            gq_ref[...] += gq
            gk_ref[...] += gk

    def rowblk(w):
        return pl.BlockSpec((tile, w), lambda i: (i, 0))

    def whole(w):
        return pl.BlockSpec((1, w), lambda i: (0, 0))

    return _pallas(
        body, name=name, grid=(rows // tile,),
        in_specs=[rowblk(Q_LORA), rowblk(KV_LORA), rowblk(1024), rowblk(LANES), rowblk(LANES), rowblk(LANES),
                  whole(Q_LORA), whole(KV_LORA)],
        out_specs=[rowblk(1024), whole(Q_LORA), whole(KV_LORA)],
        out_shape=[jax.ShapeDtypeStruct((rows, 1024), BF16), jax.ShapeDtypeStruct((1, Q_LORA), F32),
                   jax.ShapeDtypeStruct((1, KV_LORA), F32)],
        compiler_params=_params(("arbitrary",)),
    )(dcqn, dckvn, u_small, dkpe, draw_f, draw_b, qg, kvg)


def _rope(x, c, s1, s2):
    return x * c + pltpu.roll(x, 112, 1) * s1 + pltpu.roll(x, 16, 1) * s2


def _rope_t(d, c, s1, s2):
    return d * c + pltpu.roll(d * s1, 16, 1) + pltpu.roll(d * s2, 112, 1)


def _qk_prep_fwd(q_raw, k_raw, u_small, tabs, qg, kg, *, name, tile=256):
    rows = q_raw.shape[0]
    tile = min(tile, rows)
    scale = 1.0 / math.sqrt(QK_HEAD)

    def body(q_ref, k_ref, u_ref, c_ref, s1_ref, s2_ref, qg_ref, kg_ref, qo_ref, ko_ref):
        c, s1, s2 = c_ref[...], s1_ref[...], s2_ref[...]
        qgv, kgv = qg_ref[...], kg_ref[...]
        kpe = pltpu.roll(u_ref[:, U_KPE:U_KPE + LANES], 64, 1)
        for h in range(N_HEADS):
            hs = slice(h * LANES, (h + 1) * LANES)
            qr = q_ref[:, hs]
            rq = lax.rsqrt(jnp.sum(qr * qr, axis=-1, keepdims=True) / QK_HEAD + EPS)
            qo_ref[:, hs] = (_rope(qr * rq * qgv, c, s1, s2) * scale).astype(BF16)
            xk = k_ref[:, hs] + kpe
            rk = lax.rsqrt(jnp.sum(xk * xk, axis=-1, keepdims=True) / QK_HEAD + EPS)
            ko_ref[:, hs] = _rope(xk * rk * kgv, c, s1, s2).astype(BF16)

    wide = pl.BlockSpec((tile, 2048), lambda i: (i, 0))
    narrow = pl.BlockSpec((tile, LANES), lambda i: (i, 0))
    gain = pl.BlockSpec((1, LANES), lambda i: (0, 0))
    return _pallas(
        body, name=name, grid=(rows // tile,),
        in_specs=[wide, wide, pl.BlockSpec((tile, 1024), lambda i: (i, 0)), narrow, narrow, narrow, gain, gain],
        out_specs=[wide, wide], out_shape=[jax.ShapeDtypeStruct((rows, 2048), BF16)] * 2,
        compiler_params=_params(("parallel",)),
    )(q_raw, k_raw, u_small, *tabs, qg, kg)


def _qk_prep_bwd(dq, dk, q_raw, k_raw, u_small, tabs, qg, kg, *, name, tile=256):
    rows = q_raw.shape[0]
    tile = min(tile, rows)
    scale = 1.0 / math.sqrt(QK_HEAD)

    def body(dq_ref, dk_ref, q_ref, k_ref, u_ref, c_ref, s1_ref, s2_ref, qg_ref, kg_ref,
             dqo_ref, dko_ref, dkpe_ref, gq_ref, gk_ref):
        c, s1, s2 = c_ref[...], s1_ref[...], s2_ref[...]
        qgv, kgv = qg_ref[...], kg_ref[...]
        kpe = pltpu.roll(u_ref[:, U_KPE:U_KPE + LANES], 64, 1)
        lane = lax.broadcasted_iota(jnp.int32, (tile, LANES), 1)
        gq = jnp.zeros((1, LANES), F32)
        gk = jnp.zeros((1, LANES), F32)
        dkpe = jnp.zeros((tile, LANES), F32)
        for h in range(N_HEADS):
            hs = slice(h * LANES, (h + 1) * LANES)
            qr = q_ref[:, hs]
            rq = lax.rsqrt(jnp.sum(qr * qr, axis=-1, keepdims=True) / QK_HEAD + EPS)
            xh = qr * rq
            dy = _rope_t(dq_ref[:, hs] * scale, c, s1, s2)
            dxh = dy * qgv
            dqo_ref[:, hs] = (rq * (dxh - xh * (jnp.sum(dxh * xh, axis=-1, keepdims=True) / QK_HEAD))).astype(BF16)
            gq = gq + jnp.sum(dy * xh, axis=0, keepdims=True)
            xk = k_ref[:, hs] + kpe
            rk = lax.rsqrt(jnp.sum(xk * xk, axis=-1, keepdims=True) / QK_HEAD + EPS)
            kh = xk * rk
            dyk = _rope_t(dk_ref[:, hs], c, s1, s2)
            dkh = dyk * kgv
            dxk = rk * (dkh - kh * (jnp.sum(dkh * kh, axis=-1, keepdims=True) / QK_HEAD))
            gk = gk + jnp.sum(dyk * kh, axis=0, keepdims=True)
            dko_ref[:, hs] = jnp.where(lane < QK_NOPE, dxk, 0.0).astype(BF16)
            dkpe = dkpe + dxk
        dkpe_ref[...] = jnp.where(lane < QK_ROPE, pltpu.roll(dkpe, 64, 1), 0.0)
        i = pl.program_id(0)

        @pl.when(i == 0)
        def _():
            gq_ref[...] = gq
            gk_ref[...] = gk

        @pl.when(i > 0)
        def _():
            gq_ref[...] += gq
            gk_ref[...] += gk

    wide = pl.BlockSpec((tile, 2048), lambda i: (i, 0))
    narrow = pl.BlockSpec((tile, LANES), lambda i: (i, 0))
    gain = pl.BlockSpec((1, LANES), lambda i: (0, 0))
    return _pallas(
        body, name=name, grid=(rows // tile,),
        in_specs=[wide, wide, wide, wide, pl.BlockSpec((tile, 1024), lambda i: (i, 0)), narrow, narrow, narrow,
                  gain, gain],
        out_specs=[wide, wide, narrow, gain, gain],
        out_shape=[jax.ShapeDtypeStruct((rows, 2048), BF16)] * 2
        + [jax.ShapeDtypeStruct((rows, LANES), F32), jax.ShapeDtypeStruct((1, LANES), F32),
           jax.ShapeDtypeStruct((1, LANES), F32)],
        compiler_params=_params(("arbitrary",)),
    )(dq, dk, q_raw, k_raw, u_small, *tabs, qg, kg)


def _attn_fwd(q, k, v, comm_modes, comm_arrays, *, name, tq=512, tkc=512):
    T = q.shape[0]
    tq = min(tq, T)
    tkc = min(tkc, T)
    n = len(comm_arrays)
    nj, ni = N_HEADS // 2, T // tq

    def body(*refs):
        q_ref, k_ref, v_ref = refs[:3]
        srcs = refs[3:3 + n]
        o_ref, lse_ref = refs[3 + n:5 + n]
        dsts = refs[5 + n:5 + 2 * n]
        sems = refs[5 + 2 * n:]
        j, i = pl.program_id(0), pl.program_id(1)

        @pl.when((j == 0) & (i == 0))
        def _():
            _comm_start(comm_modes, srcs, dsts, sems)

        out = None
        for hh in range(2):
            sl = slice(hh * LANES, (hh + 1) * LANES)
            qv = q_ref[:, sl]
            m = l = acc = None
            for kc in range(T // tkc):
                ks = slice(kc * tkc, (kc + 1) * tkc)
                s = _dot_nt(qv, k_ref[ks, sl])
                mc = jnp.max(s, axis=-1, keepdims=True)
                if m is None:
                    m = mc
                    p = jnp.exp(s - m)
                    l = jnp.sum(p, axis=-1, keepdims=True)
                    acc = _dot(p.astype(BF16), v_ref[ks, sl])
                else:
                    m_new = jnp.maximum(m, mc)
                    alpha = jnp.exp(m - m_new)
                    p = jnp.exp(s - m_new)
                    l = alpha * l + jnp.sum(p, axis=-1, keepdims=True)
                    acc = alpha * acc + _dot(p.astype(BF16), v_ref[ks, sl])
                    m = m_new
            o = acc / l
            out = o if out is None else out + o
            lse_ref[hh] = m + jnp.log(l)
        o_ref[...] = out

        @pl.when((j == nj - 1) & (i == ni - 1))
        def _():
            _comm_wait(comm_modes, srcs, dsts, sems)

    any_spec = pl.BlockSpec(memory_space=pl.ANY)
    got = _pallas(
        body, name=name, grid=(nj, ni),
        in_specs=[pl.BlockSpec((tq, 2 * LANES), lambda j, i: (i, j)), pl.BlockSpec((T, 2 * LANES), lambda j, i: (0, j)),
                  pl.BlockSpec((T, 2 * LANES), lambda j, i: (0, j))] + [any_spec] * n,
        out_specs=[pl.BlockSpec((tq, LANES), lambda j, i: (i, j)), pl.BlockSpec((2, tq, 1), lambda j, i: (j, i, 0))]
        + [any_spec] * n,
        out_shape=[jax.ShapeDtypeStruct((T, N_HEADS * V_HEAD), F32), jax.ShapeDtypeStruct((N_HEADS, T, 1), F32)]
        + _comm_out_shapes(comm_modes, comm_arrays),
        scratch_shapes=_comm_scratch(n),
        compiler_params=_params(("arbitrary", "arbitrary")),
    )(q, k, v, *comm_arrays)
    return got[0], got[1], got[2:]


def _attn_bwd(q, k, v, o, lse, do, comm_modes, comm_arrays, *, name, tk=256, tqc=4096):
    T = q.shape[0]
    tk = min(tk, T)
    tqc = min(tqc, T)
    n = len(comm_arrays)
    nj, nkb = N_HEADS // 2, T // tk

    def body(*refs):
        q_ref, k_ref, v_ref, o_ref, lse_ref, do_ref = refs[:6]
        srcs = refs[6:6 + n]
        dq_ref, dk_ref, dv_ref = refs[6 + n:9 + n]
        dsts = refs[9 + n:9 + 2 * n]
        d_s = refs[9 + 2 * n]
        sems = refs[10 + 2 * n:]
        j, kb = pl.program_id(0), pl.program_id(1)

        @pl.when((j == 0) & (kb == 0))
        def _():
            _comm_start(comm_modes, srcs, dsts, sems)

        lane = lax.broadcasted_iota(jnp.int32, (1, LANES), 1)
        @pl.when(kb == 0)
        def _():
            prod = do_ref[...] * o_ref[...]
            for hh in range(2):
                keep = (lane < V_HEAD) if hh == 0 else (lane >= V_HEAD)
                d_s[hh] = jnp.sum(jnp.where(keep, prod, 0.0), axis=-1, keepdims=True)

        for hh in range(2):
            sl = slice(hh * LANES, (hh + 1) * LANES)
            keep = (lane < V_HEAD) if hh == 0 else (lane >= V_HEAD)
            kv, vv = k_ref[:, sl], v_ref[:, sl]
            dv_acc = dk_acc = None
            for qc in range(T // tqc):
                qs = slice(qc * tqc, (qc + 1) * tqc)
                qv = q_ref[qs, sl]
                do_b = do_ref[qs, :].astype(BF16)
                s = _dot_nt(qv, kv)
                p = jnp.exp(s - lse_ref[hh, qs])
                dp = _dot_nt(do_b, vv)
                ds = (p * (dp - d_s[hh, qs])).astype(BF16)
                dvc = _dot_tn(p.astype(BF16), do_b)
                dkc = _dot_tn(ds, qv)
                dv_acc = dvc if dv_acc is None else dv_acc + dvc
                dk_acc = dkc if dk_acc is None else dk_acc + dkc
                dqp = _dot(ds, kv)

                @pl.when(kb == 0)
                def _(dqp=dqp, sl=sl, qs=qs):
                    dq_ref[qs, sl] = dqp

                @pl.when(kb > 0)
                def _(dqp=dqp, sl=sl, qs=qs):
                    dq_ref[qs, sl] += dqp

            dv_ref[:, sl] = jnp.where(keep, dv_acc, 0.0).astype(BF16)
            dk_ref[:, sl] = dk_acc

        @pl.when((j == nj - 1) & (kb == nkb - 1))
        def _():
            _comm_wait(comm_modes, srcs, dsts, sems)

    any_spec = pl.BlockSpec(memory_space=pl.ANY)
    pair = pl.BlockSpec((T, 2 * LANES), lambda j, kb: (0, j))
    kblk = pl.BlockSpec((tk, 2 * LANES), lambda j, kb: (kb, j))
    got = _pallas(
        body, name=name, grid=(nj, nkb),
        in_specs=[pair, kblk, kblk, pl.BlockSpec((T, LANES), lambda j, kb: (0, j)),
                  pl.BlockSpec((2, T, 1), lambda j, kb: (j, 0, 0)), pl.BlockSpec((T, LANES), lambda j, kb: (0, j))]
        + [any_spec] * n,
        out_specs=[pair, kblk, kblk] + [any_spec] * n,
        out_shape=[jax.ShapeDtypeStruct((T, 2048), F32)] * 2 + [jax.ShapeDtypeStruct((T, 2048), BF16)]
        + _comm_out_shapes(comm_modes, comm_arrays),
        scratch_shapes=[pltpu.VMEM((2, T, 1), F32)] + _comm_scratch(n),
        compiler_params=_params(("arbitrary", "arbitrary")),
    )(q, k, v, o, lse, do, *comm_arrays)
    return got[0], got[1], got[2], got[3:]


CONV_ROWS, CONV_HALO = 64, 8
CONV_WIN = CONV_ROWS + 2 * CONV_HALO


def _conv_shift(x, sh, t_idx, total):
    if sh == 0:
        return x
    y = pltpu.roll(x, (-sh) % x.shape[0], 0)
    if t_idx is None:
        return y
    ok = (t_idx + sh >= 0) & (t_idx + sh < total)
    return jnp.where(ok, y, 0.0)


def _conv_positions(ws, shape):
    return ws + lax.broadcasted_iota(jnp.int32, shape, 0) if isinstance(ws, int) else None


def _aligned(v, m):
    return v if isinstance(v, int) else pl.multiple_of(v, m)


def _conv_chunks(T, chunk, carry):
    n = T // CONV_ROWS
    carry = chunk(0, 0, carry)

    def mid(ci, c):
        return chunk(pl.multiple_of(ci * CONV_ROWS - CONV_HALO, CONV_HALO), CONV_HALO, c)

    carry = lax.fori_loop(1, n - 1, mid, carry)
    return chunk(T - CONV_WIN, 2 * CONV_HALO, carry)


def _conv_pre(x, w_ref, b_ref, t_idx, total):
    pre = b_ref[...] + w_ref[2:3, :] * x
    for j in (0, 1, 3, 4):
        pre = pre + w_ref[j:j + 1, :] * _conv_shift(x, j - 2, t_idx, total)
    return pre


def _conv_fwd(u_big, conv_w, conv_b, *, name, w=256):
    T = u_big.shape[0]
    first = D_INNER // w

    def body(x_ref, w_ref, b_ref, o_ref):
        def chunk(ws, off, carry):
            x = x_ref[pl.ds(ws, CONV_WIN), :]
            pre = _conv_pre(x, w_ref, b_ref, _conv_positions(ws, x.shape), T)
            act = pre * _sigmoid(pre)
            o_ref[pl.ds(_aligned(ws + off, CONV_ROWS), CONV_ROWS), :] = act[off:off + CONV_ROWS]
            return carry

        _conv_chunks(T, chunk, 0)

    return _pallas(
        body, name=name, grid=(XBC_DIM // w,),
        in_specs=[pl.BlockSpec((T, w), lambda j: (0, j + first)), pl.BlockSpec((CONV_WIDTH, w), lambda j: (0, j)),
                  pl.BlockSpec((1, w), lambda j: (0, j))],
        out_specs=pl.BlockSpec((T, w), lambda j: (0, j)),
        out_shape=jax.ShapeDtypeStruct((T, XBC_DIM), F32),
        compiler_params=_params(("parallel",)),
    )(u_big, conv_w, conv_b)


def _conv_bwd(dact_f, dact_b, u_big, conv_w, conv_b, *, name, w=128):
    T = u_big.shape[0]
    first = D_INNER // w

    def body(df_ref, db_ref, x_ref, w_ref, b_ref, dx_ref, dw_ref, dbias_ref):
        def chunk(ws, off, sums):
            rows = pl.ds(ws, CONV_WIN)
            x = x_ref[rows, :]
            row = lax.broadcasted_iota(jnp.int32, x.shape, 0)
            t_idx = _conv_positions(ws, x.shape)
            pre = _conv_pre(x, w_ref, b_ref, t_idx, T)
            sg = _sigmoid(pre)
            dpre = (df_ref[rows, :] + db_ref[rows, :]) * (sg * (1.0 + pre * (1.0 - sg)))
            dx = w_ref[2:3, :] * dpre
            for j in (0, 1, 3, 4):
                dx = dx + w_ref[j:j + 1, :] * _conv_shift(dpre, 2 - j, t_idx, T)
            dx_ref[pl.ds(_aligned(ws + off, CONV_ROWS), CONV_ROWS), :] = dx[off:off + CONV_ROWS].astype(dx_ref.dtype)
            own = jnp.where((row >= off) & (row < off + CONV_ROWS), dpre, 0.0)
            new = [sums[5] + jnp.sum(own, axis=0, keepdims=True)]
            for j in range(CONV_WIDTH):
                new.insert(j, sums[j] + jnp.sum(own * _conv_shift(x, j - 2, t_idx, T), axis=0, keepdims=True))
            return tuple(new)

        zero = jnp.zeros((1, w), F32)
        sums = _conv_chunks(T, chunk, (zero,) * (CONV_WIDTH + 1))
        for j in range(CONV_WIDTH):
            dw_ref[j:j + 1, :] = sums[j]
        dbias_ref[...] = sums[CONV_WIDTH]

    blk = pl.BlockSpec((T, w), lambda j: (0, j))
    return _pallas(
        body, name=name, grid=(XBC_DIM // w,),
        in_specs=[blk, blk, pl.BlockSpec((T, w), lambda j: (0, j + first)),
                  pl.BlockSpec((CONV_WIDTH, w), lambda j: (0, j)), pl.BlockSpec((1, w), lambda j: (0, j))],
        out_specs=[blk, pl.BlockSpec((CONV_WIDTH, w), lambda j: (0, j)), pl.BlockSpec((1, w), lambda j: (0, j))],
        out_shape=[jax.ShapeDtypeStruct((T, XBC_DIM), BF16), jax.ShapeDtypeStruct((CONV_WIDTH, XBC_DIM), F32),
                   jax.ShapeDtypeStruct((1, XBC_DIM), F32)],
        compiler_params=_params(("parallel",)),
    )(dact_f, dact_b, u_big, conv_w, conv_b)


def _ssd_expand(rev):
    off = SSM_HEADS if rev else 0
    h = jnp.arange(LANES, dtype=jnp.int32)[:, None]
    return (jnp.arange(D_INNER, dtype=jnp.int32)[None, :] // 64 + off == h).astype(F32)


def _ssd_head_terms(dt_ref, bias_ref, alog_ref, acst_s, dtt_s, rev):
    L = CHUNK
    row = lax.broadcasted_iota(jnp.int32, (L, L), 0)
    col = lax.broadcasted_iota(jnp.int32, (L, L), 1)
    mask = (row <= col) if rev else (row >= col)
    cm = mask.astype(F32)
    cmt = ((row >= col) if rev else (row <= col)).astype(F32)
    pre = dt_ref[...] + bias_ref[...]
    dt = _softplus(pre)
    a = -jnp.exp(alog_ref[...])
    da = dt * a
    acs = _dot_h(cm, da)
    acst_s[...] = _dot_h_tn(da, cmt)
    dtt_s[...] = _dot_h_tn(dt, (row == col).astype(F32))
    tot = jnp.sum(da, axis=0, keepdims=True)
    w = jnp.exp(tot - acs)
    return dict(mask=mask, cm=cm, cmt=cmt, ident=(row == col).astype(F32), pre=pre, dt=dt, a=a, da=da, acs=acs,
                tot=tot, e=jnp.exp(acs), w=w, wdt=w * dt, dec=jnp.exp(tot))


def _pair(lo, v, h0):
    return jnp.where(lo, v[:, h0:h0 + 1], v[:, h0 + 1:h0 + 2])


def _ssd_fwd(xbc_act, u_small, bias128, alog128, *, rev, name):
    T = xbc_act.shape[0]
    L = CHUNK
    nc = T // L
    off = SSM_HEADS if rev else 0

    def cidx(c):
        return (nc - 1 - c) if rev else c

    def body(xs_ref, bm_ref, cm_ref, dt_ref, bias_ref, alog_ref, y_ref, hin_ref, ht_s, acst_s, dtt_s, wx_s, dec_s):
        c = pl.program_id(0)

        @pl.when(c == 0)
        def _():
            ht_s[...] = jnp.zeros_like(ht_s)

        t = _ssd_head_terms(dt_ref, bias_ref, alog_ref, acst_s, dtt_s, rev)
        lo = lax.broadcasted_iota(jnp.int32, (L, LANES), 1) < 64
        lo1 = lax.broadcasted_iota(jnp.int32, (1, LANES), 1) < 64
        for g in range(SSM_GROUPS):
            bmat = bm_ref[:, g * LANES:(g + 1) * LANES].astype(BF16)
            cmat = cm_ref[:, g * LANES:(g + 1) * LANES].astype(BF16)
            gmat = _dot_nt(cmat, bmat)
            ht = ht_s[g]
            ch = _dot(cmat, ht.astype(BF16))
            for pr in range(4):
                ps = slice(pr * LANES, (pr + 1) * LANES)
                cs = slice(g * 512 + pr * LANES, g * 512 + (pr + 1) * LANES)
                h0 = off + 8 * g + 2 * pr
                xp = xs_ref[:, cs]
                acc = _pair(lo, t["e"], h0) * ch[:, ps]
                for s_ in range(2):
                    h = h0 + s_
                    seg = t["acs"][:, h:h + 1] - acst_s[h:h + 1, :]
                    lam = jnp.exp(jnp.where(t["mask"], seg, -1e30))
                    m = (gmat * lam * dtt_s[h:h + 1, :]).astype(BF16)
                    xm = jnp.where(lo if s_ == 0 else jnp.logical_not(lo), xp, 0.0).astype(BF16)
                    acc = acc + _dot(m, xm)
                y_ref[:, cs] = acc
                wx_s[:, ps] = (_pair(lo, t["wdt"], h0) * xp).astype(BF16)
                dec_s[0:1, ps] = _pair(lo1, t["dec"], h0)
            hin_ref[0, g] = ht.astype(BF16)
            ht_s[g] = ht * dec_s[0:1, :] + _dot_tn(bmat, wx_s[...])

    return _pallas(
        body, name=name, grid=(nc,),
        in_specs=[pl.BlockSpec((L, D_INNER), lambda c: (cidx(c), 0)), pl.BlockSpec((L, 512), lambda c: (cidx(c), 4)),
                  pl.BlockSpec((L, 512), lambda c: (cidx(c), 5)),
                  pl.BlockSpec((L, LANES), lambda c: (cidx(c), U_DT // LANES)),
                  pl.BlockSpec((1, LANES), lambda c: (0, 0)), pl.BlockSpec((1, LANES), lambda c: (0, 0))],
        out_specs=[pl.BlockSpec((L, D_INNER), lambda c: (cidx(c), 0)),
                   pl.BlockSpec((1, SSM_GROUPS, D_STATE, 512), lambda c: (cidx(c), 0, 0, 0))],
        out_shape=[jax.ShapeDtypeStruct((T, D_INNER), F32), jax.ShapeDtypeStruct((nc, SSM_GROUPS, D_STATE, 512), BF16)],
        scratch_shapes=[pltpu.VMEM((SSM_GROUPS, D_STATE, 512), F32), pltpu.VMEM((LANES, L), F32),
                        pltpu.VMEM((LANES, L), F32), pltpu.VMEM((L, 512), BF16), pltpu.VMEM((8, 512), F32)],
        compiler_params=_params(("arbitrary",)),
    )(xbc_act, xbc_act, xbc_act, u_small, bias128, alog128)


def _ssd_bwd(dy, xbc_act, u_small, bias128, alog128, hin, skip_x, *, rev, name):
    T = xbc_act.shape[0]
    L = CHUNK
    nc = T // L
    off = SSM_HEADS if rev else 0
    has_skip = skip_x is not None

    def cidx(c):
        return c if rev else (nc - 1 - c)

    def body(*refs):
        (dy_ref, xs_ref, bm_ref, cm_ref, dt_ref, bias_ref, alog_ref, hin_ref) = refs[:8]
        k = 8
        skip_ref = refs[k] if has_skip else None
        k += 1 if has_skip else 0
        (dx_ref, draw_ref, dalog_ref, dbias_ref, dht_s, acst_s, dtt_s, rowt_s, ddtt_s, wx_s, edy_s, dec_s) = refs[k:]
        c = pl.program_id(0)

        @pl.when(c == 0)
        def _():
            dht_s[...] = jnp.zeros_like(dht_s)
            rowt_s[...] = jnp.zeros_like(rowt_s)
            ddtt_s[...] = jnp.zeros_like(ddtt_s)

        t = _ssd_head_terms(dt_ref, bias_ref, alog_ref, acst_s, dtt_s, rev)
        lane1 = lax.broadcasted_iota(jnp.int32, (1, LANES), 1)
        lo = lax.broadcasted_iota(jnp.int32, (L, LANES), 1) < 64
        lo1 = lane1 < 64
        colpart = jnp.zeros((L, LANES), F32)
        u_cols = jnp.zeros((L, LANES), F32)
        v_cols = jnp.zeros((L, LANES), F32)
        dtot_h = jnp.zeros((1, LANES), F32)
        for g in range(SSM_GROUPS):
            bmat = bm_ref[:, g * LANES:(g + 1) * LANES].astype(BF16)
            cmat = cm_ref[:, g * LANES:(g + 1) * LANES].astype(BF16)
            gmat = _dot_nt(cmat, bmat)
            ht_in = hin_ref[0, g]
            dht = dht_s[g]
            ht_in_b, dht_b = ht_in.astype(BF16), dht.astype(BF16)
            ch = _dot(cmat, ht_in_b)
            bdh = _dot(bmat, dht_b)
            th = jnp.sum(dht * ht_in, axis=0, keepdims=True)
            dgm = jnp.zeros((L, L), F32)
            for pr in range(4):
                ps = slice(pr * LANES, (pr + 1) * LANES)
                cs = slice(g * 512 + pr * LANES, g * 512 + (pr + 1) * LANES)
                h0 = off + 8 * g + 2 * pr
                xp = xs_ref[:, cs]
                dyp = dy_ref[:, cs]
                dyp_b = dyp.astype(BF16)
                wdt_p = _pair(lo, t["wdt"], h0)
                e_p = _pair(lo, t["e"], h0)
                xb = xp * bdh[:, ps]
                dc = dyp * ch[:, ps]
                dxp = wdt_p * bdh[:, ps]
                for s_ in range(2):
                    h = h0 + s_
                    keep = lo if s_ == 0 else jnp.logical_not(lo)
                    keep1 = lo1 if s_ == 0 else jnp.logical_not(lo1)
                    onehot = (lane1 == h).astype(F32)
                    dtrow = dtt_s[h:h + 1, :]
                    seg = t["acs"][:, h:h + 1] - acst_s[h:h + 1, :]
                    lam = jnp.exp(jnp.where(t["mask"], seg, -1e30))
                    mf0 = gmat * lam
                    m = (mf0 * dtrow).astype(BF16)
                    xm = jnp.where(keep, xp, 0.0).astype(BF16)
                    dm = _dot_nt(dyp_b, xm)
                    r = dm * mf0
                    q = r * dtrow
                    dgm = dgm + dm * lam * dtrow
                    colpart = colpart + jnp.sum(q, axis=1, keepdims=True) * onehot
                    rowt_s[h:h + 1, :] = jnp.sum(q, axis=0, keepdims=True)
                    ddtt_s[h:h + 1, :] = jnp.sum(r, axis=0, keepdims=True)
                    u_cols = u_cols + jnp.sum(jnp.where(keep, xb, 0.0), axis=1, keepdims=True) * onehot
                    v_cols = v_cols + jnp.sum(jnp.where(keep, dc, 0.0), axis=1, keepdims=True) * onehot
                    dtot_h = dtot_h + jnp.sum(jnp.where(keep1, th[:, ps], 0.0), axis=1, keepdims=True) * onehot
                    dxp = dxp + jnp.where(keep, _dot_tn(m, dyp_b), 0.0)
                if has_skip:
                    dxp = dxp + dyp * skip_ref[:, cs]
                dx_ref[:, cs] = dxp
                wx_s[:, ps] = (wdt_p * xp).astype(BF16)
                edy_s[:, ps] = (e_p * dyp).astype(BF16)
                dec_s[0:1, ps] = _pair(lo1, t["dec"], h0)
            edy_b = edy_s[...]
            dgm_b = dgm.astype(BF16)
            dx_ref[:, D_INNER + g * LANES:D_INNER + (g + 1) * LANES] = (
                _dot_nt(wx_s[...], dht_b) + _dot_tn(dgm_b, cmat))
            dx_ref[:, D_INNER + 512 + g * LANES:D_INNER + 512 + (g + 1) * LANES] = (
                _dot_nt(edy_b, ht_in_b) + _dot(dgm_b, bmat))
            dht_s[g] = dec_s[0:1, :] * dht + _dot_tn(cmat, edy_b)

        t_e = v_cols * t["e"]
        t_w = u_cols * t["wdt"]
        colsum_part = _dot_h_tn(rowt_s[...], t["ident"])
        dtot = jnp.sum(t_w, axis=0, keepdims=True) + t["dec"] * dtot_h
        row1 = lax.broadcasted_iota(jnp.int32, (L, LANES), 0)
        last = row1 == (0 if rev else L - 1)
        dacs = colpart - colsum_part + t_e - t_w + jnp.where(last, dtot, 0.0)
        dda = _dot_h(t["cmt"], dacs)
        ddt = dda * t["a"] + u_cols * t["w"] + _dot_h_tn(ddtt_s[...], t["ident"])
        dalog = jnp.sum(dda * t["dt"], axis=0, keepdims=True) * t["a"]
        draw = ddt * _sigmoid(t["pre"])
        draw_ref[...] = draw
        dbias = jnp.sum(draw, axis=0, keepdims=True)

        @pl.when(c == 0)
        def _():
            dalog_ref[...] = dalog
            dbias_ref[...] = dbias

        @pl.when(c > 0)
        def _():
            dalog_ref[...] += dalog
            dbias_ref[...] += dbias

    one = pl.BlockSpec((1, LANES), lambda c: (0, 0))
    in_specs = [pl.BlockSpec((L, D_INNER), lambda c: (cidx(c), 0)), pl.BlockSpec((L, D_INNER), lambda c: (cidx(c), 0)),
                pl.BlockSpec((L, 512), lambda c: (cidx(c), 4)), pl.BlockSpec((L, 512), lambda c: (cidx(c), 5)),
                pl.BlockSpec((L, LANES), lambda c: (cidx(c), U_DT // LANES)), one, one,
                pl.BlockSpec((1, SSM_GROUPS, D_STATE, 512), lambda c: (cidx(c), 0, 0, 0))]
    ins = [dy, xbc_act, xbc_act, xbc_act, u_small, bias128, alog128, hin]
    if has_skip:
        in_specs.append(pl.BlockSpec((1, D_INNER), lambda c: (0, 0)))
        ins.append(skip_x)
    return _pallas(
        body, name=name, grid=(nc,), in_specs=in_specs,
        out_specs=[pl.BlockSpec((L, XBC_DIM), lambda c: (cidx(c), 0)), pl.BlockSpec((L, LANES), lambda c: (cidx(c), 0)),
                   one, one],
        out_shape=[jax.ShapeDtypeStruct((T, XBC_DIM), F32), jax.ShapeDtypeStruct((T, LANES), F32),
                   jax.ShapeDtypeStruct((1, LANES), F32), jax.ShapeDtypeStruct((1, LANES), F32)],
        scratch_shapes=[pltpu.VMEM((SSM_GROUPS, D_STATE, 512), F32), pltpu.VMEM((LANES, L), F32),
                        pltpu.VMEM((LANES, L), F32), pltpu.VMEM((LANES, L), F32), pltpu.VMEM((LANES, L), F32),
                        pltpu.VMEM((L, 512), BF16), pltpu.VMEM((L, 512), BF16), pltpu.VMEM((8, 512), F32)],
        compiler_params=_params(("arbitrary",)),
    )(*ins)


def _ssm_out_fwd(y_f, y_b, xbc_act, u_big, skip_x, ssm_norm, *, name, tile=512):
    rows = y_f.shape[0]

    def fn(yf, yb, xs, z, sk, nw):
        yz = (yf + yb + sk * xs) * (z * _sigmoid(z))
        r = lax.rsqrt(jnp.mean(yz * yz, axis=-1, keepdims=True) + EPS)
        return yz * r * nw

    return _rowmap(fn, name=name, rows=rows, tile=tile, ncol=SSM_GROUPS,
                   ins=[(y_f, 512, _colj), (y_b, 512, _colj), (xbc_act, 512, _colj), (u_big, 512, _colj)],
                   consts=[(skip_x, 512, _colj), (ssm_norm, 512, _colj)], outs=[(D_INNER, BF16, 512, _colj)])[0]


def _ssm_out_bwd(dm, y_f, y_b, xbc_act, u_big, skip_x, ssm_norm, *, name, tile=512):
    rows = y_f.shape[0]

    def fn(dmv, yf, yb, xs, z, sk, nw):
        sg = _sigmoid(z)
        y = yf + yb + sk * xs
        yz = y * (z * sg)
        r = lax.rsqrt(jnp.mean(yz * yz, axis=-1, keepdims=True) + EPS)
        xh = yz * r
        dxh = dmv * nw
        dyz = r * (dxh - xh * jnp.mean(dxh * xh, axis=-1, keepdims=True))
        dy = dyz * (z * sg)
        dz = dyz * y * (sg * (1.0 + z * (1.0 - sg)))
        return dy, dz, jnp.sum(dmv * xh, axis=0, keepdims=True), jnp.sum(dy * xs, axis=0, keepdims=True)

    return _rowmap(fn, name=name, rows=rows, tile=tile, ncol=SSM_GROUPS,
                   ins=[(dm, 512, _colj), (y_f, 512, _colj), (y_b, 512, _colj), (xbc_act, 512, _colj),
                        (u_big, 512, _colj)],
                   consts=[(skip_x, 512, _colj), (ssm_norm, 512, _colj)],
                   outs=[(D_INNER, F32, 512, _colj), (D_INNER, BF16, 512, _colj)],
                   accs=[(D_INNER, 512, _colj), (D_INNER, 512, _colj)])


def _merge_fwd(pa, pb, u_big, *, name, tile=512):
    rows = pa.shape[0]

    def fn(a, b, ga, gb):
        return _sigmoid(ga) * a + _sigmoid(gb) * b

    return _rowmap(fn, name=name, rows=rows, tile=tile,
                   ins=[(pa, 1024, _col0), (pb, 1024, _col0), (u_big, 1024, lambda j: 5), (u_big, 1024, lambda j: 6)],
                   outs=[(1024, BF16, 1024, _col0)])[0]


def _merge_bwd(dmg, pa, pb, u_big, *, name, tile=512):
    rows = pa.shape[0]

    def fn(d, a, b, ga, gb):
        sa, sb = _sigmoid(ga), _sigmoid(gb)
        return d * sa, d * sb, d * a * sa * (1.0 - sa), d * b * sb * (1.0 - sb)

    return _rowmap(fn, name=name, rows=rows, tile=tile,
                   ins=[(dmg, 1024, _col0), (pa, 1024, _col0), (pb, 1024, _col0), (u_big, 1024, lambda j: 5),
                        (u_big, 1024, lambda j: 6)],
                   outs=[(1024, BF16, 1024, _col0)] * 4)


def _loss_bwd(y, target, *, name, tile=512):
    rows, d = y.shape

    def fn(yv, tv):
        err = yv - tv
        part = jnp.sum(jnp.sum(err * err, axis=-1, keepdims=True), axis=0, keepdims=True)
        return err * (1.0 / d), jnp.broadcast_to(part * (0.5 / d), (1, LANES))

    dy, part = _rowmap(fn, name=name, rows=rows, tile=tile, ins=[(y, d, _col0), (target, d, _col0)],
                       outs=[(d, F32, d, _col0)], accs=[(LANES, LANES, _col0)])
    return dy, part[0, 0]


def _small_slab(gs, dskip_ch, dalog_f, dalog_b, dbias_f, dbias_b, gkv, gqh, gkh, dconv_w, *, name):
    e_mat = _ssd_expand(False)
    full_names = ("ffn1_norm", "mix_norm", "q_a_norm", "conv_b", "ssm_norm", "ffn2_norm")
    full = [gs[n] for n in full_names]
    nf = len(full)

    def body(*refs):
        fulls = refs[:nf]
        (dsk_ref, e_ref, af_ref, ab_ref, bf_ref, bb_ref, gkv_ref, gqh_ref, gkh_ref, cw_ref, o_ref) = refs[nf:]
        o_ref[...] = jnp.zeros_like(o_ref)
        for n, r in zip(full_names, fulls):
            o_ref[SMALL_ROW[n]:SMALL_ROW[n] + 1, 0:r.shape[1]] = r[...]
        o_ref[SMALL_ROW["kv_a_norm"]:SMALL_ROW["kv_a_norm"] + 1, 0:KV_LORA] = gkv_ref[...]
        o_ref[SMALL_ROW["q_head_norm"]:SMALL_ROW["q_head_norm"] + 1, 0:LANES] = gqh_ref[...]
        o_ref[SMALL_ROW["k_head_norm"]:SMALL_ROW["k_head_norm"] + 1, 0:LANES] = gkh_ref[...]
        o_ref[SMALL_ROW["a_log_fwd"]:SMALL_ROW["a_log_fwd"] + 1, 0:LANES] = af_ref[...]
        o_ref[SMALL_ROW["a_log_bwd"]:SMALL_ROW["a_log_bwd"] + 1, 0:LANES] = pltpu.roll(ab_ref[...], 96, 1)
        o_ref[SMALL_ROW["dt_bias_fwd"]:SMALL_ROW["dt_bias_fwd"] + 1, 0:LANES] = bf_ref[...]
        o_ref[SMALL_ROW["dt_bias_bwd"]:SMALL_ROW["dt_bias_bwd"] + 1, 0:LANES] = pltpu.roll(bb_ref[...], 96, 1)
        dsk = _dot_h_nt(jnp.broadcast_to(dsk_ref[...], (8, D_INNER)), e_ref[...])
        o_ref[SMALL_ROW["d_skip"]:SMALL_ROW["d_skip"] + 1, 0:LANES] = dsk[0:1, :]
        o_ref[CONV_ROW:CONV_ROW + CONV_WIDTH, :] = cw_ref[...]

    return _pallas(body, name=name, out_shape=jax.ShapeDtypeStruct((SMALL_ROWS, SMALL_COLS), F32))(
        *full, dskip_ch, e_mat, dalog_f, dalog_b, dbias_f, dbias_b, gkv, gqh, gkh, dconv_w)


def _adamw_math(g, w, m, v):
    m2 = ADAM_B1 * m + (1.0 - ADAM_B1) * g
    v2 = ADAM_B2 * v + (1.0 - ADAM_B2) * (g * g)
    m_hat = m2 / (1.0 - ADAM_B1 ** ADAM_STEP)
    v_hat = v2 / (1.0 - ADAM_B2 ** ADAM_STEP)
    delta = -ADAM_LR * (m_hat / (jnp.sqrt(v_hat) + ADAM_EPS) + ADAM_WD * w)
    return delta, m2, v2


def _sum8(r_ref):
    g = r_ref[0].astype(F32)
    for s in range(1, N_DEV):
        g = g + r_ref[s].astype(F32)
    return g


def _reduce_adamw(recv, w, m, v, *, name, tile=256):
    _, R, C = recv.shape
    tile = _pick(R, tile) if R % LANES == 0 else R
    assert R % tile == 0

    def body(r_ref, w_ref, m_ref, v_ref, g_ref, d_ref, m2_ref, v2_ref):
        g = _sum8(r_ref)
        delta, m2, v2 = _adamw_math(g, w_ref[...], m_ref[...], v_ref[...])
        g_ref[...] = g
        d_ref[...] = delta
        m2_ref[...] = m2
        v2_ref[...] = v2

    blk = pl.BlockSpec((tile, C), lambda i: (i, 0))
    return _pallas(
        body, name=name, grid=(R // tile,),
        in_specs=[pl.BlockSpec((N_DEV, tile, C), lambda i: (0, i, 0)), blk, blk, blk], out_specs=[blk] * 4,
        out_shape=[jax.ShapeDtypeStruct((R, C), F32)] * 4, compiler_params=_params(("parallel",)),
    )(recv, w, m, v)


def _reduce_t_adamw(recv, w, m, v, *, name):
    R, cs = w.shape

    def body(r_ref, w_ref, m_ref, v_ref, g_ref, d_ref, m2_ref, v2_ref):
        g = _sum8(r_ref).T
        delta, m2, v2 = _adamw_math(g, w_ref[...], m_ref[...], v_ref[...])
        g_ref[...] = g
        d_ref[...] = delta
        m2_ref[...] = m2
        v2_ref[...] = v2

    return _pallas(body, name=name, out_shape=[jax.ShapeDtypeStruct((R, cs), F32)] * 4,
                   compiler_params=pltpu.CompilerParams(vmem_limit_bytes=VMEM_LIMIT))(recv, w, m, v)


def _reduce8(recv, *, name, tile):
    _, R, C = recv.shape

    def body(r_ref, g_ref):
        g_ref[...] = _sum8(r_ref)

    return _pallas(body, name=name, grid=(R // tile,),
                   in_specs=[pl.BlockSpec((N_DEV, tile, C), lambda i: (0, i, 0))],
                   out_specs=pl.BlockSpec((tile, C), lambda i: (i, 0)),
                   out_shape=jax.ShapeDtypeStruct((R, C), F32), compiler_params=_params(("parallel",)))(recv)


def _adamw(g, w, m, v, *, name, tile=256):
    R, C = w.shape

    def body(g_ref, w_ref, m_ref, v_ref, d_ref, m2_ref, v2_ref):
        delta, m2, v2 = _adamw_math(g_ref[...], w_ref[...], m_ref[...], v_ref[...])
        d_ref[...] = delta
        m2_ref[...] = m2
        v2_ref[...] = v2

    blk = pl.BlockSpec((R, tile), lambda i: (0, i))
    return _pallas(body, name=name, grid=(C // tile,), in_specs=[blk] * 4, out_specs=[blk] * 3,
                   out_shape=[jax.ShapeDtypeStruct((R, C), F32)] * 3, compiler_params=_params(("parallel",)))(g, w, m, v)


def _adamw_small(srecv, conv_g, ws, ms, vs, *, name):
    n = len(ws)

    def body(*refs):
        s_ref, c_ref = refs[0], refs[1]
        w_refs, m_refs, v_refs = refs[2:2 + n], refs[2 + n:2 + 2 * n], refs[2 + 2 * n:2 + 3 * n]
        outs = refs[2 + 3 * n:]
        gsum = _sum8(s_ref)
        for i in range(n):
            if i < len(SMALL):
                g = gsum[i:i + 1, 0:SMALL[i][1]]
            else:
                g = _sum8(c_ref)
            delta, m2, v2 = _adamw_math(g, w_refs[i][...], m_refs[i][...], v_refs[i][...])
            outs[i][...] = g
            outs[n + i][...] = delta
            outs[2 * n + i][...] = m2
            outs[3 * n + i][...] = v2

    shapes = [jax.ShapeDtypeStruct(w.shape, F32) for w in ws]
    got = _pallas(body, name=name, out_shape=shapes * 4,
                  compiler_params=pltpu.CompilerParams(vmem_limit_bytes=VMEM_LIMIT))(srecv, conv_g, *ws, *ms, *vs)
    return got[:n], got[n:2 * n], got[2 * n:3 * n], got[3 * n:]


def _ffn_fwd(x, norm, w_g_t, w_u_t, w_d, tag):
    h = _rms_fwd(x, norm, name=f"{tag}_rms")
    gu = _mm(h, jnp.concatenate([w_g_t, w_u_t], axis=0), name=f"{tag}_gu", tb=True, out_dtype=BF16)
    act = _swiglu_fwd(gu, name=f"{tag}_act")
    out = _mm(act, w_d, name=f"{tag}_down", alpha=0.5, res=x)
    return out, (h, gu, act)


def _ffn_bwd(dout, x, norm, w_g_t, w_u_t, w_d, saved, tag):
    h, gu, act = saved
    d_act = _mm(dout, w_d, name=f"{tag}_dact", tb=True, alpha=0.5, out_dtype=BF16)
    d_wd = _mm(act, dout, name=f"{tag}_dwd", ta=True, alpha=0.5, tm=1408, tn=1024, out_dtype=BF16)
    dg, du = _swiglu_bwd(d_act, gu, name=f"{tag}_dswiglu")
    d_wg_t = _mm(dg, h, name=f"{tag}_dwg", ta=True, tm=1408, tn=1024, out_dtype=BF16)
    d_wu_t = _mm(du, h, name=f"{tag}_dwu", ta=True, tm=1408, tn=1024, out_dtype=BF16)
    dh = _mm(dg, w_g_t, name=f"{tag}_dh_g")
    dh = _mm(du, w_u_t, name=f"{tag}_dh_u", res=dh)
    dx, dnorm = _rms_bwd(dh, x, norm, dout, name=f"{tag}_drms")
    return dx, dnorm, d_wg_t, d_wu_t, d_wd


def _rope_tables(positions, T):
    pos = positions.reshape(T).astype(F32)
    inv_freq = 1.0 / (ROPE_BASE ** (jnp.arange(0, QK_ROPE, 2, dtype=F32) / QK_ROPE))
    ang = pos[:, None] * inv_freq
    cos, sin = jnp.cos(ang), jnp.sin(ang)
    one64, z64 = jnp.ones((T, 64), F32), jnp.zeros((T, 64), F32)
    z16, z32, one32 = jnp.zeros((T, 16), F32), jnp.zeros((T, 32), F32), jnp.ones((T, 32), F32)
    c = jnp.concatenate([one64, cos, cos, one32], axis=1)
    s1 = jnp.concatenate([z64, -sin, z16, z32], axis=1)
    s2 = jnp.concatenate([z64, z16, sin, z32], axis=1)
    return c, s1, s2


def _cols(g):
    n, r, cs = g.shape
    return g.transpose(1, 0, 2).reshape(r, n * cs)


def _rows(g):
    n, rs, c = g.shape
    return g.reshape(n * rs, c)


def _pad_lanes(v, n=LANES):
    return jnp.pad(v, ((0, 0), (0, n - v.shape[1])))


def _in_proj_weights(w_in_t):
    z = lambda n: jnp.zeros((n, D_MODEL), w_in_t.dtype)
    w_small_t = jnp.concatenate([w_in_t[0:384], z(128), w_in_t[384:672], z(96), w_in_t[5792:5856], z(64)], axis=0)
    w_big_t = jnp.concatenate([w_in_t[672:5792], w_in_t[5856:7904]], axis=0)
    return w_small_t, w_big_t


def _mla_up_weights(w_q_b_t, w_kv_b):
    wq = w_q_b_t.reshape(N_HEADS, QK_HEAD, Q_LORA)
    wq = jnp.pad(wq, ((0, 0), (0, LANES - QK_HEAD), (0, 0))).reshape(N_HEADS * LANES, Q_LORA)
    wkv = w_kv_b.reshape(KV_LORA, N_HEADS, QK_NOPE + V_HEAD)
    wk = jnp.pad(wkv[..., :QK_NOPE], ((0, 0), (0, 0), (0, LANES - QK_NOPE))).reshape(KV_LORA, N_HEADS * LANES)
    v = wkv[..., QK_NOPE:]
    zv = jnp.zeros_like(v)
    even = (jnp.arange(N_HEADS) % 2 == 0)[None, :, None]
    wv = jnp.where(even, jnp.concatenate([v, zv], -1), jnp.concatenate([zv, v], -1)).reshape(KV_LORA, N_HEADS * LANES)
    return wq, wk, wv


def _shard_rows(g):
    return g.reshape(N_DEV, g.shape[0] // N_DEV, g.shape[1])


def kernel(x, positions, ffn1_norm, ffn1_w_gate, ffn1_w_up, ffn1_w_down, mix_norm, w_in, q_a_norm, w_q_b, kv_a_norm, w_kv_b, q_head_norm, k_head_norm, conv_w, conv_b, a_log_fwd, a_log_bwd, dt_bias_fwd, dt_bias_bwd, d_skip, ssm_norm, w_attn_branch, w_ssm_branch, w_out, ffn2_norm, ffn2_w_gate, ffn2_w_up, ffn2_w_down, loss_target, m_ffn1_norm, m_ffn1_w_gate, m_ffn1_w_up, m_ffn1_w_down, m_mix_norm, m_w_in, m_q_a_norm, m_w_q_b, m_kv_a_norm, m_w_kv_b, m_q_head_norm, m_k_head_norm, m_conv_w, m_conv_b, m_a_log_fwd, m_a_log_bwd, m_dt_bias_fwd, m_dt_bias_bwd, m_d_skip, m_ssm_norm, m_w_attn_branch, m_w_ssm_branch, m_w_out, m_ffn2_norm, m_ffn2_w_gate, m_ffn2_w_up, m_ffn2_w_down, v_ffn1_norm, v_ffn1_w_gate, v_ffn1_w_up, v_ffn1_w_down, v_mix_norm, v_w_in, v_q_a_norm, v_w_q_b, v_kv_a_norm, v_w_kv_b, v_q_head_norm, v_k_head_norm, v_conv_w, v_conv_b, v_a_log_fwd, v_a_log_bwd, v_dt_bias_fwd, v_dt_bias_bwd, v_d_skip, v_ssm_norm, v_w_attn_branch, v_w_ssm_branch, v_w_out, v_ffn2_norm, v_ffn2_w_gate, v_ffn2_w_up, v_ffn2_w_down):
    w_all = dict(ffn1_norm=ffn1_norm, ffn1_w_gate=ffn1_w_gate, ffn1_w_up=ffn1_w_up, ffn1_w_down=ffn1_w_down, mix_norm=mix_norm, w_in=w_in, q_a_norm=q_a_norm, w_q_b=w_q_b, kv_a_norm=kv_a_norm, w_kv_b=w_kv_b, q_head_norm=q_head_norm, k_head_norm=k_head_norm, conv_w=conv_w, conv_b=conv_b, a_log_fwd=a_log_fwd, a_log_bwd=a_log_bwd, dt_bias_fwd=dt_bias_fwd, dt_bias_bwd=dt_bias_bwd, d_skip=d_skip, ssm_norm=ssm_norm, w_attn_branch=w_attn_branch, w_ssm_branch=w_ssm_branch, w_out=w_out, ffn2_norm=ffn2_norm, ffn2_w_gate=ffn2_w_gate, ffn2_w_up=ffn2_w_up, ffn2_w_down=ffn2_w_down)
    m_all = dict(ffn1_norm=m_ffn1_norm, ffn1_w_gate=m_ffn1_w_gate, ffn1_w_up=m_ffn1_w_up, ffn1_w_down=m_ffn1_w_down, mix_norm=m_mix_norm, w_in=m_w_in, q_a_norm=m_q_a_norm, w_q_b=m_w_q_b, kv_a_norm=m_kv_a_norm, w_kv_b=m_w_kv_b, q_head_norm=m_q_head_norm, k_head_norm=m_k_head_norm, conv_w=m_conv_w, conv_b=m_conv_b, a_log_fwd=m_a_log_fwd, a_log_bwd=m_a_log_bwd, dt_bias_fwd=m_dt_bias_fwd, dt_bias_bwd=m_dt_bias_bwd, d_skip=m_d_skip, ssm_norm=m_ssm_norm, w_attn_branch=m_w_attn_branch, w_ssm_branch=m_w_ssm_branch, w_out=m_w_out, ffn2_norm=m_ffn2_norm, ffn2_w_gate=m_ffn2_w_gate, ffn2_w_up=m_ffn2_w_up, ffn2_w_down=m_ffn2_w_down)
    v_all = dict(ffn1_norm=v_ffn1_norm, ffn1_w_gate=v_ffn1_w_gate, ffn1_w_up=v_ffn1_w_up, ffn1_w_down=v_ffn1_w_down, mix_norm=v_mix_norm, w_in=v_w_in, q_a_norm=v_q_a_norm, w_q_b=v_w_q_b, kv_a_norm=v_kv_a_norm, w_kv_b=v_w_kv_b, q_head_norm=v_q_head_norm, k_head_norm=v_k_head_norm, conv_w=v_conv_w, conv_b=v_conv_b, a_log_fwd=v_a_log_fwd, a_log_bwd=v_a_log_bwd, dt_bias_fwd=v_dt_bias_fwd, dt_bias_bwd=v_dt_bias_bwd, d_skip=v_d_skip, ssm_norm=v_ssm_norm, w_attn_branch=v_w_attn_branch, w_ssm_branch=v_w_ssm_branch, w_out=v_w_out, ffn2_norm=v_ffn2_norm, ffn2_w_gate=v_ffn2_w_gate, ffn2_w_up=v_ffn2_w_up, ffn2_w_down=v_ffn2_w_down)
    T = x.shape[1]
    xs_in, target = x[0], loss_target[0]
    def two_d(n, a):
        if n in TRANSPOSED:
            return jnp.swapaxes(a, 1, 2).reshape(a.shape[2], a.shape[1])
        return a.reshape(-1, a.shape[-1])

    w2 = {n: two_d(n, a) for n, a in w_all.items()}
    m2 = {n: two_d(n, a) for n, a in m_all.items()}
    v2 = {n: two_d(n, a) for n, a in v_all.items()}
    p = {n: w2[n] for n, _ in SMALL}
    bf = lambda n: w2[n].astype(BF16)

    first = ["ffn1_w_gate", "ffn1_w_up", "ffn1_w_down"]
    ge = dict(zip(first, _all_gather_two_level([bf(n) for n in first], name="gather_ffn1")))
    mixw = ["w_in", "w_q_b", "w_kv_b", "conv_w"]
    mix_started, token = _exchange_start(["gather"] * len(mixw), [bf(n) for n in mixw[:3]] + [w2["conv_w"]],
                                         name="gather_mix_start")
    ffn1_norm_f = p["ffn1_norm"] + token[0:1, 0:1]
    w_g1t, w_u1t = _rows(ge["ffn1_w_gate"]), _rows(ge["ffn1_w_up"])
    w_d1 = _rows(ge["ffn1_w_down"])
    late = ["w_attn_branch", "w_ssm_branch", "w_out", "ffn2_w_gate", "ffn2_w_up", "ffn2_w_down"]
    late_shards = [bf(n) for n in late]

    tabs = _rope_tables(positions, T)
    qg, kg = _pad_lanes(p["q_head_norm"]), _pad_lanes(p["k_head_norm"])
    bias128 = _pad_lanes(jnp.concatenate([p["dt_bias_fwd"], p["dt_bias_bwd"]], axis=1))
    alog128 = _pad_lanes(jnp.concatenate([p["a_log_fwd"], p["a_log_bwd"]], axis=1))
    skip_x = jnp.repeat(p["d_skip"], 64, axis=1)

    x1, ffn1_saved = _ffn_fwd(xs_in, ffn1_norm_f, w_g1t, w_u1t, w_d1, "ffn1")
    h2 = _rms_fwd(x1, p["mix_norm"], name="mix_rms")
    ge.update(zip(mixw, _exchange_wait(["gather"] * len(mixw), mix_started, h2, name="gather_mix_wait")))
    w_small_t, w_big_t = _in_proj_weights(_rows(ge["w_in"]))
    wq_t, wk, wv = _mla_up_weights(_rows(ge["w_q_b"]), _cols(ge["w_kv_b"]))
    conv_full = _cols(ge["conv_w"])
    u_big = _mm(h2, w_big_t, name="in_big", tb=True)
    u_small = _mm(h2, w_small_t, name="in_small", tb=True)
    cqn, ckvn = _lora_norm_fwd(u_small, p["q_a_norm"], p["kv_a_norm"], name="lora_norm")
    q_raw = _mm(cqn, wq_t, name="q_up", tb=True)
    k_raw = _mm(ckvn, wk, name="k_up")
    v = _mm(ckvn, wv, name="v_up", out_dtype=BF16)
    q, k = _qk_prep_fwd(q_raw, k_raw, u_small, tabs, qg, kg, name="qk_prep")
    a_out, lse, g_late = _attn_fwd(q, k, v, ["gather"] * len(late), late_shards, name="attn_fwd")
    gl = dict(zip(late, g_late))
    w_pa, w_pb, w_o = _rows(gl["w_attn_branch"]), _rows(gl["w_ssm_branch"]), _rows(gl["w_out"])
    w_g2t, w_u2t = _rows(gl["ffn2_w_gate"]), _rows(gl["ffn2_w_up"])
    w_d2 = _rows(gl["ffn2_w_down"])
    xbc_act = _conv_fwd(u_big, conv_full, p["conv_b"], name="conv_fwd")
    y_f, hin_f = _ssd_fwd(xbc_act, u_small, bias128, alog128, rev=False, name="ssd_fwd_f")
    y_b, hin_b = _ssd_fwd(xbc_act, u_small, bias128, alog128, rev=True, name="ssd_fwd_b")
    m_out = _ssm_out_fwd(y_f, y_b, xbc_act, u_big, skip_x, p["ssm_norm"], name="ssm_out")
    pa = _mm(a_out, w_pa, name="branch_a")
    pb = _mm(m_out, w_pb, name="branch_b")
    merged = _merge_fwd(pa, pb, u_big, name="merge")
    x2 = _mm(merged, w_o, name="mix_out", res=x1)
    y, ffn2_saved = _ffn_fwd(x2, p["ffn2_norm"], w_g2t, w_u2t, w_d2, "ffn2")
    dy, loss_part = _loss_bwd(y, target, name="loss")
    loss = lax.psum(loss_part, ("x", "y", "c"))

    gs = {}
    dx2, gs["ffn2_norm"], g_gate2, g_up2, g_down2 = _ffn_bwd(dy, x2, p["ffn2_norm"], w_g2t, w_u2t, w_d2, ffn2_saved,
                                                             "ffn2b")
    dmerged = _mm(dx2, w_o, name="d_merged", tb=True)
    g_out = _mm(merged, dx2, name="d_w_out", ta=True, out_dtype=BF16)
    dpa, dpb, dga, dgb = _merge_bwd(dmerged, pa, pb, u_big, name="d_merge")
    g_pa = _mm(a_out, dpa, name="d_w_pa", ta=True, out_dtype=BF16)
    g_pb = _mm(m_out, dpb, name="d_w_pb", ta=True, out_dtype=BF16)
    da_out = _mm(dpa, w_pa, name="d_a", tb=True)
    dm_out = _mm(dpb, w_pb, name="d_m", tb=True)
    late_grads = [_shard_rows(g) for g in (g_pa, g_pb, g_out, g_gate2, g_up2, g_down2)]
    dq, dk, dv, r_late = _attn_bwd(q, k, v, a_out, lse, da_out, ["scatter"] * len(late_grads), late_grads,
                                   name="attn_bwd")
    recv = dict(zip(late, r_late))

    dyss, dz, gs["ssm_norm"], dskip_ch = _ssm_out_bwd(dm_out, y_f, y_b, xbc_act, u_big, skip_x, p["ssm_norm"],
                                                      name="d_ssm_out")
    dact_f, draw_f, dalog_f, dbias_f = _ssd_bwd(dyss, xbc_act, u_small, bias128, alog128, hin_f, skip_x,
                                                rev=False, name="ssd_bwd_f")
    dact_b, draw_b, dalog_b, dbias_b = _ssd_bwd(dyss, xbc_act, u_small, bias128, alog128, hin_b, None,
                                                rev=True, name="ssd_bwd_b")
    dxbc, g_conv, gs["conv_b"] = _conv_bwd(dact_f, dact_b, u_big, conv_full, p["conv_b"], name="conv_bwd")

    dq_raw, dk_raw, dkpe, gqh, gkh = _qk_prep_bwd(dq, dk, q_raw, k_raw, u_small, tabs, qg, kg, name="d_qk_prep")
    g_wq_t = _mm(dq_raw, cqn, name="d_w_q", ta=True, out_dtype=BF16)
    g_wk_t = _mm(dk_raw, ckvn, name="d_w_k", ta=True, out_dtype=BF16)
    g_wv_t = _mm(dv, ckvn, name="d_w_v", ta=True, out_dtype=BF16)
    dcqn = _mm(dq_raw, wq_t, name="d_cqn")
    dckvn = _mm(dk_raw, wk, name="d_ckvn_k", tb=True)
    dckvn = _mm(dv, wv, name="d_ckvn_v", tb=True, res=dckvn)
    du_small, gs["q_a_norm"], gkv = _lora_norm_bwd(dcqn, dckvn, u_small, p["q_a_norm"], p["kv_a_norm"], dkpe,
                                                   draw_f, draw_b, name="d_lora_norm")

    dh2 = _mm(du_small, w_small_t, name="d_h2_small")
    dh2 = _mm(dz, w_big_t, name="d_h2_z", b_row0=0, res=dh2)
    dh2 = _mm(dxbc, w_big_t, name="d_h2_xbc", b_row0=2048, res=dh2)
    dh2 = _mm(dga, w_big_t, name="d_h2_ga", b_row0=5120, res=dh2)
    dh2 = _mm(dgb, w_big_t, name="d_h2_gb", b_row0=6144, res=dh2)
    gt_small = _mm(du_small, h2, name="d_w_small", ta=True, out_dtype=BF16)
    gt_z = _mm(dz, h2, name="d_w_z", ta=True, out_dtype=BF16)
    gt_xbc = _mm(dxbc, h2, name="d_w_xbc", ta=True, out_dtype=BF16)
    gt_ga = _mm(dga, h2, name="d_w_ga", ta=True, out_dtype=BF16)
    gt_gb = _mm(dgb, h2, name="d_w_gb", ta=True, out_dtype=BF16)
    dx1, gs["mix_norm"] = _rms_bwd(dh2, x1, p["mix_norm"], dx2, name="d_mix_rms")

    gt_in = jnp.concatenate([gt_small[0:384], gt_small[U_CKV:U_KPE + QK_ROPE], gt_z, gt_xbc,
                             gt_small[U_DT:U_DT + 64], gt_ga, gt_gb], axis=0)
    gt_in = jnp.pad(gt_in.reshape(N_DEV, W_IN_SHARD, D_MODEL), ((0, 0), (0, W_IN_SHARD_PAD - W_IN_SHARD), (0, 0)))
    gt_q = g_wq_t.reshape(N_HEADS, LANES, Q_LORA)[:, :QK_HEAD].reshape(N_DEV, -1, Q_LORA)
    gk3 = g_wk_t.reshape(N_HEADS, LANES, KV_LORA)[:, :QK_NOPE]
    gv3 = g_wv_t.reshape(N_HEADS, LANES, KV_LORA)
    even = (jnp.arange(N_HEADS) % 2 == 0)[:, None, None]
    gv3 = jnp.where(even, gv3[:, :V_HEAD], gv3[:, V_HEAD:])
    gt_kv = jnp.concatenate([gk3, gv3], axis=1).reshape(N_DEV, -1, KV_LORA)
    mixg = ["w_in", "w_q_b", "w_kv_b"]
    grads_started, token = _exchange_start(["scatter"] * len(mixg), [gt_in, gt_q, gt_kv], name="grad_mix_start")
    grad_x, gs["ffn1_norm"], g_gate1, g_up1, g_down1 = _ffn_bwd(dx1, xs_in, p["ffn1_norm"] + token[0:1, 0:1], w_g1t,
                                                                w_u1t, w_d1, ffn1_saved, "ffn1b")
    recv.update(zip(mixg, _exchange_wait(["scatter"] * len(mixg), grads_started, grad_x, name="grad_mix_wait")))

    gsmall = _small_slab(gs, dskip_ch, dalog_f, dalog_b, dbias_f, dbias_b, gkv, gqh, gkh, g_conv, name="small_slab")
    last = ["ffn1_w_gate", "ffn1_w_up", "ffn1_w_down"]
    last_grads = [_shard_rows(g_gate1), _shard_rows(g_up1), _shard_rows(g_down1)]
    r_last = _exchange(["scatter"] * len(last) + ["gather"], last_grads + [gsmall], name="grad_exchange")
    recv.update(zip(last, r_last[:-1]))
    srecv = r_last[-1]

    out = {}
    for n in ("ffn1_w_down", "ffn2_w_down", "w_attn_branch", "w_ssm_branch", "w_out", "ffn1_w_gate", "ffn1_w_up",
              "ffn2_w_gate", "ffn2_w_up", "w_q_b"):
        out[n] = _reduce_adamw(recv[n], w2[n], m2[n], v2[n], name=f"adamw_{n}")
    out["w_kv_b"] = _reduce_t_adamw(recv["w_kv_b"], w2["w_kv_b"], m2["w_kv_b"], v2["w_kv_b"], name="adamw_w_kv_b")
    g_in = _reduce8(recv["w_in"], name="sum_w_in", tile=W_IN_SHARD_PAD // 2)[:W_IN_SHARD]
    out["w_in"] = [g_in] + list(_adamw(g_in, w2["w_in"], m2["w_in"], v2["w_in"], name="adamw_w_in"))
    me = 4 * lax.axis_index("x") + 2 * lax.axis_index("y") + lax.axis_index("c")
    conv_g = lax.dynamic_slice(srecv, (0, CONV_ROW, me * (XBC_DIM // N_DEV)), (N_DEV, CONV_WIDTH, XBC_DIM // N_DEV))
    sn = [n for n, _ in SMALL] + ["conv_w"]
    sg, sd, sm, sv = _adamw_small(srecv, conv_g, [w2[n] for n in sn], [m2[n] for n in sn], [v2[n] for n in sn],
                                  name="adamw_small")
    for i, n in enumerate(sn):
        out[n] = (sg[i], sd[i], sm[i], sv[i])
    def back(n, a):
        if n in TRANSPOSED:
            return jnp.swapaxes(a.reshape(1, a.shape[0], a.shape[1]), 1, 2)
        return a.reshape(w_all[n].shape)

    outs = [[back(n, out[n][kind]) for n in WEIGHT_ORDER] for kind in range(4)]
    return (loss, grad_x[None], *outs[0], *outs[1], *outs[2], *outs[3])
```

```python
import math

import jax
import jax.numpy as jnp
from jax import lax
from jax.experimental import pallas as pl
from jax.experimental.pallas import tpu as pltpu

F32, BF16 = jnp.float32, jnp.bfloat16
HIGHEST = lax.Precision.HIGHEST

D_MODEL, D_FF = 1024, 2816
EPS = 1e-6
N_HEADS, QK_NOPE, QK_ROPE, QK_HEAD, V_HEAD = 16, 64, 32, 96, 64
Q_LORA, KV_LORA = 384, 256
ROPE_BASE = 10000.0
D_INNER, SSM_HEADS, SSM_GROUPS, D_STATE, CONV_WIDTH, CHUNK = 2048, 32, 4, 128, 5, 128
XBC_DIM = D_INNER + 2 * SSM_GROUPS * D_STATE
IN_DIM = 7904
ADAM_LR, ADAM_B1, ADAM_B2, ADAM_EPS, ADAM_WD, ADAM_STEP = 0.001, 0.9, 0.999, 1e-08, 0.01, 10
N_DEV = 8

V7X_VMEM_BYTES = 64 * 1024 * 1024
VMEM_LIMIT = V7X_VMEM_BYTES - 8 * 1024 * 1024
LANES = 128
W_IN_SHARD = IN_DIM // N_DEV
W_IN_SHARD_PAD = 992

SMALL = (
    ("ffn1_norm", 1024), ("mix_norm", 1024), ("q_a_norm", 384), ("kv_a_norm", 256), ("q_head_norm", 96),
    ("k_head_norm", 96), ("conv_b", 3072), ("a_log_fwd", 32), ("a_log_bwd", 32), ("dt_bias_fwd", 32),
    ("dt_bias_bwd", 32), ("d_skip", 32), ("ssm_norm", 2048), ("ffn2_norm", 1024),
)
TRANSPOSED = ("ffn1_w_gate", "ffn1_w_up", "ffn2_w_gate", "ffn2_w_up", "w_in", "w_q_b")
SMALL_ROW = {n: i for i, (n, _) in enumerate(SMALL)}
CONV_ROW = len(SMALL)
SMALL_ROWS, SMALL_COLS = 24, XBC_DIM
WEIGHT_ORDER = (
    "ffn1_norm", "ffn1_w_gate", "ffn1_w_up", "ffn1_w_down", "mix_norm", "w_in", "q_a_norm", "w_q_b", "kv_a_norm",
    "w_kv_b", "q_head_norm", "k_head_norm", "conv_w", "conv_b", "a_log_fwd", "a_log_bwd", "dt_bias_fwd", "dt_bias_bwd",
    "d_skip", "ssm_norm", "w_attn_branch", "w_ssm_branch", "w_out", "ffn2_norm", "ffn2_w_gate", "ffn2_w_up",
    "ffn2_w_down",
)


def _pallas(body, **kw):
    return pl.pallas_call(body, **kw)


def _params(sem):
    return pltpu.CompilerParams(dimension_semantics=sem, vmem_limit_bytes=VMEM_LIMIT)


def _pick(dim, pref):
    if dim <= pref:
        return dim
    c = (pref // LANES) * LANES
    while c >= LANES:
        if dim % c == 0:
            return c
        c -= LANES
    raise ValueError((dim, pref))


def _sigmoid(x):
    return 1.0 / (1.0 + jnp.exp(-x))


def _softplus(x):
    return jnp.maximum(x, 0.0) + jnp.log(1.0 + jnp.exp(-jnp.abs(x)))


def _dot(a, b):
    return jnp.dot(a, b, preferred_element_type=F32)


def _dot_nt(a, b):
    return lax.dot_general(a, b, (((1,), (1,)), ((), ())), preferred_element_type=F32)


def _dot_tn(a, b):
    return lax.dot_general(a, b, (((0,), (0,)), ((), ())), preferred_element_type=F32)


def _dot_h(a, b):
    return jnp.dot(a, b, preferred_element_type=F32, precision=HIGHEST)


def _dot_h_nt(a, b):
    return lax.dot_general(a, b, (((1,), (1,)), ((), ())), preferred_element_type=F32, precision=HIGHEST)


def _dot_h_tn(a, b):
    return lax.dot_general(a, b, (((0,), (0,)), ((), ())), preferred_element_type=F32, precision=HIGHEST)


def _mesh_pos():
    return lax.axis_index("x"), lax.axis_index("y"), lax.axis_index("c")


def _comm_scratch(n):
    return [pltpu.SemaphoreType.DMA((7 * n,)), pltpu.SemaphoreType.DMA((7 * n,)), pltpu.SemaphoreType.DMA((n,))]


def _comm_copies(modes, srcs, dsts, send_sems, recv_sems, local_sems, arrivals):
    x, y, c = _mesh_pos()
    me = 4 * x + 2 * y + c
    local, remote = [], []
    for w, (mode, s, d) in enumerate(zip(modes, srcs, dsts)):
        gather = mode == "gather"
        if not arrivals and local_sems is not None:
            local.append(pltpu.make_async_copy(s if gather else s.at[me], d.at[me], local_sems.at[w]))
        for k in range(1, N_DEV):
            px = (1 - x) if (k & 4) else x
            py = (1 - y) if (k & 2) else y
            pc = (1 - c) if (k & 1) else c
            peer = 4 * px + 2 * py + pc
            idx = 7 * w + k - 1
            remote.append(pltpu.make_async_remote_copy(
                src_ref=s if gather else s.at[peer], dst_ref=d.at[peer] if arrivals else d.at[me],
                send_sem=send_sems.at[idx], recv_sem=recv_sems.at[idx],
                device_id=(px, py, pc), device_id_type=pl.DeviceIdType.MESH))
    return local, remote


def _comm_start(modes, srcs, dsts, sems):
    local, sends = _comm_copies(modes, srcs, dsts, *sems, arrivals=False)
    for cp in local + sends:
        cp.start()


def _comm_wait(modes, srcs, dsts, sems):
    _, recvs = _comm_copies(modes, srcs, dsts, *sems, arrivals=True)
    for cp in recvs:
        cp.wait_recv()
    local, sends = _comm_copies(modes, srcs, dsts, *sems, arrivals=False)
    for cp in sends:
        cp.wait_send()
    for cp in local:
        cp.wait()


def _comm_out_shapes(modes, arrays):
    return [jax.ShapeDtypeStruct((N_DEV,) + (a.shape if m == "gather" else a.shape[1:]), a.dtype)
            for m, a in zip(modes, arrays)]


def _exchange(modes, arrays, *, name):
    n = len(arrays)

    def body(*refs):
        srcs, dsts, sems = refs[:n], refs[n:2 * n], refs[2 * n:]
        _comm_start(modes, srcs, dsts, sems)
        _comm_wait(modes, srcs, dsts, sems)

    any_spec = pl.BlockSpec(memory_space=pl.ANY)
    return _pallas(body, name=name, out_shape=_comm_out_shapes(modes, arrays), in_specs=[any_spec] * n,
                   out_specs=[any_spec] * n, scratch_shapes=_comm_scratch(n))(*arrays)


def _exchange_start(modes, arrays, *, name):
    n = len(arrays)
    me = 4 * lax.axis_index("x") + 2 * lax.axis_index("y") + lax.axis_index("c")
    lands = []
    for m, a in zip(modes, arrays):
        own = a if m == "gather" else lax.dynamic_index_in_dim(a, me, 0, keepdims=False)
        zone = lax.empty((N_DEV,) + own.shape, a.dtype)
        lands.append(lax.dynamic_update_index_in_dim(zone, own, me, 0))

    def body(*refs):
        srcs, dsts = refs[:n], refs[n:2 * n]
        send_sems, recv_sems = refs[2 * n], refs[2 * n + 1]
        token = refs[-1]
        _, sends = _comm_copies(modes, srcs, dsts, send_sems, recv_sems, None, arrivals=False)
        for cp in sends:
            cp.start()
        token[...] = jnp.zeros_like(token)

    hbm = pl.BlockSpec(memory_space=pltpu.HBM)
    sem = pl.BlockSpec(memory_space=pltpu.SEMAPHORE)
    ins = [pltpu.with_memory_space_constraint(a, pltpu.HBM) for a in list(arrays) + lands]
    got = _pallas(
        body, name=name,
        out_shape=(pltpu.SemaphoreType.DMA((7 * n,)), pltpu.SemaphoreType.DMA((7 * n,)),
                   *[pltpu.HBM(a.shape, a.dtype) for a in ins], jax.ShapeDtypeStruct((8, LANES), F32)),
        in_specs=[hbm] * (2 * n), out_specs=(sem, sem, *[hbm] * (2 * n), pl.BlockSpec(memory_space=pltpu.VMEM)),
        input_output_aliases={i: 2 + i for i in range(2 * n)},
        compiler_params=pltpu.CompilerParams(has_side_effects=pltpu.SideEffectType.DATAFLOW_SIDE_EFFECTING),
    )(*ins)
    return (got[0], got[1], got[2:2 + n], got[2 + n:2 + 2 * n]), got[-1]


def _exchange_wait(modes, started, after, *, name):
    send_sems, recv_sems, srcs, lands = started
    n = len(srcs)

    def body(*refs):
        src_refs, dst_refs = refs[:n], refs[n:2 * n]
        ssem, rsem = refs[2 * n], refs[2 * n + 1]
        _, recvs = _comm_copies(modes, src_refs, dst_refs, ssem, rsem, None, arrivals=True)
        for cp in recvs:
            cp.wait_recv()
        _, sends = _comm_copies(modes, src_refs, dst_refs, ssem, rsem, None, arrivals=False)
        for cp in sends:
            cp.wait_send()

    hbm = pl.BlockSpec(memory_space=pltpu.HBM)
    sem = pl.BlockSpec(memory_space=pltpu.SEMAPHORE)
    both = list(srcs) + list(lands)
    got = _pallas(
        body, name=name, out_shape=tuple(pltpu.HBM(a.shape, a.dtype) for a in both),
        in_specs=[hbm] * (2 * n) + [sem, sem, pl.BlockSpec(memory_space=pl.ANY)], out_specs=tuple([hbm] * (2 * n)),
        input_output_aliases={i: i for i in range(2 * n)},
        compiler_params=pltpu.CompilerParams(has_side_effects=pltpu.SideEffectType.DATAFLOW_SIDE_EFFECTING),
    )(*both, send_sems, recv_sems, after)
    return got[n:]


def _all_gather_two_level(shards, *, name):
    n = len(shards)

    def body(*refs):
        srcs, outs = refs[:n], refs[n:2 * n]
        send_sems, recv_sems, local_sems = refs[2 * n:]
        x, y, c = _mesh_pos()
        me, sibling = (x, y, c), (x, y, 1 - c)
        chips = [(1 - x, y), (x, 1 - y), (1 - x, 1 - y)]

        def blk(w, px, py, pc):
            return outs[w].at[4 * px + 2 * py + pc]

        def copy(w, k, block, to, src=None):
            return pltpu.make_async_remote_copy(
                src_ref=blk(w, *block) if src is None else src, dst_ref=blk(w, *block),
                send_sem=send_sems.at[7 * w + k], recv_sem=recv_sems.at[7 * w + k], device_id=to,
                device_id_type=pl.DeviceIdType.MESH)

        mine = [pltpu.make_async_copy(srcs[w], blk(w, *me), local_sems.at[w]) for w in range(n)]
        for cp in mine:
            cp.start()
        first = []
        for w in range(n):
            first.append(copy(w, 0, me, sibling, src=srcs[w]))
            first += [copy(w, 1 + j, me, (*chip, c), src=srcs[w]) for j, chip in enumerate(chips)]
        for cp in first:
            cp.start()
        passed = []
        for w in range(n):
            for j, chip in enumerate(chips):
                copy(w, 1 + j, (*chip, c), me).wait_recv()
                fwd = copy(w, 4 + j, (*chip, c), sibling)
                fwd.start()
                passed.append(fwd)
        for w in range(n):
            copy(w, 0, sibling, me).wait_recv()
            for j, chip in enumerate(chips):
                copy(w, 4 + j, (*chip, 1 - c), me).wait_recv()
        for cp in first + passed:
            cp.wait_send()
        for cp in mine:
            cp.wait()

    any_spec = pl.BlockSpec(memory_space=pl.ANY)
    return _pallas(body, name=name, out_shape=_comm_out_shapes(["gather"] * n, shards), in_specs=[any_spec] * n,
                   out_specs=[any_spec] * n, scratch_shapes=_comm_scratch(n))(*shards)


def _mm(a, b, *, name, ta=False, tb=False, out_dtype=F32, alpha=1.0, res=None, tm=1024, tn=1408, tk=1408,
        b_row0=None, after=None):
    (K, M) = a.shape if ta else a.shape[::-1]
    (N, Kb) = b.shape if tb else b.shape[::-1]
    tm, tn, tk = _pick(M, tm), _pick(N, tn), _pick(K, tk)
    nk = K // tk
    if b_row0 is None:
        assert K == Kb, (a.shape, b.shape, ta, tb)
        kb0 = 0
    else:
        assert not tb and b_row0 % tk == 0 and b_row0 + K <= Kb, (a.shape, b.shape, b_row0)
        kb0 = b_row0 // tk
    a_spec = pl.BlockSpec((tk, tm), lambda i, j, k: (k, i)) if ta else pl.BlockSpec((tm, tk), lambda i, j, k: (i, k))
    b_spec = (pl.BlockSpec((tn, tk), lambda i, j, k: (j, k)) if tb
              else pl.BlockSpec((tk, tn), lambda i, j, k: (k + kb0, j)))
    o_spec = pl.BlockSpec((tm, tn), lambda i, j, k: (i, j))
    dn = (((0 if ta else 1,), (1 if tb else 0,)), ((), ()))
    has_res = res is not None
    n_in = 2 + has_res + (after is not None)

    def body(*refs):
        a_ref, b_ref = refs[0], refs[1]
        r_ref = refs[2] if has_res else None
        o_ref = refs[n_in]
        part =lax.dot_general(a_ref[...].astype(BF16), b_ref[...].astype(BF16), dn, preferred_element_type=F32)

        def finish(acc):
            if alpha != 1.0:
                acc = acc * alpha
            if has_res:
                acc = acc + r_ref[...]
            o_ref[...] = acc.astype(o_ref.dtype)

        if nk == 1:
            finish(part)
        else:
            acc_ref = refs[-1]
            k = pl.program_id(2)

            @pl.when(k == 0)
            def _():
                acc_ref[...] = part

            @pl.when(k > 0)
            def _():
                acc_ref[...] += part

            @pl.when(k == nk - 1)
            def _():
                finish(acc_ref[...])

    ins = [a, b] + ([res] if has_res else [])
    in_specs = [a_spec, b_spec] + ([o_spec] if has_res else [])
    if after is not None:
        ins.append(after)
        in_specs.append(pl.BlockSpec(after.shape, lambda i, j, k: (0, 0)))
    return _pallas(
        body, name=name, grid=(M // tm, N // tn, nk), in_specs=in_specs, out_specs=o_spec,
        out_shape=jax.ShapeDtypeStruct((M, N), out_dtype),
        scratch_shapes=[pltpu.VMEM((tm, tn), F32)] if nk > 1 else [],
        compiler_params=_params(("parallel", "parallel", "arbitrary")),
    )(*ins)


def _col0(j):
    return 0


def _colj(j):
    return j


def _rowmap(fn, *, name, rows, tile, ins, consts=(), outs=(), accs=(), ncol=1):
    tile = min(tile, rows)
    nrow = rows // tile
    in_specs = [pl.BlockSpec((tile, w), lambda j, i, f=f: (i, f(j))) for _, w, f in ins]
    for arr, w, f in consts:
        in_specs.append(pl.BlockSpec((arr.shape[0], w), lambda j, i, f=f: (0, f(j))))
    out_specs = [pl.BlockSpec((tile, w), lambda j, i, f=f: (i, f(j))) for _, _, w, f in outs]
    out_specs += [pl.BlockSpec((1, w), lambda j, i, f=f: (0, f(j))) for _, w, f in accs]
    out_shape = [jax.ShapeDtypeStruct((rows, c), dt) for c, dt, _, _ in outs]
    out_shape += [jax.ShapeDtypeStruct((1, c), F32) for c, _, _ in accs]
    n_in, n_out = len(ins) + len(consts), len(outs)
    acc_fixed = [f is _col0 for _, _, f in accs]

    def body(*refs):
        res = fn(*[r[...].astype(F32) for r in refs[:n_in]])
        if not isinstance(res, (tuple, list)):
            res = (res,)
        for r, v in zip(refs[n_in:n_in + n_out], res[:n_out]):
            r[...] = v.astype(r.dtype)
        j, i = pl.program_id(0), pl.program_id(1)
        for r, v, fixed in zip(refs[n_in + n_out:], res[n_out:], acc_fixed):
            first = ((i == 0) & (j == 0)) if fixed else (i == 0)

            @pl.when(first)
            def _(r=r, v=v):
                r[...] = v

            @pl.when(jnp.logical_not(first))
            def _(r=r, v=v):
                r[...] += v

    arrays = [a for a, _, _ in ins] + [a for a, _, _ in consts]
    return _pallas(
        body, name=name, grid=(ncol, nrow), in_specs=in_specs, out_specs=out_specs, out_shape=out_shape,
        compiler_params=_params(("arbitrary", "arbitrary")),
    )(*arrays)


def _rms_fwd(x, g, *, name, tile=512):
    rows, d = x.shape

    def fn(xv, gv):
        r = lax.rsqrt(jnp.mean(xv * xv, axis=-1, keepdims=True) + EPS)
        return xv * r * gv

    return _rowmap(fn, name=name, rows=rows, tile=tile, ins=[(x, d, _col0)], consts=[(g, d, _col0)],
                   outs=[(d, BF16, d, _col0)])[0]


def _rms_bwd(dh, x, g, res, *, name, tile=512):
    rows, d = x.shape

    def fn(dhv, xv, rv, gv):
        r = lax.rsqrt(jnp.mean(xv * xv, axis=-1, keepdims=True) + EPS)
        xh = xv * r
        dxh = dhv * gv
        dx = r * (dxh - xh * jnp.mean(dxh * xh, axis=-1, keepdims=True))
        return rv + dx, jnp.sum(dhv * xh, axis=0, keepdims=True)

    return _rowmap(fn, name=name, rows=rows, tile=tile, ins=[(dh, d, _col0), (x, d, _col0), (res, d, _col0)],
                   consts=[(g, d, _col0)], outs=[(d, F32, d, _col0)], accs=[(d, d, _col0)])


def _swiglu_fwd(gu, *, name, tile=512):
    rows = gu.shape[0]
    w = _pick(D_FF, 1408)
    nb = D_FF // w

    def fn(gv, uv):
        return gv * _sigmoid(gv) * uv

    return _rowmap(fn, name=name, rows=rows, tile=tile, ncol=nb,
                   ins=[(gu, w, _colj), (gu, w, lambda j: j + nb)], outs=[(D_FF, BF16, w, _colj)])[0]


def _swiglu_bwd(da, gu, *, name, tile=512):
    rows = gu.shape[0]
    w = _pick(D_FF, 1408)
    nb = D_FF // w

    def fn(dav, gv, uv):
        sg = _sigmoid(gv)
        dg = dav * uv * (sg * (1.0 + gv * (1.0 - sg)))
        du = dav * (gv * sg)
        return dg, du

    return _rowmap(fn, name=name, rows=rows, tile=tile, ncol=nb,
                   ins=[(da, w, _colj), (gu, w, _colj), (gu, w, lambda j: j + nb)],
                   outs=[(D_FF, BF16, w, _colj), (D_FF, BF16, w, _colj)])


U_CKV, U_KPE, U_DT = 512, 768, 896


def _lora_norm_fwd(u_small, qg, kvg, *, name, tile=512):
    rows = u_small.shape[0]

    def fn(cq, ckv, qgv, kgv):
        rq = lax.rsqrt(jnp.mean(cq * cq, axis=-1, keepdims=True) + EPS)
        rk = lax.rsqrt(jnp.mean(ckv * ckv, axis=-1, keepdims=True) + EPS)
        return cq * rq * qgv, ckv * rk * kgv

    return _rowmap(fn, name=name, rows=rows, tile=tile,
                   ins=[(u_small, Q_LORA, _col0), (u_small, KV_LORA, lambda j: U_CKV // KV_LORA)],
                   consts=[(qg, Q_LORA, _col0), (kvg, KV_LORA, _col0)],
                   outs=[(Q_LORA, BF16, Q_LORA, _col0), (KV_LORA, BF16, KV_LORA, _col0)])


def _lora_norm_bwd(dcqn, dckvn, u_small, qg, kvg, dkpe, draw_f, draw_b, *, name, tile=512):
    rows = u_small.shape[0]
    tile = min(tile, rows)

    def body(dq_ref, dk_ref, u_ref, dkp_ref, df_ref, db_ref, qg_ref, kg_ref, du_ref, gq_ref, gk_ref):
        cq, ckv = u_ref[:, 0:Q_LORA], u_ref[:, U_CKV:U_CKV + KV_LORA]
        dq, dk = dq_ref[...], dk_ref[...]
        rq = lax.rsqrt(jnp.mean(cq * cq, axis=-1, keepdims=True) + EPS)
        xh = cq * rq
        dxh = dq * qg_ref[...]
        du_ref[:, 0:Q_LORA] = (rq * (dxh - xh * jnp.mean(dxh * xh, axis=-1, keepdims=True))).astype(BF16)
        du_ref[:, Q_LORA:U_CKV] = jnp.zeros((tile, U_CKV - Q_LORA), BF16)
        rk = lax.rsqrt(jnp.mean(ckv * ckv, axis=-1, keepdims=True) + EPS)
        kh = ckv * rk
        dkh = dk * kg_ref[...]
        du_ref[:, U_CKV:U_KPE] = (rk * (dkh - kh * jnp.mean(dkh * kh, axis=-1, keepdims=True))).astype(BF16)
        du_ref[:, U_KPE:U_DT] = dkp_ref[...].astype(BF16)
        du_ref[:, U_DT:U_DT + LANES] = (df_ref[...] + db_ref[...]).astype(BF16)
        gq = jnp.sum(dq * xh, axis=0, keepdims=True)
        gk = jnp.sum(dk * kh, axis=0, keepdims=True)
        i = pl.program_id(0)

        @pl.when(i == 0)
        def _():
            gq_ref[...] = gq
            gk_ref[...] = gk

        @pl.when(i > 0)
        def _():
            gq_ref[...] += gq
            gk_ref[...] += gk

    def rowblk(w):
        return pl.BlockSpec((tile, w), lambda i: (i, 0))

    def whole(w):
        return pl.BlockSpec((1, w), lambda i: (0, 0))

    return _pallas(
        body, name=name, grid=(rows // tile,),
        in_specs=[rowblk(Q_LORA), rowblk(KV_LORA), rowblk(1024), rowblk(LANES), rowblk(LANES), rowblk(LANES),
                  whole(Q_LORA), whole(KV_LORA)],
        out_specs=[rowblk(1024), whole(Q_LORA), whole(KV_LORA)],
        out_shape=[jax.ShapeDtypeStruct((rows, 1024), BF16), jax.ShapeDtypeStruct((1, Q_LORA), F32),
                   jax.ShapeDtypeStruct((1, KV_LORA), F32)],
        compiler_params=_params(("arbitrary",)),
    )(dcqn, dckvn, u_small, dkpe, draw_f, draw_b, qg, kvg)


def _rope(x, c, s1, s2):
    return x * c + pltpu.roll(x, 112, 1) * s1 + pltpu.roll(x, 16, 1) * s2


def _rope_t(d, c, s1, s2):
    return d * c + pltpu.roll(d * s1, 16, 1) + pltpu.roll(d * s2, 112, 1)


def _qk_prep_fwd(q_raw, k_raw, u_small, tabs, qg, kg, *, name, tile=256):
    rows = q_raw.shape[0]
    tile = min(tile, rows)
    scale = 1.0 / math.sqrt(QK_HEAD)

    def body(q_ref, k_ref, u_ref, c_ref, s1_ref, s2_ref, qg_ref, kg_ref, qo_ref, ko_ref):
        c, s1, s2 = c_ref[...], s1_ref[...], s2_ref[...]
        qgv, kgv = qg_ref[...], kg_ref[...]
        kpe = pltpu.roll(u_ref[:, U_KPE:U_KPE + LANES], 64, 1)
        for h in range(N_HEADS):
            hs = slice(h * LANES, (h + 1) * LANES)
            qr = q_ref[:, hs]
            rq = lax.rsqrt(jnp.sum(qr * qr, axis=-1, keepdims=True) / QK_HEAD + EPS)
            qo_ref[:, hs] = (_rope(qr * rq * qgv, c, s1, s2) * scale).astype(BF16)
            xk = k_ref[:, hs] + kpe
            rk = lax.rsqrt(jnp.sum(xk * xk, axis=-1, keepdims=True) / QK_HEAD + EPS)
            ko_ref[:, hs] = _rope(xk * rk * kgv, c, s1, s2).astype(BF16)

    wide = pl.BlockSpec((tile, 2048), lambda i: (i, 0))
    narrow = pl.BlockSpec((tile, LANES), lambda i: (i, 0))
    gain = pl.BlockSpec((1, LANES), lambda i: (0, 0))
    return _pallas(
        body, name=name, grid=(rows // tile,),
        in_specs=[wide, wide, pl.BlockSpec((tile, 1024), lambda i: (i, 0)), narrow, narrow, narrow, gain, gain],
        out_specs=[wide, wide], out_shape=[jax.ShapeDtypeStruct((rows, 2048), BF16)] * 2,
        compiler_params=_params(("parallel",)),
    )(q_raw, k_raw, u_small, *tabs, qg, kg)


def _qk_prep_bwd(dq, dk, q_raw, k_raw, u_small, tabs, qg, kg, *, name, tile=256):
    rows = q_raw.shape[0]
    tile = min(tile, rows)
    scale = 1.0 / math.sqrt(QK_HEAD)

    def body(dq_ref, dk_ref, q_ref, k_ref, u_ref, c_ref, s1_ref, s2_ref, qg_ref, kg_ref,
             dqo_ref, dko_ref, dkpe_ref, gq_ref, gk_ref):
        c, s1, s2 = c_ref[...], s1_ref[...], s2_ref[...]
        qgv, kgv = qg_ref[...], kg_ref[...]
        kpe = pltpu.roll(u_ref[:, U_KPE:U_KPE + LANES], 64, 1)
        lane = lax.broadcasted_iota(jnp.int32, (tile, LANES), 1)
        gq = jnp.zeros((1, LANES), F32)
        gk = jnp.zeros((1, LANES), F32)
        dkpe = jnp.zeros((tile, LANES), F32)
        for h in range(N_HEADS):
            hs = slice(h * LANES, (h + 1) * LANES)
            qr = q_ref[:, hs]
            rq = lax.rsqrt(jnp.sum(qr * qr, axis=-1, keepdims=True) / QK_HEAD + EPS)
            xh = qr * rq
            dy = _rope_t(dq_ref[:, hs] * scale, c, s1, s2)
            dxh = dy * qgv
            dqo_ref[:, hs] = (rq * (dxh - xh * (jnp.sum(dxh * xh, axis=-1, keepdims=True) / QK_HEAD))).astype(BF16)
            gq = gq + jnp.sum(dy * xh, axis=0, keepdims=True)
            xk = k_ref[:, hs] + kpe
            rk = lax.rsqrt(jnp.sum(xk * xk, axis=-1, keepdims=True) / QK_HEAD + EPS)
            kh = xk * rk
            dyk = _rope_t(dk_ref[:, hs], c, s1, s2)
            dkh = dyk * kgv
            dxk = rk * (dkh - kh * (jnp.sum(dkh * kh, axis=-1, keepdims=True) / QK_HEAD))
            gk = gk + jnp.sum(dyk * kh, axis=0, keepdims=True)
            dko_ref[:, hs] = jnp.where(lane < QK_NOPE, dxk, 0.0).astype(BF16)
            dkpe = dkpe + dxk
        dkpe_ref[...] = jnp.where(lane < QK_ROPE, pltpu.roll(dkpe, 64, 1), 0.0)
        i = pl.program_id(0)

        @pl.when(i == 0)
        def _():
            gq_ref[...] = gq
            gk_ref[...] = gk

        @pl.when(i > 0)
        def _():
            gq_ref[...] += gq
            gk_ref[...] += gk

    wide = pl.BlockSpec((tile, 2048), lambda i: (i, 0))
    narrow = pl.BlockSpec((tile, LANES), lambda i: (i, 0))
    gain = pl.BlockSpec((1, LANES), lambda i: (0, 0))
    return _pallas(
        body, name=name, grid=(rows // tile,),
        in_specs=[wide, wide, wide, wide, pl.BlockSpec((tile, 1024), lambda i: (i, 0)), narrow, narrow, narrow,
                  gain, gain],
        out_specs=[wide, wide, narrow, gain, gain],
        out_shape=[jax.ShapeDtypeStruct((rows, 2048), BF16)] * 2
        + [jax.ShapeDtypeStruct((rows, LANES), F32), jax.ShapeDtypeStruct((1, LANES), F32),
           jax.ShapeDtypeStruct((1, LANES), F32)],
        compiler_params=_params(("arbitrary",)),
    )(dq, dk, q_raw, k_raw, u_small, *tabs, qg, kg)


def _attn_fwd(q, k, v, comm_modes, comm_arrays, *, name, tq=512, tkc=512):
    T = q.shape[0]
    tq = min(tq, T)
    tkc = min(tkc, T)
    n = len(comm_arrays)
    nj, ni = N_HEADS // 2, T // tq

    def body(*refs):
        q_ref, k_ref, v_ref = refs[:3]
        srcs = refs[3:3 + n]
        o_ref, lse_ref = refs[3 + n:5 + n]
        dsts = refs[5 + n:5 + 2 * n]
        sems = refs[5 + 2 * n:]
        j, i = pl.program_id(0), pl.program_id(1)

        @pl.when((j == 0) & (i == 0))
        def _():
            _comm_start(comm_modes, srcs, dsts, sems)

        out = None
        for hh in range(2):
            sl = slice(hh * LANES, (hh + 1) * LANES)
            qv = q_ref[:, sl]
            m = l = acc = None
            for kc in range(T // tkc):
                ks = slice(kc * tkc, (kc + 1) * tkc)
                s = _dot_nt(qv, k_ref[ks, sl])
                mc = jnp.max(s, axis=-1, keepdims=True)
                if m is None:
                    m = mc
                    p = jnp.exp(s - m)
                    l = jnp.sum(p, axis=-1, keepdims=True)
                    acc = _dot(p.astype(BF16), v_ref[ks, sl])
                else:
                    m_new = jnp.maximum(m, mc)
                    alpha = jnp.exp(m - m_new)
                    p = jnp.exp(s - m_new)
                    l = alpha * l + jnp.sum(p, axis=-1, keepdims=True)
                    acc = alpha * acc + _dot(p.astype(BF16), v_ref[ks, sl])
                    m = m_new
            o = acc / l
            out = o if out is None else out + o
            lse_ref[hh] = m + jnp.log(l)
        o_ref[...] = out

        @pl.when((j == nj - 1) & (i == ni - 1))
        def _():
            _comm_wait(comm_modes, srcs, dsts, sems)

    any_spec = pl.BlockSpec(memory_space=pl.ANY)
    got = _pallas(
        body, name=name, grid=(nj, ni),
        in_specs=[pl.BlockSpec((tq, 2 * LANES), lambda j, i: (i, j)), pl.BlockSpec((T, 2 * LANES), lambda j, i: (0, j)),
                  pl.BlockSpec((T, 2 * LANES), lambda j, i: (0, j))] + [any_spec] * n,
        out_specs=[pl.BlockSpec((tq, LANES), lambda j, i: (i, j)), pl.BlockSpec((2, tq, 1), lambda j, i: (j, i, 0))]
        + [any_spec] * n,
        out_shape=[jax.ShapeDtypeStruct((T, N_HEADS * V_HEAD), F32), jax.ShapeDtypeStruct((N_HEADS, T, 1), F32)]
        + _comm_out_shapes(comm_modes, comm_arrays),
        scratch_shapes=_comm_scratch(n),
        compiler_params=_params(("arbitrary", "arbitrary")),
    )(q, k, v, *comm_arrays)
    return got[0], got[1], got[2:]


def _attn_bwd(q, k, v, o, lse, do, comm_modes, comm_arrays, *, name, tk=256, tqc=4096):
    T = q.shape[0]
    tk = min(tk, T)
    tqc = min(tqc, T)
    n = len(comm_arrays)
    nj, nkb = N_HEADS // 2, T // tk

    def body(*refs):
        q_ref, k_ref, v_ref, o_ref, lse_ref, do_ref = refs[:6]
        srcs = refs[6:6 + n]
        dq_ref, dk_ref, dv_ref = refs[6 + n:9 + n]
        dsts = refs[9 + n:9 + 2 * n]
        d_s = refs[9 + 2 * n]
        sems = refs[10 + 2 * n:]
        j, kb = pl.program_id(0), pl.program_id(1)

        @pl.when((j == 0) & (kb == 0))
        def _():
            _comm_start(comm_modes, srcs, dsts, sems)

        lane = lax.broadcasted_iota(jnp.int32, (1, LANES), 1)
        @pl.when(kb == 0)
        def _():
            prod = do_ref[...] * o_ref[...]
            for hh in range(2):
                keep = (lane < V_HEAD) if hh == 0 else (lane >= V_HEAD)
                d_s[hh] = jnp.sum(jnp.where(keep, prod, 0.0), axis=-1, keepdims=True)

        for hh in range(2):
            sl = slice(hh * LANES, (hh + 1) * LANES)
            keep = (lane < V_HEAD) if hh == 0 else (lane >= V_HEAD)
            kv, vv = k_ref[:, sl], v_ref[:, sl]
            dv_acc = dk_acc = None
            for qc in range(T // tqc):
                qs = slice(qc * tqc, (qc + 1) * tqc)
                qv = q_ref[qs, sl]
                do_b = do_ref[qs, :].astype(BF16)
                s = _dot_nt(qv, kv)
                p = jnp.exp(s - lse_ref[hh, qs])
                dp = _dot_nt(do_b, vv)
                ds = (p * (dp - d_s[hh, qs])).astype(BF16)
                dvc = _dot_tn(p.astype(BF16), do_b)
                dkc = _dot_tn(ds, qv)
                dv_acc = dvc if dv_acc is None else dv_acc + dvc
                dk_acc = dkc if dk_acc is None else dk_acc + dkc
                dqp = _dot(ds, kv)

                @pl.when(kb == 0)
                def _(dqp=dqp, sl=sl, qs=qs):
                    dq_ref[qs, sl] = dqp

                @pl.when(kb > 0)
                def _(dqp=dqp, sl=sl, qs=qs):
                    dq_ref[qs, sl] += dqp

            dv_ref[:, sl] = jnp.where(keep, dv_acc, 0.0).astype(BF16)
            dk_ref[:, sl] = dk_acc

        @pl.when((j == nj - 1) & (kb == nkb - 1))
        def _():
            _comm_wait(comm_modes, srcs, dsts, sems)

    any_spec = pl.BlockSpec(memory_space=pl.ANY)
    pair = pl.BlockSpec((T, 2 * LANES), lambda j, kb: (0, j))
    kblk = pl.BlockSpec((tk, 2 * LANES), lambda j, kb: (kb, j))
    got = _pallas(
        body, name=name, grid=(nj, nkb),
        in_specs=[pair, kblk, kblk, pl.BlockSpec((T, LANES), lambda j, kb: (0, j)),
                  pl.BlockSpec((2, T, 1), lambda j, kb: (j, 0, 0)), pl.BlockSpec((T, LANES), lambda j, kb: (0, j))]
        + [any_spec] * n,
        out_specs=[pair, kblk, kblk] + [any_spec] * n,
        out_shape=[jax.ShapeDtypeStruct((T, 2048), F32)] * 2 + [jax.ShapeDtypeStruct((T, 2048), BF16)]
        + _comm_out_shapes(comm_modes, comm_arrays),
        scratch_shapes=[pltpu.VMEM((2, T, 1), F32)] + _comm_scratch(n),
        compiler_params=_params(("arbitrary", "arbitrary")),
    )(q, k, v, o, lse, do, *comm_arrays)
    return got[0], got[1], got[2], got[3:]


CONV_ROWS, CONV_HALO = 64, 8
CONV_WIN = CONV_ROWS + 2 * CONV_HALO


def _conv_shift(x, sh, t_idx, total):
    if sh == 0:
        return x
    y = pltpu.roll(x, (-sh) % x.shape[0], 0)
    if t_idx is None:
        return y
    ok = (t_idx + sh >= 0) & (t_idx + sh < total)
    return jnp.where(ok, y, 0.0)


def _conv_positions(ws, shape):
    return ws + lax.broadcasted_iota(jnp.int32, shape, 0) if isinstance(ws, int) else None


def _aligned(v, m):
    return v if isinstance(v, int) else pl.multiple_of(v, m)


def _conv_chunks(T, chunk, carry):
    n = T // CONV_ROWS
    carry = chunk(0, 0, carry)

    def mid(ci, c):
        return chunk(pl.multiple_of(ci * CONV_ROWS - CONV_HALO, CONV_HALO), CONV_HALO, c)

    carry = lax.fori_loop(1, n - 1, mid, carry)
    return chunk(T - CONV_WIN, 2 * CONV_HALO, carry)


def _conv_pre(x, w_ref, b_ref, t_idx, total):
    pre = b_ref[...] + w_ref[2:3, :] * x
    for j in (0, 1, 3, 4):
        pre = pre + w_ref[j:j + 1, :] * _conv_shift(x, j - 2, t_idx, total)
    return pre


def _conv_fwd(u_big, conv_w, conv_b, *, name, w=256):
    T = u_big.shape[0]
    first = D_INNER // w

    def body(x_ref, w_ref, b_ref, o_ref):
        def chunk(ws, off, carry):
            x = x_ref[pl.ds(ws, CONV_WIN), :]
            pre = _conv_pre(x, w_ref, b_ref, _conv_positions(ws, x.shape), T)
            act = pre * _sigmoid(pre)
            o_ref[pl.ds(_aligned(ws + off, CONV_ROWS), CONV_ROWS), :] = act[off:off + CONV_ROWS]
            return carry

        _conv_chunks(T, chunk, 0)

    return _pallas(
        body, name=name, grid=(XBC_DIM // w,),
        in_specs=[pl.BlockSpec((T, w), lambda j: (0, j + first)), pl.BlockSpec((CONV_WIDTH, w), lambda j: (0, j)),
                  pl.BlockSpec((1, w), lambda j: (0, j))],
        out_specs=pl.BlockSpec((T, w), lambda j: (0, j)),
        out_shape=jax.ShapeDtypeStruct((T, XBC_DIM), F32),
        compiler_params=_params(("parallel",)),
    )(u_big, conv_w, conv_b)


def _conv_bwd(dact_f, dact_b, u_big, conv_w, conv_b, *, name, w=128):
    T = u_big.shape[0]
    first = D_INNER // w

    def body(df_ref, db_ref, x_ref, w_ref, b_ref, dx_ref, dw_ref, dbias_ref):
        def chunk(ws, off, sums):
            rows = pl.ds(ws, CONV_WIN)
            x = x_ref[rows, :]
            row = lax.broadcasted_iota(jnp.int32, x.shape, 0)
            t_idx = _conv_positions(ws, x.shape)
            pre = _conv_pre(x, w_ref, b_ref, t_idx, T)
            sg = _sigmoid(pre)
            dpre = (df_ref[rows, :] + db_ref[rows, :]) * (sg * (1.0 + pre * (1.0 - sg)))
            dx = w_ref[2:3, :] * dpre
            for j in (0, 1, 3, 4):
                dx = dx + w_ref[j:j + 1, :] * _conv_shift(dpre, 2 - j, t_idx, T)
            dx_ref[pl.ds(_aligned(ws + off, CONV_ROWS), CONV_ROWS), :] = dx[off:off + CONV_ROWS].astype(dx_ref.dtype)
            own = jnp.where((row >= off) & (row < off + CONV_ROWS), dpre, 0.0)
            new = [sums[5] + jnp.sum(own, axis=0, keepdims=True)]
            for j in range(CONV_WIDTH):
                new.insert(j, sums[j] + jnp.sum(own * _conv_shift(x, j - 2, t_idx, T), axis=0, keepdims=True))
            return tuple(new)

        zero = jnp.zeros((1, w), F32)
        sums = _conv_chunks(T, chunk, (zero,) * (CONV_WIDTH + 1))
        for j in range(CONV_WIDTH):
            dw_ref[j:j + 1, :] = sums[j]
        dbias_ref[...] = sums[CONV_WIDTH]

    blk = pl.BlockSpec((T, w), lambda j: (0, j))
    return _pallas(
        body, name=name, grid=(XBC_DIM // w,),
        in_specs=[blk, blk, pl.BlockSpec((T, w), lambda j: (0, j + first)),
                  pl.BlockSpec((CONV_WIDTH, w), lambda j: (0, j)), pl.BlockSpec((1, w), lambda j: (0, j))],
        out_specs=[blk, pl.BlockSpec((CONV_WIDTH, w), lambda j: (0, j)), pl.BlockSpec((1, w), lambda j: (0, j))],
        out_shape=[jax.ShapeDtypeStruct((T, XBC_DIM), BF16), jax.ShapeDtypeStruct((CONV_WIDTH, XBC_DIM), F32),
                   jax.ShapeDtypeStruct((1, XBC_DIM), F32)],
        compiler_params=_params(("parallel",)),
    )(dact_f, dact_b, u_big, conv_w, conv_b)


def _ssd_expand(rev):
    off = SSM_HEADS if rev else 0
    h = jnp.arange(LANES, dtype=jnp.int32)[:, None]
    return (jnp.arange(D_INNER, dtype=jnp.int32)[None, :] // 64 + off == h).astype(F32)


def _ssd_head_terms(dt_ref, bias_ref, alog_ref, acst_s, dtt_s, rev):
    L = CHUNK
    row = lax.broadcasted_iota(jnp.int32, (L, L), 0)
    col = lax.broadcasted_iota(jnp.int32, (L, L), 1)
    mask = (row <= col) if rev else (row >= col)
    cm = mask.astype(F32)
    cmt = ((row >= col) if rev else (row <= col)).astype(F32)
    pre = dt_ref[...] + bias_ref[...]
    dt = _softplus(pre)
    a = -jnp.exp(alog_ref[...])
    da = dt * a
    acs = _dot_h(cm, da)
    acst_s[...] = _dot_h_tn(da, cmt)
    dtt_s[...] = _dot_h_tn(dt, (row == col).astype(F32))
    tot = jnp.sum(da, axis=0, keepdims=True)
    w = jnp.exp(tot - acs)
    return dict(mask=mask, cm=cm, cmt=cmt, ident=(row == col).astype(F32), pre=pre, dt=dt, a=a, da=da, acs=acs,
                tot=tot, e=jnp.exp(acs), w=w, wdt=w * dt, dec=jnp.exp(tot))


def _pair(lo, v, h0):
    return jnp.where(lo, v[:, h0:h0 + 1], v[:, h0 + 1:h0 + 2])


def _ssd_fwd(xbc_act, u_small, bias128, alog128, *, rev, name):
    T = xbc_act.shape[0]
    L = CHUNK
    nc = T // L
    off = SSM_HEADS if rev else 0

    def cidx(c):
        return (nc - 1 - c) if rev else c

    def body(xs_ref, bm_ref, cm_ref, dt_ref, bias_ref, alog_ref, y_ref, hin_ref, ht_s, acst_s, dtt_s, wx_s, dec_s):
        c = pl.program_id(0)

        @pl.when(c == 0)
        def _():
            ht_s[...] = jnp.zeros_like(ht_s)

        t = _ssd_head_terms(dt_ref, bias_ref, alog_ref, acst_s, dtt_s, rev)
        lo = lax.broadcasted_iota(jnp.int32, (L, LANES), 1) < 64
        lo1 = lax.broadcasted_iota(jnp.int32, (1, LANES), 1) < 64
        for g in range(SSM_GROUPS):
            bmat = bm_ref[:, g * LANES:(g + 1) * LANES].astype(BF16)
            cmat = cm_ref[:, g * LANES:(g + 1) * LANES].astype(BF16)
            gmat = _dot_nt(cmat, bmat)
            ht = ht_s[g]
            ch = _dot(cmat, ht.astype(BF16))
            for pr in range(4):
                ps = slice(pr * LANES, (pr + 1) * LANES)
                cs = slice(g * 512 + pr * LANES, g * 512 + (pr + 1) * LANES)
                h0 = off + 8 * g + 2 * pr
                xp = xs_ref[:, cs]
                acc = _pair(lo, t["e"], h0) * ch[:, ps]
                for s_ in range(2):
                    h = h0 + s_
                    seg = t["acs"][:, h:h + 1] - acst_s[h:h + 1, :]
                    lam = jnp.exp(jnp.where(t["mask"], seg, -1e30))
                    m = (gmat * lam * dtt_s[h:h + 1, :]).astype(BF16)
                    xm = jnp.where(lo if s_ == 0 else jnp.logical_not(lo), xp, 0.0).astype(BF16)
                    acc = acc + _dot(m, xm)
                y_ref[:, cs] = acc
                wx_s[:, ps] = (_pair(lo, t["wdt"], h0) * xp).astype(BF16)
                dec_s[0:1, ps] = _pair(lo1, t["dec"], h0)
            hin_ref[0, g] = ht.astype(BF16)
            ht_s[g] = ht * dec_s[0:1, :] + _dot_tn(bmat, wx_s[...])

    return _pallas(
        body, name=name, grid=(nc,),
        in_specs=[pl.BlockSpec((L, D_INNER), lambda c: (cidx(c), 0)), pl.BlockSpec((L, 512), lambda c: (cidx(c), 4)),
                  pl.BlockSpec((L, 512), lambda c: (cidx(c), 5)),
                  pl.BlockSpec((L, LANES), lambda c: (cidx(c), U_DT // LANES)),
                  pl.BlockSpec((1, LANES), lambda c: (0, 0)), pl.BlockSpec((1, LANES), lambda c: (0, 0))],
        out_specs=[pl.BlockSpec((L, D_INNER), lambda c: (cidx(c), 0)),
                   pl.BlockSpec((1, SSM_GROUPS, D_STATE, 512), lambda c: (cidx(c), 0, 0, 0))],
        out_shape=[jax.ShapeDtypeStruct((T, D_INNER), F32), jax.ShapeDtypeStruct((nc, SSM_GROUPS, D_STATE, 512), BF16)],
        scratch_shapes=[pltpu.VMEM((SSM_GROUPS, D_STATE, 512), F32), pltpu.VMEM((LANES, L), F32),
                        pltpu.VMEM((LANES, L), F32), pltpu.VMEM((L, 512), BF16), pltpu.VMEM((8, 512), F32)],
        compiler_params=_params(("arbitrary",)),
    )(xbc_act, xbc_act, xbc_act, u_small, bias128, alog128)


def _ssd_bwd(dy, xbc_act, u_small, bias128, alog128, hin, skip_x, *, rev, name):
    T = xbc_act.shape[0]
    L = CHUNK
    nc = T // L
    off = SSM_HEADS if rev else 0
    has_skip = skip_x is not None

    def cidx(c):
        return c if rev else (nc - 1 - c)

    def body(*refs):
        (dy_ref, xs_ref, bm_ref, cm_ref, dt_ref, bias_ref, alog_ref, hin_ref) = refs[:8]
        k = 8
        skip_ref = refs[k] if has_skip else None
        k += 1 if has_skip else 0
        (dx_ref, draw_ref, dalog_ref, dbias_ref, dht_s, acst_s, dtt_s, rowt_s, ddtt_s, wx_s, edy_s, dec_s) = refs[k:]
        c = pl.program_id(0)

        @pl.when(c == 0)
        def _():
            dht_s[...] = jnp.zeros_like(dht_s)
            rowt_s[...] = jnp.zeros_like(rowt_s)
            ddtt_s[...] = jnp.zeros_like(ddtt_s)

        t = _ssd_head_terms(dt_ref, bias_ref, alog_ref, acst_s, dtt_s, rev)
        lane1 = lax.broadcasted_iota(jnp.int32, (1, LANES), 1)
        lo = lax.broadcasted_iota(jnp.int32, (L, LANES), 1) < 64
        lo1 = lane1 < 64
        colpart = jnp.zeros((L, LANES), F32)
        u_cols = jnp.zeros((L, LANES), F32)
        v_cols = jnp.zeros((L, LANES), F32)
        dtot_h = jnp.zeros((1, LANES), F32)
        for g in range(SSM_GROUPS):
            bmat = bm_ref[:, g * LANES:(g + 1) * LANES].astype(BF16)
            cmat = cm_ref[:, g * LANES:(g + 1) * LANES].astype(BF16)
            gmat = _dot_nt(cmat, bmat)
            ht_in = hin_ref[0, g]
            dht = dht_s[g]
            ht_in_b, dht_b = ht_in.astype(BF16), dht.astype(BF16)
            ch = _dot(cmat, ht_in_b)
            bdh = _dot(bmat, dht_b)
            th = jnp.sum(dht * ht_in, axis=0, keepdims=True)
            dgm = jnp.zeros((L, L), F32)
            for pr in range(4):
                ps = slice(pr * LANES, (pr + 1) * LANES)
                cs = slice(g * 512 + pr * LANES, g * 512 + (pr + 1) * LANES)
                h0 = off + 8 * g + 2 * pr
                xp = xs_ref[:, cs]
                dyp = dy_ref[:, cs]
                dyp_b = dyp.astype(BF16)
                wdt_p = _pair(lo, t["wdt"], h0)
                e_p = _pair(lo, t["e"], h0)
                xb = xp * bdh[:, ps]
                dc = dyp * ch[:, ps]
                dxp = wdt_p * bdh[:, ps]
                for s_ in range(2):
                    h = h0 + s_
                    keep = lo if s_ == 0 else jnp.logical_not(lo)
                    keep1 = lo1 if s_ == 0 else jnp.logical_not(lo1)
                    onehot = (lane1 == h).astype(F32)
                    dtrow = dtt_s[h:h + 1, :]
                    seg = t["acs"][:, h:h + 1] - acst_s[h:h + 1, :]
                    lam = jnp.exp(jnp.where(t["mask"], seg, -1e30))
                    mf0 = gmat * lam
                    m = (mf0 * dtrow).astype(BF16)
                    xm = jnp.where(keep, xp, 0.0).astype(BF16)
                    dm = _dot_nt(dyp_b, xm)
                    r = dm * mf0
                    q = r * dtrow
                    dgm = dgm + dm * lam * dtrow
                    colpart = colpart + jnp.sum(q, axis=1, keepdims=True) * onehot
                    rowt_s[h:h + 1, :] = jnp.sum(q, axis=0, keepdims=True)
                    ddtt_s[h:h + 1, :] = jnp.sum(r, axis=0, keepdims=True)
                    u_cols = u_cols + jnp.sum(jnp.where(keep, xb, 0.0), axis=1, keepdims=True) * onehot
                    v_cols = v_cols + jnp.sum(jnp.where(keep, dc, 0.0), axis=1, keepdims=True) * onehot
                    dtot_h = dtot_h + jnp.sum(jnp.where(keep1, th[:, ps], 0.0), axis=1, keepdims=True) * onehot
                    dxp = dxp + jnp.where(keep, _dot_tn(m, dyp_b), 0.0)
                if has_skip:
                    dxp = dxp + dyp * skip_ref[:, cs]
                dx_ref[:, cs] = dxp
                wx_s[:, ps] = (wdt_p * xp).astype(BF16)
                edy_s[:, ps] = (e_p * dyp).astype(BF16)
                dec_s[0:1, ps] = _pair(lo1, t["dec"], h0)
            edy_b = edy_s[...]
            dgm_b = dgm.astype(BF16)
            dx_ref[:, D_INNER + g * LANES:D_INNER + (g + 1) * LANES] = (
                _dot_nt(wx_s[...], dht_b) + _dot_tn(dgm_b, cmat))
            dx_ref[:, D_INNER + 512 + g * LANES:D_INNER + 512 + (g + 1) * LANES] = (
                _dot_nt(edy_b, ht_in_b) + _dot(dgm_b, bmat))
            dht_s[g] = dec_s[0:1, :] * dht + _dot_tn(cmat, edy_b)

        t_e = v_cols * t["e"]
        t_w = u_cols * t["wdt"]
        colsum_part = _dot_h_tn(rowt_s[...], t["ident"])
        dtot = jnp.sum(t_w, axis=0, keepdims=True) + t["dec"] * dtot_h
        row1 = lax.broadcasted_iota(jnp.int32, (L, LANES), 0)
        last = row1 == (0 if rev else L - 1)
        dacs = colpart - colsum_part + t_e - t_w + jnp.where(last, dtot, 0.0)
        dda = _dot_h(t["cmt"], dacs)
        ddt = dda * t["a"] + u_cols * t["w"] + _dot_h_tn(ddtt_s[...], t["ident"])
        dalog = jnp.sum(dda * t["dt"], axis=0, keepdims=True) * t["a"]
        draw = ddt * _sigmoid(t["pre"])
        draw_ref[...] = draw
        dbias = jnp.sum(draw, axis=0, keepdims=True)

        @pl.when(c == 0)
        def _():
            dalog_ref[...] = dalog
            dbias_ref[...] = dbias

        @pl.when(c > 0)
        def _():
            dalog_ref[...] += dalog
            dbias_ref[...] += dbias

    one = pl.BlockSpec((1, LANES), lambda c: (0, 0))
    in_specs = [pl.BlockSpec((L, D_INNER), lambda c: (cidx(c), 0)), pl.BlockSpec((L, D_INNER), lambda c: (cidx(c), 0)),
                pl.BlockSpec((L, 512), lambda c: (cidx(c), 4)), pl.BlockSpec((L, 512), lambda c: (cidx(c), 5)),
                pl.BlockSpec((L, LANES), lambda c: (cidx(c), U_DT // LANES)), one, one,
                pl.BlockSpec((1, SSM_GROUPS, D_STATE, 512), lambda c: (cidx(c), 0, 0, 0))]
    ins = [dy, xbc_act, xbc_act, xbc_act, u_small, bias128, alog128, hin]
    if has_skip:
        in_specs.append(pl.BlockSpec((1, D_INNER), lambda c: (0, 0)))
        ins.append(skip_x)
    return _pallas(
        body, name=name, grid=(nc,), in_specs=in_specs,
        out_specs=[pl.BlockSpec((L, XBC_DIM), lambda c: (cidx(c), 0)), pl.BlockSpec((L, LANES), lambda c: (cidx(c), 0)),
                   one, one],
        out_shape=[jax.ShapeDtypeStruct((T, XBC_DIM), F32), jax.ShapeDtypeStruct((T, LANES), F32),
                   jax.ShapeDtypeStruct((1, LANES), F32), jax.ShapeDtypeStruct((1, LANES), F32)],
        scratch_shapes=[pltpu.VMEM((SSM_GROUPS, D_STATE, 512), F32), pltpu.VMEM((LANES, L), F32),
                        pltpu.VMEM((LANES, L), F32), pltpu.VMEM((LANES, L), F32), pltpu.VMEM((LANES, L), F32),
                        pltpu.VMEM((L, 512), BF16), pltpu.VMEM((L, 512), BF16), pltpu.VMEM((8, 512), F32)],
        compiler_params=_params(("arbitrary",)),
    )(*ins)


def _ssm_out_fwd(y_f, y_b, xbc_act, u_big, skip_x, ssm_norm, *, name, tile=512):
    rows = y_f.shape[0]

    def fn(yf, yb, xs, z, sk, nw):
        yz = (yf + yb + sk * xs) * (z * _sigmoid(z))
        r = lax.rsqrt(jnp.mean(yz * yz, axis=-1, keepdims=True) + EPS)
        return yz * r * nw

    return _rowmap(fn, name=name, rows=rows, tile=tile, ncol=SSM_GROUPS,
                   ins=[(y_f, 512, _colj), (y_b, 512, _colj), (xbc_act, 512, _colj), (u_big, 512, _colj)],
                   consts=[(skip_x, 512, _colj), (ssm_norm, 512, _colj)], outs=[(D_INNER, BF16, 512, _colj)])[0]


def _ssm_out_bwd(dm, y_f, y_b, xbc_act, u_big, skip_x, ssm_norm, *, name, tile=512):
    rows = y_f.shape[0]

    def fn(dmv, yf, yb, xs, z, sk, nw):
        sg = _sigmoid(z)
        y = yf + yb + sk * xs
        yz = y * (z * sg)
        r = lax.rsqrt(jnp.mean(yz * yz, axis=-1, keepdims=True) + EPS)
        xh = yz * r
        dxh = dmv * nw
        dyz = r * (dxh - xh * jnp.mean(dxh * xh, axis=-1, keepdims=True))
        dy = dyz * (z * sg)
        dz = dyz * y * (sg * (1.0 + z * (1.0 - sg)))
        return dy, dz, jnp.sum(dmv * xh, axis=0, keepdims=True), jnp.sum(dy * xs, axis=0, keepdims=True)

    return _rowmap(fn, name=name, rows=rows, tile=tile, ncol=SSM_GROUPS,
                   ins=[(dm, 512, _colj), (y_f, 512, _colj), (y_b, 512, _colj), (xbc_act, 512, _colj),
                        (u_big, 512, _colj)],
                   consts=[(skip_x, 512, _colj), (ssm_norm, 512, _colj)],
                   outs=[(D_INNER, F32, 512, _colj), (D_INNER, BF16, 512, _colj)],
                   accs=[(D_INNER, 512, _colj), (D_INNER, 512, _colj)])


def _merge_fwd(pa, pb, u_big, *, name, tile=512):
    rows = pa.shape[0]

    def fn(a, b, ga, gb):
        return _sigmoid(ga) * a + _sigmoid(gb) * b

    return _rowmap(fn, name=name, rows=rows, tile=tile,
                   ins=[(pa, 1024, _col0), (pb, 1024, _col0), (u_big, 1024, lambda j: 5), (u_big, 1024, lambda j: 6)],
                   outs=[(1024, BF16, 1024, _col0)])[0]


def _merge_bwd(dmg, pa, pb, u_big, *, name, tile=512):
    rows = pa.shape[0]

    def fn(d, a, b, ga, gb):
        sa, sb = _sigmoid(ga), _sigmoid(gb)
        return d * sa, d * sb, d * a * sa * (1.0 - sa), d * b * sb * (1.0 - sb)

    return _rowmap(fn, name=name, rows=rows, tile=tile,
                   ins=[(dmg, 1024, _col0), (pa, 1024, _col0), (pb, 1024, _col0), (u_big, 1024, lambda j: 5),
                        (u_big, 1024, lambda j: 6)],
                   outs=[(1024, BF16, 1024, _col0)] * 4)


def _loss_bwd(y, target, *, name, tile=512):
    rows, d = y.shape

    def fn(yv, tv):
        err = yv - tv
        part = jnp.sum(jnp.sum(err * err, axis=-1, keepdims=True), axis=0, keepdims=True)
        return err * (1.0 / d), jnp.broadcast_to(part * (0.5 / d), (1, LANES))

    dy, part = _rowmap(fn, name=name, rows=rows, tile=tile, ins=[(y, d, _col0), (target, d, _col0)],
                       outs=[(d, F32, d, _col0)], accs=[(LANES, LANES, _col0)])
    return dy, part[0, 0]


def _small_slab(gs, dskip_ch, dalog_f, dalog_b, dbias_f, dbias_b, gkv, gqh, gkh, dconv_w, *, name):
    e_mat = _ssd_expand(False)
    full_names = ("ffn1_norm", "mix_norm", "q_a_norm", "conv_b", "ssm_norm", "ffn2_norm")
    full = [gs[n] for n in full_names]
    nf = len(full)

    def body(*refs):
        fulls = refs[:nf]
        (dsk_ref, e_ref, af_ref, ab_ref, bf_ref, bb_ref, gkv_ref, gqh_ref, gkh_ref, cw_ref, o_ref) = refs[nf:]
        o_ref[...] = jnp.zeros_like(o_ref)
        for n, r in zip(full_names, fulls):
            o_ref[SMALL_ROW[n]:SMALL_ROW[n] + 1, 0:r.shape[1]] = r[...]
        o_ref[SMALL_ROW["kv_a_norm"]:SMALL_ROW["kv_a_norm"] + 1, 0:KV_LORA] = gkv_ref[...]
        o_ref[SMALL_ROW["q_head_norm"]:SMALL_ROW["q_head_norm"] + 1, 0:LANES] = gqh_ref[...]
        o_ref[SMALL_ROW["k_head_norm"]:SMALL_ROW["k_head_norm"] + 1, 0:LANES] = gkh_ref[...]
        o_ref[SMALL_ROW["a_log_fwd"]:SMALL_ROW["a_log_fwd"] + 1, 0:LANES] = af_ref[...]
        o_ref[SMALL_ROW["a_log_bwd"]:SMALL_ROW["a_log_bwd"] + 1, 0:LANES] = pltpu.roll(ab_ref[...], 96, 1)
        o_ref[SMALL_ROW["dt_bias_fwd"]:SMALL_ROW["dt_bias_fwd"] + 1, 0:LANES] = bf_ref[...]
        o_ref[SMALL_ROW["dt_bias_bwd"]:SMALL_ROW["dt_bias_bwd"] + 1, 0:LANES] = pltpu.roll(bb_ref[...], 96, 1)
        dsk = _dot_h_nt(jnp.broadcast_to(dsk_ref[...], (8, D_INNER)), e_ref[...])
        o_ref[SMALL_ROW["d_skip"]:SMALL_ROW["d_skip"] + 1, 0:LANES] = dsk[0:1, :]
        o_ref[CONV_ROW:CONV_ROW + CONV_WIDTH, :] = cw_ref[...]

    return _pallas(body, name=name, out_shape=jax.ShapeDtypeStruct((SMALL_ROWS, SMALL_COLS), F32))(
        *full, dskip_ch, e_mat, dalog_f, dalog_b, dbias_f, dbias_b, gkv, gqh, gkh, dconv_w)


def _adamw_math(g, w, m, v):
    m2 = ADAM_B1 * m + (1.0 - ADAM_B1) * g
    v2 = ADAM_B2 * v + (1.0 - ADAM_B2) * (g * g)
    m_hat = m2 / (1.0 - ADAM_B1 ** ADAM_STEP)
    v_hat = v2 / (1.0 - ADAM_B2 ** ADAM_STEP)
    delta = -ADAM_LR * (m_hat / (jnp.sqrt(v_hat) + ADAM_EPS) + ADAM_WD * w)
    return delta, m2, v2


def _sum8(r_ref):
    g = r_ref[0].astype(F32)
    for s in range(1, N_DEV):
        g = g + r_ref[s].astype(F32)
    return g


def _reduce_adamw(recv, w, m, v, *, name, tile=256):
    _, R, C = recv.shape
    tile = _pick(R, tile) if R % LANES == 0 else R
    assert R % tile == 0

    def body(r_ref, w_ref, m_ref, v_ref, g_ref, d_ref, m2_ref, v2_ref):
        g = _sum8(r_ref)
        delta, m2, v2 = _adamw_math(g, w_ref[...], m_ref[...], v_ref[...])
        g_ref[...] = g
        d_ref[...] = delta
        m2_ref[...] = m2
        v2_ref[...] = v2

    blk = pl.BlockSpec((tile, C), lambda i: (i, 0))
    return _pallas(
        body, name=name, grid=(R // tile,),
        in_specs=[pl.BlockSpec((N_DEV, tile, C), lambda i: (0, i, 0)), blk, blk, blk], out_specs=[blk] * 4,
        out_shape=[jax.ShapeDtypeStruct((R, C), F32)] * 4, compiler_params=_params(("parallel",)),
    )(recv, w, m, v)


def _reduce_t_adamw(recv, w, m, v, *, name):
    R, cs = w.shape

    def body(r_ref, w_ref, m_ref, v_ref, g_ref, d_ref, m2_ref, v2_ref):
        g = _sum8(r_ref).T
        delta, m2, v2 = _adamw_math(g, w_ref[...], m_ref[...], v_ref[...])
        g_ref[...] = g
        d_ref[...] = delta
        m2_ref[...] = m2
        v2_ref[...] = v2

    return _pallas(body, name=name, out_shape=[jax.ShapeDtypeStruct((R, cs), F32)] * 4,
                   compiler_params=pltpu.CompilerParams(vmem_limit_bytes=VMEM_LIMIT))(recv, w, m, v)


def _reduce8(recv, *, name, tile):
    _, R, C = recv.shape

    def body(r_ref, g_ref):
        g_ref[...] = _sum8(r_ref)

    return _pallas(body, name=name, grid=(R // tile,),
                   in_specs=[pl.BlockSpec((N_DEV, tile, C), lambda i: (0, i, 0))],
                   out_specs=pl.BlockSpec((tile, C), lambda i: (i, 0)),
                   out_shape=jax.ShapeDtypeStruct((R, C), F32), compiler_params=_params(("parallel",)))(recv)


def _adamw(g, w, m, v, *, name, tile=256):
    R, C = w.shape

    def body(g_ref, w_ref, m_ref, v_ref, d_ref, m2_ref, v2_ref):
        delta, m2, v2 = _adamw_math(g_ref[...], w_ref[...], m_ref[...], v_ref[...])
        d_ref[...] = delta
        m2_ref[...] = m2
        v2_ref[...] = v2

    blk = pl.BlockSpec((R, tile), lambda i: (0, i))
    return _pallas(body, name=name, grid=(C // tile,), in_specs=[blk] * 4, out_specs=[blk] * 3,
                   out_shape=[jax.ShapeDtypeStruct((R, C), F32)] * 3, compiler_params=_params(("parallel",)))(g, w, m, v)


def _adamw_small(srecv, conv_g, ws, ms, vs, *, name):
    n = len(ws)

    def body(*refs):
        s_ref, c_ref = refs[0], refs[1]
        w_refs, m_refs, v_refs = refs[2:2 + n], refs[2 + n:2 + 2 * n], refs[2 + 2 * n:2 + 3 * n]
        outs = refs[2 + 3 * n:]
        gsum = _sum8(s_ref)
        for i in range(n):
            if i < len(SMALL):
                g = gsum[i:i + 1, 0:SMALL[i][1]]
            else:
                g = _sum8(c_ref)
            delta, m2, v2 = _adamw_math(g, w_refs[i][...], m_refs[i][...], v_refs[i][...])
            outs[i][...] = g
            outs[n + i][...] = delta
            outs[2 * n + i][...] = m2
            outs[3 * n + i][...] = v2

    shapes = [jax.ShapeDtypeStruct(w.shape, F32) for w in ws]
    got = _pallas(body, name=name, out_shape=shapes * 4,
                  compiler_params=pltpu.CompilerParams(vmem_limit_bytes=VMEM_LIMIT))(srecv, conv_g, *ws, *ms, *vs)
    return got[:n], got[n:2 * n], got[2 * n:3 * n], got[3 * n:]


def _ffn_fwd(x, norm, w_g_t, w_u_t, w_d, tag):
    h = _rms_fwd(x, norm, name=f"{tag}_rms")
    gu = _mm(h, jnp.concatenate([w_g_t, w_u_t], axis=0), name=f"{tag}_gu", tb=True, out_dtype=BF16)
    act = _swiglu_fwd(gu, name=f"{tag}_act")
    out = _mm(act, w_d, name=f"{tag}_down", alpha=0.5, res=x)
    return out, (h, gu, act)


def _ffn_bwd(dout, x, norm, w_g_t, w_u_t, w_d, saved, tag, send=None):
    h, gu, act = saved
    d_act = _mm(dout, w_d, name=f"{tag}_dact", tb=True, alpha=0.5, out_dtype=BF16)
    d_wd = _mm(act, dout, name=f"{tag}_dwd", ta=True, alpha=0.5, tm=1408, tn=1024, out_dtype=BF16)
    tok = send(("down",), [d_wd]) if send else None
    dg, du = _swiglu_bwd(d_act, gu, name=f"{tag}_dswiglu")
    d_wg_t = _mm(dg, h, name=f"{tag}_dwg", ta=True, tm=1408, tn=1024, out_dtype=BF16, after=tok)
    d_wu_t = _mm(du, h, name=f"{tag}_dwu", ta=True, tm=1408, tn=1024, out_dtype=BF16)
    tok = send(("gate", "up"), [d_wg_t, d_wu_t]) if send else None
    dh = _mm(dg, w_g_t, name=f"{tag}_dh_g", after=tok)
    dh = _mm(du, w_u_t, name=f"{tag}_dh_u", res=dh)
    dx, dnorm = _rms_bwd(dh, x, norm, dout, name=f"{tag}_drms")
    return dx, dnorm, d_wg_t, d_wu_t, d_wd


def _rope_tables(positions, T):
    pos = positions.reshape(T).astype(F32)
    inv_freq = 1.0 / (ROPE_BASE ** (jnp.arange(0, QK_ROPE, 2, dtype=F32) / QK_ROPE))
    ang = pos[:, None] * inv_freq
    cos, sin = jnp.cos(ang), jnp.sin(ang)
    one64, z64 = jnp.ones((T, 64), F32), jnp.zeros((T, 64), F32)
    z16, z32, one32 = jnp.zeros((T, 16), F32), jnp.zeros((T, 32), F32), jnp.ones((T, 32), F32)
    c = jnp.concatenate([one64, cos, cos, one32], axis=1)
    s1 = jnp.concatenate([z64, -sin, z16, z32], axis=1)
    s2 = jnp.concatenate([z64, z16, sin, z32], axis=1)
    return c, s1, s2


def _cols(g):
    n, r, cs = g.shape
    return g.transpose(1, 0, 2).reshape(r, n * cs)


def _rows(g):
    n, rs, c = g.shape
    return g.reshape(n * rs, c)


def _pad_lanes(v, n=LANES):
    return jnp.pad(v, ((0, 0), (0, n - v.shape[1])))


def _in_proj_weights(w_in_t):
    z = lambda n: jnp.zeros((n, D_MODEL), w_in_t.dtype)
    w_small_t = jnp.concatenate([w_in_t[0:384], z(128), w_in_t[384:672], z(96), w_in_t[5792:5856], z(64)], axis=0)
    w_big_t = jnp.concatenate([w_in_t[672:5792], w_in_t[5856:7904]], axis=0)
    return w_small_t, w_big_t


def _mla_up_weights(w_q_b_t, w_kv_b):
    wq = w_q_b_t.reshape(N_HEADS, QK_HEAD, Q_LORA)
    wq = jnp.pad(wq, ((0, 0), (0, LANES - QK_HEAD), (0, 0))).reshape(N_HEADS * LANES, Q_LORA)
    wkv = w_kv_b.reshape(KV_LORA, N_HEADS, QK_NOPE + V_HEAD)
    wk = jnp.pad(wkv[..., :QK_NOPE], ((0, 0), (0, 0), (0, LANES - QK_NOPE))).reshape(KV_LORA, N_HEADS * LANES)
    v = wkv[..., QK_NOPE:]
    zv = jnp.zeros_like(v)
    even = (jnp.arange(N_HEADS) % 2 == 0)[None, :, None]
    wv = jnp.where(even, jnp.concatenate([v, zv], -1), jnp.concatenate([zv, v], -1)).reshape(KV_LORA, N_HEADS * LANES)
    return wq, wk, wv


def _shard_rows(g):
    return g.reshape(N_DEV, g.shape[0] // N_DEV, g.shape[1])


def kernel(x, positions, ffn1_norm, ffn1_w_gate, ffn1_w_up, ffn1_w_down, mix_norm, w_in, q_a_norm, w_q_b, kv_a_norm, w_kv_b, q_head_norm, k_head_norm, conv_w, conv_b, a_log_fwd, a_log_bwd, dt_bias_fwd, dt_bias_bwd, d_skip, ssm_norm, w_attn_branch, w_ssm_branch, w_out, ffn2_norm, ffn2_w_gate, ffn2_w_up, ffn2_w_down, loss_target, m_ffn1_norm, m_ffn1_w_gate, m_ffn1_w_up, m_ffn1_w_down, m_mix_norm, m_w_in, m_q_a_norm, m_w_q_b, m_kv_a_norm, m_w_kv_b, m_q_head_norm, m_k_head_norm, m_conv_w, m_conv_b, m_a_log_fwd, m_a_log_bwd, m_dt_bias_fwd, m_dt_bias_bwd, m_d_skip, m_ssm_norm, m_w_attn_branch, m_w_ssm_branch, m_w_out, m_ffn2_norm, m_ffn2_w_gate, m_ffn2_w_up, m_ffn2_w_down, v_ffn1_norm, v_ffn1_w_gate, v_ffn1_w_up, v_ffn1_w_down, v_mix_norm, v_w_in, v_q_a_norm, v_w_q_b, v_kv_a_norm, v_w_kv_b, v_q_head_norm, v_k_head_norm, v_conv_w, v_conv_b, v_a_log_fwd, v_a_log_bwd, v_dt_bias_fwd, v_dt_bias_bwd, v_d_skip, v_ssm_norm, v_w_attn_branch, v_w_ssm_branch, v_w_out, v_ffn2_norm, v_ffn2_w_gate, v_ffn2_w_up, v_ffn2_w_down):
    w_all = dict(ffn1_norm=ffn1_norm, ffn1_w_gate=ffn1_w_gate, ffn1_w_up=ffn1_w_up, ffn1_w_down=ffn1_w_down, mix_norm=mix_norm, w_in=w_in, q_a_norm=q_a_norm, w_q_b=w_q_b, kv_a_norm=kv_a_norm, w_kv_b=w_kv_b, q_head_norm=q_head_norm, k_head_norm=k_head_norm, conv_w=conv_w, conv_b=conv_b, a_log_fwd=a_log_fwd, a_log_bwd=a_log_bwd, dt_bias_fwd=dt_bias_fwd, dt_bias_bwd=dt_bias_bwd, d_skip=d_skip, ssm_norm=ssm_norm, w_attn_branch=w_attn_branch, w_ssm_branch=w_ssm_branch, w_out=w_out, ffn2_norm=ffn2_norm, ffn2_w_gate=ffn2_w_gate, ffn2_w_up=ffn2_w_up, ffn2_w_down=ffn2_w_down)
    m_all = dict(ffn1_norm=m_ffn1_norm, ffn1_w_gate=m_ffn1_w_gate, ffn1_w_up=m_ffn1_w_up, ffn1_w_down=m_ffn1_w_down, mix_norm=m_mix_norm, w_in=m_w_in, q_a_norm=m_q_a_norm, w_q_b=m_w_q_b, kv_a_norm=m_kv_a_norm, w_kv_b=m_w_kv_b, q_head_norm=m_q_head_norm, k_head_norm=m_k_head_norm, conv_w=m_conv_w, conv_b=m_conv_b, a_log_fwd=m_a_log_fwd, a_log_bwd=m_a_log_bwd, dt_bias_fwd=m_dt_bias_fwd, dt_bias_bwd=m_dt_bias_bwd, d_skip=m_d_skip, ssm_norm=m_ssm_norm, w_attn_branch=m_w_attn_branch, w_ssm_branch=m_w_ssm_branch, w_out=m_w_out, ffn2_norm=m_ffn2_norm, ffn2_w_gate=m_ffn2_w_gate, ffn2_w_up=m_ffn2_w_up, ffn2_w_down=m_ffn2_w_down)
    v_all = dict(ffn1_norm=v_ffn1_norm, ffn1_w_gate=v_ffn1_w_gate, ffn1_w_up=v_ffn1_w_up, ffn1_w_down=v_ffn1_w_down, mix_norm=v_mix_norm, w_in=v_w_in, q_a_norm=v_q_a_norm, w_q_b=v_w_q_b, kv_a_norm=v_kv_a_norm, w_kv_b=v_w_kv_b, q_head_norm=v_q_head_norm, k_head_norm=v_k_head_norm, conv_w=v_conv_w, conv_b=v_conv_b, a_log_fwd=v_a_log_fwd, a_log_bwd=v_a_log_bwd, dt_bias_fwd=v_dt_bias_fwd, dt_bias_bwd=v_dt_bias_bwd, d_skip=v_d_skip, ssm_norm=v_ssm_norm, w_attn_branch=v_w_attn_branch, w_ssm_branch=v_w_ssm_branch, w_out=v_w_out, ffn2_norm=v_ffn2_norm, ffn2_w_gate=v_ffn2_w_gate, ffn2_w_up=v_ffn2_w_up, ffn2_w_down=v_ffn2_w_down)
    T = x.shape[1]
    xs_in, target = x[0], loss_target[0]
    def two_d(n, a):
        if n in TRANSPOSED:
            return jnp.swapaxes(a, 1, 2).reshape(a.shape[2], a.shape[1])
        return a.reshape(-1, a.shape[-1])

    w2 = {n: two_d(n, a) for n, a in w_all.items()}
    m2 = {n: two_d(n, a) for n, a in m_all.items()}
    v2 = {n: two_d(n, a) for n, a in v_all.items()}
    p = {n: w2[n] for n, _ in SMALL}
    bf = lambda n: w2[n].astype(BF16)

    first = ["ffn1_w_gate", "ffn1_w_up", "ffn1_w_down"]
    ge = dict(zip(first, _all_gather_two_level([bf(n) for n in first], name="gather_ffn1")))
    mixw = ["w_in", "w_q_b", "w_kv_b", "conv_w"]
    behind_first = ge["ffn1_w_down"][0, 0:1, 0:1].astype(F32) * 0.0
    mix_started, token = _exchange_start(["gather"] * len(mixw),
                                         [bf(n) for n in mixw[:3]] + [w2["conv_w"] + behind_first],
                                         name="gather_mix_start")
    ffn1_norm_f = p["ffn1_norm"] + token[0:1, 0:1]
    w_g1t, w_u1t = _rows(ge["ffn1_w_gate"]), _rows(ge["ffn1_w_up"])
    w_d1 = _rows(ge["ffn1_w_down"])
    late = ["w_attn_branch", "w_ssm_branch", "w_out", "ffn2_w_gate", "ffn2_w_up", "ffn2_w_down"]
    late_shards = [bf(n) for n in late]

    tabs = _rope_tables(positions, T)
    qg, kg = _pad_lanes(p["q_head_norm"]), _pad_lanes(p["k_head_norm"])
    bias128 = _pad_lanes(jnp.concatenate([p["dt_bias_fwd"], p["dt_bias_bwd"]], axis=1))
    alog128 = _pad_lanes(jnp.concatenate([p["a_log_fwd"], p["a_log_bwd"]], axis=1))
    skip_x = jnp.repeat(p["d_skip"], 64, axis=1)

    x1, ffn1_saved = _ffn_fwd(xs_in, ffn1_norm_f, w_g1t, w_u1t, w_d1, "ffn1")
    h2 = _rms_fwd(x1, p["mix_norm"], name="mix_rms")
    ge.update(zip(mixw, _exchange_wait(["gather"] * len(mixw), mix_started, h2, name="gather_mix_wait")))
    w_small_t, w_big_t = _in_proj_weights(_rows(ge["w_in"]))
    wq_t, wk, wv = _mla_up_weights(_rows(ge["w_q_b"]), _cols(ge["w_kv_b"]))
    conv_full = _cols(ge["conv_w"])
    u_big = _mm(h2, w_big_t, name="in_big", tb=True)
    u_small = _mm(h2, w_small_t, name="in_small", tb=True)
    cqn, ckvn = _lora_norm_fwd(u_small, p["q_a_norm"], p["kv_a_norm"], name="lora_norm")
    q_raw = _mm(cqn, wq_t, name="q_up", tb=True)
    k_raw = _mm(ckvn, wk, name="k_up")
    v = _mm(ckvn, wv, name="v_up", out_dtype=BF16)
    q, k = _qk_prep_fwd(q_raw, k_raw, u_small, tabs, qg, kg, name="qk_prep")
    a_out, lse, g_late = _attn_fwd(q, k, v, ["gather"] * len(late), late_shards, name="attn_fwd")
    gl = dict(zip(late, g_late))
    w_pa, w_pb, w_o = _rows(gl["w_attn_branch"]), _rows(gl["w_ssm_branch"]), _rows(gl["w_out"])
    w_g2t, w_u2t = _rows(gl["ffn2_w_gate"]), _rows(gl["ffn2_w_up"])
    w_d2 = _rows(gl["ffn2_w_down"])
    xbc_act = _conv_fwd(u_big, conv_full, p["conv_b"], name="conv_fwd")
    y_f, hin_f = _ssd_fwd(xbc_act, u_small, bias128, alog128, rev=False, name="ssd_fwd_f")
    y_b, hin_b = _ssd_fwd(xbc_act, u_small, bias128, alog128, rev=True, name="ssd_fwd_b")
    m_out = _ssm_out_fwd(y_f, y_b, xbc_act, u_big, skip_x, p["ssm_norm"], name="ssm_out")
    pa = _mm(a_out, w_pa, name="branch_a")
    pb = _mm(m_out, w_pb, name="branch_b")
    merged = _merge_fwd(pa, pb, u_big, name="merge")
    x2 = _mm(merged, w_o, name="mix_out", res=x1)
    y, ffn2_saved = _ffn_fwd(x2, p["ffn2_norm"], w_g2t, w_u2t, w_d2, "ffn2")
    dy, loss_part = _loss_bwd(y, target, name="loss")
    loss = lax.psum(loss_part, ("x", "y", "c"))

    gs = {}
    dx2, gs["ffn2_norm"], g_gate2, g_up2, g_down2 = _ffn_bwd(dy, x2, p["ffn2_norm"], w_g2t, w_u2t, w_d2, ffn2_saved,
                                                             "ffn2b")
    dmerged = _mm(dx2, w_o, name="d_merged", tb=True)
    g_out = _mm(merged, dx2, name="d_w_out", ta=True, out_dtype=BF16)
    dpa, dpb, dga, dgb = _merge_bwd(dmerged, pa, pb, u_big, name="d_merge")
    g_pa = _mm(a_out, dpa, name="d_w_pa", ta=True, out_dtype=BF16)
    g_pb = _mm(m_out, dpb, name="d_w_pb", ta=True, out_dtype=BF16)
    da_out = _mm(dpa, w_pa, name="d_a", tb=True)
    dm_out = _mm(dpb, w_pb, name="d_m", tb=True)
    late_grads = [_shard_rows(g) for g in (g_pa, g_pb, g_out, g_gate2, g_up2, g_down2)]
    dq, dk, dv, r_late = _attn_bwd(q, k, v, a_out, lse, da_out, ["scatter"] * len(late_grads), late_grads,
                                   name="attn_bwd")
    recv = dict(zip(late, r_late))

    dyss, dz, gs["ssm_norm"], dskip_ch = _ssm_out_bwd(dm_out, y_f, y_b, xbc_act, u_big, skip_x, p["ssm_norm"],
                                                      name="d_ssm_out")
    dact_f, draw_f, dalog_f, dbias_f = _ssd_bwd(dyss, xbc_act, u_small, bias128, alog128, hin_f, skip_x,
                                                rev=False, name="ssd_bwd_f")
    dact_b, draw_b, dalog_b, dbias_b = _ssd_bwd(dyss, xbc_act, u_small, bias128, alog128, hin_b, None,
                                                rev=True, name="ssd_bwd_b")
    dxbc, g_conv, gs["conv_b"] = _conv_bwd(dact_f, dact_b, u_big, conv_full, p["conv_b"], name="conv_bwd")

    dq_raw, dk_raw, dkpe, gqh, gkh = _qk_prep_bwd(dq, dk, q_raw, k_raw, u_small, tabs, qg, kg, name="d_qk_prep")
    g_wq_t = _mm(dq_raw, cqn, name="d_w_q", ta=True, out_dtype=BF16)
    g_wk_t = _mm(dk_raw, ckvn, name="d_w_k", ta=True, out_dtype=BF16)
    g_wv_t = _mm(dv, ckvn, name="d_w_v", ta=True, out_dtype=BF16)
    dcqn = _mm(dq_raw, wq_t, name="d_cqn")
    dckvn = _mm(dk_raw, wk, name="d_ckvn_k", tb=True)
    dckvn = _mm(dv, wv, name="d_ckvn_v", tb=True, res=dckvn)
    du_small, gs["q_a_norm"], gkv = _lora_norm_bwd(dcqn, dckvn, u_small, p["q_a_norm"], p["kv_a_norm"], dkpe,
                                                   draw_f, draw_b, name="d_lora_norm")

    dh2 = _mm(du_small, w_small_t, name="d_h2_small")
    dh2 = _mm(dz, w_big_t, name="d_h2_z", b_row0=0, res=dh2)
    dh2 = _mm(dxbc, w_big_t, name="d_h2_xbc", b_row0=2048, res=dh2)
    dh2 = _mm(dga, w_big_t, name="d_h2_ga", b_row0=5120, res=dh2)
    dh2 = _mm(dgb, w_big_t, name="d_h2_gb", b_row0=6144, res=dh2)
    gt_small = _mm(du_small, h2, name="d_w_small", ta=True, out_dtype=BF16)
    gt_z = _mm(dz, h2, name="d_w_z", ta=True, out_dtype=BF16)
    gt_xbc = _mm(dxbc, h2, name="d_w_xbc", ta=True, out_dtype=BF16)
    gt_ga = _mm(dga, h2, name="d_w_ga", ta=True, out_dtype=BF16)
    gt_gb = _mm(dgb, h2, name="d_w_gb", ta=True, out_dtype=BF16)
    dx1, gs["mix_norm"] = _rms_bwd(dh2, x1, p["mix_norm"], dx2, name="d_mix_rms")

    gt_in = jnp.concatenate([gt_small[0:384], gt_small[U_CKV:U_KPE + QK_ROPE], gt_z, gt_xbc,
                             gt_small[U_DT:U_DT + 64], gt_ga, gt_gb], axis=0)
    gt_in = jnp.pad(gt_in.reshape(N_DEV, W_IN_SHARD, D_MODEL), ((0, 0), (0, W_IN_SHARD_PAD - W_IN_SHARD), (0, 0)))
    gt_q = g_wq_t.reshape(N_HEADS, LANES, Q_LORA)[:, :QK_HEAD].reshape(N_DEV, -1, Q_LORA)
    gk3 = g_wk_t.reshape(N_HEADS, LANES, KV_LORA)[:, :QK_NOPE]
    gv3 = g_wv_t.reshape(N_HEADS, LANES, KV_LORA)
    even = (jnp.arange(N_HEADS) % 2 == 0)[:, None, None]
    gv3 = jnp.where(even, gv3[:, :V_HEAD], gv3[:, V_HEAD:])
    gt_kv = jnp.concatenate([gk3, gv3], axis=1).reshape(N_DEV, -1, KV_LORA)
    mixg = ["w_in", "w_q_b", "w_kv_b"]
    grads_started, token = _exchange_start(["scatter"] * len(mixg), [gt_in, gt_q, gt_kv], name="grad_mix_start")
    ffn1_sent = []

    def send(names, grads):
        st, tok = _exchange_start(["scatter"] * len(grads), [_shard_rows(g) for g in grads],
                                  name="grad_ffn1_" + "_".join(names) + "_start")
        ffn1_sent.append((names, st))
        return tok

    grad_x, gs["ffn1_norm"], _, _, _ = _ffn_bwd(dx1, xs_in, p["ffn1_norm"] + token[0:1, 0:1], w_g1t, w_u1t, w_d1,
                                                ffn1_saved, "ffn1b", send=send)
    recv.update(zip(mixg, _exchange_wait(["scatter"] * len(mixg), grads_started, grad_x, name="grad_mix_wait")))
    for names, st in ffn1_sent:
        got = _exchange_wait(["scatter"] * len(names), st, grad_x, name="grad_ffn1_" + "_".join(names) + "_wait")
        recv.update(zip(["ffn1_w_" + n for n in names], got))

    gsmall = _small_slab(gs, dskip_ch, dalog_f, dalog_b, dbias_f, dbias_b, gkv, gqh, gkh, g_conv, name="small_slab")
    srecv = _exchange(["gather"], [gsmall], name="grad_exchange")[0]

    out = {}
    for n in ("ffn1_w_down", "ffn2_w_down", "w_attn_branch", "w_ssm_branch", "w_out", "ffn1_w_gate", "ffn1_w_up",
              "ffn2_w_gate", "ffn2_w_up", "w_q_b"):
        out[n] = _reduce_adamw(recv[n], w2[n], m2[n], v2[n], name=f"adamw_{n}")
    out["w_kv_b"] = _reduce_t_adamw(recv["w_kv_b"], w2["w_kv_b"], m2["w_kv_b"], v2["w_kv_b"], name="adamw_w_kv_b")
    g_in = _reduce8(recv["w_in"], name="sum_w_in", tile=W_IN_SHARD_PAD // 2)[:W_IN_SHARD]
    out["w_in"] = [g_in] + list(_adamw(g_in, w2["w_in"], m2["w_in"], v2["w_in"], name="adamw_w_in"))
    me = 4 * lax.axis_index("x") + 2 * lax.axis_index("y") + lax.axis_index("c")
    conv_g = lax.dynamic_slice(srecv, (0, CONV_ROW, me * (XBC_DIM // N_DEV)), (N_DEV, CONV_WIDTH, XBC_DIM // N_DEV))
    sn = [n for n, _ in SMALL] + ["conv_w"]
    sg, sd, sm, sv = _adamw_small(srecv, conv_g, [w2[n] for n in sn], [m2[n] for n in sn], [v2[n] for n in sn],
                                  name="adamw_small")
    for i, n in enumerate(sn):
        out[n] = (sg[i], sd[i], sm[i], sv[i])
    def back(n, a):
        if n in TRANSPOSED:
            return jnp.swapaxes(a.reshape(1, a.shape[0], a.shape[1]), 1, 2)
        return a.reshape(w_all[n].shape)

    outs = [[back(n, out[n][kind]) for n in WEIGHT_ORDER] for kind in range(4)]
    return (loss, grad_x[None], *outs[0], *outs[1], *outs[2], *outs[3])
```

```python
import math

import jax
import jax.numpy as jnp
from jax import lax
from jax.experimental import pallas as pl
from jax.experimental.pallas import tpu as pltpu

F32, BF16 = jnp.float32, jnp.bfloat16
HIGHEST = lax.Precision.HIGHEST

D_MODEL, D_FF = 1024, 2816
EPS = 1e-6
N_HEADS, QK_NOPE, QK_ROPE, QK_HEAD, V_HEAD = 16, 64, 32, 96, 64
Q_LORA, KV_LORA = 384, 256
ROPE_BASE = 10000.0
D_INNER, SSM_HEADS, SSM_GROUPS, D_STATE, CONV_WIDTH, CHUNK = 2048, 32, 4, 128, 5, 128
XBC_DIM = D_INNER + 2 * SSM_GROUPS * D_STATE
IN_DIM = 7904
ADAM_LR, ADAM_B1, ADAM_B2, ADAM_EPS, ADAM_WD, ADAM_STEP = 0.001, 0.9, 0.999, 1e-08, 0.01, 10
N_DEV = 8

V7X_VMEM_BYTES = 64 * 1024 * 1024
VMEM_LIMIT = V7X_VMEM_BYTES - 8 * 1024 * 1024
LANES = 128
W_IN_SHARD = IN_DIM // N_DEV
W_IN_SHARD_PAD = 992

SMALL = (
    ("ffn1_norm", 1024), ("mix_norm", 1024), ("q_a_norm", 384), ("kv_a_norm", 256), ("q_head_norm", 96),
    ("k_head_norm", 96), ("conv_b", 3072), ("a_log_fwd", 32), ("a_log_bwd", 32), ("dt_bias_fwd", 32),
    ("dt_bias_bwd", 32), ("d_skip", 32), ("ssm_norm", 2048), ("ffn2_norm", 1024),
)
TRANSPOSED = ("ffn1_w_gate", "ffn1_w_up", "ffn2_w_gate", "ffn2_w_up", "w_in", "w_q_b")
SMALL_ROW = {n: i for i, (n, _) in enumerate(SMALL)}
CONV_ROW = len(SMALL)
SMALL_ROWS, SMALL_COLS = 24, XBC_DIM
WEIGHT_ORDER = (
    "ffn1_norm", "ffn1_w_gate", "ffn1_w_up", "ffn1_w_down", "mix_norm", "w_in", "q_a_norm", "w_q_b", "kv_a_norm",
    "w_kv_b", "q_head_norm", "k_head_norm", "conv_w", "conv_b", "a_log_fwd", "a_log_bwd", "dt_bias_fwd", "dt_bias_bwd",
    "d_skip", "ssm_norm", "w_attn_branch", "w_ssm_branch", "w_out", "ffn2_norm", "ffn2_w_gate", "ffn2_w_up",
    "ffn2_w_down",
)


def _pallas(body, **kw):
    return pl.pallas_call(body, **kw)


def _params(sem):
    return pltpu.CompilerParams(dimension_semantics=sem, vmem_limit_bytes=VMEM_LIMIT)


def _pick(dim, pref):
    if dim <= pref:
        return dim
    c = (pref // LANES) * LANES
    while c >= LANES:
        if dim % c == 0:
            return c
        c -= LANES
    raise ValueError((dim, pref))


def _sigmoid(x):
    return 1.0 / (1.0 + jnp.exp(-x))


def _softplus(x):
    return jnp.maximum(x, 0.0) + jnp.log(1.0 + jnp.exp(-jnp.abs(x)))


def _dot(a, b):
    return jnp.dot(a, b, preferred_element_type=F32)


def _dot_nt(a, b):
    return lax.dot_general(a, b, (((1,), (1,)), ((), ())), preferred_element_type=F32)


def _dot_tn(a, b):
    return lax.dot_general(a, b, (((0,), (0,)), ((), ())), preferred_element_type=F32)


def _dot_h(a, b):
    return jnp.dot(a, b, preferred_element_type=F32, precision=HIGHEST)


def _dot_h_nt(a, b):
    return lax.dot_general(a, b, (((1,), (1,)), ((), ())), preferred_element_type=F32, precision=HIGHEST)


def _dot_h_tn(a, b):
    return lax.dot_general(a, b, (((0,), (0,)), ((), ())), preferred_element_type=F32, precision=HIGHEST)


def _mesh_pos():
    return lax.axis_index("x"), lax.axis_index("y"), lax.axis_index("c")


def _comm_scratch(n):
    return [pltpu.SemaphoreType.DMA((7 * n,)), pltpu.SemaphoreType.DMA((7 * n,)), pltpu.SemaphoreType.DMA((n,))]


def _comm_copies(modes, srcs, dsts, send_sems, recv_sems, local_sems, arrivals):
    x, y, c = _mesh_pos()
    me = 4 * x + 2 * y + c
    local, remote = [], []
    for w, (mode, s, d) in enumerate(zip(modes, srcs, dsts)):
        gather = mode == "gather"
        if not arrivals and local_sems is not None:
            local.append(pltpu.make_async_copy(s if gather else s.at[me], d.at[me], local_sems.at[w]))
        for k in range(1, N_DEV):
            px = (1 - x) if (k & 4) else x
            py = (1 - y) if (k & 2) else y
            pc = (1 - c) if (k & 1) else c
            peer = 4 * px + 2 * py + pc
            idx = 7 * w + k - 1
            remote.append(pltpu.make_async_remote_copy(
                src_ref=s if gather else s.at[peer], dst_ref=d.at[peer] if arrivals else d.at[me],
                send_sem=send_sems.at[idx], recv_sem=recv_sems.at[idx],
                device_id=(px, py, pc), device_id_type=pl.DeviceIdType.MESH))
    return local, remote


def _comm_start(modes, srcs, dsts, sems):
    local, sends = _comm_copies(modes, srcs, dsts, *sems, arrivals=False)
    for cp in local + sends:
        cp.start()


def _comm_wait(modes, srcs, dsts, sems):
    _, recvs = _comm_copies(modes, srcs, dsts, *sems, arrivals=True)
    for cp in recvs:
        cp.wait_recv()
    local, sends = _comm_copies(modes, srcs, dsts, *sems, arrivals=False)
    for cp in sends:
        cp.wait_send()
    for cp in local:
        cp.wait()


def _comm_out_shapes(modes, arrays):
    return [jax.ShapeDtypeStruct((N_DEV,) + (a.shape if m == "gather" else a.shape[1:]), a.dtype)
            for m, a in zip(modes, arrays)]


def _exchange(modes, arrays, *, name):
    n = len(arrays)

    def body(*refs):
        srcs, dsts, sems = refs[:n], refs[n:2 * n], refs[2 * n:]
        _comm_start(modes, srcs, dsts, sems)
        _comm_wait(modes, srcs, dsts, sems)

    any_spec = pl.BlockSpec(memory_space=pl.ANY)
    return _pallas(body, name=name, out_shape=_comm_out_shapes(modes, arrays), in_specs=[any_spec] * n,
                   out_specs=[any_spec] * n, scratch_shapes=_comm_scratch(n))(*arrays)


def _exchange_start(modes, arrays, *, name):
    n = len(arrays)
    me = 4 * lax.axis_index("x") + 2 * lax.axis_index("y") + lax.axis_index("c")
    lands = []
    for m, a in zip(modes, arrays):
        own = a if m == "gather" else lax.dynamic_index_in_dim(a, me, 0, keepdims=False)
        zone = lax.empty((N_DEV,) + own.shape, a.dtype)
        lands.append(lax.dynamic_update_index_in_dim(zone, own, me, 0))

    def body(*refs):
        srcs, dsts = refs[:n], refs[n:2 * n]
        send_sems, recv_sems = refs[2 * n], refs[2 * n + 1]
        token = refs[-1]
        _, sends = _comm_copies(modes, srcs, dsts, send_sems, recv_sems, None, arrivals=False)
        for cp in sends:
            cp.start()
        token[...] = jnp.zeros_like(token)

    hbm = pl.BlockSpec(memory_space=pltpu.HBM)
    sem = pl.BlockSpec(memory_space=pltpu.SEMAPHORE)
    ins = [pltpu.with_memory_space_constraint(a, pltpu.HBM) for a in list(arrays) + lands]
    got = _pallas(
        body, name=name,
        out_shape=(pltpu.SemaphoreType.DMA((7 * n,)), pltpu.SemaphoreType.DMA((7 * n,)),
                   *[pltpu.HBM(a.shape, a.dtype) for a in ins], jax.ShapeDtypeStruct((8, LANES), F32)),
        in_specs=[hbm] * (2 * n), out_specs=(sem, sem, *[hbm] * (2 * n), pl.BlockSpec(memory_space=pltpu.VMEM)),
        input_output_aliases={i: 2 + i for i in range(2 * n)},
        compiler_params=pltpu.CompilerParams(has_side_effects=pltpu.SideEffectType.DATAFLOW_SIDE_EFFECTING),
    )(*ins)
    return (got[0], got[1], got[2:2 + n], got[2 + n:2 + 2 * n]), got[-1]


def _exchange_wait(modes, started, after, *, name):
    send_sems, recv_sems, srcs, lands = started
    n = len(srcs)

    def body(*refs):
        src_refs, dst_refs = refs[:n], refs[n:2 * n]
        ssem, rsem = refs[2 * n], refs[2 * n + 1]
        _, recvs = _comm_copies(modes, src_refs, dst_refs, ssem, rsem, None, arrivals=True)
        for cp in recvs:
            cp.wait_recv()
        _, sends = _comm_copies(modes, src_refs, dst_refs, ssem, rsem, None, arrivals=False)
        for cp in sends:
            cp.wait_send()

    hbm = pl.BlockSpec(memory_space=pltpu.HBM)
    sem = pl.BlockSpec(memory_space=pltpu.SEMAPHORE)
    both = list(srcs) + list(lands)
    got = _pallas(
        body, name=name, out_shape=tuple(pltpu.HBM(a.shape, a.dtype) for a in both),
        in_specs=[hbm] * (2 * n) + [sem, sem, pl.BlockSpec(memory_space=pl.ANY)], out_specs=tuple([hbm] * (2 * n)),
        input_output_aliases={i: i for i in range(2 * n)},
        compiler_params=pltpu.CompilerParams(has_side_effects=pltpu.SideEffectType.DATAFLOW_SIDE_EFFECTING),
    )(*both, send_sems, recv_sems, after)
    return got[n:]


def _all_gather_two_level(shards, *, name):
    n = len(shards)

    def body(*refs):
        srcs, outs = refs[:n], refs[n:2 * n]
        send_sems, recv_sems, local_sems = refs[2 * n:]
        x, y, c = _mesh_pos()
        me, sibling = (x, y, c), (x, y, 1 - c)
        chips = [(1 - x, y), (x, 1 - y), (1 - x, 1 - y)]

        def blk(w, px, py, pc):
            return outs[w].at[4 * px + 2 * py + pc]

        def copy(w, k, block, to, src=None):
            return pltpu.make_async_remote_copy(
                src_ref=blk(w, *block) if src is None else src, dst_ref=blk(w, *block),
                send_sem=send_sems.at[7 * w + k], recv_sem=recv_sems.at[7 * w + k], device_id=to,
                device_id_type=pl.DeviceIdType.MESH)

        mine = [pltpu.make_async_copy(srcs[w], blk(w, *me), local_sems.at[w]) for w in range(n)]
        for cp in mine:
            cp.start()
        first = []
        for w in range(n):
            first.append(copy(w, 0, me, sibling, src=srcs[w]))
            first += [copy(w, 1 + j, me, (*chip, c), src=srcs[w]) for j, chip in enumerate(chips)]
        for cp in first:
            cp.start()
        passed = []
        for w in range(n):
            for j, chip in enumerate(chips):
                copy(w, 1 + j, (*chip, c), me).wait_recv()
                fwd = copy(w, 4 + j, (*chip, c), sibling)
                fwd.start()
                passed.append(fwd)
        for w in range(n):
            copy(w, 0, sibling, me).wait_recv()
            for j, chip in enumerate(chips):
                copy(w, 4 + j, (*chip, 1 - c), me).wait_recv()
        for cp in first + passed:
            cp.wait_send()
        for cp in mine:
            cp.wait()

    any_spec = pl.BlockSpec(memory_space=pl.ANY)
    return _pallas(body, name=name, out_shape=_comm_out_shapes(["gather"] * n, shards), in_specs=[any_spec] * n,
                   out_specs=[any_spec] * n, scratch_shapes=_comm_scratch(n))(*shards)


def _mm(a, b, *, name, ta=False, tb=False, out_dtype=F32, alpha=1.0, res=None, tm=1024, tn=1408, tk=1408,
        b_row0=None, after=None):
    (K, M) = a.shape if ta else a.shape[::-1]
    (N, Kb) = b.shape if tb else b.shape[::-1]
    tm, tn, tk = _pick(M, tm), _pick(N, tn), _pick(K, tk)
    nk = K // tk
    if b_row0 is None:
        assert K == Kb, (a.shape, b.shape, ta, tb)
        kb0 = 0
    else:
        assert not tb and b_row0 % tk == 0 and b_row0 + K <= Kb, (a.shape, b.shape, b_row0)
        kb0 = b_row0 // tk
    a_spec = pl.BlockSpec((tk, tm), lambda i, j, k: (k, i)) if ta else pl.BlockSpec((tm, tk), lambda i, j, k: (i, k))
    b_spec = (pl.BlockSpec((tn, tk), lambda i, j, k: (j, k)) if tb
              else pl.BlockSpec((tk, tn), lambda i, j, k: (k + kb0, j)))
    o_spec = pl.BlockSpec((tm, tn), lambda i, j, k: (i, j))
    dn = (((0 if ta else 1,), (1 if tb else 0,)), ((), ()))
    has_res = res is not None
    n_in = 2 + has_res + (after is not None)

    def body(*refs):
        a_ref, b_ref = refs[0], refs[1]
        r_ref = refs[2] if has_res else None
        o_ref = refs[n_in]
        part =lax.dot_general(a_ref[...].astype(BF16), b_ref[...].astype(BF16), dn, preferred_element_type=F32)

        def finish(acc):
            if alpha != 1.0:
                acc = acc * alpha
            if has_res:
                acc = acc + r_ref[...]
            o_ref[...] = acc.astype(o_ref.dtype)

        if nk == 1:
            finish(part)
        else:
            acc_ref = refs[-1]
            k = pl.program_id(2)

            @pl.when(k == 0)
            def _():
                acc_ref[...] = part

            @pl.when(k > 0)
            def _():
                acc_ref[...] += part

            @pl.when(k == nk - 1)
            def _():
                finish(acc_ref[...])

    ins = [a, b] + ([res] if has_res else [])
    in_specs = [a_spec, b_spec] + ([o_spec] if has_res else [])
    if after is not None:
        ins.append(after)
        in_specs.append(pl.BlockSpec(after.shape, lambda i, j, k: (0, 0)))
    return _pallas(
        body, name=name, grid=(M // tm, N // tn, nk), in_specs=in_specs, out_specs=o_spec,
        out_shape=jax.ShapeDtypeStruct((M, N), out_dtype),
        scratch_shapes=[pltpu.VMEM((tm, tn), F32)] if nk > 1 else [],
        compiler_params=_params(("parallel", "parallel", "arbitrary")),
    )(*ins)


def _col0(j):
    return 0


def _colj(j):
    return j


def _rowmap(fn, *, name, rows, tile, ins, consts=(), outs=(), accs=(), ncol=1):
    tile = min(tile, rows)
    nrow = rows // tile
    in_specs = [pl.BlockSpec((tile, w), lambda j, i, f=f: (i, f(j))) for _, w, f in ins]
    for arr, w, f in consts:
        in_specs.append(pl.BlockSpec((arr.shape[0], w), lambda j, i, f=f: (0, f(j))))
    out_specs = [pl.BlockSpec((tile, w), lambda j, i, f=f: (i, f(j))) for _, _, w, f in outs]
    out_specs += [pl.BlockSpec((1, w), lambda j, i, f=f: (0, f(j))) for _, w, f in accs]
    out_shape = [jax.ShapeDtypeStruct((rows, c), dt) for c, dt, _, _ in outs]
    out_shape += [jax.ShapeDtypeStruct((1, c), F32) for c, _, _ in accs]
    n_in, n_out = len(ins) + len(consts), len(outs)
    acc_fixed = [f is _col0 for _, _, f in accs]

    def body(*refs):
        res = fn(*[r[...].astype(F32) for r in refs[:n_in]])
        if not isinstance(res, (tuple, list)):
            res = (res,)
        for r, v in zip(refs[n_in:n_in + n_out], res[:n_out]):
            r[...] = v.astype(r.dtype)
        j, i = pl.program_id(0), pl.program_id(1)
        for r, v, fixed in zip(refs[n_in + n_out:], res[n_out:], acc_fixed):
            first = ((i == 0) & (j == 0)) if fixed else (i == 0)

            @pl.when(first)
            def _(r=r, v=v):
                r[...] = v

            @pl.when(jnp.logical_not(first))
            def _(r=r, v=v):
                r[...] += v

    arrays = [a for a, _, _ in ins] + [a for a, _, _ in consts]
    return _pallas(
        body, name=name, grid=(ncol, nrow), in_specs=in_specs, out_specs=out_specs, out_shape=out_shape,
        compiler_params=_params(("arbitrary", "arbitrary")),
    )(*arrays)


def _rms_fwd(x, g, *, name, tile=512):
    rows, d = x.shape

    def fn(xv, gv):
        r = lax.rsqrt(jnp.mean(xv * xv, axis=-1, keepdims=True) + EPS)
        return xv * r * gv

    return _rowmap(fn, name=name, rows=rows, tile=tile, ins=[(x, d, _col0)], consts=[(g, d, _col0)],
                   outs=[(d, BF16, d, _col0)])[0]


def _rms_bwd(dh, x, g, res, *, name, tile=512):
    rows, d = x.shape

    def fn(dhv, xv, rv, gv):
        r = lax.rsqrt(jnp.mean(xv * xv, axis=-1, keepdims=True) + EPS)
        xh = xv * r
        dxh = dhv * gv
        dx = r * (dxh - xh * jnp.mean(dxh * xh, axis=-1, keepdims=True))
        return rv + dx, jnp.sum(dhv * xh, axis=0, keepdims=True)

    return _rowmap(fn, name=name, rows=rows, tile=tile, ins=[(dh, d, _col0), (x, d, _col0), (res, d, _col0)],
                   consts=[(g, d, _col0)], outs=[(d, F32, d, _col0)], accs=[(d, d, _col0)])


def _swiglu_fwd(gu, *, name, tile=512):
    rows = gu.shape[0]
    w = _pick(D_FF, 1408)
    nb = D_FF // w

    def fn(gv, uv):
        return gv * _sigmoid(gv) * uv

    return _rowmap(fn, name=name, rows=rows, tile=tile, ncol=nb,
                   ins=[(gu, w, _colj), (gu, w, lambda j: j + nb)], outs=[(D_FF, BF16, w, _colj)])[0]


def _swiglu_bwd(da, gu, *, name, tile=512):
    rows = gu.shape[0]
    w = _pick(D_FF, 1408)
    nb = D_FF // w

    def fn(dav, gv, uv):
        sg = _sigmoid(gv)
        dg = dav * uv * (sg * (1.0 + gv * (1.0 - sg)))
        du = dav * (gv * sg)
        return dg, du

    return _rowmap(fn, name=name, rows=rows, tile=tile, ncol=nb,
                   ins=[(da, w, _colj), (gu, w, _colj), (gu, w, lambda j: j + nb)],
                   outs=[(D_FF, BF16, w, _colj), (D_FF, BF16, w, _colj)])


U_CKV, U_KPE, U_DT = 512, 768, 896


def _lora_norm_fwd(u_small, qg, kvg, *, name, tile=512):
    rows = u_small.shape[0]

    def fn(cq, ckv, qgv, kgv):
        rq = lax.rsqrt(jnp.mean(cq * cq, axis=-1, keepdims=True) + EPS)
        rk = lax.rsqrt(jnp.mean(ckv * ckv, axis=-1, keepdims=True) + EPS)
        return cq * rq * qgv, ckv * rk * kgv

    return _rowmap(fn, name=name, rows=rows, tile=tile,
                   ins=[(u_small, Q_LORA, _col0), (u_small, KV_LORA, lambda j: U_CKV // KV_LORA)],
                   consts=[(qg, Q_LORA, _col0), (kvg, KV_LORA, _col0)],
                   outs=[(Q_LORA, BF16, Q_LORA, _col0), (KV_LORA, BF16, KV_LORA, _col0)])


def _lora_norm_bwd(dcqn, dckvn, u_small, qg, kvg, dkpe, draw_f, draw_b, *, name, tile=512):
    rows = u_small.shape[0]
    tile = min(tile, rows)

    def body(dq_ref, dk_ref, u_ref, dkp_ref, df_ref, db_ref, qg_ref, kg_ref, du_ref, gq_ref, gk_ref):
        cq, ckv = u_ref[:, 0:Q_LORA], u_ref[:, U_CKV:U_CKV + KV_LORA]
        dq, dk = dq_ref[...], dk_ref[...]
        rq = lax.rsqrt(jnp.mean(cq * cq, axis=-1, keepdims=True) + EPS)
        xh = cq * rq
        dxh = dq * qg_ref[...]
        du_ref[:, 0:Q_LORA] = (rq * (dxh - xh * jnp.mean(dxh * xh, axis=-1, keepdims=True))).astype(BF16)
        du_ref[:, Q_LORA:U_CKV] = jnp.zeros((tile, U_CKV - Q_LORA), BF16)
        rk = lax.rsqrt(jnp.mean(ckv * ckv, axis=-1, keepdims=True) + EPS)
        kh = ckv * rk
        dkh = dk * kg_ref[...]
        du_ref[:, U_CKV:U_KPE] = (rk * (dkh - kh * jnp.mean(dkh * kh, axis=-1, keepdims=True))).astype(BF16)
        du_ref[:, U_KPE:U_DT] = dkp_ref[...].astype(BF16)
        du_ref[:, U_DT:U_DT + LANES] = (df_ref[...] + db_ref[...]).astype(BF16)
        gq = jnp.sum(dq * xh, axis=0, keepdims=True)
        gk = jnp.sum(dk * kh, axis=0, keepdims=True)
        i = pl.program_id(0)

        @pl.when(i == 0)
        def _():
            gq_ref[...] = gq
            gk_ref[...] = gk

        @pl.when(i > 0)
        def _():
            gq_ref[...] += gq
            gk_ref[...] += gk

    def rowblk(w):
        return pl.BlockSpec((tile, w), lambda i: (i, 0))

    def whole(w):
        return pl.BlockSpec((1, w), lambda i: (0, 0))

    return _pallas(
        body, name=name, grid=(rows // tile,),
        in_specs=[rowblk(Q_LORA), rowblk(KV_LORA), rowblk(1024), rowblk(LANES), rowblk(LANES), rowblk(LANES),
                  whole(Q_LORA), whole(KV_LORA)],
        out_specs=[rowblk(1024), whole(Q_LORA), whole(KV_LORA)],
        out_shape=[jax.ShapeDtypeStruct((rows, 1024), BF16), jax.ShapeDtypeStruct((1, Q_LORA), F32),
                   jax.ShapeDtypeStruct((1, KV_LORA), F32)],
        compiler_params=_params(("arbitrary",)),
    )(dcqn, dckvn, u_small, dkpe, draw_f, draw_b, qg, kvg)


def _rope(x, c, s1, s2):
    return x * c + pltpu.roll(x, 112, 1) * s1 + pltpu.roll(x, 16, 1) * s2


def _rope_t(d, c, s1, s2):
    return d * c + pltpu.roll(d * s1, 16, 1) + pltpu.roll(d * s2, 112, 1)


def _qk_prep_fwd(q_raw, k_raw, u_small, tabs, qg, kg, *, name, tile=256):
    rows = q_raw.shape[0]
    tile = min(tile, rows)
    scale = 1.0 / math.sqrt(QK_HEAD)

    def body(q_ref, k_ref, u_ref, c_ref, s1_ref, s2_ref, qg_ref, kg_ref, qo_ref, ko_ref):
        c, s1, s2 = c_ref[...], s1_ref[...], s2_ref[...]
        qgv, kgv = qg_ref[...], kg_ref[...]
        kpe = pltpu.roll(u_ref[:, U_KPE:U_KPE + LANES], 64, 1)
        for h in range(N_HEADS):
            hs = slice(h * LANES, (h + 1) * LANES)
            qr = q_ref[:, hs]
            rq = lax.rsqrt(jnp.sum(qr * qr, axis=-1, keepdims=True) / QK_HEAD + EPS)
            qo_ref[:, hs] = (_rope(qr * rq * qgv, c, s1, s2) * scale).astype(BF16)
            xk = k_ref[:, hs] + kpe
            rk = lax.rsqrt(jnp.sum(xk * xk, axis=-1, keepdims=True) / QK_HEAD + EPS)
            ko_ref[:, hs] = _rope(xk * rk * kgv, c, s1, s2).astype(BF16)

    wide = pl.BlockSpec((tile, 2048), lambda i: (i, 0))
    narrow = pl.BlockSpec((tile, LANES), lambda i: (i, 0))
    gain = pl.BlockSpec((1, LANES), lambda i: (0, 0))
    return _pallas(
        body, name=name, grid=(rows // tile,),
        in_specs=[wide, wide, pl.BlockSpec((tile, 1024), lambda i: (i, 0)), narrow, narrow, narrow, gain, gain],
        out_specs=[wide, wide], out_shape=[jax.ShapeDtypeStruct((rows, 2048), BF16)] * 2,
        compiler_params=_params(("parallel",)),
    )(q_raw, k_raw, u_small, *tabs, qg, kg)


def _qk_prep_bwd(dq, dk, q_raw, k_raw, u_small, tabs, qg, kg, *, name, tile=256):
    rows = q_raw.shape[0]
    tile = min(tile, rows)
    scale = 1.0 / math.sqrt(QK_HEAD)

    def body(dq_ref, dk_ref, q_ref, k_ref, u_ref, c_ref, s1_ref, s2_ref, qg_ref, kg_ref,
             dqo_ref, dko_ref, dkpe_ref, gq_ref, gk_ref):
        c, s1, s2 = c_ref[...], s1_ref[...], s2_ref[...]
        qgv, kgv = qg_ref[...], kg_ref[...]
        kpe = pltpu.roll(u_ref[:, U_KPE:U_KPE + LANES], 64, 1)
        lane = lax.broadcasted_iota(jnp.int32, (tile, LANES), 1)
        gq = jnp.zeros((1, LANES), F32)
        gk = jnp.zeros((1, LANES), F32)
        dkpe = jnp.zeros((tile, LANES), F32)
        for h in range(N_HEADS):
            hs = slice(h * LANES, (h + 1) * LANES)
            qr = q_ref[:, hs]
            rq = lax.rsqrt(jnp.sum(qr * qr, axis=-1, keepdims=True) / QK_HEAD + EPS)
            xh = qr * rq
            dy = _rope_t(dq_ref[:, hs] * scale, c, s1, s2)
            dxh = dy * qgv
            dqo_ref[:, hs] = (rq * (dxh - xh * (jnp.sum(dxh * xh, axis=-1, keepdims=True) / QK_HEAD))).astype(BF16)
            gq = gq + jnp.sum(dy * xh, axis=0, keepdims=True)
            xk = k_ref[:, hs] + kpe
            rk = lax.rsqrt(jnp.sum(xk * xk, axis=-1, keepdims=True) / QK_HEAD + EPS)
            kh = xk * rk
            dyk = _rope_t(dk_ref[:, hs], c, s1, s2)
            dkh = dyk * kgv
            dxk = rk * (dkh - kh * (jnp.sum(dkh * kh, axis=-1, keepdims=True) / QK_HEAD))
            gk = gk + jnp.sum(dyk * kh, axis=0, keepdims=True)
            dko_ref[:, hs] = jnp.where(lane < QK_NOPE, dxk, 0.0).astype(BF16)
            dkpe = dkpe + dxk
        dkpe_ref[...] = jnp.where(lane < QK_ROPE, pltpu.roll(dkpe, 64, 1), 0.0)
        i = pl.program_id(0)

        @pl.when(i == 0)
        def _():
            gq_ref[...] = gq
            gk_ref[...] = gk

        @pl.when(i > 0)
        def _():
            gq_ref[...] += gq
            gk_ref[...] += gk

    wide = pl.BlockSpec((tile, 2048), lambda i: (i, 0))
    narrow = pl.BlockSpec((tile, LANES), lambda i: (i, 0))
    gain = pl.BlockSpec((1, LANES), lambda i: (0, 0))
    return _pallas(
        body, name=name, grid=(rows // tile,),
        in_specs=[wide, wide, wide, wide, pl.BlockSpec((tile, 1024), lambda i: (i, 0)), narrow, narrow, narrow,
                  gain, gain],
        out_specs=[wide, wide, narrow, gain, gain],
        out_shape=[jax.ShapeDtypeStruct((rows, 2048), BF16)] * 2
        + [jax.ShapeDtypeStruct((rows, LANES), F32), jax.ShapeDtypeStruct((1, LANES), F32),
           jax.ShapeDtypeStruct((1, LANES), F32)],
        compiler_params=_params(("arbitrary",)),
    )(dq, dk, q_raw, k_raw, u_small, *tabs, qg, kg)


def _attn_fwd(q, k, v, comm_modes, comm_arrays, *, name, tq=512, tkc=512):
    T = q.shape[0]
    tq = min(tq, T)
    tkc = min(tkc, T)
    n = len(comm_arrays)
    nj, ni = N_HEADS // 2, T // tq

    def body(*refs):
        q_ref, k_ref, v_ref = refs[:3]
        srcs = refs[3:3 + n]
        o_ref, lse_ref = refs[3 + n:5 + n]
        dsts = refs[5 + n:5 + 2 * n]
        sems = refs[5 + 2 * n:]
        j, i = pl.program_id(0), pl.program_id(1)

        @pl.when((j == 0) & (i == 0))
        def _():
            _comm_start(comm_modes, srcs, dsts, sems)

        lane = lax.broadcasted_iota(jnp.int32, (1, LANES), 1)
        out = None
        for hh in range(2):
            sl = slice(hh * LANES, (hh + 1) * LANES)
            qv = q_ref[:, sl]
            spare = LANES - 1 if hh == 0 else 0
            keep = (lane < V_HEAD) if hh == 0 else (lane >= V_HEAD)
            m = acc = None
            for kc in range(T // tkc):
                ks = slice(kc * tkc, (kc + 1) * tkc)
                s = _dot_nt(qv, k_ref[ks, sl])
                vone = jnp.where(lane == spare, 1.0, v_ref[ks, sl]).astype(BF16)
                mc = jnp.max(s, axis=-1, keepdims=True)
                if m is None:
                    m = mc
                    acc = _dot(jnp.exp(s - m).astype(BF16), vone)
                else:
                    m_new = jnp.maximum(m, mc)
                    acc = jnp.exp(m - m_new) * acc + _dot(jnp.exp(s - m_new).astype(BF16), vone)
                    m = m_new
            l = acc[:, spare:spare + 1]
            o = jnp.where(keep, acc / l, 0.0)
            out = o if out is None else out + o
            lse_ref[hh] = m + jnp.log(l)
        o_ref[...] = out

        @pl.when((j == nj - 1) & (i == ni - 1))
        def _():
            _comm_wait(comm_modes, srcs, dsts, sems)

    any_spec = pl.BlockSpec(memory_space=pl.ANY)
    got = _pallas(
        body, name=name, grid=(nj, ni),
        in_specs=[pl.BlockSpec((tq, 2 * LANES), lambda j, i: (i, j)), pl.BlockSpec((T, 2 * LANES), lambda j, i: (0, j)),
                  pl.BlockSpec((T, 2 * LANES), lambda j, i: (0, j))] + [any_spec] * n,
        out_specs=[pl.BlockSpec((tq, LANES), lambda j, i: (i, j)), pl.BlockSpec((2, tq, 1), lambda j, i: (j, i, 0))]
        + [any_spec] * n,
        out_shape=[jax.ShapeDtypeStruct((T, N_HEADS * V_HEAD), F32), jax.ShapeDtypeStruct((N_HEADS, T, 1), F32)]
        + _comm_out_shapes(comm_modes, comm_arrays),
        scratch_shapes=_comm_scratch(n),
        compiler_params=_params(("arbitrary", "arbitrary")),
    )(q, k, v, *comm_arrays)
    return got[0], got[1], got[2:]


def _attn_bwd(q, k, v, o, lse, do, comm_modes, comm_arrays, *, name, tk=256, tqc=4096):
    T = q.shape[0]
    tk = min(tk, T)
    tqc = min(tqc, T)
    n = len(comm_arrays)
    nj, nkb = N_HEADS // 2, T // tk

    def body(*refs):
        q_ref, k_ref, v_ref, o_ref, lse_ref, do_ref = refs[:6]
        srcs = refs[6:6 + n]
        dq_ref, dk_ref, dv_ref = refs[6 + n:9 + n]
        dsts = refs[9 + n:9 + 2 * n]
        d_s = refs[9 + 2 * n]
        sems = refs[10 + 2 * n:]
        j, kb = pl.program_id(0), pl.program_id(1)

        @pl.when((j == 0) & (kb == 0))
        def _():
            _comm_start(comm_modes, srcs, dsts, sems)

        lane = lax.broadcasted_iota(jnp.int32, (1, LANES), 1)
        @pl.when(kb == 0)
        def _():
            prod = do_ref[...] * o_ref[...]
            for hh in range(2):
                keep = (lane < V_HEAD) if hh == 0 else (lane >= V_HEAD)
                d_s[hh] = jnp.sum(jnp.where(keep, prod, 0.0), axis=-1, keepdims=True)

        for hh in range(2):
            sl = slice(hh * LANES, (hh + 1) * LANES)
            keep = (lane < V_HEAD) if hh == 0 else (lane >= V_HEAD)
            kv, vv = k_ref[:, sl], v_ref[:, sl]
            dv_acc = dk_acc = None
            for qc in range(T // tqc):
                qs = slice(qc * tqc, (qc + 1) * tqc)
                qv = q_ref[qs, sl]
                do_b = do_ref[qs, :].astype(BF16)
                s = _dot_nt(qv, kv)
                p = jnp.exp(s - lse_ref[hh, qs])
                dp = _dot_nt(do_b, vv)
                ds = (p * (dp - d_s[hh, qs])).astype(BF16)
                dvc = _dot_tn(p.astype(BF16), do_b)
                dkc = _dot_tn(ds, qv)
                dv_acc = dvc if dv_acc is None else dv_acc + dvc
                dk_acc = dkc if dk_acc is None else dk_acc + dkc
                dqp = _dot(ds, kv)

                @pl.when(kb == 0)
                def _(dqp=dqp, sl=sl, qs=qs):
                    dq_ref[qs, sl] = dqp

                @pl.when(kb > 0)
                def _(dqp=dqp, sl=sl, qs=qs):
                    dq_ref[qs, sl] += dqp

            dv_ref[:, sl] = jnp.where(keep, dv_acc, 0.0).astype(BF16)
            dk_ref[:, sl] = dk_acc

        @pl.when((j == nj - 1) & (kb == nkb - 1))
        def _():
            _comm_wait(comm_modes, srcs, dsts, sems)

    any_spec = pl.BlockSpec(memory_space=pl.ANY)
    pair = pl.BlockSpec((T, 2 * LANES), lambda j, kb: (0, j))
    kblk = pl.BlockSpec((tk, 2 * LANES), lambda j, kb: (kb, j))
    got = _pallas(
        body, name=name, grid=(nj, nkb),
        in_specs=[pair, kblk, kblk, pl.BlockSpec((T, LANES), lambda j, kb: (0, j)),
                  pl.BlockSpec((2, T, 1), lambda j, kb: (j, 0, 0)), pl.BlockSpec((T, LANES), lambda j, kb: (0, j))]
        + [any_spec] * n,
        out_specs=[pair, kblk, kblk] + [any_spec] * n,
        out_shape=[jax.ShapeDtypeStruct((T, 2048), F32)] * 2 + [jax.ShapeDtypeStruct((T, 2048), BF16)]
        + _comm_out_shapes(comm_modes, comm_arrays),
        scratch_shapes=[pltpu.VMEM((2, T, 1), F32)] + _comm_scratch(n),
        compiler_params=_params(("arbitrary", "arbitrary")),
    )(q, k, v, o, lse, do, *comm_arrays)
    return got[0], got[1], got[2], got[3:]


CONV_ROWS, CONV_HALO = 64, 8
CONV_WIN = CONV_ROWS + 2 * CONV_HALO


def _conv_shift(x, sh, t_idx, total):
    if sh == 0:
        return x
    y = pltpu.roll(x, (-sh) % x.shape[0], 0)
    if t_idx is None:
        return y
    ok = (t_idx + sh >= 0) & (t_idx + sh < total)
    return jnp.where(ok, y, 0.0)


def _conv_positions(ws, shape):
    return ws + lax.broadcasted_iota(jnp.int32, shape, 0) if isinstance(ws, int) else None


def _aligned(v, m):
    return v if isinstance(v, int) else pl.multiple_of(v, m)


def _conv_chunks(T, chunk, carry):
    n = T // CONV_ROWS
    carry = chunk(0, 0, carry)

    def mid(ci, c):
        return chunk(pl.multiple_of(ci * CONV_ROWS - CONV_HALO, CONV_HALO), CONV_HALO, c)

    carry = lax.fori_loop(1, n - 1, mid, carry)
    return chunk(T - CONV_WIN, 2 * CONV_HALO, carry)


def _conv_pre(x, w_ref, b_ref, t_idx, total):
    pre = b_ref[...] + w_ref[2:3, :] * x
    for j in (0, 1, 3, 4):
        pre = pre + w_ref[j:j + 1, :] * _conv_shift(x, j - 2, t_idx, total)
    return pre


def _conv_fwd(u_big, conv_w, conv_b, *, name, w=256):
    T = u_big.shape[0]
    first = D_INNER // w

    def body(x_ref, w_ref, b_ref, o_ref):
        def chunk(ws, off, carry):
            x = x_ref[pl.ds(ws, CONV_WIN), :]
            pre = _conv_pre(x, w_ref, b_ref, _conv_positions(ws, x.shape), T)
            act = pre * _sigmoid(pre)
            o_ref[pl.ds(_aligned(ws + off, CONV_ROWS), CONV_ROWS), :] = act[off:off + CONV_ROWS]
            return carry

        _conv_chunks(T, chunk, 0)

    return _pallas(
        body, name=name, grid=(XBC_DIM // w,),
        in_specs=[pl.BlockSpec((T, w), lambda j: (0, j + first)), pl.BlockSpec((CONV_WIDTH, w), lambda j: (0, j)),
                  pl.BlockSpec((1, w), lambda j: (0, j))],
        out_specs=pl.BlockSpec((T, w), lambda j: (0, j)),
        out_shape=jax.ShapeDtypeStruct((T, XBC_DIM), F32),
        compiler_params=_params(("parallel",)),
    )(u_big, conv_w, conv_b)


def _conv_bwd(dact_f, dact_b, u_big, conv_w, conv_b, *, name, w=128):
    T = u_big.shape[0]
    first = D_INNER // w

    def body(df_ref, db_ref, x_ref, w_ref, b_ref, dx_ref, dw_ref, dbias_ref):
        def chunk(ws, off, sums):
            rows = pl.ds(ws, CONV_WIN)
            x = x_ref[rows, :]
            row = lax.broadcasted_iota(jnp.int32, x.shape, 0)
            t_idx = _conv_positions(ws, x.shape)
            pre = _conv_pre(x, w_ref, b_ref, t_idx, T)
            sg = _sigmoid(pre)
            dpre = (df_ref[rows, :] + db_ref[rows, :]) * (sg * (1.0 + pre * (1.0 - sg)))
            dx = w_ref[2:3, :] * dpre
            for j in (0, 1, 3, 4):
                dx = dx + w_ref[j:j + 1, :] * _conv_shift(dpre, 2 - j, t_idx, T)
            dx_ref[pl.ds(_aligned(ws + off, CONV_ROWS), CONV_ROWS), :] = dx[off:off + CONV_ROWS].astype(dx_ref.dtype)
            own = jnp.where((row >= off) & (row < off + CONV_ROWS), dpre, 0.0)
            new = [sums[5] + jnp.sum(own, axis=0, keepdims=True)]
            for j in range(CONV_WIDTH):
                new.insert(j, sums[j] + jnp.sum(own * _conv_shift(x, j - 2, t_idx, T), axis=0, keepdims=True))
            return tuple(new)

        zero = jnp.zeros((1, w), F32)
        sums = _conv_chunks(T, chunk, (zero,) * (CONV_WIDTH + 1))
        for j in range(CONV_WIDTH):
            dw_ref[j:j + 1, :] = sums[j]
        dbias_ref[...] = sums[CONV_WIDTH]

    blk = pl.BlockSpec((T, w), lambda j: (0, j))
    return _pallas(
        body, name=name, grid=(XBC_DIM // w,),
        in_specs=[blk, blk, pl.BlockSpec((T, w), lambda j: (0, j + first)),
                  pl.BlockSpec((CONV_WIDTH, w), lambda j: (0, j)), pl.BlockSpec((1, w), lambda j: (0, j))],
        out_specs=[blk, pl.BlockSpec((CONV_WIDTH, w), lambda j: (0, j)), pl.BlockSpec((1, w), lambda j: (0, j))],
        out_shape=[jax.ShapeDtypeStruct((T, XBC_DIM), BF16), jax.ShapeDtypeStruct((CONV_WIDTH, XBC_DIM), F32),
                   jax.ShapeDtypeStruct((1, XBC_DIM), F32)],
        compiler_params=_params(("parallel",)),
    )(dact_f, dact_b, u_big, conv_w, conv_b)


def _ssd_expand(rev):
    off = SSM_HEADS if rev else 0
    h = jnp.arange(LANES, dtype=jnp.int32)[:, None]
    return (jnp.arange(D_INNER, dtype=jnp.int32)[None, :] // 64 + off == h).astype(F32)


def _ssd_head_terms(dt_ref, bias_ref, alog_ref, acst_s, dtt_s, rev):
    L = CHUNK
    row = lax.broadcasted_iota(jnp.int32, (L, L), 0)
    col = lax.broadcasted_iota(jnp.int32, (L, L), 1)
    mask = (row <= col) if rev else (row >= col)
    cm = mask.astype(F32)
    cmt = ((row >= col) if rev else (row <= col)).astype(F32)
    pre = dt_ref[...] + bias_ref[...]
    dt = _softplus(pre)
    a = -jnp.exp(alog_ref[...])
    da = dt * a
    acs = _dot_h(cm, da)
    acst_s[...] = _dot_h_tn(da, cmt)
    dtt_s[...] = _dot_h_tn(dt, (row == col).astype(F32))
    tot = jnp.sum(da, axis=0, keepdims=True)
    w = jnp.exp(tot - acs)
    return dict(mask=mask, cm=cm, cmt=cmt, ident=(row == col).astype(F32), pre=pre, dt=dt, a=a, da=da, acs=acs,
                tot=tot, e=jnp.exp(acs), w=w, wdt=w * dt, dec=jnp.exp(tot))


def _pair(lo, v, h0):
    return jnp.where(lo, v[:, h0:h0 + 1], v[:, h0 + 1:h0 + 2])


def _ssd_fwd(xbc_act, u_small, bias128, alog128, *, rev, name):
    T = xbc_act.shape[0]
    L = CHUNK
    nc = T // L
    off = SSM_HEADS if rev else 0

    def cidx(c):
        return (nc - 1 - c) if rev else c

    def body(xs_ref, bm_ref, cm_ref, dt_ref, bias_ref, alog_ref, y_ref, hin_ref, ht_s, acst_s, dtt_s, wx_s, dec_s):
        c = pl.program_id(0)

        @pl.when(c == 0)
        def _():
            ht_s[...] = jnp.zeros_like(ht_s)

        t = _ssd_head_terms(dt_ref, bias_ref, alog_ref, acst_s, dtt_s, rev)
        lo = lax.broadcasted_iota(jnp.int32, (L, LANES), 1) < 64
        lo1 = lax.broadcasted_iota(jnp.int32, (1, LANES), 1) < 64
        for g in range(SSM_GROUPS):
            bmat = bm_ref[:, g * LANES:(g + 1) * LANES].astype(BF16)
            cmat = cm_ref[:, g * LANES:(g + 1) * LANES].astype(BF16)
            gmat = _dot_nt(cmat, bmat)
            ht = ht_s[g]
            ch = _dot(cmat, ht.astype(BF16))
            for pr in range(4):
                ps = slice(pr * LANES, (pr + 1) * LANES)
                cs = slice(g * 512 + pr * LANES, g * 512 + (pr + 1) * LANES)
                h0 = off + 8 * g + 2 * pr
                xp = xs_ref[:, cs]
                acc = _pair(lo, t["e"], h0) * ch[:, ps]
                for s_ in range(2):
                    h = h0 + s_
                    seg = t["acs"][:, h:h + 1] - acst_s[h:h + 1, :]
                    lam = jnp.exp(jnp.where(t["mask"], seg, -1e30))
                    m = (gmat * lam * dtt_s[h:h + 1, :]).astype(BF16)
                    xm = jnp.where(lo if s_ == 0 else jnp.logical_not(lo), xp, 0.0).astype(BF16)
                    acc = acc + _dot(m, xm)
                y_ref[:, cs] = acc
                wx_s[:, ps] = (_pair(lo, t["wdt"], h0) * xp).astype(BF16)
                dec_s[0:1, ps] = _pair(lo1, t["dec"], h0)
            hin_ref[0, g] = ht.astype(BF16)
            ht_s[g] = ht * dec_s[0:1, :] + _dot_tn(bmat, wx_s[...])

    return _pallas(
        body, name=name, grid=(nc,),
        in_specs=[pl.BlockSpec((L, D_INNER), lambda c: (cidx(c), 0)), pl.BlockSpec((L, 512), lambda c: (cidx(c), 4)),
                  pl.BlockSpec((L, 512), lambda c: (cidx(c), 5)),
                  pl.BlockSpec((L, LANES), lambda c: (cidx(c), U_DT // LANES)),
                  pl.BlockSpec((1, LANES), lambda c: (0, 0)), pl.BlockSpec((1, LANES), lambda c: (0, 0))],
        out_specs=[pl.BlockSpec((L, D_INNER), lambda c: (cidx(c), 0)),
                   pl.BlockSpec((1, SSM_GROUPS, D_STATE, 512), lambda c: (cidx(c), 0, 0, 0))],
        out_shape=[jax.ShapeDtypeStruct((T, D_INNER), F32), jax.ShapeDtypeStruct((nc, SSM_GROUPS, D_STATE, 512), BF16)],
        scratch_shapes=[pltpu.VMEM((SSM_GROUPS, D_STATE, 512), F32), pltpu.VMEM((LANES, L), F32),
                        pltpu.VMEM((LANES, L), F32), pltpu.VMEM((L, 512), BF16), pltpu.VMEM((8, 512), F32)],
        compiler_params=_params(("arbitrary",)),
    )(xbc_act, xbc_act, xbc_act, u_small, bias128, alog128)


def _ssd_bwd(dy, xbc_act, u_small, bias128, alog128, hin, skip_x, *, rev, name):
    T = xbc_act.shape[0]
    L = CHUNK
    nc = T // L
    off = SSM_HEADS if rev else 0
    has_skip = skip_x is not None

    def cidx(c):
        return c if rev else (nc - 1 - c)

    def body(*refs):
        (dy_ref, xs_ref, bm_ref, cm_ref, dt_ref, bias_ref, alog_ref, hin_ref) = refs[:8]
        k = 8
        skip_ref = refs[k] if has_skip else None
        k += 1 if has_skip else 0
        (dx_ref, draw_ref, dalog_ref, dbias_ref, dht_s, acst_s, dtt_s, rowt_s, ddtt_s, wx_s, edy_s, dec_s) = refs[k:]
        c = pl.program_id(0)

        @pl.when(c == 0)
        def _():
            dht_s[...] = jnp.zeros_like(dht_s)
            rowt_s[...] = jnp.zeros_like(rowt_s)
            ddtt_s[...] = jnp.zeros_like(ddtt_s)

        t = _ssd_head_terms(dt_ref, bias_ref, alog_ref, acst_s, dtt_s, rev)
        lane1 = lax.broadcasted_iota(jnp.int32, (1, LANES), 1)
        lo = lax.broadcasted_iota(jnp.int32, (L, LANES), 1) < 64
        lo1 = lane1 < 64
        colpart = jnp.zeros((L, LANES), F32)
        u_cols = jnp.zeros((L, LANES), F32)
        v_cols = jnp.zeros((L, LANES), F32)
        dtot_h = jnp.zeros((1, LANES), F32)
        for g in range(SSM_GROUPS):
            bmat = bm_ref[:, g * LANES:(g + 1) * LANES].astype(BF16)
            cmat = cm_ref[:, g * LANES:(g + 1) * LANES].astype(BF16)
            gmat = _dot_nt(cmat, bmat)
            ht_in = hin_ref[0, g]
            dht = dht_s[g]
            ht_in_b, dht_b = ht_in.astype(BF16), dht.astype(BF16)
            ch = _dot(cmat, ht_in_b)
            bdh = _dot(bmat, dht_b)
            th = jnp.sum(dht * ht_in, axis=0, keepdims=True)
            dgm = jnp.zeros((L, L), F32)
            for pr in range(4):
                ps = slice(pr * LANES, (pr + 1) * LANES)
                cs = slice(g * 512 + pr * LANES, g * 512 + (pr + 1) * LANES)
                h0 = off + 8 * g + 2 * pr
                xp = xs_ref[:, cs]
                dyp = dy_ref[:, cs]
                dyp_b = dyp.astype(BF16)
                wdt_p = _pair(lo, t["wdt"], h0)
                e_p = _pair(lo, t["e"], h0)
                xb = xp * bdh[:, ps]
                dc = dyp * ch[:, ps]
                dxp = wdt_p * bdh[:, ps]
                for s_ in range(2):
                    h = h0 + s_
                    keep = lo if s_ == 0 else jnp.logical_not(lo)
                    keep1 = lo1 if s_ == 0 else jnp.logical_not(lo1)
                    onehot = (lane1 == h).astype(F32)
                    dtrow = dtt_s[h:h + 1, :]
                    seg = t["acs"][:, h:h + 1] - acst_s[h:h + 1, :]
                    lam = jnp.exp(jnp.where(t["mask"], seg, -1e30))
                    mf0 = gmat * lam
                    m = (mf0 * dtrow).astype(BF16)
                    xm = jnp.where(keep, xp, 0.0).astype(BF16)
                    dm = _dot_nt(dyp_b, xm)
                    r = dm * mf0
                    q = r * dtrow
                    dgm = dgm + dm * lam * dtrow
                    colpart = colpart + jnp.sum(q, axis=1, keepdims=True) * onehot
                    rowt_s[h:h + 1, :] = jnp.sum(q, axis=0, keepdims=True)
                    ddtt_s[h:h + 1, :] = jnp.sum(r, axis=0, keepdims=True)
                    u_cols = u_cols + jnp.sum(jnp.where(keep, xb, 0.0), axis=1, keepdims=True) * onehot
                    v_cols = v_cols + jnp.sum(jnp.where(keep, dc, 0.0), axis=1, keepdims=True) * onehot
                    dtot_h = dtot_h + jnp.sum(jnp.where(keep1, th[:, ps], 0.0), axis=1, keepdims=True) * onehot
                    dxp = dxp + jnp.where(keep, _dot_tn(m, dyp_b), 0.0)
                if has_skip:
                    dxp = dxp + dyp * skip_ref[:, cs]
                dx_ref[:, cs] = dxp
                wx_s[:, ps] = (wdt_p * xp).astype(BF16)
                edy_s[:, ps] = (e_p * dyp).astype(BF16)
                dec_s[0:1, ps] = _pair(lo1, t["dec"], h0)
            edy_b = edy_s[...]
            dgm_b = dgm.astype(BF16)
            dx_ref[:, D_INNER + g * LANES:D_INNER + (g + 1) * LANES] = (
                _dot_nt(wx_s[...], dht_b) + _dot_tn(dgm_b, cmat))
            dx_ref[:, D_INNER + 512 + g * LANES:D_INNER + 512 + (g + 1) * LANES] = (
                _dot_nt(edy_b, ht_in_b) + _dot(dgm_b, bmat))
            dht_s[g] = dec_s[0:1, :] * dht + _dot_tn(cmat, edy_b)

        t_e = v_cols * t["e"]
        t_w = u_cols * t["wdt"]
        colsum_part = _dot_h_tn(rowt_s[...], t["ident"])
        dtot = jnp.sum(t_w, axis=0, keepdims=True) + t["dec"] * dtot_h
        row1 = lax.broadcasted_iota(jnp.int32, (L, LANES), 0)
        last = row1 == (0 if rev else L - 1)
        dacs = colpart - colsum_part + t_e - t_w + jnp.where(last, dtot, 0.0)
        dda = _dot_h(t["cmt"], dacs)
        ddt = dda * t["a"] + u_cols * t["w"] + _dot_h_tn(ddtt_s[...], t["ident"])
        dalog = jnp.sum(dda * t["dt"], axis=0, keepdims=True) * t["a"]
        draw = ddt * _sigmoid(t["pre"])
        draw_ref[...] = draw
        dbias = jnp.sum(draw, axis=0, keepdims=True)

        @pl.when(c == 0)
        def _():
            dalog_ref[...] = dalog
            dbias_ref[...] = dbias

        @pl.when(c > 0)
        def _():
            dalog_ref[...] += dalog
            dbias_ref[...] += dbias

    one = pl.BlockSpec((1, LANES), lambda c: (0, 0))
    in_specs = [pl.BlockSpec((L, D_INNER), lambda c: (cidx(c), 0)), pl.BlockSpec((L, D_INNER), lambda c: (cidx(c), 0)),
                pl.BlockSpec((L, 512), lambda c: (cidx(c), 4)), pl.BlockSpec((L, 512), lambda c: (cidx(c), 5)),
                pl.BlockSpec((L, LANES), lambda c: (cidx(c), U_DT // LANES)), one, one,
                pl.BlockSpec((1, SSM_GROUPS, D_STATE, 512), lambda c: (cidx(c), 0, 0, 0))]
    ins = [dy, xbc_act, xbc_act, xbc_act, u_small, bias128, alog128, hin]
    if has_skip:
        in_specs.append(pl.BlockSpec((1, D_INNER), lambda c: (0, 0)))
        ins.append(skip_x)
    return _pallas(
        body, name=name, grid=(nc,), in_specs=in_specs,
        out_specs=[pl.BlockSpec((L, XBC_DIM), lambda c: (cidx(c), 0)), pl.BlockSpec((L, LANES), lambda c: (cidx(c), 0)),
                   one, one],
        out_shape=[jax.ShapeDtypeStruct((T, XBC_DIM), F32), jax.ShapeDtypeStruct((T, LANES), F32),
                   jax.ShapeDtypeStruct((1, LANES), F32), jax.ShapeDtypeStruct((1, LANES), F32)],
        scratch_shapes=[pltpu.VMEM((SSM_GROUPS, D_STATE, 512), F32), pltpu.VMEM((LANES, L), F32),
                        pltpu.VMEM((LANES, L), F32), pltpu.VMEM((LANES, L), F32), pltpu.VMEM((LANES, L), F32),
                        pltpu.VMEM((L, 512), BF16), pltpu.VMEM((L, 512), BF16), pltpu.VMEM((8, 512), F32)],
        compiler_params=_params(("arbitrary",)),
    )(*ins)


def _ssm_out_fwd(y_f, y_b, xbc_act, u_big, skip_x, ssm_norm, *, name, tile=512):
    rows = y_f.shape[0]

    def fn(yf, yb, xs, z, sk, nw):
        yz = (yf + yb + sk * xs) * (z * _sigmoid(z))
        r = lax.rsqrt(jnp.mean(yz * yz, axis=-1, keepdims=True) + EPS)
        return yz * r * nw

    return _rowmap(fn, name=name, rows=rows, tile=tile, ncol=SSM_GROUPS,
                   ins=[(y_f, 512, _colj), (y_b, 512, _colj), (xbc_act, 512, _colj), (u_big, 512, _colj)],
                   consts=[(skip_x, 512, _colj), (ssm_norm, 512, _colj)], outs=[(D_INNER, BF16, 512, _colj)])[0]


def _ssm_out_bwd(dm, y_f, y_b, xbc_act, u_big, skip_x, ssm_norm, *, name, tile=512):
    rows = y_f.shape[0]

    def fn(dmv, yf, yb, xs, z, sk, nw):
        sg = _sigmoid(z)
        y = yf + yb + sk * xs
        yz = y * (z * sg)
        r = lax.rsqrt(jnp.mean(yz * yz, axis=-1, keepdims=True) + EPS)
        xh = yz * r
        dxh = dmv * nw
        dyz = r * (dxh - xh * jnp.mean(dxh * xh, axis=-1, keepdims=True))
        dy = dyz * (z * sg)
        dz = dyz * y * (sg * (1.0 + z * (1.0 - sg)))
        return dy, dz, jnp.sum(dmv * xh, axis=0, keepdims=True), jnp.sum(dy * xs, axis=0, keepdims=True)

    return _rowmap(fn, name=name, rows=rows, tile=tile, ncol=SSM_GROUPS,
                   ins=[(dm, 512, _colj), (y_f, 512, _colj), (y_b, 512, _colj), (xbc_act, 512, _colj),
                        (u_big, 512, _colj)],
                   consts=[(skip_x, 512, _colj), (ssm_norm, 512, _colj)],
                   outs=[(D_INNER, F32, 512, _colj), (D_INNER, BF16, 512, _colj)],
                   accs=[(D_INNER, 512, _colj), (D_INNER, 512, _colj)])


def _merge_fwd(pa, pb, u_big, *, name, tile=512):
    rows = pa.shape[0]

    def fn(a, b, ga, gb):
        return _sigmoid(ga) * a + _sigmoid(gb) * b

    return _rowmap(fn, name=name, rows=rows, tile=tile,
                   ins=[(pa, 1024, _col0), (pb, 1024, _col0), (u_big, 1024, lambda j: 5), (u_big, 1024, lambda j: 6)],
                   outs=[(1024, BF16, 1024, _col0)])[0]


def _merge_bwd(dmg, pa, pb, u_big, *, name, tile=512):
    rows = pa.shape[0]

    def fn(d, a, b, ga, gb):
        sa, sb = _sigmoid(ga), _sigmoid(gb)
        return d * sa, d * sb, d * a * sa * (1.0 - sa), d * b * sb * (1.0 - sb)

    return _rowmap(fn, name=name, rows=rows, tile=tile,
                   ins=[(dmg, 1024, _col0), (pa, 1024, _col0), (pb, 1024, _col0), (u_big, 1024, lambda j: 5),
                        (u_big, 1024, lambda j: 6)],
                   outs=[(1024, BF16, 1024, _col0)] * 4)


def _loss_bwd(y, target, *, name, tile=512):
    rows, d = y.shape

    def fn(yv, tv):
        err = yv - tv
        part = jnp.sum(jnp.sum(err * err, axis=-1, keepdims=True), axis=0, keepdims=True)
        return err * (1.0 / d), jnp.broadcast_to(part * (0.5 / d), (1, LANES))

    dy, part = _rowmap(fn, name=name, rows=rows, tile=tile, ins=[(y, d, _col0), (target, d, _col0)],
                       outs=[(d, F32, d, _col0)], accs=[(LANES, LANES, _col0)])
    return dy, part[0, 0]


def _small_slab(gs, dskip_ch, dalog_f, dalog_b, dbias_f, dbias_b, gkv, gqh, gkh, dconv_w, *, name):
    e_mat = _ssd_expand(False)
    full_names = ("ffn1_norm", "mix_norm", "q_a_norm", "conv_b", "ssm_norm", "ffn2_norm")
    full = [gs[n] for n in full_names]
    nf = len(full)

    def body(*refs):
        fulls = refs[:nf]
        (dsk_ref, e_ref, af_ref, ab_ref, bf_ref, bb_ref, gkv_ref, gqh_ref, gkh_ref, cw_ref, o_ref) = refs[nf:]
        o_ref[...] = jnp.zeros_like(o_ref)
        for n, r in zip(full_names, fulls):
            o_ref[SMALL_ROW[n]:SMALL_ROW[n] + 1, 0:r.shape[1]] = r[...]
        o_ref[SMALL_ROW["kv_a_norm"]:SMALL_ROW["kv_a_norm"] + 1, 0:KV_LORA] = gkv_ref[...]
        o_ref[SMALL_ROW["q_head_norm"]:SMALL_ROW["q_head_norm"] + 1, 0:LANES] = gqh_ref[...]
        o_ref[SMALL_ROW["k_head_norm"]:SMALL_ROW["k_head_norm"] + 1, 0:LANES] = gkh_ref[...]
        o_ref[SMALL_ROW["a_log_fwd"]:SMALL_ROW["a_log_fwd"] + 1, 0:LANES] = af_ref[...]
        o_ref[SMALL_ROW["a_log_bwd"]:SMALL_ROW["a_log_bwd"] + 1, 0:LANES] = pltpu.roll(ab_ref[...], 96, 1)
        o_ref[SMALL_ROW["dt_bias_fwd"]:SMALL_ROW["dt_bias_fwd"] + 1, 0:LANES] = bf_ref[...]
        o_ref[SMALL_ROW["dt_bias_bwd"]:SMALL_ROW["dt_bias_bwd"] + 1, 0:LANES] = pltpu.roll(bb_ref[...], 96, 1)
        dsk = _dot_h_nt(jnp.broadcast_to(dsk_ref[...], (8, D_INNER)), e_ref[...])
        o_ref[SMALL_ROW["d_skip"]:SMALL_ROW["d_skip"] + 1, 0:LANES] = dsk[0:1, :]
        o_ref[CONV_ROW:CONV_ROW + CONV_WIDTH, :] = cw_ref[...]

    return _pallas(body, name=name, out_shape=jax.ShapeDtypeStruct((SMALL_ROWS, SMALL_COLS), F32))(
        *full, dskip_ch, e_mat, dalog_f, dalog_b, dbias_f, dbias_b, gkv, gqh, gkh, dconv_w)


def _adamw_math(g, w, m, v):
    m2 = ADAM_B1 * m + (1.0 - ADAM_B1) * g
    v2 = ADAM_B2 * v + (1.0 - ADAM_B2) * (g * g)
    m_hat = m2 / (1.0 - ADAM_B1 ** ADAM_STEP)
    v_hat = v2 / (1.0 - ADAM_B2 ** ADAM_STEP)
    delta = -ADAM_LR * (m_hat / (jnp.sqrt(v_hat) + ADAM_EPS) + ADAM_WD * w)
    return delta, m2, v2


def _sum8(r_ref):
    g = r_ref[0].astype(F32)
    for s in range(1, N_DEV):
        g = g + r_ref[s].astype(F32)
    return g


def _reduce_adamw(recv, w, m, v, *, name, tile=256):
    _, R, C = recv.shape
    tile = _pick(R, tile) if R % LANES == 0 else R
    assert R % tile == 0

    def body(r_ref, w_ref, m_ref, v_ref, g_ref, d_ref, m2_ref, v2_ref):
        g = _sum8(r_ref)
        delta, m2, v2 = _adamw_math(g, w_ref[...], m_ref[...], v_ref[...])
        g_ref[...] = g
        d_ref[...] = delta
        m2_ref[...] = m2
        v2_ref[...] = v2

    blk = pl.BlockSpec((tile, C), lambda i: (i, 0))
    return _pallas(
        body, name=name, grid=(R // tile,),
        in_specs=[pl.BlockSpec((N_DEV, tile, C), lambda i: (0, i, 0)), blk, blk, blk], out_specs=[blk] * 4,
        out_shape=[jax.ShapeDtypeStruct((R, C), F32)] * 4, compiler_params=_params(("parallel",)),
    )(recv, w, m, v)


def _reduce_t_adamw(recv, w, m, v, *, name):
    R, cs = w.shape

    def body(r_ref, w_ref, m_ref, v_ref, g_ref, d_ref, m2_ref, v2_ref):
        g = _sum8(r_ref).T
        delta, m2, v2 = _adamw_math(g, w_ref[...], m_ref[...], v_ref[...])
        g_ref[...] = g
        d_ref[...] = delta
        m2_ref[...] = m2
        v2_ref[...] = v2

    return _pallas(body, name=name, out_shape=[jax.ShapeDtypeStruct((R, cs), F32)] * 4,
                   compiler_params=pltpu.CompilerParams(vmem_limit_bytes=VMEM_LIMIT))(recv, w, m, v)


def _reduce8(recv, *, name, tile):
    _, R, C = recv.shape

    def body(r_ref, g_ref):
        g_ref[...] = _sum8(r_ref)

    return _pallas(body, name=name, grid=(R // tile,),
                   in_specs=[pl.BlockSpec((N_DEV, tile, C), lambda i: (0, i, 0))],
                   out_specs=pl.BlockSpec((tile, C), lambda i: (i, 0)),
                   out_shape=jax.ShapeDtypeStruct((R, C), F32), compiler_params=_params(("parallel",)))(recv)


def _adamw(g, w, m, v, *, name, tile=256):
    R, C = w.shape

    def body(g_ref, w_ref, m_ref, v_ref, d_ref, m2_ref, v2_ref):
        delta, m2, v2 = _adamw_math(g_ref[...], w_ref[...], m_ref[...], v_ref[...])
        d_ref[...] = delta
        m2_ref[...] = m2
        v2_ref[...] = v2

    blk = pl.BlockSpec((R, tile), lambda i: (0, i))
    return _pallas(body, name=name, grid=(C // tile,), in_specs=[blk] * 4, out_specs=[blk] * 3,
                   out_shape=[jax.ShapeDtypeStruct((R, C), F32)] * 3, compiler_params=_params(("parallel",)))(g, w, m, v)


def _adamw_small(srecv, conv_g, ws, ms, vs, *, name):
    n = len(ws)

    def body(*refs):
        s_ref, c_ref = refs[0], refs[1]
        w_refs, m_refs, v_refs = refs[2:2 + n], refs[2 + n:2 + 2 * n], refs[2 + 2 * n:2 + 3 * n]
        outs = refs[2 + 3 * n:]
        gsum = _sum8(s_ref)
        for i in range(n):
            if i < len(SMALL):
                g = gsum[i:i + 1, 0:SMALL[i][1]]
            else:
                g = _sum8(c_ref)
            delta, m2, v2 = _adamw_math(g, w_refs[i][...], m_refs[i][...], v_refs[i][...])
            outs[i][...] = g
            outs[n + i][...] = delta
            outs[2 * n + i][...] = m2
            outs[3 * n + i][...] = v2

    shapes = [jax.ShapeDtypeStruct(w.shape, F32) for w in ws]
    got = _pallas(body, name=name, out_shape=shapes * 4,
                  compiler_params=pltpu.CompilerParams(vmem_limit_bytes=VMEM_LIMIT))(srecv, conv_g, *ws, *ms, *vs)
    return got[:n], got[n:2 * n], got[2 * n:3 * n], got[3 * n:]


def _ffn_fwd(x, norm, w_g_t, w_u_t, w_d, tag):
    h = _rms_fwd(x, norm, name=f"{tag}_rms")
    gu = _mm(h, jnp.concatenate([w_g_t, w_u_t], axis=0), name=f"{tag}_gu", tb=True, out_dtype=BF16)
    act = _swiglu_fwd(gu, name=f"{tag}_act")
    out = _mm(act, w_d, name=f"{tag}_down", alpha=0.5, res=x)
    return out, (h, gu, act)


def _ffn_bwd(dout, x, norm, w_g_t, w_u_t, w_d, saved, tag, send=None):
    h, gu, act = saved
    d_act = _mm(dout, w_d, name=f"{tag}_dact", tb=True, alpha=0.5, out_dtype=BF16)
    d_wd = _mm(act, dout, name=f"{tag}_dwd", ta=True, alpha=0.5, tm=1408, tn=1024, out_dtype=BF16)
    tok = send(("down",), [d_wd]) if send else None
    dg, du = _swiglu_bwd(d_act, gu, name=f"{tag}_dswiglu")
    d_wg_t = _mm(dg, h, name=f"{tag}_dwg", ta=True, tm=1408, tn=1024, out_dtype=BF16, after=tok)
    d_wu_t = _mm(du, h, name=f"{tag}_dwu", ta=True, tm=1408, tn=1024, out_dtype=BF16)
    tok = send(("gate", "up"), [d_wg_t, d_wu_t]) if send else None
    dh = _mm(dg, w_g_t, name=f"{tag}_dh_g", after=tok)
    dh = _mm(du, w_u_t, name=f"{tag}_dh_u", res=dh)
    dx, dnorm = _rms_bwd(dh, x, norm, dout, name=f"{tag}_drms")
    return dx, dnorm, d_wg_t, d_wu_t, d_wd


def _rope_tables(positions, T):
    pos = positions.reshape(T).astype(F32)
    inv_freq = 1.0 / (ROPE_BASE ** (jnp.arange(0, QK_ROPE, 2, dtype=F32) / QK_ROPE))
    ang = pos[:, None] * inv_freq
    cos, sin = jnp.cos(ang), jnp.sin(ang)
    one64, z64 = jnp.ones((T, 64), F32), jnp.zeros((T, 64), F32)
    z16, z32, one32 = jnp.zeros((T, 16), F32), jnp.zeros((T, 32), F32), jnp.ones((T, 32), F32)
    c = jnp.concatenate([one64, cos, cos, one32], axis=1)
    s1 = jnp.concatenate([z64, -sin, z16, z32], axis=1)
    s2 = jnp.concatenate([z64, z16, sin, z32], axis=1)
    return c, s1, s2


def _cols(g):
    n, r, cs = g.shape
    return g.transpose(1, 0, 2).reshape(r, n * cs)


def _rows(g):
    n, rs, c = g.shape
    return g.reshape(n * rs, c)


def _pad_lanes(v, n=LANES):
    return jnp.pad(v, ((0, 0), (0, n - v.shape[1])))


def _in_proj_weights(w_in_t):
    z = lambda n: jnp.zeros((n, D_MODEL), w_in_t.dtype)
    w_small_t = jnp.concatenate([w_in_t[0:384], z(128), w_in_t[384:672], z(96), w_in_t[5792:5856], z(64)], axis=0)
    w_big_t = jnp.concatenate([w_in_t[672:5792], w_in_t[5856:7904]], axis=0)
    return w_small_t, w_big_t


def _mla_up_weights(w_q_b_t, w_kv_b):
    wq = w_q_b_t.reshape(N_HEADS, QK_HEAD, Q_LORA)
    wq = jnp.pad(wq, ((0, 0), (0, LANES - QK_HEAD), (0, 0))).reshape(N_HEADS * LANES, Q_LORA)
    wkv = w_kv_b.reshape(KV_LORA, N_HEADS, QK_NOPE + V_HEAD)
    wk = jnp.pad(wkv[..., :QK_NOPE], ((0, 0), (0, 0), (0, LANES - QK_NOPE))).reshape(KV_LORA, N_HEADS * LANES)
    v = wkv[..., QK_NOPE:]
    zv = jnp.zeros_like(v)
    even = (jnp.arange(N_HEADS) % 2 == 0)[None, :, None]
    wv = jnp.where(even, jnp.concatenate([v, zv], -1), jnp.concatenate([zv, v], -1)).reshape(KV_LORA, N_HEADS * LANES)
    return wq, wk, wv


def _shard_rows(g):
    return g.reshape(N_DEV, g.shape[0] // N_DEV, g.shape[1])


def kernel(x, positions, ffn1_norm, ffn1_w_gate, ffn1_w_up, ffn1_w_down, mix_norm, w_in, q_a_norm, w_q_b, kv_a_norm, w_kv_b, q_head_norm, k_head_norm, conv_w, conv_b, a_log_fwd, a_log_bwd, dt_bias_fwd, dt_bias_bwd, d_skip, ssm_norm, w_attn_branch, w_ssm_branch, w_out, ffn2_norm, ffn2_w_gate, ffn2_w_up, ffn2_w_down, loss_target, m_ffn1_norm, m_ffn1_w_gate, m_ffn1_w_up, m_ffn1_w_down, m_mix_norm, m_w_in, m_q_a_norm, m_w_q_b, m_kv_a_norm, m_w_kv_b, m_q_head_norm, m_k_head_norm, m_conv_w, m_conv_b, m_a_log_fwd, m_a_log_bwd, m_dt_bias_fwd, m_dt_bias_bwd, m_d_skip, m_ssm_norm, m_w_attn_branch, m_w_ssm_branch, m_w_out, m_ffn2_norm, m_ffn2_w_gate, m_ffn2_w_up, m_ffn2_w_down, v_ffn1_norm, v_ffn1_w_gate, v_ffn1_w_up, v_ffn1_w_down, v_mix_norm, v_w_in, v_q_a_norm, v_w_q_b, v_kv_a_norm, v_w_kv_b, v_q_head_norm, v_k_head_norm, v_conv_w, v_conv_b, v_a_log_fwd, v_a_log_bwd, v_dt_bias_fwd, v_dt_bias_bwd, v_d_skip, v_ssm_norm, v_w_attn_branch, v_w_ssm_branch, v_w_out, v_ffn2_norm, v_ffn2_w_gate, v_ffn2_w_up, v_ffn2_w_down):
    w_all = dict(ffn1_norm=ffn1_norm, ffn1_w_gate=ffn1_w_gate, ffn1_w_up=ffn1_w_up, ffn1_w_down=ffn1_w_down, mix_norm=mix_norm, w_in=w_in, q_a_norm=q_a_norm, w_q_b=w_q_b, kv_a_norm=kv_a_norm, w_kv_b=w_kv_b, q_head_norm=q_head_norm, k_head_norm=k_head_norm, conv_w=conv_w, conv_b=conv_b, a_log_fwd=a_log_fwd, a_log_bwd=a_log_bwd, dt_bias_fwd=dt_bias_fwd, dt_bias_bwd=dt_bias_bwd, d_skip=d_skip, ssm_norm=ssm_norm, w_attn_branch=w_attn_branch, w_ssm_branch=w_ssm_branch, w_out=w_out, ffn2_norm=ffn2_norm, ffn2_w_gate=ffn2_w_gate, ffn2_w_up=ffn2_w_up, ffn2_w_down=ffn2_w_down)
    m_all = dict(ffn1_norm=m_ffn1_norm, ffn1_w_gate=m_ffn1_w_gate, ffn1_w_up=m_ffn1_w_up, ffn1_w_down=m_ffn1_w_down, mix_norm=m_mix_norm, w_in=m_w_in, q_a_norm=m_q_a_norm, w_q_b=m_w_q_b, kv_a_norm=m_kv_a_norm, w_kv_b=m_w_kv_b, q_head_norm=m_q_head_norm, k_head_norm=m_k_head_norm, conv_w=m_conv_w, conv_b=m_conv_b, a_log_fwd=m_a_log_fwd, a_log_bwd=m_a_log_bwd, dt_bias_fwd=m_dt_bias_fwd, dt_bias_bwd=m_dt_bias_bwd, d_skip=m_d_skip, ssm_norm=m_ssm_norm, w_attn_branch=m_w_attn_branch, w_ssm_branch=m_w_ssm_branch, w_out=m_w_out, ffn2_norm=m_ffn2_norm, ffn2_w_gate=m_ffn2_w_gate, ffn2_w_up=m_ffn2_w_up, ffn2_w_down=m_ffn2_w_down)
    v_all = dict(ffn1_norm=v_ffn1_norm, ffn1_w_gate=v_ffn1_w_gate, ffn1_w_up=v_ffn1_w_up, ffn1_w_down=v_ffn1_w_down, mix_norm=v_mix_norm, w_in=v_w_in, q_a_norm=v_q_a_norm, w_q_b=v_w_q_b, kv_a_norm=v_kv_a_norm, w_kv_b=v_w_kv_b, q_head_norm=v_q_head_norm, k_head_norm=v_k_head_norm, conv_w=v_conv_w, conv_b=v_conv_b, a_log_fwd=v_a_log_fwd, a_log_bwd=v_a_log_bwd, dt_bias_fwd=v_dt_bias_fwd, dt_bias_bwd=v_dt_bias_bwd, d_skip=v_d_skip, ssm_norm=v_ssm_norm, w_attn_branch=v_w_attn_branch, w_ssm_branch=v_w_ssm_branch, w_out=v_w_out, ffn2_norm=v_ffn2_norm, ffn2_w_gate=v_ffn2_w_gate, ffn2_w_up=v_ffn2_w_up, ffn2_w_down=v_ffn2_w_down)
    T = x.shape[1]
    xs_in, target = x[0], loss_target[0]
    def two_d(n, a):
        if n in TRANSPOSED:
            return jnp.swapaxes(a, 1, 2).reshape(a.shape[2], a.shape[1])
        return a.reshape(-1, a.shape[-1])

    w2 = {n: two_d(n, a) for n, a in w_all.items()}
    m2 = {n: two_d(n, a) for n, a in m_all.items()}
    v2 = {n: two_d(n, a) for n, a in v_all.items()}
    p = {n: w2[n] for n, _ in SMALL}
    bf = lambda n: w2[n].astype(BF16)

    first = ["ffn1_w_gate", "ffn1_w_up", "ffn1_w_down"]
    ge = dict(zip(first, _all_gather_two_level([bf(n) for n in first], name="gather_ffn1")))
    mixw = ["w_in", "w_q_b", "w_kv_b", "conv_w"]
    behind_first = ge["ffn1_w_down"][0, 0:1, 0:1].astype(F32) * 0.0
    mix_started, token = _exchange_start(["gather"] * len(mixw),
                                         [bf(n) for n in mixw[:3]] + [w2["conv_w"] + behind_first],
                                         name="gather_mix_start")
    ffn1_norm_f = p["ffn1_norm"] + token[0:1, 0:1]
    w_g1t, w_u1t = _rows(ge["ffn1_w_gate"]), _rows(ge["ffn1_w_up"])
    w_d1 = _rows(ge["ffn1_w_down"])
    late = ["w_attn_branch", "w_ssm_branch", "w_out", "ffn2_w_gate", "ffn2_w_up", "ffn2_w_down"]
    late_shards = [bf(n) for n in late]

    tabs = _rope_tables(positions, T)
    qg, kg = _pad_lanes(p["q_head_norm"]), _pad_lanes(p["k_head_norm"])
    bias128 = _pad_lanes(jnp.concatenate([p["dt_bias_fwd"], p["dt_bias_bwd"]], axis=1))
    alog128 = _pad_lanes(jnp.concatenate([p["a_log_fwd"], p["a_log_bwd"]], axis=1))
    skip_x = jnp.repeat(p["d_skip"], 64, axis=1)

    x1, ffn1_saved = _ffn_fwd(xs_in, ffn1_norm_f, w_g1t, w_u1t, w_d1, "ffn1")
    h2 = _rms_fwd(x1, p["mix_norm"], name="mix_rms")
    ge.update(zip(mixw, _exchange_wait(["gather"] * len(mixw), mix_started, h2, name="gather_mix_wait")))
    w_small_t, w_big_t = _in_proj_weights(_rows(ge["w_in"]))
    wq_t, wk, wv = _mla_up_weights(_rows(ge["w_q_b"]), _cols(ge["w_kv_b"]))
    conv_full = _cols(ge["conv_w"])
    u_big = _mm(h2, w_big_t, name="in_big", tb=True)
    u_small = _mm(h2, w_small_t, name="in_small", tb=True)
    cqn, ckvn = _lora_norm_fwd(u_small, p["q_a_norm"], p["kv_a_norm"], name="lora_norm")
    q_raw = _mm(cqn, wq_t, name="q_up", tb=True)
    k_raw = _mm(ckvn, wk, name="k_up")
    v = _mm(ckvn, wv, name="v_up", out_dtype=BF16)
    q, k = _qk_prep_fwd(q_raw, k_raw, u_small, tabs, qg, kg, name="qk_prep")
    a_out, lse, g_late = _attn_fwd(q, k, v, ["gather"] * len(late), late_shards, name="attn_fwd")
    gl = dict(zip(late, g_late))
    w_pa, w_pb, w_o = _rows(gl["w_attn_branch"]), _rows(gl["w_ssm_branch"]), _rows(gl["w_out"])
    w_g2t, w_u2t = _rows(gl["ffn2_w_gate"]), _rows(gl["ffn2_w_up"])
    w_d2 = _rows(gl["ffn2_w_down"])
    xbc_act = _conv_fwd(u_big, conv_full, p["conv_b"], name="conv_fwd")
    y_f, hin_f = _ssd_fwd(xbc_act, u_small, bias128, alog128, rev=False, name="ssd_fwd_f")
    y_b, hin_b = _ssd_fwd(xbc_act, u_small, bias128, alog128, rev=True, name="ssd_fwd_b")
    m_out = _ssm_out_fwd(y_f, y_b, xbc_act, u_big, skip_x, p["ssm_norm"], name="ssm_out")
    pa = _mm(a_out, w_pa, name="branch_a")
    pb = _mm(m_out, w_pb, name="branch_b")
    merged = _merge_fwd(pa, pb, u_big, name="merge")
    x2 = _mm(merged, w_o, name="mix_out", res=x1)
    y, ffn2_saved = _ffn_fwd(x2, p["ffn2_norm"], w_g2t, w_u2t, w_d2, "ffn2")
    dy, loss_part = _loss_bwd(y, target, name="loss")
    loss = lax.psum(loss_part, ("x", "y", "c"))

    gs = {}
    dx2, gs["ffn2_norm"], g_gate2, g_up2, g_down2 = _ffn_bwd(dy, x2, p["ffn2_norm"], w_g2t, w_u2t, w_d2, ffn2_saved,
                                                             "ffn2b")
    dmerged = _mm(dx2, w_o, name="d_merged", tb=True)
    g_out = _mm(merged, dx2, name="d_w_out", ta=True, out_dtype=BF16)
    dpa, dpb, dga, dgb = _merge_bwd(dmerged, pa, pb, u_big, name="d_merge")
    g_pa = _mm(a_out, dpa, name="d_w_pa", ta=True, out_dtype=BF16)
    g_pb = _mm(m_out, dpb, name="d_w_pb", ta=True, out_dtype=BF16)
    da_out = _mm(dpa, w_pa, name="d_a", tb=True)
    dm_out = _mm(dpb, w_pb, name="d_m", tb=True)
    late_grads = [_shard_rows(g) for g in (g_pa, g_pb, g_out, g_gate2, g_up2, g_down2)]
    dq, dk, dv, r_late = _attn_bwd(q, k, v, a_out, lse, da_out, ["scatter"] * len(late_grads), late_grads,
                                   name="attn_bwd")
    recv = dict(zip(late, r_late))

    dyss, dz, gs["ssm_norm"], dskip_ch = _ssm_out_bwd(dm_out, y_f, y_b, xbc_act, u_big, skip_x, p["ssm_norm"],
                                                      name="d_ssm_out")
    dact_f, draw_f, dalog_f, dbias_f = _ssd_bwd(dyss, xbc_act, u_small, bias128, alog128, hin_f, skip_x,
                                                rev=False, name="ssd_bwd_f")
    dact_b, draw_b, dalog_b, dbias_b = _ssd_bwd(dyss, xbc_act, u_small, bias128, alog128, hin_b, None,
                                                rev=True, name="ssd_bwd_b")
    dxbc, g_conv, gs["conv_b"] = _conv_bwd(dact_f, dact_b, u_big, conv_full, p["conv_b"], name="conv_bwd")

    dq_raw, dk_raw, dkpe, gqh, gkh = _qk_prep_bwd(dq, dk, q_raw, k_raw, u_small, tabs, qg, kg, name="d_qk_prep")
    g_wq_t = _mm(dq_raw, cqn, name="d_w_q", ta=True, out_dtype=BF16)
    g_wk_t = _mm(dk_raw, ckvn, name="d_w_k", ta=True, out_dtype=BF16)
    g_wv_t = _mm(dv, ckvn, name="d_w_v", ta=True, out_dtype=BF16)
    dcqn = _mm(dq_raw, wq_t, name="d_cqn")
    dckvn = _mm(dk_raw, wk, name="d_ckvn_k", tb=True)
    dckvn = _mm(dv, wv, name="d_ckvn_v", tb=True, res=dckvn)
    du_small, gs["q_a_norm"], gkv = _lora_norm_bwd(dcqn, dckvn, u_small, p["q_a_norm"], p["kv_a_norm"], dkpe,
                                                   draw_f, draw_b, name="d_lora_norm")

    dh2 = _mm(du_small, w_small_t, name="d_h2_small")
    dh2 = _mm(dz, w_big_t, name="d_h2_z", b_row0=0, res=dh2)
    dh2 = _mm(dxbc, w_big_t, name="d_h2_xbc", b_row0=2048, res=dh2)
    dh2 = _mm(dga, w_big_t, name="d_h2_ga", b_row0=5120, res=dh2)
    dh2 = _mm(dgb, w_big_t, name="d_h2_gb", b_row0=6144, res=dh2)
    gt_small = _mm(du_small, h2, name="d_w_small", ta=True, out_dtype=BF16)
    gt_z = _mm(dz, h2, name="d_w_z", ta=True, out_dtype=BF16)
    gt_xbc = _mm(dxbc, h2, name="d_w_xbc", ta=True, out_dtype=BF16)
    gt_ga = _mm(dga, h2, name="d_w_ga", ta=True, out_dtype=BF16)
    gt_gb = _mm(dgb, h2, name="d_w_gb", ta=True, out_dtype=BF16)
    dx1, gs["mix_norm"] = _rms_bwd(dh2, x1, p["mix_norm"], dx2, name="d_mix_rms")

    gt_in = jnp.concatenate([gt_small[0:384], gt_small[U_CKV:U_KPE + QK_ROPE], gt_z, gt_xbc,
                             gt_small[U_DT:U_DT + 64], gt_ga, gt_gb], axis=0)
    gt_in = jnp.pad(gt_in.reshape(N_DEV, W_IN_SHARD, D_MODEL), ((0, 0), (0, W_IN_SHARD_PAD - W_IN_SHARD), (0, 0)))
    gt_q = g_wq_t.reshape(N_HEADS, LANES, Q_LORA)[:, :QK_HEAD].reshape(N_DEV, -1, Q_LORA)
    gk3 = g_wk_t.reshape(N_HEADS, LANES, KV_LORA)[:, :QK_NOPE]
    gv3 = g_wv_t.reshape(N_HEADS, LANES, KV_LORA)
    even = (jnp.arange(N_HEADS) % 2 == 0)[:, None, None]
    gv3 = jnp.where(even, gv3[:, :V_HEAD], gv3[:, V_HEAD:])
    gt_kv = jnp.concatenate([gk3, gv3], axis=1).reshape(N_DEV, -1, KV_LORA)
    mixg = ["w_in", "w_q_b", "w_kv_b"]
    grads_started, token = _exchange_start(["scatter"] * len(mixg), [gt_in, gt_q, gt_kv], name="grad_mix_start")
    ffn1_sent = []

    def send(names, grads):
        st, tok = _exchange_start(["scatter"] * len(grads), [_shard_rows(g) for g in grads],
                                  name="grad_ffn1_" + "_".join(names) + "_start")
        ffn1_sent.append((names, st))
        return tok

    grad_x, gs["ffn1_norm"], _, _, _ = _ffn_bwd(dx1, xs_in, p["ffn1_norm"] + token[0:1, 0:1], w_g1t, w_u1t, w_d1,
                                                ffn1_saved, "ffn1b", send=send)
    recv.update(zip(mixg, _exchange_wait(["scatter"] * len(mixg), grads_started, grad_x, name="grad_mix_wait")))
    for names, st in ffn1_sent:
        got = _exchange_wait(["scatter"] * len(names), st, grad_x, name="grad_ffn1_" + "_".join(names) + "_wait")
        recv.update(zip(["ffn1_w_" + n for n in names], got))

    gsmall = _small_slab(gs, dskip_ch, dalog_f, dalog_b, dbias_f, dbias_b, gkv, gqh, gkh, g_conv, name="small_slab")
    srecv = _exchange(["gather"], [gsmall], name="grad_exchange")[0]

    out = {}
    for n in ("ffn1_w_down", "ffn2_w_down", "w_attn_branch", "w_ssm_branch", "w_out", "ffn1_w_gate", "ffn1_w_up",
              "ffn2_w_gate", "ffn2_w_up", "w_q_b"):
        out[n] = _reduce_adamw(recv[n], w2[n], m2[n], v2[n], name=f"adamw_{n}")
    out["w_kv_b"] = _reduce_t_adamw(recv["w_kv_b"], w2["w_kv_b"], m2["w_kv_b"], v2["w_kv_b"], name="adamw_w_kv_b")
    g_in = _reduce8(recv["w_in"], name="sum_w_in", tile=W_IN_SHARD_PAD // 2)[:W_IN_SHARD]
    out["w_in"] = [g_in] + list(_adamw(g_in, w2["w_in"], m2["w_in"], v2["w_in"], name="adamw_w_in"))
    me = 4 * lax.axis_index("x") + 2 * lax.axis_index("y") + lax.axis_index("c")
    conv_g = lax.dynamic_slice(srecv, (0, CONV_ROW, me * (XBC_DIM // N_DEV)), (N_DEV, CONV_WIDTH, XBC_DIM // N_DEV))
    sn = [n for n, _ in SMALL] + ["conv_w"]
    sg, sd, sm, sv = _adamw_small(srecv, conv_g, [w2[n] for n in sn], [m2[n] for n in sn], [v2[n] for n in sn],
                                  name="adamw_small")
    for i, n in enumerate(sn):
        out[n] = (sg[i], sd[i], sm[i], sv[i])
    def back(n, a):
        if n in TRANSPOSED:
            return jnp.swapaxes(a.reshape(1, a.shape[0], a.shape[1]), 1, 2)
        return a.reshape(w_all[n].shape)

    outs = [[back(n, out[n][kind]) for n in WEIGHT_ORDER] for kind in range(4)]
    return (loss, grad_x[None], *outs[0], *outs[1], *outs[2], *outs[3])
```

```python
import math

import jax
import jax.numpy as jnp
from jax import lax
from jax.experimental import pallas as pl
from jax.experimental.pallas import tpu as pltpu

F32, BF16 = jnp.float32, jnp.bfloat16
HIGHEST = lax.Precision.HIGHEST

D_MODEL, D_FF = 1024, 2816
EPS = 1e-6
N_HEADS, QK_NOPE, QK_ROPE, QK_HEAD, V_HEAD = 16, 64, 32, 96, 64
Q_LORA, KV_LORA = 384, 256
ROPE_BASE = 10000.0
D_INNER, SSM_HEADS, SSM_GROUPS, D_STATE, CONV_WIDTH, CHUNK = 2048, 32, 4, 128, 5, 128
XBC_DIM = D_INNER + 2 * SSM_GROUPS * D_STATE
IN_DIM = 7904
ADAM_LR, ADAM_B1, ADAM_B2, ADAM_EPS, ADAM_WD, ADAM_STEP = 0.001, 0.9, 0.999, 1e-08, 0.01, 10
N_DEV = 8

V7X_VMEM_BYTES = 64 * 1024 * 1024
VMEM_LIMIT = V7X_VMEM_BYTES - 8 * 1024 * 1024
LANES = 128
W_IN_SHARD = IN_DIM // N_DEV
W_IN_SHARD_PAD = 992

SMALL = (
    ("ffn1_norm", 1024), ("mix_norm", 1024), ("q_a_norm", 384), ("kv_a_norm", 256), ("q_head_norm", 96),
    ("k_head_norm", 96), ("conv_b", 3072), ("a_log_fwd", 32), ("a_log_bwd", 32), ("dt_bias_fwd", 32),
    ("dt_bias_bwd", 32), ("d_skip", 32), ("ssm_norm", 2048), ("ffn2_norm", 1024),
)
TRANSPOSED = ("ffn1_w_gate", "ffn1_w_up", "ffn2_w_gate", "ffn2_w_up", "w_in", "w_q_b")
SMALL_ROW = {n: i for i, (n, _) in enumerate(SMALL)}
CONV_ROW = len(SMALL)
SMALL_ROWS, SMALL_COLS = 24, XBC_DIM
WEIGHT_ORDER = (
    "ffn1_norm", "ffn1_w_gate", "ffn1_w_up", "ffn1_w_down", "mix_norm", "w_in", "q_a_norm", "w_q_b", "kv_a_norm",
    "w_kv_b", "q_head_norm", "k_head_norm", "conv_w", "conv_b", "a_log_fwd", "a_log_bwd", "dt_bias_fwd", "dt_bias_bwd",
    "d_skip", "ssm_norm", "w_attn_branch", "w_ssm_branch", "w_out", "ffn2_norm", "ffn2_w_gate", "ffn2_w_up",
    "ffn2_w_down",
)


def _pallas(body, **kw):
    return pl.pallas_call(body, **kw)


def _params(sem):
    return pltpu.CompilerParams(dimension_semantics=sem, vmem_limit_bytes=VMEM_LIMIT)


def _pick(dim, pref):
    if dim <= pref:
        return dim
    c = (pref // LANES) * LANES
    while c >= LANES:
        if dim % c == 0:
            return c
        c -= LANES
    raise ValueError((dim, pref))


def _sigmoid(x):
    return 1.0 / (1.0 + jnp.exp(-x))


def _softplus(x):
    return jnp.maximum(x, 0.0) + jnp.log(1.0 + jnp.exp(-jnp.abs(x)))


def _dot(a, b):
    return jnp.dot(a, b, preferred_element_type=F32)


def _dot_nt(a, b):
    return lax.dot_general(a, b, (((1,), (1,)), ((), ())), preferred_element_type=F32)


def _dot_tn(a, b):
    return lax.dot_general(a, b, (((0,), (0,)), ((), ())), preferred_element_type=F32)


def _dot_h(a, b):
    return jnp.dot(a, b, preferred_element_type=F32, precision=HIGHEST)


def _dot_h_nt(a, b):
    return lax.dot_general(a, b, (((1,), (1,)), ((), ())), preferred_element_type=F32, precision=HIGHEST)


def _dot_h_tn(a, b):
    return lax.dot_general(a, b, (((0,), (0,)), ((), ())), preferred_element_type=F32, precision=HIGHEST)


def _mesh_pos():
    return lax.axis_index("x"), lax.axis_index("y"), lax.axis_index("c")


def _comm_scratch(n):
    return [pltpu.SemaphoreType.DMA((7 * n,)), pltpu.SemaphoreType.DMA((7 * n,)), pltpu.SemaphoreType.DMA((n,))]


def _comm_copies(modes, srcs, dsts, send_sems, recv_sems, local_sems, arrivals):
    x, y, c = _mesh_pos()
    me = 4 * x + 2 * y + c
    local, remote = [], []
    for w, (mode, s, d) in enumerate(zip(modes, srcs, dsts)):
        gather = mode == "gather"
        if not arrivals and local_sems is not None:
            local.append(pltpu.make_async_copy(s if gather else s.at[me], d.at[me], local_sems.at[w]))
        for k in range(1, N_DEV):
            px = (1 - x) if (k & 4) else x
            py = (1 - y) if (k & 2) else y
            pc = (1 - c) if (k & 1) else c
            peer = 4 * px + 2 * py + pc
            idx = 7 * w + k - 1
            remote.append(pltpu.make_async_remote_copy(
                src_ref=s if gather else s.at[peer], dst_ref=d.at[peer] if arrivals else d.at[me],
                send_sem=send_sems.at[idx], recv_sem=recv_sems.at[idx],
                device_id=(px, py, pc), device_id_type=pl.DeviceIdType.MESH))
    return local, remote


def _comm_start(modes, srcs, dsts, sems):
    local, sends = _comm_copies(modes, srcs, dsts, *sems, arrivals=False)
    for cp in local + sends:
        cp.start()


def _comm_wait(modes, srcs, dsts, sems):
    _, recvs = _comm_copies(modes, srcs, dsts, *sems, arrivals=True)
    for cp in recvs:
        cp.wait_recv()
    local, sends = _comm_copies(modes, srcs, dsts, *sems, arrivals=False)
    for cp in sends:
        cp.wait_send()
    for cp in local:
        cp.wait()


def _comm_out_shapes(modes, arrays):
    return [jax.ShapeDtypeStruct((N_DEV,) + (a.shape if m == "gather" else a.shape[1:]), a.dtype)
            for m, a in zip(modes, arrays)]


def _exchange(modes, arrays, *, name):
    n = len(arrays)

    def body(*refs):
        srcs, dsts, sems = refs[:n], refs[n:2 * n], refs[2 * n:]
        _comm_start(modes, srcs, dsts, sems)
        _comm_wait(modes, srcs, dsts, sems)

    any_spec = pl.BlockSpec(memory_space=pl.ANY)
    return _pallas(body, name=name, out_shape=_comm_out_shapes(modes, arrays), in_specs=[any_spec] * n,
                   out_specs=[any_spec] * n, scratch_shapes=_comm_scratch(n))(*arrays)


def _exchange_start(modes, arrays, *, name):
    n = len(arrays)
    me = 4 * lax.axis_index("x") + 2 * lax.axis_index("y") + lax.axis_index("c")
    lands = []
    for m, a in zip(modes, arrays):
        own = a if m == "gather" else lax.dynamic_index_in_dim(a, me, 0, keepdims=False)
        zone = lax.empty((N_DEV,) + own.shape, a.dtype)
        lands.append(lax.dynamic_update_index_in_dim(zone, own, me, 0))

    def body(*refs):
        srcs, dsts = refs[:n], refs[n:2 * n]
        send_sems, recv_sems = refs[2 * n], refs[2 * n + 1]
        token = refs[-1]
        _, sends = _comm_copies(modes, srcs, dsts, send_sems, recv_sems, None, arrivals=False)
        for cp in sends:
            cp.start()
        token[...] = jnp.zeros_like(token)

    hbm = pl.BlockSpec(memory_space=pltpu.HBM)
    sem = pl.BlockSpec(memory_space=pltpu.SEMAPHORE)
    ins = [pltpu.with_memory_space_constraint(a, pltpu.HBM) for a in list(arrays) + lands]
    got = _pallas(
        body, name=name,
        out_shape=(pltpu.SemaphoreType.DMA((7 * n,)), pltpu.SemaphoreType.DMA((7 * n,)),
                   *[pltpu.HBM(a.shape, a.dtype) for a in ins], jax.ShapeDtypeStruct((8, LANES), F32)),
        in_specs=[hbm] * (2 * n), out_specs=(sem, sem, *[hbm] * (2 * n), pl.BlockSpec(memory_space=pltpu.VMEM)),
        input_output_aliases={i: 2 + i for i in range(2 * n)},
        compiler_params=pltpu.CompilerParams(has_side_effects=pltpu.SideEffectType.DATAFLOW_SIDE_EFFECTING),
    )(*ins)
    return (got[0], got[1], got[2:2 + n], got[2 + n:2 + 2 * n]), got[-1]


def _exchange_wait(modes, started, after, *, name):
    send_sems, recv_sems, srcs, lands = started
    n = len(srcs)

    def body(*refs):
        src_refs, dst_refs = refs[:n], refs[n:2 * n]
        ssem, rsem = refs[2 * n], refs[2 * n + 1]
        _, recvs = _comm_copies(modes, src_refs, dst_refs, ssem, rsem, None, arrivals=True)
        for cp in recvs:
            cp.wait_recv()
        _, sends = _comm_copies(modes, src_refs, dst_refs, ssem, rsem, None, arrivals=False)
        for cp in sends:
            cp.wait_send()

    hbm = pl.BlockSpec(memory_space=pltpu.HBM)
    sem = pl.BlockSpec(memory_space=pltpu.SEMAPHORE)
    both = list(srcs) + list(lands)
    got = _pallas(
        body, name=name, out_shape=tuple(pltpu.HBM(a.shape, a.dtype) for a in both),
        in_specs=[hbm] * (2 * n) + [sem, sem, pl.BlockSpec(memory_space=pl.ANY)], out_specs=tuple([hbm] * (2 * n)),
        input_output_aliases={i: i for i in range(2 * n)},
        compiler_params=pltpu.CompilerParams(has_side_effects=pltpu.SideEffectType.DATAFLOW_SIDE_EFFECTING),
    )(*both, send_sems, recv_sems, after)
    return got[n:]


def _all_gather_two_level(shards, *, name):
    n = len(shards)

    def body(*refs):
        srcs, outs = refs[:n], refs[n:2 * n]
        send_sems, recv_sems, local_sems = refs[2 * n:]
        x, y, c = _mesh_pos()
        me, sibling = (x, y, c), (x, y, 1 - c)
        chips = [(1 - x, y), (x, 1 - y), (1 - x, 1 - y)]

        def blk(w, px, py, pc):
            return outs[w].at[4 * px + 2 * py + pc]

        def copy(w, k, block, to, src=None):
            return pltpu.make_async_remote_copy(
                src_ref=blk(w, *block) if src is None else src, dst_ref=blk(w, *block),
                send_sem=send_sems.at[7 * w + k], recv_sem=recv_sems.at[7 * w + k], device_id=to,
                device_id_type=pl.DeviceIdType.MESH)

        mine = [pltpu.make_async_copy(srcs[w], blk(w, *me), local_sems.at[w]) for w in range(n)]
        for cp in mine:
            cp.start()
        first = []
        for w in range(n):
            first.append(copy(w, 0, me, sibling, src=srcs[w]))
            first += [copy(w, 1 + j, me, (*chip, c), src=srcs[w]) for j, chip in enumerate(chips)]
        for cp in first:
            cp.start()
        passed = []
        for w in range(n):
            for j, chip in enumerate(chips):
                copy(w, 1 + j, (*chip, c), me).wait_recv()
                fwd = copy(w, 4 + j, (*chip, c), sibling)
                fwd.start()
                passed.append(fwd)
        for w in range(n):
            copy(w, 0, sibling, me).wait_recv()
            for j, chip in enumerate(chips):
                copy(w, 4 + j, (*chip, 1 - c), me).wait_recv()
        for cp in first + passed:
            cp.wait_send()
        for cp in mine:
            cp.wait()

    any_spec = pl.BlockSpec(memory_space=pl.ANY)
    return _pallas(body, name=name, out_shape=_comm_out_shapes(["gather"] * n, shards), in_specs=[any_spec] * n,
                   out_specs=[any_spec] * n, scratch_shapes=_comm_scratch(n))(*shards)


def _mm(a, b, *, name, ta=False, tb=False, out_dtype=F32, alpha=1.0, res=None, tm=1024, tn=1408, tk=1408,
        b_row0=None, after=None):
    (K, M) = a.shape if ta else a.shape[::-1]
    (N, Kb) = b.shape if tb else b.shape[::-1]
    tm, tn, tk = _pick(M, tm), _pick(N, tn), _pick(K, tk)
    nk = K // tk
    if b_row0 is None:
        assert K == Kb, (a.shape, b.shape, ta, tb)
        kb0 = 0
    else:
        assert not tb and b_row0 % tk == 0 and b_row0 + K <= Kb, (a.shape, b.shape, b_row0)
        kb0 = b_row0 // tk
    a_spec = pl.BlockSpec((tk, tm), lambda i, j, k: (k, i)) if ta else pl.BlockSpec((tm, tk), lambda i, j, k: (i, k))
    b_spec = (pl.BlockSpec((tn, tk), lambda i, j, k: (j, k)) if tb
              else pl.BlockSpec((tk, tn), lambda i, j, k: (k + kb0, j)))
    o_spec = pl.BlockSpec((tm, tn), lambda i, j, k: (i, j))
    dn = (((0 if ta else 1,), (1 if tb else 0,)), ((), ()))
    has_res = res is not None
    n_in = 2 + has_res + (after is not None)

    def body(*refs):
        a_ref, b_ref = refs[0], refs[1]
        r_ref = refs[2] if has_res else None
        o_ref = refs[n_in]
        part =lax.dot_general(a_ref[...].astype(BF16), b_ref[...].astype(BF16), dn, preferred_element_type=F32)

        def finish(acc):
            if alpha != 1.0:
                acc = acc * alpha
            if has_res:
                acc = acc + r_ref[...]
            o_ref[...] = acc.astype(o_ref.dtype)

        if nk == 1:
            finish(part)
        else:
            acc_ref = refs[-1]
            k = pl.program_id(2)

            @pl.when(k == 0)
            def _():
                acc_ref[...] = part

            @pl.when(k > 0)
            def _():
                acc_ref[...] += part

            @pl.when(k == nk - 1)
            def _():
                finish(acc_ref[...])

    ins = [a, b] + ([res] if has_res else [])
    in_specs = [a_spec, b_spec] + ([o_spec] if has_res else [])
    if after is not None:
        ins.append(after)
        in_specs.append(pl.BlockSpec(after.shape, lambda i, j, k: (0, 0)))
    return _pallas(
        body, name=name, grid=(M // tm, N // tn, nk), in_specs=in_specs, out_specs=o_spec,
        out_shape=jax.ShapeDtypeStruct((M, N), out_dtype),
        scratch_shapes=[pltpu.VMEM((tm, tn), F32)] if nk > 1 else [],
        compiler_params=_params(("parallel", "parallel", "arbitrary")),
    )(*ins)


def _col0(j):
    return 0


def _colj(j):
    return j


def _rowmap(fn, *, name, rows, tile, ins, consts=(), outs=(), accs=(), ncol=1):
    tile = min(tile, rows)
    nrow = rows // tile
    in_specs = [pl.BlockSpec((tile, w), lambda j, i, f=f: (i, f(j))) for _, w, f in ins]
    for arr, w, f in consts:
        in_specs.append(pl.BlockSpec((arr.shape[0], w), lambda j, i, f=f: (0, f(j))))
    out_specs = [pl.BlockSpec((tile, w), lambda j, i, f=f: (i, f(j))) for _, _, w, f in outs]
    out_specs += [pl.BlockSpec((1, w), lambda j, i, f=f: (0, f(j))) for _, w, f in accs]
    out_shape = [jax.ShapeDtypeStruct((rows, c), dt) for c, dt, _, _ in outs]
    out_shape += [jax.ShapeDtypeStruct((1, c), F32) for c, _, _ in accs]
    n_in, n_out = len(ins) + len(consts), len(outs)
    acc_fixed = [f is _col0 for _, _, f in accs]

    def body(*refs):
        res = fn(*[r[...].astype(F32) for r in refs[:n_in]])
        if not isinstance(res, (tuple, list)):
            res = (res,)
        for r, v in zip(refs[n_in:n_in + n_out], res[:n_out]):
            r[...] = v.astype(r.dtype)
        j, i = pl.program_id(0), pl.program_id(1)
        for r, v, fixed in zip(refs[n_in + n_out:], res[n_out:], acc_fixed):
            first = ((i == 0) & (j == 0)) if fixed else (i == 0)

            @pl.when(first)
            def _(r=r, v=v):
                r[...] = v

            @pl.when(jnp.logical_not(first))
            def _(r=r, v=v):
                r[...] += v

    arrays = [a for a, _, _ in ins] + [a for a, _, _ in consts]
    return _pallas(
        body, name=name, grid=(ncol, nrow), in_specs=in_specs, out_specs=out_specs, out_shape=out_shape,
        compiler_params=_params(("arbitrary", "arbitrary")),
    )(*arrays)


def _rms_fwd(x, g, *, name, tile=512):
    rows, d = x.shape

    def fn(xv, gv):
        r = lax.rsqrt(jnp.mean(xv * xv, axis=-1, keepdims=True) + EPS)
        return xv * r * gv

    return _rowmap(fn, name=name, rows=rows, tile=tile, ins=[(x, d, _col0)], consts=[(g, d, _col0)],
                   outs=[(d, BF16, d, _col0)])[0]


def _rms_bwd(dh, x, g, res, *, name, tile=512):
    rows, d = x.shape

    def fn(dhv, xv, rv, gv):
        r = lax.rsqrt(jnp.mean(xv * xv, axis=-1, keepdims=True) + EPS)
        xh = xv * r
        dxh = dhv * gv
        dx = r * (dxh - xh * jnp.mean(dxh * xh, axis=-1, keepdims=True))
        return rv + dx, jnp.sum(dhv * xh, axis=0, keepdims=True)

    return _rowmap(fn, name=name, rows=rows, tile=tile, ins=[(dh, d, _col0), (x, d, _col0), (res, d, _col0)],
                   consts=[(g, d, _col0)], outs=[(d, F32, d, _col0)], accs=[(d, d, _col0)])


def _swiglu_fwd(gu, *, name, tile=512):
    rows = gu.shape[0]
    w = _pick(D_FF, 1408)
    nb = D_FF // w

    def fn(gv, uv):
        return gv * _sigmoid(gv) * uv

    return _rowmap(fn, name=name, rows=rows, tile=tile, ncol=nb,
                   ins=[(gu, w, _colj), (gu, w, lambda j: j + nb)], outs=[(D_FF, BF16, w, _colj)])[0]


def _swiglu_bwd(da, gu, *, name, tile=512):
    rows = gu.shape[0]
    w = _pick(D_FF, 1408)
    nb = D_FF // w

    def fn(dav, gv, uv):
        sg = _sigmoid(gv)
        dg = dav * uv * (sg * (1.0 + gv * (1.0 - sg)))
        du = dav * (gv * sg)
        return dg, du

    return _rowmap(fn, name=name, rows=rows, tile=tile, ncol=nb,
                   ins=[(da, w, _colj), (gu, w, _colj), (gu, w, lambda j: j + nb)],
                   outs=[(D_FF, BF16, w, _colj), (D_FF, BF16, w, _colj)])


U_CKV, U_KPE, U_DT = 512, 768, 896


def _lora_norm_fwd(u_small, qg, kvg, *, name, tile=512):
    rows = u_small.shape[0]

    def fn(cq, ckv, qgv, kgv):
        rq = lax.rsqrt(jnp.mean(cq * cq, axis=-1, keepdims=True) + EPS)
        rk = lax.rsqrt(jnp.mean(ckv * ckv, axis=-1, keepdims=True) + EPS)
        return cq * rq * qgv, ckv * rk * kgv

    return _rowmap(fn, name=name, rows=rows, tile=tile,
                   ins=[(u_small, Q_LORA, _col0), (u_small, KV_LORA, lambda j: U_CKV // KV_LORA)],
                   consts=[(qg, Q_LORA, _col0), (kvg, KV_LORA, _col0)],
                   outs=[(Q_LORA, BF16, Q_LORA, _col0), (KV_LORA, BF16, KV_LORA, _col0)])


def _lora_norm_bwd(dcqn, dckvn, u_small, qg, kvg, dkpe, draw_f, draw_b, *, name, tile=512):
    rows = u_small.shape[0]
    tile = min(tile, rows)

    def body(dq_ref, dk_ref, u_ref, dkp_ref, df_ref, db_ref, qg_ref, kg_ref, du_ref, gq_ref, gk_ref):
        cq, ckv = u_ref[:, 0:Q_LORA], u_ref[:, U_CKV:U_CKV + KV_LORA]
        dq, dk = dq_ref[...], dk_ref[...]
        rq = lax.rsqrt(jnp.mean(cq * cq, axis=-1, keepdims=True) + EPS)
        xh = cq * rq
        dxh = dq * qg_ref[...]
        du_ref[:, 0:Q_LORA] = (rq * (dxh - xh * jnp.mean(dxh * xh, axis=-1, keepdims=True))).astype(BF16)
        du_ref[:, Q_LORA:U_CKV] = jnp.zeros((tile, U_CKV - Q_LORA), BF16)
        rk = lax.rsqrt(jnp.mean(ckv * ckv, axis=-1, keepdims=True) + EPS)
        kh = ckv * rk
        dkh = dk * kg_ref[...]
        du_ref[:, U_CKV:U_KPE] = (rk * (dkh - kh * jnp.mean(dkh * kh, axis=-1, keepdims=True))).astype(BF16)
        du_ref[:, U_KPE:U_DT] = dkp_ref[...].astype(BF16)
        du_ref[:, U_DT:U_DT + LANES] = (df_ref[...] + db_ref[...]).astype(BF16)
        gq = jnp.sum(dq * xh, axis=0, keepdims=True)
        gk = jnp.sum(dk * kh, axis=0, keepdims=True)
        i = pl.program_id(0)

        @pl.when(i == 0)
        def _():
            gq_ref[...] = gq
            gk_ref[...] = gk

        @pl.when(i > 0)
        def _():
            gq_ref[...] += gq
            gk_ref[...] += gk

    def rowblk(w):
        return pl.BlockSpec((tile, w), lambda i: (i, 0))

    def whole(w):
        return pl.BlockSpec((1, w), lambda i: (0, 0))

    return _pallas(
        body, name=name, grid=(rows // tile,),
        in_specs=[rowblk(Q_LORA), rowblk(KV_LORA), rowblk(1024), rowblk(LANES), rowblk(LANES), rowblk(LANES),
                  whole(Q_LORA), whole(KV_LORA)],
        out_specs=[rowblk(1024), whole(Q_LORA), whole(KV_LORA)],
        out_shape=[jax.ShapeDtypeStruct((rows, 1024), BF16), jax.ShapeDtypeStruct((1, Q_LORA), F32),
                   jax.ShapeDtypeStruct((1, KV_LORA), F32)],
        compiler_params=_params(("arbitrary",)),
    )(dcqn, dckvn, u_small, dkpe, draw_f, draw_b, qg, kvg)


def _rope(x, c, s1, s2):
    return x * c + pltpu.roll(x, 112, 1) * s1 + pltpu.roll(x, 16, 1) * s2


def _rope_t(d, c, s1, s2):
    return d * c + pltpu.roll(d * s1, 16, 1) + pltpu.roll(d * s2, 112, 1)


def _qk_prep_fwd(q_raw, k_raw, u_small, tabs, qg, kg, *, name, tile=256):
    rows = q_raw.shape[0]
    tile = min(tile, rows)
    scale = 1.0 / math.sqrt(QK_HEAD)

    def body(q_ref, k_ref, u_ref, c_ref, s1_ref, s2_ref, qg_ref, kg_ref, qo_ref, ko_ref):
        c, s1, s2 = c_ref[...], s1_ref[...], s2_ref[...]
        qgv, kgv = qg_ref[...], kg_ref[...]
        kpe = pltpu.roll(u_ref[:, U_KPE:U_KPE + LANES], 64, 1)
        for h in range(N_HEADS):
            hs = slice(h * LANES, (h + 1) * LANES)
            qr = q_ref[:, hs]
            rq = lax.rsqrt(jnp.sum(qr * qr, axis=-1, keepdims=True) / QK_HEAD + EPS)
            qo_ref[:, hs] = (_rope(qr * rq * qgv, c, s1, s2) * scale).astype(BF16)
            xk = k_ref[:, hs] + kpe
            rk = lax.rsqrt(jnp.sum(xk * xk, axis=-1, keepdims=True) / QK_HEAD + EPS)
            ko_ref[:, hs] = _rope(xk * rk * kgv, c, s1, s2).astype(BF16)

    wide = pl.BlockSpec((tile, 2048), lambda i: (i, 0))
    narrow = pl.BlockSpec((tile, LANES), lambda i: (i, 0))
    gain = pl.BlockSpec((1, LANES), lambda i: (0, 0))
    return _pallas(
        body, name=name, grid=(rows // tile,),
        in_specs=[wide, wide, pl.BlockSpec((tile, 1024), lambda i: (i, 0)), narrow, narrow, narrow, gain, gain],
        out_specs=[wide, wide], out_shape=[jax.ShapeDtypeStruct((rows, 2048), BF16)] * 2,
        compiler_params=_params(("parallel",)),
    )(q_raw, k_raw, u_small, *tabs, qg, kg)


def _qk_prep_bwd(dq, dk, q_raw, k_raw, u_small, tabs, qg, kg, *, name, tile=256):
    rows = q_raw.shape[0]
    tile = min(tile, rows)
    scale = 1.0 / math.sqrt(QK_HEAD)

    def body(dq_ref, dk_ref, q_ref, k_ref, u_ref, c_ref, s1_ref, s2_ref, qg_ref, kg_ref,
             dqo_ref, dko_ref, dkpe_ref, gq_ref, gk_ref):
        c, s1, s2 = c_ref[...], s1_ref[...], s2_ref[...]
        qgv, kgv = qg_ref[...], kg_ref[...]
        kpe = pltpu.roll(u_ref[:, U_KPE:U_KPE + LANES], 64, 1)
        lane = lax.broadcasted_iota(jnp.int32, (tile, LANES), 1)
        gq = jnp.zeros((1, LANES), F32)
        gk = jnp.zeros((1, LANES), F32)
        dkpe = jnp.zeros((tile, LANES), F32)
        for h in range(N_HEADS):
            hs = slice(h * LANES, (h + 1) * LANES)
            qr = q_ref[:, hs]
            rq = lax.rsqrt(jnp.sum(qr * qr, axis=-1, keepdims=True) / QK_HEAD + EPS)
            xh = qr * rq
            dy = _rope_t(dq_ref[:, hs] * scale, c, s1, s2)
            dxh = dy * qgv
            dqo_ref[:, hs] = (rq * (dxh - xh * (jnp.sum(dxh * xh, axis=-1, keepdims=True) / QK_HEAD))).astype(BF16)
            gq = gq + jnp.sum(dy * xh, axis=0, keepdims=True)
            xk = k_ref[:, hs] + kpe
            rk = lax.rsqrt(jnp.sum(xk * xk, axis=-1, keepdims=True) / QK_HEAD + EPS)
            kh = xk * rk
            dyk = _rope_t(dk_ref[:, hs], c, s1, s2)
            dkh = dyk * kgv
            dxk = rk * (dkh - kh * (jnp.sum(dkh * kh, axis=-1, keepdims=True) / QK_HEAD))
            gk = gk + jnp.sum(dyk * kh, axis=0, keepdims=True)
            dko_ref[:, hs] = jnp.where(lane < QK_NOPE, dxk, 0.0).astype(BF16)
            dkpe = dkpe + dxk
        dkpe_ref[...] = jnp.where(lane < QK_ROPE, pltpu.roll(dkpe, 64, 1), 0.0)
        i = pl.program_id(0)

        @pl.when(i == 0)
        def _():
            gq_ref[...] = gq
            gk_ref[...] = gk

        @pl.when(i > 0)
        def _():
            gq_ref[...] += gq
            gk_ref[...] += gk

    wide = pl.BlockSpec((tile, 2048), lambda i: (i, 0))
    narrow = pl.BlockSpec((tile, LANES), lambda i: (i, 0))
    gain = pl.BlockSpec((1, LANES), lambda i: (0, 0))
    return _pallas(
        body, name=name, grid=(rows // tile,),
        in_specs=[wide, wide, wide, wide, pl.BlockSpec((tile, 1024), lambda i: (i, 0)), narrow, narrow, narrow,
                  gain, gain],
        out_specs=[wide, wide, narrow, gain, gain],
        out_shape=[jax.ShapeDtypeStruct((rows, 2048), BF16)] * 2
        + [jax.ShapeDtypeStruct((rows, LANES), F32), jax.ShapeDtypeStruct((1, LANES), F32),
           jax.ShapeDtypeStruct((1, LANES), F32)],
        compiler_params=_params(("arbitrary",)),
    )(dq, dk, q_raw, k_raw, u_small, *tabs, qg, kg)


def _attn_fwd(q, k, v, comm_modes, comm_arrays, *, name, tq=2048, tkc=512):
    T = q.shape[0]
    tq = min(tq, T)
    tkc = min(tkc, T)
    n = len(comm_arrays)
    nj, ni = N_HEADS // 2, T // tq

    def body(*refs):
        q_ref, k_ref, v_ref = refs[:3]
        srcs = refs[3:3 + n]
        o_ref, lse_ref = refs[3 + n:5 + n]
        dsts = refs[5 + n:5 + 2 * n]
        sems = refs[5 + 2 * n:]
        j, i = pl.program_id(0), pl.program_id(1)

        @pl.when((j == 0) & (i == 0))
        def _():
            _comm_start(comm_modes, srcs, dsts, sems)

        lane = lax.broadcasted_iota(jnp.int32, (1, LANES), 1)
        out = None
        for hh in range(2):
            sl = slice(hh * LANES, (hh + 1) * LANES)
            qv = q_ref[:, sl]
            spare = LANES - 1 if hh == 0 else 0
            keep = (lane < V_HEAD) if hh == 0 else (lane >= V_HEAD)
            m = acc = None
            for kc in range(T // tkc):
                ks = slice(kc * tkc, (kc + 1) * tkc)
                s = _dot_nt(qv, k_ref[ks, sl])
                vone = jnp.where(lane == spare, 1.0, v_ref[ks, sl]).astype(BF16)
                mc = jnp.max(s, axis=-1, keepdims=True)
                if m is None:
                    m = mc
                    acc = _dot(jnp.exp(s - m).astype(BF16), vone)
                else:
                    m_new = jnp.maximum(m, mc)
                    acc = jnp.exp(m - m_new) * acc + _dot(jnp.exp(s - m_new).astype(BF16), vone)
                    m = m_new
            l = acc[:, spare:spare + 1]
            o = jnp.where(keep, acc / l, 0.0)
            out = o if out is None else out + o
            lse_ref[hh] = m + jnp.log(l)
        o_ref[...] = out

        @pl.when((j == nj - 1) & (i == ni - 1))
        def _():
            _comm_wait(comm_modes, srcs, dsts, sems)

    any_spec = pl.BlockSpec(memory_space=pl.ANY)
    got = _pallas(
        body, name=name, grid=(nj, ni),
        in_specs=[pl.BlockSpec((tq, 2 * LANES), lambda j, i: (i, j)), pl.BlockSpec((T, 2 * LANES), lambda j, i: (0, j)),
                  pl.BlockSpec((T, 2 * LANES), lambda j, i: (0, j))] + [any_spec] * n,
        out_specs=[pl.BlockSpec((tq, LANES), lambda j, i: (i, j)), pl.BlockSpec((2, tq, 1), lambda j, i: (j, i, 0))]
        + [any_spec] * n,
        out_shape=[jax.ShapeDtypeStruct((T, N_HEADS * V_HEAD), F32), jax.ShapeDtypeStruct((N_HEADS, T, 1), F32)]
        + _comm_out_shapes(comm_modes, comm_arrays),
        scratch_shapes=_comm_scratch(n),
        compiler_params=_params(("arbitrary", "arbitrary")),
    )(q, k, v, *comm_arrays)
    return got[0], got[1], got[2:]


def _attn_bwd(q, k, v, o, lse, do, comm_modes, comm_arrays, *, name, tk=512, tqc=4096):
    T = q.shape[0]
    tk = min(tk, T)
    tqc = min(tqc, T)
    n = len(comm_arrays)
    nj, nkb = N_HEADS // 2, T // tk

    def body(*refs):
        q_ref, k_ref, v_ref, o_ref, lse_ref, do_ref = refs[:6]
        srcs = refs[6:6 + n]
        dq_ref, dk_ref, dv_ref = refs[6 + n:9 + n]
        dsts = refs[9 + n:9 + 2 * n]
        d_s = refs[9 + 2 * n]
        sems = refs[10 + 2 * n:]
        j, kb = pl.program_id(0), pl.program_id(1)

        @pl.when((j == 0) & (kb == 0))
        def _():
            _comm_start(comm_modes, srcs, dsts, sems)

        lane = lax.broadcasted_iota(jnp.int32, (1, LANES), 1)
        @pl.when(kb == 0)
        def _():
            prod = do_ref[...] * o_ref[...]
            for hh in range(2):
                keep = (lane < V_HEAD) if hh == 0 else (lane >= V_HEAD)
                d_s[hh] = jnp.sum(jnp.where(keep, prod, 0.0), axis=-1, keepdims=True)

        for hh in range(2):
            sl = slice(hh * LANES, (hh + 1) * LANES)
            keep = (lane < V_HEAD) if hh == 0 else (lane >= V_HEAD)
            kv, vv = k_ref[:, sl], v_ref[:, sl]
            dv_acc = dk_acc = None
            for qc in range(T // tqc):
                qs = slice(qc * tqc, (qc + 1) * tqc)
                qv = q_ref[qs, sl]
                do_b = do_ref[qs, :].astype(BF16)
                s = _dot_nt(qv, kv)
                p = jnp.exp(s - lse_ref[hh, qs])
                dp = _dot_nt(do_b, vv)
                ds = (p * (dp - d_s[hh, qs])).astype(BF16)
                dvc = _dot_tn(p.astype(BF16), do_b)
                dkc = _dot_tn(ds, qv)
                dv_acc = dvc if dv_acc is None else dv_acc + dvc
                dk_acc = dkc if dk_acc is None else dk_acc + dkc
                dqp = _dot(ds, kv)

                @pl.when(kb == 0)
                def _(dqp=dqp, sl=sl, qs=qs):
                    dq_ref[qs, sl] = dqp

                @pl.when(kb > 0)
                def _(dqp=dqp, sl=sl, qs=qs):
                    dq_ref[qs, sl] += dqp

            dv_ref[:, sl] = jnp.where(keep, dv_acc, 0.0).astype(BF16)
            dk_ref[:, sl] = dk_acc

        @pl.when((j == nj - 1) & (kb == nkb - 1))
        def _():
            _comm_wait(comm_modes, srcs, dsts, sems)

    any_spec = pl.BlockSpec(memory_space=pl.ANY)
    pair = pl.BlockSpec((T, 2 * LANES), lambda j, kb: (0, j))
    kblk = pl.BlockSpec((tk, 2 * LANES), lambda j, kb: (kb, j))
    got = _pallas(
        body, name=name, grid=(nj, nkb),
        in_specs=[pair, kblk, kblk, pl.BlockSpec((T, LANES), lambda j, kb: (0, j)),
                  pl.BlockSpec((2, T, 1), lambda j, kb: (j, 0, 0)), pl.BlockSpec((T, LANES), lambda j, kb: (0, j))]
        + [any_spec] * n,
        out_specs=[pair, kblk, kblk] + [any_spec] * n,
        out_shape=[jax.ShapeDtypeStruct((T, 2048), F32)] * 2 + [jax.ShapeDtypeStruct((T, 2048), BF16)]
        + _comm_out_shapes(comm_modes, comm_arrays),
        scratch_shapes=[pltpu.VMEM((2, T, 1), F32)] + _comm_scratch(n),
        compiler_params=_params(("arbitrary", "arbitrary")),
    )(q, k, v, o, lse, do, *comm_arrays)
    return got[0], got[1], got[2], got[3:]


CONV_ROWS, CONV_HALO = 64, 8
CONV_WIN = CONV_ROWS + 2 * CONV_HALO


def _conv_shift(x, sh, t_idx, total):
    if sh == 0:
        return x
    y = pltpu.roll(x, (-sh) % x.shape[0], 0)
    if t_idx is None:
        return y
    ok = (t_idx + sh >= 0) & (t_idx + sh < total)
    return jnp.where(ok, y, 0.0)


def _conv_positions(ws, shape):
    return ws + lax.broadcasted_iota(jnp.int32, shape, 0) if isinstance(ws, int) else None


def _aligned(v, m):
    return v if isinstance(v, int) else pl.multiple_of(v, m)


def _conv_chunks(T, chunk, carry):
    n = T // CONV_ROWS
    carry = chunk(0, 0, carry)

    def mid(ci, c):
        return chunk(pl.multiple_of(ci * CONV_ROWS - CONV_HALO, CONV_HALO), CONV_HALO, c)

    carry = lax.fori_loop(1, n - 1, mid, carry)
    return chunk(T - CONV_WIN, 2 * CONV_HALO, carry)


def _conv_pre(x, w_ref, b_ref, t_idx, total):
    pre = b_ref[...] + w_ref[2:3, :] * x
    for j in (0, 1, 3, 4):
        pre = pre + w_ref[j:j + 1, :] * _conv_shift(x, j - 2, t_idx, total)
    return pre


def _conv_fwd(u_big, conv_w, conv_b, *, name, w=256):
    T = u_big.shape[0]
    first = D_INNER // w

    def body(x_ref, w_ref, b_ref, o_ref):
        def chunk(ws, off, carry):
            x = x_ref[pl.ds(ws, CONV_WIN), :]
            pre = _conv_pre(x, w_ref, b_ref, _conv_positions(ws, x.shape), T)
            act = pre * _sigmoid(pre)
            o_ref[pl.ds(_aligned(ws + off, CONV_ROWS), CONV_ROWS), :] = act[off:off + CONV_ROWS]
            return carry

        _conv_chunks(T, chunk, 0)

    return _pallas(
        body, name=name, grid=(XBC_DIM // w,),
        in_specs=[pl.BlockSpec((T, w), lambda j: (0, j + first)), pl.BlockSpec((CONV_WIDTH, w), lambda j: (0, j)),
                  pl.BlockSpec((1, w), lambda j: (0, j))],
        out_specs=pl.BlockSpec((T, w), lambda j: (0, j)),
        out_shape=jax.ShapeDtypeStruct((T, XBC_DIM), F32),
        compiler_params=_params(("parallel",)),
    )(u_big, conv_w, conv_b)


def _conv_bwd(dact_f, dact_b, u_big, conv_w, conv_b, *, name, w=128):
    T = u_big.shape[0]
    first = D_INNER // w

    def body(df_ref, db_ref, x_ref, w_ref, b_ref, dx_ref, dw_ref, dbias_ref):
        def chunk(ws, off, sums):
            rows = pl.ds(ws, CONV_WIN)
            x = x_ref[rows, :]
            row = lax.broadcasted_iota(jnp.int32, x.shape, 0)
            t_idx = _conv_positions(ws, x.shape)
            pre = _conv_pre(x, w_ref, b_ref, t_idx, T)
            sg = _sigmoid(pre)
            dpre = (df_ref[rows, :] + db_ref[rows, :]) * (sg * (1.0 + pre * (1.0 - sg)))
            dx = w_ref[2:3, :] * dpre
            for j in (0, 1, 3, 4):
                dx = dx + w_ref[j:j + 1, :] * _conv_shift(dpre, 2 - j, t_idx, T)
            dx_ref[pl.ds(_aligned(ws + off, CONV_ROWS), CONV_ROWS), :] = dx[off:off + CONV_ROWS].astype(dx_ref.dtype)
            own = jnp.where((row >= off) & (row < off + CONV_ROWS), dpre, 0.0)
            new = [sums[5] + jnp.sum(own, axis=0, keepdims=True)]
            for j in range(CONV_WIDTH):
                new.insert(j, sums[j] + jnp.sum(own * _conv_shift(x, j - 2, t_idx, T), axis=0, keepdims=True))
            return tuple(new)

        zero = jnp.zeros((1, w), F32)
        sums = _conv_chunks(T, chunk, (zero,) * (CONV_WIDTH + 1))
        for j in range(CONV_WIDTH):
            dw_ref[j:j + 1, :] = sums[j]
        dbias_ref[...] = sums[CONV_WIDTH]

    blk = pl.BlockSpec((T, w), lambda j: (0, j))
    return _pallas(
        body, name=name, grid=(XBC_DIM // w,),
        in_specs=[blk, blk, pl.BlockSpec((T, w), lambda j: (0, j + first)),
                  pl.BlockSpec((CONV_WIDTH, w), lambda j: (0, j)), pl.BlockSpec((1, w), lambda j: (0, j))],
        out_specs=[blk, pl.BlockSpec((CONV_WIDTH, w), lambda j: (0, j)), pl.BlockSpec((1, w), lambda j: (0, j))],
        out_shape=[jax.ShapeDtypeStruct((T, XBC_DIM), BF16), jax.ShapeDtypeStruct((CONV_WIDTH, XBC_DIM), F32),
                   jax.ShapeDtypeStruct((1, XBC_DIM), F32)],
        compiler_params=_params(("parallel",)),
    )(dact_f, dact_b, u_big, conv_w, conv_b)


def _ssd_expand(rev):
    off = SSM_HEADS if rev else 0
    h = jnp.arange(LANES, dtype=jnp.int32)[:, None]
    return (jnp.arange(D_INNER, dtype=jnp.int32)[None, :] // 64 + off == h).astype(F32)


def _ssd_head_terms(dt_ref, bias_ref, alog_ref, acst_s, dtt_s, rev):
    L = CHUNK
    row = lax.broadcasted_iota(jnp.int32, (L, L), 0)
    col = lax.broadcasted_iota(jnp.int32, (L, L), 1)
    mask = (row <= col) if rev else (row >= col)
    cm = mask.astype(F32)
    cmt = ((row >= col) if rev else (row <= col)).astype(F32)
    pre = dt_ref[...] + bias_ref[...]
    dt = _softplus(pre)
    a = -jnp.exp(alog_ref[...])
    da = dt * a
    acs = _dot_h(cm, da)
    acst_s[...] = _dot_h_tn(da, cmt)
    dtt_s[...] = _dot_h_tn(dt, (row == col).astype(F32))
    tot = jnp.sum(da, axis=0, keepdims=True)
    w = jnp.exp(tot - acs)
    return dict(mask=mask, cm=cm, cmt=cmt, ident=(row == col).astype(F32), pre=pre, dt=dt, a=a, da=da, acs=acs,
                tot=tot, e=jnp.exp(acs), w=w, wdt=w * dt, dec=jnp.exp(tot))


def _pair(lo, v, h0):
    return jnp.where(lo, v[:, h0:h0 + 1], v[:, h0 + 1:h0 + 2])


def _ssd_fwd(xbc_act, u_small, bias128, alog128, *, rev, name):
    T = xbc_act.shape[0]
    L = CHUNK
    nc = T // L
    off = SSM_HEADS if rev else 0

    def cidx(c):
        return (nc - 1 - c) if rev else c

    def body(xs_ref, bm_ref, cm_ref, dt_ref, bias_ref, alog_ref, y_ref, hin_ref, ht_s, acst_s, dtt_s, wx_s, dec_s):
        c = pl.program_id(0)

        @pl.when(c == 0)
        def _():
            ht_s[...] = jnp.zeros_like(ht_s)

        t = _ssd_head_terms(dt_ref, bias_ref, alog_ref, acst_s, dtt_s, rev)
        lo = lax.broadcasted_iota(jnp.int32, (L, LANES), 1) < 64
        lo1 = lax.broadcasted_iota(jnp.int32, (1, LANES), 1) < 64
        for g in range(SSM_GROUPS):
            bmat = bm_ref[:, g * LANES:(g + 1) * LANES].astype(BF16)
            cmat = cm_ref[:, g * LANES:(g + 1) * LANES].astype(BF16)
            gmat = _dot_nt(cmat, bmat)
            ht = ht_s[g]
            ch = _dot(cmat, ht.astype(BF16))
            for pr in range(4):
                ps = slice(pr * LANES, (pr + 1) * LANES)
                cs = slice(g * 512 + pr * LANES, g * 512 + (pr + 1) * LANES)
                h0 = off + 8 * g + 2 * pr
                xp = xs_ref[:, cs]
                acc = _pair(lo, t["e"], h0) * ch[:, ps]
                for s_ in range(2):
                    h = h0 + s_
                    seg = t["acs"][:, h:h + 1] - acst_s[h:h + 1, :]
                    lam = jnp.exp(jnp.where(t["mask"], seg, -1e30))
                    m = (gmat * lam * dtt_s[h:h + 1, :]).astype(BF16)
                    xm = jnp.where(lo if s_ == 0 else jnp.logical_not(lo), xp, 0.0).astype(BF16)
                    acc = acc + _dot(m, xm)
                y_ref[:, cs] = acc
                wx_s[:, ps] = (_pair(lo, t["wdt"], h0) * xp).astype(BF16)
                dec_s[0:1, ps] = _pair(lo1, t["dec"], h0)
            hin_ref[0, g] = ht.astype(BF16)
            ht_s[g] = ht * dec_s[0:1, :] + _dot_tn(bmat, wx_s[...])

    return _pallas(
        body, name=name, grid=(nc,),
        in_specs=[pl.BlockSpec((L, D_INNER), lambda c: (cidx(c), 0)), pl.BlockSpec((L, 512), lambda c: (cidx(c), 4)),
                  pl.BlockSpec((L, 512), lambda c: (cidx(c), 5)),
                  pl.BlockSpec((L, LANES), lambda c: (cidx(c), U_DT // LANES)),
                  pl.BlockSpec((1, LANES), lambda c: (0, 0)), pl.BlockSpec((1, LANES), lambda c: (0, 0))],
        out_specs=[pl.BlockSpec((L, D_INNER), lambda c: (cidx(c), 0)),
                   pl.BlockSpec((1, SSM_GROUPS, D_STATE, 512), lambda c: (cidx(c), 0, 0, 0))],
        out_shape=[jax.ShapeDtypeStruct((T, D_INNER), F32), jax.ShapeDtypeStruct((nc, SSM_GROUPS, D_STATE, 512), BF16)],
        scratch_shapes=[pltpu.VMEM((SSM_GROUPS, D_STATE, 512), F32), pltpu.VMEM((LANES, L), F32),
                        pltpu.VMEM((LANES, L), F32), pltpu.VMEM((L, 512), BF16), pltpu.VMEM((8, 512), F32)],
        compiler_params=_params(("arbitrary",)),
    )(xbc_act, xbc_act, xbc_act, u_small, bias128, alog128)


def _ssd_bwd(dy, xbc_act, u_small, bias128, alog128, hin, skip_x, *, rev, name):
    T = xbc_act.shape[0]
    L = CHUNK
    nc = T // L
    off = SSM_HEADS if rev else 0
    has_skip = skip_x is not None

    def cidx(c):
        return c if rev else (nc - 1 - c)

    def body(*refs):
        (dy_ref, xs_ref, bm_ref, cm_ref, dt_ref, bias_ref, alog_ref, hin_ref) = refs[:8]
        k = 8
        skip_ref = refs[k] if has_skip else None
        k += 1 if has_skip else 0
        (dx_ref, draw_ref, dalog_ref, dbias_ref, dht_s, acst_s, dtt_s, rowt_s, ddtt_s, wx_s, edy_s, dec_s) = refs[k:]
        c = pl.program_id(0)

        @pl.when(c == 0)
        def _():
            dht_s[...] = jnp.zeros_like(dht_s)
            rowt_s[...] = jnp.zeros_like(rowt_s)
            ddtt_s[...] = jnp.zeros_like(ddtt_s)

        t = _ssd_head_terms(dt_ref, bias_ref, alog_ref, acst_s, dtt_s, rev)
        lane1 = lax.broadcasted_iota(jnp.int32, (1, LANES), 1)
        lo = lax.broadcasted_iota(jnp.int32, (L, LANES), 1) < 64
        lo1 = lane1 < 64
        colpart = jnp.zeros((L, LANES), F32)
        u_cols = jnp.zeros((L, LANES), F32)
        v_cols = jnp.zeros((L, LANES), F32)
        dtot_h = jnp.zeros((1, LANES), F32)
        for g in range(SSM_GROUPS):
            bmat = bm_ref[:, g * LANES:(g + 1) * LANES].astype(BF16)
            cmat = cm_ref[:, g * LANES:(g + 1) * LANES].astype(BF16)
            gmat = _dot_nt(cmat, bmat)
            ht_in = hin_ref[0, g]
            dht = dht_s[g]
            ht_in_b, dht_b = ht_in.astype(BF16), dht.astype(BF16)
            ch = _dot(cmat, ht_in_b)
            bdh = _dot(bmat, dht_b)
            th = jnp.sum(dht * ht_in, axis=0, keepdims=True)
            dgm = jnp.zeros((L, L), F32)
            for pr in range(4):
                ps = slice(pr * LANES, (pr + 1) * LANES)
                cs = slice(g * 512 + pr * LANES, g * 512 + (pr + 1) * LANES)
                h0 = off + 8 * g + 2 * pr
                xp = xs_ref[:, cs]
                dyp = dy_ref[:, cs]
                dyp_b = dyp.astype(BF16)
                wdt_p = _pair(lo, t["wdt"], h0)
                e_p = _pair(lo, t["e"], h0)
                xb = xp * bdh[:, ps]
                dc = dyp * ch[:, ps]
                dxp = wdt_p * bdh[:, ps]
                for s_ in range(2):
                    h = h0 + s_
                    keep = lo if s_ == 0 else jnp.logical_not(lo)
                    keep1 = lo1 if s_ == 0 else jnp.logical_not(lo1)
                    onehot = (lane1 == h).astype(F32)
                    dtrow = dtt_s[h:h + 1, :]
                    seg = t["acs"][:, h:h + 1] - acst_s[h:h + 1, :]
                    lam = jnp.exp(jnp.where(t["mask"], seg, -1e30))
                    mf0 = gmat * lam
                    m = (mf0 * dtrow).astype(BF16)
                    xm = jnp.where(keep, xp, 0.0).astype(BF16)
                    dm = _dot_nt(dyp_b, xm)
                    r = dm * mf0
                    q = r * dtrow
                    dgm = dgm + dm * lam * dtrow
                    colpart = colpart + jnp.sum(q, axis=1, keepdims=True) * onehot
                    rowt_s[h:h + 1, :] = jnp.sum(q, axis=0, keepdims=True)
                    ddtt_s[h:h + 1, :] = jnp.sum(r, axis=0, keepdims=True)
                    u_cols = u_cols + jnp.sum(jnp.where(keep, xb, 0.0), axis=1, keepdims=True) * onehot
                    v_cols = v_cols + jnp.sum(jnp.where(keep, dc, 0.0), axis=1, keepdims=True) * onehot
                    dtot_h = dtot_h + jnp.sum(jnp.where(keep1, th[:, ps], 0.0), axis=1, keepdims=True) * onehot
                    dxp = dxp + jnp.where(keep, _dot_tn(m, dyp_b), 0.0)
                if has_skip:
                    dxp = dxp + dyp * skip_ref[:, cs]
                dx_ref[:, cs] = dxp
                wx_s[:, ps] = (wdt_p * xp).astype(BF16)
                edy_s[:, ps] = (e_p * dyp).astype(BF16)
                dec_s[0:1, ps] = _pair(lo1, t["dec"], h0)
            edy_b = edy_s[...]
            dgm_b = dgm.astype(BF16)
            dx_ref[:, D_INNER + g * LANES:D_INNER + (g + 1) * LANES] = (
                _dot_nt(wx_s[...], dht_b) + _dot_tn(dgm_b, cmat))
            dx_ref[:, D_INNER + 512 + g * LANES:D_INNER + 512 + (g + 1) * LANES] = (
                _dot_nt(edy_b, ht_in_b) + _dot(dgm_b, bmat))
            dht_s[g] = dec_s[0:1, :] * dht + _dot_tn(cmat, edy_b)

        t_e = v_cols * t["e"]
        t_w = u_cols * t["wdt"]
        colsum_part = _dot_h_tn(rowt_s[...], t["ident"])
        dtot = jnp.sum(t_w, axis=0, keepdims=True) + t["dec"] * dtot_h
        row1 = lax.broadcasted_iota(jnp.int32, (L, LANES), 0)
        last = row1 == (0 if rev else L - 1)
        dacs = colpart - colsum_part + t_e - t_w + jnp.where(last, dtot, 0.0)
        dda = _dot_h(t["cmt"], dacs)
        ddt = dda * t["a"] + u_cols * t["w"] + _dot_h_tn(ddtt_s[...], t["ident"])
        dalog = jnp.sum(dda * t["dt"], axis=0, keepdims=True) * t["a"]
        draw = ddt * _sigmoid(t["pre"])
        draw_ref[...] = draw
        dbias = jnp.sum(draw, axis=0, keepdims=True)

        @pl.when(c == 0)
        def _():
            dalog_ref[...] = dalog
            dbias_ref[...] = dbias

        @pl.when(c > 0)
        def _():
            dalog_ref[...] += dalog
            dbias_ref[...] += dbias

    one = pl.BlockSpec((1, LANES), lambda c: (0, 0))
    in_specs = [pl.BlockSpec((L, D_INNER), lambda c: (cidx(c), 0)), pl.BlockSpec((L, D_INNER), lambda c: (cidx(c), 0)),
                pl.BlockSpec((L, 512), lambda c: (cidx(c), 4)), pl.BlockSpec((L, 512), lambda c: (cidx(c), 5)),
                pl.BlockSpec((L, LANES), lambda c: (cidx(c), U_DT // LANES)), one, one,
                pl.BlockSpec((1, SSM_GROUPS, D_STATE, 512), lambda c: (cidx(c), 0, 0, 0))]
    ins = [dy, xbc_act, xbc_act, xbc_act, u_small, bias128, alog128, hin]
    if has_skip:
        in_specs.append(pl.BlockSpec((1, D_INNER), lambda c: (0, 0)))
        ins.append(skip_x)
    return _pallas(
        body, name=name, grid=(nc,), in_specs=in_specs,
        out_specs=[pl.BlockSpec((L, XBC_DIM), lambda c: (cidx(c), 0)), pl.BlockSpec((L, LANES), lambda c: (cidx(c), 0)),
                   one, one],
        out_shape=[jax.ShapeDtypeStruct((T, XBC_DIM), F32), jax.ShapeDtypeStruct((T, LANES), F32),
                   jax.ShapeDtypeStruct((1, LANES), F32), jax.ShapeDtypeStruct((1, LANES), F32)],
        scratch_shapes=[pltpu.VMEM((SSM_GROUPS, D_STATE, 512), F32), pltpu.VMEM((LANES, L), F32),
                        pltpu.VMEM((LANES, L), F32), pltpu.VMEM((LANES, L), F32), pltpu.VMEM((LANES, L), F32),
                        pltpu.VMEM((L, 512), BF16), pltpu.VMEM((L, 512), BF16), pltpu.VMEM((8, 512), F32)],
        compiler_params=_params(("arbitrary",)),
    )(*ins)


def _ssm_out_fwd(y_f, y_b, xbc_act, u_big, skip_x, ssm_norm, *, name, tile=512):
    rows = y_f.shape[0]

    def fn(yf, yb, xs, z, sk, nw):
        yz = (yf + yb + sk * xs) * (z * _sigmoid(z))
        r = lax.rsqrt(jnp.mean(yz * yz, axis=-1, keepdims=True) + EPS)
        return yz * r * nw

    return _rowmap(fn, name=name, rows=rows, tile=tile, ncol=SSM_GROUPS,
                   ins=[(y_f, 512, _colj), (y_b, 512, _colj), (xbc_act, 512, _colj), (u_big, 512, _colj)],
                   consts=[(skip_x, 512, _colj), (ssm_norm, 512, _colj)], outs=[(D_INNER, BF16, 512, _colj)])[0]


def _ssm_out_bwd(dm, y_f, y_b, xbc_act, u_big, skip_x, ssm_norm, *, name, tile=512):
    rows = y_f.shape[0]

    def fn(dmv, yf, yb, xs, z, sk, nw):
        sg = _sigmoid(z)
        y = yf + yb + sk * xs
        yz = y * (z * sg)
        r = lax.rsqrt(jnp.mean(yz * yz, axis=-1, keepdims=True) + EPS)
        xh = yz * r
        dxh = dmv * nw
        dyz = r * (dxh - xh * jnp.mean(dxh * xh, axis=-1, keepdims=True))
        dy = dyz * (z * sg)
        dz = dyz * y * (sg * (1.0 + z * (1.0 - sg)))
        return dy, dz, jnp.sum(dmv * xh, axis=0, keepdims=True), jnp.sum(dy * xs, axis=0, keepdims=True)

    return _rowmap(fn, name=name, rows=rows, tile=tile, ncol=SSM_GROUPS,
                   ins=[(dm, 512, _colj), (y_f, 512, _colj), (y_b, 512, _colj), (xbc_act, 512, _colj),
                        (u_big, 512, _colj)],
                   consts=[(skip_x, 512, _colj), (ssm_norm, 512, _colj)],
                   outs=[(D_INNER, F32, 512, _colj), (D_INNER, BF16, 512, _colj)],
                   accs=[(D_INNER, 512, _colj), (D_INNER, 512, _colj)])


def _merge_fwd(pa, pb, u_big, *, name, tile=512):
    rows = pa.shape[0]

    def fn(a, b, ga, gb):
        return _sigmoid(ga) * a + _sigmoid(gb) * b

    return _rowmap(fn, name=name, rows=rows, tile=tile,
                   ins=[(pa, 1024, _col0), (pb, 1024, _col0), (u_big, 1024, lambda j: 5), (u_big, 1024, lambda j: 6)],
                   outs=[(1024, BF16, 1024, _col0)])[0]


def _merge_bwd(dmg, pa, pb, u_big, *, name, tile=512):
    rows = pa.shape[0]

    def fn(d, a, b, ga, gb):
        sa, sb = _sigmoid(ga), _sigmoid(gb)
        return d * sa, d * sb, d * a * sa * (1.0 - sa), d * b * sb * (1.0 - sb)

    return _rowmap(fn, name=name, rows=rows, tile=tile,
                   ins=[(dmg, 1024, _col0), (pa, 1024, _col0), (pb, 1024, _col0), (u_big, 1024, lambda j: 5),
                        (u_big, 1024, lambda j: 6)],
                   outs=[(1024, BF16, 1024, _col0)] * 4)


def _loss_bwd(y, target, *, name, tile=512):
    rows, d = y.shape

    def fn(yv, tv):
        err = yv - tv
        part = jnp.sum(jnp.sum(err * err, axis=-1, keepdims=True), axis=0, keepdims=True)
        return err * (1.0 / d), jnp.broadcast_to(part * (0.5 / d), (1, LANES))

    dy, part = _rowmap(fn, name=name, rows=rows, tile=tile, ins=[(y, d, _col0), (target, d, _col0)],
                       outs=[(d, F32, d, _col0)], accs=[(LANES, LANES, _col0)])
    return dy, part[0, 0]


def _small_slab(gs, dskip_ch, dalog_f, dalog_b, dbias_f, dbias_b, gkv, gqh, gkh, dconv_w, *, name):
    e_mat = _ssd_expand(False)
    full_names = ("ffn1_norm", "mix_norm", "q_a_norm", "conv_b", "ssm_norm", "ffn2_norm")
    full = [gs[n] for n in full_names]
    nf = len(full)

    def body(*refs):
        fulls = refs[:nf]
        (dsk_ref, e_ref, af_ref, ab_ref, bf_ref, bb_ref, gkv_ref, gqh_ref, gkh_ref, cw_ref, o_ref) = refs[nf:]
        o_ref[...] = jnp.zeros_like(o_ref)
        for n, r in zip(full_names, fulls):
            o_ref[SMALL_ROW[n]:SMALL_ROW[n] + 1, 0:r.shape[1]] = r[...]
        o_ref[SMALL_ROW["kv_a_norm"]:SMALL_ROW["kv_a_norm"] + 1, 0:KV_LORA] = gkv_ref[...]
        o_ref[SMALL_ROW["q_head_norm"]:SMALL_ROW["q_head_norm"] + 1, 0:LANES] = gqh_ref[...]
        o_ref[SMALL_ROW["k_head_norm"]:SMALL_ROW["k_head_norm"] + 1, 0:LANES] = gkh_ref[...]
        o_ref[SMALL_ROW["a_log_fwd"]:SMALL_ROW["a_log_fwd"] + 1, 0:LANES] = af_ref[...]
        o_ref[SMALL_ROW["a_log_bwd"]:SMALL_ROW["a_log_bwd"] + 1, 0:LANES] = pltpu.roll(ab_ref[...], 96, 1)
        o_ref[SMALL_ROW["dt_bias_fwd"]:SMALL_ROW["dt_bias_fwd"] + 1, 0:LANES] = bf_ref[...]
        o_ref[SMALL_ROW["dt_bias_bwd"]:SMALL_ROW["dt_bias_bwd"] + 1, 0:LANES] = pltpu.roll(bb_ref[...], 96, 1)
        dsk = _dot_h_nt(jnp.broadcast_to(dsk_ref[...], (8, D_INNER)), e_ref[...])
        o_ref[SMALL_ROW["d_skip"]:SMALL_ROW["d_skip"] + 1, 0:LANES] = dsk[0:1, :]
        o_ref[CONV_ROW:CONV_ROW + CONV_WIDTH, :] = cw_ref[...]

    return _pallas(body, name=name, out_shape=jax.ShapeDtypeStruct((SMALL_ROWS, SMALL_COLS), F32))(
        *full, dskip_ch, e_mat, dalog_f, dalog_b, dbias_f, dbias_b, gkv, gqh, gkh, dconv_w)


def _adamw_math(g, w, m, v):
    m2 = ADAM_B1 * m + (1.0 - ADAM_B1) * g
    v2 = ADAM_B2 * v + (1.0 - ADAM_B2) * (g * g)
    m_hat = m2 / (1.0 - ADAM_B1 ** ADAM_STEP)
    v_hat = v2 / (1.0 - ADAM_B2 ** ADAM_STEP)
    delta = -ADAM_LR * (m_hat / (jnp.sqrt(v_hat) + ADAM_EPS) + ADAM_WD * w)
    return delta, m2, v2


def _sum8(r_ref):
    g = r_ref[0].astype(F32)
    for s in range(1, N_DEV):
        g = g + r_ref[s].astype(F32)
    return g


def _reduce_adamw(recv, w, m, v, *, name, tile=256):
    _, R, C = recv.shape
    tile = _pick(R, tile) if R % LANES == 0 else R
    assert R % tile == 0

    def body(r_ref, w_ref, m_ref, v_ref, g_ref, d_ref, m2_ref, v2_ref):
        g = _sum8(r_ref)
        delta, m2, v2 = _adamw_math(g, w_ref[...], m_ref[...], v_ref[...])
        g_ref[...] = g
        d_ref[...] = delta
        m2_ref[...] = m2
        v2_ref[...] = v2

    blk = pl.BlockSpec((tile, C), lambda i: (i, 0))
    return _pallas(
        body, name=name, grid=(R // tile,),
        in_specs=[pl.BlockSpec((N_DEV, tile, C), lambda i: (0, i, 0)), blk, blk, blk], out_specs=[blk] * 4,
        out_shape=[jax.ShapeDtypeStruct((R, C), F32)] * 4, compiler_params=_params(("parallel",)),
    )(recv, w, m, v)


def _reduce_t_adamw(recv, w, m, v, *, name):
    R, cs = w.shape

    def body(r_ref, w_ref, m_ref, v_ref, g_ref, d_ref, m2_ref, v2_ref):
        g = _sum8(r_ref).T
        delta, m2, v2 = _adamw_math(g, w_ref[...], m_ref[...], v_ref[...])
        g_ref[...] = g
        d_ref[...] = delta
        m2_ref[...] = m2
        v2_ref[...] = v2

    return _pallas(body, name=name, out_shape=[jax.ShapeDtypeStruct((R, cs), F32)] * 4,
                   compiler_params=pltpu.CompilerParams(vmem_limit_bytes=VMEM_LIMIT))(recv, w, m, v)


def _reduce8(recv, *, name, tile):
    _, R, C = recv.shape

    def body(r_ref, g_ref):
        g_ref[...] = _sum8(r_ref)

    return _pallas(body, name=name, grid=(R // tile,),
                   in_specs=[pl.BlockSpec((N_DEV, tile, C), lambda i: (0, i, 0))],
                   out_specs=pl.BlockSpec((tile, C), lambda i: (i, 0)),
                   out_shape=jax.ShapeDtypeStruct((R, C), F32), compiler_params=_params(("parallel",)))(recv)


def _adamw(g, w, m, v, *, name, tile=256):
    R, C = w.shape

    def body(g_ref, w_ref, m_ref, v_ref, d_ref, m2_ref, v2_ref):
        delta, m2, v2 = _adamw_math(g_ref[...], w_ref[...], m_ref[...], v_ref[...])
        d_ref[...] = delta
        m2_ref[...] = m2
        v2_ref[...] = v2

    blk = pl.BlockSpec((R, tile), lambda i: (0, i))
    return _pallas(body, name=name, grid=(C // tile,), in_specs=[blk] * 4, out_specs=[blk] * 3,
                   out_shape=[jax.ShapeDtypeStruct((R, C), F32)] * 3, compiler_params=_params(("parallel",)))(g, w, m, v)


def _adamw_small(srecv, conv_g, ws, ms, vs, *, name):
    n = len(ws)

    def body(*refs):
        s_ref, c_ref = refs[0], refs[1]
        w_refs, m_refs, v_refs = refs[2:2 + n], refs[2 + n:2 + 2 * n], refs[2 + 2 * n:2 + 3 * n]
        outs = refs[2 + 3 * n:]
        gsum = _sum8(s_ref)
        for i in range(n):
            if i < len(SMALL):
                g = gsum[i:i + 1, 0:SMALL[i][1]]
            else:
                g = _sum8(c_ref)
            delta, m2, v2 = _adamw_math(g, w_refs[i][...], m_refs[i][...], v_refs[i][...])
            outs[i][...] = g
            outs[n + i][...] = delta
            outs[2 * n + i][...] = m2
            outs[3 * n + i][...] = v2

    shapes = [jax.ShapeDtypeStruct(w.shape, F32) for w in ws]
    got = _pallas(body, name=name, out_shape=shapes * 4,
                  compiler_params=pltpu.CompilerParams(vmem_limit_bytes=VMEM_LIMIT))(srecv, conv_g, *ws, *ms, *vs)
    return got[:n], got[n:2 * n], got[2 * n:3 * n], got[3 * n:]


def _ffn_fwd(x, norm, w_g_t, w_u_t, w_d, tag):
    h = _rms_fwd(x, norm, name=f"{tag}_rms")
    gu = _mm(h, jnp.concatenate([w_g_t, w_u_t], axis=0), name=f"{tag}_gu", tb=True, out_dtype=BF16)
    act = _swiglu_fwd(gu, name=f"{tag}_act")
    out = _mm(act, w_d, name=f"{tag}_down", alpha=0.5, res=x)
    return out, (h, gu, act)


def _ffn_bwd(dout, x, norm, w_g_t, w_u_t, w_d, saved, tag, send=None):
    h, gu, act = saved
    d_act = _mm(dout, w_d, name=f"{tag}_dact", tb=True, alpha=0.5, out_dtype=BF16)
    d_wd = _mm(act, dout, name=f"{tag}_dwd", ta=True, alpha=0.5, tm=1408, tn=1024, out_dtype=BF16)
    tok = send(("down",), [d_wd]) if send else None
    dg, du = _swiglu_bwd(d_act, gu, name=f"{tag}_dswiglu")
    d_wg_t = _mm(dg, h, name=f"{tag}_dwg", ta=True, tm=1408, tn=1024, out_dtype=BF16, after=tok)
    d_wu_t = _mm(du, h, name=f"{tag}_dwu", ta=True, tm=1408, tn=1024, out_dtype=BF16)
    tok = send(("gate", "up"), [d_wg_t, d_wu_t]) if send else None
    dh = _mm(dg, w_g_t, name=f"{tag}_dh_g", after=tok)
    dh = _mm(du, w_u_t, name=f"{tag}_dh_u", res=dh)
    dx, dnorm = _rms_bwd(dh, x, norm, dout, name=f"{tag}_drms")
    return dx, dnorm, d_wg_t, d_wu_t, d_wd


def _rope_tables(positions, T):
    pos = positions.reshape(T).astype(F32)
    inv_freq = 1.0 / (ROPE_BASE ** (jnp.arange(0, QK_ROPE, 2, dtype=F32) / QK_ROPE))
    ang = pos[:, None] * inv_freq
    cos, sin = jnp.cos(ang), jnp.sin(ang)
    one64, z64 = jnp.ones((T, 64), F32), jnp.zeros((T, 64), F32)
    z16, z32, one32 = jnp.zeros((T, 16), F32), jnp.zeros((T, 32), F32), jnp.ones((T, 32), F32)
    c = jnp.concatenate([one64, cos, cos, one32], axis=1)
    s1 = jnp.concatenate([z64, -sin, z16, z32], axis=1)
    s2 = jnp.concatenate([z64, z16, sin, z32], axis=1)
    return c, s1, s2


def _cols(g):
    n, r, cs = g.shape
    return g.transpose(1, 0, 2).reshape(r, n * cs)


def _rows(g):
    n, rs, c = g.shape
    return g.reshape(n * rs, c)


def _pad_lanes(v, n=LANES):
    return jnp.pad(v, ((0, 0), (0, n - v.shape[1])))


def _in_proj_weights(w_in_t):
    z = lambda n: jnp.zeros((n, D_MODEL), w_in_t.dtype)
    w_small_t = jnp.concatenate([w_in_t[0:384], z(128), w_in_t[384:672], z(96), w_in_t[5792:5856], z(64)], axis=0)
    w_big_t = jnp.concatenate([w_in_t[672:5792], w_in_t[5856:7904]], axis=0)
    return w_small_t, w_big_t


def _mla_up_weights(w_q_b_t, w_kv_b):
    wq = w_q_b_t.reshape(N_HEADS, QK_HEAD, Q_LORA)
    wq = jnp.pad(wq, ((0, 0), (0, LANES - QK_HEAD), (0, 0))).reshape(N_HEADS * LANES, Q_LORA)
    wkv = w_kv_b.reshape(KV_LORA, N_HEADS, QK_NOPE + V_HEAD)
    wk = jnp.pad(wkv[..., :QK_NOPE], ((0, 0), (0, 0), (0, LANES - QK_NOPE))).reshape(KV_LORA, N_HEADS * LANES)
    v = wkv[..., QK_NOPE:]
    zv = jnp.zeros_like(v)
    even = (jnp.arange(N_HEADS) % 2 == 0)[None, :, None]
    wv = jnp.where(even, jnp.concatenate([v, zv], -1), jnp.concatenate([zv, v], -1)).reshape(KV_LORA, N_HEADS * LANES)
    return wq, wk, wv


def _shard_rows(g):
    return g.reshape(N_DEV, g.shape[0] // N_DEV, g.shape[1])


def kernel(x, positions, ffn1_norm, ffn1_w_gate, ffn1_w_up, ffn1_w_down, mix_norm, w_in, q_a_norm, w_q_b, kv_a_norm, w_kv_b, q_head_norm, k_head_norm, conv_w, conv_b, a_log_fwd, a_log_bwd, dt_bias_fwd, dt_bias_bwd, d_skip, ssm_norm, w_attn_branch, w_ssm_branch, w_out, ffn2_norm, ffn2_w_gate, ffn2_w_up, ffn2_w_down, loss_target, m_ffn1_norm, m_ffn1_w_gate, m_ffn1_w_up, m_ffn1_w_down, m_mix_norm, m_w_in, m_q_a_norm, m_w_q_b, m_kv_a_norm, m_w_kv_b, m_q_head_norm, m_k_head_norm, m_conv_w, m_conv_b, m_a_log_fwd, m_a_log_bwd, m_dt_bias_fwd, m_dt_bias_bwd, m_d_skip, m_ssm_norm, m_w_attn_branch, m_w_ssm_branch, m_w_out, m_ffn2_norm, m_ffn2_w_gate, m_ffn2_w_up, m_ffn2_w_down, v_ffn1_norm, v_ffn1_w_gate, v_ffn1_w_up, v_ffn1_w_down, v_mix_norm, v_w_in, v_q_a_norm, v_w_q_b, v_kv_a_norm, v_w_kv_b, v_q_head_norm, v_k_head_norm, v_conv_w, v_conv_b, v_a_log_fwd, v_a_log_bwd, v_dt_bias_fwd, v_dt_bias_bwd, v_d_skip, v_ssm_norm, v_w_attn_branch, v_w_ssm_branch, v_w_out, v_ffn2_norm, v_ffn2_w_gate, v_ffn2_w_up, v_ffn2_w_down):
    w_all = dict(ffn1_norm=ffn1_norm, ffn1_w_gate=ffn1_w_gate, ffn1_w_up=ffn1_w_up, ffn1_w_down=ffn1_w_down, mix_norm=mix_norm, w_in=w_in, q_a_norm=q_a_norm, w_q_b=w_q_b, kv_a_norm=kv_a_norm, w_kv_b=w_kv_b, q_head_norm=q_head_norm, k_head_norm=k_head_norm, conv_w=conv_w, conv_b=conv_b, a_log_fwd=a_log_fwd, a_log_bwd=a_log_bwd, dt_bias_fwd=dt_bias_fwd, dt_bias_bwd=dt_bias_bwd, d_skip=d_skip, ssm_norm=ssm_norm, w_attn_branch=w_attn_branch, w_ssm_branch=w_ssm_branch, w_out=w_out, ffn2_norm=ffn2_norm, ffn2_w_gate=ffn2_w_gate, ffn2_w_up=ffn2_w_up, ffn2_w_down=ffn2_w_down)
    m_all = dict(ffn1_norm=m_ffn1_norm, ffn1_w_gate=m_ffn1_w_gate, ffn1_w_up=m_ffn1_w_up, ffn1_w_down=m_ffn1_w_down, mix_norm=m_mix_norm, w_in=m_w_in, q_a_norm=m_q_a_norm, w_q_b=m_w_q_b, kv_a_norm=m_kv_a_norm, w_kv_b=m_w_kv_b, q_head_norm=m_q_head_norm, k_head_norm=m_k_head_norm, conv_w=m_conv_w, conv_b=m_conv_b, a_log_fwd=m_a_log_fwd, a_log_bwd=m_a_log_bwd, dt_bias_fwd=m_dt_bias_fwd, dt_bias_bwd=m_dt_bias_bwd, d_skip=m_d_skip, ssm_norm=m_ssm_norm, w_attn_branch=m_w_attn_branch, w_ssm_branch=m_w_ssm_branch, w_out=m_w_out, ffn2_norm=m_ffn2_norm, ffn2_w_gate=m_ffn2_w_gate, ffn2_w_up=m_ffn2_w_up, ffn2_w_down=m_ffn2_w_down)
    v_all = dict(ffn1_norm=v_ffn1_norm, ffn1_w_gate=v_ffn1_w_gate, ffn1_w_up=v_ffn1_w_up, ffn1_w_down=v_ffn1_w_down, mix_norm=v_mix_norm, w_in=v_w_in, q_a_norm=v_q_a_norm, w_q_b=v_w_q_b, kv_a_norm=v_kv_a_norm, w_kv_b=v_w_kv_b, q_head_norm=v_q_head_norm, k_head_norm=v_k_head_norm, conv_w=v_conv_w, conv_b=v_conv_b, a_log_fwd=v_a_log_fwd, a_log_bwd=v_a_log_bwd, dt_bias_fwd=v_dt_bias_fwd, dt_bias_bwd=v_dt_bias_bwd, d_skip=v_d_skip, ssm_norm=v_ssm_norm, w_attn_branch=v_w_attn_branch, w_ssm_branch=v_w_ssm_branch, w_out=v_w_out, ffn2_norm=v_ffn2_norm, ffn2_w_gate=v_ffn2_w_gate, ffn2_w_up=v_ffn2_w_up, ffn2_w_down=v_ffn2_w_down)
    T = x.shape[1]
    xs_in, target = x[0], loss_target[0]
    def two_d(n, a):
        if n in TRANSPOSED:
            return jnp.swapaxes(a, 1, 2).reshape(a.shape[2], a.shape[1])
        return a.reshape(-1, a.shape[-1])

    w2 = {n: two_d(n, a) for n, a in w_all.items()}
    m2 = {n: two_d(n, a) for n, a in m_all.items()}
    v2 = {n: two_d(n, a) for n, a in v_all.items()}
    p = {n: w2[n] for n, _ in SMALL}
    bf = lambda n: w2[n].astype(BF16)

    first = ["ffn1_w_gate", "ffn1_w_up", "ffn1_w_down"]
    ge = dict(zip(first, _all_gather_two_level([bf(n) for n in first], name="gather_ffn1")))
    mixw = ["w_in", "w_q_b", "w_kv_b", "conv_w"]
    behind_first = ge["ffn1_w_down"][0, 0:1, 0:1].astype(F32) * 0.0
    mix_started, token = _exchange_start(["gather"] * len(mixw),
                                         [bf(n) for n in mixw[:3]] + [w2["conv_w"] + behind_first],
                                         name="gather_mix_start")
    ffn1_norm_f = p["ffn1_norm"] + token[0:1, 0:1]
    w_g1t, w_u1t = _rows(ge["ffn1_w_gate"]), _rows(ge["ffn1_w_up"])
    w_d1 = _rows(ge["ffn1_w_down"])
    late = ["w_attn_branch", "w_ssm_branch", "w_out", "ffn2_w_gate", "ffn2_w_up", "ffn2_w_down"]
    late_shards = [bf(n) for n in late]

    tabs = _rope_tables(positions, T)
    qg, kg = _pad_lanes(p["q_head_norm"]), _pad_lanes(p["k_head_norm"])
    bias128 = _pad_lanes(jnp.concatenate([p["dt_bias_fwd"], p["dt_bias_bwd"]], axis=1))
    alog128 = _pad_lanes(jnp.concatenate([p["a_log_fwd"], p["a_log_bwd"]], axis=1))
    skip_x = jnp.repeat(p["d_skip"], 64, axis=1)

    x1, ffn1_saved = _ffn_fwd(xs_in, ffn1_norm_f, w_g1t, w_u1t, w_d1, "ffn1")
    h2 = _rms_fwd(x1, p["mix_norm"], name="mix_rms")
    ge.update(zip(mixw, _exchange_wait(["gather"] * len(mixw), mix_started, h2, name="gather_mix_wait")))
    w_small_t, w_big_t = _in_proj_weights(_rows(ge["w_in"]))
    wq_t, wk, wv = _mla_up_weights(_rows(ge["w_q_b"]), _cols(ge["w_kv_b"]))
    conv_full = _cols(ge["conv_w"])
    u_big = _mm(h2, w_big_t, name="in_big", tb=True)
    u_small = _mm(h2, w_small_t, name="in_small", tb=True)
    cqn, ckvn = _lora_norm_fwd(u_small, p["q_a_norm"], p["kv_a_norm"], name="lora_norm")
    q_raw = _mm(cqn, wq_t, name="q_up", tb=True)
    k_raw = _mm(ckvn, wk, name="k_up")
    v = _mm(ckvn, wv, name="v_up", out_dtype=BF16)
    q, k = _qk_prep_fwd(q_raw, k_raw, u_small, tabs, qg, kg, name="qk_prep")
    a_out, lse, g_late = _attn_fwd(q, k, v, ["gather"] * len(late), late_shards, name="attn_fwd")
    gl = dict(zip(late, g_late))
    w_pa, w_pb, w_o = _rows(gl["w_attn_branch"]), _rows(gl["w_ssm_branch"]), _rows(gl["w_out"])
    w_g2t, w_u2t = _rows(gl["ffn2_w_gate"]), _rows(gl["ffn2_w_up"])
    w_d2 = _rows(gl["ffn2_w_down"])
    xbc_act = _conv_fwd(u_big, conv_full, p["conv_b"], name="conv_fwd")
    y_f, hin_f = _ssd_fwd(xbc_act, u_small, bias128, alog128, rev=False, name="ssd_fwd_f")
    y_b, hin_b = _ssd_fwd(xbc_act, u_small, bias128, alog128, rev=True, name="ssd_fwd_b")
    m_out = _ssm_out_fwd(y_f, y_b, xbc_act, u_big, skip_x, p["ssm_norm"], name="ssm_out")
    pa = _mm(a_out, w_pa, name="branch_a")
    pb = _mm(m_out, w_pb, name="branch_b")
    merged = _merge_fwd(pa, pb, u_big, name="merge")
    x2 = _mm(merged, w_o, name="mix_out", res=x1)
    y, ffn2_saved = _ffn_fwd(x2, p["ffn2_norm"], w_g2t, w_u2t, w_d2, "ffn2")
    dy, loss_part = _loss_bwd(y, target, name="loss")
    loss = lax.psum(loss_part, ("x", "y", "c"))

    gs = {}
    dx2, gs["ffn2_norm"], g_gate2, g_up2, g_down2 = _ffn_bwd(dy, x2, p["ffn2_norm"], w_g2t, w_u2t, w_d2, ffn2_saved,
                                                             "ffn2b")
    dmerged = _mm(dx2, w_o, name="d_merged", tb=True)
    g_out = _mm(merged, dx2, name="d_w_out", ta=True, out_dtype=BF16)
    dpa, dpb, dga, dgb = _merge_bwd(dmerged, pa, pb, u_big, name="d_merge")
    g_pa = _mm(a_out, dpa, name="d_w_pa", ta=True, out_dtype=BF16)
    g_pb = _mm(m_out, dpb, name="d_w_pb", ta=True, out_dtype=BF16)
    da_out = _mm(dpa, w_pa, name="d_a", tb=True)
    dm_out = _mm(dpb, w_pb, name="d_m", tb=True)
    late_grads = [_shard_rows(g) for g in (g_pa, g_pb, g_out, g_gate2, g_up2, g_down2)]
    dq, dk, dv, r_late = _attn_bwd(q, k, v, a_out, lse, da_out, ["scatter"] * len(late_grads), late_grads,
                                   name="attn_bwd")
    recv = dict(zip(late, r_late))

    dyss, dz, gs["ssm_norm"], dskip_ch = _ssm_out_bwd(dm_out, y_f, y_b, xbc_act, u_big, skip_x, p["ssm_norm"],
                                                      name="d_ssm_out")
    dact_f, draw_f, dalog_f, dbias_f = _ssd_bwd(dyss, xbc_act, u_small, bias128, alog128, hin_f, skip_x,
                                                rev=False, name="ssd_bwd_f")
    dact_b, draw_b, dalog_b, dbias_b = _ssd_bwd(dyss, xbc_act, u_small, bias128, alog128, hin_b, None,
                                                rev=True, name="ssd_bwd_b")
    dxbc, g_conv, gs["conv_b"] = _conv_bwd(dact_f, dact_b, u_big, conv_full, p["conv_b"], name="conv_bwd")

    dq_raw, dk_raw, dkpe, gqh, gkh = _qk_prep_bwd(dq, dk, q_raw, k_raw, u_small, tabs, qg, kg, name="d_qk_prep")
    g_wq_t = _mm(dq_raw, cqn, name="d_w_q", ta=True, out_dtype=BF16)
    g_wk_t = _mm(dk_raw, ckvn, name="d_w_k", ta=True, out_dtype=BF16)
    g_wv_t = _mm(dv, ckvn, name="d_w_v", ta=True, out_dtype=BF16)
    dcqn = _mm(dq_raw, wq_t, name="d_cqn")
    dckvn = _mm(dk_raw, wk, name="d_ckvn_k", tb=True)
    dckvn = _mm(dv, wv, name="d_ckvn_v", tb=True, res=dckvn)
    du_small, gs["q_a_norm"], gkv = _lora_norm_bwd(dcqn, dckvn, u_small, p["q_a_norm"], p["kv_a_norm"], dkpe,
                                                   draw_f, draw_b, name="d_lora_norm")

    dh2 = _mm(du_small, w_small_t, name="d_h2_small")
    dh2 = _mm(dz, w_big_t, name="d_h2_z", b_row0=0, res=dh2)
    dh2 = _mm(dxbc, w_big_t, name="d_h2_xbc", b_row0=2048, res=dh2)
    dh2 = _mm(dga, w_big_t, name="d_h2_ga", b_row0=5120, res=dh2)
    dh2 = _mm(dgb, w_big_t, name="d_h2_gb", b_row0=6144, res=dh2)
    gt_small = _mm(du_small, h2, name="d_w_small", ta=True, out_dtype=BF16)
    gt_z = _mm(dz, h2, name="d_w_z", ta=True, out_dtype=BF16)
    gt_xbc = _mm(dxbc, h2, name="d_w_xbc", ta=True, out_dtype=BF16)
    gt_ga = _mm(dga, h2, name="d_w_ga", ta=True, out_dtype=BF16)
    gt_gb = _mm(dgb, h2, name="d_w_gb", ta=True, out_dtype=BF16)
    dx1, gs["mix_norm"] = _rms_bwd(dh2, x1, p["mix_norm"], dx2, name="d_mix_rms")

    gt_in = jnp.concatenate([gt_small[0:384], gt_small[U_CKV:U_KPE + QK_ROPE], gt_z, gt_xbc,
                             gt_small[U_DT:U_DT + 64], gt_ga, gt_gb], axis=0)
    gt_in = jnp.pad(gt_in.reshape(N_DEV, W_IN_SHARD, D_MODEL), ((0, 0), (0, W_IN_SHARD_PAD - W_IN_SHARD), (0, 0)))
    gt_q = g_wq_t.reshape(N_HEADS, LANES, Q_LORA)[:, :QK_HEAD].reshape(N_DEV, -1, Q_LORA)
    gk3 = g_wk_t.reshape(N_HEADS, LANES, KV_LORA)[:, :QK_NOPE]
    gv3 = g_wv_t.reshape(N_HEADS, LANES, KV_LORA)
    even = (jnp.arange(N_HEADS) % 2 == 0)[:, None, None]
    gv3 = jnp.where(even, gv3[:, :V_HEAD], gv3[:, V_HEAD:])
    gt_kv = jnp.concatenate([gk3, gv3], axis=1).reshape(N_DEV, -1, KV_LORA)
    mixg = ["w_in", "w_q_b", "w_kv_b"]
    grads_started, token = _exchange_start(["scatter"] * len(mixg), [gt_in, gt_q, gt_kv], name="grad_mix_start")
    ffn1_sent = []

    def send(names, grads):
        st, tok = _exchange_start(["scatter"] * len(grads), [_shard_rows(g) for g in grads],
                                  name="grad_ffn1_" + "_".join(names) + "_start")
        ffn1_sent.append((names, st))
        return tok

    grad_x, gs["ffn1_norm"], _, _, _ = _ffn_bwd(dx1, xs_in, p["ffn1_norm"] + token[0:1, 0:1], w_g1t, w_u1t, w_d1,
                                                ffn1_saved, "ffn1b", send=send)
    recv.update(zip(mixg, _exchange_wait(["scatter"] * len(mixg), grads_started, grad_x, name="grad_mix_wait")))
    for names, st in ffn1_sent:
        got = _exchange_wait(["scatter"] * len(names), st, grad_x, name="grad_ffn1_" + "_".join(names) + "_wait")
        recv.update(zip(["ffn1_w_" + n for n in names], got))

    gsmall = _small_slab(gs, dskip_ch, dalog_f, dalog_b, dbias_f, dbias_b, gkv, gqh, gkh, g_conv, name="small_slab")
    srecv = _exchange(["gather"], [gsmall], name="grad_exchange")[0]

    out = {}
    for n in ("ffn1_w_down", "ffn2_w_down", "w_attn_branch", "w_ssm_branch", "w_out", "ffn1_w_gate", "ffn1_w_up",
              "ffn2_w_gate", "ffn2_w_up", "w_q_b"):
        out[n] = _reduce_adamw(recv[n], w2[n], m2[n], v2[n], name=f"adamw_{n}")
    out["w_kv_b"] = _reduce_t_adamw(recv["w_kv_b"], w2["w_kv_b"], m2["w_kv_b"], v2["w_kv_b"], name="adamw_w_kv_b")
    g_in = _reduce8(recv["w_in"], name="sum_w_in", tile=W_IN_SHARD_PAD // 2)[:W_IN_SHARD]
    out["w_in"] = [g_in] + list(_adamw(g_in, w2["w_in"], m2["w_in"], v2["w_in"], name="adamw_w_in"))
    me = 4 * lax.axis_index("x") + 2 * lax.axis_index("y") + lax.axis_index("c")
    conv_g = lax.dynamic_slice(srecv, (0, CONV_ROW, me * (XBC_DIM // N_DEV)), (N_DEV, CONV_WIDTH, XBC_DIM // N_DEV))
    sn = [n for n, _ in SMALL] + ["conv_w"]
    sg, sd, sm, sv = _adamw_small(srecv, conv_g, [w2[n] for n in sn], [m2[n] for n in sn], [v2[n] for n in sn],
                                  name="adamw_small")
    for i, n in enumerate(sn):
        out[n] = (sg[i], sd[i], sm[i], sv[i])
    def back(n, a):
        if n in TRANSPOSED:
            return jnp.swapaxes(a.reshape(1, a.shape[0], a.shape[1]), 1, 2)
        return a.reshape(w_all[n].shape)

    outs = [[back(n, out[n][kind]) for n in WEIGHT_ORDER] for kind in range(4)]
    return (loss, grad_x[None], *outs[0], *outs[1], *outs[2], *outs[3])
```

```python
import math

import jax
import jax.numpy as jnp
from jax import lax
from jax.experimental import pallas as pl
from jax.experimental.pallas import tpu as pltpu

F32, BF16 = jnp.float32, jnp.bfloat16
HIGHEST = lax.Precision.HIGHEST

D_MODEL, D_FF = 1024, 2816
EPS = 1e-6
N_HEADS, QK_NOPE, QK_ROPE, QK_HEAD, V_HEAD = 16, 64, 32, 96, 64
Q_LORA, KV_LORA = 384, 256
ROPE_BASE = 10000.0
D_INNER, SSM_HEADS, SSM_GROUPS, D_STATE, CONV_WIDTH, CHUNK = 2048, 32, 4, 128, 5, 128
XBC_DIM = D_INNER + 2 * SSM_GROUPS * D_STATE
IN_DIM = 7904
ADAM_LR, ADAM_B1, ADAM_B2, ADAM_EPS, ADAM_WD, ADAM_STEP = 0.001, 0.9, 0.999, 1e-08, 0.01, 10
N_DEV = 8

V7X_VMEM_BYTES = 64 * 1024 * 1024
VMEM_LIMIT = V7X_VMEM_BYTES - 8 * 1024 * 1024
LANES = 128
W_IN_SHARD = IN_DIM // N_DEV
W_IN_SHARD_PAD = 992

SMALL = (
    ("ffn1_norm", 1024), ("mix_norm", 1024), ("q_a_norm", 384), ("kv_a_norm", 256), ("q_head_norm", 96),
    ("k_head_norm", 96), ("conv_b", 3072), ("a_log_fwd", 32), ("a_log_bwd", 32), ("dt_bias_fwd", 32),
    ("dt_bias_bwd", 32), ("d_skip", 32), ("ssm_norm", 2048), ("ffn2_norm", 1024),
)
TRANSPOSED = ("ffn1_w_gate", "ffn1_w_up", "ffn2_w_gate", "ffn2_w_up", "w_in", "w_q_b")
SMALL_ROW = {n: i for i, (n, _) in enumerate(SMALL)}
CONV_ROW = len(SMALL)
SMALL_ROWS, SMALL_COLS = 24, XBC_DIM
WEIGHT_ORDER = (
    "ffn1_norm", "ffn1_w_gate", "ffn1_w_up", "ffn1_w_down", "mix_norm", "w_in", "q_a_norm", "w_q_b", "kv_a_norm",
    "w_kv_b", "q_head_norm", "k_head_norm", "conv_w", "conv_b", "a_log_fwd", "a_log_bwd", "dt_bias_fwd", "dt_bias_bwd",
    "d_skip", "ssm_norm", "w_attn_branch", "w_ssm_branch", "w_out", "ffn2_norm", "ffn2_w_gate", "ffn2_w_up",
    "ffn2_w_down",
)


def _pallas(body, **kw):
    return pl.pallas_call(body, **kw)


def _params(sem):
    return pltpu.CompilerParams(dimension_semantics=sem, vmem_limit_bytes=VMEM_LIMIT)


def _pick(dim, pref):
    if dim <= pref:
        return dim
    c = (pref // LANES) * LANES
    while c >= LANES:
        if dim % c == 0:
            return c
        c -= LANES
    raise ValueError((dim, pref))


def _sigmoid(x):
    return 1.0 / (1.0 + jnp.exp(-x))


def _softplus(x):
    return jnp.maximum(x, 0.0) + jnp.log(1.0 + jnp.exp(-jnp.abs(x)))


def _dot(a, b):
    return jnp.dot(a, b, preferred_element_type=F32)


def _dot_nt(a, b):
    return lax.dot_general(a, b, (((1,), (1,)), ((), ())), preferred_element_type=F32)


def _dot_tn(a, b):
    return lax.dot_general(a, b, (((0,), (0,)), ((), ())), preferred_element_type=F32)


def _dot_h(a, b):
    return jnp.dot(a, b, preferred_element_type=F32, precision=HIGHEST)


def _dot_h_nt(a, b):
    return lax.dot_general(a, b, (((1,), (1,)), ((), ())), preferred_element_type=F32, precision=HIGHEST)


def _dot_h_tn(a, b):
    return lax.dot_general(a, b, (((0,), (0,)), ((), ())), preferred_element_type=F32, precision=HIGHEST)


def _mesh_pos():
    return lax.axis_index("x"), lax.axis_index("y"), lax.axis_index("c")


def _comm_scratch(n):
    return [pltpu.SemaphoreType.DMA((7 * n,)), pltpu.SemaphoreType.DMA((7 * n,)), pltpu.SemaphoreType.DMA((n,))]


def _comm_copies(modes, srcs, dsts, send_sems, recv_sems, local_sems, arrivals):
    x, y, c = _mesh_pos()
    me = 4 * x + 2 * y + c
    local, remote = [], []
    for w, (mode, s, d) in enumerate(zip(modes, srcs, dsts)):
        gather = mode == "gather"
        if not arrivals and local_sems is not None:
            local.append(pltpu.make_async_copy(s if gather else s.at[me], d.at[me], local_sems.at[w]))
        for k in range(1, N_DEV):
            px = (1 - x) if (k & 4) else x
            py = (1 - y) if (k & 2) else y
            pc = (1 - c) if (k & 1) else c
            peer = 4 * px + 2 * py + pc
            idx = 7 * w + k - 1
            remote.append(pltpu.make_async_remote_copy(
                src_ref=s if gather else s.at[peer], dst_ref=d.at[peer] if arrivals else d.at[me],
                send_sem=send_sems.at[idx], recv_sem=recv_sems.at[idx],
                device_id=(px, py, pc), device_id_type=pl.DeviceIdType.MESH))
    return local, remote


def _comm_start(modes, srcs, dsts, sems):
    local, sends = _comm_copies(modes, srcs, dsts, *sems, arrivals=False)
    for cp in local + sends:
        cp.start()


def _comm_wait(modes, srcs, dsts, sems):
    _, recvs = _comm_copies(modes, srcs, dsts, *sems, arrivals=True)
    for cp in recvs:
        cp.wait_recv()
    local, sends = _comm_copies(modes, srcs, dsts, *sems, arrivals=False)
    for cp in sends:
        cp.wait_send()
    for cp in local:
        cp.wait()


def _comm_out_shapes(modes, arrays):
    return [jax.ShapeDtypeStruct((N_DEV,) + (a.shape if m == "gather" else a.shape[1:]), a.dtype)
            for m, a in zip(modes, arrays)]


def _exchange(modes, arrays, *, name):
    n = len(arrays)

    def body(*refs):
        srcs, dsts, sems = refs[:n], refs[n:2 * n], refs[2 * n:]
        _comm_start(modes, srcs, dsts, sems)
        _comm_wait(modes, srcs, dsts, sems)

    any_spec = pl.BlockSpec(memory_space=pl.ANY)
    return _pallas(body, name=name, out_shape=_comm_out_shapes(modes, arrays), in_specs=[any_spec] * n,
                   out_specs=[any_spec] * n, scratch_shapes=_comm_scratch(n))(*arrays)


def _exchange_start(modes, arrays, *, name):
    n = len(arrays)
    me = 4 * lax.axis_index("x") + 2 * lax.axis_index("y") + lax.axis_index("c")
    lands = []
    for m, a in zip(modes, arrays):
        own = a if m == "gather" else lax.dynamic_index_in_dim(a, me, 0, keepdims=False)
        zone = lax.empty((N_DEV,) + own.shape, a.dtype)
        lands.append(lax.dynamic_update_index_in_dim(zone, own, me, 0))

    def body(*refs):
        srcs, dsts = refs[:n], refs[n:2 * n]
        send_sems, recv_sems = refs[2 * n], refs[2 * n + 1]
        token = refs[-1]
        _, sends = _comm_copies(modes, srcs, dsts, send_sems, recv_sems, None, arrivals=False)
        for cp in sends:
            cp.start()
        token[...] = jnp.zeros_like(token)

    hbm = pl.BlockSpec(memory_space=pltpu.HBM)
    sem = pl.BlockSpec(memory_space=pltpu.SEMAPHORE)
    ins = [pltpu.with_memory_space_constraint(a, pltpu.HBM) for a in list(arrays) + lands]
    got = _pallas(
        body, name=name,
        out_shape=(pltpu.SemaphoreType.DMA((7 * n,)), pltpu.SemaphoreType.DMA((7 * n,)),
                   *[pltpu.HBM(a.shape, a.dtype) for a in ins], jax.ShapeDtypeStruct((8, LANES), F32)),
        in_specs=[hbm] * (2 * n), out_specs=(sem, sem, *[hbm] * (2 * n), pl.BlockSpec(memory_space=pltpu.VMEM)),
        input_output_aliases={i: 2 + i for i in range(2 * n)},
        compiler_params=pltpu.CompilerParams(has_side_effects=pltpu.SideEffectType.DATAFLOW_SIDE_EFFECTING),
    )(*ins)
    return (got[0], got[1], got[2:2 + n], got[2 + n:2 + 2 * n]), got[-1]


def _exchange_wait(modes, started, after, *, name):
    send_sems, recv_sems, srcs, lands = started
    n = len(srcs)

    def body(*refs):
        src_refs, dst_refs = refs[:n], refs[n:2 * n]
        ssem, rsem = refs[2 * n], refs[2 * n + 1]
        _, recvs = _comm_copies(modes, src_refs, dst_refs, ssem, rsem, None, arrivals=True)
        for cp in recvs:
            cp.wait_recv()
        _, sends = _comm_copies(modes, src_refs, dst_refs, ssem, rsem, None, arrivals=False)
        for cp in sends:
            cp.wait_send()

    hbm = pl.BlockSpec(memory_space=pltpu.HBM)
    sem = pl.BlockSpec(memory_space=pltpu.SEMAPHORE)
    both = list(srcs) + list(lands)
    got = _pallas(
        body, name=name, out_shape=tuple(pltpu.HBM(a.shape, a.dtype) for a in both),
        in_specs=[hbm] * (2 * n) + [sem, sem, pl.BlockSpec(memory_space=pl.ANY)], out_specs=tuple([hbm] * (2 * n)),
        input_output_aliases={i: i for i in range(2 * n)},
        compiler_params=pltpu.CompilerParams(has_side_effects=pltpu.SideEffectType.DATAFLOW_SIDE_EFFECTING),
    )(*both, send_sems, recv_sems, after)
    return got[n:]


def _all_gather_two_level(shards, *, name):
    n = len(shards)

    def body(*refs):
        srcs, outs = refs[:n], refs[n:2 * n]
        send_sems, recv_sems, local_sems = refs[2 * n:]
        x, y, c = _mesh_pos()
        me, sibling = (x, y, c), (x, y, 1 - c)
        chips = [(1 - x, y), (x, 1 - y), (1 - x, 1 - y)]

        def blk(w, px, py, pc):
            return outs[w].at[4 * px + 2 * py + pc]

        def copy(w, k, block, to, src=None):
            return pltpu.make_async_remote_copy(
                src_ref=blk(w, *block) if src is None else src, dst_ref=blk(w, *block),
                send_sem=send_sems.at[7 * w + k], recv_sem=recv_sems.at[7 * w + k], device_id=to,
                device_id_type=pl.DeviceIdType.MESH)

        mine = [pltpu.make_async_copy(srcs[w], blk(w, *me), local_sems.at[w]) for w in range(n)]
        for cp in mine:
            cp.start()
        first = []
        for w in range(n):
            first.append(copy(w, 0, me, sibling, src=srcs[w]))
            first += [copy(w, 1 + j, me, (*chip, c), src=srcs[w]) for j, chip in enumerate(chips)]
        for cp in first:
            cp.start()
        passed = []
        for w in range(n):
            for j, chip in enumerate(chips):
                copy(w, 1 + j, (*chip, c), me).wait_recv()
                fwd = copy(w, 4 + j, (*chip, c), sibling)
                fwd.start()
                passed.append(fwd)
        for w in range(n):
            copy(w, 0, sibling, me).wait_recv()
            for j, chip in enumerate(chips):
                copy(w, 4 + j, (*chip, 1 - c), me).wait_recv()
        for cp in first + passed:
            cp.wait_send()
        for cp in mine:
            cp.wait()

    any_spec = pl.BlockSpec(memory_space=pl.ANY)
    return _pallas(body, name=name, out_shape=_comm_out_shapes(["gather"] * n, shards), in_specs=[any_spec] * n,
                   out_specs=[any_spec] * n, scratch_shapes=_comm_scratch(n))(*shards)


def _mm(a, b, *, name, ta=False, tb=False, out_dtype=F32, alpha=1.0, res=None, tm=1024, tn=1408, tk=1408,
        b_row0=None, after=None):
    (K, M) = a.shape if ta else a.shape[::-1]
    (N, Kb) = b.shape if tb else b.shape[::-1]
    tm, tn, tk = _pick(M, tm), _pick(N, tn), _pick(K, tk)
    nk = K // tk
    if b_row0 is None:
        assert K == Kb, (a.shape, b.shape, ta, tb)
        kb0 = 0
    else:
        assert not tb and b_row0 % tk == 0 and b_row0 + K <= Kb, (a.shape, b.shape, b_row0)
        kb0 = b_row0 // tk
    a_spec = pl.BlockSpec((tk, tm), lambda i, j, k: (k, i)) if ta else pl.BlockSpec((tm, tk), lambda i, j, k: (i, k))
    b_spec = (pl.BlockSpec((tn, tk), lambda i, j, k: (j, k)) if tb
              else pl.BlockSpec((tk, tn), lambda i, j, k: (k + kb0, j)))
    o_spec = pl.BlockSpec((tm, tn), lambda i, j, k: (i, j))
    dn = (((0 if ta else 1,), (1 if tb else 0,)), ((), ()))
    has_res = res is not None
    n_in = 2 + has_res + (after is not None)

    def body(*refs):
        a_ref, b_ref = refs[0], refs[1]
        r_ref = refs[2] if has_res else None
        o_ref = refs[n_in]
        part =lax.dot_general(a_ref[...].astype(BF16), b_ref[...].astype(BF16), dn, preferred_element_type=F32)

        def finish(acc):
            if alpha != 1.0:
                acc = acc * alpha
            if has_res:
                acc = acc + r_ref[...]
            o_ref[...] = acc.astype(o_ref.dtype)

        if nk == 1:
            finish(part)
        else:
            acc_ref = refs[-1]
            k = pl.program_id(2)

            @pl.when(k == 0)
            def _():
                acc_ref[...] = part

            @pl.when(k > 0)
            def _():
                acc_ref[...] += part

            @pl.when(k == nk - 1)
            def _():
                finish(acc_ref[...])

    ins = [a, b] + ([res] if has_res else [])
    in_specs = [a_spec, b_spec] + ([o_spec] if has_res else [])
    if after is not None:
        ins.append(after)
        in_specs.append(pl.BlockSpec(after.shape, lambda i, j, k: (0, 0)))
    return _pallas(
        body, name=name, grid=(M // tm, N // tn, nk), in_specs=in_specs, out_specs=o_spec,
        out_shape=jax.ShapeDtypeStruct((M, N), out_dtype),
        scratch_shapes=[pltpu.VMEM((tm, tn), F32)] if nk > 1 else [],
        compiler_params=_params(("parallel", "parallel", "arbitrary")),
    )(*ins)


def _col0(j):
    return 0


def _colj(j):
    return j


def _rowmap(fn, *, name, rows, tile, ins, consts=(), outs=(), accs=(), ncol=1):
    tile = min(tile, rows)
    nrow = rows // tile
    in_specs = [pl.BlockSpec((tile, w), lambda j, i, f=f: (i, f(j))) for _, w, f in ins]
    for arr, w, f in consts:
        in_specs.append(pl.BlockSpec((arr.shape[0], w), lambda j, i, f=f: (0, f(j))))
    out_specs = [pl.BlockSpec((tile, w), lambda j, i, f=f: (i, f(j))) for _, _, w, f in outs]
    out_specs += [pl.BlockSpec((1, w), lambda j, i, f=f: (0, f(j))) for _, w, f in accs]
    out_shape = [jax.ShapeDtypeStruct((rows, c), dt) for c, dt, _, _ in outs]
    out_shape += [jax.ShapeDtypeStruct((1, c), F32) for c, _, _ in accs]
    n_in, n_out = len(ins) + len(consts), len(outs)
    acc_fixed = [f is _col0 for _, _, f in accs]

    def body(*refs):
        res = fn(*[r[...].astype(F32) for r in refs[:n_in]])
        if not isinstance(res, (tuple, list)):
            res = (res,)
        for r, v in zip(refs[n_in:n_in + n_out], res[:n_out]):
            r[...] = v.astype(r.dtype)
        j, i = pl.program_id(0), pl.program_id(1)
        for r, v, fixed in zip(refs[n_in + n_out:], res[n_out:], acc_fixed):
            first = ((i == 0) & (j == 0)) if fixed else (i == 0)

            @pl.when(first)
            def _(r=r, v=v):
                r[...] = v

            @pl.when(jnp.logical_not(first))
            def _(r=r, v=v):
                r[...] += v

    arrays = [a for a, _, _ in ins] + [a for a, _, _ in consts]
    return _pallas(
        body, name=name, grid=(ncol, nrow), in_specs=in_specs, out_specs=out_specs, out_shape=out_shape,
        compiler_params=_params(("arbitrary", "arbitrary")),
    )(*arrays)


def _rms_fwd(x, g, *, name, tile=512):
    rows, d = x.shape

    def fn(xv, gv):
        r = lax.rsqrt(jnp.mean(xv * xv, axis=-1, keepdims=True) + EPS)
        return xv * r * gv

    return _rowmap(fn, name=name, rows=rows, tile=tile, ins=[(x, d, _col0)], consts=[(g, d, _col0)],
                   outs=[(d, BF16, d, _col0)])[0]


def _rms_bwd(dh, x, g, res, *, name, tile=512):
    rows, d = x.shape

    def fn(dhv, xv, rv, gv):
        r = lax.rsqrt(jnp.mean(xv * xv, axis=-1, keepdims=True) + EPS)
        xh = xv * r
        dxh = dhv * gv
        dx = r * (dxh - xh * jnp.mean(dxh * xh, axis=-1, keepdims=True))
        return rv + dx, jnp.sum(dhv * xh, axis=0, keepdims=True)

    return _rowmap(fn, name=name, rows=rows, tile=tile, ins=[(dh, d, _col0), (x, d, _col0), (res, d, _col0)],
                   consts=[(g, d, _col0)], outs=[(d, F32, d, _col0)], accs=[(d, d, _col0)])


def _swiglu_fwd(gu, *, name, tile=512):
    rows = gu.shape[0]
    w = _pick(D_FF, 1408)
    nb = D_FF // w

    def fn(gv, uv):
        return gv * _sigmoid(gv) * uv

    return _rowmap(fn, name=name, rows=rows, tile=tile, ncol=nb,
                   ins=[(gu, w, _colj), (gu, w, lambda j: j + nb)], outs=[(D_FF, BF16, w, _colj)])[0]


def _swiglu_bwd(da, gu, *, name, tile=512):
    rows = gu.shape[0]
    w = _pick(D_FF, 1408)
    nb = D_FF // w

    def fn(dav, gv, uv):
        sg = _sigmoid(gv)
        dg = dav * uv * (sg * (1.0 + gv * (1.0 - sg)))
        du = dav * (gv * sg)
        return dg, du

    return _rowmap(fn, name=name, rows=rows, tile=tile, ncol=nb,
                   ins=[(da, w, _colj), (gu, w, _colj), (gu, w, lambda j: j + nb)],
                   outs=[(D_FF, BF16, w, _colj), (D_FF, BF16, w, _colj)])


U_CKV, U_KPE, U_DT = 512, 768, 896


def _lora_norm_fwd(u_small, qg, kvg, *, name, tile=512):
    rows = u_small.shape[0]

    def fn(cq, ckv, qgv, kgv):
        rq = lax.rsqrt(jnp.mean(cq * cq, axis=-1, keepdims=True) + EPS)
        rk = lax.rsqrt(jnp.mean(ckv * ckv, axis=-1, keepdims=True) + EPS)
        return cq * rq * qgv, ckv * rk * kgv

    return _rowmap(fn, name=name, rows=rows, tile=tile,
                   ins=[(u_small, Q_LORA, _col0), (u_small, KV_LORA, lambda j: U_CKV // KV_LORA)],
                   consts=[(qg, Q_LORA, _col0), (kvg, KV_LORA, _col0)],
                   outs=[(Q_LORA, BF16, Q_LORA, _col0), (KV_LORA, BF16, KV_LORA, _col0)])


def _lora_norm_bwd(dcqn, dckvn, u_small, qg, kvg, dkpe, draw_f, draw_b, *, name, tile=512):
    rows = u_small.shape[0]
    tile = min(tile, rows)

    def body(dq_ref, dk_ref, u_ref, dkp_ref, df_ref, db_ref, qg_ref, kg_ref, du_ref, gq_ref, gk_ref):
        cq, ckv = u_ref[:, 0:Q_LORA], u_ref[:, U_CKV:U_CKV + KV_LORA]
        dq, dk = dq_ref[...], dk_ref[...]
        rq = lax.rsqrt(jnp.mean(cq * cq, axis=-1, keepdims=True) + EPS)
        xh = cq * rq
        dxh = dq * qg_ref[...]
        du_ref[:, 0:Q_LORA] = (rq * (dxh - xh * jnp.mean(dxh * xh, axis=-1, keepdims=True))).astype(BF16)
        du_ref[:, Q_LORA:U_CKV] = jnp.zeros((tile, U_CKV - Q_LORA), BF16)
        rk = lax.rsqrt(jnp.mean(ckv * ckv, axis=-1, keepdims=True) + EPS)
        kh = ckv * rk
        dkh = dk * kg_ref[...]
        du_ref[:, U_CKV:U_KPE] = (rk * (dkh - kh * jnp.mean(dkh * kh, axis=-1, keepdims=True))).astype(BF16)
        du_ref[:, U_KPE:U_DT] = dkp_ref[...].astype(BF16)
        du_ref[:, U_DT:U_DT + LANES] = (df_ref[...] + db_ref[...]).astype(BF16)
        gq = jnp.sum(dq * xh, axis=0, keepdims=True)
        gk = jnp.sum(dk * kh, axis=0, keepdims=True)
        i = pl.program_id(0)

        @pl.when(i == 0)
        def _():
            gq_ref[...] = gq
            gk_ref[...] = gk

        @pl.when(i > 0)
        def _():
            gq_ref[...] += gq
            gk_ref[...] += gk

    def rowblk(w):
        return pl.BlockSpec((tile, w), lambda i: (i, 0))

    def whole(w):
        return pl.BlockSpec((1, w), lambda i: (0, 0))

    return _pallas(
        body, name=name, grid=(rows // tile,),
        in_specs=[rowblk(Q_LORA), rowblk(KV_LORA), rowblk(1024), rowblk(LANES), rowblk(LANES), rowblk(LANES),
                  whole(Q_LORA), whole(KV_LORA)],
        out_specs=[rowblk(1024), whole(Q_LORA), whole(KV_LORA)],
        out_shape=[jax.ShapeDtypeStruct((rows, 1024), BF16), jax.ShapeDtypeStruct((1, Q_LORA), F32),
                   jax.ShapeDtypeStruct((1, KV_LORA), F32)],
        compiler_params=_params(("arbitrary",)),
    )(dcqn, dckvn, u_small, dkpe, draw_f, draw_b, qg, kvg)


def _rope(x, c, s1, s2):
    return x * c + pltpu.roll(x, 112, 1) * s1 + pltpu.roll(x, 16, 1) * s2


def _rope_t(d, c, s1, s2):
    return d * c + pltpu.roll(d * s1, 16, 1) + pltpu.roll(d * s2, 112, 1)


def _qk_prep_fwd(q_raw, k_raw, u_small, tabs, qg, kg, *, name, tile=256):
    rows = q_raw.shape[0]
    tile = min(tile, rows)
    scale = 1.0 / math.sqrt(QK_HEAD)

    def body(q_ref, k_ref, u_ref, c_ref, s1_ref, s2_ref, qg_ref, kg_ref, qo_ref, ko_ref):
        c, s1, s2 = c_ref[...], s1_ref[...], s2_ref[...]
        qgv, kgv = qg_ref[...], kg_ref[...]
        kpe = pltpu.roll(u_ref[:, U_KPE:U_KPE + LANES], 64, 1)
        for h in range(N_HEADS):
            hs = slice(h * LANES, (h + 1) * LANES)
            qr = q_ref[:, hs]
            rq = lax.rsqrt(jnp.sum(qr * qr, axis=-1, keepdims=True) / QK_HEAD + EPS)
            qo_ref[:, hs] = (_rope(qr * rq * qgv, c, s1, s2) * scale).astype(BF16)
            xk = k_ref[:, hs] + kpe
            rk = lax.rsqrt(jnp.sum(xk * xk, axis=-1, keepdims=True) / QK_HEAD + EPS)
            ko_ref[:, hs] = _rope(xk * rk * kgv, c, s1, s2).astype(BF16)

    wide = pl.BlockSpec((tile, 2048), lambda i: (i, 0))
    narrow = pl.BlockSpec((tile, LANES), lambda i: (i, 0))
    gain = pl.BlockSpec((1, LANES), lambda i: (0, 0))
    return _pallas(
        body, name=name, grid=(rows // tile,),
        in_specs=[wide, wide, pl.BlockSpec((tile, 1024), lambda i: (i, 0)), narrow, narrow, narrow, gain, gain],
        out_specs=[wide, wide], out_shape=[jax.ShapeDtypeStruct((rows, 2048), BF16)] * 2,
        compiler_params=_params(("parallel",)),
    )(q_raw, k_raw, u_small, *tabs, qg, kg)


def _qk_prep_bwd(dq, dk, q_raw, k_raw, u_small, tabs, qg, kg, *, name, tile=256):
    rows = q_raw.shape[0]
    tile = min(tile, rows)
    scale = 1.0 / math.sqrt(QK_HEAD)

    def body(dq_ref, dk_ref, q_ref, k_ref, u_ref, c_ref, s1_ref, s2_ref, qg_ref, kg_ref,
             dqo_ref, dko_ref, dkpe_ref, gq_ref, gk_ref):
        c, s1, s2 = c_ref[...], s1_ref[...], s2_ref[...]
        qgv, kgv = qg_ref[...], kg_ref[...]
        kpe = pltpu.roll(u_ref[:, U_KPE:U_KPE + LANES], 64, 1)
        lane = lax.broadcasted_iota(jnp.int32, (tile, LANES), 1)
        gq = jnp.zeros((1, LANES), F32)
        gk = jnp.zeros((1, LANES), F32)
        dkpe = jnp.zeros((tile, LANES), F32)
        for h in range(N_HEADS):
            hs = slice(h * LANES, (h + 1) * LANES)
            qr = q_ref[:, hs]
            rq = lax.rsqrt(jnp.sum(qr * qr, axis=-1, keepdims=True) / QK_HEAD + EPS)
            xh = qr * rq
            dy = _rope_t(dq_ref[:, hs] * scale, c, s1, s2)
            dxh = dy * qgv
            dqo_ref[:, hs] = (rq * (dxh - xh * (jnp.sum(dxh * xh, axis=-1, keepdims=True) / QK_HEAD))).astype(BF16)
            gq = gq + jnp.sum(dy * xh, axis=0, keepdims=True)
            xk = k_ref[:, hs] + kpe
            rk = lax.rsqrt(jnp.sum(xk * xk, axis=-1, keepdims=True) / QK_HEAD + EPS)
            kh = xk * rk
            dyk = _rope_t(dk_ref[:, hs], c, s1, s2)
            dkh = dyk * kgv
            dxk = rk * (dkh - kh * (jnp.sum(dkh * kh, axis=-1, keepdims=True) / QK_HEAD))
            gk = gk + jnp.sum(dyk * kh, axis=0, keepdims=True)
            dko_ref[:, hs] = jnp.where(lane < QK_NOPE, dxk, 0.0).astype(BF16)
            dkpe = dkpe + dxk
        dkpe_ref[...] = jnp.where(lane < QK_ROPE, pltpu.roll(dkpe, 64, 1), 0.0)
        i = pl.program_id(0)

        @pl.when(i == 0)
        def _():
            gq_ref[...] = gq
            gk_ref[...] = gk

        @pl.when(i > 0)
        def _():
            gq_ref[...] += gq
            gk_ref[...] += gk

    wide = pl.BlockSpec((tile, 2048), lambda i: (i, 0))
    narrow = pl.BlockSpec((tile, LANES), lambda i: (i, 0))
    gain = pl.BlockSpec((1, LANES), lambda i: (0, 0))
    return _pallas(
        body, name=name, grid=(rows // tile,),
        in_specs=[wide, wide, wide, wide, pl.BlockSpec((tile, 1024), lambda i: (i, 0)), narrow, narrow, narrow,
                  gain, gain],
        out_specs=[wide, wide, narrow, gain, gain],
        out_shape=[jax.ShapeDtypeStruct((rows, 2048), BF16)] * 2
        + [jax.ShapeDtypeStruct((rows, LANES), F32), jax.ShapeDtypeStruct((1, LANES), F32),
           jax.ShapeDtypeStruct((1, LANES), F32)],
        compiler_params=_params(("arbitrary",)),
    )(dq, dk, q_raw, k_raw, u_small, *tabs, qg, kg)


def _attn_fwd(q, k, v, comm_modes, comm_arrays, *, name, tq=2048, tkc=512):
    T = q.shape[0]
    tq = min(tq, T)
    tkc = min(tkc, T)
    n = len(comm_arrays)
    nj, ni = N_HEADS // 2, T // tq

    def body(*refs):
        q_ref, k_ref, v_ref = refs[:3]
        srcs = refs[3:3 + n]
        o_ref, lse_ref = refs[3 + n:5 + n]
        dsts = refs[5 + n:5 + 2 * n]
        sems = refs[5 + 2 * n:]
        j, i = pl.program_id(0), pl.program_id(1)

        @pl.when((j == 0) & (i == 0))
        def _():
            _comm_start(comm_modes, srcs, dsts, sems)

        lane = lax.broadcasted_iota(jnp.int32, (1, LANES), 1)
        out = None
        for hh in range(2):
            sl = slice(hh * LANES, (hh + 1) * LANES)
            qv = q_ref[:, sl]
            spare = LANES - 1 if hh == 0 else 0
            keep = (lane < V_HEAD) if hh == 0 else (lane >= V_HEAD)
            m = acc = None
            for kc in range(T // tkc):
                ks = slice(kc * tkc, (kc + 1) * tkc)
                s = _dot_nt(qv, k_ref[ks, sl])
                vone = jnp.where(lane == spare, 1.0, v_ref[ks, sl]).astype(BF16)
                mc = jnp.max(s, axis=-1, keepdims=True)
                if m is None:
                    m = mc
                    acc = _dot(jnp.exp(s - m).astype(BF16), vone)
                else:
                    m_new = jnp.maximum(m, mc)
                    acc = jnp.exp(m - m_new) * acc + _dot(jnp.exp(s - m_new).astype(BF16), vone)
                    m = m_new
            l = acc[:, spare:spare + 1]
            o = jnp.where(keep, acc / l, 0.0)
            out = o if out is None else out + o
            lse_ref[hh] = m + jnp.log(l)
        o_ref[...] = out

        @pl.when((j == nj - 1) & (i == ni - 1))
        def _():
            _comm_wait(comm_modes, srcs, dsts, sems)

    any_spec = pl.BlockSpec(memory_space=pl.ANY)
    got = _pallas(
        body, name=name, grid=(nj, ni),
        in_specs=[pl.BlockSpec((tq, 2 * LANES), lambda j, i: (i, j)), pl.BlockSpec((T, 2 * LANES), lambda j, i: (0, j)),
                  pl.BlockSpec((T, 2 * LANES), lambda j, i: (0, j))] + [any_spec] * n,
        out_specs=[pl.BlockSpec((tq, LANES), lambda j, i: (i, j)), pl.BlockSpec((2, tq, 1), lambda j, i: (j, i, 0))]
        + [any_spec] * n,
        out_shape=[jax.ShapeDtypeStruct((T, N_HEADS * V_HEAD), F32), jax.ShapeDtypeStruct((N_HEADS, T, 1), F32)]
        + _comm_out_shapes(comm_modes, comm_arrays),
        scratch_shapes=_comm_scratch(n),
        compiler_params=_params(("arbitrary", "arbitrary")),
    )(q, k, v, *comm_arrays)
    return got[0], got[1], got[2:]


def _attn_bwd(q, k, v, o, lse, do, comm_modes, comm_arrays, *, name, tk=256, tqc=4096):
    T = q.shape[0]
    tk = min(tk, T)
    tqc = min(tqc, T)
    n = len(comm_arrays)
    nj, nkb = N_HEADS // 2, T // tk

    def body(*refs):
        q_ref, k_ref, v_ref, o_ref, lse_ref, do_ref = refs[:6]
        srcs = refs[6:6 + n]
        dq_ref, dk_ref, dv_ref = refs[6 + n:9 + n]
        dsts = refs[9 + n:9 + 2 * n]
        d_s = refs[9 + 2 * n]
        sems = refs[10 + 2 * n:]
        j, kb = pl.program_id(0), pl.program_id(1)

        @pl.when((j == 0) & (kb == 0))
        def _():
            _comm_start(comm_modes, srcs, dsts, sems)

        lane = lax.broadcasted_iota(jnp.int32, (1, LANES), 1)
        @pl.when(kb == 0)
        def _():
            prod = do_ref[...] * o_ref[...]
            for hh in range(2):
                keep = (lane < V_HEAD) if hh == 0 else (lane >= V_HEAD)
                d_s[hh] = jnp.sum(jnp.where(keep, prod, 0.0), axis=-1, keepdims=True)

        for hh in range(2):
            sl = slice(hh * LANES, (hh + 1) * LANES)
            keep = (lane < V_HEAD) if hh == 0 else (lane >= V_HEAD)
            kv, vv = k_ref[:, sl], v_ref[:, sl]
            dv_acc = dk_acc = None
            for qc in range(T // tqc):
                qs = slice(qc * tqc, (qc + 1) * tqc)
                qv = q_ref[qs, sl]
                do_b = do_ref[qs, :].astype(BF16)
                s = _dot_nt(qv, kv)
                p = jnp.exp(s - lse_ref[hh, qs])
                dp = _dot_nt(do_b, vv)
                ds = (p * (dp - d_s[hh, qs])).astype(BF16)
                dvc = _dot_tn(p.astype(BF16), do_b)
                dkc = _dot_tn(ds, qv)
                dv_acc = dvc if dv_acc is None else dv_acc + dvc
                dk_acc = dkc if dk_acc is None else dk_acc + dkc
                dqp = _dot(ds, kv)

                @pl.when(kb == 0)
                def _(dqp=dqp, sl=sl, qs=qs):
                    dq_ref[qs, sl] = dqp

                @pl.when(kb > 0)
                def _(dqp=dqp, sl=sl, qs=qs):
                    dq_ref[qs, sl] += dqp

            dv_ref[:, sl] = jnp.where(keep, dv_acc, 0.0).astype(BF16)
            dk_ref[:, sl] = dk_acc

        @pl.when((j == nj - 1) & (kb == nkb - 1))
        def _():
            _comm_wait(comm_modes, srcs, dsts, sems)

    any_spec = pl.BlockSpec(memory_space=pl.ANY)
    pair = pl.BlockSpec((T, 2 * LANES), lambda j, kb: (0, j))
    kblk = pl.BlockSpec((tk, 2 * LANES), lambda j, kb: (kb, j))
    got = _pallas(
        body, name=name, grid=(nj, nkb),
        in_specs=[pair, kblk, kblk, pl.BlockSpec((T, LANES), lambda j, kb: (0, j)),
                  pl.BlockSpec((2, T, 1), lambda j, kb: (j, 0, 0)), pl.BlockSpec((T, LANES), lambda j, kb: (0, j))]
        + [any_spec] * n,
        out_specs=[pair, kblk, kblk] + [any_spec] * n,
        out_shape=[jax.ShapeDtypeStruct((T, 2048), F32)] * 2 + [jax.ShapeDtypeStruct((T, 2048), BF16)]
        + _comm_out_shapes(comm_modes, comm_arrays),
        scratch_shapes=[pltpu.VMEM((2, T, 1), F32)] + _comm_scratch(n),
        compiler_params=_params(("arbitrary", "arbitrary")),
    )(q, k, v, o, lse, do, *comm_arrays)
    return got[0], got[1], got[2], got[3:]


CONV_ROWS, CONV_HALO = 64, 8
CONV_WIN = CONV_ROWS + 2 * CONV_HALO


def _conv_shift(x, sh, t_idx, total):
    if sh == 0:
        return x
    y = pltpu.roll(x, (-sh) % x.shape[0], 0)
    if t_idx is None:
        return y
    ok = (t_idx + sh >= 0) & (t_idx + sh < total)
    return jnp.where(ok, y, 0.0)


def _conv_positions(ws, shape):
    return ws + lax.broadcasted_iota(jnp.int32, shape, 0) if isinstance(ws, int) else None


def _aligned(v, m):
    return v if isinstance(v, int) else pl.multiple_of(v, m)


def _conv_chunks(T, chunk, carry):
    n = T // CONV_ROWS
    carry = chunk(0, 0, carry)

    def mid(ci, c):
        return chunk(pl.multiple_of(ci * CONV_ROWS - CONV_HALO, CONV_HALO), CONV_HALO, c)

    carry = lax.fori_loop(1, n - 1, mid, carry)
    return chunk(T - CONV_WIN, 2 * CONV_HALO, carry)


def _conv_pre(x, w_ref, b_ref, t_idx, total):
    pre = b_ref[...] + w_ref[2:3, :] * x
    for j in (0, 1, 3, 4):
        pre = pre + w_ref[j:j + 1, :] * _conv_shift(x, j - 2, t_idx, total)
    return pre


def _conv_fwd(u_big, conv_w, conv_b, *, name, w=256):
    T = u_big.shape[0]
    first = D_INNER // w

    def body(x_ref, w_ref, b_ref, o_ref):
        def chunk(ws, off, carry):
            x = x_ref[pl.ds(ws, CONV_WIN), :]
            pre = _conv_pre(x, w_ref, b_ref, _conv_positions(ws, x.shape), T)
            act = pre * _sigmoid(pre)
            o_ref[pl.ds(_aligned(ws + off, CONV_ROWS), CONV_ROWS), :] = act[off:off + CONV_ROWS]
            return carry

        _conv_chunks(T, chunk, 0)

    return _pallas(
        body, name=name, grid=(XBC_DIM // w,),
        in_specs=[pl.BlockSpec((T, w), lambda j: (0, j + first)), pl.BlockSpec((CONV_WIDTH, w), lambda j: (0, j)),
                  pl.BlockSpec((1, w), lambda j: (0, j))],
        out_specs=pl.BlockSpec((T, w), lambda j: (0, j)),
        out_shape=jax.ShapeDtypeStruct((T, XBC_DIM), F32),
        compiler_params=_params(("parallel",)),
    )(u_big, conv_w, conv_b)


def _conv_bwd(dact_f, dact_b, u_big, conv_w, conv_b, *, name, w=128):
    T = u_big.shape[0]
    first = D_INNER // w

    def body(df_ref, db_ref, x_ref, w_ref, b_ref, dx_ref, dw_ref, dbias_ref):
        def chunk(ws, off, sums):
            rows = pl.ds(ws, CONV_WIN)
            x = x_ref[rows, :]
            row = lax.broadcasted_iota(jnp.int32, x.shape, 0)
            t_idx = _conv_positions(ws, x.shape)
            pre = _conv_pre(x, w_ref, b_ref, t_idx, T)
            sg = _sigmoid(pre)
            dpre = (df_ref[rows, :] + db_ref[rows, :]) * (sg * (1.0 + pre * (1.0 - sg)))
            dx = w_ref[2:3, :] * dpre
            for j in (0, 1, 3, 4):
                dx = dx + w_ref[j:j + 1, :] * _conv_shift(dpre, 2 - j, t_idx, T)
            dx_ref[pl.ds(_aligned(ws + off, CONV_ROWS), CONV_ROWS), :] = dx[off:off + CONV_ROWS].astype(dx_ref.dtype)
            own = jnp.where((row >= off) & (row < off + CONV_ROWS), dpre, 0.0)
            new = [sums[5] + jnp.sum(own, axis=0, keepdims=True)]
            for j in range(CONV_WIDTH):
                new.insert(j, sums[j] + jnp.sum(own * _conv_shift(x, j - 2, t_idx, T), axis=0, keepdims=True))
            return tuple(new)

        zero = jnp.zeros((1, w), F32)
        sums = _conv_chunks(T, chunk, (zero,) * (CONV_WIDTH + 1))
        for j in range(CONV_WIDTH):
            dw_ref[j:j + 1, :] = sums[j]
        dbias_ref[...] = sums[CONV_WIDTH]

    blk = pl.BlockSpec((T, w), lambda j: (0, j))
    return _pallas(
        body, name=name, grid=(XBC_DIM // w,),
        in_specs=[blk, blk, pl.BlockSpec((T, w), lambda j: (0, j + first)),
                  pl.BlockSpec((CONV_WIDTH, w), lambda j: (0, j)), pl.BlockSpec((1, w), lambda j: (0, j))],
        out_specs=[blk, pl.BlockSpec((CONV_WIDTH, w), lambda j: (0, j)), pl.BlockSpec((1, w), lambda j: (0, j))],
        out_shape=[jax.ShapeDtypeStruct((T, XBC_DIM), BF16), jax.ShapeDtypeStruct((CONV_WIDTH, XBC_DIM), F32),
                   jax.ShapeDtypeStruct((1, XBC_DIM), F32)],
        compiler_params=_params(("parallel",)),
    )(dact_f, dact_b, u_big, conv_w, conv_b)


def _ssd_expand(rev):
    off = SSM_HEADS if rev else 0
    h = jnp.arange(LANES, dtype=jnp.int32)[:, None]
    return (jnp.arange(D_INNER, dtype=jnp.int32)[None, :] // 64 + off == h).astype(F32)


def _ssd_head_terms(dt_ref, bias_ref, alog_ref, acst_s, dtt_s, rev):
    L = CHUNK
    row = lax.broadcasted_iota(jnp.int32, (L, L), 0)
    col = lax.broadcasted_iota(jnp.int32, (L, L), 1)
    mask = (row <= col) if rev else (row >= col)
    cm = mask.astype(F32)
    cmt = ((row >= col) if rev else (row <= col)).astype(F32)
    pre = dt_ref[...] + bias_ref[...]
    dt = _softplus(pre)
    a = -jnp.exp(alog_ref[...])
    da = dt * a
    acs = _dot_h(cm, da)
    acst_s[...] = _dot_h_tn(da, cmt)
    dtt_s[...] = _dot_h_tn(dt, (row == col).astype(F32))
    tot = jnp.sum(da, axis=0, keepdims=True)
    w = jnp.exp(tot - acs)
    return dict(mask=mask, cm=cm, cmt=cmt, ident=(row == col).astype(F32), pre=pre, dt=dt, a=a, da=da, acs=acs,
                tot=tot, e=jnp.exp(acs), w=w, wdt=w * dt, dec=jnp.exp(tot))


def _pair(lo, v, h0):
    return jnp.where(lo, v[:, h0:h0 + 1], v[:, h0 + 1:h0 + 2])


def _ssd_fwd(xbc_act, u_small, bias128, alog128, *, rev, name):
    T = xbc_act.shape[0]
    L = CHUNK
    nc = T // L
    off = SSM_HEADS if rev else 0

    def cidx(c):
        return (nc - 1 - c) if rev else c

    def body(xs_ref, bm_ref, cm_ref, dt_ref, bias_ref, alog_ref, y_ref, hin_ref, ht_s, acst_s, dtt_s, wx_s, dec_s):
        c = pl.program_id(0)

        @pl.when(c == 0)
        def _():
            ht_s[...] = jnp.zeros_like(ht_s)

        t = _ssd_head_terms(dt_ref, bias_ref, alog_ref, acst_s, dtt_s, rev)
        lo = lax.broadcasted_iota(jnp.int32, (L, LANES), 1) < 64
        lo1 = lax.broadcasted_iota(jnp.int32, (1, LANES), 1) < 64
        for g in range(SSM_GROUPS):
            bmat = bm_ref[:, g * LANES:(g + 1) * LANES].astype(BF16)
            cmat = cm_ref[:, g * LANES:(g + 1) * LANES].astype(BF16)
            gmat = _dot_nt(cmat, bmat)
            ht = ht_s[g]
            ch = _dot(cmat, ht.astype(BF16))
            for pr in range(4):
                ps = slice(pr * LANES, (pr + 1) * LANES)
                cs = slice(g * 512 + pr * LANES, g * 512 + (pr + 1) * LANES)
                h0 = off + 8 * g + 2 * pr
                xp = xs_ref[:, cs]
                acc = _pair(lo, t["e"], h0) * ch[:, ps]
                for s_ in range(2):
                    h = h0 + s_
                    seg = t["acs"][:, h:h + 1] - acst_s[h:h + 1, :]
                    lam = jnp.exp(jnp.where(t["mask"], seg, -1e30))
                    m = (gmat * lam * dtt_s[h:h + 1, :]).astype(BF16)
                    xm = jnp.where(lo if s_ == 0 else jnp.logical_not(lo), xp, 0.0).astype(BF16)
                    acc = acc + _dot(m, xm)
                y_ref[:, cs] = acc
                wx_s[:, ps] = (_pair(lo, t["wdt"], h0) * xp).astype(BF16)
                dec_s[0:1, ps] = _pair(lo1, t["dec"], h0)
            hin_ref[0, g] = ht.astype(BF16)
            ht_s[g] = ht * dec_s[0:1, :] + _dot_tn(bmat, wx_s[...])

    return _pallas(
        body, name=name, grid=(nc,),
        in_specs=[pl.BlockSpec((L, D_INNER), lambda c: (cidx(c), 0)), pl.BlockSpec((L, 512), lambda c: (cidx(c), 4)),
                  pl.BlockSpec((L, 512), lambda c: (cidx(c), 5)),
                  pl.BlockSpec((L, LANES), lambda c: (cidx(c), U_DT // LANES)),
                  pl.BlockSpec((1, LANES), lambda c: (0, 0)), pl.BlockSpec((1, LANES), lambda c: (0, 0))],
        out_specs=[pl.BlockSpec((L, D_INNER), lambda c: (cidx(c), 0)),
                   pl.BlockSpec((1, SSM_GROUPS, D_STATE, 512), lambda c: (cidx(c), 0, 0, 0))],
        out_shape=[jax.ShapeDtypeStruct((T, D_INNER), F32), jax.ShapeDtypeStruct((nc, SSM_GROUPS, D_STATE, 512), BF16)],
        scratch_shapes=[pltpu.VMEM((SSM_GROUPS, D_STATE, 512), F32), pltpu.VMEM((LANES, L), F32),
                        pltpu.VMEM((LANES, L), F32), pltpu.VMEM((L, 512), BF16), pltpu.VMEM((8, 512), F32)],
        compiler_params=_params(("arbitrary",)),
    )(xbc_act, xbc_act, xbc_act, u_small, bias128, alog128)


def _ssd_bwd(dy, xbc_act, u_small, bias128, alog128, hin, skip_x, *, rev, name):
    T = xbc_act.shape[0]
    L = CHUNK
    nc = T // L
    off = SSM_HEADS if rev else 0
    has_skip = skip_x is not None

    def cidx(c):
        return c if rev else (nc - 1 - c)

    def body(*refs):
        (dy_ref, xs_ref, bm_ref, cm_ref, dt_ref, bias_ref, alog_ref, hin_ref) = refs[:8]
        k = 8
        skip_ref = refs[k] if has_skip else None
        k += 1 if has_skip else 0
        (dx_ref, draw_ref, dalog_ref, dbias_ref, dht_s, acst_s, dtt_s, rowt_s, ddtt_s, wx_s, edy_s, dec_s) = refs[k:]
        c = pl.program_id(0)

        @pl.when(c == 0)
        def _():
            dht_s[...] = jnp.zeros_like(dht_s)
            rowt_s[...] = jnp.zeros_like(rowt_s)
            ddtt_s[...] = jnp.zeros_like(ddtt_s)

        t = _ssd_head_terms(dt_ref, bias_ref, alog_ref, acst_s, dtt_s, rev)
        lane1 = lax.broadcasted_iota(jnp.int32, (1, LANES), 1)
        lo = lax.broadcasted_iota(jnp.int32, (L, LANES), 1) < 64
        lo1 = lane1 < 64
        colpart = jnp.zeros((L, LANES), F32)
        u_cols = jnp.zeros((L, LANES), F32)
        v_cols = jnp.zeros((L, LANES), F32)
        dtot_h = jnp.zeros((1, LANES), F32)
        for g in range(SSM_GROUPS):
            bmat = bm_ref[:, g * LANES:(g + 1) * LANES].astype(BF16)
            cmat = cm_ref[:, g * LANES:(g + 1) * LANES].astype(BF16)
            gmat = _dot_nt(cmat, bmat)
            ht_in = hin_ref[0, g]
            dht = dht_s[g]
            ht_in_b, dht_b = ht_in.astype(BF16), dht.astype(BF16)
            ch = _dot(cmat, ht_in_b)
            bdh = _dot(bmat, dht_b)
            th = jnp.sum(dht * ht_in, axis=0, keepdims=True)
            dgm = jnp.zeros((L, L), F32)
            for pr in range(4):
                ps = slice(pr * LANES, (pr + 1) * LANES)
                cs = slice(g * 512 + pr * LANES, g * 512 + (pr + 1) * LANES)
                h0 = off + 8 * g + 2 * pr
                xp = xs_ref[:, cs]
                dyp = dy_ref[:, cs]
                dyp_b = dyp.astype(BF16)
                wdt_p = _pair(lo, t["wdt"], h0)
                e_p = _pair(lo, t["e"], h0)
                xb = xp * bdh[:, ps]
                dc = dyp * ch[:, ps]
                dxp = wdt_p * bdh[:, ps]
                for s_ in range(2):
                    h = h0 + s_
                    keep = lo if s_ == 0 else jnp.logical_not(lo)
                    keep1 = lo1 if s_ == 0 else jnp.logical_not(lo1)
                    onehot = (lane1 == h).astype(F32)
                    dtrow = dtt_s[h:h + 1, :]
                    seg = t["acs"][:, h:h + 1] - acst_s[h:h + 1, :]
                    lam = jnp.exp(jnp.where(t["mask"], seg, -1e30))
                    mf0 = gmat * lam
                    m = (mf0 * dtrow).astype(BF16)
                    xm = jnp.where(keep, xp, 0.0).astype(BF16)
                    dm = _dot_nt(dyp_b, xm)
                    r = dm * mf0
                    q = r * dtrow
                    dgm = dgm + dm * lam * dtrow
                    colpart = colpart + jnp.sum(q, axis=1, keepdims=True) * onehot
                    rowt_s[h:h + 1, :] = jnp.sum(q, axis=0, keepdims=True)
                    ddtt_s[h:h + 1, :] = jnp.sum(r, axis=0, keepdims=True)
                    u_cols = u_cols + jnp.sum(jnp.where(keep, xb, 0.0), axis=1, keepdims=True) * onehot
                    v_cols = v_cols + jnp.sum(jnp.where(keep, dc, 0.0), axis=1, keepdims=True) * onehot
                    dtot_h = dtot_h + jnp.sum(jnp.where(keep1, th[:, ps], 0.0), axis=1, keepdims=True) * onehot
                    dxp = dxp + jnp.where(keep, _dot_tn(m, dyp_b), 0.0)
                if has_skip:
                    dxp = dxp + dyp * skip_ref[:, cs]
                dx_ref[:, cs] = dxp
                wx_s[:, ps] = (wdt_p * xp).astype(BF16)
                edy_s[:, ps] = (e_p * dyp).astype(BF16)
                dec_s[0:1, ps] = _pair(lo1, t["dec"], h0)
            edy_b = edy_s[...]
            dgm_b = dgm.astype(BF16)
            dx_ref[:, D_INNER + g * LANES:D_INNER + (g + 1) * LANES] = (
                _dot_nt(wx_s[...], dht_b) + _dot_tn(dgm_b, cmat))
            dx_ref[:, D_INNER + 512 + g * LANES:D_INNER + 512 + (g + 1) * LANES] = (
                _dot_nt(edy_b, ht_in_b) + _dot(dgm_b, bmat))
            dht_s[g] = dec_s[0:1, :] * dht + _dot_tn(cmat, edy_b)

        t_e = v_cols * t["e"]
        t_w = u_cols * t["wdt"]
        colsum_part = _dot_h_tn(rowt_s[...], t["ident"])
        dtot = jnp.sum(t_w, axis=0, keepdims=True) + t["dec"] * dtot_h
        row1 = lax.broadcasted_iota(jnp.int32, (L, LANES), 0)
        last = row1 == (0 if rev else L - 1)
        dacs = colpart - colsum_part + t_e - t_w + jnp.where(last, dtot, 0.0)
        dda = _dot_h(t["cmt"], dacs)
        ddt = dda * t["a"] + u_cols * t["w"] + _dot_h_tn(ddtt_s[...], t["ident"])
        dalog = jnp.sum(dda * t["dt"], axis=0, keepdims=True) * t["a"]
        draw = ddt * _sigmoid(t["pre"])
        draw_ref[...] = draw
        dbias = jnp.sum(draw, axis=0, keepdims=True)

        @pl.when(c == 0)
        def _():
            dalog_ref[...] = dalog
            dbias_ref[...] = dbias

        @pl.when(c > 0)
        def _():
            dalog_ref[...] += dalog
            dbias_ref[...] += dbias

    one = pl.BlockSpec((1, LANES), lambda c: (0, 0))
    in_specs = [pl.BlockSpec((L, D_INNER), lambda c: (cidx(c), 0)), pl.BlockSpec((L, D_INNER), lambda c: (cidx(c), 0)),
                pl.BlockSpec((L, 512), lambda c: (cidx(c), 4)), pl.BlockSpec((L, 512), lambda c: (cidx(c), 5)),
                pl.BlockSpec((L, LANES), lambda c: (cidx(c), U_DT // LANES)), one, one,
                pl.BlockSpec((1, SSM_GROUPS, D_STATE, 512), lambda c: (cidx(c), 0, 0, 0))]
    ins = [dy, xbc_act, xbc_act, xbc_act, u_small, bias128, alog128, hin]
    if has_skip:
        in_specs.append(pl.BlockSpec((1, D_INNER), lambda c: (0, 0)))
        ins.append(skip_x)
    return _pallas(
        body, name=name, grid=(nc,), in_specs=in_specs,
        out_specs=[pl.BlockSpec((L, XBC_DIM), lambda c: (cidx(c), 0)), pl.BlockSpec((L, LANES), lambda c: (cidx(c), 0)),
                   one, one],
        out_shape=[jax.ShapeDtypeStruct((T, XBC_DIM), F32), jax.ShapeDtypeStruct((T, LANES), F32),
                   jax.ShapeDtypeStruct((1, LANES), F32), jax.ShapeDtypeStruct((1, LANES), F32)],
        scratch_shapes=[pltpu.VMEM((SSM_GROUPS, D_STATE, 512), F32), pltpu.VMEM((LANES, L), F32),
                        pltpu.VMEM((LANES, L), F32), pltpu.VMEM((LANES, L), F32), pltpu.VMEM((LANES, L), F32),
                        pltpu.VMEM((L, 512), BF16), pltpu.VMEM((L, 512), BF16), pltpu.VMEM((8, 512), F32)],
        compiler_params=_params(("arbitrary",)),
    )(*ins)


def _ssm_out_fwd(y_f, y_b, xbc_act, u_big, skip_x, ssm_norm, *, name, tile=512):
    rows = y_f.shape[0]

    def fn(yf, yb, xs, z, sk, nw):
        yz = (yf + yb + sk * xs) * (z * _sigmoid(z))
        r = lax.rsqrt(jnp.mean(yz * yz, axis=-1, keepdims=True) + EPS)
        return yz * r * nw

    return _rowmap(fn, name=name, rows=rows, tile=tile, ncol=SSM_GROUPS,
                   ins=[(y_f, 512, _colj), (y_b, 512, _colj), (xbc_act, 512, _colj), (u_big, 512, _colj)],
                   consts=[(skip_x, 512, _colj), (ssm_norm, 512, _colj)], outs=[(D_INNER, BF16, 512, _colj)])[0]


def _ssm_out_bwd(dm, y_f, y_b, xbc_act, u_big, skip_x, ssm_norm, *, name, tile=512):
    rows = y_f.shape[0]

    def fn(dmv, yf, yb, xs, z, sk, nw):
        sg = _sigmoid(z)
        y = yf + yb + sk * xs
        yz = y * (z * sg)
        r = lax.rsqrt(jnp.mean(yz * yz, axis=-1, keepdims=True) + EPS)
        xh = yz * r
        dxh = dmv * nw
        dyz = r * (dxh - xh * jnp.mean(dxh * xh, axis=-1, keepdims=True))
        dy = dyz * (z * sg)
        dz = dyz * y * (sg * (1.0 + z * (1.0 - sg)))
        return dy, dz, jnp.sum(dmv * xh, axis=0, keepdims=True), jnp.sum(dy * xs, axis=0, keepdims=True)

    return _rowmap(fn, name=name, rows=rows, tile=tile, ncol=SSM_GROUPS,
                   ins=[(dm, 512, _colj), (y_f, 512, _colj), (y_b, 512, _colj), (xbc_act, 512, _colj),
                        (u_big, 512, _colj)],
                   consts=[(skip_x, 512, _colj), (ssm_norm, 512, _colj)],
                   outs=[(D_INNER, F32, 512, _colj), (D_INNER, BF16, 512, _colj)],
                   accs=[(D_INNER, 512, _colj), (D_INNER, 512, _colj)])


def _merge_fwd(pa, pb, u_big, *, name, tile=512):
    rows = pa.shape[0]

    def fn(a, b, ga, gb):
        return _sigmoid(ga) * a + _sigmoid(gb) * b

    return _rowmap(fn, name=name, rows=rows, tile=tile,
                   ins=[(pa, 1024, _col0), (pb, 1024, _col0), (u_big, 1024, lambda j: 5), (u_big, 1024, lambda j: 6)],
                   outs=[(1024, BF16, 1024, _col0)])[0]


def _merge_bwd(dmg, pa, pb, u_big, *, name, tile=512):
    rows = pa.shape[0]

    def fn(d, a, b, ga, gb):
        sa, sb = _sigmoid(ga), _sigmoid(gb)
        return d * sa, d * sb, d * a * sa * (1.0 - sa), d * b * sb * (1.0 - sb)

    return _rowmap(fn, name=name, rows=rows, tile=tile,
                   ins=[(dmg, 1024, _col0), (pa, 1024, _col0), (pb, 1024, _col0), (u_big, 1024, lambda j: 5),
                        (u_big, 1024, lambda j: 6)],
                   outs=[(1024, BF16, 1024, _col0)] * 4)


def _loss_bwd(y, target, *, name, tile=512):
    rows, d = y.shape

    def fn(yv, tv):
        err = yv - tv
        part = jnp.sum(jnp.sum(err * err, axis=-1, keepdims=True), axis=0, keepdims=True)
        return err * (1.0 / d), jnp.broadcast_to(part * (0.5 / d), (1, LANES))

    dy, part = _rowmap(fn, name=name, rows=rows, tile=tile, ins=[(y, d, _col0), (target, d, _col0)],
                       outs=[(d, F32, d, _col0)], accs=[(LANES, LANES, _col0)])
    return dy, part[0, 0]


def _small_slab(gs, dskip_ch, dalog_f, dalog_b, dbias_f, dbias_b, gkv, gqh, gkh, dconv_w, *, name):
    e_mat = _ssd_expand(False)
    full_names = ("ffn1_norm", "mix_norm", "q_a_norm", "conv_b", "ssm_norm", "ffn2_norm")
    full = [gs[n] for n in full_names]
    nf = len(full)

    def body(*refs):
        fulls = refs[:nf]
        (dsk_ref, e_ref, af_ref, ab_ref, bf_ref, bb_ref, gkv_ref, gqh_ref, gkh_ref, cw_ref, o_ref) = refs[nf:]
        o_ref[...] = jnp.zeros_like(o_ref)
        for n, r in zip(full_names, fulls):
            o_ref[SMALL_ROW[n]:SMALL_ROW[n] + 1, 0:r.shape[1]] = r[...]
        o_ref[SMALL_ROW["kv_a_norm"]:SMALL_ROW["kv_a_norm"] + 1, 0:KV_LORA] = gkv_ref[...]
        o_ref[SMALL_ROW["q_head_norm"]:SMALL_ROW["q_head_norm"] + 1, 0:LANES] = gqh_ref[...]
        o_ref[SMALL_ROW["k_head_norm"]:SMALL_ROW["k_head_norm"] + 1, 0:LANES] = gkh_ref[...]
        o_ref[SMALL_ROW["a_log_fwd"]:SMALL_ROW["a_log_fwd"] + 1, 0:LANES] = af_ref[...]
        o_ref[SMALL_ROW["a_log_bwd"]:SMALL_ROW["a_log_bwd"] + 1, 0:LANES] = pltpu.roll(ab_ref[...], 96, 1)
        o_ref[SMALL_ROW["dt_bias_fwd"]:SMALL_ROW["dt_bias_fwd"] + 1, 0:LANES] = bf_ref[...]
        o_ref[SMALL_ROW["dt_bias_bwd"]:SMALL_ROW["dt_bias_bwd"] + 1, 0:LANES] = pltpu.roll(bb_ref[...], 96, 1)
        dsk = _dot_h_nt(jnp.broadcast_to(dsk_ref[...], (8, D_INNER)), e_ref[...])
        o_ref[SMALL_ROW["d_skip"]:SMALL_ROW["d_skip"] + 1, 0:LANES] = dsk[0:1, :]
        o_ref[CONV_ROW:CONV_ROW + CONV_WIDTH, :] = cw_ref[...]

    return _pallas(body, name=name, out_shape=jax.ShapeDtypeStruct((SMALL_ROWS, SMALL_COLS), F32))(
        *full, dskip_ch, e_mat, dalog_f, dalog_b, dbias_f, dbias_b, gkv, gqh, gkh, dconv_w)


def _adamw_math(g, w, m, v):
    m2 = ADAM_B1 * m + (1.0 - ADAM_B1) * g
    v2 = ADAM_B2 * v + (1.0 - ADAM_B2) * (g * g)
    m_hat = m2 / (1.0 - ADAM_B1 ** ADAM_STEP)
    v_hat = v2 / (1.0 - ADAM_B2 ** ADAM_STEP)
    delta = -ADAM_LR * (m_hat / (jnp.sqrt(v_hat) + ADAM_EPS) + ADAM_WD * w)
    return delta, m2, v2


def _sum8(r_ref):
    g = r_ref[0].astype(F32)
    for s in range(1, N_DEV):
        g = g + r_ref[s].astype(F32)
    return g


def _reduce_adamw(recv, w, m, v, *, name, tile=256):
    _, R, C = recv.shape
    tile = _pick(R, tile) if R % LANES == 0 else R
    assert R % tile == 0

    def body(r_ref, w_ref, m_ref, v_ref, g_ref, d_ref, m2_ref, v2_ref):
        g = _sum8(r_ref)
        delta, m2, v2 = _adamw_math(g, w_ref[...], m_ref[...], v_ref[...])
        g_ref[...] = g
        d_ref[...] = delta
        m2_ref[...] = m2
        v2_ref[...] = v2

    blk = pl.BlockSpec((tile, C), lambda i: (i, 0))
    return _pallas(
        body, name=name, grid=(R // tile,),
        in_specs=[pl.BlockSpec((N_DEV, tile, C), lambda i: (0, i, 0)), blk, blk, blk], out_specs=[blk] * 4,
        out_shape=[jax.ShapeDtypeStruct((R, C), F32)] * 4, compiler_params=_params(("parallel",)),
    )(recv, w, m, v)


def _reduce_t_adamw(recv, w, m, v, *, name):
    R, cs = w.shape

    def body(r_ref, w_ref, m_ref, v_ref, g_ref, d_ref, m2_ref, v2_ref):
        g = _sum8(r_ref).T
        delta, m2, v2 = _adamw_math(g, w_ref[...], m_ref[...], v_ref[...])
        g_ref[...] = g
        d_ref[...] = delta
        m2_ref[...] = m2
        v2_ref[...] = v2

    return _pallas(body, name=name, out_shape=[jax.ShapeDtypeStruct((R, cs), F32)] * 4,
                   compiler_params=pltpu.CompilerParams(vmem_limit_bytes=VMEM_LIMIT))(recv, w, m, v)


def _reduce8(recv, *, name, tile):
    _, R, C = recv.shape

    def body(r_ref, g_ref):
        g_ref[...] = _sum8(r_ref)

    return _pallas(body, name=name, grid=(R // tile,),
                   in_specs=[pl.BlockSpec((N_DEV, tile, C), lambda i: (0, i, 0))],
                   out_specs=pl.BlockSpec((tile, C), lambda i: (i, 0)),
                   out_shape=jax.ShapeDtypeStruct((R, C), F32), compiler_params=_params(("parallel",)))(recv)


def _adamw(g, w, m, v, *, name, tile=256):
    R, C = w.shape

    def body(g_ref, w_ref, m_ref, v_ref, d_ref, m2_ref, v2_ref):
        delta, m2, v2 = _adamw_math(g_ref[...], w_ref[...], m_ref[...], v_ref[...])
        d_ref[...] = delta
        m2_ref[...] = m2
        v2_ref[...] = v2

    blk = pl.BlockSpec((R, tile), lambda i: (0, i))
    return _pallas(body, name=name, grid=(C // tile,), in_specs=[blk] * 4, out_specs=[blk] * 3,
                   out_shape=[jax.ShapeDtypeStruct((R, C), F32)] * 3, compiler_params=_params(("parallel",)))(g, w, m, v)


def _adamw_small(srecv, conv_g, ws, ms, vs, *, name):
    n = len(ws)

    def body(*refs):
        s_ref, c_ref = refs[0], refs[1]
        w_refs, m_refs, v_refs = refs[2:2 + n], refs[2 + n:2 + 2 * n], refs[2 + 2 * n:2 + 3 * n]
        outs = refs[2 + 3 * n:]
        gsum = _sum8(s_ref)
        for i in range(n):
            if i < len(SMALL):
                g = gsum[i:i + 1, 0:SMALL[i][1]]
            else:
                g = _sum8(c_ref)
            delta, m2, v2 = _adamw_math(g, w_refs[i][...], m_refs[i][...], v_refs[i][...])
            outs[i][...] = g
            outs[n + i][...] = delta
            outs[2 * n + i][...] = m2
            outs[3 * n + i][...] = v2

    shapes = [jax.ShapeDtypeStruct(w.shape, F32) for w in ws]
    got = _pallas(body, name=name, out_shape=shapes * 4,
                  compiler_params=pltpu.CompilerParams(vmem_limit_bytes=VMEM_LIMIT))(srecv, conv_g, *ws, *ms, *vs)
    return got[:n], got[n:2 * n], got[2 * n:3 * n], got[3 * n:]


def _ffn_fwd(x, norm, w_g_t, w_u_t, w_d, tag):
    h = _rms_fwd(x, norm, name=f"{tag}_rms")
    gu = _mm(h, jnp.concatenate([w_g_t, w_u_t], axis=0), name=f"{tag}_gu", tb=True, out_dtype=BF16)
    act = _swiglu_fwd(gu, name=f"{tag}_act")
    out = _mm(act, w_d, name=f"{tag}_down", alpha=0.5, res=x)
    return out, (h, gu, act)


def _ffn_bwd(dout, x, norm, w_g_t, w_u_t, w_d, saved, tag, send=None):
    h, gu, act = saved
    d_act = _mm(dout, w_d, name=f"{tag}_dact", tb=True, alpha=0.5, out_dtype=BF16)
    d_wd = _mm(act, dout, name=f"{tag}_dwd", ta=True, alpha=0.5, tm=1408, tn=1024, out_dtype=BF16)
    tok = send(("down",), [d_wd]) if send else None
    dg, du = _swiglu_bwd(d_act, gu, name=f"{tag}_dswiglu")
    d_wg_t = _mm(dg, h, name=f"{tag}_dwg", ta=True, tm=1408, tn=1024, out_dtype=BF16, after=tok)
    d_wu_t = _mm(du, h, name=f"{tag}_dwu", ta=True, tm=1408, tn=1024, out_dtype=BF16)
    tok = send(("gate", "up"), [d_wg_t, d_wu_t]) if send else None
    dh = _mm(dg, w_g_t, name=f"{tag}_dh_g", after=tok)
    dh = _mm(du, w_u_t, name=f"{tag}_dh_u", res=dh)
    dx, dnorm = _rms_bwd(dh, x, norm, dout, name=f"{tag}_drms")
    return dx, dnorm, d_wg_t, d_wu_t, d_wd


def _rope_tables(positions, T):
    pos = positions.reshape(T).astype(F32)
    inv_freq = 1.0 / (ROPE_BASE ** (jnp.arange(0, QK_ROPE, 2, dtype=F32) / QK_ROPE))
    ang = pos[:, None] * inv_freq
    cos, sin = jnp.cos(ang), jnp.sin(ang)
    one64, z64 = jnp.ones((T, 64), F32), jnp.zeros((T, 64), F32)
    z16, z32, one32 = jnp.zeros((T, 16), F32), jnp.zeros((T, 32), F32), jnp.ones((T, 32), F32)
    c = jnp.concatenate([one64, cos, cos, one32], axis=1)
    s1 = jnp.concatenate([z64, -sin, z16, z32], axis=1)
    s2 = jnp.concatenate([z64, z16, sin, z32], axis=1)
    return c, s1, s2


def _cols(g):
    n, r, cs = g.shape
    return g.transpose(1, 0, 2).reshape(r, n * cs)


def _rows(g):
    n, rs, c = g.shape
    return g.reshape(n * rs, c)


def _pad_lanes(v, n=LANES):
    return jnp.pad(v, ((0, 0), (0, n - v.shape[1])))


def _in_proj_weights(w_in_t):
    z = lambda n: jnp.zeros((n, D_MODEL), w_in_t.dtype)
    w_small_t = jnp.concatenate([w_in_t[0:384], z(128), w_in_t[384:672], z(96), w_in_t[5792:5856], z(64)], axis=0)
    w_big_t = jnp.concatenate([w_in_t[672:5792], w_in_t[5856:7904]], axis=0)
    return w_small_t, w_big_t


def _mla_up_weights(w_q_b_t, w_kv_b):
    wq = w_q_b_t.reshape(N_HEADS, QK_HEAD, Q_LORA)
    wq = jnp.pad(wq, ((0, 0), (0, LANES - QK_HEAD), (0, 0))).reshape(N_HEADS * LANES, Q_LORA)
    wkv = w_kv_b.reshape(KV_LORA, N_HEADS, QK_NOPE + V_HEAD)
    wk = jnp.pad(wkv[..., :QK_NOPE], ((0, 0), (0, 0), (0, LANES - QK_NOPE))).reshape(KV_LORA, N_HEADS * LANES)
    v = wkv[..., QK_NOPE:]
    zv = jnp.zeros_like(v)
    even = (jnp.arange(N_HEADS) % 2 == 0)[None, :, None]
    wv = jnp.where(even, jnp.concatenate([v, zv], -1), jnp.concatenate([zv, v], -1)).reshape(KV_LORA, N_HEADS * LANES)
    return wq, wk, wv


def _shard_rows(g):
    return g.reshape(N_DEV, g.shape[0] // N_DEV, g.shape[1])


def kernel(x, positions, ffn1_norm, ffn1_w_gate, ffn1_w_up, ffn1_w_down, mix_norm, w_in, q_a_norm, w_q_b, kv_a_norm, w_kv_b, q_head_norm, k_head_norm, conv_w, conv_b, a_log_fwd, a_log_bwd, dt_bias_fwd, dt_bias_bwd, d_skip, ssm_norm, w_attn_branch, w_ssm_branch, w_out, ffn2_norm, ffn2_w_gate, ffn2_w_up, ffn2_w_down, loss_target, m_ffn1_norm, m_ffn1_w_gate, m_ffn1_w_up, m_ffn1_w_down, m_mix_norm, m_w_in, m_q_a_norm, m_w_q_b, m_kv_a_norm, m_w_kv_b, m_q_head_norm, m_k_head_norm, m_conv_w, m_conv_b, m_a_log_fwd, m_a_log_bwd, m_dt_bias_fwd, m_dt_bias_bwd, m_d_skip, m_ssm_norm, m_w_attn_branch, m_w_ssm_branch, m_w_out, m_ffn2_norm, m_ffn2_w_gate, m_ffn2_w_up, m_ffn2_w_down, v_ffn1_norm, v_ffn1_w_gate, v_ffn1_w_up, v_ffn1_w_down, v_mix_norm, v_w_in, v_q_a_norm, v_w_q_b, v_kv_a_norm, v_w_kv_b, v_q_head_norm, v_k_head_norm, v_conv_w, v_conv_b, v_a_log_fwd, v_a_log_bwd, v_dt_bias_fwd, v_dt_bias_bwd, v_d_skip, v_ssm_norm, v_w_attn_branch, v_w_ssm_branch, v_w_out, v_ffn2_norm, v_ffn2_w_gate, v_ffn2_w_up, v_ffn2_w_down):
    w_all = dict(ffn1_norm=ffn1_norm, ffn1_w_gate=ffn1_w_gate, ffn1_w_up=ffn1_w_up, ffn1_w_down=ffn1_w_down, mix_norm=mix_norm, w_in=w_in, q_a_norm=q_a_norm, w_q_b=w_q_b, kv_a_norm=kv_a_norm, w_kv_b=w_kv_b, q_head_norm=q_head_norm, k_head_norm=k_head_norm, conv_w=conv_w, conv_b=conv_b, a_log_fwd=a_log_fwd, a_log_bwd=a_log_bwd, dt_bias_fwd=dt_bias_fwd, dt_bias_bwd=dt_bias_bwd, d_skip=d_skip, ssm_norm=ssm_norm, w_attn_branch=w_attn_branch, w_ssm_branch=w_ssm_branch, w_out=w_out, ffn2_norm=ffn2_norm, ffn2_w_gate=ffn2_w_gate, ffn2_w_up=ffn2_w_up, ffn2_w_down=ffn2_w_down)
    m_all = dict(ffn1_norm=m_ffn1_norm, ffn1_w_gate=m_ffn1_w_gate, ffn1_w_up=m_ffn1_w_up, ffn1_w_down=m_ffn1_w_down, mix_norm=m_mix_norm, w_in=m_w_in, q_a_norm=m_q_a_norm, w_q_b=m_w_q_b, kv_a_norm=m_kv_a_norm, w_kv_b=m_w_kv_b, q_head_norm=m_q_head_norm, k_head_norm=m_k_head_norm, conv_w=m_conv_w, conv_b=m_conv_b, a_log_fwd=m_a_log_fwd, a_log_bwd=m_a_log_bwd, dt_bias_fwd=m_dt_bias_fwd, dt_bias_bwd=m_dt_bias_bwd, d_skip=m_d_skip, ssm_norm=m_ssm_norm, w_attn_branch=m_w_attn_branch, w_ssm_branch=m_w_ssm_branch, w_out=m_w_out, ffn2_norm=m_ffn2_norm, ffn2_w_gate=m_ffn2_w_gate, ffn2_w_up=m_ffn2_w_up, ffn2_w_down=m_ffn2_w_down)
    v_all = dict(ffn1_norm=v_ffn1_norm, ffn1_w_gate=v_ffn1_w_gate, ffn1_w_up=v_ffn1_w_up, ffn1_w_down=v_ffn1_w_down, mix_norm=v_mix_norm, w_in=v_w_in, q_a_norm=v_q_a_norm, w_q_b=v_w_q_b, kv_a_norm=v_kv_a_norm, w_kv_b=v_w_kv_b, q_head_norm=v_q_head_norm, k_head_norm=v_k_head_norm, conv_w=v_conv_w, conv_b=v_conv_b, a_log_fwd=v_a_log_fwd, a_log_bwd=v_a_log_bwd, dt_bias_fwd=v_dt_bias_fwd, dt_bias_bwd=v_dt_bias_bwd, d_skip=v_d_skip, ssm_norm=v_ssm_norm, w_attn_branch=v_w_attn_branch, w_ssm_branch=v_w_ssm_branch, w_out=v_w_out, ffn2_norm=v_ffn2_norm, ffn2_w_gate=v_ffn2_w_gate, ffn2_w_up=v_ffn2_w_up, ffn2_w_down=v_ffn2_w_down)
    T = x.shape[1]
    xs_in, target = x[0], loss_target[0]
    def two_d(n, a):
        if n in TRANSPOSED:
            return jnp.swapaxes(a, 1, 2).reshape(a.shape[2], a.shape[1])
        return a.reshape(-1, a.shape[-1])

    w2 = {n: two_d(n, a) for n, a in w_all.items()}
    m2 = {n: two_d(n, a) for n, a in m_all.items()}
    v2 = {n: two_d(n, a) for n, a in v_all.items()}
    p = {n: w2[n] for n, _ in SMALL}
    bf = lambda n: w2[n].astype(BF16)

    first = ["ffn1_w_gate", "ffn1_w_up", "ffn1_w_down"]
    ge = dict(zip(first, _all_gather_two_level([bf(n) for n in first], name="gather_ffn1")))
    mixw = ["w_in", "w_q_b", "w_kv_b", "conv_w"]
    behind_first = ge["ffn1_w_down"][0, 0:1, 0:1].astype(F32) * 0.0
    mix_started, token = _exchange_start(["gather"] * len(mixw),
                                         [bf(n) for n in mixw[:3]] + [w2["conv_w"] + behind_first],
                                         name="gather_mix_start")
    ffn1_norm_f = p["ffn1_norm"] + token[0:1, 0:1]
    w_g1t, w_u1t = _rows(ge["ffn1_w_gate"]), _rows(ge["ffn1_w_up"])
    w_d1 = _rows(ge["ffn1_w_down"])
    late = ["w_attn_branch", "w_ssm_branch", "w_out", "ffn2_w_gate", "ffn2_w_up", "ffn2_w_down"]
    late_shards = [bf(n) for n in late]

    tabs = _rope_tables(positions, T)
    qg, kg = _pad_lanes(p["q_head_norm"]), _pad_lanes(p["k_head_norm"])
    bias128 = _pad_lanes(jnp.concatenate([p["dt_bias_fwd"], p["dt_bias_bwd"]], axis=1))
    alog128 = _pad_lanes(jnp.concatenate([p["a_log_fwd"], p["a_log_bwd"]], axis=1))
    skip_x = jnp.repeat(p["d_skip"], 64, axis=1)

    x1, ffn1_saved = _ffn_fwd(xs_in, ffn1_norm_f, w_g1t, w_u1t, w_d1, "ffn1")
    h2 = _rms_fwd(x1, p["mix_norm"], name="mix_rms")
    ge.update(zip(mixw, _exchange_wait(["gather"] * len(mixw), mix_started, h2, name="gather_mix_wait")))
    w_small_t, w_big_t = _in_proj_weights(_rows(ge["w_in"]))
    wq_t, wk, wv = _mla_up_weights(_rows(ge["w_q_b"]), _cols(ge["w_kv_b"]))
    conv_full = _cols(ge["conv_w"])
    u_big = _mm(h2, w_big_t, name="in_big", tb=True)
    u_small = _mm(h2, w_small_t, name="in_small", tb=True)
    cqn, ckvn = _lora_norm_fwd(u_small, p["q_a_norm"], p["kv_a_norm"], name="lora_norm")
    q_raw = _mm(cqn, wq_t, name="q_up", tb=True)
    k_raw = _mm(ckvn, wk, name="k_up")
    v = _mm(ckvn, wv, name="v_up", out_dtype=BF16)
    q, k = _qk_prep_fwd(q_raw, k_raw, u_small, tabs, qg, kg, name="qk_prep")
    a_out, lse, g_late = _attn_fwd(q, k, v, ["gather"] * len(late), late_shards, name="attn_fwd")
    gl = dict(zip(late, g_late))
    w_pa, w_pb, w_o = _rows(gl["w_attn_branch"]), _rows(gl["w_ssm_branch"]), _rows(gl["w_out"])
    w_g2t, w_u2t = _rows(gl["ffn2_w_gate"]), _rows(gl["ffn2_w_up"])
    w_d2 = _rows(gl["ffn2_w_down"])
    xbc_act = _conv_fwd(u_big, conv_full, p["conv_b"], name="conv_fwd")
    y_f, hin_f = _ssd_fwd(xbc_act, u_small, bias128, alog128, rev=False, name="ssd_fwd_f")
    y_b, hin_b = _ssd_fwd(xbc_act, u_small, bias128, alog128, rev=True, name="ssd_fwd_b")
    m_out = _ssm_out_fwd(y_f, y_b, xbc_act, u_big, skip_x, p["ssm_norm"], name="ssm_out")
    pa = _mm(a_out, w_pa, name="branch_a")
    pb = _mm(m_out, w_pb, name="branch_b")
    merged = _merge_fwd(pa, pb, u_big, name="merge")
    x2 = _mm(merged, w_o, name="mix_out", res=x1)
    y, ffn2_saved = _ffn_fwd(x2, p["ffn2_norm"], w_g2t, w_u2t, w_d2, "ffn2")
    dy, loss_part = _loss_bwd(y, target, name="loss")
    loss = lax.psum(loss_part, ("x", "y", "c"))

    gs = {}
    dx2, gs["ffn2_norm"], g_gate2, g_up2, g_down2 = _ffn_bwd(dy, x2, p["ffn2_norm"], w_g2t, w_u2t, w_d2, ffn2_saved,
                                                             "ffn2b")
    dmerged = _mm(dx2, w_o, name="d_merged", tb=True)
    g_out = _mm(merged, dx2, name="d_w_out", ta=True, out_dtype=BF16)
    dpa, dpb, dga, dgb = _merge_bwd(dmerged, pa, pb, u_big, name="d_merge")
    g_pa = _mm(a_out, dpa, name="d_w_pa", ta=True, out_dtype=BF16)
    g_pb = _mm(m_out, dpb, name="d_w_pb", ta=True, out_dtype=BF16)
    da_out = _mm(dpa, w_pa, name="d_a", tb=True)
    dm_out = _mm(dpb, w_pb, name="d_m", tb=True)
    late_grads = [_shard_rows(g) for g in (g_pa, g_pb, g_out, g_gate2, g_up2, g_down2)]
    dq, dk, dv, r_late = _attn_bwd(q, k, v, a_out, lse, da_out, ["scatter"] * len(late_grads), late_grads,
                                   name="attn_bwd")
    recv = dict(zip(late, r_late))

    dyss, dz, gs["ssm_norm"], dskip_ch = _ssm_out_bwd(dm_out, y_f, y_b, xbc_act, u_big, skip_x, p["ssm_norm"],
                                                      name="d_ssm_out")
    dact_f, draw_f, dalog_f, dbias_f = _ssd_bwd(dyss, xbc_act, u_small, bias128, alog128, hin_f, skip_x,
                                                rev=False, name="ssd_bwd_f")
    dact_b, draw_b, dalog_b, dbias_b = _ssd_bwd(dyss, xbc_act, u_small, bias128, alog128, hin_b, None,
                                                rev=True, name="ssd_bwd_b")
    dxbc, g_conv, gs["conv_b"] = _conv_bwd(dact_f, dact_b, u_big, conv_full, p["conv_b"], name="conv_bwd")

    dq_raw, dk_raw, dkpe, gqh, gkh = _qk_prep_bwd(dq, dk, q_raw, k_raw, u_small, tabs, qg, kg, name="d_qk_prep")
    g_wq_t = _mm(dq_raw, cqn, name="d_w_q", ta=True, out_dtype=BF16)
    g_wk_t = _mm(dk_raw, ckvn, name="d_w_k", ta=True, out_dtype=BF16)
    g_wv_t = _mm(dv, ckvn, name="d_w_v", ta=True, out_dtype=BF16)
    dcqn = _mm(dq_raw, wq_t, name="d_cqn")
    dckvn = _mm(dk_raw, wk, name="d_ckvn_k", tb=True)
    dckvn = _mm(dv, wv, name="d_ckvn_v", tb=True, res=dckvn)
    du_small, gs["q_a_norm"], gkv = _lora_norm_bwd(dcqn, dckvn, u_small, p["q_a_norm"], p["kv_a_norm"], dkpe,
                                                   draw_f, draw_b, name="d_lora_norm")

    dh2 = _mm(du_small, w_small_t, name="d_h2_small")
    dh2 = _mm(dz, w_big_t, name="d_h2_z", b_row0=0, res=dh2)
    dh2 = _mm(dxbc, w_big_t, name="d_h2_xbc", b_row0=2048, res=dh2)
    dh2 = _mm(dga, w_big_t, name="d_h2_ga", b_row0=5120, res=dh2)
    dh2 = _mm(dgb, w_big_t, name="d_h2_gb", b_row0=6144, res=dh2)
    gt_small = _mm(du_small, h2, name="d_w_small", ta=True, out_dtype=BF16)
    gt_z = _mm(dz, h2, name="d_w_z", ta=True, out_dtype=BF16)
    gt_xbc = _mm(dxbc, h2, name="d_w_xbc", ta=True, out_dtype=BF16)
    gt_ga = _mm(dga, h2, name="d_w_ga", ta=True, out_dtype=BF16)
    gt_gb = _mm(dgb, h2, name="d_w_gb", ta=True, out_dtype=BF16)
    dx1, gs["mix_norm"] = _rms_bwd(dh2, x1, p["mix_norm"], dx2, name="d_mix_rms")

    gt_in = jnp.concatenate([gt_small[0:384], gt_small[U_CKV:U_KPE + QK_ROPE], gt_z, gt_xbc,
                             gt_small[U_DT:U_DT + 64], gt_ga, gt_gb], axis=0)
    gt_in = jnp.pad(gt_in.reshape(N_DEV, W_IN_SHARD, D_MODEL), ((0, 0), (0, W_IN_SHARD_PAD - W_IN_SHARD), (0, 0)))
    gt_q = g_wq_t.reshape(N_HEADS, LANES, Q_LORA)[:, :QK_HEAD].reshape(N_DEV, -1, Q_LORA)
    gk3 = g_wk_t.reshape(N_HEADS, LANES, KV_LORA)[:, :QK_NOPE]
    gv3 = g_wv_t.reshape(N_HEADS, LANES, KV_LORA)
    even = (jnp.arange(N_HEADS) % 2 == 0)[:, None, None]
    gv3 = jnp.where(even, gv3[:, :V_HEAD], gv3[:, V_HEAD:])
    gt_kv = jnp.concatenate([gk3, gv3], axis=1).reshape(N_DEV, -1, KV_LORA)
    mixg = ["w_in", "w_q_b", "w_kv_b"]
    grads_started, token = _exchange_start(["scatter"] * len(mixg), [gt_in, gt_q, gt_kv], name="grad_mix_start")
    ffn1_sent = []

    def send(names, grads):
        st, tok = _exchange_start(["scatter"] * len(grads), [_shard_rows(g) for g in grads],
                                  name="grad_ffn1_" + "_".join(names) + "_start")
        ffn1_sent.append((names, st))
        return tok

    grad_x, gs["ffn1_norm"], _, _, _ = _ffn_bwd(dx1, xs_in, p["ffn1_norm"] + token[0:1, 0:1], w_g1t, w_u1t, w_d1,
                                                ffn1_saved, "ffn1b", send=send)
    recv.update(zip(mixg, _exchange_wait(["scatter"] * len(mixg), grads_started, grad_x, name="grad_mix_wait")))
    for names, st in ffn1_sent:
        got = _exchange_wait(["scatter"] * len(names), st, grad_x, name="grad_ffn1_" + "_".join(names) + "_wait")
        recv.update(zip(["ffn1_w_" + n for n in names], got))

    gsmall = _small_slab(gs, dskip_ch, dalog_f, dalog_b, dbias_f, dbias_b, gkv, gqh, gkh, g_conv, name="small_slab")
    srecv = _exchange(["gather"], [gsmall], name="grad_exchange")[0]

    out = {}
    for n in ("ffn1_w_down", "ffn2_w_down", "w_attn_branch", "w_ssm_branch", "w_out", "ffn1_w_gate", "ffn1_w_up",
              "ffn2_w_gate", "ffn2_w_up", "w_q_b"):
        out[n] = _reduce_adamw(recv[n], w2[n], m2[n], v2[n], name=f"adamw_{n}")
    out["w_kv_b"] = _reduce_t_adamw(recv["w_kv_b"], w2["w_kv_b"], m2["w_kv_b"], v2["w_kv_b"], name="adamw_w_kv_b")
    g_in = _reduce8(recv["w_in"], name="sum_w_in", tile=W_IN_SHARD_PAD // 2)[:W_IN_SHARD]
    out["w_in"] = [g_in] + list(_adamw(g_in, w2["w_in"], m2["w_in"], v2["w_in"], name="adamw_w_in"))
    me = 4 * lax.axis_index("x") + 2 * lax.axis_index("y") + lax.axis_index("c")
    conv_g = lax.dynamic_slice(srecv, (0, CONV_ROW, me * (XBC_DIM // N_DEV)), (N_DEV, CONV_WIDTH, XBC_DIM // N_DEV))
    sn = [n for n, _ in SMALL] + ["conv_w"]
    sg, sd, sm, sv = _adamw_small(srecv, conv_g, [w2[n] for n in sn], [m2[n] for n in sn], [v2[n] for n in sn],
                                  name="adamw_small")
    for i, n in enumerate(sn):
        out[n] = (sg[i], sd[i], sm[i], sv[i])
    def back(n, a):
        if n in TRANSPOSED:
            return jnp.swapaxes(a.reshape(1, a.shape[0], a.shape[1]), 1, 2)
        return a.reshape(w_all[n].shape)

    outs = [[back(n, out[n][kind]) for n in WEIGHT_ORDER] for kind in range(4)]
    return (loss, grad_x[None], *outs[0], *outs[1], *outs[2], *outs[3])
```

```python
import math

import jax
import jax.numpy as jnp
from jax import lax
from jax.experimental import pallas as pl
from jax.experimental.pallas import tpu as pltpu

F32, BF16 = jnp.float32, jnp.bfloat16
HIGHEST = lax.Precision.HIGHEST

D_MODEL, D_FF = 1024, 2816
EPS = 1e-6
N_HEADS, QK_NOPE, QK_ROPE, QK_HEAD, V_HEAD = 16, 64, 32, 96, 64
Q_LORA, KV_LORA = 384, 256
ROPE_BASE = 10000.0
D_INNER, SSM_HEADS, SSM_GROUPS, D_STATE, CONV_WIDTH, CHUNK = 2048, 32, 4, 128, 5, 128
XBC_DIM = D_INNER + 2 * SSM_GROUPS * D_STATE
IN_DIM = 7904
ADAM_LR, ADAM_B1, ADAM_B2, ADAM_EPS, ADAM_WD, ADAM_STEP = 0.001, 0.9, 0.999, 1e-08, 0.01, 10
N_DEV = 8

V7X_VMEM_BYTES = 64 * 1024 * 1024
VMEM_LIMIT = V7X_VMEM_BYTES - 8 * 1024 * 1024
LANES = 128
W_IN_SHARD = IN_DIM // N_DEV
W_IN_SHARD_PAD = 992

SMALL = (
    ("ffn1_norm", 1024), ("mix_norm", 1024), ("q_a_norm", 384), ("kv_a_norm", 256), ("q_head_norm", 96),
    ("k_head_norm", 96), ("conv_b", 3072), ("a_log_fwd", 32), ("a_log_bwd", 32), ("dt_bias_fwd", 32),
    ("dt_bias_bwd", 32), ("d_skip", 32), ("ssm_norm", 2048), ("ffn2_norm", 1024),
)
TRANSPOSED = ("ffn1_w_gate", "ffn1_w_up", "ffn2_w_gate", "ffn2_w_up", "w_in", "w_q_b")
SMALL_ROW = {n: i for i, (n, _) in enumerate(SMALL)}
CONV_ROW = len(SMALL)
SMALL_ROWS, SMALL_COLS = 24, XBC_DIM
WEIGHT_ORDER = (
    "ffn1_norm", "ffn1_w_gate", "ffn1_w_up", "ffn1_w_down", "mix_norm", "w_in", "q_a_norm", "w_q_b", "kv_a_norm",
    "w_kv_b", "q_head_norm", "k_head_norm", "conv_w", "conv_b", "a_log_fwd", "a_log_bwd", "dt_bias_fwd", "dt_bias_bwd",
    "d_skip", "ssm_norm", "w_attn_branch", "w_ssm_branch", "w_out", "ffn2_norm", "ffn2_w_gate", "ffn2_w_up",
    "ffn2_w_down",
)


def _pallas(body, **kw):
    return pl.pallas_call(body, **kw)


def _params(sem):
    return pltpu.CompilerParams(dimension_semantics=sem, vmem_limit_bytes=VMEM_LIMIT)


def _pick(dim, pref):
    if dim <= pref:
        return dim
    c = (pref // LANES) * LANES
    while c >= LANES:
        if dim % c == 0:
            return c
        c -= LANES
    raise ValueError((dim, pref))


def _sigmoid(x):
    return 1.0 / (1.0 + jnp.exp(-x))


def _softplus(x):
    return jnp.maximum(x, 0.0) + jnp.log(1.0 + jnp.exp(-jnp.abs(x)))


def _dot(a, b):
    return jnp.dot(a, b, preferred_element_type=F32)


def _dot_nt(a, b):
    return lax.dot_general(a, b, (((1,), (1,)), ((), ())), preferred_element_type=F32)


def _dot_tn(a, b):
    return lax.dot_general(a, b, (((0,), (0,)), ((), ())), preferred_element_type=F32)


def _dot_h(a, b):
    return jnp.dot(a, b, preferred_element_type=F32, precision=HIGHEST)


def _dot_h_nt(a, b):
    return lax.dot_general(a, b, (((1,), (1,)), ((), ())), preferred_element_type=F32, precision=HIGHEST)


def _dot_h_tn(a, b):
    return lax.dot_general(a, b, (((0,), (0,)), ((), ())), preferred_element_type=F32, precision=HIGHEST)


def _mesh_pos():
    return lax.axis_index("x"), lax.axis_index("y"), lax.axis_index("c")


def _comm_scratch(n):
    return [pltpu.SemaphoreType.DMA((7 * n,)), pltpu.SemaphoreType.DMA((7 * n,)), pltpu.SemaphoreType.DMA((n,))]


def _comm_copies(modes, srcs, dsts, send_sems, recv_sems, local_sems, arrivals):
    x, y, c = _mesh_pos()
    me = 4 * x + 2 * y + c
    local, remote = [], []
    for w, (mode, s, d) in enumerate(zip(modes, srcs, dsts)):
        gather = mode == "gather"
        if not arrivals and local_sems is not None:
            local.append(pltpu.make_async_copy(s if gather else s.at[me], d.at[me], local_sems.at[w]))
        for k in range(1, N_DEV):
            px = (1 - x) if (k & 4) else x
            py = (1 - y) if (k & 2) else y
            pc = (1 - c) if (k & 1) else c
            peer = 4 * px + 2 * py + pc
            idx = 7 * w + k - 1
            remote.append(pltpu.make_async_remote_copy(
                src_ref=s if gather else s.at[peer], dst_ref=d.at[peer] if arrivals else d.at[me],
                send_sem=send_sems.at[idx], recv_sem=recv_sems.at[idx],
                device_id=(px, py, pc), device_id_type=pl.DeviceIdType.MESH))
    return local, remote


def _comm_start(modes, srcs, dsts, sems):
    local, sends = _comm_copies(modes, srcs, dsts, *sems, arrivals=False)
    for cp in local + sends:
        cp.start()


def _comm_wait(modes, srcs, dsts, sems):
    _, recvs = _comm_copies(modes, srcs, dsts, *sems, arrivals=True)
    for cp in recvs:
        cp.wait_recv()
    local, sends = _comm_copies(modes, srcs, dsts, *sems, arrivals=False)
    for cp in sends:
        cp.wait_send()
    for cp in local:
        cp.wait()


def _comm_out_shapes(modes, arrays):
    return [jax.ShapeDtypeStruct((N_DEV,) + (a.shape if m == "gather" else a.shape[1:]), a.dtype)
            for m, a in zip(modes, arrays)]


def _exchange(modes, arrays, *, name):
    n = len(arrays)

    def body(*refs):
        srcs, dsts, sems = refs[:n], refs[n:2 * n], refs[2 * n:]
        _comm_start(modes, srcs, dsts, sems)
        _comm_wait(modes, srcs, dsts, sems)

    any_spec = pl.BlockSpec(memory_space=pl.ANY)
    return _pallas(body, name=name, out_shape=_comm_out_shapes(modes, arrays), in_specs=[any_spec] * n,
                   out_specs=[any_spec] * n, scratch_shapes=_comm_scratch(n))(*arrays)


def _exchange_start(modes, arrays, *, name):
    n = len(arrays)
    me = 4 * lax.axis_index("x") + 2 * lax.axis_index("y") + lax.axis_index("c")
    lands = []
    for m, a in zip(modes, arrays):
        own = a if m == "gather" else lax.dynamic_index_in_dim(a, me, 0, keepdims=False)
        zone = lax.empty((N_DEV,) + own.shape, a.dtype)
        lands.append(lax.dynamic_update_index_in_dim(zone, own, me, 0))

    def body(*refs):
        srcs, dsts = refs[:n], refs[n:2 * n]
        send_sems, recv_sems = refs[2 * n], refs[2 * n + 1]
        token = refs[-1]
        _, sends = _comm_copies(modes, srcs, dsts, send_sems, recv_sems, None, arrivals=False)
        for cp in sends:
            cp.start()
        token[...] = jnp.zeros_like(token)

    hbm = pl.BlockSpec(memory_space=pltpu.HBM)
    sem = pl.BlockSpec(memory_space=pltpu.SEMAPHORE)
    ins = [pltpu.with_memory_space_constraint(a, pltpu.HBM) for a in list(arrays) + lands]
    got = _pallas(
        body, name=name,
        out_shape=(pltpu.SemaphoreType.DMA((7 * n,)), pltpu.SemaphoreType.DMA((7 * n,)),
                   *[pltpu.HBM(a.shape, a.dtype) for a in ins], jax.ShapeDtypeStruct((8, LANES), F32)),
        in_specs=[hbm] * (2 * n), out_specs=(sem, sem, *[hbm] * (2 * n), pl.BlockSpec(memory_space=pltpu.VMEM)),
        input_output_aliases={i: 2 + i for i in range(2 * n)},
        compiler_params=pltpu.CompilerParams(has_side_effects=pltpu.SideEffectType.DATAFLOW_SIDE_EFFECTING),
    )(*ins)
    return (got[0], got[1], got[2:2 + n], got[2 + n:2 + 2 * n]), got[-1]


def _exchange_wait(modes, started, after, *, name):
    send_sems, recv_sems, srcs, lands = started
    n = len(srcs)

    def body(*refs):
        src_refs, dst_refs = refs[:n], refs[n:2 * n]
        ssem, rsem = refs[2 * n], refs[2 * n + 1]
        _, recvs = _comm_copies(modes, src_refs, dst_refs, ssem, rsem, None, arrivals=True)
        for cp in recvs:
            cp.wait_recv()
        _, sends = _comm_copies(modes, src_refs, dst_refs, ssem, rsem, None, arrivals=False)
        for cp in sends:
            cp.wait_send()

    hbm = pl.BlockSpec(memory_space=pltpu.HBM)
    sem = pl.BlockSpec(memory_space=pltpu.SEMAPHORE)
    both = list(srcs) + list(lands)
    got = _pallas(
        body, name=name, out_shape=tuple(pltpu.HBM(a.shape, a.dtype) for a in both),
        in_specs=[hbm] * (2 * n) + [sem, sem, pl.BlockSpec(memory_space=pl.ANY)], out_specs=tuple([hbm] * (2 * n)),
        input_output_aliases={i: i for i in range(2 * n)},
        compiler_params=pltpu.CompilerParams(has_side_effects=pltpu.SideEffectType.DATAFLOW_SIDE_EFFECTING),
    )(*both, send_sems, recv_sems, after)
    return got[n:]


def _all_gather_two_level(shards, *, name):
    n = len(shards)

    def body(*refs):
        srcs, outs = refs[:n], refs[n:2 * n]
        send_sems, recv_sems, local_sems = refs[2 * n:]
        x, y, c = _mesh_pos()
        me, sibling = (x, y, c), (x, y, 1 - c)
        chips = [(1 - x, y), (x, 1 - y), (1 - x, 1 - y)]

        def blk(w, px, py, pc):
            return outs[w].at[4 * px + 2 * py + pc]

        def copy(w, k, block, to, src=None):
            return pltpu.make_async_remote_copy(
                src_ref=blk(w, *block) if src is None else src, dst_ref=blk(w, *block),
                send_sem=send_sems.at[7 * w + k], recv_sem=recv_sems.at[7 * w + k], device_id=to,
                device_id_type=pl.DeviceIdType.MESH)

        mine = [pltpu.make_async_copy(srcs[w], blk(w, *me), local_sems.at[w]) for w in range(n)]
        for cp in mine:
            cp.start()
        first = []
        for w in range(n):
            first.append(copy(w, 0, me, sibling, src=srcs[w]))
            first += [copy(w, 1 + j, me, (*chip, c), src=srcs[w]) for j, chip in enumerate(chips)]
        for cp in first:
            cp.start()
        passed = []
        for w in range(n):
            for j, chip in enumerate(chips):
                copy(w, 1 + j, (*chip, c), me).wait_recv()
                fwd = copy(w, 4 + j, (*chip, c), sibling)
                fwd.start()
                passed.append(fwd)
        for w in range(n):
            copy(w, 0, sibling, me).wait_recv()
            for j, chip in enumerate(chips):
                copy(w, 4 + j, (*chip, 1 - c), me).wait_recv()
        for cp in first + passed:
            cp.wait_send()
        for cp in mine:
            cp.wait()

    any_spec = pl.BlockSpec(memory_space=pl.ANY)
    return _pallas(body, name=name, out_shape=_comm_out_shapes(["gather"] * n, shards), in_specs=[any_spec] * n,
                   out_specs=[any_spec] * n, scratch_shapes=_comm_scratch(n))(*shards)


def _mm(a, b, *, name, ta=False, tb=False, out_dtype=F32, alpha=1.0, res=None, tm=1024, tn=1408, tk=1408,
        b_row0=None, after=None):
    (K, M) = a.shape if ta else a.shape[::-1]
    (N, Kb) = b.shape if tb else b.shape[::-1]
    tm, tn, tk = _pick(M, tm), _pick(N, tn), _pick(K, tk)
    nk = K // tk
    if b_row0 is None:
        assert K == Kb, (a.shape, b.shape, ta, tb)
        kb0 = 0
    else:
        assert not tb and b_row0 % tk == 0 and b_row0 + K <= Kb, (a.shape, b.shape, b_row0)
        kb0 = b_row0 // tk
    a_spec = pl.BlockSpec((tk, tm), lambda i, j, k: (k, i)) if ta else pl.BlockSpec((tm, tk), lambda i, j, k: (i, k))
    b_spec = (pl.BlockSpec((tn, tk), lambda i, j, k: (j, k)) if tb
              else pl.BlockSpec((tk, tn), lambda i, j, k: (k + kb0, j)))
    o_spec = pl.BlockSpec((tm, tn), lambda i, j, k: (i, j))
    dn = (((0 if ta else 1,), (1 if tb else 0,)), ((), ()))
    has_res = res is not None
    n_in = 2 + has_res + (after is not None)

    def body(*refs):
        a_ref, b_ref = refs[0], refs[1]
        r_ref = refs[2] if has_res else None
        o_ref = refs[n_in]
        part =lax.dot_general(a_ref[...].astype(BF16), b_ref[...].astype(BF16), dn, preferred_element_type=F32)

        def finish(acc):
            if alpha != 1.0:
                acc = acc * alpha
            if has_res:
                acc = acc + r_ref[...]
            o_ref[...] = acc.astype(o_ref.dtype)

        if nk == 1:
            finish(part)
        else:
            acc_ref = refs[-1]
            k = pl.program_id(2)

            @pl.when(k == 0)
            def _():
                acc_ref[...] = part

            @pl.when(k > 0)
            def _():
                acc_ref[...] += part

            @pl.when(k == nk - 1)
            def _():
                finish(acc_ref[...])

    ins = [a, b] + ([res] if has_res else [])
    in_specs = [a_spec, b_spec] + ([o_spec] if has_res else [])
    if after is not None:
        ins.append(after)
        in_specs.append(pl.BlockSpec(after.shape, lambda i, j, k: (0, 0)))
    return _pallas(
        body, name=name, grid=(M // tm, N // tn, nk), in_specs=in_specs, out_specs=o_spec,
        out_shape=jax.ShapeDtypeStruct((M, N), out_dtype),
        scratch_shapes=[pltpu.VMEM((tm, tn), F32)] if nk > 1 else [],
        compiler_params=_params(("parallel", "parallel", "arbitrary")),
    )(*ins)


def _col0(j):
    return 0


def _colj(j):
    return j


def _rowmap(fn, *, name, rows, tile, ins, consts=(), outs=(), accs=(), ncol=1):
    tile = min(tile, rows)
    nrow = rows // tile
    in_specs = [pl.BlockSpec((tile, w), lambda j, i, f=f: (i, f(j))) for _, w, f in ins]
    for arr, w, f in consts:
        in_specs.append(pl.BlockSpec((arr.shape[0], w), lambda j, i, f=f: (0, f(j))))
    out_specs = [pl.BlockSpec((tile, w), lambda j, i, f=f: (i, f(j))) for _, _, w, f in outs]
    out_specs += [pl.BlockSpec((1, w), lambda j, i, f=f: (0, f(j))) for _, w, f in accs]
    out_shape = [jax.ShapeDtypeStruct((rows, c), dt) for c, dt, _, _ in outs]
    out_shape += [jax.ShapeDtypeStruct((1, c), F32) for c, _, _ in accs]
    n_in, n_out = len(ins) + len(consts), len(outs)
    acc_fixed = [f is _col0 for _, _, f in accs]

    def body(*refs):
        res = fn(*[r[...].astype(F32) for r in refs[:n_in]])
        if not isinstance(res, (tuple, list)):
            res = (res,)
        for r, v in zip(refs[n_in:n_in + n_out], res[:n_out]):
            r[...] = v.astype(r.dtype)
        j, i = pl.program_id(0), pl.program_id(1)
        for r, v, fixed in zip(refs[n_in + n_out:], res[n_out:], acc_fixed):
            first = ((i == 0) & (j == 0)) if fixed else (i == 0)

            @pl.when(first)
            def _(r=r, v=v):
                r[...] = v

            @pl.when(jnp.logical_not(first))
            def _(r=r, v=v):
                r[...] += v

    arrays = [a for a, _, _ in ins] + [a for a, _, _ in consts]
    return _pallas(
        body, name=name, grid=(ncol, nrow), in_specs=in_specs, out_specs=out_specs, out_shape=out_shape,
        compiler_params=_params(("arbitrary", "arbitrary")),
    )(*arrays)


def _rms_fwd(x, g, *, name, tile=512):
    rows, d = x.shape

    def fn(xv, gv):
        r = lax.rsqrt(jnp.mean(xv * xv, axis=-1, keepdims=True) + EPS)
        return xv * r * gv

    return _rowmap(fn, name=name, rows=rows, tile=tile, ins=[(x, d, _col0)], consts=[(g, d, _col0)],
                   outs=[(d, BF16, d, _col0)])[0]


def _rms_bwd(dh, x, g, res, *, name, tile=512):
    rows, d = x.shape

    def fn(dhv, xv, rv, gv):
        r = lax.rsqrt(jnp.mean(xv * xv, axis=-1, keepdims=True) + EPS)
        xh = xv * r
        dxh = dhv * gv
        dx = r * (dxh - xh * jnp.mean(dxh * xh, axis=-1, keepdims=True))
        return rv + dx, jnp.sum(dhv * xh, axis=0, keepdims=True)

    return _rowmap(fn, name=name, rows=rows, tile=tile, ins=[(dh, d, _col0), (x, d, _col0), (res, d, _col0)],
                   consts=[(g, d, _col0)], outs=[(d, F32, d, _col0)], accs=[(d, d, _col0)])


def _swiglu_fwd(gu, *, name, tile=512):
    rows = gu.shape[0]
    w = _pick(D_FF, 1408)
    nb = D_FF // w

    def fn(gv, uv):
        return gv * _sigmoid(gv) * uv

    return _rowmap(fn, name=name, rows=rows, tile=tile, ncol=nb,
                   ins=[(gu, w, _colj), (gu, w, lambda j: j + nb)], outs=[(D_FF, BF16, w, _colj)])[0]


def _swiglu_bwd(da, gu, *, name, tile=512):
    rows = gu.shape[0]
    w = _pick(D_FF, 1408)
    nb = D_FF // w

    def fn(dav, gv, uv):
        sg = _sigmoid(gv)
        dg = dav * uv * (sg * (1.0 + gv * (1.0 - sg)))
        du = dav * (gv * sg)
        return dg, du

    return _rowmap(fn, name=name, rows=rows, tile=tile, ncol=nb,
                   ins=[(da, w, _colj), (gu, w, _colj), (gu, w, lambda j: j + nb)],
                   outs=[(D_FF, BF16, w, _colj), (D_FF, BF16, w, _colj)])


U_CKV, U_KPE, U_DT = 512, 768, 896


def _lora_norm_fwd(u_small, qg, kvg, *, name, tile=512):
    rows = u_small.shape[0]

    def fn(cq, ckv, qgv, kgv):
        rq = lax.rsqrt(jnp.mean(cq * cq, axis=-1, keepdims=True) + EPS)
        rk = lax.rsqrt(jnp.mean(ckv * ckv, axis=-1, keepdims=True) + EPS)
        return cq * rq * qgv, ckv * rk * kgv

    return _rowmap(fn, name=name, rows=rows, tile=tile,
                   ins=[(u_small, Q_LORA, _col0), (u_small, KV_LORA, lambda j: U_CKV // KV_LORA)],
                   consts=[(qg, Q_LORA, _col0), (kvg, KV_LORA, _col0)],
                   outs=[(Q_LORA, BF16, Q_LORA, _col0), (KV_LORA, BF16, KV_LORA, _col0)])


def _lora_norm_bwd(dcqn, dckvn, u_small, qg, kvg, dkpe, draw_f, draw_b, *, name, tile=512):
    rows = u_small.shape[0]
    tile = min(tile, rows)

    def body(dq_ref, dk_ref, u_ref, dkp_ref, df_ref, db_ref, qg_ref, kg_ref, du_ref, gq_ref, gk_ref):
        cq, ckv = u_ref[:, 0:Q_LORA], u_ref[:, U_CKV:U_CKV + KV_LORA]
        dq, dk = dq_ref[...], dk_ref[...]
        rq = lax.rsqrt(jnp.mean(cq * cq, axis=-1, keepdims=True) + EPS)
        xh = cq * rq
        dxh = dq * qg_ref[...]
        du_ref[:, 0:Q_LORA] = (rq * (dxh - xh * jnp.mean(dxh * xh, axis=-1, keepdims=True))).astype(BF16)
        du_ref[:, Q_LORA:U_CKV] = jnp.zeros((tile, U_CKV - Q_LORA), BF16)
        rk = lax.rsqrt(jnp.mean(ckv * ckv, axis=-1, keepdims=True) + EPS)
        kh = ckv * rk
        dkh = dk * kg_ref[...]
        du_ref[:, U_CKV:U_KPE] = (rk * (dkh - kh * jnp.mean(dkh * kh, axis=-1, keepdims=True))).astype(BF16)
        du_ref[:, U_KPE:U_DT] = dkp_ref[...].astype(BF16)
        du_ref[:, U_DT:U_DT + LANES] = (df_ref[...] + db_ref[...]).astype(BF16)
        gq = jnp.sum(dq * xh, axis=0, keepdims=True)
        gk = jnp.sum(dk * kh, axis=0, keepdims=True)
        i = pl.program_id(0)

        @pl.when(i == 0)
        def _():
            gq_ref[...] = gq
            gk_ref[...] = gk

        @pl.when(i > 0)
        def _():
            gq_ref[...] += gq
            gk_ref[...] += gk

    def rowblk(w):
        return pl.BlockSpec((tile, w), lambda i: (i, 0))

    def whole(w):
        return pl.BlockSpec((1, w), lambda i: (0, 0))

    return _pallas(
        body, name=name, grid=(rows // tile,),
        in_specs=[rowblk(Q_LORA), rowblk(KV_LORA), rowblk(1024), rowblk(LANES), rowblk(LANES), rowblk(LANES),
                  whole(Q_LORA), whole(KV_LORA)],
        out_specs=[rowblk(1024), whole(Q_LORA), whole(KV_LORA)],
        out_shape=[jax.ShapeDtypeStruct((rows, 1024), BF16), jax.ShapeDtypeStruct((1, Q_LORA), F32),
                   jax.ShapeDtypeStruct((1, KV_LORA), F32)],
        compiler_params=_params(("arbitrary",)),
    )(dcqn, dckvn, u_small, dkpe, draw_f, draw_b, qg, kvg)


def _rope(x, c, s1, s2):
    return x * c + pltpu.roll(x, 112, 1) * s1 + pltpu.roll(x, 16, 1) * s2


def _rope_t(d, c, s1, s2):
    return d * c + pltpu.roll(d * s1, 16, 1) + pltpu.roll(d * s2, 112, 1)


def _qk_prep_fwd(q_raw, k_raw, u_small, tabs, qg, kg, *, name, tile=256):
    rows = q_raw.shape[0]
    tile = min(tile, rows)
    scale = 1.0 / math.sqrt(QK_HEAD)

    def body(q_ref, k_ref, u_ref, c_ref, s1_ref, s2_ref, qg_ref, kg_ref, qo_ref, ko_ref):
        c, s1, s2 = c_ref[...], s1_ref[...], s2_ref[...]
        qgv, kgv = qg_ref[...], kg_ref[...]
        kpe = pltpu.roll(u_ref[:, U_KPE:U_KPE + LANES], 64, 1)
        for h in range(N_HEADS):
            hs = slice(h * LANES, (h + 1) * LANES)
            qr = q_ref[:, hs]
            rq = lax.rsqrt(jnp.sum(qr * qr, axis=-1, keepdims=True) / QK_HEAD + EPS)
            qo_ref[:, hs] = (_rope(qr * rq * qgv, c, s1, s2) * scale).astype(BF16)
            xk = k_ref[:, hs] + kpe
            rk = lax.rsqrt(jnp.sum(xk * xk, axis=-1, keepdims=True) / QK_HEAD + EPS)
            ko_ref[:, hs] = _rope(xk * rk * kgv, c, s1, s2).astype(BF16)

    wide = pl.BlockSpec((tile, 2048), lambda i: (i, 0))
    narrow = pl.BlockSpec((tile, LANES), lambda i: (i, 0))
    gain = pl.BlockSpec((1, LANES), lambda i: (0, 0))
    return _pallas(
        body, name=name, grid=(rows // tile,),
        in_specs=[wide, wide, pl.BlockSpec((tile, 1024), lambda i: (i, 0)), narrow, narrow, narrow, gain, gain],
        out_specs=[wide, wide], out_shape=[jax.ShapeDtypeStruct((rows, 2048), BF16)] * 2,
        compiler_params=_params(("parallel",)),
    )(q_raw, k_raw, u_small, *tabs, qg, kg)


def _qk_prep_bwd(dq, dk, q_raw, k_raw, u_small, tabs, qg, kg, *, name, tile=256):
    rows = q_raw.shape[0]
    tile = min(tile, rows)
    scale = 1.0 / math.sqrt(QK_HEAD)

    def body(dq_ref, dk_ref, q_ref, k_ref, u_ref, c_ref, s1_ref, s2_ref, qg_ref, kg_ref,
             dqo_ref, dko_ref, dkpe_ref, gq_ref, gk_ref):
        c, s1, s2 = c_ref[...], s1_ref[...], s2_ref[...]
        qgv, kgv = qg_ref[...], kg_ref[...]
        kpe = pltpu.roll(u_ref[:, U_KPE:U_KPE + LANES], 64, 1)
        lane = lax.broadcasted_iota(jnp.int32, (tile, LANES), 1)
        gq = jnp.zeros((1, LANES), F32)
        gk = jnp.zeros((1, LANES), F32)
        dkpe = jnp.zeros((tile, LANES), F32)
        for h in range(N_HEADS):
            hs = slice(h * LANES, (h + 1) * LANES)
            qr = q_ref[:, hs]
            rq = lax.rsqrt(jnp.sum(qr * qr, axis=-1, keepdims=True) / QK_HEAD + EPS)
            xh = qr * rq
            dy = _rope_t(dq_ref[:, hs] * scale, c, s1, s2)
            dxh = dy * qgv
            dqo_ref[:, hs] = (rq * (dxh - xh * (jnp.sum(dxh * xh, axis=-1, keepdims=True) / QK_HEAD))).astype(BF16)
            gq = gq + jnp.sum(dy * xh, axis=0, keepdims=True)
            xk = k_ref[:, hs] + kpe
            rk = lax.rsqrt(jnp.sum(xk * xk, axis=-1, keepdims=True) / QK_HEAD + EPS)
            kh = xk * rk
            dyk = _rope_t(dk_ref[:, hs], c, s1, s2)
            dkh = dyk * kgv
            dxk = rk * (dkh - kh * (jnp.sum(dkh * kh, axis=-1, keepdims=True) / QK_HEAD))
            gk = gk + jnp.sum(dyk * kh, axis=0, keepdims=True)
            dko_ref[:, hs] = jnp.where(lane < QK_NOPE, dxk, 0.0).astype(BF16)
            dkpe = dkpe + dxk
        dkpe_ref[...] = jnp.where(lane < QK_ROPE, pltpu.roll(dkpe, 64, 1), 0.0)
        i = pl.program_id(0)

        @pl.when(i == 0)
        def _():
            gq_ref[...] = gq
            gk_ref[...] = gk

        @pl.when(i > 0)
        def _():
            gq_ref[...] += gq
            gk_ref[...] += gk

    wide = pl.BlockSpec((tile, 2048), lambda i: (i, 0))
    narrow = pl.BlockSpec((tile, LANES), lambda i: (i, 0))
    gain = pl.BlockSpec((1, LANES), lambda i: (0, 0))
    return _pallas(
        body, name=name, grid=(rows // tile,),
        in_specs=[wide, wide, wide, wide, pl.BlockSpec((tile, 1024), lambda i: (i, 0)), narrow, narrow, narrow,
                  gain, gain],
        out_specs=[wide, wide, narrow, gain, gain],
        out_shape=[jax.ShapeDtypeStruct((rows, 2048), BF16)] * 2
        + [jax.ShapeDtypeStruct((rows, LANES), F32), jax.ShapeDtypeStruct((1, LANES), F32),
           jax.ShapeDtypeStruct((1, LANES), F32)],
        compiler_params=_params(("arbitrary",)),
    )(dq, dk, q_raw, k_raw, u_small, *tabs, qg, kg)


def _attn_fwd(q, k, v, comm_modes, comm_arrays, *, name, tq=2048, tkc=512):
    T = q.shape[0]
    tq = min(tq, T)
    tkc = min(tkc, T)
    n = len(comm_arrays)
    nj, ni = N_HEADS // 2, T // tq

    def body(*refs):
        q_ref, k_ref, v_ref = refs[:3]
        srcs = refs[3:3 + n]
        o_ref, lse_ref = refs[3 + n:5 + n]
        dsts = refs[5 + n:5 + 2 * n]
        sems = refs[5 + 2 * n:]
        j, i = pl.program_id(0), pl.program_id(1)

        @pl.when((j == 0) & (i == 0))
        def _():
            _comm_start(comm_modes, srcs, dsts, sems)

        lane = lax.broadcasted_iota(jnp.int32, (1, LANES), 1)
        out = None
        for hh in range(2):
            sl = slice(hh * LANES, (hh + 1) * LANES)
            qv = q_ref[:, sl]
            spare = LANES - 1 if hh == 0 else 0
            keep = (lane < V_HEAD) if hh == 0 else (lane >= V_HEAD)
            m = acc = None
            for kc in range(T // tkc):
                ks = slice(kc * tkc, (kc + 1) * tkc)
                s = _dot_nt(qv, k_ref[ks, sl])
                vone = jnp.where(lane == spare, 1.0, v_ref[ks, sl]).astype(BF16)
                mc = jnp.max(s, axis=-1, keepdims=True)
                if m is None:
                    m = mc
                    acc = _dot(jnp.exp(s - m).astype(BF16), vone)
                else:
                    m_new = jnp.maximum(m, mc)
                    acc = jnp.exp(m - m_new) * acc + _dot(jnp.exp(s - m_new).astype(BF16), vone)
                    m = m_new
            l = acc[:, spare:spare + 1]
            o = jnp.where(keep, acc / l, 0.0)
            out = o if out is None else out + o
            lse_ref[hh] = m + jnp.log(l)
        o_ref[...] = out

        @pl.when((j == nj - 1) & (i == ni - 1))
        def _():
            _comm_wait(comm_modes, srcs, dsts, sems)

    any_spec = pl.BlockSpec(memory_space=pl.ANY)
    got = _pallas(
        body, name=name, grid=(nj, ni),
        in_specs=[pl.BlockSpec((tq, 2 * LANES), lambda j, i: (i, j)), pl.BlockSpec((T, 2 * LANES), lambda j, i: (0, j)),
                  pl.BlockSpec((T, 2 * LANES), lambda j, i: (0, j))] + [any_spec] * n,
        out_specs=[pl.BlockSpec((tq, LANES), lambda j, i: (i, j)), pl.BlockSpec((2, tq, 1), lambda j, i: (j, i, 0))]
        + [any_spec] * n,
        out_shape=[jax.ShapeDtypeStruct((T, N_HEADS * V_HEAD), F32), jax.ShapeDtypeStruct((N_HEADS, T, 1), F32)]
        + _comm_out_shapes(comm_modes, comm_arrays),
        scratch_shapes=_comm_scratch(n),
        compiler_params=_params(("arbitrary", "arbitrary")),
    )(q, k, v, *comm_arrays)
    return got[0], got[1], got[2:]


def _attn_bwd(q, k, v, o, lse, do, comm_modes, comm_arrays, *, name, tk=256, tqc=4096):
    T = q.shape[0]
    tk = min(tk, T)
    tqc = min(tqc, T)
    n = len(comm_arrays)
    nj, nkb = N_HEADS // 2, T // tk

    def body(*refs):
        q_ref, k_ref, v_ref, o_ref, lse_ref, do_ref = refs[:6]
        srcs = refs[6:6 + n]
        dq_ref, dk_ref, dv_ref = refs[6 + n:9 + n]
        dsts = refs[9 + n:9 + 2 * n]
        d_s = refs[9 + 2 * n]
        sems = refs[10 + 2 * n:]
        j, kb = pl.program_id(0), pl.program_id(1)

        @pl.when((j == 0) & (kb == 0))
        def _():
            _comm_start(comm_modes, srcs, dsts, sems)

        lane = lax.broadcasted_iota(jnp.int32, (1, LANES), 1)
        @pl.when(kb == 0)
        def _():
            prod = do_ref[...] * o_ref[...]
            for hh in range(2):
                keep = (lane < V_HEAD) if hh == 0 else (lane >= V_HEAD)
                d_s[hh] = jnp.sum(jnp.where(keep, prod, 0.0), axis=-1, keepdims=True)

        for hh in range(2):
            sl = slice(hh * LANES, (hh + 1) * LANES)
            keep = (lane < V_HEAD) if hh == 0 else (lane >= V_HEAD)
            kv, vv = k_ref[:, sl], v_ref[:, sl]
            dv_acc = dk_acc = None
            for qc in range(T // tqc):
                qs = slice(qc * tqc, (qc + 1) * tqc)
                qv = q_ref[qs, sl]
                do_b = do_ref[qs, :].astype(BF16)
                s = _dot_nt(qv, kv)
                p = jnp.exp(s - lse_ref[hh, qs])
                dp = _dot_nt(do_b, vv)
                ds = (p * (dp - d_s[hh, qs])).astype(BF16)
                dvc = _dot_tn(p.astype(BF16), do_b)
                dkc = _dot_tn(ds, qv)
                dv_acc = dvc if dv_acc is None else dv_acc + dvc
                dk_acc = dkc if dk_acc is None else dk_acc + dkc
                dqp = _dot(ds, kv)

                @pl.when(kb == 0)
                def _(dqp=dqp, sl=sl, qs=qs):
                    dq_ref[qs, sl] = dqp

                @pl.when(kb > 0)
                def _(dqp=dqp, sl=sl, qs=qs):
                    dq_ref[qs, sl] += dqp

            dv_ref[:, sl] = jnp.where(keep, dv_acc, 0.0).astype(BF16)
            dk_ref[:, sl] = dk_acc

        @pl.when((j == nj - 1) & (kb == nkb - 1))
        def _():
            _comm_wait(comm_modes, srcs, dsts, sems)

    any_spec = pl.BlockSpec(memory_space=pl.ANY)
    pair = pl.BlockSpec((T, 2 * LANES), lambda j, kb: (0, j))
    kblk = pl.BlockSpec((tk, 2 * LANES), lambda j, kb: (kb, j))
    got = _pallas(
        body, name=name, grid=(nj, nkb),
        in_specs=[pair, kblk, kblk, pl.BlockSpec((T, LANES), lambda j, kb: (0, j)),
                  pl.BlockSpec((2, T, 1), lambda j, kb: (j, 0, 0)), pl.BlockSpec((T, LANES), lambda j, kb: (0, j))]
        + [any_spec] * n,
        out_specs=[pair, kblk, kblk] + [any_spec] * n,
        out_shape=[jax.ShapeDtypeStruct((T, 2048), F32)] * 2 + [jax.ShapeDtypeStruct((T, 2048), BF16)]
        + _comm_out_shapes(comm_modes, comm_arrays),
        scratch_shapes=[pltpu.VMEM((2, T, 1), F32)] + _comm_scratch(n),
        compiler_params=_params(("arbitrary", "arbitrary")),
    )(q, k, v, o, lse, do, *comm_arrays)
    return got[0], got[1], got[2], got[3:]


CONV_ROWS, CONV_HALO = 64, 8
CONV_WIN = CONV_ROWS + 2 * CONV_HALO


def _conv_shift(x, sh, t_idx, total):
    if sh == 0:
        return x
    y = pltpu.roll(x, (-sh) % x.shape[0], 0)
    if t_idx is None:
        return y
    ok = (t_idx + sh >= 0) & (t_idx + sh < total)
    return jnp.where(ok, y, 0.0)


def _conv_positions(ws, shape):
    return ws + lax.broadcasted_iota(jnp.int32, shape, 0) if isinstance(ws, int) else None


def _aligned(v, m):
    return v if isinstance(v, int) else pl.multiple_of(v, m)


def _conv_chunks(T, chunk, carry):
    n = T // CONV_ROWS
    carry = chunk(0, 0, carry)

    def mid(ci, c):
        return chunk(pl.multiple_of(ci * CONV_ROWS - CONV_HALO, CONV_HALO), CONV_HALO, c)

    carry = lax.fori_loop(1, n - 1, mid, carry)
    return chunk(T - CONV_WIN, 2 * CONV_HALO, carry)


def _conv_pre(x, w_ref, b_ref, t_idx, total):
    pre = b_ref[...] + w_ref[2:3, :] * x
    for j in (0, 1, 3, 4):
        pre = pre + w_ref[j:j + 1, :] * _conv_shift(x, j - 2, t_idx, total)
    return pre


def _conv_fwd(u_big, conv_w, conv_b, *, name, w=256):
    T = u_big.shape[0]
    first = D_INNER // w

    def body(x_ref, w_ref, b_ref, o_ref):
        def chunk(ws, off, carry):
            x = x_ref[pl.ds(ws, CONV_WIN), :]
            pre = _conv_pre(x, w_ref, b_ref, _conv_positions(ws, x.shape), T)
            act = pre * _sigmoid(pre)
            o_ref[pl.ds(_aligned(ws + off, CONV_ROWS), CONV_ROWS), :] = act[off:off + CONV_ROWS]
            return carry

        _conv_chunks(T, chunk, 0)

    return _pallas(
        body, name=name, grid=(XBC_DIM // w,),
        in_specs=[pl.BlockSpec((T, w), lambda j: (0, j + first)), pl.BlockSpec((CONV_WIDTH, w), lambda j: (0, j)),
                  pl.BlockSpec((1, w), lambda j: (0, j))],
        out_specs=pl.BlockSpec((T, w), lambda j: (0, j)),
        out_shape=jax.ShapeDtypeStruct((T, XBC_DIM), F32),
        compiler_params=_params(("parallel",)),
    )(u_big, conv_w, conv_b)


def _conv_bwd(dact_f, dact_b, u_big, conv_w, conv_b, *, name, w=128):
    T = u_big.shape[0]
    first = D_INNER // w

    def body(df_ref, db_ref, x_ref, w_ref, b_ref, dx_ref, dw_ref, dbias_ref):
        def chunk(ws, off, sums):
            rows = pl.ds(ws, CONV_WIN)
            x = x_ref[rows, :]
            row = lax.broadcasted_iota(jnp.int32, x.shape, 0)
            t_idx = _conv_positions(ws, x.shape)
            pre = _conv_pre(x, w_ref, b_ref, t_idx, T)
            sg = _sigmoid(pre)
            dpre = (df_ref[rows, :] + db_ref[rows, :]) * (sg * (1.0 + pre * (1.0 - sg)))
            dx = w_ref[2:3, :] * dpre
            for j in (0, 1, 3, 4):
                dx = dx + w_ref[j:j + 1, :] * _conv_shift(dpre, 2 - j, t_idx, T)
            dx_ref[pl.ds(_aligned(ws + off, CONV_ROWS), CONV_ROWS), :] = dx[off:off + CONV_ROWS].astype(dx_ref.dtype)
            own = jnp.where((row >= off) & (row < off + CONV_ROWS), dpre, 0.0)
            new = [sums[5] + jnp.sum(own, axis=0, keepdims=True)]
            for j in range(CONV_WIDTH):
                new.insert(j, sums[j] + jnp.sum(own * _conv_shift(x, j - 2, t_idx, T), axis=0, keepdims=True))
            return tuple(new)

        zero = jnp.zeros((1, w), F32)
        sums = _conv_chunks(T, chunk, (zero,) * (CONV_WIDTH + 1))
        for j in range(CONV_WIDTH):
            dw_ref[j:j + 1, :] = sums[j]
        dbias_ref[...] = sums[CONV_WIDTH]

    blk = pl.BlockSpec((T, w), lambda j: (0, j))
    return _pallas(
        body, name=name, grid=(XBC_DIM // w,),
        in_specs=[blk, blk, pl.BlockSpec((T, w), lambda j: (0, j + first)),
                  pl.BlockSpec((CONV_WIDTH, w), lambda j: (0, j)), pl.BlockSpec((1, w), lambda j: (0, j))],
        out_specs=[blk, pl.BlockSpec((CONV_WIDTH, w), lambda j: (0, j)), pl.BlockSpec((1, w), lambda j: (0, j))],
        out_shape=[jax.ShapeDtypeStruct((T, XBC_DIM), BF16), jax.ShapeDtypeStruct((CONV_WIDTH, XBC_DIM), F32),
                   jax.ShapeDtypeStruct((1, XBC_DIM), F32)],
        compiler_params=_params(("parallel",)),
    )(dact_f, dact_b, u_big, conv_w, conv_b)


def _ssd_expand():
    h = jnp.arange(LANES, dtype=jnp.int32)[:, None]
    return (jnp.arange(D_INNER, dtype=jnp.int32)[None, :] // 64 == h).astype(F32)


def _ssd_head_terms(dt_ref, bias_ref, alog_ref, acst_s, dtt_s, rev):
    L = CHUNK
    row = lax.broadcasted_iota(jnp.int32, (L, L), 0)
    col = lax.broadcasted_iota(jnp.int32, (L, L), 1)
    mask = (row <= col) if rev else (row >= col)
    cm = mask.astype(F32)
    cmt = ((row >= col) if rev else (row <= col)).astype(F32)
    pre = dt_ref[...] + bias_ref[...]
    dt = _softplus(pre)
    a = -jnp.exp(alog_ref[...])
    da = dt * a
    acs = _dot_h(cm, da)
    acst_s[...] = _dot_h_tn(da, cmt)
    dtt_s[...] = _dot_h_tn(dt, (row == col).astype(F32))
    tot = jnp.sum(da, axis=0, keepdims=True)
    w = jnp.exp(tot - acs)
    return dict(mask=mask, cm=cm, cmt=cmt, ident=(row == col).astype(F32), pre=pre, dt=dt, a=a, da=da, acs=acs,
                tot=tot, e=jnp.exp(acs), w=w, wdt=w * dt, dec=jnp.exp(tot))


def _pair(lo, v, h0):
    return jnp.where(lo, v[:, h0:h0 + 1], v[:, h0 + 1:h0 + 2])


def _ssd_fwd(xbc_act, u_small, bias128, alog128, *, rev, name):
    T = xbc_act.shape[0]
    L = CHUNK
    nc = T // L
    off = SSM_HEADS if rev else 0

    def cidx(c):
        return (nc - 1 - c) if rev else c

    def body(xs_ref, bm_ref, cm_ref, dt_ref, bias_ref, alog_ref, y_ref, hin_ref, ht_s, acst_s, dtt_s, wx_s, dec_s):
        c = pl.program_id(0)

        @pl.when(c == 0)
        def _():
            ht_s[...] = jnp.zeros_like(ht_s)

        t = _ssd_head_terms(dt_ref, bias_ref, alog_ref, acst_s, dtt_s, rev)
        lo = lax.broadcasted_iota(jnp.int32, (L, LANES), 1) < 64
        lo1 = lax.broadcasted_iota(jnp.int32, (1, LANES), 1) < 64
        for g in range(SSM_GROUPS):
            bmat = bm_ref[:, g * LANES:(g + 1) * LANES].astype(BF16)
            cmat = cm_ref[:, g * LANES:(g + 1) * LANES].astype(BF16)
            gmat = _dot_nt(cmat, bmat)
            ht = ht_s[g]
            ch = _dot(cmat, ht.astype(BF16))
            for pr in range(4):
                ps = slice(pr * LANES, (pr + 1) * LANES)
                cs = slice(g * 512 + pr * LANES, g * 512 + (pr + 1) * LANES)
                h0 = off + 8 * g + 2 * pr
                xp = xs_ref[:, cs]
                acc = _pair(lo, t["e"], h0) * ch[:, ps]
                for s_ in range(2):
                    h = h0 + s_
                    seg = t["acs"][:, h:h + 1] - acst_s[h:h + 1, :]
                    lam = jnp.exp(jnp.where(t["mask"], seg, -1e30))
                    m = (gmat * lam * dtt_s[h:h + 1, :]).astype(BF16)
                    xm = jnp.where(lo if s_ == 0 else jnp.logical_not(lo), xp, 0.0).astype(BF16)
                    acc = acc + _dot(m, xm)
                y_ref[:, cs] = acc
                wx_s[:, ps] = (_pair(lo, t["wdt"], h0) * xp).astype(BF16)
                dec_s[0:1, ps] = _pair(lo1, t["dec"], h0)
            hin_ref[0, g] = ht.astype(BF16)
            ht_s[g] = ht * dec_s[0:1, :] + _dot_tn(bmat, wx_s[...])

    return _pallas(
        body, name=name, grid=(nc,),
        in_specs=[pl.BlockSpec((L, D_INNER), lambda c: (cidx(c), 0)), pl.BlockSpec((L, 512), lambda c: (cidx(c), 4)),
                  pl.BlockSpec((L, 512), lambda c: (cidx(c), 5)),
                  pl.BlockSpec((L, LANES), lambda c: (cidx(c), U_DT // LANES)),
                  pl.BlockSpec((1, LANES), lambda c: (0, 0)), pl.BlockSpec((1, LANES), lambda c: (0, 0))],
        out_specs=[pl.BlockSpec((L, D_INNER), lambda c: (cidx(c), 0)),
                   pl.BlockSpec((1, SSM_GROUPS, D_STATE, 512), lambda c: (cidx(c), 0, 0, 0))],
        out_shape=[jax.ShapeDtypeStruct((T, D_INNER), F32), jax.ShapeDtypeStruct((nc, SSM_GROUPS, D_STATE, 512), BF16)],
        scratch_shapes=[pltpu.VMEM((SSM_GROUPS, D_STATE, 512), F32), pltpu.VMEM((LANES, L), F32),
                        pltpu.VMEM((LANES, L), F32), pltpu.VMEM((L, 512), BF16), pltpu.VMEM((8, 512), F32)],
        compiler_params=_params(("arbitrary",)),
    )(xbc_act, xbc_act, xbc_act, u_small, bias128, alog128)


def _ssd_bwd(dy, xbc_act, u_small, bias128, alog128, hin, skip_x, *, rev, name):
    T = xbc_act.shape[0]
    L = CHUNK
    nc = T // L
    off = SSM_HEADS if rev else 0
    has_skip = skip_x is not None

    def cidx(c):
        return c if rev else (nc - 1 - c)

    def body(*refs):
        (dy_ref, xs_ref, bm_ref, cm_ref, dt_ref, bias_ref, alog_ref, hin_ref) = refs[:8]
        k = 8
        skip_ref = refs[k] if has_skip else None
        k += 1 if has_skip else 0
        (dx_ref, draw_ref, dalog_ref, dbias_ref, dht_s, acst_s, dtt_s, rowt_s, ddtt_s, wx_s, edy_s, dec_s) = refs[k:]
        c = pl.program_id(0)

        @pl.when(c == 0)
        def _():
            dht_s[...] = jnp.zeros_like(dht_s)
            rowt_s[...] = jnp.zeros_like(rowt_s)
            ddtt_s[...] = jnp.zeros_like(ddtt_s)

        t = _ssd_head_terms(dt_ref, bias_ref, alog_ref, acst_s, dtt_s, rev)
        lane1 = lax.broadcasted_iota(jnp.int32, (1, LANES), 1)
        lo = lax.broadcasted_iota(jnp.int32, (L, LANES), 1) < 64
        lo1 = lane1 < 64
        colpart = jnp.zeros((L, LANES), F32)
        u_cols = jnp.zeros((L, LANES), F32)
        v_cols = jnp.zeros((L, LANES), F32)
        dtot_h = jnp.zeros((1, LANES), F32)
        for g in range(SSM_GROUPS):
            bmat = bm_ref[:, g * LANES:(g + 1) * LANES].astype(BF16)
            cmat = cm_ref[:, g * LANES:(g + 1) * LANES].astype(BF16)
            gmat = _dot_nt(cmat, bmat)
            ht_in = hin_ref[0, g]
            dht = dht_s[g]
            ht_in_b, dht_b = ht_in.astype(BF16), dht.astype(BF16)
            ch = _dot(cmat, ht_in_b)
            bdh = _dot(bmat, dht_b)
            th = jnp.sum(dht * ht_in, axis=0, keepdims=True)
            dgm = jnp.zeros((L, L), F32)
            for pr in range(4):
                ps = slice(pr * LANES, (pr + 1) * LANES)
                cs = slice(g * 512 + pr * LANES, g * 512 + (pr + 1) * LANES)
                h0 = off + 8 * g + 2 * pr
                xp = xs_ref[:, cs]
                dyp = dy_ref[:, cs]
                dyp_b = dyp.astype(BF16)
                wdt_p = _pair(lo, t["wdt"], h0)
                e_p = _pair(lo, t["e"], h0)
                xb = xp * bdh[:, ps]
                dc = dyp * ch[:, ps]
                dxp = wdt_p * bdh[:, ps]
                for s_ in range(2):
                    h = h0 + s_
                    keep = lo if s_ == 0 else jnp.logical_not(lo)
                    keep1 = lo1 if s_ == 0 else jnp.logical_not(lo1)
                    onehot = (lane1 == h).astype(F32)
                    dtrow = dtt_s[h:h + 1, :]
                    seg = t["acs"][:, h:h + 1] - acst_s[h:h + 1, :]
                    lam = jnp.exp(jnp.where(t["mask"], seg, -1e30))
                    mf0 = gmat * lam
                    m = (mf0 * dtrow).astype(BF16)
                    xm = jnp.where(keep, xp, 0.0).astype(BF16)
                    dm = _dot_nt(dyp_b, xm)
                    r = dm * mf0
                    q = r * dtrow
                    dgm = dgm + dm * lam * dtrow
                    colpart = colpart + jnp.sum(q, axis=1, keepdims=True) * onehot
                    rowt_s[h:h + 1, :] = jnp.sum(q, axis=0, keepdims=True)
                    ddtt_s[h:h + 1, :] = jnp.sum(r, axis=0, keepdims=True)
                    u_cols = u_cols + jnp.sum(jnp.where(keep, xb, 0.0), axis=1, keepdims=True) * onehot
                    v_cols = v_cols + jnp.sum(jnp.where(keep, dc, 0.0), axis=1, keepdims=True) * onehot
                    dtot_h = dtot_h + jnp.sum(jnp.where(keep1, th[:, ps], 0.0), axis=1, keepdims=True) * onehot
                    dxp = dxp + jnp.where(keep, _dot_tn(m, dyp_b), 0.0)
                if has_skip:
                    dxp = dxp + dyp * skip_ref[:, cs]
                dx_ref[:, cs] = dxp
                wx_s[:, ps] = (wdt_p * xp).astype(BF16)
                edy_s[:, ps] = (e_p * dyp).astype(BF16)
                dec_s[0:1, ps] = _pair(lo1, t["dec"], h0)
            edy_b = edy_s[...]
            dgm_b = dgm.astype(BF16)
            dx_ref[:, D_INNER + g * LANES:D_INNER + (g + 1) * LANES] = (
                _dot_nt(wx_s[...], dht_b) + _dot_tn(dgm_b, cmat))
            dx_ref[:, D_INNER + 512 + g * LANES:D_INNER + 512 + (g + 1) * LANES] = (
                _dot_nt(edy_b, ht_in_b) + _dot(dgm_b, bmat))
            dht_s[g] = dec_s[0:1, :] * dht + _dot_tn(cmat, edy_b)

        t_e = v_cols * t["e"]
        t_w = u_cols * t["wdt"]
        colsum_part = _dot_h_tn(rowt_s[...], t["ident"])
        dtot = jnp.sum(t_w, axis=0, keepdims=True) + t["dec"] * dtot_h
        row1 = lax.broadcasted_iota(jnp.int32, (L, LANES), 0)
        last = row1 == (0 if rev else L - 1)
        dacs = colpart - colsum_part + t_e - t_w + jnp.where(last, dtot, 0.0)
        dda = _dot_h(t["cmt"], dacs)
        ddt = dda * t["a"] + u_cols * t["w"] + _dot_h_tn(ddtt_s[...], t["ident"])
        dalog = jnp.sum(dda * t["dt"], axis=0, keepdims=True) * t["a"]
        draw = ddt * _sigmoid(t["pre"])
        draw_ref[...] = draw
        dbias = jnp.sum(draw, axis=0, keepdims=True)

        @pl.when(c == 0)
        def _():
            dalog_ref[...] = dalog
            dbias_ref[...] = dbias

        @pl.when(c > 0)
        def _():
            dalog_ref[...] += dalog
            dbias_ref[...] += dbias

    one = pl.BlockSpec((1, LANES), lambda c: (0, 0))
    in_specs = [pl.BlockSpec((L, D_INNER), lambda c: (cidx(c), 0)), pl.BlockSpec((L, D_INNER), lambda c: (cidx(c), 0)),
                pl.BlockSpec((L, 512), lambda c: (cidx(c), 4)), pl.BlockSpec((L, 512), lambda c: (cidx(c), 5)),
                pl.BlockSpec((L, LANES), lambda c: (cidx(c), U_DT // LANES)), one, one,
                pl.BlockSpec((1, SSM_GROUPS, D_STATE, 512), lambda c: (cidx(c), 0, 0, 0))]
    ins = [dy, xbc_act, xbc_act, xbc_act, u_small, bias128, alog128, hin]
    if has_skip:
        in_specs.append(pl.BlockSpec((1, D_INNER), lambda c: (0, 0)))
        ins.append(skip_x)
    return _pallas(
        body, name=name, grid=(nc,), in_specs=in_specs,
        out_specs=[pl.BlockSpec((L, XBC_DIM), lambda c: (cidx(c), 0)), pl.BlockSpec((L, LANES), lambda c: (cidx(c), 0)),
                   one, one],
        out_shape=[jax.ShapeDtypeStruct((T, XBC_DIM), F32), jax.ShapeDtypeStruct((T, LANES), F32),
                   jax.ShapeDtypeStruct((1, LANES), F32), jax.ShapeDtypeStruct((1, LANES), F32)],
        scratch_shapes=[pltpu.VMEM((SSM_GROUPS, D_STATE, 512), F32), pltpu.VMEM((LANES, L), F32),
                        pltpu.VMEM((LANES, L), F32), pltpu.VMEM((LANES, L), F32), pltpu.VMEM((LANES, L), F32),
                        pltpu.VMEM((L, 512), BF16), pltpu.VMEM((L, 512), BF16), pltpu.VMEM((8, 512), F32)],
        compiler_params=_params(("arbitrary",)),
    )(*ins)


def _ssm_out_fwd(y_f, y_b, xbc_act, u_big, skip_x, ssm_norm, *, name, tile=512):
    rows = y_f.shape[0]

    def fn(yf, yb, xs, z, sk, nw):
        yz = (yf + yb + sk * xs) * (z * _sigmoid(z))
        r = lax.rsqrt(jnp.mean(yz * yz, axis=-1, keepdims=True) + EPS)
        return yz * r * nw

    return _rowmap(fn, name=name, rows=rows, tile=tile, ncol=SSM_GROUPS,
                   ins=[(y_f, 512, _colj), (y_b, 512, _colj), (xbc_act, 512, _colj), (u_big, 512, _colj)],
                   consts=[(skip_x, 512, _colj), (ssm_norm, 512, _colj)], outs=[(D_INNER, BF16, 512, _colj)])[0]


def _ssm_out_bwd(dm, y_f, y_b, xbc_act, u_big, skip_x, ssm_norm, *, name, tile=512):
    rows = y_f.shape[0]

    def fn(dmv, yf, yb, xs, z, sk, nw):
        sg = _sigmoid(z)
        y = yf + yb + sk * xs
        yz = y * (z * sg)
        r = lax.rsqrt(jnp.mean(yz * yz, axis=-1, keepdims=True) + EPS)
        xh = yz * r
        dxh = dmv * nw
        dyz = r * (dxh - xh * jnp.mean(dxh * xh, axis=-1, keepdims=True))
        dy = dyz * (z * sg)
        dz = dyz * y * (sg * (1.0 + z * (1.0 - sg)))
        return dy, dz, jnp.sum(dmv * xh, axis=0, keepdims=True), jnp.sum(dy * xs, axis=0, keepdims=True)

    return _rowmap(fn, name=name, rows=rows, tile=tile, ncol=SSM_GROUPS,
                   ins=[(dm, 512, _colj), (y_f, 512, _colj), (y_b, 512, _colj), (xbc_act, 512, _colj),
                        (u_big, 512, _colj)],
                   consts=[(skip_x, 512, _colj), (ssm_norm, 512, _colj)],
                   outs=[(D_INNER, F32, 512, _colj), (D_INNER, BF16, 512, _colj)],
                   accs=[(D_INNER, 512, _colj), (D_INNER, 512, _colj)])


def _merge_fwd(pa, pb, u_big, *, name, tile=512):
    rows = pa.shape[0]

    def fn(a, b, ga, gb):
        return _sigmoid(ga) * a + _sigmoid(gb) * b

    return _rowmap(fn, name=name, rows=rows, tile=tile,
                   ins=[(pa, 1024, _col0), (pb, 1024, _col0), (u_big, 1024, lambda j: 5), (u_big, 1024, lambda j: 6)],
                   outs=[(1024, BF16, 1024, _col0)])[0]


def _merge_bwd(dmg, pa, pb, u_big, *, name, tile=512):
    rows = pa.shape[0]

    def fn(d, a, b, ga, gb):
        sa, sb = _sigmoid(ga), _sigmoid(gb)
        return d * sa, d * sb, d * a * sa * (1.0 - sa), d * b * sb * (1.0 - sb)

    return _rowmap(fn, name=name, rows=rows, tile=tile,
                   ins=[(dmg, 1024, _col0), (pa, 1024, _col0), (pb, 1024, _col0), (u_big, 1024, lambda j: 5),
                        (u_big, 1024, lambda j: 6)],
                   outs=[(1024, BF16, 1024, _col0)] * 4)


def _loss_bwd(y, target, *, name, tile=512):
    rows, d = y.shape

    def fn(yv, tv):
        err = yv - tv
        part = jnp.sum(jnp.sum(err * err, axis=-1, keepdims=True), axis=0, keepdims=True)
        return err * (1.0 / d), jnp.broadcast_to(part * (0.5 / d), (1, LANES))

    dy, part = _rowmap(fn, name=name, rows=rows, tile=tile, ins=[(y, d, _col0), (target, d, _col0)],
                       outs=[(d, F32, d, _col0)], accs=[(LANES, LANES, _col0)])
    return dy, part[0, 0]


def _small_slab(gs, dskip_ch, dalog_f, dalog_b, dbias_f, dbias_b, gkv, gqh, gkh, dconv_w, *, name):
    e_mat = _ssd_expand()
    full_names = ("ffn1_norm", "mix_norm", "q_a_norm", "conv_b", "ssm_norm", "ffn2_norm")
    full = [gs[n] for n in full_names]
    nf = len(full)

    def body(*refs):
        fulls = refs[:nf]
        (dsk_ref, e_ref, af_ref, ab_ref, bf_ref, bb_ref, gkv_ref, gqh_ref, gkh_ref, cw_ref, o_ref) = refs[nf:]
        o_ref[...] = jnp.zeros_like(o_ref)
        for n, r in zip(full_names, fulls):
            o_ref[SMALL_ROW[n]:SMALL_ROW[n] + 1, 0:r.shape[1]] = r[...]
        o_ref[SMALL_ROW["kv_a_norm"]:SMALL_ROW["kv_a_norm"] + 1, 0:KV_LORA] = gkv_ref[...]
        o_ref[SMALL_ROW["q_head_norm"]:SMALL_ROW["q_head_norm"] + 1, 0:LANES] = gqh_ref[...]
        o_ref[SMALL_ROW["k_head_norm"]:SMALL_ROW["k_head_norm"] + 1, 0:LANES] = gkh_ref[...]
        o_ref[SMALL_ROW["a_log_fwd"]:SMALL_ROW["a_log_fwd"] + 1, 0:LANES] = af_ref[...]
        o_ref[SMALL_ROW["a_log_bwd"]:SMALL_ROW["a_log_bwd"] + 1, 0:LANES] = pltpu.roll(ab_ref[...], 96, 1)
        o_ref[SMALL_ROW["dt_bias_fwd"]:SMALL_ROW["dt_bias_fwd"] + 1, 0:LANES] = bf_ref[...]
        o_ref[SMALL_ROW["dt_bias_bwd"]:SMALL_ROW["dt_bias_bwd"] + 1, 0:LANES] = pltpu.roll(bb_ref[...], 96, 1)
        dsk = _dot_h_nt(jnp.broadcast_to(dsk_ref[...], (8, D_INNER)), e_ref[...])
        o_ref[SMALL_ROW["d_skip"]:SMALL_ROW["d_skip"] + 1, 0:LANES] = dsk[0:1, :]
        o_ref[CONV_ROW:CONV_ROW + CONV_WIDTH, :] = cw_ref[...]

    return _pallas(body, name=name, out_shape=jax.ShapeDtypeStruct((SMALL_ROWS, SMALL_COLS), F32))(
        *full, dskip_ch, e_mat, dalog_f, dalog_b, dbias_f, dbias_b, gkv, gqh, gkh, dconv_w)


def _adamw_math(g, w, m, v):
    m2 = ADAM_B1 * m + (1.0 - ADAM_B1) * g
    v2 = ADAM_B2 * v + (1.0 - ADAM_B2) * (g * g)
    m_hat = m2 / (1.0 - ADAM_B1 ** ADAM_STEP)
    v_hat = v2 / (1.0 - ADAM_B2 ** ADAM_STEP)
    delta = -ADAM_LR * (m_hat / (jnp.sqrt(v_hat) + ADAM_EPS) + ADAM_WD * w)
    return delta, m2, v2


def _sum8(r_ref):
    g = r_ref[0].astype(F32)
    for s in range(1, N_DEV):
        g = g + r_ref[s].astype(F32)
    return g


def _reduce_adamw(recv, w, m, v, *, name, tile=256):
    _, R, C = recv.shape
    tile = _pick(R, tile) if R % LANES == 0 else R
    assert R % tile == 0

    def body(r_ref, w_ref, m_ref, v_ref, g_ref, d_ref, m2_ref, v2_ref):
        g = _sum8(r_ref)
        delta, m2, v2 = _adamw_math(g, w_ref[...], m_ref[...], v_ref[...])
        g_ref[...] = g
        d_ref[...] = delta
        m2_ref[...] = m2
        v2_ref[...] = v2

    blk = pl.BlockSpec((tile, C), lambda i: (i, 0))
    return _pallas(
        body, name=name, grid=(R // tile,),
        in_specs=[pl.BlockSpec((N_DEV, tile, C), lambda i: (0, i, 0)), blk, blk, blk], out_specs=[blk] * 4,
        out_shape=[jax.ShapeDtypeStruct((R, C), F32)] * 4, compiler_params=_params(("parallel",)),
    )(recv, w, m, v)


def _reduce_t_adamw(recv, w, m, v, *, name):
    R, cs = w.shape

    def body(r_ref, w_ref, m_ref, v_ref, g_ref, d_ref, m2_ref, v2_ref):
        g = _sum8(r_ref).T
        delta, m2, v2 = _adamw_math(g, w_ref[...], m_ref[...], v_ref[...])
        g_ref[...] = g
        d_ref[...] = delta
        m2_ref[...] = m2
        v2_ref[...] = v2

    return _pallas(body, name=name, out_shape=[jax.ShapeDtypeStruct((R, cs), F32)] * 4,
                   compiler_params=pltpu.CompilerParams(vmem_limit_bytes=VMEM_LIMIT))(recv, w, m, v)


def _reduce8(recv, *, name, tile):
    _, R, C = recv.shape

    def body(r_ref, g_ref):
        g_ref[...] = _sum8(r_ref)

    return _pallas(body, name=name, grid=(R // tile,),
                   in_specs=[pl.BlockSpec((N_DEV, tile, C), lambda i: (0, i, 0))],
                   out_specs=pl.BlockSpec((tile, C), lambda i: (i, 0)),
                   out_shape=jax.ShapeDtypeStruct((R, C), F32), compiler_params=_params(("parallel",)))(recv)


def _adamw(g, w, m, v, *, name, tile=256):
    R, C = w.shape

    def body(g_ref, w_ref, m_ref, v_ref, d_ref, m2_ref, v2_ref):
        delta, m2, v2 = _adamw_math(g_ref[...], w_ref[...], m_ref[...], v_ref[...])
        d_ref[...] = delta
        m2_ref[...] = m2
        v2_ref[...] = v2

    blk = pl.BlockSpec((R, tile), lambda i: (0, i))
    return _pallas(body, name=name, grid=(C // tile,), in_specs=[blk] * 4, out_specs=[blk] * 3,
                   out_shape=[jax.ShapeDtypeStruct((R, C), F32)] * 3, compiler_params=_params(("parallel",)))(g, w, m, v)


def _adamw_small(srecv, conv_g, ws, ms, vs, *, name):
    n = len(ws)

    def body(*refs):
        s_ref, c_ref = refs[0], refs[1]
        w_refs, m_refs, v_refs = refs[2:2 + n], refs[2 + n:2 + 2 * n], refs[2 + 2 * n:2 + 3 * n]
        outs = refs[2 + 3 * n:]
        gsum = _sum8(s_ref)
        for i in range(n):
            if i < len(SMALL):
                g = gsum[i:i + 1, 0:SMALL[i][1]]
            else:
                g = _sum8(c_ref)
            delta, m2, v2 = _adamw_math(g, w_refs[i][...], m_refs[i][...], v_refs[i][...])
            outs[i][...] = g
            outs[n + i][...] = delta
            outs[2 * n + i][...] = m2
            outs[3 * n + i][...] = v2

    shapes = [jax.ShapeDtypeStruct(w.shape, F32) for w in ws]
    got = _pallas(body, name=name, out_shape=shapes * 4,
                  compiler_params=pltpu.CompilerParams(vmem_limit_bytes=VMEM_LIMIT))(srecv, conv_g, *ws, *ms, *vs)
    return got[:n], got[n:2 * n], got[2 * n:3 * n], got[3 * n:]


def _ffn_fwd(x, norm, w_g_t, w_u_t, w_d, tag):
    h = _rms_fwd(x, norm, name=f"{tag}_rms")
    gu = _mm(h, jnp.concatenate([w_g_t, w_u_t], axis=0), name=f"{tag}_gu", tb=True, out_dtype=BF16)
    act = _swiglu_fwd(gu, name=f"{tag}_act")
    out = _mm(act, w_d, name=f"{tag}_down", alpha=0.5, res=x)
    return out, (h, gu, act)


def _ffn_bwd(dout, x, norm, w_g_t, w_u_t, w_d, saved, tag, send=None):
    h, gu, act = saved
    d_act = _mm(dout, w_d, name=f"{tag}_dact", tb=True, alpha=0.5, out_dtype=BF16)
    d_wd = _mm(act, dout, name=f"{tag}_dwd", ta=True, alpha=0.5, tm=1408, tn=1024, out_dtype=BF16)
    tok = send(("down",), [d_wd]) if send else None
    dg, du = _swiglu_bwd(d_act, gu, name=f"{tag}_dswiglu")
    d_wg_t = _mm(dg, h, name=f"{tag}_dwg", ta=True, tm=1408, tn=1024, out_dtype=BF16, after=tok)
    d_wu_t = _mm(du, h, name=f"{tag}_dwu", ta=True, tm=1408, tn=1024, out_dtype=BF16)
    tok = send(("gate", "up"), [d_wg_t, d_wu_t]) if send else None
    dh = _mm(dg, w_g_t, name=f"{tag}_dh_g", after=tok)
    dh = _mm(du, w_u_t, name=f"{tag}_dh_u", res=dh)
    dx, dnorm = _rms_bwd(dh, x, norm, dout, name=f"{tag}_drms")
    return dx, dnorm, d_wg_t, d_wu_t, d_wd


def _rope_tables(positions, T):
    pos = positions.reshape(T).astype(F32)
    inv_freq = 1.0 / (ROPE_BASE ** (jnp.arange(0, QK_ROPE, 2, dtype=F32) / QK_ROPE))
    ang = pos[:, None] * inv_freq
    cos, sin = jnp.cos(ang), jnp.sin(ang)
    one64, z64 = jnp.ones((T, 64), F32), jnp.zeros((T, 64), F32)
    z16, z32, one32 = jnp.zeros((T, 16), F32), jnp.zeros((T, 32), F32), jnp.ones((T, 32), F32)
    c = jnp.concatenate([one64, cos, cos, one32], axis=1)
    s1 = jnp.concatenate([z64, -sin, z16, z32], axis=1)
    s2 = jnp.concatenate([z64, z16, sin, z32], axis=1)
    return c, s1, s2


def _cols(g):
    n, r, cs = g.shape
    return g.transpose(1, 0, 2).reshape(r, n * cs)


def _rows(g):
    n, rs, c = g.shape
    return g.reshape(n * rs, c)


def _pad_lanes(v, n=LANES):
    return jnp.pad(v, ((0, 0), (0, n - v.shape[1])))


def _in_proj_weights(w_in_t):
    z = lambda n: jnp.zeros((n, D_MODEL), w_in_t.dtype)
    w_small_t = jnp.concatenate([w_in_t[0:384], z(128), w_in_t[384:672], z(96), w_in_t[5792:5856], z(64)], axis=0)
    w_big_t = jnp.concatenate([w_in_t[672:5792], w_in_t[5856:7904]], axis=0)
    return w_small_t, w_big_t


def _mla_up_weights(w_q_b_t, w_kv_b):
    wq = w_q_b_t.reshape(N_HEADS, QK_HEAD, Q_LORA)
    wq = jnp.pad(wq, ((0, 0), (0, LANES - QK_HEAD), (0, 0))).reshape(N_HEADS * LANES, Q_LORA)
    wkv = w_kv_b.reshape(KV_LORA, N_HEADS, QK_NOPE + V_HEAD)
    wk = jnp.pad(wkv[..., :QK_NOPE], ((0, 0), (0, 0), (0, LANES - QK_NOPE))).reshape(KV_LORA, N_HEADS * LANES)
    v = wkv[..., QK_NOPE:]
    zv = jnp.zeros_like(v)
    even = (jnp.arange(N_HEADS) % 2 == 0)[None, :, None]
    wv = jnp.where(even, jnp.concatenate([v, zv], -1), jnp.concatenate([zv, v], -1)).reshape(KV_LORA, N_HEADS * LANES)
    return wq, wk, wv


def _shard_rows(g):
    return g.reshape(N_DEV, g.shape[0] // N_DEV, g.shape[1])


def kernel(x, positions, ffn1_norm, ffn1_w_gate, ffn1_w_up, ffn1_w_down, mix_norm, w_in, q_a_norm, w_q_b, kv_a_norm, w_kv_b, q_head_norm, k_head_norm, conv_w, conv_b, a_log_fwd, a_log_bwd, dt_bias_fwd, dt_bias_bwd, d_skip, ssm_norm, w_attn_branch, w_ssm_branch, w_out, ffn2_norm, ffn2_w_gate, ffn2_w_up, ffn2_w_down, loss_target, m_ffn1_norm, m_ffn1_w_gate, m_ffn1_w_up, m_ffn1_w_down, m_mix_norm, m_w_in, m_q_a_norm, m_w_q_b, m_kv_a_norm, m_w_kv_b, m_q_head_norm, m_k_head_norm, m_conv_w, m_conv_b, m_a_log_fwd, m_a_log_bwd, m_dt_bias_fwd, m_dt_bias_bwd, m_d_skip, m_ssm_norm, m_w_attn_branch, m_w_ssm_branch, m_w_out, m_ffn2_norm, m_ffn2_w_gate, m_ffn2_w_up, m_ffn2_w_down, v_ffn1_norm, v_ffn1_w_gate, v_ffn1_w_up, v_ffn1_w_down, v_mix_norm, v_w_in, v_q_a_norm, v_w_q_b, v_kv_a_norm, v_w_kv_b, v_q_head_norm, v_k_head_norm, v_conv_w, v_conv_b, v_a_log_fwd, v_a_log_bwd, v_dt_bias_fwd, v_dt_bias_bwd, v_d_skip, v_ssm_norm, v_w_attn_branch, v_w_ssm_branch, v_w_out, v_ffn2_norm, v_ffn2_w_gate, v_ffn2_w_up, v_ffn2_w_down):
    w_all = dict(ffn1_norm=ffn1_norm, ffn1_w_gate=ffn1_w_gate, ffn1_w_up=ffn1_w_up, ffn1_w_down=ffn1_w_down, mix_norm=mix_norm, w_in=w_in, q_a_norm=q_a_norm, w_q_b=w_q_b, kv_a_norm=kv_a_norm, w_kv_b=w_kv_b, q_head_norm=q_head_norm, k_head_norm=k_head_norm, conv_w=conv_w, conv_b=conv_b, a_log_fwd=a_log_fwd, a_log_bwd=a_log_bwd, dt_bias_fwd=dt_bias_fwd, dt_bias_bwd=dt_bias_bwd, d_skip=d_skip, ssm_norm=ssm_norm, w_attn_branch=w_attn_branch, w_ssm_branch=w_ssm_branch, w_out=w_out, ffn2_norm=ffn2_norm, ffn2_w_gate=ffn2_w_gate, ffn2_w_up=ffn2_w_up, ffn2_w_down=ffn2_w_down)
    m_all = dict(ffn1_norm=m_ffn1_norm, ffn1_w_gate=m_ffn1_w_gate, ffn1_w_up=m_ffn1_w_up, ffn1_w_down=m_ffn1_w_down, mix_norm=m_mix_norm, w_in=m_w_in, q_a_norm=m_q_a_norm, w_q_b=m_w_q_b, kv_a_norm=m_kv_a_norm, w_kv_b=m_w_kv_b, q_head_norm=m_q_head_norm, k_head_norm=m_k_head_norm, conv_w=m_conv_w, conv_b=m_conv_b, a_log_fwd=m_a_log_fwd, a_log_bwd=m_a_log_bwd, dt_bias_fwd=m_dt_bias_fwd, dt_bias_bwd=m_dt_bias_bwd, d_skip=m_d_skip, ssm_norm=m_ssm_norm, w_attn_branch=m_w_attn_branch, w_ssm_branch=m_w_ssm_branch, w_out=m_w_out, ffn2_norm=m_ffn2_norm, ffn2_w_gate=m_ffn2_w_gate, ffn2_w_up=m_ffn2_w_up, ffn2_w_down=m_ffn2_w_down)
    v_all = dict(ffn1_norm=v_ffn1_norm, ffn1_w_gate=v_ffn1_w_gate, ffn1_w_up=v_ffn1_w_up, ffn1_w_down=v_ffn1_w_down, mix_norm=v_mix_norm, w_in=v_w_in, q_a_norm=v_q_a_norm, w_q_b=v_w_q_b, kv_a_norm=v_kv_a_norm, w_kv_b=v_w_kv_b, q_head_norm=v_q_head_norm, k_head_norm=v_k_head_norm, conv_w=v_conv_w, conv_b=v_conv_b, a_log_fwd=v_a_log_fwd, a_log_bwd=v_a_log_bwd, dt_bias_fwd=v_dt_bias_fwd, dt_bias_bwd=v_dt_bias_bwd, d_skip=v_d_skip, ssm_norm=v_ssm_norm, w_attn_branch=v_w_attn_branch, w_ssm_branch=v_w_ssm_branch, w_out=v_w_out, ffn2_norm=v_ffn2_norm, ffn2_w_gate=v_ffn2_w_gate, ffn2_w_up=v_ffn2_w_up, ffn2_w_down=v_ffn2_w_down)
    T = x.shape[1]
    xs_in, target = x[0], loss_target[0]
    def two_d(n, a):
        if n in TRANSPOSED:
            return jnp.swapaxes(a, 1, 2).reshape(a.shape[2], a.shape[1])
        return a.reshape(-1, a.shape[-1])

    w2 = {n: two_d(n, a) for n, a in w_all.items()}
    m2 = {n: two_d(n, a) for n, a in m_all.items()}
    v2 = {n: two_d(n, a) for n, a in v_all.items()}
    p = {n: w2[n] for n, _ in SMALL}
    bf = lambda n: w2[n].astype(BF16)

    first = ["ffn1_w_gate", "ffn1_w_up", "ffn1_w_down"]
    ge = dict(zip(first, _all_gather_two_level([bf(n) for n in first], name="gather_ffn1")))
    mixw = ["w_in", "w_q_b", "w_kv_b", "conv_w"]
    behind_first = ge["ffn1_w_down"][0, 0:1, 0:1].astype(F32) * 0.0
    mix_started, token = _exchange_start(["gather"] * len(mixw),
                                         [bf(n) for n in mixw[:3]] + [w2["conv_w"] + behind_first],
                                         name="gather_mix_start")
    ffn1_norm_f = p["ffn1_norm"] + token[0:1, 0:1]
    w_g1t, w_u1t = _rows(ge["ffn1_w_gate"]), _rows(ge["ffn1_w_up"])
    w_d1 = _rows(ge["ffn1_w_down"])
    late = ["w_attn_branch", "w_ssm_branch", "w_out", "ffn2_w_gate", "ffn2_w_up", "ffn2_w_down"]
    late_shards = [bf(n) for n in late]

    tabs = _rope_tables(positions, T)
    qg, kg = _pad_lanes(p["q_head_norm"]), _pad_lanes(p["k_head_norm"])
    bias128 = _pad_lanes(jnp.concatenate([p["dt_bias_fwd"], p["dt_bias_bwd"]], axis=1))
    alog128 = _pad_lanes(jnp.concatenate([p["a_log_fwd"], p["a_log_bwd"]], axis=1))
    skip_x = jnp.repeat(p["d_skip"], 64, axis=1)

    x1, ffn1_saved = _ffn_fwd(xs_in, ffn1_norm_f, w_g1t, w_u1t, w_d1, "ffn1")
    h2 = _rms_fwd(x1, p["mix_norm"], name="mix_rms")
    ge.update(zip(mixw, _exchange_wait(["gather"] * len(mixw), mix_started, h2, name="gather_mix_wait")))
    w_small_t, w_big_t = _in_proj_weights(_rows(ge["w_in"]))
    wq_t, wk, wv = _mla_up_weights(_rows(ge["w_q_b"]), _cols(ge["w_kv_b"]))
    conv_full = _cols(ge["conv_w"])
    u_big = _mm(h2, w_big_t, name="in_big", tb=True)
    u_small = _mm(h2, w_small_t, name="in_small", tb=True)
    cqn, ckvn = _lora_norm_fwd(u_small, p["q_a_norm"], p["kv_a_norm"], name="lora_norm")
    q_raw = _mm(cqn, wq_t, name="q_up", tb=True)
    k_raw = _mm(ckvn, wk, name="k_up")
    v = _mm(ckvn, wv, name="v_up", out_dtype=BF16)
    q, k = _qk_prep_fwd(q_raw, k_raw, u_small, tabs, qg, kg, name="qk_prep")
    a_out, lse, g_late = _attn_fwd(q, k, v, ["gather"] * len(late), late_shards, name="attn_fwd")
    gl = dict(zip(late, g_late))
    w_pa, w_pb, w_o = _rows(gl["w_attn_branch"]), _rows(gl["w_ssm_branch"]), _rows(gl["w_out"])
    w_g2t, w_u2t = _rows(gl["ffn2_w_gate"]), _rows(gl["ffn2_w_up"])
    w_d2 = _rows(gl["ffn2_w_down"])
    xbc_act = _conv_fwd(u_big, conv_full, p["conv_b"], name="conv_fwd")
    y_f, hin_f = _ssd_fwd(xbc_act, u_small, bias128, alog128, rev=False, name="ssd_fwd_f")
    y_b, hin_b = _ssd_fwd(xbc_act, u_small, bias128, alog128, rev=True, name="ssd_fwd_b")
    m_out = _ssm_out_fwd(y_f, y_b, xbc_act, u_big, skip_x, p["ssm_norm"], name="ssm_out")
    pa = _mm(a_out, w_pa, name="branch_a")
    pb = _mm(m_out, w_pb, name="branch_b")
    merged = _merge_fwd(pa, pb, u_big, name="merge")
    x2 = _mm(merged, w_o, name="mix_out", res=x1)
    y, ffn2_saved = _ffn_fwd(x2, p["ffn2_norm"], w_g2t, w_u2t, w_d2, "ffn2")
    dy, loss_part = _loss_bwd(y, target, name="loss")
    loss = lax.psum(loss_part, ("x", "y", "c"))

    gs = {}
    dx2, gs["ffn2_norm"], g_gate2, g_up2, g_down2 = _ffn_bwd(dy, x2, p["ffn2_norm"], w_g2t, w_u2t, w_d2, ffn2_saved,
                                                             "ffn2b")
    dmerged = _mm(dx2, w_o, name="d_merged", tb=True)
    g_out = _mm(merged, dx2, name="d_w_out", ta=True, out_dtype=BF16)
    dpa, dpb, dga, dgb = _merge_bwd(dmerged, pa, pb, u_big, name="d_merge")
    g_pa = _mm(a_out, dpa, name="d_w_pa", ta=True, out_dtype=BF16)
    g_pb = _mm(m_out, dpb, name="d_w_pb", ta=True, out_dtype=BF16)
    da_out = _mm(dpa, w_pa, name="d_a", tb=True)
    dm_out = _mm(dpb, w_pb, name="d_m", tb=True)
    late_grads = [_shard_rows(g) for g in (g_pa, g_pb, g_out, g_gate2, g_up2, g_down2)]
    dq, dk, dv, r_late = _attn_bwd(q, k, v, a_out, lse, da_out, ["scatter"] * len(late_grads), late_grads,
                                   name="attn_bwd")
    recv = dict(zip(late, r_late))

    dyss, dz, gs["ssm_norm"], dskip_ch = _ssm_out_bwd(dm_out, y_f, y_b, xbc_act, u_big, skip_x, p["ssm_norm"],
                                                      name="d_ssm_out")
    dact_f, draw_f, dalog_f, dbias_f = _ssd_bwd(dyss, xbc_act, u_small, bias128, alog128, hin_f, skip_x,
                                                rev=False, name="ssd_bwd_f")
    dact_b, draw_b, dalog_b, dbias_b = _ssd_bwd(dyss, xbc_act, u_small, bias128, alog128, hin_b, None,
                                                rev=True, name="ssd_bwd_b")
    dxbc, g_conv, gs["conv_b"] = _conv_bwd(dact_f, dact_b, u_big, conv_full, p["conv_b"], name="conv_bwd")

    dq_raw, dk_raw, dkpe, gqh, gkh = _qk_prep_bwd(dq, dk, q_raw, k_raw, u_small, tabs, qg, kg, name="d_qk_prep")
    g_wq_t = _mm(dq_raw, cqn, name="d_w_q", ta=True, out_dtype=BF16)
    g_wk_t = _mm(dk_raw, ckvn, name="d_w_k", ta=True, out_dtype=BF16)
    g_wv_t = _mm(dv, ckvn, name="d_w_v", ta=True, out_dtype=BF16)
    dcqn = _mm(dq_raw, wq_t, name="d_cqn")
    dckvn = _mm(dk_raw, wk, name="d_ckvn_k", tb=True)
    dckvn = _mm(dv, wv, name="d_ckvn_v", tb=True, res=dckvn)
    du_small, gs["q_a_norm"], gkv = _lora_norm_bwd(dcqn, dckvn, u_small, p["q_a_norm"], p["kv_a_norm"], dkpe,
                                                   draw_f, draw_b, name="d_lora_norm")

    dh2 = _mm(du_small, w_small_t, name="d_h2_small")
    dh2 = _mm(dz, w_big_t, name="d_h2_z", b_row0=0, res=dh2)
    dh2 = _mm(dxbc, w_big_t, name="d_h2_xbc", b_row0=2048, res=dh2)
    dh2 = _mm(dga, w_big_t, name="d_h2_ga", b_row0=5120, res=dh2)
    dh2 = _mm(dgb, w_big_t, name="d_h2_gb", b_row0=6144, res=dh2)
    gt_small = _mm(du_small, h2, name="d_w_small", ta=True, out_dtype=BF16)
    gt_z = _mm(dz, h2, name="d_w_z", ta=True, out_dtype=BF16)
    gt_xbc = _mm(dxbc, h2, name="d_w_xbc", ta=True, out_dtype=BF16)
    gt_ga = _mm(dga, h2, name="d_w_ga", ta=True, out_dtype=BF16)
    gt_gb = _mm(dgb, h2, name="d_w_gb", ta=True, out_dtype=BF16)
    dx1, gs["mix_norm"] = _rms_bwd(dh2, x1, p["mix_norm"], dx2, name="d_mix_rms")

    gt_in = jnp.concatenate([gt_small[0:384], gt_small[U_CKV:U_KPE + QK_ROPE], gt_z, gt_xbc,
                             gt_small[U_DT:U_DT + 64], gt_ga, gt_gb], axis=0)
    gt_in = jnp.pad(gt_in.reshape(N_DEV, W_IN_SHARD, D_MODEL), ((0, 0), (0, W_IN_SHARD_PAD - W_IN_SHARD), (0, 0)))
    gt_q = g_wq_t.reshape(N_HEADS, LANES, Q_LORA)[:, :QK_HEAD].reshape(N_DEV, -1, Q_LORA)
    gk3 = g_wk_t.reshape(N_HEADS, LANES, KV_LORA)[:, :QK_NOPE]
    gv3 = g_wv_t.reshape(N_HEADS, LANES, KV_LORA)
    even = (jnp.arange(N_HEADS) % 2 == 0)[:, None, None]
    gv3 = jnp.where(even, gv3[:, :V_HEAD], gv3[:, V_HEAD:])
    gt_kv = jnp.concatenate([gk3, gv3], axis=1).reshape(N_DEV, -1, KV_LORA)
    mixg = ["w_in", "w_q_b", "w_kv_b"]
    grads_started, token = _exchange_start(["scatter"] * len(mixg), [gt_in, gt_q, gt_kv], name="grad_mix_start")
    ffn1_sent = []

    def send(names, grads):
        st, tok = _exchange_start(["scatter"] * len(grads), [_shard_rows(g) for g in grads],
                                  name="grad_ffn1_" + "_".join(names) + "_start")
        ffn1_sent.append((names, st))
        return tok

    grad_x, gs["ffn1_norm"], _, _, _ = _ffn_bwd(dx1, xs_in, p["ffn1_norm"] + token[0:1, 0:1], w_g1t, w_u1t, w_d1,
                                                ffn1_saved, "ffn1b", send=send)
    recv.update(zip(mixg, _exchange_wait(["scatter"] * len(mixg), grads_started, grad_x, name="grad_mix_wait")))
    for names, st in ffn1_sent:
        got = _exchange_wait(["scatter"] * len(names), st, grad_x, name="grad_ffn1_" + "_".join(names) + "_wait")
        recv.update(zip(["ffn1_w_" + n for n in names], got))

    gsmall = _small_slab(gs, dskip_ch, dalog_f, dalog_b, dbias_f, dbias_b, gkv, gqh, gkh, g_conv, name="small_slab")
    srecv = _exchange(["gather"], [gsmall], name="grad_exchange")[0]

    out = {}
    for n in ("ffn1_w_down", "ffn2_w_down", "w_attn_branch", "w_ssm_branch", "w_out", "ffn1_w_gate", "ffn1_w_up",
              "ffn2_w_gate", "ffn2_w_up", "w_q_b"):
        out[n] = _reduce_adamw(recv[n], w2[n], m2[n], v2[n], name=f"adamw_{n}")
    out["w_kv_b"] = _reduce_t_adamw(recv["w_kv_b"], w2["w_kv_b"], m2["w_kv_b"], v2["w_kv_b"], name="adamw_w_kv_b")
    g_in = _reduce8(recv["w_in"], name="sum_w_in", tile=W_IN_SHARD_PAD // 2)[:W_IN_SHARD]
    out["w_in"] = [g_in] + list(_adamw(g_in, w2["w_in"], m2["w_in"], v2["w_in"], name="adamw_w_in"))
    me = 4 * lax.axis_index("x") + 2 * lax.axis_index("y") + lax.axis_index("c")
    conv_g = lax.dynamic_slice(srecv, (0, CONV_ROW, me * (XBC_DIM // N_DEV)), (N_DEV, CONV_WIDTH, XBC_DIM // N_DEV))
    sn = [n for n, _ in SMALL] + ["conv_w"]
    sg, sd, sm, sv = _adamw_small(srecv, conv_g, [w2[n] for n in sn], [m2[n] for n in sn], [v2[n] for n in sn],
                                  name="adamw_small")
    for i, n in enumerate(sn):
        out[n] = (sg[i], sd[i], sm[i], sv[i])
    def back(n, a):
        if n in TRANSPOSED:
            return jnp.swapaxes(a.reshape(1, a.shape[0], a.shape[1]), 1, 2)
        return a.reshape(w_all[n].shape)

    outs = [[back(n, out[n][kind]) for n in WEIGHT_ORDER] for kind in range(4)]
    return (loss, grad_x[None], *outs[0], *outs[1], *outs[2], *outs[3])
```

```python
import math

import jax
import jax.numpy as jnp
from jax import lax
from jax.experimental import pallas as pl
from jax.experimental.pallas import tpu as pltpu

F32, BF16 = jnp.float32, jnp.bfloat16
HIGHEST = lax.Precision.HIGHEST

D_MODEL, D_FF = 1024, 2816
EPS = 1e-6
N_HEADS, QK_NOPE, QK_ROPE, QK_HEAD, V_HEAD = 16, 64, 32, 96, 64
Q_LORA, KV_LORA = 384, 256
ROPE_BASE = 10000.0
D_INNER, SSM_HEADS, SSM_GROUPS, D_STATE, CONV_WIDTH, CHUNK = 2048, 32, 4, 128, 5, 128
XBC_DIM = D_INNER + 2 * SSM_GROUPS * D_STATE
IN_DIM = 7904
ADAM_LR, ADAM_B1, ADAM_B2, ADAM_EPS, ADAM_WD, ADAM_STEP = 0.001, 0.9, 0.999, 1e-08, 0.01, 10
N_DEV = 8

V7X_VMEM_BYTES = 64 * 1024 * 1024
VMEM_LIMIT = V7X_VMEM_BYTES - 8 * 1024 * 1024
LANES = 128
W_IN_SHARD = IN_DIM // N_DEV
W_IN_SHARD_PAD = 992

SMALL = (
    ("ffn1_norm", 1024), ("mix_norm", 1024), ("q_a_norm", 384), ("kv_a_norm", 256), ("q_head_norm", 96),
    ("k_head_norm", 96), ("conv_b", 3072), ("a_log_fwd", 32), ("a_log_bwd", 32), ("dt_bias_fwd", 32),
    ("dt_bias_bwd", 32), ("d_skip", 32), ("ssm_norm", 2048), ("ffn2_norm", 1024),
)
TRANSPOSED = ("ffn1_w_gate", "ffn1_w_up", "ffn2_w_gate", "ffn2_w_up", "w_in", "w_q_b")
SMALL_ROW = {n: i for i, (n, _) in enumerate(SMALL)}
CONV_ROW = len(SMALL)
SMALL_ROWS, SMALL_COLS = 24, XBC_DIM
WEIGHT_ORDER = (
    "ffn1_norm", "ffn1_w_gate", "ffn1_w_up", "ffn1_w_down", "mix_norm", "w_in", "q_a_norm", "w_q_b", "kv_a_norm",
    "w_kv_b", "q_head_norm", "k_head_norm", "conv_w", "conv_b", "a_log_fwd", "a_log_bwd", "dt_bias_fwd", "dt_bias_bwd",
    "d_skip", "ssm_norm", "w_attn_branch", "w_ssm_branch", "w_out", "ffn2_norm", "ffn2_w_gate", "ffn2_w_up",
    "ffn2_w_down",
)


def _pallas(body, **kw):
    return pl.pallas_call(body, **kw)


def _params(sem):
    return pltpu.CompilerParams(dimension_semantics=sem, vmem_limit_bytes=VMEM_LIMIT)


def _pick(dim, pref):
    if dim <= pref:
        return dim
    c = (pref // LANES) * LANES
    while c >= LANES:
        if dim % c == 0:
            return c
        c -= LANES
    raise ValueError((dim, pref))


def _sigmoid(x):
    return 1.0 / (1.0 + jnp.exp(-x))


def _softplus(x):
    return jnp.maximum(x, 0.0) + jnp.log(1.0 + jnp.exp(-jnp.abs(x)))


def _dot(a, b):
    return jnp.dot(a, b, preferred_element_type=F32)


def _dot_nt(a, b):
    return lax.dot_general(a, b, (((1,), (1,)), ((), ())), preferred_element_type=F32)


def _dot_tn(a, b):
    return lax.dot_general(a, b, (((0,), (0,)), ((), ())), preferred_element_type=F32)


def _dot_h(a, b):
    return jnp.dot(a, b, preferred_element_type=F32, precision=HIGHEST)


def _dot_h_nt(a, b):
    return lax.dot_general(a, b, (((1,), (1,)), ((), ())), preferred_element_type=F32, precision=HIGHEST)


def _dot_h_tn(a, b):
    return lax.dot_general(a, b, (((0,), (0,)), ((), ())), preferred_element_type=F32, precision=HIGHEST)


def _mesh_pos():
    return lax.axis_index("x"), lax.axis_index("y"), lax.axis_index("c")


def _comm_scratch(n):
    return [pltpu.SemaphoreType.DMA((7 * n,)), pltpu.SemaphoreType.DMA((7 * n,)), pltpu.SemaphoreType.DMA((n,))]


def _comm_copies(modes, srcs, dsts, send_sems, recv_sems, local_sems, arrivals):
    x, y, c = _mesh_pos()
    me = 4 * x + 2 * y + c
    local, remote = [], []
    for w, (mode, s, d) in enumerate(zip(modes, srcs, dsts)):
        gather = mode == "gather"
        if not arrivals and local_sems is not None:
            local.append(pltpu.make_async_copy(s if gather else s.at[me], d.at[me], local_sems.at[w]))
        for k in range(1, N_DEV):
            px = (1 - x) if (k & 4) else x
            py = (1 - y) if (k & 2) else y
            pc = (1 - c) if (k & 1) else c
            peer = 4 * px + 2 * py + pc
            idx = 7 * w + k - 1
            remote.append(pltpu.make_async_remote_copy(
                src_ref=s if gather else s.at[peer], dst_ref=d.at[peer] if arrivals else d.at[me],
                send_sem=send_sems.at[idx], recv_sem=recv_sems.at[idx],
                device_id=(px, py, pc), device_id_type=pl.DeviceIdType.MESH))
    return local, remote


def _comm_start(modes, srcs, dsts, sems):
    local, sends = _comm_copies(modes, srcs, dsts, *sems, arrivals=False)
    for cp in local + sends:
        cp.start()


def _comm_wait(modes, srcs, dsts, sems):
    _, recvs = _comm_copies(modes, srcs, dsts, *sems, arrivals=True)
    for cp in recvs:
        cp.wait_recv()
    local, sends = _comm_copies(modes, srcs, dsts, *sems, arrivals=False)
    for cp in sends:
        cp.wait_send()
    for cp in local:
        cp.wait()


def _comm_out_shapes(modes, arrays):
    return [jax.ShapeDtypeStruct((N_DEV,) + (a.shape if m == "gather" else a.shape[1:]), a.dtype)
            for m, a in zip(modes, arrays)]


def _exchange(modes, arrays, *, name):
    n = len(arrays)

    def body(*refs):
        srcs, dsts, sems = refs[:n], refs[n:2 * n], refs[2 * n:]
        _comm_start(modes, srcs, dsts, sems)
        _comm_wait(modes, srcs, dsts, sems)

    any_spec = pl.BlockSpec(memory_space=pl.ANY)
    return _pallas(body, name=name, out_shape=_comm_out_shapes(modes, arrays), in_specs=[any_spec] * n,
                   out_specs=[any_spec] * n, scratch_shapes=_comm_scratch(n))(*arrays)


def _exchange_start(modes, arrays, *, name):
    n = len(arrays)
    me = 4 * lax.axis_index("x") + 2 * lax.axis_index("y") + lax.axis_index("c")
    lands = []
    for m, a in zip(modes, arrays):
        own = a if m == "gather" else lax.dynamic_index_in_dim(a, me, 0, keepdims=False)
        zone = lax.empty((N_DEV,) + own.shape, a.dtype)
        lands.append(lax.dynamic_update_index_in_dim(zone, own, me, 0))

    def body(*refs):
        srcs, dsts = refs[:n], refs[n:2 * n]
        send_sems, recv_sems = refs[2 * n], refs[2 * n + 1]
        token = refs[-1]
        _, sends = _comm_copies(modes, srcs, dsts, send_sems, recv_sems, None, arrivals=False)
        for cp in sends:
            cp.start()
        token[...] = jnp.zeros_like(token)

    hbm = pl.BlockSpec(memory_space=pltpu.HBM)
    sem = pl.BlockSpec(memory_space=pltpu.SEMAPHORE)
    ins = [pltpu.with_memory_space_constraint(a, pltpu.HBM) for a in list(arrays) + lands]
    got = _pallas(
        body, name=name,
        out_shape=(pltpu.SemaphoreType.DMA((7 * n,)), pltpu.SemaphoreType.DMA((7 * n,)),
                   *[pltpu.HBM(a.shape, a.dtype) for a in ins], jax.ShapeDtypeStruct((8, LANES), F32)),
        in_specs=[hbm] * (2 * n), out_specs=(sem, sem, *[hbm] * (2 * n), pl.BlockSpec(memory_space=pltpu.VMEM)),
        input_output_aliases={i: 2 + i for i in range(2 * n)},
        compiler_params=pltpu.CompilerParams(has_side_effects=pltpu.SideEffectType.DATAFLOW_SIDE_EFFECTING),
    )(*ins)
    return (got[0], got[1], got[2:2 + n], got[2 + n:2 + 2 * n]), got[-1]


def _exchange_wait(modes, started, after, *, name):
    send_sems, recv_sems, srcs, lands = started
    n = len(srcs)

    def body(*refs):
        src_refs, dst_refs = refs[:n], refs[n:2 * n]
        ssem, rsem = refs[2 * n], refs[2 * n + 1]
        _, recvs = _comm_copies(modes, src_refs, dst_refs, ssem, rsem, None, arrivals=True)
        for cp in recvs:
            cp.wait_recv()
        _, sends = _comm_copies(modes, src_refs, dst_refs, ssem, rsem, None, arrivals=False)
        for cp in sends:
            cp.wait_send()

    hbm = pl.BlockSpec(memory_space=pltpu.HBM)
    sem = pl.BlockSpec(memory_space=pltpu.SEMAPHORE)
    both = list(srcs) + list(lands)
    got = _pallas(
        body, name=name, out_shape=tuple(pltpu.HBM(a.shape, a.dtype) for a in both),
        in_specs=[hbm] * (2 * n) + [sem, sem, pl.BlockSpec(memory_space=pl.ANY)], out_specs=tuple([hbm] * (2 * n)),
        input_output_aliases={i: i for i in range(2 * n)},
        compiler_params=pltpu.CompilerParams(has_side_effects=pltpu.SideEffectType.DATAFLOW_SIDE_EFFECTING),
    )(*both, send_sems, recv_sems, after)
    return got[n:]


def _all_gather_two_level(shards, *, name):
    n = len(shards)

    def body(*refs):
        srcs, outs = refs[:n], refs[n:2 * n]
        send_sems, recv_sems, local_sems = refs[2 * n:]
        x, y, c = _mesh_pos()
        me, sibling = (x, y, c), (x, y, 1 - c)
        chips = [(1 - x, y), (x, 1 - y), (1 - x, 1 - y)]

        def blk(w, px, py, pc):
            return outs[w].at[4 * px + 2 * py + pc]

        def copy(w, k, block, to, src=None):
            return pltpu.make_async_remote_copy(
                src_ref=blk(w, *block) if src is None else src, dst_ref=blk(w, *block),
                send_sem=send_sems.at[7 * w + k], recv_sem=recv_sems.at[7 * w + k], device_id=to,
                device_id_type=pl.DeviceIdType.MESH)

        mine = [pltpu.make_async_copy(srcs[w], blk(w, *me), local_sems.at[w]) for w in range(n)]
        for cp in mine:
            cp.start()
        first = []
        for w in range(n):
            first.append(copy(w, 0, me, sibling, src=srcs[w]))
            first += [copy(w, 1 + j, me, (*chip, c), src=srcs[w]) for j, chip in enumerate(chips)]
        for cp in first:
            cp.start()
        passed = []
        for w in range(n):
            for j, chip in enumerate(chips):
                copy(w, 1 + j, (*chip, c), me).wait_recv()
                fwd = copy(w, 4 + j, (*chip, c), sibling)
                fwd.start()
                passed.append(fwd)
        for w in range(n):
            copy(w, 0, sibling, me).wait_recv()
            for j, chip in enumerate(chips):
                copy(w, 4 + j, (*chip, 1 - c), me).wait_recv()
        for cp in first + passed:
            cp.wait_send()
        for cp in mine:
            cp.wait()

    any_spec = pl.BlockSpec(memory_space=pl.ANY)
    return _pallas(body, name=name, out_shape=_comm_out_shapes(["gather"] * n, shards), in_specs=[any_spec] * n,
                   out_specs=[any_spec] * n, scratch_shapes=_comm_scratch(n))(*shards)


def _mm(a, b, *, name, ta=False, tb=False, out_dtype=F32, alpha=1.0, res=None, tm=1024, tn=1408, tk=1408,
        b_row0=None, after=None):
    (K, M) = a.shape if ta else a.shape[::-1]
    (N, Kb) = b.shape if tb else b.shape[::-1]
    tm, tn, tk = _pick(M, tm), _pick(N, tn), _pick(K, tk)
    nk = K // tk
    if b_row0 is None:
        assert K == Kb, (a.shape, b.shape, ta, tb)
        kb0 = 0
    else:
        assert not tb and b_row0 % tk == 0 and b_row0 + K <= Kb, (a.shape, b.shape, b_row0)
        kb0 = b_row0 // tk
    a_spec = pl.BlockSpec((tk, tm), lambda i, j, k: (k, i)) if ta else pl.BlockSpec((tm, tk), lambda i, j, k: (i, k))
    b_spec = (pl.BlockSpec((tn, tk), lambda i, j, k: (j, k)) if tb
              else pl.BlockSpec((tk, tn), lambda i, j, k: (k + kb0, j)))
    o_spec = pl.BlockSpec((tm, tn), lambda i, j, k: (i, j))
    dn = (((0 if ta else 1,), (1 if tb else 0,)), ((), ()))
    has_res = res is not None
    n_in = 2 + has_res + (after is not None)

    def body(*refs):
        a_ref, b_ref = refs[0], refs[1]
        r_ref = refs[2] if has_res else None
        o_ref = refs[n_in]
        part =lax.dot_general(a_ref[...].astype(BF16), b_ref[...].astype(BF16), dn, preferred_element_type=F32)

        def finish(acc):
            if alpha != 1.0:
                acc = acc * alpha
            if has_res:
                acc = acc + r_ref[...]
            o_ref[...] = acc.astype(o_ref.dtype)

        if nk == 1:
            finish(part)
        else:
            acc_ref = refs[-1]
            k = pl.program_id(2)

            @pl.when(k == 0)
            def _():
                acc_ref[...] = part

            @pl.when(k > 0)
            def _():
                acc_ref[...] += part

            @pl.when(k == nk - 1)
            def _():
                finish(acc_ref[...])

    ins = [a, b] + ([res] if has_res else [])
    in_specs = [a_spec, b_spec] + ([o_spec] if has_res else [])
    if after is not None:
        ins.append(after)
        in_specs.append(pl.BlockSpec(after.shape, lambda i, j, k: (0, 0)))
    return _pallas(
        body, name=name, grid=(M // tm, N // tn, nk), in_specs=in_specs, out_specs=o_spec,
        out_shape=jax.ShapeDtypeStruct((M, N), out_dtype),
        scratch_shapes=[pltpu.VMEM((tm, tn), F32)] if nk > 1 else [],
        compiler_params=_params(("parallel", "parallel", "arbitrary")),
    )(*ins)


def _mm_swiglu(h, w_cat_t, *, name, tm=512):
    M, K = h.shape
    hw = D_FF // 2
    tm = min(tm, M)

    def body(a_ref, b_ref, gu_ref, act_ref):
        r = lax.dot_general(a_ref[...], b_ref[...], (((1,), (1,)), ((), ())), preferred_element_type=F32)
        gu_ref[...] = r.astype(BF16)
        g, u = r[:, :hw], r[:, hw:]
        act_ref[...] = (g * _sigmoid(g) * u).astype(BF16)

    return _pallas(
        body, name=name, grid=(2, M // tm),
        in_specs=[pl.BlockSpec((tm, K), lambda j, i: (i, 0)), pl.BlockSpec((2 * hw, K), lambda j, i: (j, 0))],
        out_specs=[pl.BlockSpec((tm, 2 * hw), lambda j, i: (i, j)), pl.BlockSpec((tm, hw), lambda j, i: (i, j))],
        out_shape=[jax.ShapeDtypeStruct((M, 2 * D_FF), BF16), jax.ShapeDtypeStruct((M, D_FF), BF16)],
        compiler_params=_params(("parallel", "parallel")),
    )(h, w_cat_t)


def _col0(j):
    return 0


def _colj(j):
    return j


def _rowmap(fn, *, name, rows, tile, ins, consts=(), outs=(), accs=(), ncol=1):
    tile = min(tile, rows)
    nrow = rows // tile
    in_specs = [pl.BlockSpec((tile, w), lambda j, i, f=f: (i, f(j))) for _, w, f in ins]
    for arr, w, f in consts:
        in_specs.append(pl.BlockSpec((arr.shape[0], w), lambda j, i, f=f: (0, f(j))))
    out_specs = [pl.BlockSpec((tile, w), lambda j, i, f=f: (i, f(j))) for _, _, w, f in outs]
    out_specs += [pl.BlockSpec((1, w), lambda j, i, f=f: (0, f(j))) for _, w, f in accs]
    out_shape = [jax.ShapeDtypeStruct((rows, c), dt) for c, dt, _, _ in outs]
    out_shape += [jax.ShapeDtypeStruct((1, c), F32) for c, _, _ in accs]
    n_in, n_out = len(ins) + len(consts), len(outs)
    acc_fixed = [f is _col0 for _, _, f in accs]

    def body(*refs):
        res = fn(*[r[...].astype(F32) for r in refs[:n_in]])
        if not isinstance(res, (tuple, list)):
            res = (res,)
        for r, v in zip(refs[n_in:n_in + n_out], res[:n_out]):
            r[...] = v.astype(r.dtype)
        j, i = pl.program_id(0), pl.program_id(1)
        for r, v, fixed in zip(refs[n_in + n_out:], res[n_out:], acc_fixed):
            first = ((i == 0) & (j == 0)) if fixed else (i == 0)

            @pl.when(first)
            def _(r=r, v=v):
                r[...] = v

            @pl.when(jnp.logical_not(first))
            def _(r=r, v=v):
                r[...] += v

    arrays = [a for a, _, _ in ins] + [a for a, _, _ in consts]
    return _pallas(
        body, name=name, grid=(ncol, nrow), in_specs=in_specs, out_specs=out_specs, out_shape=out_shape,
        compiler_params=_params(("arbitrary", "arbitrary")),
    )(*arrays)


def _rms_fwd(x, g, *, name, tile=512):
    rows, d = x.shape

    def fn(xv, gv):
        r = lax.rsqrt(jnp.mean(xv * xv, axis=-1, keepdims=True) + EPS)
        return xv * r * gv

    return _rowmap(fn, name=name, rows=rows, tile=tile, ins=[(x, d, _col0)], consts=[(g, d, _col0)],
                   outs=[(d, BF16, d, _col0)])[0]


def _rms_bwd(dh, x, g, res, *, name, tile=512):
    rows, d = x.shape

    def fn(dhv, xv, rv, gv):
        r = lax.rsqrt(jnp.mean(xv * xv, axis=-1, keepdims=True) + EPS)
        xh = xv * r
        dxh = dhv * gv
        dx = r * (dxh - xh * jnp.mean(dxh * xh, axis=-1, keepdims=True))
        return rv + dx, jnp.sum(dhv * xh, axis=0, keepdims=True)

    return _rowmap(fn, name=name, rows=rows, tile=tile, ins=[(dh, d, _col0), (x, d, _col0), (res, d, _col0)],
                   consts=[(g, d, _col0)], outs=[(d, F32, d, _col0)], accs=[(d, d, _col0)])


def _swiglu_fwd(gu, *, name, tile=512):
    rows = gu.shape[0]
    w = _pick(D_FF, 1408)
    nb = D_FF // w

    def fn(gv, uv):
        return gv * _sigmoid(gv) * uv

    return _rowmap(fn, name=name, rows=rows, tile=tile, ncol=nb,
                   ins=[(gu, w, _colj), (gu, w, lambda j: j + nb)], outs=[(D_FF, BF16, w, _colj)])[0]


def _swiglu_bwd(da, gu, *, name, tile=512):
    rows = gu.shape[0]
    w = _pick(D_FF, 1408)
    nb = D_FF // w

    def fn(dav, gv, uv):
        sg = _sigmoid(gv)
        dg = dav * uv * (sg * (1.0 + gv * (1.0 - sg)))
        du = dav * (gv * sg)
        return dg, du

    return _rowmap(fn, name=name, rows=rows, tile=tile, ncol=nb,
                   ins=[(da, w, _colj), (gu, w, lambda j: 2 * j), (gu, w, lambda j: 2 * j + 1)],
                   outs=[(D_FF, BF16, w, _colj), (D_FF, BF16, w, _colj)])


U_CKV, U_KPE, U_DT = 512, 768, 896


def _lora_norm_fwd(u_small, qg, kvg, *, name, tile=512):
    rows = u_small.shape[0]

    def fn(cq, ckv, qgv, kgv):
        rq = lax.rsqrt(jnp.mean(cq * cq, axis=-1, keepdims=True) + EPS)
        rk = lax.rsqrt(jnp.mean(ckv * ckv, axis=-1, keepdims=True) + EPS)
        return cq * rq * qgv, ckv * rk * kgv

    return _rowmap(fn, name=name, rows=rows, tile=tile,
                   ins=[(u_small, Q_LORA, _col0), (u_small, KV_LORA, lambda j: U_CKV // KV_LORA)],
                   consts=[(qg, Q_LORA, _col0), (kvg, KV_LORA, _col0)],
                   outs=[(Q_LORA, BF16, Q_LORA, _col0), (KV_LORA, BF16, KV_LORA, _col0)])


def _lora_norm_bwd(dcqn, dckvn, u_small, qg, kvg, dkpe, draw_f, draw_b, *, name, tile=512):
    rows = u_small.shape[0]
    tile = min(tile, rows)

    def body(dq_ref, dk_ref, u_ref, dkp_ref, df_ref, db_ref, qg_ref, kg_ref, du_ref, gq_ref, gk_ref):
        cq, ckv = u_ref[:, 0:Q_LORA], u_ref[:, U_CKV:U_CKV + KV_LORA]
        dq, dk = dq_ref[...], dk_ref[...]
        rq = lax.rsqrt(jnp.mean(cq * cq, axis=-1, keepdims=True) + EPS)
        xh = cq * rq
        dxh = dq * qg_ref[...]
        du_ref[:, 0:Q_LORA] = (rq * (dxh - xh * jnp.mean(dxh * xh, axis=-1, keepdims=True))).astype(BF16)
        du_ref[:, Q_LORA:U_CKV] = jnp.zeros((tile, U_CKV - Q_LORA), BF16)
        rk = lax.rsqrt(jnp.mean(ckv * ckv, axis=-1, keepdims=True) + EPS)
        kh = ckv * rk
        dkh = dk * kg_ref[...]
        du_ref[:, U_CKV:U_KPE] = (rk * (dkh - kh * jnp.mean(dkh * kh, axis=-1, keepdims=True))).astype(BF16)
        du_ref[:, U_KPE:U_DT] = dkp_ref[...].astype(BF16)
        du_ref[:, U_DT:U_DT + LANES] = (df_ref[...] + db_ref[...]).astype(BF16)
        gq = jnp.sum(dq * xh, axis=0, keepdims=True)
        gk = jnp.sum(dk * kh, axis=0, keepdims=True)
        i = pl.program_id(0)

        @pl.when(i == 0)
        def _():
            gq_ref[...] = gq
            gk_ref[...] = gk

        @pl.when(i > 0)
        def _():
            gq_ref[...] += gq
            gk_ref[...] += gk

    def rowblk(w):
        return pl.BlockSpec((tile, w), lambda i: (i, 0))

    def whole(w):
        return pl.BlockSpec((1, w), lambda i: (0, 0))

    return _pallas(
        body, name=name, grid=(rows // tile,),
        in_specs=[rowblk(Q_LORA), rowblk(KV_LORA), rowblk(1024), rowblk(LANES), rowblk(LANES), rowblk(LANES),
                  whole(Q_LORA), whole(KV_LORA)],
        out_specs=[rowblk(1024), whole(Q_LORA), whole(KV_LORA)],
        out_shape=[jax.ShapeDtypeStruct((rows, 1024), BF16), jax.ShapeDtypeStruct((1, Q_LORA), F32),
                   jax.ShapeDtypeStruct((1, KV_LORA), F32)],
        compiler_params=_params(("arbitrary",)),
    )(dcqn, dckvn, u_small, dkpe, draw_f, draw_b, qg, kvg)


def _rope(x, c, s1, s2):
    return x * c + pltpu.roll(x, 112, 1) * s1 + pltpu.roll(x, 16, 1) * s2


def _rope_t(d, c, s1, s2):
    return d * c + pltpu.roll(d * s1, 16, 1) + pltpu.roll(d * s2, 112, 1)


def _qk_prep_fwd(q_raw, k_raw, u_small, tabs, qg, kg, *, name, tile=256):
    rows = q_raw.shape[0]
    tile = min(tile, rows)
    scale = 1.0 / math.sqrt(QK_HEAD)

    def body(q_ref, k_ref, u_ref, c_ref, s1_ref, s2_ref, qg_ref, kg_ref, qo_ref, ko_ref):
        c, s1, s2 = c_ref[...], s1_ref[...], s2_ref[...]
        qgv, kgv = qg_ref[...], kg_ref[...]
        kpe = pltpu.roll(u_ref[:, U_KPE:U_KPE + LANES], 64, 1)
        for h in range(N_HEADS):
            hs = slice(h * LANES, (h + 1) * LANES)
            qr = q_ref[:, hs]
            rq = lax.rsqrt(jnp.sum(qr * qr, axis=-1, keepdims=True) / QK_HEAD + EPS)
            qo_ref[:, hs] = (_rope(qr * rq * qgv, c, s1, s2) * scale).astype(BF16)
            xk = k_ref[:, hs] + kpe
            rk = lax.rsqrt(jnp.sum(xk * xk, axis=-1, keepdims=True) / QK_HEAD + EPS)
            ko_ref[:, hs] = _rope(xk * rk * kgv, c, s1, s2).astype(BF16)

    wide = pl.BlockSpec((tile, 2048), lambda i: (i, 0))
    narrow = pl.BlockSpec((tile, LANES), lambda i: (i, 0))
    gain = pl.BlockSpec((1, LANES), lambda i: (0, 0))
    return _pallas(
        body, name=name, grid=(rows // tile,),
        in_specs=[wide, wide, pl.BlockSpec((tile, 1024), lambda i: (i, 0)), narrow, narrow, narrow, gain, gain],
        out_specs=[wide, wide], out_shape=[jax.ShapeDtypeStruct((rows, 2048), BF16)] * 2,
        compiler_params=_params(("parallel",)),
    )(q_raw, k_raw, u_small, *tabs, qg, kg)


def _qk_prep_bwd(dq, dk, q_raw, k_raw, u_small, tabs, qg, kg, *, name, tile=256):
    rows = q_raw.shape[0]
    tile = min(tile, rows)
    scale = 1.0 / math.sqrt(QK_HEAD)

    def body(dq_ref, dk_ref, q_ref, k_ref, u_ref, c_ref, s1_ref, s2_ref, qg_ref, kg_ref,
             dqo_ref, dko_ref, dkpe_ref, gq_ref, gk_ref):
        c, s1, s2 = c_ref[...], s1_ref[...], s2_ref[...]
        qgv, kgv = qg_ref[...], kg_ref[...]
        kpe = pltpu.roll(u_ref[:, U_KPE:U_KPE + LANES], 64, 1)
        lane = lax.broadcasted_iota(jnp.int32, (tile, LANES), 1)
        gq = jnp.zeros((1, LANES), F32)
        gk = jnp.zeros((1, LANES), F32)
        dkpe = jnp.zeros((tile, LANES), F32)
        for h in range(N_HEADS):
            hs = slice(h * LANES, (h + 1) * LANES)
            qr = q_ref[:, hs]
            rq = lax.rsqrt(jnp.sum(qr * qr, axis=-1, keepdims=True) / QK_HEAD + EPS)
            xh = qr * rq
            dy = _rope_t(dq_ref[:, hs] * scale, c, s1, s2)
            dxh = dy * qgv
            dqo_ref[:, hs] = (rq * (dxh - xh * (jnp.sum(dxh * xh, axis=-1, keepdims=True) / QK_HEAD))).astype(BF16)
            gq = gq + jnp.sum(dy * xh, axis=0, keepdims=True)
            xk = k_ref[:, hs] + kpe
            rk = lax.rsqrt(jnp.sum(xk * xk, axis=-1, keepdims=True) / QK_HEAD + EPS)
            kh = xk * rk
            dyk = _rope_t(dk_ref[:, hs], c, s1, s2)
            dkh = dyk * kgv
            dxk = rk * (dkh - kh * (jnp.sum(dkh * kh, axis=-1, keepdims=True) / QK_HEAD))
            gk = gk + jnp.sum(dyk * kh, axis=0, keepdims=True)
            dko_ref[:, hs] = jnp.where(lane < QK_NOPE, dxk, 0.0).astype(BF16)
            dkpe = dkpe + dxk
        dkpe_ref[...] = jnp.where(lane < QK_ROPE, pltpu.roll(dkpe, 64, 1), 0.0)
        i = pl.program_id(0)

        @pl.when(i == 0)
        def _():
            gq_ref[...] = gq
            gk_ref[...] = gk

        @pl.when(i > 0)
        def _():
            gq_ref[...] += gq
            gk_ref[...] += gk

    wide = pl.BlockSpec((tile, 2048), lambda i: (i, 0))
    narrow = pl.BlockSpec((tile, LANES), lambda i: (i, 0))
    gain = pl.BlockSpec((1, LANES), lambda i: (0, 0))
    return _pallas(
        body, name=name, grid=(rows // tile,),
        in_specs=[wide, wide, wide, wide, pl.BlockSpec((tile, 1024), lambda i: (i, 0)), narrow, narrow, narrow,
                  gain, gain],
        out_specs=[wide, wide, narrow, gain, gain],
        out_shape=[jax.ShapeDtypeStruct((rows, 2048), BF16)] * 2
        + [jax.ShapeDtypeStruct((rows, LANES), F32), jax.ShapeDtypeStruct((1, LANES), F32),
           jax.ShapeDtypeStruct((1, LANES), F32)],
        compiler_params=_params(("arbitrary",)),
    )(dq, dk, q_raw, k_raw, u_small, *tabs, qg, kg)


def _attn_fwd(q, k, v, comm_modes, comm_arrays, *, name, tq=2048, tkc=512):
    T = q.shape[0]
    tq = min(tq, T)
    tkc = min(tkc, T)
    n = len(comm_arrays)
    nj, ni = N_HEADS // 2, T // tq

    def body(*refs):
        q_ref, k_ref, v_ref = refs[:3]
        srcs = refs[3:3 + n]
        o_ref, lse_ref = refs[3 + n:5 + n]
        dsts = refs[5 + n:5 + 2 * n]
        sems = refs[5 + 2 * n:]
        j, i = pl.program_id(0), pl.program_id(1)

        @pl.when((j == 0) & (i == 0))
        def _():
            _comm_start(comm_modes, srcs, dsts, sems)

        lane = lax.broadcasted_iota(jnp.int32, (1, LANES), 1)
        out = None
        for hh in range(2):
            sl = slice(hh * LANES, (hh + 1) * LANES)
            qv = q_ref[:, sl]
            spare = LANES - 1 if hh == 0 else 0
            keep = (lane < V_HEAD) if hh == 0 else (lane >= V_HEAD)
            m = acc = None
            for kc in range(T // tkc):
                ks = slice(kc * tkc, (kc + 1) * tkc)
                s = _dot_nt(qv, k_ref[ks, sl])
                vone = jnp.where(lane == spare, 1.0, v_ref[ks, sl]).astype(BF16)
                mc = jnp.max(s, axis=-1, keepdims=True)
                if m is None:
                    m = mc
                    acc = _dot(jnp.exp(s - m).astype(BF16), vone)
                else:
                    m_new = jnp.maximum(m, mc)
                    acc = jnp.exp(m - m_new) * acc + _dot(jnp.exp(s - m_new).astype(BF16), vone)
                    m = m_new
            l = acc[:, spare:spare + 1]
            o = jnp.where(keep, acc / l, 0.0)
            out = o if out is None else out + o
            lse_ref[hh] = m + jnp.log(l)
        o_ref[...] = out

        @pl.when((j == nj - 1) & (i == ni - 1))
        def _():
            _comm_wait(comm_modes, srcs, dsts, sems)

    any_spec = pl.BlockSpec(memory_space=pl.ANY)
    got = _pallas(
        body, name=name, grid=(nj, ni),
        in_specs=[pl.BlockSpec((tq, 2 * LANES), lambda j, i: (i, j)), pl.BlockSpec((T, 2 * LANES), lambda j, i: (0, j)),
                  pl.BlockSpec((T, 2 * LANES), lambda j, i: (0, j))] + [any_spec] * n,
        out_specs=[pl.BlockSpec((tq, LANES), lambda j, i: (i, j)), pl.BlockSpec((2, tq, 1), lambda j, i: (j, i, 0))]
        + [any_spec] * n,
        out_shape=[jax.ShapeDtypeStruct((T, N_HEADS * V_HEAD), F32), jax.ShapeDtypeStruct((N_HEADS, T, 1), F32)]
        + _comm_out_shapes(comm_modes, comm_arrays),
        scratch_shapes=_comm_scratch(n),
        compiler_params=_params(("arbitrary", "arbitrary")),
    )(q, k, v, *comm_arrays)
    return got[0], got[1], got[2:]


def _attn_bwd(q, k, v, o, lse, do, comm_modes, comm_arrays, *, name, tk=256, tqc=4096):
    T = q.shape[0]
    tk = min(tk, T)
    tqc = min(tqc, T)
    n = len(comm_arrays)
    nj, nkb = N_HEADS // 2, T // tk

    def body(*refs):
        q_ref, k_ref, v_ref, o_ref, lse_ref, do_ref = refs[:6]
        srcs = refs[6:6 + n]
        dq_ref, dk_ref, dv_ref = refs[6 + n:9 + n]
        dsts = refs[9 + n:9 + 2 * n]
        d_s = refs[9 + 2 * n]
        sems = refs[10 + 2 * n:]
        j, kb = pl.program_id(0), pl.program_id(1)

        @pl.when((j == 0) & (kb == 0))
        def _():
            _comm_start(comm_modes, srcs, dsts, sems)

        lane = lax.broadcasted_iota(jnp.int32, (1, LANES), 1)
        @pl.when(kb == 0)
        def _():
            prod = do_ref[...] * o_ref[...]
            for hh in range(2):
                keep = (lane < V_HEAD) if hh == 0 else (lane >= V_HEAD)
                d_s[hh] = jnp.sum(jnp.where(keep, prod, 0.0), axis=-1, keepdims=True)

        for hh in range(2):
            sl = slice(hh * LANES, (hh + 1) * LANES)
            keep = (lane < V_HEAD) if hh == 0 else (lane >= V_HEAD)
            kv, vv = k_ref[:, sl], v_ref[:, sl]
            dv_acc = dk_acc = None
            for qc in range(T // tqc):
                qs = slice(qc * tqc, (qc + 1) * tqc)
                qv = q_ref[qs, sl]
                do_b = do_ref[qs, :].astype(BF16)
                s = _dot_nt(qv, kv)
                p = jnp.exp(s - lse_ref[hh, qs])
                dp = _dot_nt(do_b, vv)
                ds = (p * (dp - d_s[hh, qs])).astype(BF16)
                dvc = _dot_tn(p.astype(BF16), do_b)
                dkc = _dot_tn(ds, qv)
                dv_acc = dvc if dv_acc is None else dv_acc + dvc
                dk_acc = dkc if dk_acc is None else dk_acc + dkc
                dqp = _dot(ds, kv)

                @pl.when(kb == 0)
                def _(dqp=dqp, sl=sl, qs=qs):
                    dq_ref[qs, sl] = dqp

                @pl.when(kb > 0)
                def _(dqp=dqp, sl=sl, qs=qs):
                    dq_ref[qs, sl] += dqp

            dv_ref[:, sl] = jnp.where(keep, dv_acc, 0.0).astype(BF16)
            dk_ref[:, sl] = dk_acc

        @pl.when((j == nj - 1) & (kb == nkb - 1))
        def _():
            _comm_wait(comm_modes, srcs, dsts, sems)

    any_spec = pl.BlockSpec(memory_space=pl.ANY)
    pair = pl.BlockSpec((T, 2 * LANES), lambda j, kb: (0, j))
    kblk = pl.BlockSpec((tk, 2 * LANES), lambda j, kb: (kb, j))
    got = _pallas(
        body, name=name, grid=(nj, nkb),
        in_specs=[pair, kblk, kblk, pl.BlockSpec((T, LANES), lambda j, kb: (0, j)),
                  pl.BlockSpec((2, T, 1), lambda j, kb: (j, 0, 0)), pl.BlockSpec((T, LANES), lambda j, kb: (0, j))]
        + [any_spec] * n,
        out_specs=[pair, kblk, kblk] + [any_spec] * n,
        out_shape=[jax.ShapeDtypeStruct((T, 2048), F32)] * 2 + [jax.ShapeDtypeStruct((T, 2048), BF16)]
        + _comm_out_shapes(comm_modes, comm_arrays),
        scratch_shapes=[pltpu.VMEM((2, T, 1), F32)] + _comm_scratch(n),
        compiler_params=_params(("arbitrary", "arbitrary")),
    )(q, k, v, o, lse, do, *comm_arrays)
    return got[0], got[1], got[2], got[3:]


CONV_ROWS, CONV_HALO = 64, 8
CONV_WIN = CONV_ROWS + 2 * CONV_HALO


def _conv_shift(x, sh, t_idx, total):
    if sh == 0:
        return x
    y = pltpu.roll(x, (-sh) % x.shape[0], 0)
    if t_idx is None:
        return y
    ok = (t_idx + sh >= 0) & (t_idx + sh < total)
    return jnp.where(ok, y, 0.0)


def _conv_positions(ws, shape):
    return ws + lax.broadcasted_iota(jnp.int32, shape, 0) if isinstance(ws, int) else None


def _aligned(v, m):
    return v if isinstance(v, int) else pl.multiple_of(v, m)


def _conv_chunks(T, chunk, carry):
    n = T // CONV_ROWS
    carry = chunk(0, 0, carry)

    def mid(ci, c):
        return chunk(pl.multiple_of(ci * CONV_ROWS - CONV_HALO, CONV_HALO), CONV_HALO, c)

    carry = lax.fori_loop(1, n - 1, mid, carry)
    return chunk(T - CONV_WIN, 2 * CONV_HALO, carry)


def _conv_pre(x, w_ref, b_ref, t_idx, total):
    pre = b_ref[...] + w_ref[2:3, :] * x
    for j in (0, 1, 3, 4):
        pre = pre + w_ref[j:j + 1, :] * _conv_shift(x, j - 2, t_idx, total)
    return pre


def _conv_fwd(u_big, conv_w, conv_b, *, name, w=256):
    T = u_big.shape[0]
    first = D_INNER // w

    def body(x_ref, w_ref, b_ref, o_ref):
        def chunk(ws, off, carry):
            x = x_ref[pl.ds(ws, CONV_WIN), :]
            pre = _conv_pre(x, w_ref, b_ref, _conv_positions(ws, x.shape), T)
            act = pre * _sigmoid(pre)
            o_ref[pl.ds(_aligned(ws + off, CONV_ROWS), CONV_ROWS), :] = act[off:off + CONV_ROWS]
            return carry

        _conv_chunks(T, chunk, 0)

    return _pallas(
        body, name=name, grid=(XBC_DIM // w,),
        in_specs=[pl.BlockSpec((T, w), lambda j: (0, j + first)), pl.BlockSpec((CONV_WIDTH, w), lambda j: (0, j)),
                  pl.BlockSpec((1, w), lambda j: (0, j))],
        out_specs=pl.BlockSpec((T, w), lambda j: (0, j)),
        out_shape=jax.ShapeDtypeStruct((T, XBC_DIM), F32),
        compiler_params=_params(("parallel",)),
    )(u_big, conv_w, conv_b)


def _conv_bwd(dact_f, dact_b, u_big, conv_w, conv_b, *, name, w=128):
    T = u_big.shape[0]
    first = D_INNER // w

    def body(df_ref, db_ref, x_ref, w_ref, b_ref, dx_ref, dw_ref, dbias_ref):
        def chunk(ws, off, sums):
            rows = pl.ds(ws, CONV_WIN)
            x = x_ref[rows, :]
            row = lax.broadcasted_iota(jnp.int32, x.shape, 0)
            t_idx = _conv_positions(ws, x.shape)
            pre = _conv_pre(x, w_ref, b_ref, t_idx, T)
            sg = _sigmoid(pre)
            dpre = (df_ref[rows, :] + db_ref[rows, :]) * (sg * (1.0 + pre * (1.0 - sg)))
            dx = w_ref[2:3, :] * dpre
            for j in (0, 1, 3, 4):
                dx = dx + w_ref[j:j + 1, :] * _conv_shift(dpre, 2 - j, t_idx, T)
            dx_ref[pl.ds(_aligned(ws + off, CONV_ROWS), CONV_ROWS), :] = dx[off:off + CONV_ROWS].astype(dx_ref.dtype)
            own = jnp.where((row >= off) & (row < off + CONV_ROWS), dpre, 0.0)
            new = [sums[5] + jnp.sum(own, axis=0, keepdims=True)]
            for j in range(CONV_WIDTH):
                new.insert(j, sums[j] + jnp.sum(own * _conv_shift(x, j - 2, t_idx, T), axis=0, keepdims=True))
            return tuple(new)

        zero = jnp.zeros((1, w), F32)
        sums = _conv_chunks(T, chunk, (zero,) * (CONV_WIDTH + 1))
        for j in range(CONV_WIDTH):
            dw_ref[j:j + 1, :] = sums[j]
        dbias_ref[...] = sums[CONV_WIDTH]

    blk = pl.BlockSpec((T, w), lambda j: (0, j))
    return _pallas(
        body, name=name, grid=(XBC_DIM // w,),
        in_specs=[blk, blk, pl.BlockSpec((T, w), lambda j: (0, j + first)),
                  pl.BlockSpec((CONV_WIDTH, w), lambda j: (0, j)), pl.BlockSpec((1, w), lambda j: (0, j))],
        out_specs=[blk, pl.BlockSpec((CONV_WIDTH, w), lambda j: (0, j)), pl.BlockSpec((1, w), lambda j: (0, j))],
        out_shape=[jax.ShapeDtypeStruct((T, XBC_DIM), BF16), jax.ShapeDtypeStruct((CONV_WIDTH, XBC_DIM), F32),
                   jax.ShapeDtypeStruct((1, XBC_DIM), F32)],
        compiler_params=_params(("parallel",)),
    )(dact_f, dact_b, u_big, conv_w, conv_b)


def _ssd_expand():
    h = jnp.arange(LANES, dtype=jnp.int32)[:, None]
    return (jnp.arange(D_INNER, dtype=jnp.int32)[None, :] // 64 == h).astype(F32)


def _ssd_head_terms(dt_ref, bias_ref, alog_ref, acst_s, dtt_s, rev):
    L = CHUNK
    row = lax.broadcasted_iota(jnp.int32, (L, L), 0)
    col = lax.broadcasted_iota(jnp.int32, (L, L), 1)
    mask = (row <= col) if rev else (row >= col)
    cm = mask.astype(F32)
    cmt = ((row >= col) if rev else (row <= col)).astype(F32)
    pre = dt_ref[...] + bias_ref[...]
    dt = _softplus(pre)
    a = -jnp.exp(alog_ref[...])
    da = dt * a
    acs = _dot_h(cm, da)
    acst_s[...] = _dot_h_tn(da, cmt)
    dtt_s[...] = _dot_h_tn(dt, (row == col).astype(F32))
    tot = jnp.sum(da, axis=0, keepdims=True)
    w = jnp.exp(tot - acs)
    return dict(mask=mask, cm=cm, cmt=cmt, ident=(row == col).astype(F32), pre=pre, dt=dt, a=a, da=da, acs=acs,
                tot=tot, e=jnp.exp(acs), w=w, wdt=w * dt, dec=jnp.exp(tot))


def _pair(lo, v, h0):
    return jnp.where(lo, v[:, h0:h0 + 1], v[:, h0 + 1:h0 + 2])


def _ssd_fwd(xbc_act, u_small, bias128, alog128, *, rev, name):
    T = xbc_act.shape[0]
    L = CHUNK
    nc = T // L
    off = SSM_HEADS if rev else 0

    def cidx(c):
        return (nc - 1 - c) if rev else c

    def body(xs_ref, bm_ref, cm_ref, dt_ref, bias_ref, alog_ref, y_ref, hin_ref, ht_s, acst_s, dtt_s, wx_s, dec_s):
        c = pl.program_id(0)

        @pl.when(c == 0)
        def _():
            ht_s[...] = jnp.zeros_like(ht_s)

        t = _ssd_head_terms(dt_ref, bias_ref, alog_ref, acst_s, dtt_s, rev)
        lo = lax.broadcasted_iota(jnp.int32, (L, LANES), 1) < 64
        lo1 = lax.broadcasted_iota(jnp.int32, (1, LANES), 1) < 64
        for g in range(SSM_GROUPS):
            bmat = bm_ref[:, g * LANES:(g + 1) * LANES].astype(BF16)
            cmat = cm_ref[:, g * LANES:(g + 1) * LANES].astype(BF16)
            gmat = _dot_nt(cmat, bmat)
            ht = ht_s[g]
            ch = _dot(cmat, ht.astype(BF16))
            for pr in range(4):
                ps = slice(pr * LANES, (pr + 1) * LANES)
                cs = slice(g * 512 + pr * LANES, g * 512 + (pr + 1) * LANES)
                h0 = off + 8 * g + 2 * pr
                xp = xs_ref[:, cs]
                acc = _pair(lo, t["e"], h0) * ch[:, ps]
                for s_ in range(2):
                    h = h0 + s_
                    seg = t["acs"][:, h:h + 1] - acst_s[h:h + 1, :]
                    lam = jnp.exp(jnp.where(t["mask"], seg, -1e30))
                    m = (gmat * lam * dtt_s[h:h + 1, :]).astype(BF16)
                    xm = jnp.where(lo if s_ == 0 else jnp.logical_not(lo), xp, 0.0).astype(BF16)
                    acc = acc + _dot(m, xm)
                y_ref[:, cs] = acc
                wx_s[:, ps] = (_pair(lo, t["wdt"], h0) * xp).astype(BF16)
                dec_s[0:1, ps] = _pair(lo1, t["dec"], h0)
            hin_ref[0, g] = ht.astype(BF16)
            ht_s[g] = ht * dec_s[0:1, :] + _dot_tn(bmat, wx_s[...])

    return _pallas(
        body, name=name, grid=(nc,),
        in_specs=[pl.BlockSpec((L, D_INNER), lambda c: (cidx(c), 0)), pl.BlockSpec((L, 512), lambda c: (cidx(c), 4)),
                  pl.BlockSpec((L, 512), lambda c: (cidx(c), 5)),
                  pl.BlockSpec((L, LANES), lambda c: (cidx(c), U_DT // LANES)),
                  pl.BlockSpec((1, LANES), lambda c: (0, 0)), pl.BlockSpec((1, LANES), lambda c: (0, 0))],
        out_specs=[pl.BlockSpec((L, D_INNER), lambda c: (cidx(c), 0)),
                   pl.BlockSpec((1, SSM_GROUPS, D_STATE, 512), lambda c: (cidx(c), 0, 0, 0))],
        out_shape=[jax.ShapeDtypeStruct((T, D_INNER), F32), jax.ShapeDtypeStruct((nc, SSM_GROUPS, D_STATE, 512), BF16)],
        scratch_shapes=[pltpu.VMEM((SSM_GROUPS, D_STATE, 512), F32), pltpu.VMEM((LANES, L), F32),
                        pltpu.VMEM((LANES, L), F32), pltpu.VMEM((L, 512), BF16), pltpu.VMEM((8, 512), F32)],
        compiler_params=_params(("arbitrary",)),
    )(xbc_act, xbc_act, xbc_act, u_small, bias128, alog128)


def _ssd_bwd(dy, xbc_act, u_small, bias128, alog128, hin, skip_x, *, rev, name):
    T = xbc_act.shape[0]
    L = CHUNK
    nc = T // L
    off = SSM_HEADS if rev else 0
    has_skip = skip_x is not None

    def cidx(c):
        return c if rev else (nc - 1 - c)

    def body(*refs):
        (dy_ref, xs_ref, bm_ref, cm_ref, dt_ref, bias_ref, alog_ref, hin_ref) = refs[:8]
        k = 8
        skip_ref = refs[k] if has_skip else None
        k += 1 if has_skip else 0
        (dx_ref, draw_ref, dalog_ref, dbias_ref, dht_s, acst_s, dtt_s, rowt_s, ddtt_s, wx_s, edy_s, dec_s) = refs[k:]
        c = pl.program_id(0)

        @pl.when(c == 0)
        def _():
            dht_s[...] = jnp.zeros_like(dht_s)
            rowt_s[...] = jnp.zeros_like(rowt_s)
            ddtt_s[...] = jnp.zeros_like(ddtt_s)

        t = _ssd_head_terms(dt_ref, bias_ref, alog_ref, acst_s, dtt_s, rev)
        lane1 = lax.broadcasted_iota(jnp.int32, (1, LANES), 1)
        lo = lax.broadcasted_iota(jnp.int32, (L, LANES), 1) < 64
        lo1 = lane1 < 64
        colpart = jnp.zeros((L, LANES), F32)
        u_cols = jnp.zeros((L, LANES), F32)
        v_cols = jnp.zeros((L, LANES), F32)
        dtot_h = jnp.zeros((1, LANES), F32)
        for g in range(SSM_GROUPS):
            bmat = bm_ref[:, g * LANES:(g + 1) * LANES].astype(BF16)
            cmat = cm_ref[:, g * LANES:(g + 1) * LANES].astype(BF16)
            gmat = _dot_nt(cmat, bmat)
            ht_in = hin_ref[0, g]
            dht = dht_s[g]
            ht_in_b, dht_b = ht_in.astype(BF16), dht.astype(BF16)
            ch = _dot(cmat, ht_in_b)
            bdh = _dot(bmat, dht_b)
            th = jnp.sum(dht * ht_in, axis=0, keepdims=True)
            dgm = jnp.zeros((L, L), F32)
            for pr in range(4):
                ps = slice(pr * LANES, (pr + 1) * LANES)
                cs = slice(g * 512 + pr * LANES, g * 512 + (pr + 1) * LANES)
                h0 = off + 8 * g + 2 * pr
                xp = xs_ref[:, cs]
                dyp = dy_ref[:, cs]
                dyp_b = dyp.astype(BF16)
                wdt_p = _pair(lo, t["wdt"], h0)
                e_p = _pair(lo, t["e"], h0)
                xb = xp * bdh[:, ps]
                dc = dyp * ch[:, ps]
                dxp = wdt_p * bdh[:, ps]
                for s_ in range(2):
                    h = h0 + s_
                    keep = lo if s_ == 0 else jnp.logical_not(lo)
                    keep1 = lo1 if s_ == 0 else jnp.logical_not(lo1)
                    onehot = (lane1 == h).astype(F32)
                    dtrow = dtt_s[h:h + 1, :]
                    seg = t["acs"][:, h:h + 1] - acst_s[h:h + 1, :]
                    lam = jnp.exp(jnp.where(t["mask"], seg, -1e30))
                    mf0 = gmat * lam
                    m = (mf0 * dtrow).astype(BF16)
                    xm = jnp.where(keep, xp, 0.0).astype(BF16)
                    dm = _dot_nt(dyp_b, xm)
                    r = dm * mf0
                    q = r * dtrow
                    dgm = dgm + dm * lam * dtrow
                    colpart = colpart + jnp.sum(q, axis=1, keepdims=True) * onehot
                    rowt_s[h:h + 1, :] = jnp.sum(q, axis=0, keepdims=True)
                    ddtt_s[h:h + 1, :] = jnp.sum(r, axis=0, keepdims=True)
                    u_cols = u_cols + jnp.sum(jnp.where(keep, xb, 0.0), axis=1, keepdims=True) * onehot
                    v_cols = v_cols + jnp.sum(jnp.where(keep, dc, 0.0), axis=1, keepdims=True) * onehot
                    dtot_h = dtot_h + jnp.sum(jnp.where(keep1, th[:, ps], 0.0), axis=1, keepdims=True) * onehot
                    dxp = dxp + jnp.where(keep, _dot_tn(m, dyp_b), 0.0)
                if has_skip:
                    dxp = dxp + dyp * skip_ref[:, cs]
                dx_ref[:, cs] = dxp
                wx_s[:, ps] = (wdt_p * xp).astype(BF16)
                edy_s[:, ps] = (e_p * dyp).astype(BF16)
                dec_s[0:1, ps] = _pair(lo1, t["dec"], h0)
            edy_b = edy_s[...]
            dgm_b = dgm.astype(BF16)
            dx_ref[:, D_INNER + g * LANES:D_INNER + (g + 1) * LANES] = (
                _dot_nt(wx_s[...], dht_b) + _dot_tn(dgm_b, cmat))
            dx_ref[:, D_INNER + 512 + g * LANES:D_INNER + 512 + (g + 1) * LANES] = (
                _dot_nt(edy_b, ht_in_b) + _dot(dgm_b, bmat))
            dht_s[g] = dec_s[0:1, :] * dht + _dot_tn(cmat, edy_b)

        t_e = v_cols * t["e"]
        t_w = u_cols * t["wdt"]
        colsum_part = _dot_h_tn(rowt_s[...], t["ident"])
        dtot = jnp.sum(t_w, axis=0, keepdims=True) + t["dec"] * dtot_h
        row1 = lax.broadcasted_iota(jnp.int32, (L, LANES), 0)
        last = row1 == (0 if rev else L - 1)
        dacs = colpart - colsum_part + t_e - t_w + jnp.where(last, dtot, 0.0)
        dda = _dot_h(t["cmt"], dacs)
        ddt = dda * t["a"] + u_cols * t["w"] + _dot_h_tn(ddtt_s[...], t["ident"])
        dalog = jnp.sum(dda * t["dt"], axis=0, keepdims=True) * t["a"]
        draw = ddt * _sigmoid(t["pre"])
        draw_ref[...] = draw
        dbias = jnp.sum(draw, axis=0, keepdims=True)

        @pl.when(c == 0)
        def _():
            dalog_ref[...] = dalog
            dbias_ref[...] = dbias

        @pl.when(c > 0)
        def _():
            dalog_ref[...] += dalog
            dbias_ref[...] += dbias

    one = pl.BlockSpec((1, LANES), lambda c: (0, 0))
    in_specs = [pl.BlockSpec((L, D_INNER), lambda c: (cidx(c), 0)), pl.BlockSpec((L, D_INNER), lambda c: (cidx(c), 0)),
                pl.BlockSpec((L, 512), lambda c: (cidx(c), 4)), pl.BlockSpec((L, 512), lambda c: (cidx(c), 5)),
                pl.BlockSpec((L, LANES), lambda c: (cidx(c), U_DT // LANES)), one, one,
                pl.BlockSpec((1, SSM_GROUPS, D_STATE, 512), lambda c: (cidx(c), 0, 0, 0))]
    ins = [dy, xbc_act, xbc_act, xbc_act, u_small, bias128, alog128, hin]
    if has_skip:
        in_specs.append(pl.BlockSpec((1, D_INNER), lambda c: (0, 0)))
        ins.append(skip_x)
    return _pallas(
        body, name=name, grid=(nc,), in_specs=in_specs,
        out_specs=[pl.BlockSpec((L, XBC_DIM), lambda c: (cidx(c), 0)), pl.BlockSpec((L, LANES), lambda c: (cidx(c), 0)),
                   one, one],
        out_shape=[jax.ShapeDtypeStruct((T, XBC_DIM), F32), jax.ShapeDtypeStruct((T, LANES), F32),
                   jax.ShapeDtypeStruct((1, LANES), F32), jax.ShapeDtypeStruct((1, LANES), F32)],
        scratch_shapes=[pltpu.VMEM((SSM_GROUPS, D_STATE, 512), F32), pltpu.VMEM((LANES, L), F32),
                        pltpu.VMEM((LANES, L), F32), pltpu.VMEM((LANES, L), F32), pltpu.VMEM((LANES, L), F32),
                        pltpu.VMEM((L, 512), BF16), pltpu.VMEM((L, 512), BF16), pltpu.VMEM((8, 512), F32)],
        compiler_params=_params(("arbitrary",)),
    )(*ins)


def _ssm_out_fwd(y_f, y_b, xbc_act, u_big, skip_x, ssm_norm, *, name, tile=512):
    rows = y_f.shape[0]

    def fn(yf, yb, xs, z, sk, nw):
        yz = (yf + yb + sk * xs) * (z * _sigmoid(z))
        r = lax.rsqrt(jnp.mean(yz * yz, axis=-1, keepdims=True) + EPS)
        return yz * r * nw

    return _rowmap(fn, name=name, rows=rows, tile=tile, ncol=SSM_GROUPS,
                   ins=[(y_f, 512, _colj), (y_b, 512, _colj), (xbc_act, 512, _colj), (u_big, 512, _colj)],
                   consts=[(skip_x, 512, _colj), (ssm_norm, 512, _colj)], outs=[(D_INNER, BF16, 512, _colj)])[0]


def _ssm_out_bwd(dm, y_f, y_b, xbc_act, u_big, skip_x, ssm_norm, *, name, tile=512):
    rows = y_f.shape[0]

    def fn(dmv, yf, yb, xs, z, sk, nw):
        sg = _sigmoid(z)
        y = yf + yb + sk * xs
        yz = y * (z * sg)
        r = lax.rsqrt(jnp.mean(yz * yz, axis=-1, keepdims=True) + EPS)
        xh = yz * r
        dxh = dmv * nw
        dyz = r * (dxh - xh * jnp.mean(dxh * xh, axis=-1, keepdims=True))
        dy = dyz * (z * sg)
        dz = dyz * y * (sg * (1.0 + z * (1.0 - sg)))
        return dy, dz, jnp.sum(dmv * xh, axis=0, keepdims=True), jnp.sum(dy * xs, axis=0, keepdims=True)

    return _rowmap(fn, name=name, rows=rows, tile=tile, ncol=SSM_GROUPS,
                   ins=[(dm, 512, _colj), (y_f, 512, _colj), (y_b, 512, _colj), (xbc_act, 512, _colj),
                        (u_big, 512, _colj)],
                   consts=[(skip_x, 512, _colj), (ssm_norm, 512, _colj)],
                   outs=[(D_INNER, F32, 512, _colj), (D_INNER, BF16, 512, _colj)],
                   accs=[(D_INNER, 512, _colj), (D_INNER, 512, _colj)])


def _merge_fwd(pa, pb, u_big, *, name, tile=512):
    rows = pa.shape[0]

    def fn(a, b, ga, gb):
        return _sigmoid(ga) * a + _sigmoid(gb) * b

    return _rowmap(fn, name=name, rows=rows, tile=tile,
                   ins=[(pa, 1024, _col0), (pb, 1024, _col0), (u_big, 1024, lambda j: 5), (u_big, 1024, lambda j: 6)],
                   outs=[(1024, BF16, 1024, _col0)])[0]


def _merge_bwd(dmg, pa, pb, u_big, *, name, tile=512):
    rows = pa.shape[0]

    def fn(d, a, b, ga, gb):
        sa, sb = _sigmoid(ga), _sigmoid(gb)
        return d * sa, d * sb, d * a * sa * (1.0 - sa), d * b * sb * (1.0 - sb)

    return _rowmap(fn, name=name, rows=rows, tile=tile,
                   ins=[(dmg, 1024, _col0), (pa, 1024, _col0), (pb, 1024, _col0), (u_big, 1024, lambda j: 5),
                        (u_big, 1024, lambda j: 6)],
                   outs=[(1024, BF16, 1024, _col0)] * 4)


def _loss_bwd(y, target, *, name, tile=512):
    rows, d = y.shape

    def fn(yv, tv):
        err = yv - tv
        part = jnp.sum(jnp.sum(err * err, axis=-1, keepdims=True), axis=0, keepdims=True)
        return err * (1.0 / d), jnp.broadcast_to(part * (0.5 / d), (1, LANES))

    dy, part = _rowmap(fn, name=name, rows=rows, tile=tile, ins=[(y, d, _col0), (target, d, _col0)],
                       outs=[(d, F32, d, _col0)], accs=[(LANES, LANES, _col0)])
    return dy, part[0, 0]


def _small_slab(gs, dskip_ch, dalog_f, dalog_b, dbias_f, dbias_b, gkv, gqh, gkh, dconv_w, *, name):
    e_mat = _ssd_expand()
    full_names = ("ffn1_norm", "mix_norm", "q_a_norm", "conv_b", "ssm_norm", "ffn2_norm")
    full = [gs[n] for n in full_names]
    nf = len(full)

    def body(*refs):
        fulls = refs[:nf]
        (dsk_ref, e_ref, af_ref, ab_ref, bf_ref, bb_ref, gkv_ref, gqh_ref, gkh_ref, cw_ref, o_ref) = refs[nf:]
        o_ref[...] = jnp.zeros_like(o_ref)
        for n, r in zip(full_names, fulls):
            o_ref[SMALL_ROW[n]:SMALL_ROW[n] + 1, 0:r.shape[1]] = r[...]
        o_ref[SMALL_ROW["kv_a_norm"]:SMALL_ROW["kv_a_norm"] + 1, 0:KV_LORA] = gkv_ref[...]
        o_ref[SMALL_ROW["q_head_norm"]:SMALL_ROW["q_head_norm"] + 1, 0:LANES] = gqh_ref[...]
        o_ref[SMALL_ROW["k_head_norm"]:SMALL_ROW["k_head_norm"] + 1, 0:LANES] = gkh_ref[...]
        o_ref[SMALL_ROW["a_log_fwd"]:SMALL_ROW["a_log_fwd"] + 1, 0:LANES] = af_ref[...]
        o_ref[SMALL_ROW["a_log_bwd"]:SMALL_ROW["a_log_bwd"] + 1, 0:LANES] = pltpu.roll(ab_ref[...], 96, 1)
        o_ref[SMALL_ROW["dt_bias_fwd"]:SMALL_ROW["dt_bias_fwd"] + 1, 0:LANES] = bf_ref[...]
        o_ref[SMALL_ROW["dt_bias_bwd"]:SMALL_ROW["dt_bias_bwd"] + 1, 0:LANES] = pltpu.roll(bb_ref[...], 96, 1)
        dsk = _dot_h_nt(jnp.broadcast_to(dsk_ref[...], (8, D_INNER)), e_ref[...])
        o_ref[SMALL_ROW["d_skip"]:SMALL_ROW["d_skip"] + 1, 0:LANES] = dsk[0:1, :]
        o_ref[CONV_ROW:CONV_ROW + CONV_WIDTH, :] = cw_ref[...]

    return _pallas(body, name=name, out_shape=jax.ShapeDtypeStruct((SMALL_ROWS, SMALL_COLS), F32))(
        *full, dskip_ch, e_mat, dalog_f, dalog_b, dbias_f, dbias_b, gkv, gqh, gkh, dconv_w)


def _adamw_math(g, w, m, v):
    m2 = ADAM_B1 * m + (1.0 - ADAM_B1) * g
    v2 = ADAM_B2 * v + (1.0 - ADAM_B2) * (g * g)
    m_hat = m2 / (1.0 - ADAM_B1 ** ADAM_STEP)
    v_hat = v2 / (1.0 - ADAM_B2 ** ADAM_STEP)
    delta = -ADAM_LR * (m_hat / (jnp.sqrt(v_hat) + ADAM_EPS) + ADAM_WD * w)
    return delta, m2, v2


def _sum8(r_ref):
    g = r_ref[0].astype(F32)
    for s in range(1, N_DEV):
        g = g + r_ref[s].astype(F32)
    return g


def _reduce_adamw(recv, w, m, v, *, name, tile=256):
    _, R, C = recv.shape
    tile = _pick(R, tile) if R % LANES == 0 else R
    assert R % tile == 0

    def body(r_ref, w_ref, m_ref, v_ref, g_ref, d_ref, m2_ref, v2_ref):
        g = _sum8(r_ref)
        delta, m2, v2 = _adamw_math(g, w_ref[...], m_ref[...], v_ref[...])
        g_ref[...] = g
        d_ref[...] = delta
        m2_ref[...] = m2
        v2_ref[...] = v2

    blk = pl.BlockSpec((tile, C), lambda i: (i, 0))
    return _pallas(
        body, name=name, grid=(R // tile,),
        in_specs=[pl.BlockSpec((N_DEV, tile, C), lambda i: (0, i, 0)), blk, blk, blk], out_specs=[blk] * 4,
        out_shape=[jax.ShapeDtypeStruct((R, C), F32)] * 4, compiler_params=_params(("parallel",)),
    )(recv, w, m, v)


def _reduce_t_adamw(recv, w, m, v, *, name):
    R, cs = w.shape

    def body(r_ref, w_ref, m_ref, v_ref, g_ref, d_ref, m2_ref, v2_ref):
        g = _sum8(r_ref).T
        delta, m2, v2 = _adamw_math(g, w_ref[...], m_ref[...], v_ref[...])
        g_ref[...] = g
        d_ref[...] = delta
        m2_ref[...] = m2
        v2_ref[...] = v2

    return _pallas(body, name=name, out_shape=[jax.ShapeDtypeStruct((R, cs), F32)] * 4,
                   compiler_params=pltpu.CompilerParams(vmem_limit_bytes=VMEM_LIMIT))(recv, w, m, v)


def _reduce8(recv, *, name, tile):
    _, R, C = recv.shape

    def body(r_ref, g_ref):
        g_ref[...] = _sum8(r_ref)

    return _pallas(body, name=name, grid=(R // tile,),
                   in_specs=[pl.BlockSpec((N_DEV, tile, C), lambda i: (0, i, 0))],
                   out_specs=pl.BlockSpec((tile, C), lambda i: (i, 0)),
                   out_shape=jax.ShapeDtypeStruct((R, C), F32), compiler_params=_params(("parallel",)))(recv)


def _adamw(g, w, m, v, *, name, tile=256):
    R, C = w.shape

    def body(g_ref, w_ref, m_ref, v_ref, d_ref, m2_ref, v2_ref):
        delta, m2, v2 = _adamw_math(g_ref[...], w_ref[...], m_ref[...], v_ref[...])
        d_ref[...] = delta
        m2_ref[...] = m2
        v2_ref[...] = v2

    blk = pl.BlockSpec((R, tile), lambda i: (0, i))
    return _pallas(body, name=name, grid=(C // tile,), in_specs=[blk] * 4, out_specs=[blk] * 3,
                   out_shape=[jax.ShapeDtypeStruct((R, C), F32)] * 3, compiler_params=_params(("parallel",)))(g, w, m, v)


def _adamw_small(srecv, conv_g, ws, ms, vs, *, name):
    n = len(ws)

    def body(*refs):
        s_ref, c_ref = refs[0], refs[1]
        w_refs, m_refs, v_refs = refs[2:2 + n], refs[2 + n:2 + 2 * n], refs[2 + 2 * n:2 + 3 * n]
        outs = refs[2 + 3 * n:]
        gsum = _sum8(s_ref)
        for i in range(n):
            if i < len(SMALL):
                g = gsum[i:i + 1, 0:SMALL[i][1]]
            else:
                g = _sum8(c_ref)
            delta, m2, v2 = _adamw_math(g, w_refs[i][...], m_refs[i][...], v_refs[i][...])
            outs[i][...] = g
            outs[n + i][...] = delta
            outs[2 * n + i][...] = m2
            outs[3 * n + i][...] = v2

    shapes = [jax.ShapeDtypeStruct(w.shape, F32) for w in ws]
    got = _pallas(body, name=name, out_shape=shapes * 4,
                  compiler_params=pltpu.CompilerParams(vmem_limit_bytes=VMEM_LIMIT))(srecv, conv_g, *ws, *ms, *vs)
    return got[:n], got[n:2 * n], got[2 * n:3 * n], got[3 * n:]


def _ffn_fwd(x, norm, w_g_t, w_u_t, w_d, tag):
    h = _rms_fwd(x, norm, name=f"{tag}_rms")
    hw = D_FF // 2
    w_cat = jnp.concatenate([w_g_t[:hw], w_u_t[:hw], w_g_t[hw:], w_u_t[hw:]], axis=0)
    gu, act = _mm_swiglu(h, w_cat, name=f"{tag}_gu")
    out = _mm(act, w_d, name=f"{tag}_down", alpha=0.5, res=x)
    return out, (h, gu, act)


def _ffn_bwd(dout, x, norm, w_g_t, w_u_t, w_d, saved, tag, send=None):
    h, gu, act = saved
    d_act = _mm(dout, w_d, name=f"{tag}_dact", tb=True, alpha=0.5, out_dtype=BF16)
    d_wd = _mm(act, dout, name=f"{tag}_dwd", ta=True, alpha=0.5, tm=1408, tn=1024, out_dtype=BF16)
    tok = send(("down",), [d_wd]) if send else None
    dg, du = _swiglu_bwd(d_act, gu, name=f"{tag}_dswiglu")
    d_wg_t = _mm(dg, h, name=f"{tag}_dwg", ta=True, tm=1408, tn=1024, out_dtype=BF16, after=tok)
    d_wu_t = _mm(du, h, name=f"{tag}_dwu", ta=True, tm=1408, tn=1024, out_dtype=BF16)
    tok = send(("gate", "up"), [d_wg_t, d_wu_t]) if send else None
    dh = _mm(dg, w_g_t, name=f"{tag}_dh_g", after=tok)
    dh = _mm(du, w_u_t, name=f"{tag}_dh_u", res=dh)
    dx, dnorm = _rms_bwd(dh, x, norm, dout, name=f"{tag}_drms")
    return dx, dnorm, d_wg_t, d_wu_t, d_wd


def _rope_tables(positions, T):
    pos = positions.reshape(T).astype(F32)
    inv_freq = 1.0 / (ROPE_BASE ** (jnp.arange(0, QK_ROPE, 2, dtype=F32) / QK_ROPE))
    ang = pos[:, None] * inv_freq
    cos, sin = jnp.cos(ang), jnp.sin(ang)
    one64, z64 = jnp.ones((T, 64), F32), jnp.zeros((T, 64), F32)
    z16, z32, one32 = jnp.zeros((T, 16), F32), jnp.zeros((T, 32), F32), jnp.ones((T, 32), F32)
    c = jnp.concatenate([one64, cos, cos, one32], axis=1)
    s1 = jnp.concatenate([z64, -sin, z16, z32], axis=1)
    s2 = jnp.concatenate([z64, z16, sin, z32], axis=1)
    return c, s1, s2


def _cols(g):
    n, r, cs = g.shape
    return g.transpose(1, 0, 2).reshape(r, n * cs)


def _rows(g):
    n, rs, c = g.shape
    return g.reshape(n * rs, c)


def _pad_lanes(v, n=LANES):
    return jnp.pad(v, ((0, 0), (0, n - v.shape[1])))


def _in_proj_weights(w_in_t):
    z = lambda n: jnp.zeros((n, D_MODEL), w_in_t.dtype)
    w_small_t = jnp.concatenate([w_in_t[0:384], z(128), w_in_t[384:672], z(96), w_in_t[5792:5856], z(64)], axis=0)
    w_big_t = jnp.concatenate([w_in_t[672:5792], w_in_t[5856:7904]], axis=0)
    return w_small_t, w_big_t


def _mla_up_weights(w_q_b_t, w_kv_b):
    wq = w_q_b_t.reshape(N_HEADS, QK_HEAD, Q_LORA)
    wq = jnp.pad(wq, ((0, 0), (0, LANES - QK_HEAD), (0, 0))).reshape(N_HEADS * LANES, Q_LORA)
    wkv = w_kv_b.reshape(KV_LORA, N_HEADS, QK_NOPE + V_HEAD)
    wk = jnp.pad(wkv[..., :QK_NOPE], ((0, 0), (0, 0), (0, LANES - QK_NOPE))).reshape(KV_LORA, N_HEADS * LANES)
    v = wkv[..., QK_NOPE:]
    zv = jnp.zeros_like(v)
    even = (jnp.arange(N_HEADS) % 2 == 0)[None, :, None]
    wv = jnp.where(even, jnp.concatenate([v, zv], -1), jnp.concatenate([zv, v], -1)).reshape(KV_LORA, N_HEADS * LANES)
    return wq, wk, wv


def _shard_rows(g):
    return g.reshape(N_DEV, g.shape[0] // N_DEV, g.shape[1])


def kernel(x, positions, ffn1_norm, ffn1_w_gate, ffn1_w_up, ffn1_w_down, mix_norm, w_in, q_a_norm, w_q_b, kv_a_norm, w_kv_b, q_head_norm, k_head_norm, conv_w, conv_b, a_log_fwd, a_log_bwd, dt_bias_fwd, dt_bias_bwd, d_skip, ssm_norm, w_attn_branch, w_ssm_branch, w_out, ffn2_norm, ffn2_w_gate, ffn2_w_up, ffn2_w_down, loss_target, m_ffn1_norm, m_ffn1_w_gate, m_ffn1_w_up, m_ffn1_w_down, m_mix_norm, m_w_in, m_q_a_norm, m_w_q_b, m_kv_a_norm, m_w_kv_b, m_q_head_norm, m_k_head_norm, m_conv_w, m_conv_b, m_a_log_fwd, m_a_log_bwd, m_dt_bias_fwd, m_dt_bias_bwd, m_d_skip, m_ssm_norm, m_w_attn_branch, m_w_ssm_branch, m_w_out, m_ffn2_norm, m_ffn2_w_gate, m_ffn2_w_up, m_ffn2_w_down, v_ffn1_norm, v_ffn1_w_gate, v_ffn1_w_up, v_ffn1_w_down, v_mix_norm, v_w_in, v_q_a_norm, v_w_q_b, v_kv_a_norm, v_w_kv_b, v_q_head_norm, v_k_head_norm, v_conv_w, v_conv_b, v_a_log_fwd, v_a_log_bwd, v_dt_bias_fwd, v_dt_bias_bwd, v_d_skip, v_ssm_norm, v_w_attn_branch, v_w_ssm_branch, v_w_out, v_ffn2_norm, v_ffn2_w_gate, v_ffn2_w_up, v_ffn2_w_down):
    w_all = dict(ffn1_norm=ffn1_norm, ffn1_w_gate=ffn1_w_gate, ffn1_w_up=ffn1_w_up, ffn1_w_down=ffn1_w_down, mix_norm=mix_norm, w_in=w_in, q_a_norm=q_a_norm, w_q_b=w_q_b, kv_a_norm=kv_a_norm, w_kv_b=w_kv_b, q_head_norm=q_head_norm, k_head_norm=k_head_norm, conv_w=conv_w, conv_b=conv_b, a_log_fwd=a_log_fwd, a_log_bwd=a_log_bwd, dt_bias_fwd=dt_bias_fwd, dt_bias_bwd=dt_bias_bwd, d_skip=d_skip, ssm_norm=ssm_norm, w_attn_branch=w_attn_branch, w_ssm_branch=w_ssm_branch, w_out=w_out, ffn2_norm=ffn2_norm, ffn2_w_gate=ffn2_w_gate, ffn2_w_up=ffn2_w_up, ffn2_w_down=ffn2_w_down)
    m_all = dict(ffn1_norm=m_ffn1_norm, ffn1_w_gate=m_ffn1_w_gate, ffn1_w_up=m_ffn1_w_up, ffn1_w_down=m_ffn1_w_down, mix_norm=m_mix_norm, w_in=m_w_in, q_a_norm=m_q_a_norm, w_q_b=m_w_q_b, kv_a_norm=m_kv_a_norm, w_kv_b=m_w_kv_b, q_head_norm=m_q_head_norm, k_head_norm=m_k_head_norm, conv_w=m_conv_w, conv_b=m_conv_b, a_log_fwd=m_a_log_fwd, a_log_bwd=m_a_log_bwd, dt_bias_fwd=m_dt_bias_fwd, dt_bias_bwd=m_dt_bias_bwd, d_skip=m_d_skip, ssm_norm=m_ssm_norm, w_attn_branch=m_w_attn_branch, w_ssm_branch=m_w_ssm_branch, w_out=m_w_out, ffn2_norm=m_ffn2_norm, ffn2_w_gate=m_ffn2_w_gate, ffn2_w_up=m_ffn2_w_up, ffn2_w_down=m_ffn2_w_down)
    v_all = dict(ffn1_norm=v_ffn1_norm, ffn1_w_gate=v_ffn1_w_gate, ffn1_w_up=v_ffn1_w_up, ffn1_w_down=v_ffn1_w_down, mix_norm=v_mix_norm, w_in=v_w_in, q_a_norm=v_q_a_norm, w_q_b=v_w_q_b, kv_a_norm=v_kv_a_norm, w_kv_b=v_w_kv_b, q_head_norm=v_q_head_norm, k_head_norm=v_k_head_norm, conv_w=v_conv_w, conv_b=v_conv_b, a_log_fwd=v_a_log_fwd, a_log_bwd=v_a_log_bwd, dt_bias_fwd=v_dt_bias_fwd, dt_bias_bwd=v_dt_bias_bwd, d_skip=v_d_skip, ssm_norm=v_ssm_norm, w_attn_branch=v_w_attn_branch, w_ssm_branch=v_w_ssm_branch, w_out=v_w_out, ffn2_norm=v_ffn2_norm, ffn2_w_gate=v_ffn2_w_gate, ffn2_w_up=v_ffn2_w_up, ffn2_w_down=v_ffn2_w_down)
    T = x.shape[1]
    xs_in, target = x[0], loss_target[0]
    def two_d(n, a):
        if n in TRANSPOSED:
            return jnp.swapaxes(a, 1, 2).reshape(a.shape[2], a.shape[1])
        return a.reshape(-1, a.shape[-1])

    w2 = {n: two_d(n, a) for n, a in w_all.items()}
    m2 = {n: two_d(n, a) for n, a in m_all.items()}
    v2 = {n: two_d(n, a) for n, a in v_all.items()}
    p = {n: w2[n] for n, _ in SMALL}
    bf = lambda n: w2[n].astype(BF16)

    first = ["ffn1_w_gate", "ffn1_w_up", "ffn1_w_down"]
    ge = dict(zip(first, _all_gather_two_level([bf(n) for n in first], name="gather_ffn1")))
    mixw = ["w_in", "w_q_b", "w_kv_b", "conv_w"]
    behind_first = ge["ffn1_w_down"][0, 0:1, 0:1].astype(F32) * 0.0
    mix_started, token = _exchange_start(["gather"] * len(mixw),
                                         [bf(n) for n in mixw[:3]] + [w2["conv_w"] + behind_first],
                                         name="gather_mix_start")
    ffn1_norm_f = p["ffn1_norm"] + token[0:1, 0:1]
    w_g1t, w_u1t = _rows(ge["ffn1_w_gate"]), _rows(ge["ffn1_w_up"])
    w_d1 = _rows(ge["ffn1_w_down"])
    late = ["w_attn_branch", "w_ssm_branch", "w_out", "ffn2_w_gate", "ffn2_w_up", "ffn2_w_down"]
    late_shards = [bf(n) for n in late]

    tabs = _rope_tables(positions, T)
    qg, kg = _pad_lanes(p["q_head_norm"]), _pad_lanes(p["k_head_norm"])
    bias128 = _pad_lanes(jnp.concatenate([p["dt_bias_fwd"], p["dt_bias_bwd"]], axis=1))
    alog128 = _pad_lanes(jnp.concatenate([p["a_log_fwd"], p["a_log_bwd"]], axis=1))
    skip_x = jnp.repeat(p["d_skip"], 64, axis=1)

    x1, ffn1_saved = _ffn_fwd(xs_in, ffn1_norm_f, w_g1t, w_u1t, w_d1, "ffn1")
    h2 = _rms_fwd(x1, p["mix_norm"], name="mix_rms")
    ge.update(zip(mixw, _exchange_wait(["gather"] * len(mixw), mix_started, h2, name="gather_mix_wait")))
    w_small_t, w_big_t = _in_proj_weights(_rows(ge["w_in"]))
    wq_t, wk, wv = _mla_up_weights(_rows(ge["w_q_b"]), _cols(ge["w_kv_b"]))
    conv_full = _cols(ge["conv_w"])
    u_big = _mm(h2, w_big_t, name="in_big", tb=True)
    u_small = _mm(h2, w_small_t, name="in_small", tb=True)
    cqn, ckvn = _lora_norm_fwd(u_small, p["q_a_norm"], p["kv_a_norm"], name="lora_norm")
    q_raw = _mm(cqn, wq_t, name="q_up", tb=True)
    k_raw = _mm(ckvn, wk, name="k_up")
    v = _mm(ckvn, wv, name="v_up", out_dtype=BF16)
    q, k = _qk_prep_fwd(q_raw, k_raw, u_small, tabs, qg, kg, name="qk_prep")
    a_out, lse, g_late = _attn_fwd(q, k, v, ["gather"] * len(late), late_shards, name="attn_fwd")
    gl = dict(zip(late, g_late))
    w_pa, w_pb, w_o = _rows(gl["w_attn_branch"]), _rows(gl["w_ssm_branch"]), _rows(gl["w_out"])
    w_g2t, w_u2t = _rows(gl["ffn2_w_gate"]), _rows(gl["ffn2_w_up"])
    w_d2 = _rows(gl["ffn2_w_down"])
    xbc_act = _conv_fwd(u_big, conv_full, p["conv_b"], name="conv_fwd")
    y_f, hin_f = _ssd_fwd(xbc_act, u_small, bias128, alog128, rev=False, name="ssd_fwd_f")
    y_b, hin_b = _ssd_fwd(xbc_act, u_small, bias128, alog128, rev=True, name="ssd_fwd_b")
    m_out = _ssm_out_fwd(y_f, y_b, xbc_act, u_big, skip_x, p["ssm_norm"], name="ssm_out")
    pa = _mm(a_out, w_pa, name="branch_a")
    pb = _mm(m_out, w_pb, name="branch_b")
    merged = _merge_fwd(pa, pb, u_big, name="merge")
    x2 = _mm(merged, w_o, name="mix_out", res=x1)
    y, ffn2_saved = _ffn_fwd(x2, p["ffn2_norm"], w_g2t, w_u2t, w_d2, "ffn2")
    dy, loss_part = _loss_bwd(y, target, name="loss")
    loss = lax.psum(loss_part, ("x", "y", "c"))

    gs = {}
    dx2, gs["ffn2_norm"], g_gate2, g_up2, g_down2 = _ffn_bwd(dy, x2, p["ffn2_norm"], w_g2t, w_u2t, w_d2, ffn2_saved,
                                                             "ffn2b")
    dmerged = _mm(dx2, w_o, name="d_merged", tb=True)
    g_out = _mm(merged, dx2, name="d_w_out", ta=True, out_dtype=BF16)
    dpa, dpb, dga, dgb = _merge_bwd(dmerged, pa, pb, u_big, name="d_merge")
    g_pa = _mm(a_out, dpa, name="d_w_pa", ta=True, out_dtype=BF16)
    g_pb = _mm(m_out, dpb, name="d_w_pb", ta=True, out_dtype=BF16)
    da_out = _mm(dpa, w_pa, name="d_a", tb=True)
    dm_out = _mm(dpb, w_pb, name="d_m", tb=True)
    late_grads = [_shard_rows(g) for g in (g_pa, g_pb, g_out, g_gate2, g_up2, g_down2)]
    dq, dk, dv, r_late = _attn_bwd(q, k, v, a_out, lse, da_out, ["scatter"] * len(late_grads), late_grads,
                                   name="attn_bwd")
    recv = dict(zip(late, r_late))

    dyss, dz, gs["ssm_norm"], dskip_ch = _ssm_out_bwd(dm_out, y_f, y_b, xbc_act, u_big, skip_x, p["ssm_norm"],
                                                      name="d_ssm_out")
    dact_f, draw_f, dalog_f, dbias_f = _ssd_bwd(dyss, xbc_act, u_small, bias128, alog128, hin_f, skip_x,
                                                rev=False, name="ssd_bwd_f")
    dact_b, draw_b, dalog_b, dbias_b = _ssd_bwd(dyss, xbc_act, u_small, bias128, alog128, hin_b, None,
                                                rev=True, name="ssd_bwd_b")
    dxbc, g_conv, gs["conv_b"] = _conv_bwd(dact_f, dact_b, u_big, conv_full, p["conv_b"], name="conv_bwd")

    dq_raw, dk_raw, dkpe, gqh, gkh = _qk_prep_bwd(dq, dk, q_raw, k_raw, u_small, tabs, qg, kg, name="d_qk_prep")
    g_wq_t = _mm(dq_raw, cqn, name="d_w_q", ta=True, out_dtype=BF16)
    g_wk_t = _mm(dk_raw, ckvn, name="d_w_k", ta=True, out_dtype=BF16)
    g_wv_t = _mm(dv, ckvn, name="d_w_v", ta=True, out_dtype=BF16)
    dcqn = _mm(dq_raw, wq_t, name="d_cqn")
    dckvn = _mm(dk_raw, wk, name="d_ckvn_k", tb=True)
    dckvn = _mm(dv, wv, name="d_ckvn_v", tb=True, res=dckvn)
    du_small, gs["q_a_norm"], gkv = _lora_norm_bwd(dcqn, dckvn, u_small, p["q_a_norm"], p["kv_a_norm"], dkpe,
                                                   draw_f, draw_b, name="d_lora_norm")

    dh2 = _mm(du_small, w_small_t, name="d_h2_small")
    dh2 = _mm(dz, w_big_t, name="d_h2_z", b_row0=0, res=dh2)
    dh2 = _mm(dxbc, w_big_t, name="d_h2_xbc", b_row0=2048, res=dh2)
    dh2 = _mm(dga, w_big_t, name="d_h2_ga", b_row0=5120, res=dh2)
    dh2 = _mm(dgb, w_big_t, name="d_h2_gb", b_row0=6144, res=dh2)
    gt_small = _mm(du_small, h2, name="d_w_small", ta=True, out_dtype=BF16)
    gt_z = _mm(dz, h2, name="d_w_z", ta=True, out_dtype=BF16)
    gt_xbc = _mm(dxbc, h2, name="d_w_xbc", ta=True, out_dtype=BF16)
    gt_ga = _mm(dga, h2, name="d_w_ga", ta=True, out_dtype=BF16)
    gt_gb = _mm(dgb, h2, name="d_w_gb", ta=True, out_dtype=BF16)
    dx1, gs["mix_norm"] = _rms_bwd(dh2, x1, p["mix_norm"], dx2, name="d_mix_rms")

    gt_in = jnp.concatenate([gt_small[0:384], gt_small[U_CKV:U_KPE + QK_ROPE], gt_z, gt_xbc,
                             gt_small[U_DT:U_DT + 64], gt_ga, gt_gb], axis=0)
    gt_in = jnp.pad(gt_in.reshape(N_DEV, W_IN_SHARD, D_MODEL), ((0, 0), (0, W_IN_SHARD_PAD - W_IN_SHARD), (0, 0)))
    gt_q = g_wq_t.reshape(N_HEADS, LANES, Q_LORA)[:, :QK_HEAD].reshape(N_DEV, -1, Q_LORA)
    gk3 = g_wk_t.reshape(N_HEADS, LANES, KV_LORA)[:, :QK_NOPE]
    gv3 = g_wv_t.reshape(N_HEADS, LANES, KV_LORA)
    even = (jnp.arange(N_HEADS) % 2 == 0)[:, None, None]
    gv3 = jnp.where(even, gv3[:, :V_HEAD], gv3[:, V_HEAD:])
    gt_kv = jnp.concatenate([gk3, gv3], axis=1).reshape(N_DEV, -1, KV_LORA)
    mixg = ["w_in", "w_q_b", "w_kv_b"]
    grads_started, token = _exchange_start(["scatter"] * len(mixg), [gt_in, gt_q, gt_kv], name="grad_mix_start")
    ffn1_sent = []

    def send(names, grads):
        st, tok = _exchange_start(["scatter"] * len(grads), [_shard_rows(g) for g in grads],
                                  name="grad_ffn1_" + "_".join(names) + "_start")
        ffn1_sent.append((names, st))
        return tok

    grad_x, gs["ffn1_norm"], _, _, _ = _ffn_bwd(dx1, xs_in, p["ffn1_norm"] + token[0:1, 0:1], w_g1t, w_u1t, w_d1,
                                                ffn1_saved, "ffn1b", send=send)
    recv.update(zip(mixg, _exchange_wait(["scatter"] * len(mixg), grads_started, grad_x, name="grad_mix_wait")))
    for names, st in ffn1_sent:
        got = _exchange_wait(["scatter"] * len(names), st, grad_x, name="grad_ffn1_" + "_".join(names) + "_wait")
        recv.update(zip(["ffn1_w_" + n for n in names], got))

    gsmall = _small_slab(gs, dskip_ch, dalog_f, dalog_b, dbias_f, dbias_b, gkv, gqh, gkh, g_conv, name="small_slab")
    srecv = _exchange(["gather"], [gsmall], name="grad_exchange")[0]

    out = {}
    for n in ("ffn1_w_down", "ffn2_w_down", "w_attn_branch", "w_ssm_branch", "w_out", "ffn1_w_gate", "ffn1_w_up",
              "ffn2_w_gate", "ffn2_w_up", "w_q_b"):
        out[n] = _reduce_adamw(recv[n], w2[n], m2[n], v2[n], name=f"adamw_{n}")
    out["w_kv_b"] = _reduce_t_adamw(recv["w_kv_b"], w2["w_kv_b"], m2["w_kv_b"], v2["w_kv_b"], name="adamw_w_kv_b")
    g_in = _reduce8(recv["w_in"], name="sum_w_in", tile=W_IN_SHARD_PAD // 2)[:W_IN_SHARD]
    out["w_in"] = [g_in] + list(_adamw(g_in, w2["w_in"], m2["w_in"], v2["w_in"], name="adamw_w_in"))
    me = 4 * lax.axis_index("x") + 2 * lax.axis_index("y") + lax.axis_index("c")
    conv_g = lax.dynamic_slice(srecv, (0, CONV_ROW, me * (XBC_DIM // N_DEV)), (N_DEV, CONV_WIDTH, XBC_DIM // N_DEV))
    sn = [n for n, _ in SMALL] + ["conv_w"]
    sg, sd, sm, sv = _adamw_small(srecv, conv_g, [w2[n] for n in sn], [m2[n] for n in sn], [v2[n] for n in sn],
                                  name="adamw_small")
    for i, n in enumerate(sn):
        out[n] = (sg[i], sd[i], sm[i], sv[i])
    def back(n, a):
        if n in TRANSPOSED:
            return jnp.swapaxes(a.reshape(1, a.shape[0], a.shape[1]), 1, 2)
        return a.reshape(w_all[n].shape)

    outs = [[back(n, out[n][kind]) for n in WEIGHT_ORDER] for kind in range(4)]
    return (loss, grad_x[None], *outs[0], *outs[1], *outs[2], *outs[3])
```
